```python
import math
import jax
import jax.numpy as jnp
from jax import lax
import numpy as np

D_MODEL = 2048
BATCH = 4
SEQ = 2048
DEPTH = 1
DEC_BATCH = 128
DEC_SEQ = 1
PAST_LEN = 16384
PAGE_SIZE = 128

D_A = D_MODEL
H_A = 8
DK = D_A // H_A
DV = D_A // H_A
CHUNK = 128
CONV_W = 4
F_BIAS_LO = 3.0
F_BIAS_HI = 6.0
D_B = D_MODEL // 2
S5_GROUP = 16
G_B = D_B // S5_GROUP
P_S5 = 64
DT_MIN = 1e-3
DT_MAX = 1e-1
N_GROUPS = 4
EXP_PER_GROUP = 8
N_EXPERTS = N_GROUPS * EXP_PER_GROUP
TOP_K = 2
D_EXPERT = D_MODEL // 4
EPS = 1e-6

kernel_name = "hybrid_mlstm_s5_hmoe_step"


def rmsnorm(x, g):
    xf = x.astype(jnp.float32)
    xf = xf * lax.rsqrt(jnp.mean(xf * xf, axis=-1, keepdims=True) + EPS)
    return (xf * g.astype(jnp.float32)).astype(x.dtype)


def split_cols(a, sizes):
    out = []
    off = 0
    for s in sizes:
        out.append(a[..., off:off + s])
        off += s
    return out


def causal_conv(x, buf, w, b):
    S = x.shape[1]
    xp = jnp.concatenate([buf.astype(x.dtype), x], axis=1)
    y = b
    for j in range(CONV_W):
        y = y + w[j] * xp[:, j:j + S]
    return y, xp[:, -(CONV_W - 1):]


def mlstm_chunk(carry, inp):
    C, n, m = carry
    q, k, v, ig, lf = inp
    L = q.shape[2]
    b = jnp.cumsum(lf, axis=-1)
    causal = jnp.tril(jnp.ones((L, L), dtype=bool))
    d_log = jnp.where(causal, b[..., :, None] - b[..., None, :] + ig[..., None, :], -jnp.inf)
    inter_log = b + m[..., None]
    m_t = jnp.maximum(inter_log, jnp.max(d_log, axis=-1))
    s = jnp.einsum('bhtd,bhsd->bhts', q, k) * jnp.exp(d_log - m_t[..., None])
    inter_w = jnp.exp(inter_log - m_t)
    num = jnp.einsum('bhts,bhse->bhte', s, v) + inter_w[..., None] * jnp.einsum('bhtd,bhde->bhte', q, C)
    nq = jnp.sum(s, axis=-1) + inter_w * jnp.einsum('bhtd,bhd->bht', q, n)
    den = jnp.maximum(jnp.abs(nq), jnp.exp(-m_t))
    h = num / den[..., None]
    m_new = m_t[..., -1]
    decay_prev = jnp.exp(b[..., -1] + m - m_new)
    w_end = jnp.exp(b[..., -1:] - b + ig - m_new[..., None])
    C_new = decay_prev[..., None, None] * C + jnp.einsum('bhs,bhsd,bhse->bhde', w_end, k, v)
    n_new = decay_prev[..., None] * n + jnp.einsum('bhs,bhsd->bhd', w_end, k)
    return (C_new, n_new, m_new), h


def mlstm_sequence(q, k, v, ig, lf, C0, n0, m0):
    B, H, S, _ = q.shape
    L = CHUNK if S % CHUNK == 0 else S
    nc = S // L

    def split(a):
        return jnp.moveaxis(a.reshape((B, H, nc, L) + a.shape[3:]), 2, 0)

    carry, h = lax.scan(mlstm_chunk, (C0, n0, m0), (split(q), split(k), split(v), split(ig), split(lf)))
    h = jnp.moveaxis(h, 0, 2).reshape(B, H, S, DV)
    return carry, h


def s5_layer(u, s_re0, s_im0, p):
    B, S, _ = u.shape
    f32 = jnp.float32
    uf = u.astype(f32)
    a_re = p['s5_a_re'].astype(f32)
    a_im = p['s5_a_im'].astype(f32)
    dt = jnp.exp(p['s5_log_step'].astype(f32))[:, None]
    mag = jnp.exp(a_re * dt)
    lb_re = mag * jnp.cos(a_im * dt)
    lb_im = mag * jnp.sin(a_im * dt)
    den = a_re * a_re + a_im * a_im
    nr = lb_re - 1.0
    coef_re = (nr * a_re + lb_im * a_im) / den
    coef_im = (lb_im * a_re - nr * a_im) / den
    b_re = p['s5_b_re'].astype(f32)
    b_im = p['s5_b_im'].astype(f32)
    bb_re = coef_re[..., None] * b_re - coef_im[..., None] * b_im
    bb_im = coef_re[..., None] * b_im + coef_im[..., None] * b_re
    ug = uf.reshape(B, S, G_B, S5_GROUP)
    bu_re = jnp.einsum('bsgc,gpc->bsgp', ug, bb_re)
    bu_im = jnp.einsum('bsgc,gpc->bsgp', ug, bb_im)
    sr = s_re0.astype(f32)
    si = s_im0.astype(f32)
    bu_re = bu_re.at[:, 0].add(lb_re * sr - lb_im * si)
    bu_im = bu_im.at[:, 0].add(lb_re * si + lb_im * sr)
    ar = jnp.broadcast_to(lb_re, bu_re.shape)
    ai = jnp.broadcast_to(lb_im, bu_im.shape)

    def combine(e1, e2):
        a1r, a1i, b1r, b1i = e1
        a2r, a2i, b2r, b2i = e2
        return (a2r * a1r - a2i * a1i, a2r * a1i + a2i * a1r,
                a2r * b1r - a2i * b1i + b2r, a2r * b1i + a2i * b1r + b2i)

    _, _, xr, xi = lax.associative_scan(combine, (ar, ai, bu_re, bu_im), axis=1)
    y = (jnp.einsum('gcp,bsgp->bsgc', p['s5_c_re'].astype(f32), xr)
         - jnp.einsum('gcp,bsgp->bsgc', p['s5_c_im'].astype(f32), xi)).reshape(B, S, D_B)
    y = y + p['s5_d'].astype(f32) * uf
    y = jax.nn.gelu(y).astype(u.dtype)
    y = y * jax.nn.sigmoid(y @ p['s5_w_glu'] + p['s5_b_glu'])
    return y, xr[:, -1], xi[:, -1]


def mixer(xn, C0, n0, m0, conv0, sre0, sim0, p):
    B, S, _ = xn.shape
    f32 = jnp.float32
    proj = xn @ p['w_in']
    qk_raw, v, o_pre, i_pre, f_pre, u, g_a, g_b = split_cols(
        proj, (2 * D_A, D_A, D_A, H_A, H_A, D_B, D_MODEL, D_MODEL))
    qk, conv1 = causal_conv(qk_raw, conv0, p['w_conv'], p['b_conv'])
    qk = jax.nn.silu(qk)
    q, k = split_cols(qk, (D_A, D_A))

    def heads(a):
        return a.reshape(B, S, H_A, -1).transpose(0, 2, 1, 3).astype(f32)

    qh = heads(q) * (DK ** -0.5)
    kh = heads(k)
    vh = heads(v)
    ig = (i_pre + p['b_i']).astype(f32).transpose(0, 2, 1)
    lf = jax.nn.log_sigmoid((f_pre + p['b_f']).astype(f32)).transpose(0, 2, 1)
    (C1, n1, m1), h = mlstm_sequence(qh, kh, vh, ig, lf,
                                     C0.astype(f32), n0.astype(f32), m0.astype(f32))
    h = h.transpose(0, 2, 1, 3) * jax.nn.sigmoid(o_pre.astype(f32).reshape(B, S, H_A, DV))
    h = h * lax.rsqrt(jnp.mean(h * h, axis=-1, keepdims=True) + EPS)
    h = (h.reshape(B, S, D_A) * p['head_norm_g'].astype(f32)).astype(xn.dtype)
    ya = h @ p['w_pa']
    ys, sre1, sim1 = s5_layer(u, sre0, sim0, p)
    yb = ys @ p['w_pb']
    z = jax.nn.sigmoid(g_a) * ya + jax.nn.sigmoid(g_b) * yb
    out = z @ p['w_out']
    return out, (C1.astype(C0.dtype), n1.astype(n0.dtype), m1.astype(m0.dtype),
                 conv1.astype(conv0.dtype), sre1.astype(sre0.dtype), sim1.astype(sim0.dtype))


def hier_moe(xn, p):
    B, S, D = xn.shape
    T = B * S
    xt = xn.reshape(T, D)
    pg = jax.nn.softmax((xt @ p['w_rg'] + p['b_rg']).astype(jnp.float32), axis=-1)
    g = jnp.argmax(pg, axis=-1)
    pg_sel = jnp.take_along_axis(pg, g[:, None], axis=-1)
    le = (xt @ p['w_rexp'] + p['b_rexp']).astype(jnp.float32).reshape(T, N_GROUPS, EXP_PER_GROUP)
    le = jnp.take_along_axis(le, g[:, None, None], axis=1)[:, 0]
    pe = jax.nn.softmax(le, axis=-1)
    top_v, top_i = lax.top_k(pe, TOP_K)
    top_v = top_v / jnp.sum(top_v, axis=-1, keepdims=True)
    wts = pg_sel * top_v
    eidx = g[:, None] * EXP_PER_GROUP + top_i
    combine = jnp.sum(jax.nn.one_hot(eidx, N_EXPERTS, dtype=jnp.float32) * wts[..., None], axis=1)
    hg = jnp.einsum('td,edf->tef', xt, p['w_gate'])
    hu = jnp.einsum('td,edf->tef', xt, p['w_up'])
    hh = jax.nn.silu(hg) * hu * combine.astype(xt.dtype)[..., None]
    y = jnp.einsum('tef,efd->td', hh, p['w_down'])
    return y.reshape(B, S, D)


def layer(x, C0, n0, m0, conv0, sre0, sim0, p):
    mix, st = mixer(rmsnorm(x, p['norm_mix_g']), C0, n0, m0, conv0, sre0, sim0, p)
    x = x + mix
    x = x + hier_moe(rmsnorm(x, p['norm_ffn_g']), p)
    return x, st


def setup_inputs(seed: int = 0) -> dict:
    key = jax.random.key(seed)
    ks = iter(jax.random.split(key, 48))
    f32 = jnp.float32

    def nrm(shape, scale):
        return jax.random.normal(next(ks), shape, f32) * scale

    Dp = DEPTH
    inp = {}
    inp['x_prompt'] = nrm((BATCH, SEQ, D_MODEL), 1.0)
    inp['x_sample'] = nrm((DEC_BATCH, DEC_SEQ, D_MODEL), 1.0)
    inp['state_mlstm_C'] = nrm((Dp, DEC_BATCH, H_A, DK, DV), 0.1)
    inp['state_mlstm_n'] = nrm((Dp, DEC_BATCH, H_A, DK), 0.1)
    inp['state_mlstm_m'] = nrm((Dp, DEC_BATCH, H_A), 1.0)
    inp['state_conv'] = nrm((Dp, DEC_BATCH, CONV_W - 1, 2 * D_A), 1.0)
    inp['state_s5_re'] = nrm((Dp, DEC_BATCH, G_B, P_S5), 0.3)
    inp['state_s5_im'] = nrm((Dp, DEC_BATCH, G_B, P_S5), 0.3)
    n_in = 4 * D_A + 2 * H_A + D_B + 2 * D_MODEL
    inp['norm_mix_g'] = 1.0 + nrm((Dp, D_MODEL), 0.02)
    inp['w_in'] = nrm((Dp, D_MODEL, n_in), D_MODEL ** -0.5)
    inp['b_i'] = nrm((Dp, H_A), 0.1)
    inp['b_f'] = jnp.linspace(F_BIAS_LO, F_BIAS_HI, H_A, dtype=f32) + nrm((Dp, H_A), 0.1)
    inp['w_conv'] = nrm((Dp, CONV_W, 2 * D_A), CONV_W ** -0.5)
    inp['b_conv'] = nrm((Dp, 2 * D_A), 0.01)
    inp['head_norm_g'] = 1.0 + nrm((Dp, D_A), 0.02)
    inp['w_pa'] = nrm((Dp, D_A, D_MODEL), D_A ** -0.5)
    inp['s5_a_re'] = -0.5 + nrm((Dp, G_B, P_S5), 0.01)
    inp['s5_a_im'] = math.pi * jnp.arange(P_S5, dtype=f32) + nrm((Dp, G_B, P_S5), 0.01)
    inp['s5_log_step'] = jax.random.uniform(next(ks), (Dp, G_B), f32, math.log(DT_MIN), math.log(DT_MAX))
    inp['s5_b_re'] = nrm((Dp, G_B, P_S5, S5_GROUP), S5_GROUP ** -0.5)
    inp['s5_b_im'] = nrm((Dp, G_B, P_S5, S5_GROUP), S5_GROUP ** -0.5)
    inp['s5_c_re'] = nrm((Dp, G_B, S5_GROUP, P_S5), P_S5 ** -0.5)
    inp['s5_c_im'] = nrm((Dp, G_B, S5_GROUP, P_S5), P_S5 ** -0.5)
    inp['s5_d'] = nrm((Dp, D_B), 1.0)
    inp['s5_w_glu'] = nrm((Dp, D_B, D_B), D_B ** -0.5)
    inp['s5_b_glu'] = nrm((Dp, D_B), 0.01)
    inp['w_pb'] = nrm((Dp, D_B, D_MODEL), D_B ** -0.5)
    inp['w_out'] = nrm((Dp, D_MODEL, D_MODEL), D_MODEL ** -0.5)
    inp['norm_ffn_g'] = 1.0 + nrm((Dp, D_MODEL), 0.02)
    inp['w_rg'] = nrm((Dp, D_MODEL, N_GROUPS), D_MODEL ** -0.5)
    inp['b_rg'] = nrm((Dp, N_GROUPS), 0.01)
    inp['w_rexp'] = nrm((Dp, D_MODEL, N_EXPERTS), D_MODEL ** -0.5)
    inp['b_rexp'] = nrm((Dp, N_EXPERTS), 0.01)
    inp['w_gate'] = nrm((Dp, N_EXPERTS, D_MODEL, D_EXPERT), D_MODEL ** -0.5)
    inp['w_up'] = nrm((Dp, N_EXPERTS, D_MODEL, D_EXPERT), D_MODEL ** -0.5)
    inp['w_down'] = nrm((Dp, N_EXPERTS, D_EXPERT, D_MODEL), D_EXPERT ** -0.5)
    inp['norm_final_g'] = 1.0 + nrm((D_MODEL,), 0.02)
    return inp


def reference(x_prompt, x_sample, state_mlstm_C, state_mlstm_n, state_mlstm_m, state_conv,
              state_s5_re, state_s5_im, norm_mix_g, w_in, b_i, b_f, w_conv, b_conv, head_norm_g,
              w_pa, s5_a_re, s5_a_im, s5_log_step, s5_b_re, s5_b_im, s5_c_re, s5_c_im, s5_d,
              s5_w_glu, s5_b_glu, w_pb, w_out, norm_ffn_g, w_rg, b_rg, w_rexp, b_rexp,
              w_gate, w_up, w_down, norm_final_g):
    stacked = dict(norm_mix_g=norm_mix_g, w_in=w_in, b_i=b_i, b_f=b_f, w_conv=w_conv, b_conv=b_conv,
                   head_norm_g=head_norm_g, w_pa=w_pa, s5_a_re=s5_a_re, s5_a_im=s5_a_im,
                   s5_log_step=s5_log_step, s5_b_re=s5_b_re, s5_b_im=s5_b_im, s5_c_re=s5_c_re,
                   s5_c_im=s5_c_im, s5_d=s5_d, s5_w_glu=s5_w_glu, s5_b_glu=s5_b_glu, w_pb=w_pb,
                   w_out=w_out, norm_ffn_g=norm_ffn_g, w_rg=w_rg, b_rg=b_rg, w_rexp=w_rexp,
                   b_rexp=b_rexp, w_gate=w_gate, w_up=w_up, w_down=w_down)
    f32 = jnp.float32
    B = x_prompt.shape[0]
    xp = x_prompt
    xs = x_sample
    sp = [[] for _ in range(6)]
    ss = [[] for _ in range(6)]
    for l in range(DEPTH):
        p = {name: arr[l] for name, arr in stacked.items()}
        zC = jnp.zeros((B, H_A, DK, DV), f32)
        zn = jnp.zeros((B, H_A, DK), f32)
        zm = jnp.zeros((B, H_A), f32)
        zconv = jnp.zeros((B, CONV_W - 1, 2 * D_A), xp.dtype)
        zs = jnp.zeros((B, G_B, P_S5), f32)
        xp, st_p = layer(xp, zC, zn, zm, zconv, zs, zs, p)
        xs, st_s = layer(xs, state_mlstm_C[l], state_mlstm_n[l], state_mlstm_m[l], state_conv[l],
                         state_s5_re[l], state_s5_im[l], p)
        for j in range(6):
            sp[j].append(st_p[j])
            ss[j].append(st_s[j])
    y_prompt = rmsnorm(xp, norm_final_g)
    y_sample = rmsnorm(xs, norm_final_g)
    C_p, n_p, m_p, conv_p, s5re_p, s5im_p = [jnp.stack(a, axis=0) for a in sp]
    C_s, n_s, m_s, conv_s, s5re_s, s5im_s = [jnp.stack(a, axis=0) for a in ss]
    return (y_prompt, y_sample, C_p, n_p, m_p, conv_p, s5re_p, s5im_p,
            C_s, n_s, m_s, conv_s, s5re_s, s5im_s)
```

```python
import functools
import math

import jax
import jax.numpy as jnp
from jax import lax
from jax.experimental import pallas as pl
from jax.experimental.pallas import tpu as pltpu

F32 = jnp.float32
BF16 = jnp.bfloat16

D_MODEL = 2048
H_A = 8
DK = 256
CONV_W = 4
CHUNK = 128
D_B = 1024
S5_GROUP = 16
G_B = 64
P_S5 = 64
N_GROUPS = 4
EXP_PER_GROUP = 8
N_EXPERTS = 32
D_EXPERT = 512
EPS = 1e-6

LANES = 128
SUBLANES = 8
VMEM_LIMIT = 56 * 1024 * 1024

S5_GB = 4
S5_GPB = G_B // S5_GB
S5_UW = S5_GPB * S5_GROUP
S5_XW = S5_GPB * P_S5
S5_NSEG = SUBLANES
MOE_TILE = 256


def _cparams(sem):
    return pltpu.CompilerParams(dimension_semantics=sem, vmem_limit_bytes=VMEM_LIMIT)


def _silu(x):
    return x * (1.0 / (1.0 + jnp.exp(-x)))


def _sigmoid(x):
    return 1.0 / (1.0 + jnp.exp(-x))


def _log_sigmoid(x):
    return jnp.minimum(x, 0.0) - jnp.log1p(jnp.exp(-jnp.abs(x)))


def _gelu_tanh(x):
    c = math.sqrt(2.0 / math.pi)
    return 0.5 * x * (1.0 + jnp.tanh(c * (x + 0.044715 * (x * x * x))))


def _split3(x):
    hi = x.astype(BF16)
    r = x - hi.astype(F32)
    mid = r.astype(BF16)
    lo = (r - mid.astype(F32)).astype(BF16)
    return hi, mid, lo


def _dot(a, b):
    return jnp.dot(a, b, preferred_element_type=F32)


def _dot_nt(a, b):
    return lax.dot_general(a, b, (((1,), (1,)), ((), ())), preferred_element_type=F32)


def _dot_tn(a, b):
    return lax.dot_general(a, b, (((0,), (0,)), ((), ())), preferred_element_type=F32)


def _rmsnorm_kernel(x_ref, g_ref, *rest):
    o_ref = rest[-1]
    x = x_ref[...]
    r = lax.rsqrt(jnp.mean(x * x, axis=-1, keepdims=True) + EPS)
    o_ref[...] = (x * r * g_ref[...]).astype(o_ref.dtype)


def _rmsnorm_rows(x, g, t_all, row0, tm, alias=None):
    n = x.shape[0]
    blk0 = row0 // tm
    in_specs = [pl.BlockSpec((tm, D_MODEL), lambda i: (i, 0)),
                pl.BlockSpec((1, D_MODEL), lambda i: (0, 0))]
    args = [x, g.reshape(1, D_MODEL)]
    aliases = {}
    if alias is not None:
        in_specs.append(pl.BlockSpec(memory_space=pl.ANY))
        args.append(alias)
        aliases = {2: 0}
    return pl.pallas_call(
        _rmsnorm_kernel,
        out_shape=jax.ShapeDtypeStruct((t_all, D_MODEL), BF16),
        grid=(n // tm,),
        in_specs=in_specs,
        out_specs=pl.BlockSpec((tm, D_MODEL), lambda i: (i + blk0, 0)),
        input_output_aliases=aliases,
        compiler_params=_cparams(("arbitrary",)),
        name="rmsnorm_rows",
    )(*args)


def _mm_kernel(a_ref, w_ref, b_ref, o_ref, wb_ref):
    @pl.when(pl.program_id(1) == 0)
    def _():
        wb_ref[...] = w_ref[...].astype(BF16)

    o_ref[...] = (_dot(a_ref[...], wb_ref[...]) + b_ref[...]).astype(o_ref.dtype)


def _matmul(a, w, bias, n_out, tm, tn, out_dtype):
    m, k = a.shape
    return pl.pallas_call(
        _mm_kernel,
        out_shape=jax.ShapeDtypeStruct((m, n_out), out_dtype),
        grid=(n_out // tn, m // tm),
        in_specs=[pl.BlockSpec((tm, k), lambda j, i: (i, 0)),
                  pl.BlockSpec((k, tn), lambda j, i: (0, j)),
                  pl.BlockSpec((1, tn), lambda j, i: (0, j))],
        out_specs=pl.BlockSpec((tm, tn), lambda j, i: (i, j)),
        scratch_shapes=[pltpu.VMEM((k, tn), BF16)],
        compiler_params=_cparams(("arbitrary", "arbitrary")),
        name="rows_matmul",
    )(a, w, bias)


def _mlstm_kernel(q_ref, k_ref, v_ref, o_ref, gcol_ref, grow_ref, wq_ref, wk_ref, bq_ref, bk_ref,
                  hg_ref, h_ref, c_out, n_out, m_out, c_sc, n_sc, m_sc, qx_sc, kx_sc):
    h = pl.program_id(1)
    c = pl.program_id(2)
    L = CHUNK
    pad = SUBLANES

    @pl.when(c == 0)
    def _():
        c_sc[...] = jnp.zeros_like(c_sc)
        n_sc[...] = jnp.zeros_like(n_sc)
        m_sc[...] = jnp.zeros_like(m_sc)
        qx_sc[0:pad, :] = jnp.zeros((pad, DK), F32)
        kx_sc[0:pad, :] = jnp.zeros((pad, DK), F32)

    def conv_silu(x_ref, ext, w_ref, b_ref):
        x = x_ref[...]
        ext[pad:pad + L, :] = x
        y = b_ref[...] + w_ref[CONV_W - 1:CONV_W, :] * x
        for j in range(1, CONV_W):
            y = y + w_ref[CONV_W - 1 - j:CONV_W - j, :] * ext[pad - j:pad - j + L, :]
        ext[0:pad, :] = x[L - pad:L, :]
        return _silu(y)

    q = conv_silu(q_ref, qx_sc, wq_ref, bq_ref) * (DK ** -0.5)
    k = conv_silu(k_ref, kx_sc, wk_ref, bk_ref)
    v = v_ref[...]

    gcol = gcol_ref[...]
    lane = lax.broadcasted_iota(jnp.int32, gcol.shape, 1)
    ig_col = jnp.sum(jnp.where(lane == h, gcol, 0.0), axis=1, keepdims=True)
    fp_col = jnp.sum(jnp.where(lane == h + H_A, gcol, 0.0), axis=1, keepdims=True)
    grow = grow_ref[...]
    sub = lax.broadcasted_iota(jnp.int32, grow.shape, 0)
    ig_row = jnp.sum(jnp.where(sub == h, grow, 0.0), axis=0, keepdims=True)
    fp_row = jnp.sum(jnp.where(sub == h + H_A, grow, 0.0), axis=0, keepdims=True)
    lf_col = _log_sigmoid(fp_col)
    lf_row = _log_sigmoid(fp_row)

    ri = lax.broadcasted_iota(jnp.int32, (L, L), 0)
    ci = lax.broadcasted_iota(jnp.int32, (L, L), 1)
    causal = ci <= ri
    tril = jnp.where(causal, 1.0, 0.0).astype(BF16)
    triu = jnp.where(ri <= ci, 1.0, 0.0).astype(BF16)
    b_colm = sum(_dot(tril, jnp.broadcast_to(p, (L, L))) for p in _split3(lf_col))
    b_rowm = sum(_dot(jnp.broadcast_to(p, (SUBLANES, L)), triu) for p in _split3(lf_row))
    b_col = b_colm[:, 0:1]
    b_row = b_rowm[0:1, :]

    m_prev = m_sc[...]
    d_log = jnp.where(causal, b_colm - b_row + ig_row, -jnp.inf)
    inter_log = b_col + m_prev
    m_t = jnp.maximum(inter_log, jnp.max(d_log, axis=1, keepdims=True))
    qb = q.astype(BF16)
    kb = k.astype(BF16)
    vb = v.astype(BF16)
    s = _dot_nt(qb, kb) * jnp.exp(d_log - m_t)
    inter_w = jnp.exp(inter_log - m_t)
    c_prev = c_sc[...]
    n_prev = n_sc[...]
    num = _dot(s.astype(BF16), vb) + inter_w * _dot(qb, c_prev.astype(BF16))
    nq = jnp.sum(s, axis=1, keepdims=True) + inter_w * jnp.sum(q * n_prev, axis=1, keepdims=True)
    den = jnp.maximum(jnp.abs(nq), jnp.exp(-m_t))
    hh = num / den
    hh = hh * _sigmoid(o_ref[...])
    hh = hh * lax.rsqrt(jnp.mean(hh * hh, axis=1, keepdims=True) + EPS)
    h_ref[...] = (hh * hg_ref[...]).astype(h_ref.dtype)

    m_new = m_t[L - 1:L, :]
    b_last = b_col[L - 1:L, :]
    decay = jnp.exp(b_last + m_prev - m_new)
    w_end = jnp.exp(b_last - b_col + ig_col - m_new)
    kw = k * w_end
    c_new = decay * c_prev + _dot_tn(kw.astype(BF16), vb)
    n_new = decay * n_prev + jnp.sum(kw, axis=0, keepdims=True)
    c_sc[...] = c_new
    n_sc[...] = n_new
    m_sc[...] = m_new

    @pl.when(c == pl.num_programs(2) - 1)
    def _():
        c_out[...] = c_new
        n_out[...] = n_new
        m_out[...] = jnp.broadcast_to(m_new, m_out.shape)


def _mlstm_prompt(qkvo, gates, gates_t, w_conv, b_conv, head_g, batch, seq, t_all):
    nc = seq // CHUNK
    L = CHUNK
    row = lambda b, h, c: b * nc + c
    in_specs = [
        pl.BlockSpec((L, DK), lambda b, h, c: (row(b, h, c), h)),
        pl.BlockSpec((L, DK), lambda b, h, c: (row(b, h, c), H_A + h)),
        pl.BlockSpec((L, DK), lambda b, h, c: (row(b, h, c), 2 * H_A + h)),
        pl.BlockSpec((L, DK), lambda b, h, c: (row(b, h, c), 3 * H_A + h)),
        pl.BlockSpec((L, LANES), lambda b, h, c: (row(b, h, c), 0)),
        pl.BlockSpec((2 * H_A, L), lambda b, h, c: (0, row(b, h, c))),
        pl.BlockSpec((CONV_W, DK), lambda b, h, c: (0, h)),
        pl.BlockSpec((CONV_W, DK), lambda b, h, c: (0, H_A + h)),
        pl.BlockSpec((1, DK), lambda b, h, c: (0, h)),
        pl.BlockSpec((1, DK), lambda b, h, c: (0, H_A + h)),
        pl.BlockSpec((1, DK), lambda b, h, c: (0, h)),
    ]
    out_shape = (
        jax.ShapeDtypeStruct((t_all, H_A * DK), BF16),
        jax.ShapeDtypeStruct((batch, H_A, DK, DK), F32),
        jax.ShapeDtypeStruct((batch, H_A, 1, DK), F32),
        jax.ShapeDtypeStruct((batch, H_A, 1, LANES), F32),
    )
    out_specs = (
        pl.BlockSpec((L, DK), lambda b, h, c: (row(b, h, c), h)),
        pl.BlockSpec((None, None, DK, DK), lambda b, h, c: (b, h, 0, 0)),
        pl.BlockSpec((None, None, 1, DK), lambda b, h, c: (b, h, 0, 0)),
        pl.BlockSpec((None, None, 1, LANES), lambda b, h, c: (b, h, 0, 0)),
    )
    return pl.pallas_call(
        _mlstm_kernel,
        out_shape=out_shape,
        grid=(batch, H_A, nc),
        in_specs=in_specs,
        out_specs=out_specs,
        scratch_shapes=[pltpu.VMEM((DK, DK), F32), pltpu.VMEM((1, DK), F32), pltpu.VMEM((1, 1), F32),
                        pltpu.VMEM((SUBLANES + L, DK), F32), pltpu.VMEM((SUBLANES + L, DK), F32)],
        compiler_params=_cparams(("arbitrary", "arbitrary", "arbitrary")),
        name="mlstm_prompt",
    )(qkvo, qkvo, qkvo, qkvo, gates, gates_t, w_conv, w_conv, b_conv, b_conv, head_g)


def _s5_prompt_kernel(u_ref, bb_ref, cc_ref, lbr_ref, lbi_ref, d_ref, ys_ref, sre_ref, sim_ref, x_sc,
                      *, seq):
    seg = seq // S5_NSEG
    nt = S5_XW // LANES
    rows = 256
    for r in range(seq // rows):
        ub = u_ref[r * rows:(r + 1) * rows, :].astype(BF16)
        bu = _dot(ub, bb_ref[...])
        for j in range(2 * nt):
            x_sc[j, r * rows:(r + 1) * rows, :] = bu[:, j * LANES:(j + 1) * LANES]

    grp = 4
    for t0 in range(0, nt, grp):
        tiles = list(range(t0, t0 + grp))
        lbr = [jnp.broadcast_to(lbr_ref[:, j * LANES:(j + 1) * LANES], (S5_NSEG, LANES)) for j in tiles]
        lbi = [jnp.broadcast_to(lbi_ref[:, j * LANES:(j + 1) * LANES], (S5_NSEG, LANES)) for j in tiles]

        def scan_body(kk, carry, lbr=lbr, lbi=lbi, tiles=tiles):
            seg_rows = pl.ds(kk, S5_NSEG, stride=seg)
            out = []
            for i, j in enumerate(tiles):
                xr, xi = carry[i]
                nr = lbr[i] * xr - lbi[i] * xi + x_sc[j, seg_rows, :]
                ni = lbr[i] * xi + lbi[i] * xr + x_sc[nt + j, seg_rows, :]
                x_sc[j, seg_rows, :] = nr
                x_sc[nt + j, seg_rows, :] = ni
                out.append((nr, ni))
            return tuple(out)

        z = jnp.zeros((S5_NSEG, LANES), F32)
        ends = lax.fori_loop(0, seg, scan_body, tuple((z, z) for _ in tiles))

        starts = []
        for i, j in enumerate(tiles):
            er, ei = ends[i]
            pr, pi = lbr[i][0:1, :], lbi[i][0:1, :]
            for _ in range(int(math.log2(seg))):
                pr, pi = pr * pr - pi * pi, 2.0 * pr * pi
            sr_rows = [jnp.zeros((1, LANES), F32)]
            si_rows = [jnp.zeros((1, LANES), F32)]
            for s in range(S5_NSEG):
                sr_rows.append(er[s:s + 1, :] + pr * sr_rows[s] - pi * si_rows[s])
                si_rows.append(ei[s:s + 1, :] + pr * si_rows[s] + pi * sr_rows[s])
            sre_ref[:, j * LANES:(j + 1) * LANES] = sr_rows[S5_NSEG]
            sim_ref[:, j * LANES:(j + 1) * LANES] = si_rows[S5_NSEG]
            starts.append((jnp.concatenate(sr_rows[:S5_NSEG], axis=0),
                           jnp.concatenate(si_rows[:S5_NSEG], axis=0)))

        def fix_body(kk, pw, lbr=lbr, lbi=lbi, starts=starts, tiles=tiles):
            seg_rows = pl.ds(kk, S5_NSEG, stride=seg)
            out = []
            for i, j in enumerate(tiles):
                wr, wi = pw[i]
                s_r, s_i = starts[i]
                x_sc[j, seg_rows, :] = x_sc[j, seg_rows, :] + wr * s_r - wi * s_i
                x_sc[nt + j, seg_rows, :] = x_sc[nt + j, seg_rows, :] + wr * s_i + wi * s_r
                out.append((wr * lbr[i] - wi * lbi[i], wr * lbi[i] + wi * lbr[i]))
            return tuple(out)

        lax.fori_loop(0, seg, fix_body, tuple((lbr[i], lbi[i]) for i in range(grp)))

    for r in range(seq // rows):
        xb = jnp.concatenate([x_sc[j, r * rows:(r + 1) * rows, :] for j in range(2 * nt)], axis=1)
        y = _dot(xb.astype(BF16), cc_ref[...]) + d_ref[...] * u_ref[r * rows:(r + 1) * rows, :]
        ys_ref[r * rows:(r + 1) * rows, :] = _gelu_tanh(y).astype(ys_ref.dtype)


def _s5_prompt(proj2, u_col0, bb, cc, lbr, lbi, d_row, batch, seq, t_all):
    kern = functools.partial(_s5_prompt_kernel, seq=seq)
    ub0 = u_col0 // S5_UW
    return pl.pallas_call(
        kern,
        out_shape=(jax.ShapeDtypeStruct((t_all, D_B), F32),
                   jax.ShapeDtypeStruct((batch, 1, G_B * P_S5), F32),
                   jax.ShapeDtypeStruct((batch, 1, G_B * P_S5), F32)),
        grid=(S5_GB, batch),
        in_specs=[pl.BlockSpec((seq, S5_UW), lambda g, b: (b, ub0 + g)),
                  pl.BlockSpec((None, S5_UW, 2 * S5_XW), lambda g, b: (g, 0, 0)),
                  pl.BlockSpec((None, 2 * S5_XW, S5_UW), lambda g, b: (g, 0, 0)),
                  pl.BlockSpec((None, 1, S5_XW), lambda g, b: (g, 0, 0)),
                  pl.BlockSpec((None, 1, S5_XW), lambda g, b: (g, 0, 0)),
                  pl.BlockSpec((1, S5_UW), lambda g, b: (0, g))],
        out_specs=(pl.BlockSpec((seq, S5_UW), lambda g, b: (b, g)),
                   pl.BlockSpec((None, 1, S5_XW), lambda g, b: (b, 0, g)),
                   pl.BlockSpec((None, 1, S5_XW), lambda g, b: (b, 0, g))),
        scratch_shapes=[pltpu.VMEM((2 * S5_XW // LANES, seq, LANES), F32)],
        compiler_params=_cparams(("arbitrary", "arbitrary")),
        name="s5_prompt",
    )(proj2, bb, cc, lbr, lbi, d_row)


def _sample_pre_kernel(qk_ref, conv_ref, wc_ref, bc_ref, g_ref, m_ref, n_ref,
                       q_out, kw_out, a_out, s_out, den_out, conv_out, n_out, m_out):
    c2 = 2 * H_A * DK
    x_new = qk_ref[...]
    y = bc_ref[...] + wc_ref[CONV_W - 1:CONV_W, :] * x_new
    for j in range(CONV_W - 1):
        y = y + wc_ref[j:j + 1, :] * conv_ref[:, j * c2:(j + 1) * c2]
    y = _silu(y)
    conv_out[:, 0:(CONV_W - 2) * c2] = conv_ref[:, c2:(CONV_W - 1) * c2]
    conv_out[:, (CONV_W - 2) * c2:(CONV_W - 1) * c2] = x_new
    g = g_ref[...]
    bd = x_new.shape[0]
    m_cols = []
    for h in range(H_A):
        sl = slice(h * DK, (h + 1) * DK)
        q = y[:, sl] * (DK ** -0.5)
        k = y[:, H_A * DK + h * DK:H_A * DK + (h + 1) * DK]
        ig = g[:, h:h + 1]
        lf = _log_sigmoid(g[:, H_A + h:H_A + h + 1])
        m_prev = m_ref[:, h:h + 1]
        m_t = jnp.maximum(lf + m_prev, ig)
        a = jnp.exp(lf + m_prev - m_t)
        wgt = jnp.exp(ig - m_t)
        n_prev = n_ref[:, sl]
        s = jnp.sum(q * k, axis=1, keepdims=True) * wgt
        nq = s + a * jnp.sum(q * n_prev, axis=1, keepdims=True)
        den = jnp.maximum(jnp.abs(nq), jnp.exp(-m_t))
        kw = wgt * k
        q_out[:, sl] = q
        kw_out[:, sl] = kw
        a_out[:, sl] = jnp.broadcast_to(a, (bd, DK))
        s_out[:, sl] = jnp.broadcast_to(s, (bd, DK))
        den_out[:, sl] = jnp.broadcast_to(den, (bd, DK))
        n_out[:, sl] = a * n_prev + kw
        m_cols.append(m_t)
    lane = lax.broadcasted_iota(jnp.int32, (bd, LANES), 1)
    m_full = jnp.zeros((bd, LANES), F32)
    for h in range(H_A):
        m_full = jnp.where(lane == h, m_cols[h], m_full)
    m_out[...] = m_full


def _sample_pre(qkvo, conv_state, w_conv, b_conv, gates, m_state, n_state, dec, row_blk):
    c2 = 2 * H_A * DK
    d = H_A * DK
    full = lambda shape: pl.BlockSpec(shape, lambda i: (0,) * len(shape))
    rows = lambda: jax.ShapeDtypeStruct((dec, d), F32)
    return pl.pallas_call(
        _sample_pre_kernel,
        out_shape=(rows(), rows(), rows(), rows(), rows(),
                   jax.ShapeDtypeStruct((dec, (CONV_W - 1) * c2), F32), rows(),
                   jax.ShapeDtypeStruct((dec, LANES), F32)),
        grid=(1,),
        in_specs=[pl.BlockSpec((dec, c2), lambda i: (row_blk, 0)),
                  full((dec, (CONV_W - 1) * c2)), full((CONV_W, c2)), full((1, c2)),
                  pl.BlockSpec((dec, LANES), lambda i: (row_blk, 0)),
                  full((dec, H_A)), full((dec, d))],
        out_specs=(full((dec, d)), full((dec, d)), full((dec, d)), full((dec, d)), full((dec, d)),
                   full((dec, (CONV_W - 1) * c2)), full((dec, d)), full((dec, LANES))),
        compiler_params=_cparams(("arbitrary",)),
        name="sample_pre",
    )(qkvo, conv_state, w_conv, b_conv, gates, m_state, n_state)


def _sample_c_kernel(q_ref, kw_ref, v_ref, a_ref, c_ref, c_out, qc_out):
    eye = jnp.where(lax.broadcasted_iota(jnp.int32, (DK, DK), 0)
                    == lax.broadcasted_iota(jnp.int32, (DK, DK), 1), 1.0, 0.0).astype(BF16)
    q_t = sum(_dot_nt(eye, p) for p in _split3(q_ref[...]))
    kw_t = sum(_dot_nt(eye, p) for p in _split3(kw_ref[...]))
    qc_rows = []
    for h in range(H_A):
        c_prev = c_ref[h]
        qc_rows.append(jnp.sum(q_t[:, h:h + 1] * c_prev, axis=0, keepdims=True))
        c_out[h] = a_ref[h:h + 1, :] * c_prev + kw_t[:, h:h + 1] * v_ref[h:h + 1, :]
    qc_out[...] = jnp.concatenate(qc_rows, axis=0)


def _sample_c(q3, kw3, v3, a3, c_state, dec):
    vec = pl.BlockSpec((None, H_A, DK), lambda b: (b, 0, 0))
    mat = pl.BlockSpec((None, H_A, DK, DK), lambda b: (b, 0, 0, 0))
    return pl.pallas_call(
        _sample_c_kernel,
        out_shape=(jax.ShapeDtypeStruct((dec, H_A, DK, DK), F32),
                   jax.ShapeDtypeStruct((dec, H_A, DK), F32)),
        grid=(dec,),
        in_specs=[vec, vec, vec, vec, mat],
        out_specs=(mat, vec),
        compiler_params=_cparams(("arbitrary",)),
        name="sample_c_update",
    )(q3, kw3, v3, a3, c_state)


def _sample_post_kernel(qc_ref, s_ref, a_ref, den_ref, vo_ref, hg_ref, hn_in, h_ref):
    del hn_in
    d = H_A * DK
    num = s_ref[...] * vo_ref[:, 0:d] + a_ref[...] * qc_ref[...]
    hh = num / den_ref[...]
    hh = hh * _sigmoid(vo_ref[:, d:2 * d])
    for h in range(H_A):
        sl = slice(h * DK, (h + 1) * DK)
        seg = hh[:, sl]
        seg = seg * lax.rsqrt(jnp.mean(seg * seg, axis=1, keepdims=True) + EPS)
        h_ref[:, sl] = (seg * hg_ref[:, sl]).astype(h_ref.dtype)


def _sample_post(qc, s_e, a_e, den_e, qkvo, head_g, hn_all, dec, row_blk):
    d = H_A * DK
    full = lambda shape: pl.BlockSpec(shape, lambda i: (0,) * len(shape))
    return pl.pallas_call(
        _sample_post_kernel,
        out_shape=jax.ShapeDtypeStruct(hn_all.shape, hn_all.dtype),
        grid=(1,),
        in_specs=[full((dec, d)), full((dec, d)), full((dec, d)), full((dec, d)),
                  pl.BlockSpec((dec, 2 * d), lambda i: (row_blk, 1)),
                  full((1, d)), pl.BlockSpec(memory_space=pl.ANY)],
        out_specs=pl.BlockSpec((dec, d), lambda i: (row_blk, 0)),
        input_output_aliases={6: 0},
        compiler_params=_cparams(("arbitrary",)),
        name="sample_post",
    )(qc, s_e, a_e, den_e, qkvo, head_g, hn_all)


def _s5_sample_kernel(u_ref, sr_ref, si_ref, bb_ref, cc_ref, lbr_ref, lbi_ref, d_ref, ys_in,
                      ys_ref, sre_out, sim_out):
    del ys_in
    for g in range(S5_GB):
        u = u_ref[:, g * S5_UW:(g + 1) * S5_UW]
        bu = _dot(u.astype(BF16), bb_ref[g])
        sl = slice(g * S5_XW, (g + 1) * S5_XW)
        lbr = lbr_ref[g]
        lbi = lbi_ref[g]
        sr = sr_ref[:, sl]
        si = si_ref[:, sl]
        xr = lbr * sr - lbi * si + bu[:, 0:S5_XW]
        xi = lbr * si + lbi * sr + bu[:, S5_XW:2 * S5_XW]
        sre_out[:, sl] = xr
        sim_out[:, sl] = xi
        x = jnp.concatenate([xr, xi], axis=1).astype(BF16)
        y = _dot(x, cc_ref[g]) + d_ref[:, g * S5_UW:(g + 1) * S5_UW] * u
        ys_ref[:, g * S5_UW:(g + 1) * S5_UW] = _gelu_tanh(y).astype(ys_ref.dtype)


def _s5_sample(proj2, u_col0, s_re, s_im, bb, cc, lbr, lbi, d_row, ys_all, dec, row_blk):
    full = lambda shape: pl.BlockSpec(shape, lambda i: (0,) * len(shape))
    n_state = G_B * P_S5
    ub0 = u_col0 // D_B
    return pl.pallas_call(
        _s5_sample_kernel,
        out_shape=(jax.ShapeDtypeStruct(ys_all.shape, ys_all.dtype),
                   jax.ShapeDtypeStruct((dec, n_state), F32),
                   jax.ShapeDtypeStruct((dec, n_state), F32)),
        grid=(1,),
        in_specs=[pl.BlockSpec((dec, D_B), lambda i: (row_blk, ub0)),
                  full((dec, n_state)), full((dec, n_state)),
                  full(bb.shape), full(cc.shape), full(lbr.shape), full(lbi.shape), full((1, D_B)),
                  pl.BlockSpec(memory_space=pl.ANY)],
        out_specs=(pl.BlockSpec((dec, D_B), lambda i: (row_blk, 0)),
                   full((dec, n_state)), full((dec, n_state))),
        input_output_aliases={8: 0},
        compiler_params=_cparams(("arbitrary",)),
        name="s5_sample",
    )(proj2, s_re, s_im, bb, cc, lbr, lbi, d_row, ys_all)


def _merge_kernel(x_ref, hn_ref, ys_ref, ga_ref, gb_ref, bglu_ref, gffn_ref, wrh_ref, wrl_ref, br_ref,
                  wpa_hbm, wglu_hbm, wpb_hbm, wout_hbm, *rest):
    n_alias = len(rest) - 9
    x1_ref, xn_ref, ids_ref, wts_ref = rest[n_alias:n_alias + 4]
    wpa, wglu, wpb, wout, sem = rest[n_alias + 4:]

    @pl.when(pl.program_id(0) == 0)
    def _():
        copies = [pltpu.make_async_copy(src, dst, sem.at[i])
                  for i, (src, dst) in enumerate(((wpa_hbm, wpa), (wglu_hbm, wglu),
                                                  (wpb_hbm, wpb), (wout_hbm, wout)))]
        for cp in copies:
            cp.start()
        for cp in copies:
            cp.wait()

    ya = _dot(hn_ref[...], wpa[...])
    ys = ys_ref[...]
    gate = _sigmoid(_dot(ys.astype(BF16), wglu[...]) + bglu_ref[...])
    yb = _dot((ys * gate).astype(BF16), wpb[...])
    z = _sigmoid(ga_ref[...]) * ya + _sigmoid(gb_ref[...]) * yb
    x1 = x_ref[...] + _dot(z.astype(BF16), wout[...])
    x1_ref[...] = x1
    xn = x1 * lax.rsqrt(jnp.mean(x1 * x1, axis=1, keepdims=True) + EPS) * gffn_ref[...]
    xh = xn.astype(BF16)
    xn_ref[...] = xh
    xl = (xn - xh.astype(F32)).astype(BF16)
    logits = _dot(xh, wrh_ref[...]) + _dot(xl, wrh_ref[...]) + _dot(xh, wrl_ref[...]) + br_ref[...]

    lane_i = lax.broadcasted_iota(jnp.int32, logits.shape, 1)
    lane = lane_i.astype(F32)
    neg = -jnp.inf
    big = float(1 << 20)
    gl = jnp.where(lane_i < N_GROUPS, logits, neg)
    gmax = jnp.max(gl, axis=1, keepdims=True)
    gsum = jnp.sum(jnp.exp(gl - gmax), axis=1, keepdims=True)
    gidx = jnp.min(jnp.where(gl == gmax, lane, big), axis=1, keepdims=True)
    pg_sel = 1.0 / gsum
    lo = N_GROUPS + gidx * EXP_PER_GROUP
    in_grp = (lane >= lo) & (lane < lo + EXP_PER_GROUP)
    el = jnp.where(in_grp, logits, neg)
    emax = jnp.max(el, axis=1, keepdims=True)
    ee = jnp.exp(el - emax)
    pe = ee / jnp.sum(ee, axis=1, keepdims=True)
    v0 = jnp.max(pe, axis=1, keepdims=True)
    i0 = jnp.min(jnp.where(in_grp & (pe == v0), lane, big), axis=1, keepdims=True)
    rest_m = in_grp & (lane != i0)
    pe1 = jnp.where(rest_m, pe, neg)
    v1 = jnp.max(pe1, axis=1, keepdims=True)
    i1 = jnp.min(jnp.where(rest_m & (pe1 == v1), lane, big), axis=1, keepdims=True)
    tot = v0 + v1
    w0 = pg_sel * (v0 / tot)
    w1 = pg_sel * (v1 / tot)
    ids = jnp.where(lane_i == 0, i0 - N_GROUPS, jnp.where(lane_i == 1, i1 - N_GROUPS, 0.0))
    ids_ref[...] = ids.astype(jnp.int32)
    wts_ref[...] = jnp.where(lane_i == 0, w0, jnp.where(lane_i == 1, w1, 0.0))


def _merge(x, hn_all, ys_all, proj2, b_glu, g_ffn, wr_hi, wr_lo, b_r, wpa, wglu, wpb, wout,
           t_all, row0, tm, aliases=None):
    n = x.shape[0]
    blk0 = row0 // tm
    const = lambda shape: pl.BlockSpec(shape, lambda i: (0,) * len(shape))
    any_spec = pl.BlockSpec(memory_space=pl.ANY)
    in_specs = [pl.BlockSpec((tm, D_MODEL), lambda i: (i, 0)),
                pl.BlockSpec((tm, D_MODEL), lambda i: (i + blk0, 0)),
                pl.BlockSpec((tm, D_B), lambda i: (i + blk0, 0)),
                pl.BlockSpec((tm, D_MODEL), lambda i: (i + blk0, 0)),
                pl.BlockSpec((tm, D_MODEL), lambda i: (i + blk0, 1)),
                const((1, D_B)), const((1, D_MODEL)),
                const((D_MODEL, LANES)), const((D_MODEL, LANES)), const((1, LANES)),
                any_spec, any_spec, any_spec, any_spec]
    args = [x, hn_all, ys_all, proj2, proj2, b_glu, g_ffn, wr_hi, wr_lo, b_r, wpa, wglu, wpb, wout]
    io_alias = {}
    if aliases is not None:
        for j, a in enumerate(aliases):
            in_specs.append(any_spec)
            args.append(a)
            io_alias[14 + j] = j
    out_shape = (jax.ShapeDtypeStruct((t_all, D_MODEL), F32),
                 jax.ShapeDtypeStruct((t_all, D_MODEL), BF16),
                 jax.ShapeDtypeStruct((t_all, LANES), jnp.int32),
                 jax.ShapeDtypeStruct((t_all, LANES), F32))
    out_specs = (pl.BlockSpec((tm, D_MODEL), lambda i: (i + blk0, 0)),
                 pl.BlockSpec((tm, D_MODEL), lambda i: (i + blk0, 0)),
                 pl.BlockSpec((tm, LANES), lambda i: (i + blk0, 0)),
                 pl.BlockSpec((tm, LANES), lambda i: (i + blk0, 0)))
    return pl.pallas_call(
        _merge_kernel,
        out_shape=out_shape,
        grid=(n // tm,),
        in_specs=in_specs,
        out_specs=out_specs,
        scratch_shapes=[pltpu.VMEM(wpa.shape, BF16), pltpu.VMEM(wglu.shape, BF16),
                        pltpu.VMEM(wpb.shape, BF16), pltpu.VMEM(wout.shape, BF16),
                        pltpu.SemaphoreType.DMA((4,))],
        input_output_aliases=io_alias,
        compiler_params=_cparams(("arbitrary",)),
        name="merge_router",
    )(*args)


def _moe_kernel(te_ref, nt_ref, xs_ref, wg_ref, wu_ref, wd_ref, o_ref, wg_sc, wu_sc, wd_sc):
    i = pl.program_id(0)
    e = te_ref[i]
    prev = te_ref[jnp.maximum(i - 1, 0)]

    @pl.when((i == 0) | (e != prev))
    def _():
        wg_sc[...] = wg_ref[...].astype(BF16)
        wu_sc[...] = wu_ref[...].astype(BF16)
        wd_sc[...] = wd_ref[...].astype(BF16)

    @pl.when(i < nt_ref[0])
    def _():
        x = xs_ref[...]
        hg = _dot(x, wg_sc[...])
        hu = _dot(x, wu_sc[...])
        hh = (_silu(hg) * hu).astype(BF16)
        o_ref[...] = _dot(hh, wd_sc[...])


def _moe_experts(tile_expert, n_tiles, xs, w_gate, w_up, w_down):
    p_rows = xs.shape[0]
    last = lambda i, te, nt: jnp.minimum(i, nt[0] - 1)
    grid_spec = pltpu.PrefetchScalarGridSpec(
        num_scalar_prefetch=2,
        grid=(p_rows // MOE_TILE,),
        in_specs=[pl.BlockSpec((MOE_TILE, D_MODEL), lambda i, te, nt: (last(i, te, nt), 0)),
                  pl.BlockSpec((None, D_MODEL, D_EXPERT), lambda i, te, nt: (te[i], 0, 0)),
                  pl.BlockSpec((None, D_MODEL, D_EXPERT), lambda i, te, nt: (te[i], 0, 0)),
                  pl.BlockSpec((None, D_EXPERT, D_MODEL), lambda i, te, nt: (te[i], 0, 0))],
        out_specs=pl.BlockSpec((MOE_TILE, D_MODEL), lambda i, te, nt: (last(i, te, nt), 0)),
        scratch_shapes=[pltpu.VMEM((D_MODEL, D_EXPERT), BF16), pltpu.VMEM((D_MODEL, D_EXPERT), BF16),
                        pltpu.VMEM((D_EXPERT, D_MODEL), BF16)],
    )
    return pl.pallas_call(
        _moe_kernel,
        out_shape=jax.ShapeDtypeStruct((p_rows, D_MODEL), F32),
        grid_spec=grid_spec,
        compiler_params=_cparams(("arbitrary",)),
        name="moe_experts",
    )(tile_expert, n_tiles, xs, w_gate, w_up, w_down)


def _final_kernel(x1_ref, y0_ref, y1_ref, w_ref, g_ref, o_ref):
    w = w_ref[...]
    x2 = x1_ref[...] + w[:, 0:1] * y0_ref[...] + w[:, 1:2] * y1_ref[...]
    o_ref[...] = x2 * lax.rsqrt(jnp.mean(x2 * x2, axis=1, keepdims=True) + EPS) * g_ref[...]


def _final(x1_all, yg0, yg1, wts, g_final, row0, n, tm):
    blk0 = row0 // tm
    rows = pl.BlockSpec((tm, D_MODEL), lambda i: (i + blk0, 0))
    return pl.pallas_call(
        _final_kernel,
        out_shape=jax.ShapeDtypeStruct((n, D_MODEL), F32),
        grid=(n // tm,),
        in_specs=[rows, rows, rows,
                  pl.BlockSpec((tm, LANES), lambda i: (i + blk0, 0)),
                  pl.BlockSpec((1, D_MODEL), lambda i: (0, 0))],
        out_specs=pl.BlockSpec((tm, D_MODEL), lambda i: (i, 0)),
        compiler_params=_cparams(("arbitrary",)),
        name="combine_final_norm",
    )(x1_all, yg0, yg1, wts, g_final)


def _s5_params(a_re, a_im, log_step, b_re, b_im, c_re, c_im):
    dt = jnp.exp(log_step)[:, None]
    mag = jnp.exp(a_re * dt)
    lb_re = mag * jnp.cos(a_im * dt)
    lb_im = mag * jnp.sin(a_im * dt)
    den = a_re * a_re + a_im * a_im
    nr = lb_re - 1.0
    coef_re = (nr * a_re + lb_im * a_im) / den
    coef_im = (lb_im * a_re - nr * a_im) / den
    bb_re = coef_re[..., None] * b_re - coef_im[..., None] * b_im
    bb_im = coef_re[..., None] * b_im + coef_im[..., None] * b_re
    eye = jnp.eye(S5_GPB, dtype=F32)

    def blockdiag_in(m):
        m = m.reshape(S5_GB, S5_GPB, P_S5, S5_GROUP)
        return jnp.einsum('bgpc,gh->bgchp', m, eye).reshape(S5_GB, S5_UW, S5_XW)

    def blockdiag_out(m):
        m = m.reshape(S5_GB, S5_GPB, S5_GROUP, P_S5)
        return jnp.einsum('bgcp,gh->bgphc', m, eye).reshape(S5_GB, S5_XW, S5_UW)

    bb = jnp.concatenate([blockdiag_in(bb_re), blockdiag_in(bb_im)], axis=2).astype(BF16)
    cc = jnp.concatenate([blockdiag_out(c_re), -blockdiag_out(c_im)], axis=1).astype(BF16)
    lbr = lb_re.reshape(S5_GB, 1, S5_XW)
    lbi = lb_im.reshape(S5_GB, 1, S5_XW)
    return bb, cc, lbr, lbi


def _dispatch(ids, t_all, p_rows):
    e = ids[:, :2].reshape(-1)
    onehot = (e[:, None] == jnp.arange(N_EXPERTS, dtype=jnp.int32)[None, :]).astype(jnp.int32)
    csum = jnp.cumsum(onehot, axis=0)
    rank = jnp.sum((csum - onehot) * onehot, axis=1)
    counts = csum[-1]
    tiles = (counts + MOE_TILE - 1) // MOE_TILE
    tile_end = jnp.cumsum(tiles)
    tile_start = tile_end - tiles
    pos = (tile_start * MOE_TILE)[e] + rank
    tok = jnp.arange(2 * t_all, dtype=jnp.int32) // 2
    src = jnp.zeros((p_rows,), jnp.int32).at[pos].set(tok)
    n_tiles = tile_end[-1]
    tidx = jnp.arange(p_rows // MOE_TILE, dtype=jnp.int32)
    tclamp = jnp.minimum(tidx, n_tiles - 1)
    tile_expert = jnp.sum((tile_end[None, :] <= tclamp[:, None]).astype(jnp.int32), axis=1)
    return pos.reshape(t_all, 2), src, tile_expert, n_tiles.reshape(1).astype(jnp.int32)


def _pick_tile(n, candidates):
    for c in candidates:
        if n % c == 0:
            return c
    raise ValueError(f"no row tile for {n}")


def kernel(x_prompt, x_sample, state_mlstm_C, state_mlstm_n, state_mlstm_m, state_conv, state_s5_re,
           state_s5_im, norm_mix_g, w_in, b_i, b_f, w_conv, b_conv, head_norm_g, w_pa, s5_a_re, s5_a_im,
           s5_log_step, s5_b_re, s5_b_im, s5_c_re, s5_c_im, s5_d, s5_w_glu, s5_b_glu, w_pb, w_out,
           norm_ffn_g, w_rg, b_rg, w_rexp, b_rexp, w_gate, w_up, w_down, norm_final_g):
    assert state_mlstm_C.shape[0] == 1 and x_sample.shape[1] == 1
    batch, seq, _ = x_prompt.shape
    dec = x_sample.shape[0]
    t_p = batch * seq
    t_all = t_p + dec
    assert seq % CHUNK == 0 and t_p % dec == 0 and dec % LANES == 0
    d_a = H_A * DK
    tm_p = _pick_tile(t_p, (512, 256, 128))
    tm_all = _pick_tile(t_all, (640, 384, 128))
    sample_blk = t_p // dec

    xp = x_prompt.reshape(t_p, D_MODEL)
    xs = x_sample.reshape(dec, D_MODEL)

    g_mix = norm_mix_g[0]
    xn_all = _rmsnorm_rows(xp, g_mix, t_all, 0, tm_p)
    xn_all = _rmsnorm_rows(xs, g_mix, t_all, t_p, dec, alias=xn_all)
    w_in0 = w_in[0]
    n_qkvo = 4 * d_a
    zero_bias = jnp.zeros((1, w_in0.shape[1]), F32)
    qkvo = _matmul(xn_all, w_in0, zero_bias, n_qkvo, tm_all, 1024, F32)
    n_gate_cols = 2 * H_A
    w_gates = jnp.pad(w_in0[:, n_qkvo:n_qkvo + n_gate_cols], ((0, 0), (0, LANES - n_gate_cols)))
    b_gates = jnp.pad(jnp.concatenate([b_i[0], b_f[0]]), (0, LANES - n_gate_cols)).reshape(1, LANES)
    gates = _matmul(xn_all, w_gates, b_gates, LANES, tm_all, LANES, F32)
    c_u = n_qkvo + n_gate_cols
    c_ga = c_u + D_B
    w_rest = jnp.concatenate([w_in0[:, c_ga:], w_in0[:, c_u:c_ga]], axis=1)
    proj2 = _matmul(xn_all, w_rest, jnp.zeros((1, w_rest.shape[1]), F32), w_rest.shape[1], tm_all, 1024, F32)
    u_col0 = 2 * D_MODEL

    gates_t = gates[:t_p, :n_gate_cols].T
    head_g = head_norm_g[0].reshape(1, d_a)
    hn_all, c_p, n_p, m_p = _mlstm_prompt(qkvo, gates, gates_t, w_conv[0], b_conv[0].reshape(1, -1),
                                          head_g, batch, seq, t_all)
    conv_p = jnp.stack([qkvo[b * seq + seq - (CONV_W - 1):(b + 1) * seq, :2 * d_a] for b in range(batch)])

    bb, cc, lbr, lbi = _s5_params(s5_a_re[0], s5_a_im[0], s5_log_step[0], s5_b_re[0], s5_b_im[0],
                                  s5_c_re[0], s5_c_im[0])
    d_row = s5_d[0].reshape(1, D_B)
    ys_all, s5re_p, s5im_p = _s5_prompt(proj2, u_col0, bb, cc, lbr, lbi, d_row, batch, seq, t_all)

    conv_s_in = state_conv[0].reshape(dec, (CONV_W - 1) * 2 * d_a)
    q_s, kw_s, a_e, s_e, den_e, conv_s, n_s, m_s = _sample_pre(
        qkvo, conv_s_in, w_conv[0], b_conv[0].reshape(1, -1), gates, state_mlstm_m[0],
        state_mlstm_n[0].reshape(dec, d_a), dec, sample_blk)
    v_s = qkvo[t_p:, 2 * d_a:3 * d_a]
    r3 = lambda a: a.reshape(dec, H_A, DK)
    c_s, qc = _sample_c(r3(q_s), r3(kw_s), r3(v_s), r3(a_e), state_mlstm_C[0], dec)
    hn_all = _sample_post(qc.reshape(dec, d_a), s_e, a_e, den_e, qkvo, head_g, hn_all, dec, sample_blk)

    ys_all, s5re_s, s5im_s = _s5_sample(proj2, u_col0, state_s5_re[0].reshape(dec, -1),
                                        state_s5_im[0].reshape(dec, -1),
                                        bb, cc, lbr, lbi, d_row, ys_all, dec, sample_blk)

    wr = jnp.pad(jnp.concatenate([w_rg[0], w_rexp[0]], axis=1), ((0, 0), (0, LANES - N_GROUPS - N_EXPERTS)))
    wr_hi = wr.astype(BF16)
    wr_lo = (wr - wr_hi.astype(F32)).astype(BF16)
    b_r = jnp.pad(jnp.concatenate([b_rg[0], b_rexp[0]]), (0, LANES - N_GROUPS - N_EXPERTS)).reshape(1, LANES)
    merge_w = (w_pa[0].astype(BF16), s5_w_glu[0].astype(BF16), w_pb[0].astype(BF16), w_out[0].astype(BF16))
    b_glu = s5_b_glu[0].reshape(1, D_B)
    g_ffn = norm_ffn_g[0].reshape(1, D_MODEL)
    tm_m = _pick_tile(t_p, (256, 128))
    outs = _merge(xp, hn_all, ys_all, proj2, b_glu, g_ffn, wr_hi, wr_lo, b_r, *merge_w, t_all, 0, tm_m)
    x1_all, xn2_all, ids, wts = _merge(xs, hn_all, ys_all, proj2, b_glu, g_ffn, wr_hi, wr_lo, b_r,
                                       *merge_w, t_all, t_p, dec, aliases=outs)

    p_rows = -(-(2 * t_all + N_EXPERTS * (MOE_TILE - 1)) // MOE_TILE) * MOE_TILE
    pos, src, tile_expert, n_tiles = _dispatch(ids, t_all, p_rows)
    xs_sorted = jnp.take(xn2_all, src, axis=0)
    yp = _moe_experts(tile_expert, n_tiles, xs_sorted, w_gate[0], w_up[0], w_down[0])
    yg0 = jnp.take(yp, pos[:, 0], axis=0)
    yg1 = jnp.take(yp, pos[:, 1], axis=0)

    g_fin = norm_final_g.reshape(1, D_MODEL)
    y_prompt = _final(x1_all, yg0, yg1, wts, g_fin, 0, t_p, tm_p).reshape(batch, seq, D_MODEL)
    y_sample = _final(x1_all, yg0, yg1, wts, g_fin, t_p, dec, dec).reshape(dec, 1, D_MODEL)

    lead = lambda a, shape: a.reshape((1,) + shape)
    return (y_prompt, y_sample,
            lead(c_p, (batch, H_A, DK, DK)), lead(n_p, (batch, H_A, DK)), lead(m_p[..., 0, 0], (batch, H_A)),
            lead(conv_p, (batch, CONV_W - 1, 2 * d_a)),
            lead(s5re_p, (batch, G_B, P_S5)), lead(s5im_p, (batch, G_B, P_S5)),
            lead(c_s, (dec, H_A, DK, DK)), lead(n_s, (dec, H_A, DK)), lead(m_s[:, :H_A], (dec, H_A)),
            lead(conv_s, (dec, CONV_W - 1, 2 * d_a)),
            lead(s5re_s, (dec, G_B, P_S5)), lead(s5im_s, (dec, G_B, P_S5)))
```

```python
import functools
import math

import jax
import jax.numpy as jnp
from jax import lax
from jax.experimental import pallas as pl
from jax.experimental.pallas import tpu as pltpu

F32 = jnp.float32
BF16 = jnp.bfloat16

D_MODEL = 2048
H_A = 8
DK = 256
CONV_W = 4
CHUNK = 128
D_B = 1024
S5_GROUP = 16
G_B = 64
P_S5 = 64
N_GROUPS = 4
EXP_PER_GROUP = 8
N_EXPERTS = 32
D_EXPERT = 512
EPS = 1e-6

LANES = 128
SUBLANES = 8
VMEM_LIMIT = 56 * 1024 * 1024

S5_GB = 4
S5_GPB = G_B // S5_GB
S5_UW = S5_GPB * S5_GROUP
S5_XW = S5_GPB * P_S5
S5_NSEG = SUBLANES
MOE_TILE = 256


def _cparams(sem):
    return pltpu.CompilerParams(dimension_semantics=sem, vmem_limit_bytes=VMEM_LIMIT)


def _silu(x):
    return x * (1.0 / (1.0 + jnp.exp(-x)))


def _sigmoid(x):
    return 1.0 / (1.0 + jnp.exp(-x))


def _log_sigmoid(x):
    return jnp.minimum(x, 0.0) - jnp.log1p(jnp.exp(-jnp.abs(x)))


def _gelu_tanh(x):
    c = math.sqrt(2.0 / math.pi)
    return 0.5 * x * (1.0 + jnp.tanh(c * (x + 0.044715 * (x * x * x))))


def _split3(x):
    hi = x.astype(BF16)
    r = x - hi.astype(F32)
    mid = r.astype(BF16)
    lo = (r - mid.astype(F32)).astype(BF16)
    return hi, mid, lo


def _dot(a, b):
    return jnp.dot(a, b, preferred_element_type=F32)


def _dot_nt(a, b):
    return lax.dot_general(a, b, (((1,), (1,)), ((), ())), preferred_element_type=F32)


def _dot_tn(a, b):
    return lax.dot_general(a, b, (((0,), (0,)), ((), ())), preferred_element_type=F32)


def _rmsnorm_kernel(x_ref, g_ref, *rest):
    o_ref = rest[-1]
    x = x_ref[...]
    r = lax.rsqrt(jnp.mean(x * x, axis=-1, keepdims=True) + EPS)
    o_ref[...] = (x * r * g_ref[...]).astype(o_ref.dtype)


def _rmsnorm_rows(x, g, t_all, row0, tm, alias=None):
    n = x.shape[0]
    blk0 = row0 // tm
    in_specs = [pl.BlockSpec((tm, D_MODEL), lambda i: (i, 0)),
                pl.BlockSpec((1, D_MODEL), lambda i: (0, 0))]
    args = [x, g.reshape(1, D_MODEL)]
    aliases = {}
    if alias is not None:
        in_specs.append(pl.BlockSpec(memory_space=pl.ANY))
        args.append(alias)
        aliases = {2: 0}
    return pl.pallas_call(
        _rmsnorm_kernel,
        out_shape=jax.ShapeDtypeStruct((t_all, D_MODEL), BF16),
        grid=(n // tm,),
        in_specs=in_specs,
        out_specs=pl.BlockSpec((tm, D_MODEL), lambda i: (i + blk0, 0)),
        input_output_aliases=aliases,
        compiler_params=_cparams(("arbitrary",)),
        name="rmsnorm_rows",
    )(*args)


def _mm_kernel(a_ref, w_ref, b_ref, o_ref, wb_ref):
    @pl.when(pl.program_id(1) == 0)
    def _():
        wb_ref[...] = w_ref[...].astype(BF16)

    o_ref[...] = (_dot(a_ref[...], wb_ref[...]) + b_ref[...]).astype(o_ref.dtype)


def _matmul(a, w, bias, n_out, tm, tn, out_dtype, col_blk0=0):
    m, k = a.shape
    return pl.pallas_call(
        _mm_kernel,
        out_shape=jax.ShapeDtypeStruct((m, n_out), out_dtype),
        grid=(n_out // tn, m // tm),
        in_specs=[pl.BlockSpec((tm, k), lambda j, i: (i, 0)),
                  pl.BlockSpec((k, tn), lambda j, i: (0, j + col_blk0)),
                  pl.BlockSpec((1, tn), lambda j, i: (0, j))],
        out_specs=pl.BlockSpec((tm, tn), lambda j, i: (i, j)),
        scratch_shapes=[pltpu.VMEM((k, tn), BF16)],
        compiler_params=_cparams(("arbitrary", "arbitrary")),
        name="rows_matmul",
    )(a, w, bias)


def _mm_shift_kernel(a_ref, wlo_ref, whi_ref, o_ref, wb_ref, *, shift):
    tn = wlo_ref.shape[1]
    rows = 256

    @pl.when(pl.program_id(1) == 0)
    def _():
        for r in range(wlo_ref.shape[0] // rows):
            sl = slice(r * rows, (r + 1) * rows)
            w = jnp.concatenate([wlo_ref[sl, :], whi_ref[sl, :]], axis=1)
            wb_ref[sl, :] = w[:, shift:shift + tn].astype(BF16)

    o_ref[...] = _dot(a_ref[...], wb_ref[...]).astype(o_ref.dtype)


def _matmul_shifted(a, w, col_starts, tm, tn, out_dtype):
    m, k = a.shape
    shift = col_starts[0] % tn
    assert all(c % tn == shift for c in col_starts) and 0 < shift < LANES
    lo_blk = [c // tn for c in col_starts]
    n_t = len(col_starts)
    base, first = min(lo_blk), lo_blk[0]
    assert all(lo_blk[j] == base + (j + first - base) % n_t for j in range(n_t))
    lo_idx = lambda j: base + (j + (first - base)) % n_t
    per = tn // LANES
    kern = functools.partial(_mm_shift_kernel, shift=shift)
    return pl.pallas_call(
        kern,
        out_shape=jax.ShapeDtypeStruct((m, n_t * tn), out_dtype),
        grid=(n_t, m // tm),
        in_specs=[pl.BlockSpec((tm, k), lambda j, i: (i, 0)),
                  pl.BlockSpec((k, tn), lambda j, i: (0, lo_idx(j))),
                  pl.BlockSpec((k, LANES), lambda j, i: (0, (lo_idx(j) + 1) * per))],
        out_specs=pl.BlockSpec((tm, tn), lambda j, i: (i, j)),
        scratch_shapes=[pltpu.VMEM((k, tn), BF16)],
        compiler_params=_cparams(("arbitrary", "arbitrary")),
        name="rows_matmul_shifted",
    )(a, w, w)


def _mlstm_kernel(q_ref, k_ref, v_ref, o_ref, gcol_ref, grow_ref, wq_ref, wk_ref, bq_ref, bk_ref,
                  hg_ref, h_ref, c_out, n_out, m_out, c_sc, n_sc, m_sc, qx_sc, kx_sc):
    h = pl.program_id(1)
    c = pl.program_id(2)
    L = CHUNK
    pad = SUBLANES

    @pl.when(c == 0)
    def _():
        c_sc[...] = jnp.zeros_like(c_sc)
        n_sc[...] = jnp.zeros_like(n_sc)
        m_sc[...] = jnp.zeros_like(m_sc)
        qx_sc[0:pad, :] = jnp.zeros((pad, DK), F32)
        kx_sc[0:pad, :] = jnp.zeros((pad, DK), F32)

    def conv_silu(x_ref, ext, w_ref, b_ref):
        x = x_ref[...]
        ext[pad:pad + L, :] = x
        y = b_ref[...] + w_ref[CONV_W - 1:CONV_W, :] * x
        for j in range(1, CONV_W):
            y = y + w_ref[CONV_W - 1 - j:CONV_W - j, :] * ext[pad - j:pad - j + L, :]
        ext[0:pad, :] = x[L - pad:L, :]
        return _silu(y)

    q = conv_silu(q_ref, qx_sc, wq_ref, bq_ref) * (DK ** -0.5)
    k = conv_silu(k_ref, kx_sc, wk_ref, bk_ref)
    v = v_ref[...]

    gcol = gcol_ref[...]
    lane = lax.broadcasted_iota(jnp.int32, gcol.shape, 1)
    ig_col = jnp.sum(jnp.where(lane == h, gcol, 0.0), axis=1, keepdims=True)
    fp_col = jnp.sum(jnp.where(lane == h + H_A, gcol, 0.0), axis=1, keepdims=True)
    grow = grow_ref[...]
    sub = lax.broadcasted_iota(jnp.int32, grow.shape, 0)
    ig_row = jnp.sum(jnp.where(sub == h, grow, 0.0), axis=0, keepdims=True)
    fp_row = jnp.sum(jnp.where(sub == h + H_A, grow, 0.0), axis=0, keepdims=True)
    lf_col = _log_sigmoid(fp_col)
    lf_row = _log_sigmoid(fp_row)

    ri = lax.broadcasted_iota(jnp.int32, (L, L), 0)
    ci = lax.broadcasted_iota(jnp.int32, (L, L), 1)
    causal = ci <= ri
    tril = jnp.where(causal, 1.0, 0.0).astype(BF16)
    triu = jnp.where(ri <= ci, 1.0, 0.0).astype(BF16)
    b_colm = sum(_dot(tril, jnp.broadcast_to(p, (L, L))) for p in _split3(lf_col))
    b_rowm = sum(_dot(jnp.broadcast_to(p, (SUBLANES, L)), triu) for p in _split3(lf_row))
    b_col = b_colm[:, 0:1]
    b_row = b_rowm[0:1, :]

    m_prev = m_sc[...]
    d_log = jnp.where(causal, b_colm - b_row + ig_row, -jnp.inf)
    inter_log = b_col + m_prev
    m_t = jnp.maximum(inter_log, jnp.max(d_log, axis=1, keepdims=True))
    qb = q.astype(BF16)
    kb = k.astype(BF16)
    vb = v.astype(BF16)
    s = _dot_nt(qb, kb) * jnp.exp(d_log - m_t)
    inter_w = jnp.exp(inter_log - m_t)
    c_prev = c_sc[...]
    n_prev = n_sc[...]
    num = _dot(s.astype(BF16), vb) + inter_w * _dot(qb, c_prev.astype(BF16))
    nq = jnp.sum(s, axis=1, keepdims=True) + inter_w * jnp.sum(q * n_prev, axis=1, keepdims=True)
    den = jnp.maximum(jnp.abs(nq), jnp.exp(-m_t))
    hh = num / den
    hh = hh * _sigmoid(o_ref[...])
    hh = hh * lax.rsqrt(jnp.mean(hh * hh, axis=1, keepdims=True) + EPS)
    h_ref[...] = (hh * hg_ref[...]).astype(h_ref.dtype)

    m_new = m_t[L - 1:L, :]
    b_last = b_col[L - 1:L, :]
    decay = jnp.exp(b_last + m_prev - m_new)
    w_end = jnp.exp(b_last - b_col + ig_col - m_new)
    kw = k * w_end
    c_new = decay * c_prev + _dot_tn(kw.astype(BF16), vb)
    n_new = decay * n_prev + jnp.sum(kw, axis=0, keepdims=True)
    c_sc[...] = c_new
    n_sc[...] = n_new
    m_sc[...] = m_new

    @pl.when(c == pl.num_programs(2) - 1)
    def _():
        c_out[...] = c_new
        n_out[...] = n_new
        m_out[...] = jnp.broadcast_to(m_new, m_out.shape)


def _mlstm_prompt(qkvo, gates, gates_t, w_conv, b_conv, head_g, batch, seq, t_all):
    nc = seq // CHUNK
    L = CHUNK
    row = lambda b, h, c: b * nc + c
    in_specs = [
        pl.BlockSpec((L, DK), lambda b, h, c: (row(b, h, c), h)),
        pl.BlockSpec((L, DK), lambda b, h, c: (row(b, h, c), H_A + h)),
        pl.BlockSpec((L, DK), lambda b, h, c: (row(b, h, c), 2 * H_A + h)),
        pl.BlockSpec((L, DK), lambda b, h, c: (row(b, h, c), 3 * H_A + h)),
        pl.BlockSpec((L, LANES), lambda b, h, c: (row(b, h, c), 0)),
        pl.BlockSpec((2 * H_A, L), lambda b, h, c: (0, row(b, h, c))),
        pl.BlockSpec((CONV_W, DK), lambda b, h, c: (0, h)),
        pl.BlockSpec((CONV_W, DK), lambda b, h, c: (0, H_A + h)),
        pl.BlockSpec((1, DK), lambda b, h, c: (0, h)),
        pl.BlockSpec((1, DK), lambda b, h, c: (0, H_A + h)),
        pl.BlockSpec((1, DK), lambda b, h, c: (0, h)),
    ]
    out_shape = (
        jax.ShapeDtypeStruct((t_all, H_A * DK), BF16),
        jax.ShapeDtypeStruct((batch, H_A, DK, DK), F32),
        jax.ShapeDtypeStruct((batch, H_A, 1, DK), F32),
        jax.ShapeDtypeStruct((batch, H_A, 1, LANES), F32),
    )
    out_specs = (
        pl.BlockSpec((L, DK), lambda b, h, c: (row(b, h, c), h)),
        pl.BlockSpec((None, None, DK, DK), lambda b, h, c: (b, h, 0, 0)),
        pl.BlockSpec((None, None, 1, DK), lambda b, h, c: (b, h, 0, 0)),
        pl.BlockSpec((None, None, 1, LANES), lambda b, h, c: (b, h, 0, 0)),
    )
    return pl.pallas_call(
        _mlstm_kernel,
        out_shape=out_shape,
        grid=(batch, H_A, nc),
        in_specs=in_specs,
        out_specs=out_specs,
        scratch_shapes=[pltpu.VMEM((DK, DK), F32), pltpu.VMEM((1, DK), F32), pltpu.VMEM((1, 1), F32),
                        pltpu.VMEM((SUBLANES + L, DK), F32), pltpu.VMEM((SUBLANES + L, DK), F32)],
        compiler_params=_cparams(("arbitrary", "arbitrary", "arbitrary")),
        name="mlstm_prompt",
    )(qkvo, qkvo, qkvo, qkvo, gates, gates_t, w_conv, w_conv, b_conv, b_conv, head_g)


def _s5_prompt_kernel(u_ref, bb_ref, cc_ref, lbr_ref, lbi_ref, d_ref, ys_ref, sre_ref, sim_ref, x_sc,
                      *, seq):
    seg = seq // S5_NSEG
    sstr = seg + SUBLANES
    nt = S5_XW // LANES
    rows = min(256, seg)
    srow = lambda t: (t // seg) * sstr + t % seg
    for r in range(seq // rows):
        ub = u_ref[r * rows:(r + 1) * rows, :].astype(BF16)
        bu = _dot(ub, bb_ref[...])
        r0 = srow(r * rows)
        for j in range(2 * nt):
            x_sc[j, r0:r0 + rows, :] = bu[:, j * LANES:(j + 1) * LANES]

    grp = 4
    for t0 in range(0, nt, grp):
        tiles = list(range(t0, t0 + grp))
        lbr = [jnp.broadcast_to(lbr_ref[:, j * LANES:(j + 1) * LANES], (S5_NSEG, LANES)) for j in tiles]
        lbi = [jnp.broadcast_to(lbi_ref[:, j * LANES:(j + 1) * LANES], (S5_NSEG, LANES)) for j in tiles]

        def scan_body(kk, carry, lbr=lbr, lbi=lbi, tiles=tiles):
            seg_rows = pl.ds(kk, S5_NSEG, stride=sstr)
            out = []
            for i, j in enumerate(tiles):
                xr, xi = carry[i]
                nr = lbr[i] * xr - lbi[i] * xi + x_sc[j, seg_rows, :]
                ni = lbr[i] * xi + lbi[i] * xr + x_sc[nt + j, seg_rows, :]
                x_sc[j, seg_rows, :] = nr
                x_sc[nt + j, seg_rows, :] = ni
                out.append((nr, ni))
            return tuple(out)

        z = jnp.zeros((S5_NSEG, LANES), F32)
        ends = lax.fori_loop(0, seg, scan_body, tuple((z, z) for _ in tiles))

        starts = []
        for i, j in enumerate(tiles):
            er, ei = ends[i]
            pr, pi = lbr[i][0:1, :], lbi[i][0:1, :]
            for _ in range(int(math.log2(seg))):
                pr, pi = pr * pr - pi * pi, 2.0 * pr * pi
            sr_rows = [jnp.zeros((1, LANES), F32)]
            si_rows = [jnp.zeros((1, LANES), F32)]
            for s in range(S5_NSEG):
                sr_rows.append(er[s:s + 1, :] + pr * sr_rows[s] - pi * si_rows[s])
                si_rows.append(ei[s:s + 1, :] + pr * si_rows[s] + pi * sr_rows[s])
            sre_ref[:, j * LANES:(j + 1) * LANES] = sr_rows[S5_NSEG]
            sim_ref[:, j * LANES:(j + 1) * LANES] = si_rows[S5_NSEG]
            starts.append((jnp.concatenate(sr_rows[:S5_NSEG], axis=0),
                           jnp.concatenate(si_rows[:S5_NSEG], axis=0)))

        def fix_body(kk, pw, lbr=lbr, lbi=lbi, starts=starts, tiles=tiles):
            seg_rows = pl.ds(kk, S5_NSEG, stride=sstr)
            out = []
            for i, j in enumerate(tiles):
                wr, wi = pw[i]
                s_r, s_i = starts[i]
                x_sc[j, seg_rows, :] = x_sc[j, seg_rows, :] + wr * s_r - wi * s_i
                x_sc[nt + j, seg_rows, :] = x_sc[nt + j, seg_rows, :] + wr * s_i + wi * s_r
                out.append((wr * lbr[i] - wi * lbi[i], wr * lbi[i] + wi * lbr[i]))
            return tuple(out)

        lax.fori_loop(0, seg, fix_body, tuple((lbr[i], lbi[i]) for i in range(grp)))

    for r in range(seq // rows):
        r0 = srow(r * rows)
        xb = jnp.concatenate([x_sc[j, r0:r0 + rows, :] for j in range(2 * nt)], axis=1)
        y = _dot(xb.astype(BF16), cc_ref[...]) + d_ref[...] * u_ref[r * rows:(r + 1) * rows, :]
        ys_ref[r * rows:(r + 1) * rows, :] = _gelu_tanh(y).astype(ys_ref.dtype)


def _s5_prompt(proj2, u_col0, bb, cc, lbr, lbi, d_row, batch, seq, t_all):
    kern = functools.partial(_s5_prompt_kernel, seq=seq)
    ub0 = u_col0 // S5_UW
    return pl.pallas_call(
        kern,
        out_shape=(jax.ShapeDtypeStruct((t_all, D_B), F32),
                   jax.ShapeDtypeStruct((batch, 1, G_B * P_S5), F32),
                   jax.ShapeDtypeStruct((batch, 1, G_B * P_S5), F32)),
        grid=(S5_GB, batch),
        in_specs=[pl.BlockSpec((seq, S5_UW), lambda g, b: (b, ub0 + g)),
                  pl.BlockSpec((None, S5_UW, 2 * S5_XW), lambda g, b: (g, 0, 0)),
                  pl.BlockSpec((None, 2 * S5_XW, S5_UW), lambda g, b: (g, 0, 0)),
                  pl.BlockSpec((None, 1, S5_XW), lambda g, b: (g, 0, 0)),
                  pl.BlockSpec((None, 1, S5_XW), lambda g, b: (g, 0, 0)),
                  pl.BlockSpec((1, S5_UW), lambda g, b: (0, g))],
        out_specs=(pl.BlockSpec((seq, S5_UW), lambda g, b: (b, g)),
                   pl.BlockSpec((None, 1, S5_XW), lambda g, b: (b, 0, g)),
                   pl.BlockSpec((None, 1, S5_XW), lambda g, b: (b, 0, g))),
        scratch_shapes=[pltpu.VMEM((2 * S5_XW // LANES, seq + S5_NSEG * SUBLANES, LANES), F32)],
        compiler_params=_cparams(("arbitrary", "arbitrary")),
        name="s5_prompt",
    )(proj2, bb, cc, lbr, lbi, d_row)


def _sample_pre_kernel(qk_ref, conv_ref, wc_ref, bc_ref, g_ref, m_ref, n_ref,
                       q_out, kw_out, a_out, s_out, den_out, conv_out, n_out, m_out):
    c2 = 2 * H_A * DK
    x_new = qk_ref[...]
    y = bc_ref[...] + wc_ref[CONV_W - 1:CONV_W, :] * x_new
    for j in range(CONV_W - 1):
        y = y + wc_ref[j:j + 1, :] * conv_ref[:, j * c2:(j + 1) * c2]
    y = _silu(y)
    conv_out[:, 0:(CONV_W - 2) * c2] = conv_ref[:, c2:(CONV_W - 1) * c2]
    conv_out[:, (CONV_W - 2) * c2:(CONV_W - 1) * c2] = x_new
    g = g_ref[...]
    bd = x_new.shape[0]
    m_cols = []
    for h in range(H_A):
        sl = slice(h * DK, (h + 1) * DK)
        q = y[:, sl] * (DK ** -0.5)
        k = y[:, H_A * DK + h * DK:H_A * DK + (h + 1) * DK]
        ig = g[:, h:h + 1]
        lf = _log_sigmoid(g[:, H_A + h:H_A + h + 1])
        m_prev = m_ref[:, h:h + 1]
        m_t = jnp.maximum(lf + m_prev, ig)
        a = jnp.exp(lf + m_prev - m_t)
        wgt = jnp.exp(ig - m_t)
        n_prev = n_ref[:, sl]
        s = jnp.sum(q * k, axis=1, keepdims=True) * wgt
        nq = s + a * jnp.sum(q * n_prev, axis=1, keepdims=True)
        den = jnp.maximum(jnp.abs(nq), jnp.exp(-m_t))
        kw = wgt * k
        q_out[:, sl] = q
        kw_out[:, sl] = kw
        a_out[:, sl] = jnp.broadcast_to(a, (bd, DK))
        s_out[:, sl] = jnp.broadcast_to(s, (bd, DK))
        den_out[:, sl] = jnp.broadcast_to(den, (bd, DK))
        n_out[:, sl] = a * n_prev + kw
        m_cols.append(m_t)
    lane = lax.broadcasted_iota(jnp.int32, (bd, LANES), 1)
    m_full = jnp.zeros((bd, LANES), F32)
    for h in range(H_A):
        m_full = jnp.where(lane == h, m_cols[h], m_full)
    m_out[...] = m_full


def _sample_pre(qkvo, conv_state, w_conv, b_conv, gates, m_state, n_state, dec, row_blk):
    c2 = 2 * H_A * DK
    d = H_A * DK
    full = lambda shape: pl.BlockSpec(shape, lambda i: (0,) * len(shape))
    rows = lambda: jax.ShapeDtypeStruct((dec, d), F32)
    return pl.pallas_call(
        _sample_pre_kernel,
        out_shape=(rows(), rows(), rows(), rows(), rows(),
                   jax.ShapeDtypeStruct((dec, (CONV_W - 1) * c2), F32), rows(),
                   jax.ShapeDtypeStruct((dec, LANES), F32)),
        grid=(1,),
        in_specs=[pl.BlockSpec((dec, c2), lambda i: (row_blk, 0)),
                  full((dec, (CONV_W - 1) * c2)), full((CONV_W, c2)), full((1, c2)),
                  pl.BlockSpec((dec, LANES), lambda i: (row_blk, 0)),
                  full((dec, H_A)), full((dec, d))],
        out_specs=(full((dec, d)), full((dec, d)), full((dec, d)), full((dec, d)), full((dec, d)),
                   full((dec, (CONV_W - 1) * c2)), full((dec, d)), full((dec, LANES))),
        compiler_params=_cparams(("arbitrary",)),
        name="sample_pre",
    )(qkvo, conv_state, w_conv, b_conv, gates, m_state, n_state)


def _sample_c_kernel(q_ref, kw_ref, v_ref, a_ref, c_ref, c_out, qc_out):
    eye = jnp.where(lax.broadcasted_iota(jnp.int32, (DK, DK), 0)
                    == lax.broadcasted_iota(jnp.int32, (DK, DK), 1), 1.0, 0.0).astype(BF16)
    q_t = sum(_dot_nt(eye, p) for p in _split3(q_ref[...]))
    kw_t = sum(_dot_nt(eye, p) for p in _split3(kw_ref[...]))
    qc_rows = []
    for h in range(H_A):
        c_prev = c_ref[h]
        qc_rows.append(jnp.sum(q_t[:, h:h + 1] * c_prev, axis=0, keepdims=True))
        c_out[h] = a_ref[h:h + 1, :] * c_prev + kw_t[:, h:h + 1] * v_ref[h:h + 1, :]
    qc_out[...] = jnp.concatenate(qc_rows, axis=0)


def _sample_c(q3, kw3, v3, a3, c_state, dec):
    vec = pl.BlockSpec((None, H_A, DK), lambda b: (b, 0, 0))
    mat = pl.BlockSpec((None, H_A, DK, DK), lambda b: (b, 0, 0, 0))
    return pl.pallas_call(
        _sample_c_kernel,
        out_shape=(jax.ShapeDtypeStruct((dec, H_A, DK, DK), F32),
                   jax.ShapeDtypeStruct((dec, H_A, DK), F32)),
        grid=(dec,),
        in_specs=[vec, vec, vec, vec, mat],
        out_specs=(mat, vec),
        compiler_params=_cparams(("arbitrary",)),
        name="sample_c_update",
    )(q3, kw3, v3, a3, c_state)


def _sample_post_kernel(qc_ref, s_ref, a_ref, den_ref, vo_ref, hg_ref, hn_in, h_ref):
    del hn_in
    d = H_A * DK
    num = s_ref[...] * vo_ref[:, 0:d] + a_ref[...] * qc_ref[...]
    hh = num / den_ref[...]
    hh = hh * _sigmoid(vo_ref[:, d:2 * d])
    for h in range(H_A):
        sl = slice(h * DK, (h + 1) * DK)
        seg = hh[:, sl]
        seg = seg * lax.rsqrt(jnp.mean(seg * seg, axis=1, keepdims=True) + EPS)
        h_ref[:, sl] = (seg * hg_ref[:, sl]).astype(h_ref.dtype)


def _sample_post(qc, s_e, a_e, den_e, qkvo, head_g, hn_all, dec, row_blk):
    d = H_A * DK
    full = lambda shape: pl.BlockSpec(shape, lambda i: (0,) * len(shape))
    return pl.pallas_call(
        _sample_post_kernel,
        out_shape=jax.ShapeDtypeStruct(hn_all.shape, hn_all.dtype),
        grid=(1,),
        in_specs=[full((dec, d)), full((dec, d)), full((dec, d)), full((dec, d)),
                  pl.BlockSpec((dec, 2 * d), lambda i: (row_blk, 1)),
                  full((1, d)), pl.BlockSpec(memory_space=pl.ANY)],
        out_specs=pl.BlockSpec((dec, d), lambda i: (row_blk, 0)),
        input_output_aliases={6: 0},
        compiler_params=_cparams(("arbitrary",)),
        name="sample_post",
    )(qc, s_e, a_e, den_e, qkvo, head_g, hn_all)


def _s5_sample_kernel(u_ref, sr_ref, si_ref, bb_ref, cc_ref, lbr_ref, lbi_ref, d_ref, ys_in,
                      ys_ref, sre_out, sim_out):
    del ys_in
    for g in range(S5_GB):
        u = u_ref[:, g * S5_UW:(g + 1) * S5_UW]
        bu = _dot(u.astype(BF16), bb_ref[g])
        sl = slice(g * S5_XW, (g + 1) * S5_XW)
        lbr = lbr_ref[g]
        lbi = lbi_ref[g]
        sr = sr_ref[:, sl]
        si = si_ref[:, sl]
        xr = lbr * sr - lbi * si + bu[:, 0:S5_XW]
        xi = lbr * si + lbi * sr + bu[:, S5_XW:2 * S5_XW]
        sre_out[:, sl] = xr
        sim_out[:, sl] = xi
        x = jnp.concatenate([xr, xi], axis=1).astype(BF16)
        y = _dot(x, cc_ref[g]) + d_ref[:, g * S5_UW:(g + 1) * S5_UW] * u
        ys_ref[:, g * S5_UW:(g + 1) * S5_UW] = _gelu_tanh(y).astype(ys_ref.dtype)


def _s5_sample(proj2, u_col0, s_re, s_im, bb, cc, lbr, lbi, d_row, ys_all, dec, row_blk):
    full = lambda shape: pl.BlockSpec(shape, lambda i: (0,) * len(shape))
    n_state = G_B * P_S5
    ub0 = u_col0 // D_B
    return pl.pallas_call(
        _s5_sample_kernel,
        out_shape=(jax.ShapeDtypeStruct(ys_all.shape, ys_all.dtype),
                   jax.ShapeDtypeStruct((dec, n_state), F32),
                   jax.ShapeDtypeStruct((dec, n_state), F32)),
        grid=(1,),
        in_specs=[pl.BlockSpec((dec, D_B), lambda i: (row_blk, ub0)),
                  full((dec, n_state)), full((dec, n_state)),
                  full(bb.shape), full(cc.shape), full(lbr.shape), full(lbi.shape), full((1, D_B)),
                  pl.BlockSpec(memory_space=pl.ANY)],
        out_specs=(pl.BlockSpec((dec, D_B), lambda i: (row_blk, 0)),
                   full((dec, n_state)), full((dec, n_state))),
        input_output_aliases={8: 0},
        compiler_params=_cparams(("arbitrary",)),
        name="s5_sample",
    )(proj2, s_re, s_im, bb, cc, lbr, lbi, d_row, ys_all)


def _merge_kernel(x_ref, hn_ref, ys_ref, ga_ref, gb_ref, bglu_ref, gffn_ref, wrh_ref, wrl_ref, br_ref,
                  wpa_hbm, wglu_hbm, wpb_hbm, wout_hbm, *rest):
    n_alias = len(rest) - 9
    x1_ref, xn_ref, ids_ref, wts_ref = rest[n_alias:n_alias + 4]
    wpa, wglu, wpb, wout, sem = rest[n_alias + 4:]

    @pl.when(pl.program_id(0) == 0)
    def _():
        copies = [pltpu.make_async_copy(src, dst, sem.at[i])
                  for i, (src, dst) in enumerate(((wpa_hbm, wpa), (wglu_hbm, wglu),
                                                  (wpb_hbm, wpb), (wout_hbm, wout)))]
        for cp in copies:
            cp.start()
        for cp in copies:
            cp.wait()

    ya = _dot(hn_ref[...], wpa[...])
    ys = ys_ref[...]
    gate = _sigmoid(_dot(ys.astype(BF16), wglu[...]) + bglu_ref[...])
    yb = _dot((ys * gate).astype(BF16), wpb[...])
    z = _sigmoid(ga_ref[...]) * ya + _sigmoid(gb_ref[...]) * yb
    x1 = x_ref[...] + _dot(z.astype(BF16), wout[...])
    x1_ref[...] = x1
    xn = x1 * lax.rsqrt(jnp.mean(x1 * x1, axis=1, keepdims=True) + EPS) * gffn_ref[...]
    xn_ref[...] = xn
    xh = xn.astype(BF16)
    xl = (xn - xh.astype(F32)).astype(BF16)
    logits = _dot(xh, wrh_ref[...]) + _dot(xl, wrh_ref[...]) + _dot(xh, wrl_ref[...]) + br_ref[...]

    lane_i = lax.broadcasted_iota(jnp.int32, logits.shape, 1)
    lane = lane_i.astype(F32)
    neg = -jnp.inf
    big = float(1 << 20)
    gl = jnp.where(lane_i < N_GROUPS, logits, neg)
    gmax = jnp.max(gl, axis=1, keepdims=True)
    gsum = jnp.sum(jnp.exp(gl - gmax), axis=1, keepdims=True)
    gidx = jnp.min(jnp.where(gl == gmax, lane, big), axis=1, keepdims=True)
    pg_sel = 1.0 / gsum
    lo = N_GROUPS + gidx * EXP_PER_GROUP
    in_grp = (lane >= lo) & (lane < lo + EXP_PER_GROUP)
    el = jnp.where(in_grp, logits, neg)
    emax = jnp.max(el, axis=1, keepdims=True)
    ee = jnp.exp(el - emax)
    pe = ee / jnp.sum(ee, axis=1, keepdims=True)
    v0 = jnp.max(pe, axis=1, keepdims=True)
    i0 = jnp.min(jnp.where(in_grp & (pe == v0), lane, big), axis=1, keepdims=True)
    rest_m = in_grp & (lane != i0)
    pe1 = jnp.where(rest_m, pe, neg)
    v1 = jnp.max(pe1, axis=1, keepdims=True)
    i1 = jnp.min(jnp.where(rest_m & (pe1 == v1), lane, big), axis=1, keepdims=True)
    tot = v0 + v1
    w0 = pg_sel * (v0 / tot)
    w1 = pg_sel * (v1 / tot)
    ids = jnp.where(lane_i == 0, i0 - N_GROUPS, jnp.where(lane_i == 1, i1 - N_GROUPS, 0.0))
    ids_ref[...] = ids.astype(jnp.int32)
    wts_ref[...] = jnp.where(lane_i == 0, w0, jnp.where(lane_i == 1, w1, 0.0))


def _merge(x, hn_all, ys_all, proj2, b_glu, g_ffn, wr_hi, wr_lo, b_r, wpa, wglu, wpb, wout,
           t_all, row0, tm, aliases=None):
    n = x.shape[0]
    blk0 = row0 // tm
    const = lambda shape: pl.BlockSpec(shape, lambda i: (0,) * len(shape))
    any_spec = pl.BlockSpec(memory_space=pl.ANY)
    in_specs = [pl.BlockSpec((tm, D_MODEL), lambda i: (i, 0)),
                pl.BlockSpec((tm, D_MODEL), lambda i: (i + blk0, 0)),
                pl.BlockSpec((tm, D_B), lambda i: (i + blk0, 0)),
                pl.BlockSpec((tm, D_MODEL), lambda i: (i + blk0, 0)),
                pl.BlockSpec((tm, D_MODEL), lambda i: (i + blk0, 1)),
                const((1, D_B)), const((1, D_MODEL)),
                const((D_MODEL, LANES)), const((D_MODEL, LANES)), const((1, LANES)),
                any_spec, any_spec, any_spec, any_spec]
    args = [x, hn_all, ys_all, proj2, proj2, b_glu, g_ffn, wr_hi, wr_lo, b_r, wpa, wglu, wpb, wout]
    io_alias = {}
    if aliases is not None:
        for j, a in enumerate(aliases):
            in_specs.append(any_spec)
            args.append(a)
            io_alias[14 + j] = j
    out_shape = (jax.ShapeDtypeStruct((t_all, D_MODEL), F32),
                 jax.ShapeDtypeStruct((t_all, D_MODEL), F32),
                 jax.ShapeDtypeStruct((t_all, LANES), jnp.int32),
                 jax.ShapeDtypeStruct((t_all, LANES), F32))
    out_specs = (pl.BlockSpec((tm, D_MODEL), lambda i: (i + blk0, 0)),
                 pl.BlockSpec((tm, D_MODEL), lambda i: (i + blk0, 0)),
                 pl.BlockSpec((tm, LANES), lambda i: (i + blk0, 0)),
                 pl.BlockSpec((tm, LANES), lambda i: (i + blk0, 0)))
    return pl.pallas_call(
        _merge_kernel,
        out_shape=out_shape,
        grid=(n // tm,),
        in_specs=in_specs,
        out_specs=out_specs,
        scratch_shapes=[pltpu.VMEM(wpa.shape, BF16), pltpu.VMEM(wglu.shape, BF16),
                        pltpu.VMEM(wpb.shape, BF16), pltpu.VMEM(wout.shape, BF16),
                        pltpu.SemaphoreType.DMA((4,))],
        input_output_aliases=io_alias,
        compiler_params=_cparams(("arbitrary",)),
        name="merge_router",
    )(*args)


def _moe_kernel(te_ref, nt_ref, xs_ref, wg_ref, wu_ref, wd_ref, o_ref, wg_sc, wu_sc, wd_sc):
    i = pl.program_id(0)
    e = te_ref[i]
    prev = te_ref[jnp.maximum(i - 1, 0)]

    @pl.when((i == 0) | (e != prev))
    def _():
        wg_sc[...] = wg_ref[...].astype(BF16)
        wu_sc[...] = wu_ref[...].astype(BF16)
        wd_sc[...] = wd_ref[...].astype(BF16)

    @pl.when(i < nt_ref[0])
    def _():
        x = xs_ref[...].astype(BF16)
        hg = _dot(x, wg_sc[...])
        hu = _dot(x, wu_sc[...])
        hh = (_silu(hg) * hu).astype(BF16)
        o_ref[...] = _dot(hh, wd_sc[...])


def _moe_experts(tile_expert, n_tiles, xs, w_gate, w_up, w_down):
    p_rows = xs.shape[0]
    last = lambda i, te, nt: jnp.minimum(i, nt[0] - 1)
    grid_spec = pltpu.PrefetchScalarGridSpec(
        num_scalar_prefetch=2,
        grid=(p_rows // MOE_TILE,),
        in_specs=[pl.BlockSpec((MOE_TILE, D_MODEL), lambda i, te, nt: (last(i, te, nt), 0)),
                  pl.BlockSpec((None, D_MODEL, D_EXPERT), lambda i, te, nt: (te[i], 0, 0)),
                  pl.BlockSpec((None, D_MODEL, D_EXPERT), lambda i, te, nt: (te[i], 0, 0)),
                  pl.BlockSpec((None, D_EXPERT, D_MODEL), lambda i, te, nt: (te[i], 0, 0))],
        out_specs=pl.BlockSpec((MOE_TILE, D_MODEL), lambda i, te, nt: (last(i, te, nt), 0)),
        scratch_shapes=[pltpu.VMEM((D_MODEL, D_EXPERT), BF16), pltpu.VMEM((D_MODEL, D_EXPERT), BF16),
                        pltpu.VMEM((D_EXPERT, D_MODEL), BF16)],
    )
    return pl.pallas_call(
        _moe_kernel,
        out_shape=jax.ShapeDtypeStruct((p_rows, D_MODEL), F32),
        grid_spec=grid_spec,
        compiler_params=_cparams(("arbitrary",)),
        name="moe_experts",
    )(tile_expert, n_tiles, xs, w_gate, w_up, w_down)


def _final_kernel(x1_ref, y0_ref, y1_ref, w_ref, g_ref, o_ref):
    w = w_ref[...]
    x2 = x1_ref[...] + w[:, 0:1] * y0_ref[...] + w[:, 1:2] * y1_ref[...]
    o_ref[...] = x2 * lax.rsqrt(jnp.mean(x2 * x2, axis=1, keepdims=True) + EPS) * g_ref[...]


def _final(x1_all, yg0, yg1, wts, g_final, row0, n, tm):
    blk0 = row0 // tm
    rows = pl.BlockSpec((tm, D_MODEL), lambda i: (i + blk0, 0))
    return pl.pallas_call(
        _final_kernel,
        out_shape=jax.ShapeDtypeStruct((n, D_MODEL), F32),
        grid=(n // tm,),
        in_specs=[rows, rows, rows,
                  pl.BlockSpec((tm, LANES), lambda i: (i + blk0, 0)),
                  pl.BlockSpec((1, D_MODEL), lambda i: (0, 0))],
        out_specs=pl.BlockSpec((tm, D_MODEL), lambda i: (i, 0)),
        compiler_params=_cparams(("arbitrary",)),
        name="combine_final_norm",
    )(x1_all, yg0, yg1, wts, g_final)


def _s5_params(a_re, a_im, log_step, b_re, b_im, c_re, c_im):
    dt = jnp.exp(log_step)[:, None]
    mag = jnp.exp(a_re * dt)
    lb_re = mag * jnp.cos(a_im * dt)
    lb_im = mag * jnp.sin(a_im * dt)
    den = a_re * a_re + a_im * a_im
    nr = lb_re - 1.0
    coef_re = (nr * a_re + lb_im * a_im) / den
    coef_im = (lb_im * a_re - nr * a_im) / den
    bb_re = coef_re[..., None] * b_re - coef_im[..., None] * b_im
    bb_im = coef_re[..., None] * b_im + coef_im[..., None] * b_re
    eye = jnp.eye(S5_GPB, dtype=F32)

    def blockdiag_in(m):
        m = m.reshape(S5_GB, S5_GPB, P_S5, S5_GROUP)
        return jnp.einsum('bgpc,gh->bgchp', m, eye).reshape(S5_GB, S5_UW, S5_XW)

    def blockdiag_out(m):
        m = m.reshape(S5_GB, S5_GPB, S5_GROUP, P_S5)
        return jnp.einsum('bgcp,gh->bgphc', m, eye).reshape(S5_GB, S5_XW, S5_UW)

    bb = jnp.concatenate([blockdiag_in(bb_re), blockdiag_in(bb_im)], axis=2).astype(BF16)
    cc = jnp.concatenate([blockdiag_out(c_re), -blockdiag_out(c_im)], axis=1).astype(BF16)
    lbr = lb_re.reshape(S5_GB, 1, S5_XW)
    lbi = lb_im.reshape(S5_GB, 1, S5_XW)
    return bb, cc, lbr, lbi


def _dispatch(ids, t_all, p_rows):
    e = ids[:, :2].reshape(-1)
    onehot = (e[:, None] == jnp.arange(N_EXPERTS, dtype=jnp.int32)[None, :]).astype(jnp.int32)
    csum = jnp.cumsum(onehot, axis=0)
    rank = jnp.sum((csum - onehot) * onehot, axis=1)
    counts = csum[-1]
    tiles = (counts + MOE_TILE - 1) // MOE_TILE
    tile_end = jnp.cumsum(tiles)
    tile_start = tile_end - tiles
    pos = jnp.sum(onehot * (tile_start * MOE_TILE)[None, :], axis=1) + rank
    tok = jnp.arange(2 * t_all, dtype=jnp.int32) // 2
    src = jnp.zeros((p_rows,), jnp.int32).at[pos].set(tok)
    n_tiles = tile_end[-1]
    tidx = jnp.arange(p_rows // MOE_TILE, dtype=jnp.int32)
    tclamp = jnp.minimum(tidx, n_tiles - 1)
    tile_expert = jnp.sum((tile_end[None, :] <= tclamp[:, None]).astype(jnp.int32), axis=1)
    return pos.reshape(t_all, 2), src, tile_expert, n_tiles.reshape(1).astype(jnp.int32)


def _pick_tile(n, candidates):
    for c in candidates:
        if n % c == 0:
            return c
    raise ValueError(f"no row tile for {n}")


def kernel(x_prompt, x_sample, state_mlstm_C, state_mlstm_n, state_mlstm_m, state_conv, state_s5_re,
           state_s5_im, norm_mix_g, w_in, b_i, b_f, w_conv, b_conv, head_norm_g, w_pa, s5_a_re, s5_a_im,
           s5_log_step, s5_b_re, s5_b_im, s5_c_re, s5_c_im, s5_d, s5_w_glu, s5_b_glu, w_pb, w_out,
           norm_ffn_g, w_rg, b_rg, w_rexp, b_rexp, w_gate, w_up, w_down, norm_final_g):
    assert state_mlstm_C.shape[0] == 1 and x_sample.shape[1] == 1
    batch, seq, _ = x_prompt.shape
    dec = x_sample.shape[0]
    t_p = batch * seq
    t_all = t_p + dec
    assert seq % CHUNK == 0 and t_p % dec == 0 and dec % LANES == 0
    d_a = H_A * DK
    tm_p = _pick_tile(t_p, (512, 256, 128))
    tm_all = _pick_tile(t_all, (640, 384, 128))
    sample_blk = t_p // dec

    xp = x_prompt.reshape(t_p, D_MODEL)
    xs = x_sample.reshape(dec, D_MODEL)

    g_mix = norm_mix_g[0]
    xn_all = _rmsnorm_rows(xp, g_mix, t_all, 0, tm_p)
    xn_all = _rmsnorm_rows(xs, g_mix, t_all, t_p, dec, alias=xn_all)
    w_in0 = w_in.reshape(w_in.shape[1:])
    n_qkvo = 4 * d_a
    tn = 1024
    qkvo = _matmul(xn_all, w_in0, jnp.zeros((1, n_qkvo), F32), n_qkvo, tm_all, tn, F32)
    n_gate_cols = 2 * H_A
    b_gates = jnp.pad(jnp.concatenate([b_i[0], b_f[0]]), (0, LANES - n_gate_cols)).reshape(1, LANES)
    gates = _matmul(xn_all, w_in0, b_gates, LANES, tm_all, LANES, F32, col_blk0=n_qkvo // LANES)
    c_u = n_qkvo + n_gate_cols
    c_ga = c_u + D_B
    starts = [c_ga + j * tn for j in range(2 * D_MODEL // tn)] + [c_u]
    proj2 = _matmul_shifted(xn_all, w_in0, starts, tm_all, tn, F32)
    u_col0 = 2 * D_MODEL

    gates_t = gates[:t_p, :n_gate_cols].T
    head_g = head_norm_g[0].reshape(1, d_a)
    hn_all, c_p, n_p, m_p = _mlstm_prompt(qkvo, gates, gates_t, w_conv[0], b_conv[0].reshape(1, -1),
                                          head_g, batch, seq, t_all)
    conv_p = jnp.stack([qkvo[b * seq + seq - (CONV_W - 1):(b + 1) * seq, :2 * d_a] for b in range(batch)])

    bb, cc, lbr, lbi = _s5_params(s5_a_re[0], s5_a_im[0], s5_log_step[0], s5_b_re[0], s5_b_im[0],
                                  s5_c_re[0], s5_c_im[0])
    d_row = s5_d[0].reshape(1, D_B)
    ys_all, s5re_p, s5im_p = _s5_prompt(proj2, u_col0, bb, cc, lbr, lbi, d_row, batch, seq, t_all)

    conv_s_in = state_conv[0].reshape(dec, (CONV_W - 1) * 2 * d_a)
    q_s, kw_s, a_e, s_e, den_e, conv_s, n_s, m_s = _sample_pre(
        qkvo, conv_s_in, w_conv[0], b_conv[0].reshape(1, -1), gates, state_mlstm_m[0],
        state_mlstm_n[0].reshape(dec, d_a), dec, sample_blk)
    v_s = qkvo[t_p:, 2 * d_a:3 * d_a]
    r3 = lambda a: a.reshape(dec, H_A, DK)
    c_s, qc = _sample_c(r3(q_s), r3(kw_s), r3(v_s), r3(a_e), state_mlstm_C[0], dec)
    hn_all = _sample_post(qc.reshape(dec, d_a), s_e, a_e, den_e, qkvo, head_g, hn_all, dec, sample_blk)

    ys_all, s5re_s, s5im_s = _s5_sample(proj2, u_col0, state_s5_re[0].reshape(dec, -1),
                                        state_s5_im[0].reshape(dec, -1),
                                        bb, cc, lbr, lbi, d_row, ys_all, dec, sample_blk)

    wr = jnp.pad(jnp.concatenate([w_rg[0], w_rexp[0]], axis=1), ((0, 0), (0, LANES - N_GROUPS - N_EXPERTS)))
    wr_hi = wr.astype(BF16)
    wr_lo = (wr - wr_hi.astype(F32)).astype(BF16)
    b_r = jnp.pad(jnp.concatenate([b_rg[0], b_rexp[0]]), (0, LANES - N_GROUPS - N_EXPERTS)).reshape(1, LANES)
    merge_w = (w_pa[0].astype(BF16), s5_w_glu[0].astype(BF16), w_pb[0].astype(BF16), w_out[0].astype(BF16))
    b_glu = s5_b_glu[0].reshape(1, D_B)
    g_ffn = norm_ffn_g[0].reshape(1, D_MODEL)
    tm_m = _pick_tile(t_p, (256, 128))
    outs = _merge(xp, hn_all, ys_all, proj2, b_glu, g_ffn, wr_hi, wr_lo, b_r, *merge_w, t_all, 0, tm_m)
    x1_all, xn2_all, ids, wts = _merge(xs, hn_all, ys_all, proj2, b_glu, g_ffn, wr_hi, wr_lo, b_r,
                                       *merge_w, t_all, t_p, dec, aliases=outs)

    p_rows = -(-(2 * t_all + N_EXPERTS * (MOE_TILE - 1)) // MOE_TILE) * MOE_TILE
    pos, src, tile_expert, n_tiles = _dispatch(ids, t_all, p_rows)
    take_rows = lambda a, idx: a.at[idx].get(mode='promise_in_bounds')
    xs_sorted = take_rows(xn2_all, src)
    yp = _moe_experts(tile_expert, n_tiles, xs_sorted, w_gate[0], w_up[0], w_down[0])
    yg0 = take_rows(yp, pos[:, 0])
    yg1 = take_rows(yp, pos[:, 1])

    g_fin = norm_final_g.reshape(1, D_MODEL)
    y_prompt = _final(x1_all, yg0, yg1, wts, g_fin, 0, t_p, tm_p).reshape(batch, seq, D_MODEL)
    y_sample = _final(x1_all, yg0, yg1, wts, g_fin, t_p, dec, dec).reshape(dec, 1, D_MODEL)

    lead = lambda a, shape: a.reshape((1,) + shape)
    return (y_prompt, y_sample,
            lead(c_p, (batch, H_A, DK, DK)), lead(n_p, (batch, H_A, DK)), lead(m_p[..., 0, 0], (batch, H_A)),
            lead(conv_p, (batch, CONV_W - 1, 2 * d_a)),
            lead(s5re_p, (batch, G_B, P_S5)), lead(s5im_p, (batch, G_B, P_S5)),
            lead(c_s, (dec, H_A, DK, DK)), lead(n_s, (dec, H_A, DK)), lead(m_s[:, :H_A], (dec, H_A)),
            lead(conv_s, (dec, CONV_W - 1, 2 * d_a)),
            lead(s5re_s, (dec, G_B, P_S5)), lead(s5im_s, (dec, G_B, P_S5)))
```

```python
import functools
import math

import jax
import jax.numpy as jnp
from jax import lax
from jax.experimental import pallas as pl
from jax.experimental.pallas import tpu as pltpu

F32 = jnp.float32
BF16 = jnp.bfloat16

D_MODEL = 2048
H_A = 8
DK = 256
CONV_W = 4
CHUNK = 128
D_B = 1024
S5_GROUP = 16
G_B = 64
P_S5 = 64
N_GROUPS = 4
EXP_PER_GROUP = 8
N_EXPERTS = 32
D_EXPERT = 512
EPS = 1e-6

LANES = 128
SUBLANES = 8
VMEM_LIMIT = 56 * 1024 * 1024

S5_GB = 4
S5_GPB = G_B // S5_GB
S5_UW = S5_GPB * S5_GROUP
S5_XW = S5_GPB * P_S5
S5_NSEG = SUBLANES
MOE_TILE = 256


def _cparams(sem):
    return pltpu.CompilerParams(dimension_semantics=sem, vmem_limit_bytes=VMEM_LIMIT)


def _silu(x):
    return x * (1.0 / (1.0 + jnp.exp(-x)))


def _sigmoid(x):
    return 1.0 / (1.0 + jnp.exp(-x))


def _log_sigmoid(x):
    return jnp.minimum(x, 0.0) - jnp.log1p(jnp.exp(-jnp.abs(x)))


def _gelu_tanh(x):
    c = math.sqrt(2.0 / math.pi)
    return 0.5 * x * (1.0 + jnp.tanh(c * (x + 0.044715 * (x * x * x))))


def _split3(x):
    hi = x.astype(BF16)
    r = x - hi.astype(F32)
    mid = r.astype(BF16)
    lo = (r - mid.astype(F32)).astype(BF16)
    return hi, mid, lo


def _dot(a, b):
    return jnp.dot(a, b, preferred_element_type=F32)


def _dot_nt(a, b):
    return lax.dot_general(a, b, (((1,), (1,)), ((), ())), preferred_element_type=F32)


def _dot_tn(a, b):
    return lax.dot_general(a, b, (((0,), (0,)), ((), ())), preferred_element_type=F32)


def _rmsnorm_kernel(x_ref, g_ref, *rest):
    o_ref = rest[-1]
    x = x_ref[...]
    r = lax.rsqrt(jnp.mean(x * x, axis=-1, keepdims=True) + EPS)
    o_ref[...] = (x * r * g_ref[...]).astype(o_ref.dtype)


def _rmsnorm_rows(x, g, t_all, row0, tm, alias=None):
    n = x.shape[0]
    blk0 = row0 // tm
    in_specs = [pl.BlockSpec((tm, D_MODEL), lambda i: (i, 0)),
                pl.BlockSpec((1, D_MODEL), lambda i: (0, 0))]
    args = [x, g.reshape(1, D_MODEL)]
    aliases = {}
    if alias is not None:
        in_specs.append(pl.BlockSpec(memory_space=pl.ANY))
        args.append(alias)
        aliases = {2: 0}
    return pl.pallas_call(
        _rmsnorm_kernel,
        out_shape=jax.ShapeDtypeStruct((t_all, D_MODEL), BF16),
        grid=(n // tm,),
        in_specs=in_specs,
        out_specs=pl.BlockSpec((tm, D_MODEL), lambda i: (i + blk0, 0)),
        input_output_aliases=aliases,
        compiler_params=_cparams(("arbitrary",)),
        name="rmsnorm_rows",
    )(*args)


def _mm_kernel(a_ref, w_ref, b_ref, o_ref, wb_ref):
    @pl.when(pl.program_id(1) == 0)
    def _():
        wb_ref[...] = w_ref[...].astype(BF16)

    o_ref[...] = (_dot(a_ref[...], wb_ref[...]) + b_ref[...]).astype(o_ref.dtype)


def _matmul(a, w, bias, n_out, tm, tn, out_dtype, col_blk0=0):
    m, k = a.shape
    return pl.pallas_call(
        _mm_kernel,
        out_shape=jax.ShapeDtypeStruct((m, n_out), out_dtype),
        grid=(n_out // tn, m // tm),
        in_specs=[pl.BlockSpec((tm, k), lambda j, i: (i, 0)),
                  pl.BlockSpec((k, tn), lambda j, i: (0, j + col_blk0)),
                  pl.BlockSpec((1, tn), lambda j, i: (0, j))],
        out_specs=pl.BlockSpec((tm, tn), lambda j, i: (i, j)),
        scratch_shapes=[pltpu.VMEM((k, tn), BF16)],
        compiler_params=_cparams(("arbitrary", "arbitrary")),
        name="rows_matmul",
    )(a, w, bias)


def _mm_t_kernel(starts_ref, a_ref, wt_ref, b_ref, o_ref, wb_ref):
    del starts_ref

    @pl.when(pl.program_id(1) == 0)
    def _():
        wb_ref[...] = wt_ref[...].astype(BF16)

    o_ref[...] = (_dot_nt(a_ref[...], wb_ref[...]) + b_ref[...]).astype(o_ref.dtype)


def _matmul_t(a, w_t, row_starts, bias, tm, tn, out_dtype):
    m, k = a.shape
    n_t = len(row_starts)
    assert all(s % SUBLANES == 0 for s in row_starts)
    grid_spec = pltpu.PrefetchScalarGridSpec(
        num_scalar_prefetch=1,
        grid=(n_t, m // tm),
        in_specs=[pl.BlockSpec((tm, k), lambda j, i, st: (i, 0)),
                  pl.BlockSpec((pl.Element(tn), pl.Element(k)), lambda j, i, st: (st[j] * SUBLANES, 0)),
                  pl.BlockSpec((1, tn), lambda j, i, st: (0, j))],
        out_specs=pl.BlockSpec((tm, tn), lambda j, i, st: (i, j)),
        scratch_shapes=[pltpu.VMEM((tn, k), BF16)],
    )
    return pl.pallas_call(
        _mm_t_kernel,
        out_shape=jax.ShapeDtypeStruct((m, n_t * tn), out_dtype),
        grid_spec=grid_spec,
        compiler_params=_cparams(("arbitrary", "arbitrary")),
        name="rows_matmul_t",
    )(jnp.asarray([s // SUBLANES for s in row_starts], jnp.int32), a, w_t, bias)


def _mm_shift_kernel(a_ref, wlo_ref, whi_ref, o_ref, wb_ref, *, shift):
    tn = wlo_ref.shape[1]
    rows = 256

    @pl.when(pl.program_id(1) == 0)
    def _():
        for r in range(wlo_ref.shape[0] // rows):
            sl = slice(r * rows, (r + 1) * rows)
            w = jnp.concatenate([wlo_ref[sl, :], whi_ref[sl, :]], axis=1)
            wb_ref[sl, :] = w[:, shift:shift + tn].astype(BF16)

    o_ref[...] = _dot(a_ref[...], wb_ref[...]).astype(o_ref.dtype)


def _matmul_shifted(a, w, col_starts, tm, tn, out_dtype):
    m, k = a.shape
    shift = col_starts[0] % tn
    assert all(c % tn == shift for c in col_starts) and 0 < shift < LANES
    lo_blk = [c // tn for c in col_starts]
    n_t = len(col_starts)
    base, first = min(lo_blk), lo_blk[0]
    assert all(lo_blk[j] == base + (j + first - base) % n_t for j in range(n_t))
    lo_idx = lambda j: base + (j + (first - base)) % n_t
    per = tn // LANES
    kern = functools.partial(_mm_shift_kernel, shift=shift)
    return pl.pallas_call(
        kern,
        out_shape=jax.ShapeDtypeStruct((m, n_t * tn), out_dtype),
        grid=(n_t, m // tm),
        in_specs=[pl.BlockSpec((tm, k), lambda j, i: (i, 0)),
                  pl.BlockSpec((k, tn), lambda j, i: (0, lo_idx(j))),
                  pl.BlockSpec((k, LANES), lambda j, i: (0, (lo_idx(j) + 1) * per))],
        out_specs=pl.BlockSpec((tm, tn), lambda j, i: (i, j)),
        scratch_shapes=[pltpu.VMEM((k, tn), BF16)],
        compiler_params=_cparams(("arbitrary", "arbitrary")),
        name="rows_matmul_shifted",
    )(a, w, w)


def _mlstm_kernel(qk_ref, v_ref, o_ref, gcol_ref, grow_ref, wc_ref, bc_ref, hg_ref,
                  h_ref, c_out, n_out, m_out, c_sc, n_sc, m_sc, ext_sc):
    c = pl.program_id(1)
    L = CHUNK
    pad = SUBLANES
    d_a = H_A * DK

    @pl.when(c == 0)
    def _():
        c_sc[...] = jnp.zeros_like(c_sc)
        n_sc[...] = jnp.zeros_like(n_sc)
        m_sc[...] = jnp.zeros_like(m_sc)
        ext_sc[0:pad, :] = jnp.zeros((pad, 2 * d_a), F32)

    x = qk_ref[...]
    ext_sc[pad:pad + L, :] = x
    y = bc_ref[...] + wc_ref[CONV_W - 1:CONV_W, :] * x
    for j in range(1, CONV_W):
        y = y + wc_ref[CONV_W - 1 - j:CONV_W - j, :] * ext_sc[pad - j:pad - j + L, :]
    ext_sc[0:pad, :] = x[L - pad:L, :]
    y = _silu(y)

    gcol = gcol_ref[...]
    grow = grow_ref[...]
    ri = lax.broadcasted_iota(jnp.int32, (L, L), 0)
    ci = lax.broadcasted_iota(jnp.int32, (L, L), 1)
    causal = ci <= ri
    tril = jnp.where(causal, 1.0, 0.0).astype(BF16)
    triu = jnp.where(ri <= ci, 1.0, 0.0).astype(BF16)
    b_cols = sum(_dot(tril, p) for p in _split3(_log_sigmoid(gcol)))
    b_rows = sum(_dot(p, triu) for p in _split3(_log_sigmoid(grow)))

    for h in range(H_A):
        sl = slice(h * DK, (h + 1) * DK)
        q = y[:, sl] * (DK ** -0.5)
        k = y[:, d_a + h * DK:d_a + (h + 1) * DK]
        ig_col = gcol[:, h:h + 1]
        ig_row = grow[h:h + 1, :]
        b_col = b_cols[:, H_A + h:H_A + h + 1]
        b_row = b_rows[H_A + h:H_A + h + 1, :]

        m_prev = m_sc[:, h:h + 1]
        d_log = jnp.where(causal, b_col - b_row + ig_row, -jnp.inf)
        inter_log = b_col + m_prev
        m_t = jnp.maximum(inter_log, jnp.max(d_log, axis=1, keepdims=True))
        qb = q.astype(BF16)
        kb = k.astype(BF16)
        vb = v_ref[:, sl].astype(BF16)
        s = _dot_nt(qb, kb) * jnp.exp(d_log - m_t)
        inter_w = jnp.exp(inter_log - m_t)
        c_prev = c_sc[h]
        n_prev = n_sc[h:h + 1, :]
        num = _dot(s.astype(BF16), vb) + inter_w * _dot(qb, c_prev.astype(BF16))
        nq = jnp.sum(s, axis=1, keepdims=True) + inter_w * jnp.sum(q * n_prev, axis=1, keepdims=True)
        den = jnp.maximum(jnp.abs(nq), jnp.exp(-m_t))
        hh = num / den
        hh = hh * _sigmoid(o_ref[:, sl])
        hh = hh * lax.rsqrt(jnp.mean(hh * hh, axis=1, keepdims=True) + EPS)
        h_ref[:, sl] = (hh * hg_ref[:, sl]).astype(h_ref.dtype)

        m_new = m_t[L - 1:L, :]
        b_last = b_col[L - 1:L, :]
        decay = jnp.exp(b_last + m_prev - m_new)
        w_end = jnp.exp(b_last - b_col + ig_col - m_new)
        kw = k * w_end
        c_sc[h] = decay * c_prev + _dot_tn(kw.astype(BF16), vb)
        n_sc[h:h + 1, :] = decay * n_prev + jnp.sum(kw, axis=0, keepdims=True)
        m_sc[:, h:h + 1] = m_new

    @pl.when(c == pl.num_programs(1) - 1)
    def _():
        c_out[...] = c_sc[...]
        n_out[...] = n_sc[...]
        m_out[...] = m_sc[...]


def _mlstm_prompt(qkvo, gates, gates_t, w_conv, b_conv, head_g, batch, seq, t_all):
    nc = seq // CHUNK
    L = CHUNK
    d_a = H_A * DK
    row = lambda b, c: b * nc + c
    in_specs = [
        pl.BlockSpec((L, 2 * d_a), lambda b, c: (row(b, c), 0)),
        pl.BlockSpec((L, d_a), lambda b, c: (row(b, c), 2)),
        pl.BlockSpec((L, d_a), lambda b, c: (row(b, c), 3)),
        pl.BlockSpec((L, LANES), lambda b, c: (row(b, c), 0)),
        pl.BlockSpec((2 * H_A, L), lambda b, c: (0, row(b, c))),
        pl.BlockSpec((CONV_W, 2 * d_a), lambda b, c: (0, 0)),
        pl.BlockSpec((1, 2 * d_a), lambda b, c: (0, 0)),
        pl.BlockSpec((1, d_a), lambda b, c: (0, 0)),
    ]
    out_shape = (
        jax.ShapeDtypeStruct((t_all, d_a), BF16),
        jax.ShapeDtypeStruct((batch, H_A, DK, DK), F32),
        jax.ShapeDtypeStruct((batch, H_A, DK), F32),
        jax.ShapeDtypeStruct((batch, 1, LANES), F32),
    )
    out_specs = (
        pl.BlockSpec((L, d_a), lambda b, c: (row(b, c), 0)),
        pl.BlockSpec((None, H_A, DK, DK), lambda b, c: (b, 0, 0, 0)),
        pl.BlockSpec((None, H_A, DK), lambda b, c: (b, 0, 0)),
        pl.BlockSpec((None, 1, LANES), lambda b, c: (b, 0, 0)),
    )
    return pl.pallas_call(
        _mlstm_kernel,
        out_shape=out_shape,
        grid=(batch, nc),
        in_specs=in_specs,
        out_specs=out_specs,
        scratch_shapes=[pltpu.VMEM((H_A, DK, DK), F32), pltpu.VMEM((H_A, DK), F32),
                        pltpu.VMEM((1, LANES), F32), pltpu.VMEM((SUBLANES + L, 2 * d_a), F32)],
        compiler_params=_cparams(("arbitrary", "arbitrary")),
        name="mlstm_prompt",
    )(qkvo, qkvo, qkvo, gates, gates_t, w_conv, b_conv, head_g)


def _s5_prompt_kernel(u_ref, bb_ref, cc_ref, lbr_ref, lbi_ref, d_ref, ys_ref, sre_ref, sim_ref, x_sc,
                      *, seq):
    seg = seq // S5_NSEG
    sstr = seg + SUBLANES
    nt = S5_XW // LANES
    rows = min(256, seg)
    srow = lambda t: (t // seg) * sstr + t % seg
    for r in range(seq // rows):
        ub = u_ref[r * rows:(r + 1) * rows, :].astype(BF16)
        bu = _dot(ub, bb_ref[...])
        r0 = srow(r * rows)
        for j in range(2 * nt):
            x_sc[j, r0:r0 + rows, :] = bu[:, j * LANES:(j + 1) * LANES]

    grp = 4
    for t0 in range(0, nt, grp):
        tiles = list(range(t0, t0 + grp))
        lbr = [jnp.broadcast_to(lbr_ref[:, j * LANES:(j + 1) * LANES], (S5_NSEG, LANES)) for j in tiles]
        lbi = [jnp.broadcast_to(lbi_ref[:, j * LANES:(j + 1) * LANES], (S5_NSEG, LANES)) for j in tiles]

        def scan_body(kk, carry, lbr=lbr, lbi=lbi, tiles=tiles):
            seg_rows = pl.ds(kk, S5_NSEG, stride=sstr)
            out = []
            for i, j in enumerate(tiles):
                xr, xi = carry[i]
                nr = lbr[i] * xr - lbi[i] * xi + x_sc[j, seg_rows, :]
                ni = lbr[i] * xi + lbi[i] * xr + x_sc[nt + j, seg_rows, :]
                x_sc[j, seg_rows, :] = nr
                x_sc[nt + j, seg_rows, :] = ni
                out.append((nr, ni))
            return tuple(out)

        z = jnp.zeros((S5_NSEG, LANES), F32)
        ends = lax.fori_loop(0, seg, scan_body, tuple((z, z) for _ in tiles))

        starts = []
        for i, j in enumerate(tiles):
            er, ei = ends[i]
            pr, pi = lbr[i][0:1, :], lbi[i][0:1, :]
            for _ in range(int(math.log2(seg))):
                pr, pi = pr * pr - pi * pi, 2.0 * pr * pi
            sr_rows = [jnp.zeros((1, LANES), F32)]
            si_rows = [jnp.zeros((1, LANES), F32)]
            for s in range(S5_NSEG):
                sr_rows.append(er[s:s + 1, :] + pr * sr_rows[s] - pi * si_rows[s])
                si_rows.append(ei[s:s + 1, :] + pr * si_rows[s] + pi * sr_rows[s])
            sre_ref[:, j * LANES:(j + 1) * LANES] = sr_rows[S5_NSEG]
            sim_ref[:, j * LANES:(j + 1) * LANES] = si_rows[S5_NSEG]
            starts.append((jnp.concatenate(sr_rows[:S5_NSEG], axis=0),
                           jnp.concatenate(si_rows[:S5_NSEG], axis=0)))

        def fix_body(kk, pw, lbr=lbr, lbi=lbi, starts=starts, tiles=tiles):
            seg_rows = pl.ds(kk, S5_NSEG, stride=sstr)
            out = []
            for i, j in enumerate(tiles):
                wr, wi = pw[i]
                s_r, s_i = starts[i]
                x_sc[j, seg_rows, :] = x_sc[j, seg_rows, :] + wr * s_r - wi * s_i
                x_sc[nt + j, seg_rows, :] = x_sc[nt + j, seg_rows, :] + wr * s_i + wi * s_r
                out.append((wr * lbr[i] - wi * lbi[i], wr * lbi[i] + wi * lbr[i]))
            return tuple(out)

        lax.fori_loop(0, seg, fix_body, tuple((lbr[i], lbi[i]) for i in range(grp)))

    for r in range(seq // rows):
        r0 = srow(r * rows)
        xb = jnp.concatenate([x_sc[j, r0:r0 + rows, :] for j in range(2 * nt)], axis=1)
        y = _dot(xb.astype(BF16), cc_ref[...]) + d_ref[...] * u_ref[r * rows:(r + 1) * rows, :]
        ys_ref[r * rows:(r + 1) * rows, :] = _gelu_tanh(y).astype(ys_ref.dtype)


def _s5_prompt(proj2, u_col0, bb, cc, lbr, lbi, d_row, batch, seq, t_all):
    kern = functools.partial(_s5_prompt_kernel, seq=seq)
    ub0 = u_col0 // S5_UW
    return pl.pallas_call(
        kern,
        out_shape=(jax.ShapeDtypeStruct((t_all, D_B), F32),
                   jax.ShapeDtypeStruct((batch, 1, G_B * P_S5), F32),
                   jax.ShapeDtypeStruct((batch, 1, G_B * P_S5), F32)),
        grid=(S5_GB, batch),
        in_specs=[pl.BlockSpec((seq, S5_UW), lambda g, b: (b, ub0 + g)),
                  pl.BlockSpec((None, S5_UW, 2 * S5_XW), lambda g, b: (g, 0, 0)),
                  pl.BlockSpec((None, 2 * S5_XW, S5_UW), lambda g, b: (g, 0, 0)),
                  pl.BlockSpec((None, 1, S5_XW), lambda g, b: (g, 0, 0)),
                  pl.BlockSpec((None, 1, S5_XW), lambda g, b: (g, 0, 0)),
                  pl.BlockSpec((1, S5_UW), lambda g, b: (0, g))],
        out_specs=(pl.BlockSpec((seq, S5_UW), lambda g, b: (b, g)),
                   pl.BlockSpec((None, 1, S5_XW), lambda g, b: (b, 0, g)),
                   pl.BlockSpec((None, 1, S5_XW), lambda g, b: (b, 0, g))),
        scratch_shapes=[pltpu.VMEM((2 * S5_XW // LANES, seq + S5_NSEG * SUBLANES, LANES), F32)],
        compiler_params=_cparams(("arbitrary", "arbitrary")),
        name="s5_prompt",
    )(proj2, bb, cc, lbr, lbi, d_row)


def _sample_pre_kernel(qk_ref, conv_ref, wc_ref, bc_ref, g_ref, m_ref, n_ref,
                       q_out, kw_out, a_out, s_out, den_out, conv_out, n_out, m_out):
    c2 = 2 * H_A * DK
    x_new = qk_ref[...]
    y = bc_ref[...] + wc_ref[CONV_W - 1:CONV_W, :] * x_new
    for j in range(CONV_W - 1):
        y = y + wc_ref[j:j + 1, :] * conv_ref[:, j * c2:(j + 1) * c2]
    y = _silu(y)
    conv_out[:, 0:(CONV_W - 2) * c2] = conv_ref[:, c2:(CONV_W - 1) * c2]
    conv_out[:, (CONV_W - 2) * c2:(CONV_W - 1) * c2] = x_new
    g = g_ref[...]
    bd = x_new.shape[0]
    m_cols = []
    for h in range(H_A):
        sl = slice(h * DK, (h + 1) * DK)
        q = y[:, sl] * (DK ** -0.5)
        k = y[:, H_A * DK + h * DK:H_A * DK + (h + 1) * DK]
        ig = g[:, h:h + 1]
        lf = _log_sigmoid(g[:, H_A + h:H_A + h + 1])
        m_prev = m_ref[:, h:h + 1]
        m_t = jnp.maximum(lf + m_prev, ig)
        a = jnp.exp(lf + m_prev - m_t)
        wgt = jnp.exp(ig - m_t)
        n_prev = n_ref[:, sl]
        s = jnp.sum(q * k, axis=1, keepdims=True) * wgt
        nq = s + a * jnp.sum(q * n_prev, axis=1, keepdims=True)
        den = jnp.maximum(jnp.abs(nq), jnp.exp(-m_t))
        kw = wgt * k
        q_out[:, sl] = q
        kw_out[:, sl] = kw
        a_out[:, sl] = jnp.broadcast_to(a, (bd, DK))
        s_out[:, sl] = jnp.broadcast_to(s, (bd, DK))
        den_out[:, sl] = jnp.broadcast_to(den, (bd, DK))
        n_out[:, sl] = a * n_prev + kw
        m_cols.append(m_t)
    lane = lax.broadcasted_iota(jnp.int32, (bd, LANES), 1)
    m_full = jnp.zeros((bd, LANES), F32)
    for h in range(H_A):
        m_full = jnp.where(lane == h, m_cols[h], m_full)
    m_out[...] = m_full


def _sample_pre(qkvo, conv_state, w_conv, b_conv, gates, m_state, n_state, dec, row_blk):
    c2 = 2 * H_A * DK
    d = H_A * DK
    full = lambda shape: pl.BlockSpec(shape, lambda i: (0,) * len(shape))
    rows = lambda: jax.ShapeDtypeStruct((dec, d), F32)
    return pl.pallas_call(
        _sample_pre_kernel,
        out_shape=(rows(), rows(), rows(), rows(), rows(),
                   jax.ShapeDtypeStruct((dec, (CONV_W - 1) * c2), F32), rows(),
                   jax.ShapeDtypeStruct((dec, LANES), F32)),
        grid=(1,),
        in_specs=[pl.BlockSpec((dec, c2), lambda i: (row_blk, 0)),
                  full((dec, (CONV_W - 1) * c2)), full((CONV_W, c2)), full((1, c2)),
                  pl.BlockSpec((dec, LANES), lambda i: (row_blk, 0)),
                  full((dec, H_A)), full((dec, d))],
        out_specs=(full((dec, d)), full((dec, d)), full((dec, d)), full((dec, d)), full((dec, d)),
                   full((dec, (CONV_W - 1) * c2)), full((dec, d)), full((dec, LANES))),
        compiler_params=_cparams(("arbitrary",)),
        name="sample_pre",
    )(qkvo, conv_state, w_conv, b_conv, gates, m_state, n_state)


def _sample_c_kernel(q_ref, kw_ref, v_ref, a_ref, c_ref, c_out, qc_out):
    eye = jnp.where(lax.broadcasted_iota(jnp.int32, (DK, DK), 0)
                    == lax.broadcasted_iota(jnp.int32, (DK, DK), 1), 1.0, 0.0).astype(BF16)
    q_t = sum(_dot_nt(eye, p) for p in _split3(q_ref[...]))
    kw_t = sum(_dot_nt(eye, p) for p in _split3(kw_ref[...]))
    qc_rows = []
    for h in range(H_A):
        c_prev = c_ref[h]
        qc_rows.append(jnp.sum(q_t[:, h:h + 1] * c_prev, axis=0, keepdims=True))
        c_out[h] = a_ref[h:h + 1, :] * c_prev + kw_t[:, h:h + 1] * v_ref[h:h + 1, :]
    qc_out[...] = jnp.concatenate(qc_rows, axis=0)


def _sample_c(q3, kw3, v3, a3, c_state, dec):
    vec = pl.BlockSpec((None, H_A, DK), lambda b: (b, 0, 0))
    mat = pl.BlockSpec((None, H_A, DK, DK), lambda b: (b, 0, 0, 0))
    return pl.pallas_call(
        _sample_c_kernel,
        out_shape=(jax.ShapeDtypeStruct((dec, H_A, DK, DK), F32),
                   jax.ShapeDtypeStruct((dec, H_A, DK), F32)),
        grid=(dec,),
        in_specs=[vec, vec, vec, vec, mat],
        out_specs=(mat, vec),
        compiler_params=_cparams(("arbitrary",)),
        name="sample_c_update",
    )(q3, kw3, v3, a3, c_state)


def _sample_post_kernel(qc_ref, s_ref, a_ref, den_ref, vo_ref, hg_ref, hn_in, h_ref):
    del hn_in
    d = H_A * DK
    num = s_ref[...] * vo_ref[:, 0:d] + a_ref[...] * qc_ref[...]
    hh = num / den_ref[...]
    hh = hh * _sigmoid(vo_ref[:, d:2 * d])
    for h in range(H_A):
        sl = slice(h * DK, (h + 1) * DK)
        seg = hh[:, sl]
        seg = seg * lax.rsqrt(jnp.mean(seg * seg, axis=1, keepdims=True) + EPS)
        h_ref[:, sl] = (seg * hg_ref[:, sl]).astype(h_ref.dtype)


def _sample_post(qc, s_e, a_e, den_e, qkvo, head_g, hn_all, dec, row_blk):
    d = H_A * DK
    full = lambda shape: pl.BlockSpec(shape, lambda i: (0,) * len(shape))
    return pl.pallas_call(
        _sample_post_kernel,
        out_shape=jax.ShapeDtypeStruct(hn_all.shape, hn_all.dtype),
        grid=(1,),
        in_specs=[full((dec, d)), full((dec, d)), full((dec, d)), full((dec, d)),
                  pl.BlockSpec((dec, 2 * d), lambda i: (row_blk, 1)),
                  full((1, d)), pl.BlockSpec(memory_space=pl.ANY)],
        out_specs=pl.BlockSpec((dec, d), lambda i: (row_blk, 0)),
        input_output_aliases={6: 0},
        compiler_params=_cparams(("arbitrary",)),
        name="sample_post",
    )(qc, s_e, a_e, den_e, qkvo, head_g, hn_all)


def _s5_sample_kernel(u_ref, sr_ref, si_ref, bb_ref, cc_ref, lbr_ref, lbi_ref, d_ref, ys_in,
                      ys_ref, sre_out, sim_out):
    del ys_in
    for g in range(S5_GB):
        u = u_ref[:, g * S5_UW:(g + 1) * S5_UW]
        bu = _dot(u.astype(BF16), bb_ref[g])
        sl = slice(g * S5_XW, (g + 1) * S5_XW)
        lbr = lbr_ref[g]
        lbi = lbi_ref[g]
        sr = sr_ref[:, sl]
        si = si_ref[:, sl]
        xr = lbr * sr - lbi * si + bu[:, 0:S5_XW]
        xi = lbr * si + lbi * sr + bu[:, S5_XW:2 * S5_XW]
        sre_out[:, sl] = xr
        sim_out[:, sl] = xi
        x = jnp.concatenate([xr, xi], axis=1).astype(BF16)
        y = _dot(x, cc_ref[g]) + d_ref[:, g * S5_UW:(g + 1) * S5_UW] * u
        ys_ref[:, g * S5_UW:(g + 1) * S5_UW] = _gelu_tanh(y).astype(ys_ref.dtype)


def _s5_sample(proj2, u_col0, s_re, s_im, bb, cc, lbr, lbi, d_row, ys_all, dec, row_blk):
    full = lambda shape: pl.BlockSpec(shape, lambda i: (0,) * len(shape))
    n_state = G_B * P_S5
    ub0 = u_col0 // D_B
    return pl.pallas_call(
        _s5_sample_kernel,
        out_shape=(jax.ShapeDtypeStruct(ys_all.shape, ys_all.dtype),
                   jax.ShapeDtypeStruct((dec, n_state), F32),
                   jax.ShapeDtypeStruct((dec, n_state), F32)),
        grid=(1,),
        in_specs=[pl.BlockSpec((dec, D_B), lambda i: (row_blk, ub0)),
                  full((dec, n_state)), full((dec, n_state)),
                  full(bb.shape), full(cc.shape), full(lbr.shape), full(lbi.shape), full((1, D_B)),
                  pl.BlockSpec(memory_space=pl.ANY)],
        out_specs=(pl.BlockSpec((dec, D_B), lambda i: (row_blk, 0)),
                   full((dec, n_state)), full((dec, n_state))),
        input_output_aliases={8: 0},
        compiler_params=_cparams(("arbitrary",)),
        name="s5_sample",
    )(proj2, s_re, s_im, bb, cc, lbr, lbi, d_row, ys_all)


def _merge_kernel(x_ref, hn_ref, ys_ref, ga_ref, gb_ref, bglu_ref, gffn_ref, wrh_ref, wrl_ref, br_ref,
                  wpa_hbm, wglu_hbm, wpb_hbm, wout_hbm, *rest):
    n_alias = len(rest) - 9
    x1_ref, xn_ref, ids_ref, wts_ref = rest[n_alias:n_alias + 4]
    wpa, wglu, wpb, wout, sem = rest[n_alias + 4:]

    @pl.when(pl.program_id(0) == 0)
    def _():
        copies = [pltpu.make_async_copy(src, dst, sem.at[i])
                  for i, (src, dst) in enumerate(((wpa_hbm, wpa), (wglu_hbm, wglu),
                                                  (wpb_hbm, wpb), (wout_hbm, wout)))]
        for cp in copies:
            cp.start()
        for cp in copies:
            cp.wait()

    ya = _dot(hn_ref[...], wpa[...])
    ys = ys_ref[...]
    gate = _sigmoid(_dot(ys.astype(BF16), wglu[...]) + bglu_ref[...])
    yb = _dot((ys * gate).astype(BF16), wpb[...])
    z = _sigmoid(ga_ref[...]) * ya + _sigmoid(gb_ref[...]) * yb
    x1 = x_ref[...] + _dot(z.astype(BF16), wout[...])
    x1_ref[...] = x1
    xn = x1 * lax.rsqrt(jnp.mean(x1 * x1, axis=1, keepdims=True) + EPS) * gffn_ref[...]
    xn_ref[...] = xn
    xh = xn.astype(BF16)
    xl = (xn - xh.astype(F32)).astype(BF16)
    logits = _dot(xh, wrh_ref[...]) + _dot(xl, wrh_ref[...]) + _dot(xh, wrl_ref[...]) + br_ref[...]

    lane_i = lax.broadcasted_iota(jnp.int32, logits.shape, 1)
    lane = lane_i.astype(F32)
    neg = -jnp.inf
    big = float(1 << 20)
    gl = jnp.where(lane_i < N_GROUPS, logits, neg)
    gmax = jnp.max(gl, axis=1, keepdims=True)
    gsum = jnp.sum(jnp.exp(gl - gmax), axis=1, keepdims=True)
    gidx = jnp.min(jnp.where(gl == gmax, lane, big), axis=1, keepdims=True)
    pg_sel = 1.0 / gsum
    lo = N_GROUPS + gidx * EXP_PER_GROUP
    in_grp = (lane >= lo) & (lane < lo + EXP_PER_GROUP)
    el = jnp.where(in_grp, logits, neg)
    emax = jnp.max(el, axis=1, keepdims=True)
    ee = jnp.exp(el - emax)
    pe = ee / jnp.sum(ee, axis=1, keepdims=True)
    v0 = jnp.max(pe, axis=1, keepdims=True)
    i0 = jnp.min(jnp.where(in_grp & (pe == v0), lane, big), axis=1, keepdims=True)
    rest_m = in_grp & (lane != i0)
    pe1 = jnp.where(rest_m, pe, neg)
    v1 = jnp.max(pe1, axis=1, keepdims=True)
    i1 = jnp.min(jnp.where(rest_m & (pe1 == v1), lane, big), axis=1, keepdims=True)
    tot = v0 + v1
    w0 = pg_sel * (v0 / tot)
    w1 = pg_sel * (v1 / tot)
    ids = jnp.where(lane_i == 0, i0 - N_GROUPS, jnp.where(lane_i == 1, i1 - N_GROUPS, 0.0))
    ids_ref[...] = ids.astype(jnp.int32)
    wts_ref[...] = jnp.where(lane_i == 0, w0, jnp.where(lane_i == 1, w1, 0.0))


def _merge(x, hn_all, ys_all, proj2, b_glu, g_ffn, wr_hi, wr_lo, b_r, wpa, wglu, wpb, wout,
           t_all, row0, tm, aliases=None):
    n = x.shape[0]
    blk0 = row0 // tm
    const = lambda shape: pl.BlockSpec(shape, lambda i: (0,) * len(shape))
    any_spec = pl.BlockSpec(memory_space=pl.ANY)
    in_specs = [pl.BlockSpec((tm, D_MODEL), lambda i: (i, 0)),
                pl.BlockSpec((tm, D_MODEL), lambda i: (i + blk0, 0)),
                pl.BlockSpec((tm, D_B), lambda i: (i + blk0, 0)),
                pl.BlockSpec((tm, D_MODEL), lambda i: (i + blk0, 0)),
                pl.BlockSpec((tm, D_MODEL), lambda i: (i + blk0, 1)),
                const((1, D_B)), const((1, D_MODEL)),
                const((D_MODEL, LANES)), const((D_MODEL, LANES)), const((1, LANES)),
                any_spec, any_spec, any_spec, any_spec]
    args = [x, hn_all, ys_all, proj2, proj2, b_glu, g_ffn, wr_hi, wr_lo, b_r, wpa, wglu, wpb, wout]
    io_alias = {}
    if aliases is not None:
        for j, a in enumerate(aliases):
            in_specs.append(any_spec)
            args.append(a)
            io_alias[14 + j] = j
    out_shape = (jax.ShapeDtypeStruct((t_all, D_MODEL), F32),
                 jax.ShapeDtypeStruct((t_all, D_MODEL), F32),
                 jax.ShapeDtypeStruct((t_all, LANES), jnp.int32),
                 jax.ShapeDtypeStruct((t_all, LANES), F32))
    out_specs = (pl.BlockSpec((tm, D_MODEL), lambda i: (i + blk0, 0)),
                 pl.BlockSpec((tm, D_MODEL), lambda i: (i + blk0, 0)),
                 pl.BlockSpec((tm, LANES), lambda i: (i + blk0, 0)),
                 pl.BlockSpec((tm, LANES), lambda i: (i + blk0, 0)))
    return pl.pallas_call(
        _merge_kernel,
        out_shape=out_shape,
        grid=(n // tm,),
        in_specs=in_specs,
        out_specs=out_specs,
        scratch_shapes=[pltpu.VMEM(wpa.shape, BF16), pltpu.VMEM(wglu.shape, BF16),
                        pltpu.VMEM(wpb.shape, BF16), pltpu.VMEM(wout.shape, BF16),
                        pltpu.SemaphoreType.DMA((4,))],
        input_output_aliases=io_alias,
        compiler_params=_cparams(("arbitrary",)),
        name="merge_router",
    )(*args)


def _moe_kernel(te_ref, nt_ref, xs_ref, wg_ref, wu_ref, wd_ref, o_ref, wg_sc, wu_sc, wd_sc):
    i = pl.program_id(0)
    e = te_ref[i]
    prev = te_ref[jnp.maximum(i - 1, 0)]

    @pl.when((i == 0) | (e != prev))
    def _():
        wg_sc[...] = wg_ref[...].astype(BF16)
        wu_sc[...] = wu_ref[...].astype(BF16)
        wd_sc[...] = wd_ref[...].astype(BF16)

    @pl.when(i < nt_ref[0])
    def _():
        x = xs_ref[...].astype(BF16)
        hg = _dot(x, wg_sc[...])
        hu = _dot(x, wu_sc[...])
        hh = (_silu(hg) * hu).astype(BF16)
        o_ref[...] = _dot(hh, wd_sc[...])


def _moe_experts(tile_expert, n_tiles, xs, w_gate, w_up, w_down):
    p_rows = xs.shape[0]
    last = lambda i, te, nt: jnp.minimum(i, nt[0] - 1)
    grid_spec = pltpu.PrefetchScalarGridSpec(
        num_scalar_prefetch=2,
        grid=(p_rows // MOE_TILE,),
        in_specs=[pl.BlockSpec((MOE_TILE, D_MODEL), lambda i, te, nt: (last(i, te, nt), 0)),
                  pl.BlockSpec((None, D_MODEL, D_EXPERT), lambda i, te, nt: (te[i], 0, 0)),
                  pl.BlockSpec((None, D_MODEL, D_EXPERT), lambda i, te, nt: (te[i], 0, 0)),
                  pl.BlockSpec((None, D_EXPERT, D_MODEL), lambda i, te, nt: (te[i], 0, 0))],
        out_specs=pl.BlockSpec((MOE_TILE, D_MODEL), lambda i, te, nt: (last(i, te, nt), 0)),
        scratch_shapes=[pltpu.VMEM((D_MODEL, D_EXPERT), BF16), pltpu.VMEM((D_MODEL, D_EXPERT), BF16),
                        pltpu.VMEM((D_EXPERT, D_MODEL), BF16)],
    )
    return pl.pallas_call(
        _moe_kernel,
        out_shape=jax.ShapeDtypeStruct((p_rows, D_MODEL), F32),
        grid_spec=grid_spec,
        compiler_params=_cparams(("arbitrary",)),
        name="moe_experts",
    )(tile_expert, n_tiles, xs, w_gate, w_up, w_down)


def _final_kernel(x1_ref, y0_ref, y1_ref, w_ref, g_ref, o_ref):
    w = w_ref[...]
    x2 = x1_ref[...] + w[:, 0:1] * y0_ref[...] + w[:, 1:2] * y1_ref[...]
    o_ref[...] = x2 * lax.rsqrt(jnp.mean(x2 * x2, axis=1, keepdims=True) + EPS) * g_ref[...]


def _final(x1_all, yg0, yg1, wts, g_final, row0, n, tm):
    blk0 = row0 // tm
    rows = pl.BlockSpec((tm, D_MODEL), lambda i: (i + blk0, 0))
    return pl.pallas_call(
        _final_kernel,
        out_shape=jax.ShapeDtypeStruct((n, D_MODEL), F32),
        grid=(n // tm,),
        in_specs=[rows, rows, rows,
                  pl.BlockSpec((tm, LANES), lambda i: (i + blk0, 0)),
                  pl.BlockSpec((1, D_MODEL), lambda i: (0, 0))],
        out_specs=pl.BlockSpec((tm, D_MODEL), lambda i: (i, 0)),
        compiler_params=_cparams(("arbitrary",)),
        name="combine_final_norm",
    )(x1_all, yg0, yg1, wts, g_final)


def _s5_params(a_re, a_im, log_step, b_re, b_im, c_re, c_im):
    dt = jnp.exp(log_step)[:, None]
    mag = jnp.exp(a_re * dt)
    lb_re = mag * jnp.cos(a_im * dt)
    lb_im = mag * jnp.sin(a_im * dt)
    den = a_re * a_re + a_im * a_im
    nr = lb_re - 1.0
    coef_re = (nr * a_re + lb_im * a_im) / den
    coef_im = (lb_im * a_re - nr * a_im) / den
    bb_re = coef_re[..., None] * b_re - coef_im[..., None] * b_im
    bb_im = coef_re[..., None] * b_im + coef_im[..., None] * b_re
    eye = jnp.eye(S5_GPB, dtype=F32)

    def blockdiag_in(m):
        m = m.reshape(S5_GB, S5_GPB, P_S5, S5_GROUP)
        return jnp.einsum('bgpc,gh->bgchp', m, eye).reshape(S5_GB, S5_UW, S5_XW)

    def blockdiag_out(m):
        m = m.reshape(S5_GB, S5_GPB, S5_GROUP, P_S5)
        return jnp.einsum('bgcp,gh->bgphc', m, eye).reshape(S5_GB, S5_XW, S5_UW)

    bb = jnp.concatenate([blockdiag_in(bb_re), blockdiag_in(bb_im)], axis=2).astype(BF16)
    cc = jnp.concatenate([blockdiag_out(c_re), -blockdiag_out(c_im)], axis=1).astype(BF16)
    lbr = lb_re.reshape(S5_GB, 1, S5_XW)
    lbi = lb_im.reshape(S5_GB, 1, S5_XW)
    return bb, cc, lbr, lbi


def _dispatch(ids, t_all, p_rows):
    e = ids[:, :2].reshape(-1)
    onehot = (e[:, None] == jnp.arange(N_EXPERTS, dtype=jnp.int32)[None, :]).astype(jnp.int32)
    csum = jnp.cumsum(onehot, axis=0)
    rank = jnp.sum((csum - onehot) * onehot, axis=1)
    counts = csum[-1]
    tiles = (counts + MOE_TILE - 1) // MOE_TILE
    tile_end = jnp.cumsum(tiles)
    tile_start = tile_end - tiles
    pos = jnp.sum(onehot * (tile_start * MOE_TILE)[None, :], axis=1) + rank
    tok = jnp.arange(2 * t_all, dtype=jnp.int32) // 2
    src = (jnp.arange(p_rows, dtype=jnp.int32) % t_all).at[pos].set(tok)
    n_tiles = tile_end[-1]
    tidx = jnp.arange(p_rows // MOE_TILE, dtype=jnp.int32)
    tclamp = jnp.minimum(tidx, n_tiles - 1)
    tile_expert = jnp.sum((tile_end[None, :] <= tclamp[:, None]).astype(jnp.int32), axis=1)
    return pos.reshape(t_all, 2), src, tile_expert, n_tiles.reshape(1).astype(jnp.int32)


def _pick_tile(n, candidates):
    for c in candidates:
        if n % c == 0:
            return c
    raise ValueError(f"no row tile for {n}")


def kernel(x_prompt, x_sample, state_mlstm_C, state_mlstm_n, state_mlstm_m, state_conv, state_s5_re,
           state_s5_im, norm_mix_g, w_in, b_i, b_f, w_conv, b_conv, head_norm_g, w_pa, s5_a_re, s5_a_im,
           s5_log_step, s5_b_re, s5_b_im, s5_c_re, s5_c_im, s5_d, s5_w_glu, s5_b_glu, w_pb, w_out,
           norm_ffn_g, w_rg, b_rg, w_rexp, b_rexp, w_gate, w_up, w_down, norm_final_g):
    assert state_mlstm_C.shape[0] == 1 and x_sample.shape[1] == 1
    batch, seq, _ = x_prompt.shape
    dec = x_sample.shape[0]
    t_p = batch * seq
    t_all = t_p + dec
    assert seq % CHUNK == 0 and t_p % dec == 0 and dec % LANES == 0
    d_a = H_A * DK
    tm_p = _pick_tile(t_p, (512, 256, 128))
    tm_all = _pick_tile(t_all, (640, 384, 128))
    sample_blk = t_p // dec

    xp = x_prompt.reshape(t_p, D_MODEL)
    xs = x_sample.reshape(dec, D_MODEL)

    g_mix = norm_mix_g[0]
    xn_all = _rmsnorm_rows(xp, g_mix, t_all, 0, tm_p)
    xn_all = _rmsnorm_rows(xs, g_mix, t_all, t_p, dec, alias=xn_all)
    w_in_t = w_in.reshape(w_in.shape[1:]).T
    n_qkvo = 4 * d_a
    tn = 1024
    qkvo = _matmul_t(xn_all, w_in_t, [j * tn for j in range(n_qkvo // tn)], jnp.zeros((1, n_qkvo), F32),
                     tm_all, tn, F32)
    n_gate_cols = 2 * H_A
    b_gates = jnp.pad(jnp.concatenate([b_i[0], b_f[0]]), (0, LANES - n_gate_cols)).reshape(1, LANES)
    gates = _matmul_t(xn_all, w_in_t, [n_qkvo], b_gates, tm_all, LANES, F32)
    c_u = n_qkvo + n_gate_cols
    c_ga = c_u + D_B
    starts = [c_ga + j * tn for j in range(2 * D_MODEL // tn)] + [c_u]
    proj2 = _matmul_t(xn_all, w_in_t, starts, jnp.zeros((1, len(starts) * tn), F32), tm_all, tn, F32)
    u_col0 = 2 * D_MODEL

    gates_t = gates[:t_p, :n_gate_cols].T
    head_g = head_norm_g[0].reshape(1, d_a)
    hn_all, c_p, n_p, m_p = _mlstm_prompt(qkvo, gates, gates_t, w_conv[0], b_conv[0].reshape(1, -1),
                                          head_g, batch, seq, t_all)
    conv_p = jnp.stack([qkvo[b * seq + seq - (CONV_W - 1):(b + 1) * seq, :2 * d_a] for b in range(batch)])

    bb, cc, lbr, lbi = _s5_params(s5_a_re[0], s5_a_im[0], s5_log_step[0], s5_b_re[0], s5_b_im[0],
                                  s5_c_re[0], s5_c_im[0])
    d_row = s5_d[0].reshape(1, D_B)
    ys_all, s5re_p, s5im_p = _s5_prompt(proj2, u_col0, bb, cc, lbr, lbi, d_row, batch, seq, t_all)

    conv_s_in = state_conv[0].reshape(dec, (CONV_W - 1) * 2 * d_a)
    q_s, kw_s, a_e, s_e, den_e, conv_s, n_s, m_s = _sample_pre(
        qkvo, conv_s_in, w_conv[0], b_conv[0].reshape(1, -1), gates, state_mlstm_m[0],
        state_mlstm_n[0].reshape(dec, d_a), dec, sample_blk)
    v_s = qkvo[t_p:, 2 * d_a:3 * d_a]
    r3 = lambda a: a.reshape(dec, H_A, DK)
    c_s, qc = _sample_c(r3(q_s), r3(kw_s), r3(v_s), r3(a_e), state_mlstm_C[0], dec)
    hn_all = _sample_post(qc.reshape(dec, d_a), s_e, a_e, den_e, qkvo, head_g, hn_all, dec, sample_blk)

    ys_all, s5re_s, s5im_s = _s5_sample(proj2, u_col0, state_s5_re[0].reshape(dec, -1),
                                        state_s5_im[0].reshape(dec, -1),
                                        bb, cc, lbr, lbi, d_row, ys_all, dec, sample_blk)

    wr = jnp.pad(jnp.concatenate([w_rg[0], w_rexp[0]], axis=1), ((0, 0), (0, LANES - N_GROUPS - N_EXPERTS)))
    wr_hi = wr.astype(BF16)
    wr_lo = (wr - wr_hi.astype(F32)).astype(BF16)
    b_r = jnp.pad(jnp.concatenate([b_rg[0], b_rexp[0]]), (0, LANES - N_GROUPS - N_EXPERTS)).reshape(1, LANES)
    merge_w = (w_pa[0].astype(BF16), s5_w_glu[0].astype(BF16), w_pb[0].astype(BF16), w_out[0].astype(BF16))
    b_glu = s5_b_glu[0].reshape(1, D_B)
    g_ffn = norm_ffn_g[0].reshape(1, D_MODEL)
    tm_m = _pick_tile(t_p, (256, 128))
    outs = _merge(xp, hn_all, ys_all, proj2, b_glu, g_ffn, wr_hi, wr_lo, b_r, *merge_w, t_all, 0, tm_m)
    x1_all, xn2_all, ids, wts = _merge(xs, hn_all, ys_all, proj2, b_glu, g_ffn, wr_hi, wr_lo, b_r,
                                       *merge_w, t_all, t_p, dec, aliases=outs)

    p_rows = -(-(2 * t_all + N_EXPERTS * (MOE_TILE - 1)) // MOE_TILE) * MOE_TILE
    pos, src, tile_expert, n_tiles = _dispatch(ids, t_all, p_rows)
    take_rows = lambda a, idx: a.at[idx].get(mode='promise_in_bounds')
    xs_sorted = take_rows(xn2_all, src)
    yp = _moe_experts(tile_expert, n_tiles, xs_sorted, w_gate[0], w_up[0], w_down[0])
    yg0 = take_rows(yp, pos[:, 0])
    yg1 = take_rows(yp, pos[:, 1])

    g_fin = norm_final_g.reshape(1, D_MODEL)
    y_prompt = _final(x1_all, yg0, yg1, wts, g_fin, 0, t_p, tm_p).reshape(batch, seq, D_MODEL)
    y_sample = _final(x1_all, yg0, yg1, wts, g_fin, t_p, dec, dec).reshape(dec, 1, D_MODEL)

    lead = lambda a, shape: a.reshape((1,) + shape)
    return (y_prompt, y_sample,
            lead(c_p, (batch, H_A, DK, DK)), lead(n_p, (batch, H_A, DK)), lead(m_p[:, 0, :H_A], (batch, H_A)),
            lead(conv_p, (batch, CONV_W - 1, 2 * d_a)),
            lead(s5re_p, (batch, G_B, P_S5)), lead(s5im_p, (batch, G_B, P_S5)),
            lead(c_s, (dec, H_A, DK, DK)), lead(n_s, (dec, H_A, DK)), lead(m_s[:, :H_A], (dec, H_A)),
            lead(conv_s, (dec, CONV_W - 1, 2 * d_a)),
            lead(s5re_s, (dec, G_B, P_S5)), lead(s5im_s, (dec, G_B, P_S5)))
```

```python
import functools
import math

import jax
import jax.numpy as jnp
from jax import lax
from jax.experimental import pallas as pl
from jax.experimental.pallas import tpu as pltpu

F32 = jnp.float32
BF16 = jnp.bfloat16

D_MODEL = 2048
H_A = 8
DK = 256
CONV_W = 4
CHUNK = 128
D_B = 1024
S5_GROUP = 16
G_B = 64
P_S5 = 64
N_GROUPS = 4
EXP_PER_GROUP = 8
N_EXPERTS = 32
D_EXPERT = 512
EPS = 1e-6

LANES = 128
SUBLANES = 8
VMEM_LIMIT = 56 * 1024 * 1024

S5_GB = 4
S5_GPB = G_B // S5_GB
S5_UW = S5_GPB * S5_GROUP
S5_XW = S5_GPB * P_S5
S5_CHUNK = 8
S5_CST = (LANES // S5_GROUP) * P_S5
MOE_TILE = 256


def _cparams(sem):
    return pltpu.CompilerParams(dimension_semantics=sem, vmem_limit_bytes=VMEM_LIMIT)


def _silu(x):
    return x * (1.0 / (1.0 + jnp.exp(-x)))


def _sigmoid(x):
    return 1.0 / (1.0 + jnp.exp(-x))


def _log_sigmoid(x):
    return jnp.minimum(x, 0.0) - jnp.log1p(jnp.exp(-jnp.abs(x)))


def _gelu_tanh(x):
    c = math.sqrt(2.0 / math.pi)
    return 0.5 * x * (1.0 + jnp.tanh(c * (x + 0.044715 * (x * x * x))))


def _split3(x):
    hi = x.astype(BF16)
    r = x - hi.astype(F32)
    mid = r.astype(BF16)
    lo = (r - mid.astype(F32)).astype(BF16)
    return hi, mid, lo


def _dot(a, b):
    return jnp.dot(a, b, preferred_element_type=F32)


def _dot_nt(a, b):
    return lax.dot_general(a, b, (((1,), (1,)), ((), ())), preferred_element_type=F32)


def _dot_tn(a, b):
    return lax.dot_general(a, b, (((0,), (0,)), ((), ())), preferred_element_type=F32)


def _rmsnorm_kernel(x_ref, g_ref, *rest):
    o_ref = rest[-1]
    x = x_ref[...]
    r = lax.rsqrt(jnp.mean(x * x, axis=-1, keepdims=True) + EPS)
    o_ref[...] = (x * r * g_ref[...]).astype(o_ref.dtype)


def _rmsnorm_rows(x, g, t_all, row0, tm, alias=None):
    n = x.shape[0]
    blk0 = row0 // tm
    in_specs = [pl.BlockSpec((tm, D_MODEL), lambda i: (i, 0)),
                pl.BlockSpec((1, D_MODEL), lambda i: (0, 0))]
    args = [x, g.reshape(1, D_MODEL)]
    aliases = {}
    if alias is not None:
        in_specs.append(pl.BlockSpec(memory_space=pl.ANY))
        args.append(alias)
        aliases = {2: 0}
    return pl.pallas_call(
        _rmsnorm_kernel,
        out_shape=jax.ShapeDtypeStruct((t_all, D_MODEL), BF16),
        grid=(n // tm,),
        in_specs=in_specs,
        out_specs=pl.BlockSpec((tm, D_MODEL), lambda i: (i + blk0, 0)),
        input_output_aliases=aliases,
        compiler_params=_cparams(("arbitrary",)),
        name="rmsnorm_rows",
    )(*args)


def _mm_kernel(a_ref, w_ref, b_ref, o_ref, wb_ref):
    @pl.when(pl.program_id(1) == 0)
    def _():
        wb_ref[...] = w_ref[...].astype(BF16)

    o_ref[...] = (_dot(a_ref[...], wb_ref[...]) + b_ref[...]).astype(o_ref.dtype)


def _matmul(a, w, bias, n_out, tm, tn, out_dtype, col_blk0=0):
    m, k = a.shape
    return pl.pallas_call(
        _mm_kernel,
        out_shape=jax.ShapeDtypeStruct((m, n_out), out_dtype),
        grid=(n_out // tn, m // tm),
        in_specs=[pl.BlockSpec((tm, k), lambda j, i: (i, 0)),
                  pl.BlockSpec((k, tn), lambda j, i: (0, j + col_blk0)),
                  pl.BlockSpec((1, tn), lambda j, i: (0, j))],
        out_specs=pl.BlockSpec((tm, tn), lambda j, i: (i, j)),
        scratch_shapes=[pltpu.VMEM((k, tn), BF16)],
        compiler_params=_cparams(("arbitrary", "arbitrary")),
        name="rows_matmul",
    )(a, w, bias)


def _mm_t_kernel(starts_ref, a_ref, wt_ref, b_ref, o_ref, wb_ref):
    del starts_ref

    @pl.when(pl.program_id(1) == 0)
    def _():
        wb_ref[...] = wt_ref[...].astype(BF16)

    o_ref[...] = (_dot_nt(a_ref[...], wb_ref[...]) + b_ref[...]).astype(o_ref.dtype)


def _matmul_t(a, w_t, row_starts, bias, tm, tn, out_dtype):
    m, k = a.shape
    n_t = len(row_starts)
    assert all(s % SUBLANES == 0 for s in row_starts)
    grid_spec = pltpu.PrefetchScalarGridSpec(
        num_scalar_prefetch=1,
        grid=(n_t, m // tm),
        in_specs=[pl.BlockSpec((tm, k), lambda j, i, st: (i, 0)),
                  pl.BlockSpec((pl.Element(tn), pl.Element(k)), lambda j, i, st: (st[j] * SUBLANES, 0)),
                  pl.BlockSpec((1, tn), lambda j, i, st: (0, j))],
        out_specs=pl.BlockSpec((tm, tn), lambda j, i, st: (i, j)),
        scratch_shapes=[pltpu.VMEM((tn, k), BF16)],
    )
    return pl.pallas_call(
        _mm_t_kernel,
        out_shape=jax.ShapeDtypeStruct((m, n_t * tn), out_dtype),
        grid_spec=grid_spec,
        compiler_params=_cparams(("arbitrary", "arbitrary")),
        name="rows_matmul_t",
    )(jnp.asarray([s // SUBLANES for s in row_starts], jnp.int32), a, w_t, bias)


def _mm_shift_kernel(a_ref, wlo_ref, whi_ref, o_ref, wb_ref, *, shift):
    tn = wlo_ref.shape[1]
    rows = 256

    @pl.when(pl.program_id(1) == 0)
    def _():
        for r in range(wlo_ref.shape[0] // rows):
            sl = slice(r * rows, (r + 1) * rows)
            w = jnp.concatenate([wlo_ref[sl, :], whi_ref[sl, :]], axis=1)
            wb_ref[sl, :] = w[:, shift:shift + tn].astype(BF16)

    o_ref[...] = _dot(a_ref[...], wb_ref[...]).astype(o_ref.dtype)


def _matmul_shifted(a, w, col_starts, tm, tn, out_dtype):
    m, k = a.shape
    shift = col_starts[0] % tn
    assert all(c % tn == shift for c in col_starts) and 0 < shift < LANES
    lo_blk = [c // tn for c in col_starts]
    n_t = len(col_starts)
    base, first = min(lo_blk), lo_blk[0]
    assert all(lo_blk[j] == base + (j + first - base) % n_t for j in range(n_t))
    lo_idx = lambda j: base + (j + (first - base)) % n_t
    per = tn // LANES
    kern = functools.partial(_mm_shift_kernel, shift=shift)
    return pl.pallas_call(
        kern,
        out_shape=jax.ShapeDtypeStruct((m, n_t * tn), out_dtype),
        grid=(n_t, m // tm),
        in_specs=[pl.BlockSpec((tm, k), lambda j, i: (i, 0)),
                  pl.BlockSpec((k, tn), lambda j, i: (0, lo_idx(j))),
                  pl.BlockSpec((k, LANES), lambda j, i: (0, (lo_idx(j) + 1) * per))],
        out_specs=pl.BlockSpec((tm, tn), lambda j, i: (i, j)),
        scratch_shapes=[pltpu.VMEM((k, tn), BF16)],
        compiler_params=_cparams(("arbitrary", "arbitrary")),
        name="rows_matmul_shifted",
    )(a, w, w)


def _mlstm_kernel(qk_ref, v_ref, o_ref, gcol_ref, grow_ref, wc_ref, bc_ref, hg_ref,
                  h_ref, c_out, n_out, m_out, c_sc, n_sc, m_sc, ext_sc):
    c = pl.program_id(1)
    L = CHUNK
    pad = SUBLANES
    d_a = H_A * DK

    @pl.when(c == 0)
    def _():
        c_sc[...] = jnp.zeros_like(c_sc)
        n_sc[...] = jnp.zeros_like(n_sc)
        m_sc[...] = jnp.zeros_like(m_sc)
        ext_sc[0:pad, :] = jnp.zeros((pad, 2 * d_a), F32)

    x = qk_ref[...]
    ext_sc[pad:pad + L, :] = x
    y = bc_ref[...] + wc_ref[CONV_W - 1:CONV_W, :] * x
    for j in range(1, CONV_W):
        y = y + wc_ref[CONV_W - 1 - j:CONV_W - j, :] * ext_sc[pad - j:pad - j + L, :]
    ext_sc[0:pad, :] = x[L - pad:L, :]
    y = _silu(y)

    gcol = gcol_ref[...]
    grow = grow_ref[...]
    ri = lax.broadcasted_iota(jnp.int32, (L, L), 0)
    ci = lax.broadcasted_iota(jnp.int32, (L, L), 1)
    causal = ci <= ri
    tril = jnp.where(causal, 1.0, 0.0).astype(BF16)
    triu = jnp.where(ri <= ci, 1.0, 0.0).astype(BF16)
    b_cols = sum(_dot(tril, p) for p in _split3(_log_sigmoid(gcol)))
    b_rows = sum(_dot(p, triu) for p in _split3(_log_sigmoid(grow)))

    for h in range(H_A):
        sl = slice(h * DK, (h + 1) * DK)
        q = y[:, sl] * (DK ** -0.5)
        k = y[:, d_a + h * DK:d_a + (h + 1) * DK]
        ig_col = gcol[:, h:h + 1]
        ig_row = grow[h:h + 1, :]
        b_col = b_cols[:, H_A + h:H_A + h + 1]
        b_row = b_rows[H_A + h:H_A + h + 1, :]

        m_prev = m_sc[:, h:h + 1]
        d_log = jnp.where(causal, b_col - b_row + ig_row, -jnp.inf)
        inter_log = b_col + m_prev
        m_t = jnp.maximum(inter_log, jnp.max(d_log, axis=1, keepdims=True))
        qb = q.astype(BF16)
        kb = k.astype(BF16)
        vb = v_ref[:, sl].astype(BF16)
        s = _dot_nt(qb, kb) * jnp.exp(d_log - m_t)
        inter_w = jnp.exp(inter_log - m_t)
        c_prev = c_sc[h]
        n_prev = n_sc[h:h + 1, :]
        num = _dot(s.astype(BF16), vb) + inter_w * _dot(qb, c_prev.astype(BF16))
        nq = jnp.sum(s, axis=1, keepdims=True) + inter_w * jnp.sum(q * n_prev, axis=1, keepdims=True)
        den = jnp.maximum(jnp.abs(nq), jnp.exp(-m_t))
        hh = num / den
        hh = hh * _sigmoid(o_ref[:, sl])
        hh = hh * lax.rsqrt(jnp.mean(hh * hh, axis=1, keepdims=True) + EPS)
        h_ref[:, sl] = (hh * hg_ref[:, sl]).astype(h_ref.dtype)

        m_new = m_t[L - 1:L, :]
        b_last = b_col[L - 1:L, :]
        decay = jnp.exp(b_last + m_prev - m_new)
        w_end = jnp.exp(b_last - b_col + ig_col - m_new)
        kw = k * w_end
        c_sc[h] = decay * c_prev + _dot_tn(kw.astype(BF16), vb)
        n_sc[h:h + 1, :] = decay * n_prev + jnp.sum(kw, axis=0, keepdims=True)
        m_sc[:, h:h + 1] = m_new

    @pl.when(c == pl.num_programs(1) - 1)
    def _():
        c_out[...] = c_sc[...]
        n_out[...] = n_sc[...]
        m_out[...] = m_sc[...]


def _mlstm_prompt(qkvo, gates, gates_t, w_conv, b_conv, head_g, batch, seq, t_all):
    nc = seq // CHUNK
    L = CHUNK
    d_a = H_A * DK
    row = lambda b, c: b * nc + c
    in_specs = [
        pl.BlockSpec((L, 2 * d_a), lambda b, c: (row(b, c), 0)),
        pl.BlockSpec((L, d_a), lambda b, c: (row(b, c), 2)),
        pl.BlockSpec((L, d_a), lambda b, c: (row(b, c), 3)),
        pl.BlockSpec((L, LANES), lambda b, c: (row(b, c), 0)),
        pl.BlockSpec((2 * H_A, L), lambda b, c: (0, row(b, c))),
        pl.BlockSpec((CONV_W, 2 * d_a), lambda b, c: (0, 0)),
        pl.BlockSpec((1, 2 * d_a), lambda b, c: (0, 0)),
        pl.BlockSpec((1, d_a), lambda b, c: (0, 0)),
    ]
    out_shape = (
        jax.ShapeDtypeStruct((t_all, d_a), BF16),
        jax.ShapeDtypeStruct((batch, H_A, DK, DK), F32),
        jax.ShapeDtypeStruct((batch, H_A, DK), F32),
        jax.ShapeDtypeStruct((batch, 1, LANES), F32),
    )
    out_specs = (
        pl.BlockSpec((L, d_a), lambda b, c: (row(b, c), 0)),
        pl.BlockSpec((None, H_A, DK, DK), lambda b, c: (b, 0, 0, 0)),
        pl.BlockSpec((None, H_A, DK), lambda b, c: (b, 0, 0)),
        pl.BlockSpec((None, 1, LANES), lambda b, c: (b, 0, 0)),
    )
    return pl.pallas_call(
        _mlstm_kernel,
        out_shape=out_shape,
        grid=(batch, nc),
        in_specs=in_specs,
        out_specs=out_specs,
        scratch_shapes=[pltpu.VMEM((H_A, DK, DK), F32), pltpu.VMEM((H_A, DK), F32),
                        pltpu.VMEM((1, LANES), F32), pltpu.VMEM((SUBLANES + L, 2 * d_a), F32)],
        compiler_params=_cparams(("arbitrary", "arbitrary")),
        name="mlstm_prompt",
    )(qkvo, qkvo, qkvo, gates, gates_t, w_conv, b_conv, head_g)


def _s5_prompt_kernel(u_ref, winc_ref, wintra_ref, wout_ref, lbr_ref, lbi_ref, d_ref,
                      ys_ref, sre_ref, sim_ref, x_sc, *, batch, nchunk):
    L = S5_CHUNK
    nrow = batch * nchunk
    nst = S5_CST // LANES
    rstr = nchunk + SUBLANES
    u_t = [u_ref[pl.ds(t, nrow, stride=L), :] for t in range(L)]
    lhs = jnp.concatenate([a.astype(BF16) for a in u_t], axis=1)
    inc = _dot(lhs, winc_ref[...])
    for j in range(2 * nst):
        for b in range(batch):
            x_sc[j, b * rstr:b * rstr + nchunk, :] = inc[b * nchunk:(b + 1) * nchunk, j * LANES:(j + 1) * LANES]

    lbr = [jnp.broadcast_to(lbr_ref[:, j * LANES:(j + 1) * LANES], (batch, LANES)) for j in range(nst)]
    lbi = [jnp.broadcast_to(lbi_ref[:, j * LANES:(j + 1) * LANES], (batch, LANES)) for j in range(nst)]

    def scan_body(r, carry):
        rows = pl.ds(r, batch, stride=rstr)
        out = []
        for j in range(nst):
            xr, xi = carry[j]
            ir = x_sc[j, rows, :]
            ii = x_sc[nst + j, rows, :]
            x_sc[j, rows, :] = xr
            x_sc[nst + j, rows, :] = xi
            out.append((lbr[j] * xr - lbi[j] * xi + ir, lbr[j] * xi + lbi[j] * xr + ii))
        return tuple(out)

    z = jnp.zeros((batch, LANES), F32)
    fin = lax.fori_loop(0, nchunk, scan_body, tuple((z, z) for _ in range(nst)))
    for j in range(nst):
        sre_ref[:, j * LANES:(j + 1) * LANES] = fin[j][0]
        sim_ref[:, j * LANES:(j + 1) * LANES] = fin[j][1]

    xprev = jnp.concatenate(
        [jnp.concatenate([x_sc[j, b * rstr:b * rstr + nchunk, :] for b in range(batch)], axis=0)
         for j in range(2 * nst)], axis=1).astype(BF16)
    y = _dot(lhs, wintra_ref[...]) + _dot(xprev, wout_ref[...])
    for t in range(L):
        yt = y[:, t * LANES:(t + 1) * LANES] + d_ref[...] * u_t[t]
        ys_ref[pl.ds(t, nrow, stride=L), :] = _gelu_tanh(yt).astype(ys_ref.dtype)


def _s5_prompt(proj2, u_col0, winc, wintra, wout, lb8r, lb8i, d_row, batch, seq, t_all):
    nchunk = seq // S5_CHUNK
    t_p = batch * seq
    kern = functools.partial(_s5_prompt_kernel, batch=batch, nchunk=nchunk)
    ub0 = u_col0 // LANES
    ncb = D_B // LANES
    kw = S5_CHUNK * LANES
    return pl.pallas_call(
        kern,
        out_shape=(jax.ShapeDtypeStruct((t_all, D_B), F32),
                   jax.ShapeDtypeStruct((batch, G_B * P_S5), F32),
                   jax.ShapeDtypeStruct((batch, G_B * P_S5), F32)),
        grid=(ncb,),
        in_specs=[pl.BlockSpec((t_p, LANES), lambda g: (0, ub0 + g)),
                  pl.BlockSpec((None, kw, 2 * S5_CST), lambda g: (g, 0, 0)),
                  pl.BlockSpec((None, kw, kw), lambda g: (g, 0, 0)),
                  pl.BlockSpec((None, 2 * S5_CST, kw), lambda g: (g, 0, 0)),
                  pl.BlockSpec((None, 1, S5_CST), lambda g: (g, 0, 0)),
                  pl.BlockSpec((None, 1, S5_CST), lambda g: (g, 0, 0)),
                  pl.BlockSpec((1, LANES), lambda g: (0, g))],
        out_specs=(pl.BlockSpec((t_p, LANES), lambda g: (0, g)),
                   pl.BlockSpec((batch, S5_CST), lambda g: (0, g)),
                   pl.BlockSpec((batch, S5_CST), lambda g: (0, g))),
        scratch_shapes=[pltpu.VMEM((2 * S5_CST // LANES, batch * (nchunk + SUBLANES), LANES), F32)],
        compiler_params=_cparams(("arbitrary",)),
        name="s5_prompt",
    )(proj2, winc, wintra, wout, lb8r, lb8i, d_row)


def _sample_pre_kernel(qk_ref, conv_ref, wc_ref, bc_ref, g_ref, m_ref, n_ref,
                       q_out, kw_out, a_out, s_out, den_out, conv_out, n_out, m_out):
    c2 = 2 * H_A * DK
    x_new = qk_ref[...]
    y = bc_ref[...] + wc_ref[CONV_W - 1:CONV_W, :] * x_new
    for j in range(CONV_W - 1):
        y = y + wc_ref[j:j + 1, :] * conv_ref[:, j * c2:(j + 1) * c2]
    y = _silu(y)
    conv_out[:, 0:(CONV_W - 2) * c2] = conv_ref[:, c2:(CONV_W - 1) * c2]
    conv_out[:, (CONV_W - 2) * c2:(CONV_W - 1) * c2] = x_new
    g = g_ref[...]
    bd = x_new.shape[0]
    m_cols = []
    for h in range(H_A):
        sl = slice(h * DK, (h + 1) * DK)
        q = y[:, sl] * (DK ** -0.5)
        k = y[:, H_A * DK + h * DK:H_A * DK + (h + 1) * DK]
        ig = g[:, h:h + 1]
        lf = _log_sigmoid(g[:, H_A + h:H_A + h + 1])
        m_prev = m_ref[:, h:h + 1]
        m_t = jnp.maximum(lf + m_prev, ig)
        a = jnp.exp(lf + m_prev - m_t)
        wgt = jnp.exp(ig - m_t)
        n_prev = n_ref[:, sl]
        s = jnp.sum(q * k, axis=1, keepdims=True) * wgt
        nq = s + a * jnp.sum(q * n_prev, axis=1, keepdims=True)
        den = jnp.maximum(jnp.abs(nq), jnp.exp(-m_t))
        kw = wgt * k
        q_out[:, sl] = q
        kw_out[:, sl] = kw
        a_out[:, sl] = jnp.broadcast_to(a, (bd, DK))
        s_out[:, sl] = jnp.broadcast_to(s, (bd, DK))
        den_out[:, sl] = jnp.broadcast_to(den, (bd, DK))
        n_out[:, sl] = a * n_prev + kw
        m_cols.append(m_t)
    lane = lax.broadcasted_iota(jnp.int32, (bd, LANES), 1)
    m_full = jnp.zeros((bd, LANES), F32)
    for h in range(H_A):
        m_full = jnp.where(lane == h, m_cols[h], m_full)
    m_out[...] = m_full


def _sample_pre(qkvo, conv_state, w_conv, b_conv, gates, m_state, n_state, dec, row_blk):
    c2 = 2 * H_A * DK
    d = H_A * DK
    full = lambda shape: pl.BlockSpec(shape, lambda i: (0,) * len(shape))
    rows = lambda: jax.ShapeDtypeStruct((dec, d), F32)
    return pl.pallas_call(
        _sample_pre_kernel,
        out_shape=(rows(), rows(), rows(), rows(), rows(),
                   jax.ShapeDtypeStruct((dec, (CONV_W - 1) * c2), F32), rows(),
                   jax.ShapeDtypeStruct((dec, LANES), F32)),
        grid=(1,),
        in_specs=[pl.BlockSpec((dec, c2), lambda i: (row_blk, 0)),
                  full((dec, (CONV_W - 1) * c2)), full((CONV_W, c2)), full((1, c2)),
                  pl.BlockSpec((dec, LANES), lambda i: (row_blk, 0)),
                  full((dec, H_A)), full((dec, d))],
        out_specs=(full((dec, d)), full((dec, d)), full((dec, d)), full((dec, d)), full((dec, d)),
                   full((dec, (CONV_W - 1) * c2)), full((dec, d)), full((dec, LANES))),
        compiler_params=_cparams(("arbitrary",)),
        name="sample_pre",
    )(qkvo, conv_state, w_conv, b_conv, gates, m_state, n_state)


def _sample_c_kernel(q_ref, kw_ref, v_ref, a_ref, c_ref, c_out, qc_out):
    eye = jnp.where(lax.broadcasted_iota(jnp.int32, (DK, DK), 0)
                    == lax.broadcasted_iota(jnp.int32, (DK, DK), 1), 1.0, 0.0).astype(BF16)
    for b in range(q_ref.shape[0]):
        q_t = sum(_dot_nt(eye, p) for p in _split3(q_ref[b]))
        kw_t = sum(_dot_nt(eye, p) for p in _split3(kw_ref[b]))
        qc_rows = []
        for h in range(H_A):
            c_prev = c_ref[b, h]
            qc_rows.append(jnp.sum(q_t[:, h:h + 1] * c_prev, axis=0, keepdims=True))
            c_out[b, h] = a_ref[b, h:h + 1, :] * c_prev + kw_t[:, h:h + 1] * v_ref[b, h:h + 1, :]
        qc_out[b] = jnp.concatenate(qc_rows, axis=0)


SAMPLE_C_BATCH = 4


def _sample_c(q3, kw3, v3, a3, c_state, dec):
    nb = SAMPLE_C_BATCH
    vec = pl.BlockSpec((nb, H_A, DK), lambda b: (b, 0, 0))
    mat = pl.BlockSpec((nb, H_A, DK, DK), lambda b: (b, 0, 0, 0))
    return pl.pallas_call(
        _sample_c_kernel,
        out_shape=(jax.ShapeDtypeStruct((dec, H_A, DK, DK), F32),
                   jax.ShapeDtypeStruct((dec, H_A, DK), F32)),
        grid=(dec // nb,),
        in_specs=[vec, vec, vec, vec, mat],
        out_specs=(mat, vec),
        compiler_params=_cparams(("arbitrary",)),
        name="sample_c_update",
    )(q3, kw3, v3, a3, c_state)


def _sample_post_kernel(qc_ref, s_ref, a_ref, den_ref, vo_ref, hg_ref, hn_in, h_ref):
    del hn_in
    d = H_A * DK
    num = s_ref[...] * vo_ref[:, 0:d] + a_ref[...] * qc_ref[...]
    hh = num / den_ref[...]
    hh = hh * _sigmoid(vo_ref[:, d:2 * d])
    for h in range(H_A):
        sl = slice(h * DK, (h + 1) * DK)
        seg = hh[:, sl]
        seg = seg * lax.rsqrt(jnp.mean(seg * seg, axis=1, keepdims=True) + EPS)
        h_ref[:, sl] = (seg * hg_ref[:, sl]).astype(h_ref.dtype)


def _sample_post(qc, s_e, a_e, den_e, qkvo, head_g, hn_all, dec, row_blk):
    d = H_A * DK
    full = lambda shape: pl.BlockSpec(shape, lambda i: (0,) * len(shape))
    return pl.pallas_call(
        _sample_post_kernel,
        out_shape=jax.ShapeDtypeStruct(hn_all.shape, hn_all.dtype),
        grid=(1,),
        in_specs=[full((dec, d)), full((dec, d)), full((dec, d)), full((dec, d)),
                  pl.BlockSpec((dec, 2 * d), lambda i: (row_blk, 1)),
                  full((1, d)), pl.BlockSpec(memory_space=pl.ANY)],
        out_specs=pl.BlockSpec((dec, d), lambda i: (row_blk, 0)),
        input_output_aliases={6: 0},
        compiler_params=_cparams(("arbitrary",)),
        name="sample_post",
    )(qc, s_e, a_e, den_e, qkvo, head_g, hn_all)


def _s5_sample_kernel(u_ref, sr_ref, si_ref, bb_ref, cc_ref, lbr_ref, lbi_ref, d_ref, ys_in,
                      ys_ref, sre_out, sim_out):
    del ys_in
    for g in range(S5_GB):
        u = u_ref[:, g * S5_UW:(g + 1) * S5_UW]
        bu = _dot(u.astype(BF16), bb_ref[g])
        sl = slice(g * S5_XW, (g + 1) * S5_XW)
        lbr = lbr_ref[g]
        lbi = lbi_ref[g]
        sr = sr_ref[:, sl]
        si = si_ref[:, sl]
        xr = lbr * sr - lbi * si + bu[:, 0:S5_XW]
        xi = lbr * si + lbi * sr + bu[:, S5_XW:2 * S5_XW]
        sre_out[:, sl] = xr
        sim_out[:, sl] = xi
        x = jnp.concatenate([xr, xi], axis=1).astype(BF16)
        y = _dot(x, cc_ref[g]) + d_ref[:, g * S5_UW:(g + 1) * S5_UW] * u
        ys_ref[:, g * S5_UW:(g + 1) * S5_UW] = _gelu_tanh(y).astype(ys_ref.dtype)


def _s5_sample(proj2, u_col0, s_re, s_im, bb, cc, lbr, lbi, d_row, ys_all, dec, row_blk):
    full = lambda shape: pl.BlockSpec(shape, lambda i: (0,) * len(shape))
    n_state = G_B * P_S5
    ub0 = u_col0 // D_B
    return pl.pallas_call(
        _s5_sample_kernel,
        out_shape=(jax.ShapeDtypeStruct(ys_all.shape, ys_all.dtype),
                   jax.ShapeDtypeStruct((dec, n_state), F32),
                   jax.ShapeDtypeStruct((dec, n_state), F32)),
        grid=(1,),
        in_specs=[pl.BlockSpec((dec, D_B), lambda i: (row_blk, ub0)),
                  full((dec, n_state)), full((dec, n_state)),
                  full(bb.shape), full(cc.shape), full(lbr.shape), full(lbi.shape), full((1, D_B)),
                  pl.BlockSpec(memory_space=pl.ANY)],
        out_specs=(pl.BlockSpec((dec, D_B), lambda i: (row_blk, 0)),
                   full((dec, n_state)), full((dec, n_state))),
        input_output_aliases={8: 0},
        compiler_params=_cparams(("arbitrary",)),
        name="s5_sample",
    )(proj2, s_re, s_im, bb, cc, lbr, lbi, d_row, ys_all)


def _merge_kernel(x_ref, hn_ref, ys_ref, ga_ref, gb_ref, bglu_ref, gffn_ref, wrh_ref, wrl_ref, br_ref,
                  wpa_hbm, wglu_hbm, wpb_hbm, wout_hbm, *rest):
    n_alias = len(rest) - 9
    x1_ref, xn_ref, ids_ref, wts_ref = rest[n_alias:n_alias + 4]
    wpa, wglu, wpb, wout, sem = rest[n_alias + 4:]

    @pl.when(pl.program_id(0) == 0)
    def _():
        copies = [pltpu.make_async_copy(src, dst, sem.at[i])
                  for i, (src, dst) in enumerate(((wpa_hbm, wpa), (wglu_hbm, wglu),
                                                  (wpb_hbm, wpb), (wout_hbm, wout)))]
        for cp in copies:
            cp.start()
        for cp in copies:
            cp.wait()

    ya = _dot(hn_ref[...], wpa[...])
    ys = ys_ref[...]
    gate = _sigmoid(_dot(ys.astype(BF16), wglu[...]) + bglu_ref[...])
    yb = _dot((ys * gate).astype(BF16), wpb[...])
    z = _sigmoid(ga_ref[...]) * ya + _sigmoid(gb_ref[...]) * yb
    x1 = x_ref[...] + _dot(z.astype(BF16), wout[...])
    x1_ref[...] = x1
    xn = x1 * lax.rsqrt(jnp.mean(x1 * x1, axis=1, keepdims=True) + EPS) * gffn_ref[...]
    xn_ref[...] = xn
    xh = xn.astype(BF16)
    xl = (xn - xh.astype(F32)).astype(BF16)
    logits = _dot(xh, wrh_ref[...]) + _dot(xl, wrh_ref[...]) + _dot(xh, wrl_ref[...]) + br_ref[...]

    lane_i = lax.broadcasted_iota(jnp.int32, logits.shape, 1)
    lane = lane_i.astype(F32)
    neg = -jnp.inf
    big = float(1 << 20)
    gl = jnp.where(lane_i < N_GROUPS, logits, neg)
    gmax = jnp.max(gl, axis=1, keepdims=True)
    gsum = jnp.sum(jnp.exp(gl - gmax), axis=1, keepdims=True)
    gidx = jnp.min(jnp.where(gl == gmax, lane, big), axis=1, keepdims=True)
    pg_sel = 1.0 / gsum
    lo = N_GROUPS + gidx * EXP_PER_GROUP
    in_grp = (lane >= lo) & (lane < lo + EXP_PER_GROUP)
    el = jnp.where(in_grp, logits, neg)
    emax = jnp.max(el, axis=1, keepdims=True)
    ee = jnp.exp(el - emax)
    pe = ee / jnp.sum(ee, axis=1, keepdims=True)
    v0 = jnp.max(pe, axis=1, keepdims=True)
    i0 = jnp.min(jnp.where(in_grp & (pe == v0), lane, big), axis=1, keepdims=True)
    rest_m = in_grp & (lane != i0)
    pe1 = jnp.where(rest_m, pe, neg)
    v1 = jnp.max(pe1, axis=1, keepdims=True)
    i1 = jnp.min(jnp.where(rest_m & (pe1 == v1), lane, big), axis=1, keepdims=True)
    tot = v0 + v1
    w0 = pg_sel * (v0 / tot)
    w1 = pg_sel * (v1 / tot)
    ids = jnp.where(lane_i == 0, i0 - N_GROUPS, jnp.where(lane_i == 1, i1 - N_GROUPS, 0.0))
    ids_ref[...] = ids.astype(jnp.int32)
    wts_ref[...] = jnp.where(lane_i == 0, w0, jnp.where(lane_i == 1, w1, 0.0))


def _merge(x, hn_all, ys_all, proj2, b_glu, g_ffn, wr_hi, wr_lo, b_r, wpa, wglu, wpb, wout,
           t_all, row0, tm, aliases=None):
    n = x.shape[0]
    blk0 = row0 // tm
    const = lambda shape: pl.BlockSpec(shape, lambda i: (0,) * len(shape))
    any_spec = pl.BlockSpec(memory_space=pl.ANY)
    in_specs = [pl.BlockSpec((tm, D_MODEL), lambda i: (i, 0)),
                pl.BlockSpec((tm, D_MODEL), lambda i: (i + blk0, 0)),
                pl.BlockSpec((tm, D_B), lambda i: (i + blk0, 0)),
                pl.BlockSpec((tm, D_MODEL), lambda i: (i + blk0, 0)),
                pl.BlockSpec((tm, D_MODEL), lambda i: (i + blk0, 1)),
                const((1, D_B)), const((1, D_MODEL)),
                const((D_MODEL, LANES)), const((D_MODEL, LANES)), const((1, LANES)),
                any_spec, any_spec, any_spec, any_spec]
    args = [x, hn_all, ys_all, proj2, proj2, b_glu, g_ffn, wr_hi, wr_lo, b_r, wpa, wglu, wpb, wout]
    io_alias = {}
    if aliases is not None:
        for j, a in enumerate(aliases):
            in_specs.append(any_spec)
            args.append(a)
            io_alias[14 + j] = j
    out_shape = (jax.ShapeDtypeStruct((t_all, D_MODEL), F32),
                 jax.ShapeDtypeStruct((t_all, D_MODEL), F32),
                 jax.ShapeDtypeStruct((t_all, LANES), jnp.int32),
                 jax.ShapeDtypeStruct((t_all, LANES), F32))
    out_specs = (pl.BlockSpec((tm, D_MODEL), lambda i: (i + blk0, 0)),
                 pl.BlockSpec((tm, D_MODEL), lambda i: (i + blk0, 0)),
                 pl.BlockSpec((tm, LANES), lambda i: (i + blk0, 0)),
                 pl.BlockSpec((tm, LANES), lambda i: (i + blk0, 0)))
    return pl.pallas_call(
        _merge_kernel,
        out_shape=out_shape,
        grid=(n // tm,),
        in_specs=in_specs,
        out_specs=out_specs,
        scratch_shapes=[pltpu.VMEM(wpa.shape, BF16), pltpu.VMEM(wglu.shape, BF16),
                        pltpu.VMEM(wpb.shape, BF16), pltpu.VMEM(wout.shape, BF16),
                        pltpu.SemaphoreType.DMA((4,))],
        input_output_aliases=io_alias,
        compiler_params=_cparams(("arbitrary",)),
        name="merge_router",
    )(*args)


def _moe_kernel(te_ref, nt_ref, xs_ref, wg_ref, wu_ref, wd_ref, o_ref, wg_sc, wu_sc, wd_sc):
    i = pl.program_id(0)
    e = te_ref[i]
    prev = te_ref[jnp.maximum(i - 1, 0)]

    @pl.when((i == 0) | (e != prev))
    def _():
        wg_sc[...] = wg_ref[...].astype(BF16)
        wu_sc[...] = wu_ref[...].astype(BF16)
        wd_sc[...] = wd_ref[...].astype(BF16)

    @pl.when(i < nt_ref[0])
    def _():
        x = xs_ref[...].astype(BF16)
        hg = _dot(x, wg_sc[...])
        hu = _dot(x, wu_sc[...])
        hh = (_silu(hg) * hu).astype(BF16)
        o_ref[...] = _dot(hh, wd_sc[...])


def _moe_experts(tile_expert, n_tiles, xs, w_gate, w_up, w_down):
    p_rows = xs.shape[0]
    last = lambda i, te, nt: jnp.minimum(i, nt[0] - 1)
    grid_spec = pltpu.PrefetchScalarGridSpec(
        num_scalar_prefetch=2,
        grid=(p_rows // MOE_TILE,),
        in_specs=[pl.BlockSpec((MOE_TILE, D_MODEL), lambda i, te, nt: (last(i, te, nt), 0)),
                  pl.BlockSpec((None, D_MODEL, D_EXPERT), lambda i, te, nt: (te[i], 0, 0)),
                  pl.BlockSpec((None, D_MODEL, D_EXPERT), lambda i, te, nt: (te[i], 0, 0)),
                  pl.BlockSpec((None, D_EXPERT, D_MODEL), lambda i, te, nt: (te[i], 0, 0))],
        out_specs=pl.BlockSpec((MOE_TILE, D_MODEL), lambda i, te, nt: (last(i, te, nt), 0)),
        scratch_shapes=[pltpu.VMEM((D_MODEL, D_EXPERT), BF16), pltpu.VMEM((D_MODEL, D_EXPERT), BF16),
                        pltpu.VMEM((D_EXPERT, D_MODEL), BF16)],
    )
    return pl.pallas_call(
        _moe_kernel,
        out_shape=jax.ShapeDtypeStruct((p_rows, D_MODEL), F32),
        grid_spec=grid_spec,
        compiler_params=_cparams(("arbitrary",)),
        name="moe_experts",
    )(tile_expert, n_tiles, xs, w_gate, w_up, w_down)


def _final_kernel(x1_ref, y0_ref, y1_ref, w_ref, g_ref, o_ref):
    w = w_ref[...]
    x2 = x1_ref[...] + w[:, 0:1] * y0_ref[...] + w[:, 1:2] * y1_ref[...]
    o_ref[...] = x2 * lax.rsqrt(jnp.mean(x2 * x2, axis=1, keepdims=True) + EPS) * g_ref[...]


def _final(x1_all, yg0, yg1, wts, g_final, row0, n, tm):
    blk0 = row0 // tm
    rows = pl.BlockSpec((tm, D_MODEL), lambda i: (i + blk0, 0))
    return pl.pallas_call(
        _final_kernel,
        out_shape=jax.ShapeDtypeStruct((n, D_MODEL), F32),
        grid=(n // tm,),
        in_specs=[rows, rows, rows,
                  pl.BlockSpec((tm, LANES), lambda i: (i + blk0, 0)),
                  pl.BlockSpec((1, D_MODEL), lambda i: (0, 0))],
        out_specs=pl.BlockSpec((tm, D_MODEL), lambda i: (i, 0)),
        compiler_params=_cparams(("arbitrary",)),
        name="combine_final_norm",
    )(x1_all, yg0, yg1, wts, g_final)


def _s5_discretise(a_re, a_im, log_step, b_re, b_im):
    dt = jnp.exp(log_step)[:, None]
    mag = jnp.exp(a_re * dt)
    lb_re = mag * jnp.cos(a_im * dt)
    lb_im = mag * jnp.sin(a_im * dt)
    den = a_re * a_re + a_im * a_im
    nr = lb_re - 1.0
    coef_re = (nr * a_re + lb_im * a_im) / den
    coef_im = (lb_im * a_re - nr * a_im) / den
    bb_re = coef_re[..., None] * b_re - coef_im[..., None] * b_im
    bb_im = coef_re[..., None] * b_im + coef_im[..., None] * b_re
    return lb_re, lb_im, bb_re, bb_im


def _s5_chunk_params(a_re, a_im, log_step, b_re, b_im, c_re, c_im):
    lb_re, lb_im, bb_re, bb_im = _s5_discretise(a_re, a_im, log_step, b_re, b_im)
    L = S5_CHUNK
    gpb = LANES // S5_GROUP
    ncb = G_B // gpb
    pr, pi = [jnp.ones_like(lb_re)], [jnp.zeros_like(lb_re)]
    for _ in range(L):
        pr, pi = pr + [pr[-1] * lb_re - pi[-1] * lb_im], pi + [pr[-1] * lb_im + pi[-1] * lb_re]
    pw_r, pw_i = jnp.stack(pr), jnp.stack(pi)
    lbb_r = pw_r[:L, :, :, None] * bb_re - pw_i[:L, :, :, None] * bb_im
    lbb_i = pw_r[:L, :, :, None] * bb_im + pw_i[:L, :, :, None] * bb_re
    eye = jnp.eye(gpb, dtype=F32)

    def place_inc(m):
        m = m[::-1].reshape(L, ncb, gpb, P_S5, S5_GROUP).transpose(1, 0, 2, 4, 3)
        m = m[:, :, :, :, None, :] * eye[None, None, :, None, :, None]
        return m.reshape(ncb, L * LANES, S5_CST)

    winc = jnp.concatenate([place_inc(lbb_r), place_inc(lbb_i)], axis=2).astype(BF16)
    hp = lax.Precision.HIGHEST
    kd = (jnp.einsum('gop,dgpc->dgoc', c_re, lbb_r, precision=hp)
          - jnp.einsum('gop,dgpc->dgoc', c_im, lbb_i, precision=hp))
    lag = jnp.arange(L)[None, :] - jnp.arange(L)[:, None]
    toep = jnp.where((lag >= 0)[:, :, None, None, None], kd[jnp.clip(lag, 0, L - 1)], 0.0)
    toep = toep.reshape(L, L, ncb, gpb, S5_GROUP, S5_GROUP).transpose(2, 0, 3, 5, 1, 4)
    wintra = toep[:, :, :, :, :, None, :] * eye[None, None, :, None, None, :, None]
    wintra = wintra.reshape(ncb, L * LANES, L * LANES).astype(BF16)
    a_r = c_re[None] * pw_r[1:, :, None, :] - c_im[None] * pw_i[1:, :, None, :]
    a_i = -(c_re[None] * pw_i[1:, :, None, :] + c_im[None] * pw_r[1:, :, None, :])

    def place_out(m):
        m = m.reshape(L, ncb, gpb, S5_GROUP, P_S5).transpose(1, 2, 4, 0, 3)
        m = m[:, :, :, :, None, :] * eye[None, :, None, None, :, None]
        return m.reshape(ncb, S5_CST, L * LANES)

    wout = jnp.concatenate([place_out(a_r), place_out(a_i)], axis=1).astype(BF16)
    lbl_r = pw_r[L].reshape(ncb, 1, S5_CST)
    lbl_i = pw_i[L].reshape(ncb, 1, S5_CST)
    return winc, wintra, wout, lbl_r, lbl_i


def _s5_params(a_re, a_im, log_step, b_re, b_im, c_re, c_im):
    lb_re, lb_im, bb_re, bb_im = _s5_discretise(a_re, a_im, log_step, b_re, b_im)
    eye = jnp.eye(S5_GPB, dtype=F32)

    def blockdiag_in(m):
        m = m.reshape(S5_GB, S5_GPB, P_S5, S5_GROUP)
        return jnp.einsum('bgpc,gh->bgchp', m, eye).reshape(S5_GB, S5_UW, S5_XW)

    def blockdiag_out(m):
        m = m.reshape(S5_GB, S5_GPB, S5_GROUP, P_S5)
        return jnp.einsum('bgcp,gh->bgphc', m, eye).reshape(S5_GB, S5_XW, S5_UW)

    bb = jnp.concatenate([blockdiag_in(bb_re), blockdiag_in(bb_im)], axis=2).astype(BF16)
    cc = jnp.concatenate([blockdiag_out(c_re), -blockdiag_out(c_im)], axis=1).astype(BF16)
    lbr = lb_re.reshape(S5_GB, 1, S5_XW)
    lbi = lb_im.reshape(S5_GB, 1, S5_XW)
    return bb, cc, lbr, lbi


def _dispatch(ids, t_all, p_rows):
    e = ids[:, :2].reshape(-1)
    onehot = (e[:, None] == jnp.arange(N_EXPERTS, dtype=jnp.int32)[None, :]).astype(jnp.int32)
    csum = jnp.cumsum(onehot, axis=0)
    rank = jnp.sum((csum - onehot) * onehot, axis=1)
    counts = csum[-1]
    tiles = (counts + MOE_TILE - 1) // MOE_TILE
    tile_end = jnp.cumsum(tiles)
    tile_start = tile_end - tiles
    pos = jnp.sum(onehot * (tile_start * MOE_TILE)[None, :], axis=1) + rank
    tok = jnp.arange(2 * t_all, dtype=jnp.int32) // 2
    src = (jnp.arange(p_rows, dtype=jnp.int32) % t_all).at[pos].set(tok)
    n_tiles = tile_end[-1]
    tidx = jnp.arange(p_rows // MOE_TILE, dtype=jnp.int32)
    tclamp = jnp.minimum(tidx, n_tiles - 1)
    tile_expert = jnp.sum((tile_end[None, :] <= tclamp[:, None]).astype(jnp.int32), axis=1)
    return pos.reshape(t_all, 2), src, tile_expert, n_tiles.reshape(1).astype(jnp.int32)


def _pick_tile(n, candidates):
    for c in candidates:
        if n % c == 0:
            return c
    raise ValueError(f"no row tile for {n}")


def kernel(x_prompt, x_sample, state_mlstm_C, state_mlstm_n, state_mlstm_m, state_conv, state_s5_re,
           state_s5_im, norm_mix_g, w_in, b_i, b_f, w_conv, b_conv, head_norm_g, w_pa, s5_a_re, s5_a_im,
           s5_log_step, s5_b_re, s5_b_im, s5_c_re, s5_c_im, s5_d, s5_w_glu, s5_b_glu, w_pb, w_out,
           norm_ffn_g, w_rg, b_rg, w_rexp, b_rexp, w_gate, w_up, w_down, norm_final_g):
    assert state_mlstm_C.shape[0] == 1 and x_sample.shape[1] == 1
    batch, seq, _ = x_prompt.shape
    dec = x_sample.shape[0]
    t_p = batch * seq
    t_all = t_p + dec
    assert seq % CHUNK == 0 and t_p % dec == 0 and dec % LANES == 0
    d_a = H_A * DK
    tm_p = _pick_tile(t_p, (512, 256, 128))
    tm_all = _pick_tile(t_all, (640, 384, 128))
    sample_blk = t_p // dec

    xp = x_prompt.reshape(t_p, D_MODEL)
    xs = x_sample.reshape(dec, D_MODEL)

    g_mix = norm_mix_g[0]
    xn_all = _rmsnorm_rows(xp, g_mix, t_all, 0, tm_p)
    xn_all = _rmsnorm_rows(xs, g_mix, t_all, t_p, dec, alias=xn_all)
    w_in_t = w_in.reshape(w_in.shape[1:]).T
    n_qkvo = 4 * d_a
    tn = 1024
    qkvo = _matmul_t(xn_all, w_in_t, [j * tn for j in range(n_qkvo // tn)], jnp.zeros((1, n_qkvo), F32),
                     tm_all, tn, F32)
    n_gate_cols = 2 * H_A
    b_gates = jnp.pad(jnp.concatenate([b_i[0], b_f[0]]), (0, LANES - n_gate_cols)).reshape(1, LANES)
    gates = _matmul_t(xn_all, w_in_t, [n_qkvo], b_gates, tm_all, LANES, F32)
    c_u = n_qkvo + n_gate_cols
    c_ga = c_u + D_B
    starts = [c_ga + j * tn for j in range(2 * D_MODEL // tn)] + [c_u]
    proj2 = _matmul_t(xn_all, w_in_t, starts, jnp.zeros((1, len(starts) * tn), F32), tm_all, tn, F32)
    u_col0 = 2 * D_MODEL

    gates_t = gates[:t_p, :n_gate_cols].T
    head_g = head_norm_g[0].reshape(1, d_a)
    hn_all, c_p, n_p, m_p = _mlstm_prompt(qkvo, gates, gates_t, w_conv[0], b_conv[0].reshape(1, -1),
                                          head_g, batch, seq, t_all)
    conv_p = jnp.stack([qkvo[b * seq + seq - (CONV_W - 1):(b + 1) * seq, :2 * d_a] for b in range(batch)])

    bb, cc, lbr, lbi = _s5_params(s5_a_re[0], s5_a_im[0], s5_log_step[0], s5_b_re[0], s5_b_im[0],
                                  s5_c_re[0], s5_c_im[0])
    d_row = s5_d[0].reshape(1, D_B)
    chunk_w = _s5_chunk_params(s5_a_re[0], s5_a_im[0], s5_log_step[0], s5_b_re[0], s5_b_im[0],
                               s5_c_re[0], s5_c_im[0])
    ys_all, s5re_p, s5im_p = _s5_prompt(proj2, u_col0, *chunk_w, d_row, batch, seq, t_all)

    conv_s_in = state_conv[0].reshape(dec, (CONV_W - 1) * 2 * d_a)
    q_s, kw_s, a_e, s_e, den_e, conv_s, n_s, m_s = _sample_pre(
        qkvo, conv_s_in, w_conv[0], b_conv[0].reshape(1, -1), gates, state_mlstm_m[0],
        state_mlstm_n[0].reshape(dec, d_a), dec, sample_blk)
    v_s = qkvo[t_p:, 2 * d_a:3 * d_a]
    r3 = lambda a: a.reshape(dec, H_A, DK)
    c_s, qc = _sample_c(r3(q_s), r3(kw_s), r3(v_s), r3(a_e), state_mlstm_C[0], dec)
    hn_all = _sample_post(qc.reshape(dec, d_a), s_e, a_e, den_e, qkvo, head_g, hn_all, dec, sample_blk)

    ys_all, s5re_s, s5im_s = _s5_sample(proj2, u_col0, state_s5_re[0].reshape(dec, -1),
                                        state_s5_im[0].reshape(dec, -1),
                                        bb, cc, lbr, lbi, d_row, ys_all, dec, sample_blk)

    wr = jnp.pad(jnp.concatenate([w_rg[0], w_rexp[0]], axis=1), ((0, 0), (0, LANES - N_GROUPS - N_EXPERTS)))
    wr_hi = wr.astype(BF16)
    wr_lo = (wr - wr_hi.astype(F32)).astype(BF16)
    b_r = jnp.pad(jnp.concatenate([b_rg[0], b_rexp[0]]), (0, LANES - N_GROUPS - N_EXPERTS)).reshape(1, LANES)
    merge_w = (w_pa[0].astype(BF16), s5_w_glu[0].astype(BF16), w_pb[0].astype(BF16), w_out[0].astype(BF16))
    b_glu = s5_b_glu[0].reshape(1, D_B)
    g_ffn = norm_ffn_g[0].reshape(1, D_MODEL)
    tm_m = _pick_tile(t_p, (256, 128))
    outs = _merge(xp, hn_all, ys_all, proj2, b_glu, g_ffn, wr_hi, wr_lo, b_r, *merge_w, t_all, 0, tm_m)
    x1_all, xn2_all, ids, wts = _merge(xs, hn_all, ys_all, proj2, b_glu, g_ffn, wr_hi, wr_lo, b_r,
                                       *merge_w, t_all, t_p, dec, aliases=outs)

    p_rows = -(-(2 * t_all + N_EXPERTS * (MOE_TILE - 1)) // MOE_TILE) * MOE_TILE
    pos, src, tile_expert, n_tiles = _dispatch(ids, t_all, p_rows)
    take_rows = lambda a, idx: a.at[idx].get(mode='promise_in_bounds')
    xs_sorted = take_rows(xn2_all, src)
    yp = _moe_experts(tile_expert, n_tiles, xs_sorted, w_gate[0], w_up[0], w_down[0])
    yg0 = take_rows(yp, pos[:, 0])
    yg1 = take_rows(yp, pos[:, 1])

    g_fin = norm_final_g.reshape(1, D_MODEL)
    y_prompt = _final(x1_all, yg0, yg1, wts, g_fin, 0, t_p, tm_p).reshape(batch, seq, D_MODEL)
    y_sample = _final(x1_all, yg0, yg1, wts, g_fin, t_p, dec, dec).reshape(dec, 1, D_MODEL)

    lead = lambda a, shape: a.reshape((1,) + shape)
    return (y_prompt, y_sample,
            lead(c_p, (batch, H_A, DK, DK)), lead(n_p, (batch, H_A, DK)), lead(m_p[:, 0, :H_A], (batch, H_A)),
            lead(conv_p, (batch, CONV_W - 1, 2 * d_a)),
            lead(s5re_p, (batch, G_B, P_S5)), lead(s5im_p, (batch, G_B, P_S5)),
            lead(c_s, (dec, H_A, DK, DK)), lead(n_s, (dec, H_A, DK)), lead(m_s[:, :H_A], (dec, H_A)),
            lead(conv_s, (dec, CONV_W - 1, 2 * d_a)),
            lead(s5re_s, (dec, G_B, P_S5)), lead(s5im_s, (dec, G_B, P_S5)))
```

```python
import functools
import math

import jax
import jax.numpy as jnp
from jax import lax
from jax.experimental import pallas as pl
from jax.experimental.pallas import tpu as pltpu

F32 = jnp.float32
BF16 = jnp.bfloat16

D_MODEL = 2048
H_A = 8
DK = 256
CONV_W = 4
CHUNK = 128
D_B = 1024
S5_GROUP = 16
G_B = 64
P_S5 = 64
N_GROUPS = 4
EXP_PER_GROUP = 8
N_EXPERTS = 32
D_EXPERT = 512
EPS = 1e-6

LANES = 128
SUBLANES = 8
VMEM_LIMIT = 56 * 1024 * 1024

S5_GB = 4
S5_GPB = G_B // S5_GB
S5_UW = S5_GPB * S5_GROUP
S5_XW = S5_GPB * P_S5
S5_CHUNK = 8
S5_CST = (LANES // S5_GROUP) * P_S5
MOE_TILE = 256


def _cparams(sem):
    return pltpu.CompilerParams(dimension_semantics=sem, vmem_limit_bytes=VMEM_LIMIT)


def _silu(x):
    return x * (1.0 / (1.0 + jnp.exp(-x)))


def _sigmoid(x):
    return 1.0 / (1.0 + jnp.exp(-x))


def _log_sigmoid(x):
    return jnp.minimum(x, 0.0) - jnp.log1p(jnp.exp(-jnp.abs(x)))


def _gelu_tanh(x):
    c = math.sqrt(2.0 / math.pi)
    return 0.5 * x * (1.0 + jnp.tanh(c * (x + 0.044715 * (x * x * x))))


def _split3(x):
    hi = x.astype(BF16)
    r = x - hi.astype(F32)
    mid = r.astype(BF16)
    lo = (r - mid.astype(F32)).astype(BF16)
    return hi, mid, lo


def _dot(a, b):
    return jnp.dot(a, b, preferred_element_type=F32)


def _dot_nt(a, b):
    return lax.dot_general(a, b, (((1,), (1,)), ((), ())), preferred_element_type=F32)


def _dot_tn(a, b):
    return lax.dot_general(a, b, (((0,), (0,)), ((), ())), preferred_element_type=F32)


def _rmsnorm_kernel(x_ref, g_ref, *rest):
    o_ref = rest[-1]
    x = x_ref[...]
    r = lax.rsqrt(jnp.mean(x * x, axis=-1, keepdims=True) + EPS)
    o_ref[...] = (x * r * g_ref[...]).astype(o_ref.dtype)


def _rmsnorm_rows(x, g, t_all, row0, tm, alias=None):
    n = x.shape[0]
    blk0 = row0 // tm
    in_specs = [pl.BlockSpec((tm, D_MODEL), lambda i: (i, 0)),
                pl.BlockSpec((1, D_MODEL), lambda i: (0, 0))]
    args = [x, g.reshape(1, D_MODEL)]
    aliases = {}
    if alias is not None:
        in_specs.append(pl.BlockSpec(memory_space=pl.ANY))
        args.append(alias)
        aliases = {2: 0}
    return pl.pallas_call(
        _rmsnorm_kernel,
        out_shape=jax.ShapeDtypeStruct((t_all, D_MODEL), BF16),
        grid=(n // tm,),
        in_specs=in_specs,
        out_specs=pl.BlockSpec((tm, D_MODEL), lambda i: (i + blk0, 0)),
        input_output_aliases=aliases,
        compiler_params=_cparams(("arbitrary",)),
        name="rmsnorm_rows",
    )(*args)


def _mm_kernel(a_ref, w_ref, b_ref, o_ref, wb_ref):
    @pl.when(pl.program_id(1) == 0)
    def _():
        wb_ref[...] = w_ref[...].astype(BF16)

    o_ref[...] = (_dot(a_ref[...], wb_ref[...]) + b_ref[...]).astype(o_ref.dtype)


def _matmul(a, w, bias, n_out, tm, tn, out_dtype, col_blk0=0):
    m, k = a.shape
    return pl.pallas_call(
        _mm_kernel,
        out_shape=jax.ShapeDtypeStruct((m, n_out), out_dtype),
        grid=(n_out // tn, m // tm),
        in_specs=[pl.BlockSpec((tm, k), lambda j, i: (i, 0)),
                  pl.BlockSpec((k, tn), lambda j, i: (0, j + col_blk0)),
                  pl.BlockSpec((1, tn), lambda j, i: (0, j))],
        out_specs=pl.BlockSpec((tm, tn), lambda j, i: (i, j)),
        scratch_shapes=[pltpu.VMEM((k, tn), BF16)],
        compiler_params=_cparams(("arbitrary", "arbitrary")),
        name="rows_matmul",
    )(a, w, bias)


def _mm_t_kernel(starts_ref, a_ref, wt_ref, b_ref, o_ref, wb_ref):
    del starts_ref

    @pl.when(pl.program_id(1) == 0)
    def _():
        wb_ref[...] = wt_ref[...].astype(BF16)

    o_ref[...] = (_dot_nt(a_ref[...], wb_ref[...]) + b_ref[...]).astype(o_ref.dtype)


def _matmul_t(a, w_t, row_starts, bias, tm, tn, out_dtype):
    m, k = a.shape
    n_t = len(row_starts)
    assert all(s % SUBLANES == 0 for s in row_starts)
    grid_spec = pltpu.PrefetchScalarGridSpec(
        num_scalar_prefetch=1,
        grid=(n_t, m // tm),
        in_specs=[pl.BlockSpec((tm, k), lambda j, i, st: (i, 0)),
                  pl.BlockSpec((pl.Element(tn), pl.Element(k)), lambda j, i, st: (st[j] * SUBLANES, 0)),
                  pl.BlockSpec((1, tn), lambda j, i, st: (0, j))],
        out_specs=pl.BlockSpec((tm, tn), lambda j, i, st: (i, j)),
        scratch_shapes=[pltpu.VMEM((tn, k), BF16)],
    )
    return pl.pallas_call(
        _mm_t_kernel,
        out_shape=jax.ShapeDtypeStruct((m, n_t * tn), out_dtype),
        grid_spec=grid_spec,
        compiler_params=_cparams(("arbitrary", "arbitrary")),
        name="rows_matmul_t",
    )(jnp.asarray([s // SUBLANES for s in row_starts], jnp.int32), a, w_t, bias)


def _mm_shift_kernel(a_ref, wlo_ref, whi_ref, o_ref, wb_ref, *, shift):
    tn = wlo_ref.shape[1]
    rows = 256

    @pl.when(pl.program_id(1) == 0)
    def _():
        for r in range(wlo_ref.shape[0] // rows):
            sl = slice(r * rows, (r + 1) * rows)
            w = jnp.concatenate([wlo_ref[sl, :], whi_ref[sl, :]], axis=1)
            wb_ref[sl, :] = w[:, shift:shift + tn].astype(BF16)

    o_ref[...] = _dot(a_ref[...], wb_ref[...]).astype(o_ref.dtype)


def _matmul_shifted(a, w, col_starts, tm, tn, out_dtype):
    m, k = a.shape
    shift = col_starts[0] % tn
    assert all(c % tn == shift for c in col_starts) and 0 < shift < LANES
    lo_blk = [c // tn for c in col_starts]
    n_t = len(col_starts)
    base, first = min(lo_blk), lo_blk[0]
    assert all(lo_blk[j] == base + (j + first - base) % n_t for j in range(n_t))
    lo_idx = lambda j: base + (j + (first - base)) % n_t
    per = tn // LANES
    kern = functools.partial(_mm_shift_kernel, shift=shift)
    return pl.pallas_call(
        kern,
        out_shape=jax.ShapeDtypeStruct((m, n_t * tn), out_dtype),
        grid=(n_t, m // tm),
        in_specs=[pl.BlockSpec((tm, k), lambda j, i: (i, 0)),
                  pl.BlockSpec((k, tn), lambda j, i: (0, lo_idx(j))),
                  pl.BlockSpec((k, LANES), lambda j, i: (0, (lo_idx(j) + 1) * per))],
        out_specs=pl.BlockSpec((tm, tn), lambda j, i: (i, j)),
        scratch_shapes=[pltpu.VMEM((k, tn), BF16)],
        compiler_params=_cparams(("arbitrary", "arbitrary")),
        name="rows_matmul_shifted",
    )(a, w, w)


def _mlstm_kernel(qk_ref, v_ref, o_ref, gcol_ref, grow_ref, wc_ref, bc_ref, hg_ref,
                  h_ref, c_out, n_out, m_out, c_sc, n_sc, m_sc, ext_sc):
    c = pl.program_id(1)
    L = CHUNK
    pad = SUBLANES
    d_a = H_A * DK

    @pl.when(c == 0)
    def _():
        c_sc[...] = jnp.zeros_like(c_sc)
        n_sc[...] = jnp.zeros_like(n_sc)
        m_sc[...] = jnp.zeros_like(m_sc)
        ext_sc[0:pad, :] = jnp.zeros((pad, 2 * d_a), F32)

    x = qk_ref[...]
    ext_sc[pad:pad + L, :] = x
    y = bc_ref[...] + wc_ref[CONV_W - 1:CONV_W, :] * x
    for j in range(1, CONV_W):
        y = y + wc_ref[CONV_W - 1 - j:CONV_W - j, :] * ext_sc[pad - j:pad - j + L, :]
    ext_sc[0:pad, :] = x[L - pad:L, :]
    y = _silu(y)

    gcol = gcol_ref[...]
    grow = grow_ref[...]
    ri = lax.broadcasted_iota(jnp.int32, (L, L), 0)
    ci = lax.broadcasted_iota(jnp.int32, (L, L), 1)
    causal = ci <= ri
    tril = jnp.where(causal, 1.0, 0.0).astype(BF16)
    triu = jnp.where(ri <= ci, 1.0, 0.0).astype(BF16)
    b_cols = sum(_dot(tril, p) for p in _split3(_log_sigmoid(gcol)))
    b_rows = sum(_dot(p, triu) for p in _split3(_log_sigmoid(grow)))

    for h in range(H_A):
        sl = slice(h * DK, (h + 1) * DK)
        q = y[:, sl] * (DK ** -0.5)
        k = y[:, d_a + h * DK:d_a + (h + 1) * DK]
        ig_col = gcol[:, h:h + 1]
        ig_row = grow[h:h + 1, :]
        b_col = b_cols[:, H_A + h:H_A + h + 1]
        b_row = b_rows[H_A + h:H_A + h + 1, :]

        m_prev = m_sc[:, h:h + 1]
        d_log = jnp.where(causal, b_col - b_row + ig_row, -jnp.inf)
        inter_log = b_col + m_prev
        m_t = jnp.maximum(inter_log, jnp.max(d_log, axis=1, keepdims=True))
        qb = q.astype(BF16)
        kb = k.astype(BF16)
        vb = v_ref[:, sl].astype(BF16)
        s = _dot_nt(qb, kb) * jnp.exp(d_log - m_t)
        inter_w = jnp.exp(inter_log - m_t)
        c_prev = c_sc[h]
        n_prev = n_sc[h:h + 1, :]
        num = _dot(s.astype(BF16), vb) + inter_w * _dot(qb, c_prev.astype(BF16))
        nq = jnp.sum(s, axis=1, keepdims=True) + inter_w * jnp.sum(q * n_prev, axis=1, keepdims=True)
        den = jnp.maximum(jnp.abs(nq), jnp.exp(-m_t))
        hh = num / den
        hh = hh * _sigmoid(o_ref[:, sl])
        hh = hh * lax.rsqrt(jnp.mean(hh * hh, axis=1, keepdims=True) + EPS)
        h_ref[:, sl] = (hh * hg_ref[:, sl]).astype(h_ref.dtype)

        m_new = m_t[L - 1:L, :]
        b_last = b_col[L - 1:L, :]
        decay = jnp.exp(b_last + m_prev - m_new)
        w_end = jnp.exp(b_last - b_col + ig_col - m_new)
        kw = k * w_end
        c_sc[h] = decay * c_prev + _dot_tn(kw.astype(BF16), vb)
        n_sc[h:h + 1, :] = decay * n_prev + jnp.sum(kw, axis=0, keepdims=True)
        m_sc[:, h:h + 1] = m_new

    @pl.when(c == pl.num_programs(1) - 1)
    def _():
        c_out[...] = c_sc[...]
        n_out[...] = n_sc[...]
        m_out[...] = m_sc[...]


def _mlstm_prompt(qkvo, gates, gates_t, w_conv, b_conv, head_g, batch, seq, t_all):
    nc = seq // CHUNK
    L = CHUNK
    d_a = H_A * DK
    row = lambda b, c: b * nc + c
    in_specs = [
        pl.BlockSpec((L, 2 * d_a), lambda b, c: (row(b, c), 0)),
        pl.BlockSpec((L, d_a), lambda b, c: (row(b, c), 2)),
        pl.BlockSpec((L, d_a), lambda b, c: (row(b, c), 3)),
        pl.BlockSpec((L, LANES), lambda b, c: (row(b, c), 0)),
        pl.BlockSpec((2 * H_A, L), lambda b, c: (0, row(b, c))),
        pl.BlockSpec((CONV_W, 2 * d_a), lambda b, c: (0, 0)),
        pl.BlockSpec((1, 2 * d_a), lambda b, c: (0, 0)),
        pl.BlockSpec((1, d_a), lambda b, c: (0, 0)),
    ]
    out_shape = (
        jax.ShapeDtypeStruct((t_all, d_a), BF16),
        jax.ShapeDtypeStruct((batch, H_A, DK, DK), F32),
        jax.ShapeDtypeStruct((batch, H_A, DK), F32),
        jax.ShapeDtypeStruct((batch, 1, LANES), F32),
    )
    out_specs = (
        pl.BlockSpec((L, d_a), lambda b, c: (row(b, c), 0)),
        pl.BlockSpec((None, H_A, DK, DK), lambda b, c: (b, 0, 0, 0)),
        pl.BlockSpec((None, H_A, DK), lambda b, c: (b, 0, 0)),
        pl.BlockSpec((None, 1, LANES), lambda b, c: (b, 0, 0)),
    )
    return pl.pallas_call(
        _mlstm_kernel,
        out_shape=out_shape,
        grid=(batch, nc),
        in_specs=in_specs,
        out_specs=out_specs,
        scratch_shapes=[pltpu.VMEM((H_A, DK, DK), F32), pltpu.VMEM((H_A, DK), F32),
                        pltpu.VMEM((1, LANES), F32), pltpu.VMEM((SUBLANES + L, 2 * d_a), F32)],
        compiler_params=_cparams(("arbitrary", "arbitrary")),
        name="mlstm_prompt",
    )(qkvo, qkvo, qkvo, gates, gates_t, w_conv, b_conv, head_g)


def _lane_tile(x, reps):
    w = x.shape[1]
    ri = lax.broadcasted_iota(jnp.int32, (w, w * reps), 0)
    ci = lax.broadcasted_iota(jnp.int32, (w, w * reps), 1)
    rep = jnp.where(jnp.bitwise_and(ci, w - 1) == ri, 1.0, 0.0).astype(BF16)
    return _dot(x.astype(BF16), rep)


def _group_mask(rows, cols, row_per, col_per):
    ri = lax.broadcasted_iota(jnp.int32, (rows, cols), 0)
    ci = lax.broadcasted_iota(jnp.int32, (rows, cols), 1)
    return (jnp.right_shift(ri, int(math.log2(row_per))) == jnp.right_shift(ci, int(math.log2(col_per))))


def _s5_build_weights(bin_r, bin_i, kdt, aout_r, aout_i, winc_ref, wintra_ref, wout_ref):
    L = S5_CHUNK
    gpb = LANES // S5_GROUP
    m_inc = _group_mask(LANES, S5_CST, S5_GROUP, P_S5)
    m_lag = _group_mask(LANES, LANES, S5_GROUP, S5_GROUP)
    m_out = _group_mask(S5_CST, LANES, P_S5, S5_GROUP)
    zero = jnp.zeros((LANES, LANES), BF16)
    lag = [jnp.where(m_lag, _lane_tile(kdt[d], gpb), 0.0).astype(BF16) for d in range(L)]
    for t in range(L):
        rows = slice(t * LANES, (t + 1) * LANES)
        d = L - 1 - t
        winc_ref[rows, 0:S5_CST] = jnp.where(m_inc, _lane_tile(bin_r[d], gpb), 0.0).astype(BF16)
        winc_ref[rows, S5_CST:2 * S5_CST] = jnp.where(m_inc, _lane_tile(bin_i[d], gpb), 0.0).astype(BF16)
        wout_ref[0:S5_CST, rows] = jnp.where(m_out, _lane_tile(aout_r[t], gpb), 0.0).astype(BF16)
        wout_ref[S5_CST:2 * S5_CST, rows] = jnp.where(m_out, _lane_tile(aout_i[t], gpb), 0.0).astype(BF16)
        for t2 in range(L):
            wintra_ref[rows, t2 * LANES:(t2 + 1) * LANES] = lag[t2 - t] if t2 >= t else zero


def _s5_prompt_kernel(u_ref, binr_ref, bini_ref, kdt_ref, aoutr_ref, aouti_ref, lbr_ref, lbi_ref, d_ref,
                      ys_ref, sre_ref, sim_ref, x_sc, winc_ref, wintra_ref, wout_ref, *, batch, nchunk):
    L = S5_CHUNK
    nrow = batch * nchunk
    nst = S5_CST // LANES
    rstr = nchunk + SUBLANES
    _s5_build_weights(binr_ref, bini_ref, kdt_ref, aoutr_ref, aouti_ref, winc_ref, wintra_ref, wout_ref)
    u_t = [u_ref[pl.ds(t, nrow, stride=L), :] for t in range(L)]
    lhs = jnp.concatenate([a.astype(BF16) for a in u_t], axis=1)
    inc = _dot(lhs, winc_ref[...])
    for j in range(2 * nst):
        for b in range(batch):
            x_sc[j, b * rstr:b * rstr + nchunk, :] = inc[b * nchunk:(b + 1) * nchunk, j * LANES:(j + 1) * LANES]

    lbr = [jnp.broadcast_to(lbr_ref[:, j * LANES:(j + 1) * LANES], (batch, LANES)) for j in range(nst)]
    lbi = [jnp.broadcast_to(lbi_ref[:, j * LANES:(j + 1) * LANES], (batch, LANES)) for j in range(nst)]

    def scan_body(r, carry):
        rows = pl.ds(r, batch, stride=rstr)
        out = []
        for j in range(nst):
            xr, xi = carry[j]
            ir = x_sc[j, rows, :]
            ii = x_sc[nst + j, rows, :]
            x_sc[j, rows, :] = xr
            x_sc[nst + j, rows, :] = xi
            out.append((lbr[j] * xr - lbi[j] * xi + ir, lbr[j] * xi + lbi[j] * xr + ii))
        return tuple(out)

    z = jnp.zeros((batch, LANES), F32)
    fin = lax.fori_loop(0, nchunk, scan_body, tuple((z, z) for _ in range(nst)))
    for j in range(nst):
        sre_ref[:, j * LANES:(j + 1) * LANES] = fin[j][0]
        sim_ref[:, j * LANES:(j + 1) * LANES] = fin[j][1]

    xprev = jnp.concatenate(
        [jnp.concatenate([x_sc[j, b * rstr:b * rstr + nchunk, :] for b in range(batch)], axis=0)
         for j in range(2 * nst)], axis=1).astype(BF16)
    y = _dot(lhs, wintra_ref[...]) + _dot(xprev, wout_ref[...])
    for t in range(L):
        yt = y[:, t * LANES:(t + 1) * LANES] + d_ref[...] * u_t[t]
        ys_ref[pl.ds(t, nrow, stride=L), :] = _gelu_tanh(yt).astype(ys_ref.dtype)


def _s5_prompt(proj2, u_col0, bin_r, bin_i, kdt, aout_r, aout_i, lb8r, lb8i, d_row, batch, seq, t_all):
    nchunk = seq // S5_CHUNK
    t_p = batch * seq
    kern = functools.partial(_s5_prompt_kernel, batch=batch, nchunk=nchunk)
    ub0 = u_col0 // LANES
    ncb = D_B // LANES
    kw = S5_CHUNK * LANES
    per_blk = lambda a: pl.BlockSpec((a.shape[0], None) + a.shape[2:], lambda g: (0, g, 0, 0))
    return pl.pallas_call(
        kern,
        out_shape=(jax.ShapeDtypeStruct((t_all, D_B), F32),
                   jax.ShapeDtypeStruct((batch, G_B * P_S5), F32),
                   jax.ShapeDtypeStruct((batch, G_B * P_S5), F32)),
        grid=(ncb,),
        in_specs=[pl.BlockSpec((t_p, LANES), lambda g: (0, ub0 + g)),
                  per_blk(bin_r), per_blk(bin_i), per_blk(kdt), per_blk(aout_r), per_blk(aout_i),
                  pl.BlockSpec((None, 1, S5_CST), lambda g: (g, 0, 0)),
                  pl.BlockSpec((None, 1, S5_CST), lambda g: (g, 0, 0)),
                  pl.BlockSpec((1, LANES), lambda g: (0, g))],
        out_specs=(pl.BlockSpec((t_p, LANES), lambda g: (0, g)),
                   pl.BlockSpec((batch, S5_CST), lambda g: (0, g)),
                   pl.BlockSpec((batch, S5_CST), lambda g: (0, g))),
        scratch_shapes=[pltpu.VMEM((2 * S5_CST // LANES, batch * (nchunk + SUBLANES), LANES), F32),
                        pltpu.VMEM((kw, 2 * S5_CST), BF16), pltpu.VMEM((kw, kw), BF16),
                        pltpu.VMEM((2 * S5_CST, kw), BF16)],
        compiler_params=_cparams(("arbitrary",)),
        name="s5_prompt",
    )(proj2, bin_r, bin_i, kdt, aout_r, aout_i, lb8r, lb8i, d_row)


def _sample_pre_kernel(qk_ref, conv_ref, wc_ref, bc_ref, g_ref, m_ref, n_ref,
                       q_out, kw_out, a_out, s_out, den_out, conv_out, n_out, m_out):
    c2 = 2 * H_A * DK
    x_new = qk_ref[...]
    y = bc_ref[...] + wc_ref[CONV_W - 1:CONV_W, :] * x_new
    for j in range(CONV_W - 1):
        y = y + wc_ref[j:j + 1, :] * conv_ref[:, j * c2:(j + 1) * c2]
    y = _silu(y)
    conv_out[:, 0:(CONV_W - 2) * c2] = conv_ref[:, c2:(CONV_W - 1) * c2]
    conv_out[:, (CONV_W - 2) * c2:(CONV_W - 1) * c2] = x_new
    g = g_ref[...]
    bd = x_new.shape[0]
    m_cols = []
    for h in range(H_A):
        sl = slice(h * DK, (h + 1) * DK)
        q = y[:, sl] * (DK ** -0.5)
        k = y[:, H_A * DK + h * DK:H_A * DK + (h + 1) * DK]
        ig = g[:, h:h + 1]
        lf = _log_sigmoid(g[:, H_A + h:H_A + h + 1])
        m_prev = m_ref[:, h:h + 1]
        m_t = jnp.maximum(lf + m_prev, ig)
        a = jnp.exp(lf + m_prev - m_t)
        wgt = jnp.exp(ig - m_t)
        n_prev = n_ref[:, sl]
        s = jnp.sum(q * k, axis=1, keepdims=True) * wgt
        nq = s + a * jnp.sum(q * n_prev, axis=1, keepdims=True)
        den = jnp.maximum(jnp.abs(nq), jnp.exp(-m_t))
        kw = wgt * k
        q_out[:, sl] = q
        kw_out[:, sl] = kw
        a_out[:, sl] = jnp.broadcast_to(a, (bd, DK))
        s_out[:, sl] = jnp.broadcast_to(s, (bd, DK))
        den_out[:, sl] = jnp.broadcast_to(den, (bd, DK))
        n_out[:, sl] = a * n_prev + kw
        m_cols.append(m_t)
    lane = lax.broadcasted_iota(jnp.int32, (bd, LANES), 1)
    m_full = jnp.zeros((bd, LANES), F32)
    for h in range(H_A):
        m_full = jnp.where(lane == h, m_cols[h], m_full)
    m_out[...] = m_full


def _sample_pre(qkvo, conv_state, w_conv, b_conv, gates, m_state, n_state, dec, row_blk):
    c2 = 2 * H_A * DK
    d = H_A * DK
    full = lambda shape: pl.BlockSpec(shape, lambda i: (0,) * len(shape))
    rows = lambda: jax.ShapeDtypeStruct((dec, d), F32)
    return pl.pallas_call(
        _sample_pre_kernel,
        out_shape=(rows(), rows(), rows(), rows(), rows(),
                   jax.ShapeDtypeStruct((dec, (CONV_W - 1) * c2), F32), rows(),
                   jax.ShapeDtypeStruct((dec, LANES), F32)),
        grid=(1,),
        in_specs=[pl.BlockSpec((dec, c2), lambda i: (row_blk, 0)),
                  full((dec, (CONV_W - 1) * c2)), full((CONV_W, c2)), full((1, c2)),
                  pl.BlockSpec((dec, LANES), lambda i: (row_blk, 0)),
                  full((dec, H_A)), full((dec, d))],
        out_specs=(full((dec, d)), full((dec, d)), full((dec, d)), full((dec, d)), full((dec, d)),
                   full((dec, (CONV_W - 1) * c2)), full((dec, d)), full((dec, LANES))),
        compiler_params=_cparams(("arbitrary",)),
        name="sample_pre",
    )(qkvo, conv_state, w_conv, b_conv, gates, m_state, n_state)


def _sample_c_kernel(q_ref, kw_ref, v_ref, a_ref, c_ref, c_out, qc_out):
    eye = jnp.where(lax.broadcasted_iota(jnp.int32, (DK, DK), 0)
                    == lax.broadcasted_iota(jnp.int32, (DK, DK), 1), 1.0, 0.0).astype(BF16)
    for b in range(q_ref.shape[0]):
        q_t = sum(_dot_nt(eye, p) for p in _split3(q_ref[b]))
        kw_t = sum(_dot_nt(eye, p) for p in _split3(kw_ref[b]))
        qc_rows = []
        for h in range(H_A):
            c_prev = c_ref[b, h]
            qc_rows.append(jnp.sum(q_t[:, h:h + 1] * c_prev, axis=0, keepdims=True))
            c_out[b, h] = a_ref[b, h:h + 1, :] * c_prev + kw_t[:, h:h + 1] * v_ref[b, h:h + 1, :]
        qc_out[b] = jnp.concatenate(qc_rows, axis=0)


SAMPLE_C_BATCH = 4


def _sample_c(q3, kw3, v3, a3, c_state, dec):
    nb = SAMPLE_C_BATCH
    vec = pl.BlockSpec((nb, H_A, DK), lambda b: (b, 0, 0))
    mat = pl.BlockSpec((nb, H_A, DK, DK), lambda b: (b, 0, 0, 0))
    return pl.pallas_call(
        _sample_c_kernel,
        out_shape=(jax.ShapeDtypeStruct((dec, H_A, DK, DK), F32),
                   jax.ShapeDtypeStruct((dec, H_A, DK), F32)),
        grid=(dec // nb,),
        in_specs=[vec, vec, vec, vec, mat],
        out_specs=(mat, vec),
        compiler_params=_cparams(("arbitrary",)),
        name="sample_c_update",
    )(q3, kw3, v3, a3, c_state)


def _sample_post_kernel(qc_ref, s_ref, a_ref, den_ref, vo_ref, hg_ref, hn_in, h_ref):
    del hn_in
    d = H_A * DK
    num = s_ref[...] * vo_ref[:, 0:d] + a_ref[...] * qc_ref[...]
    hh = num / den_ref[...]
    hh = hh * _sigmoid(vo_ref[:, d:2 * d])
    for h in range(H_A):
        sl = slice(h * DK, (h + 1) * DK)
        seg = hh[:, sl]
        seg = seg * lax.rsqrt(jnp.mean(seg * seg, axis=1, keepdims=True) + EPS)
        h_ref[:, sl] = (seg * hg_ref[:, sl]).astype(h_ref.dtype)


def _sample_post(qc, s_e, a_e, den_e, qkvo, head_g, hn_all, dec, row_blk):
    d = H_A * DK
    full = lambda shape: pl.BlockSpec(shape, lambda i: (0,) * len(shape))
    return pl.pallas_call(
        _sample_post_kernel,
        out_shape=jax.ShapeDtypeStruct(hn_all.shape, hn_all.dtype),
        grid=(1,),
        in_specs=[full((dec, d)), full((dec, d)), full((dec, d)), full((dec, d)),
                  pl.BlockSpec((dec, 2 * d), lambda i: (row_blk, 1)),
                  full((1, d)), pl.BlockSpec(memory_space=pl.ANY)],
        out_specs=pl.BlockSpec((dec, d), lambda i: (row_blk, 0)),
        input_output_aliases={6: 0},
        compiler_params=_cparams(("arbitrary",)),
        name="sample_post",
    )(qc, s_e, a_e, den_e, qkvo, head_g, hn_all)


def _s5_sample_kernel(u_ref, sr_ref, si_ref, bb_ref, cc_ref, lbr_ref, lbi_ref, d_ref, ys_in,
                      ys_ref, sre_out, sim_out):
    del ys_in
    for g in range(S5_GB):
        u = u_ref[:, g * S5_UW:(g + 1) * S5_UW]
        bu = _dot(u.astype(BF16), bb_ref[g])
        sl = slice(g * S5_XW, (g + 1) * S5_XW)
        lbr = lbr_ref[g]
        lbi = lbi_ref[g]
        sr = sr_ref[:, sl]
        si = si_ref[:, sl]
        xr = lbr * sr - lbi * si + bu[:, 0:S5_XW]
        xi = lbr * si + lbi * sr + bu[:, S5_XW:2 * S5_XW]
        sre_out[:, sl] = xr
        sim_out[:, sl] = xi
        x = jnp.concatenate([xr, xi], axis=1).astype(BF16)
        y = _dot(x, cc_ref[g]) + d_ref[:, g * S5_UW:(g + 1) * S5_UW] * u
        ys_ref[:, g * S5_UW:(g + 1) * S5_UW] = _gelu_tanh(y).astype(ys_ref.dtype)


def _s5_sample(proj2, u_col0, s_re, s_im, bb, cc, lbr, lbi, d_row, ys_all, dec, row_blk):
    full = lambda shape: pl.BlockSpec(shape, lambda i: (0,) * len(shape))
    n_state = G_B * P_S5
    ub0 = u_col0 // D_B
    return pl.pallas_call(
        _s5_sample_kernel,
        out_shape=(jax.ShapeDtypeStruct(ys_all.shape, ys_all.dtype),
                   jax.ShapeDtypeStruct((dec, n_state), F32),
                   jax.ShapeDtypeStruct((dec, n_state), F32)),
        grid=(1,),
        in_specs=[pl.BlockSpec((dec, D_B), lambda i: (row_blk, ub0)),
                  full((dec, n_state)), full((dec, n_state)),
                  full(bb.shape), full(cc.shape), full(lbr.shape), full(lbi.shape), full((1, D_B)),
                  pl.BlockSpec(memory_space=pl.ANY)],
        out_specs=(pl.BlockSpec((dec, D_B), lambda i: (row_blk, 0)),
                   full((dec, n_state)), full((dec, n_state))),
        input_output_aliases={8: 0},
        compiler_params=_cparams(("arbitrary",)),
        name="s5_sample",
    )(proj2, s_re, s_im, bb, cc, lbr, lbi, d_row, ys_all)


def _merge_kernel(x_ref, hn_ref, ys_ref, ga_ref, gb_ref, bglu_ref, gffn_ref, wrh_ref, wrl_ref, br_ref,
                  wpa_hbm, wglu_hbm, wpb_hbm, wout_hbm, *rest):
    n_alias = len(rest) - 9
    x1_ref, xn_ref, ids_ref, wts_ref = rest[n_alias:n_alias + 4]
    wpa, wglu, wpb, wout, sem = rest[n_alias + 4:]

    @pl.when(pl.program_id(0) == 0)
    def _():
        copies = [pltpu.make_async_copy(src, dst, sem.at[i])
                  for i, (src, dst) in enumerate(((wpa_hbm, wpa), (wglu_hbm, wglu),
                                                  (wpb_hbm, wpb), (wout_hbm, wout)))]
        for cp in copies:
            cp.start()
        for cp in copies:
            cp.wait()

    ya = _dot(hn_ref[...], wpa[...])
    ys = ys_ref[...]
    gate = _sigmoid(_dot(ys.astype(BF16), wglu[...]) + bglu_ref[...])
    yb = _dot((ys * gate).astype(BF16), wpb[...])
    z = _sigmoid(ga_ref[...]) * ya + _sigmoid(gb_ref[...]) * yb
    x1 = x_ref[...] + _dot(z.astype(BF16), wout[...])
    x1_ref[...] = x1
    xn = x1 * lax.rsqrt(jnp.mean(x1 * x1, axis=1, keepdims=True) + EPS) * gffn_ref[...]
    xn_ref[...] = xn
    xh = xn.astype(BF16)
    xl = (xn - xh.astype(F32)).astype(BF16)
    logits = _dot(xh, wrh_ref[...]) + _dot(xl, wrh_ref[...]) + _dot(xh, wrl_ref[...]) + br_ref[...]

    lane_i = lax.broadcasted_iota(jnp.int32, logits.shape, 1)
    lane = lane_i.astype(F32)
    neg = -jnp.inf
    big = float(1 << 20)
    gl = jnp.where(lane_i < N_GROUPS, logits, neg)
    gmax = jnp.max(gl, axis=1, keepdims=True)
    gsum = jnp.sum(jnp.exp(gl - gmax), axis=1, keepdims=True)
    gidx = jnp.min(jnp.where(gl == gmax, lane, big), axis=1, keepdims=True)
    pg_sel = 1.0 / gsum
    lo = N_GROUPS + gidx * EXP_PER_GROUP
    in_grp = (lane >= lo) & (lane < lo + EXP_PER_GROUP)
    el = jnp.where(in_grp, logits, neg)
    emax = jnp.max(el, axis=1, keepdims=True)
    ee = jnp.exp(el - emax)
    pe = ee / jnp.sum(ee, axis=1, keepdims=True)
    v0 = jnp.max(pe, axis=1, keepdims=True)
    i0 = jnp.min(jnp.where(in_grp & (pe == v0), lane, big), axis=1, keepdims=True)
    rest_m = in_grp & (lane != i0)
    pe1 = jnp.where(rest_m, pe, neg)
    v1 = jnp.max(pe1, axis=1, keepdims=True)
    i1 = jnp.min(jnp.where(rest_m & (pe1 == v1), lane, big), axis=1, keepdims=True)
    tot = v0 + v1
    w0 = pg_sel * (v0 / tot)
    w1 = pg_sel * (v1 / tot)
    ids = jnp.where(lane_i == 0, i0 - N_GROUPS, jnp.where(lane_i == 1, i1 - N_GROUPS, 0.0))
    ids_ref[...] = ids.astype(jnp.int32)
    wts_ref[...] = jnp.where(lane_i == 0, w0, jnp.where(lane_i == 1, w1, 0.0))


def _merge(x, hn_all, ys_all, proj2, b_glu, g_ffn, wr_hi, wr_lo, b_r, wpa, wglu, wpb, wout,
           t_all, row0, tm, aliases=None):
    n = x.shape[0]
    blk0 = row0 // tm
    const = lambda shape: pl.BlockSpec(shape, lambda i: (0,) * len(shape))
    any_spec = pl.BlockSpec(memory_space=pl.ANY)
    in_specs = [pl.BlockSpec((tm, D_MODEL), lambda i: (i, 0)),
                pl.BlockSpec((tm, D_MODEL), lambda i: (i + blk0, 0)),
                pl.BlockSpec((tm, D_B), lambda i: (i + blk0, 0)),
                pl.BlockSpec((tm, D_MODEL), lambda i: (i + blk0, 0)),
                pl.BlockSpec((tm, D_MODEL), lambda i: (i + blk0, 1)),
                const((1, D_B)), const((1, D_MODEL)),
                const((D_MODEL, LANES)), const((D_MODEL, LANES)), const((1, LANES)),
                any_spec, any_spec, any_spec, any_spec]
    args = [x, hn_all, ys_all, proj2, proj2, b_glu, g_ffn, wr_hi, wr_lo, b_r, wpa, wglu, wpb, wout]
    io_alias = {}
    if aliases is not None:
        for j, a in enumerate(aliases):
            in_specs.append(any_spec)
            args.append(a)
            io_alias[14 + j] = j
    out_shape = (jax.ShapeDtypeStruct((t_all, D_MODEL), F32),
                 jax.ShapeDtypeStruct((t_all, D_MODEL), F32),
                 jax.ShapeDtypeStruct((t_all, LANES), jnp.int32),
                 jax.ShapeDtypeStruct((t_all, LANES), F32))
    out_specs = (pl.BlockSpec((tm, D_MODEL), lambda i: (i + blk0, 0)),
                 pl.BlockSpec((tm, D_MODEL), lambda i: (i + blk0, 0)),
                 pl.BlockSpec((tm, LANES), lambda i: (i + blk0, 0)),
                 pl.BlockSpec((tm, LANES), lambda i: (i + blk0, 0)))
    return pl.pallas_call(
        _merge_kernel,
        out_shape=out_shape,
        grid=(n // tm,),
        in_specs=in_specs,
        out_specs=out_specs,
        scratch_shapes=[pltpu.VMEM(wpa.shape, BF16), pltpu.VMEM(wglu.shape, BF16),
                        pltpu.VMEM(wpb.shape, BF16), pltpu.VMEM(wout.shape, BF16),
                        pltpu.SemaphoreType.DMA((4,))],
        input_output_aliases=io_alias,
        compiler_params=_cparams(("arbitrary",)),
        name="merge_router",
    )(*args)


def _moe_kernel(te_ref, nt_ref, xs_ref, wg_ref, wu_ref, wd_ref, o_ref, wg_sc, wu_sc, wd_sc):
    i = pl.program_id(0)
    e = te_ref[i]
    prev = te_ref[jnp.maximum(i - 1, 0)]

    @pl.when((i == 0) | (e != prev))
    def _():
        wg_sc[...] = wg_ref[...].astype(BF16)
        wu_sc[...] = wu_ref[...].astype(BF16)
        wd_sc[...] = wd_ref[...].astype(BF16)

    @pl.when(i < nt_ref[0])
    def _():
        x = xs_ref[...].astype(BF16)
        hg = _dot(x, wg_sc[...])
        hu = _dot(x, wu_sc[...])
        hh = (_silu(hg) * hu).astype(BF16)
        o_ref[...] = _dot(hh, wd_sc[...])


def _moe_experts(tile_expert, n_tiles, xs, w_gate, w_up, w_down):
    p_rows = xs.shape[0]
    last = lambda i, te, nt: jnp.minimum(i, nt[0] - 1)
    grid_spec = pltpu.PrefetchScalarGridSpec(
        num_scalar_prefetch=2,
        grid=(p_rows // MOE_TILE,),
        in_specs=[pl.BlockSpec((MOE_TILE, D_MODEL), lambda i, te, nt: (last(i, te, nt), 0)),
                  pl.BlockSpec((None, D_MODEL, D_EXPERT), lambda i, te, nt: (te[i], 0, 0)),
                  pl.BlockSpec((None, D_MODEL, D_EXPERT), lambda i, te, nt: (te[i], 0, 0)),
                  pl.BlockSpec((None, D_EXPERT, D_MODEL), lambda i, te, nt: (te[i], 0, 0))],
        out_specs=pl.BlockSpec((MOE_TILE, D_MODEL), lambda i, te, nt: (last(i, te, nt), 0)),
        scratch_shapes=[pltpu.VMEM((D_MODEL, D_EXPERT), BF16), pltpu.VMEM((D_MODEL, D_EXPERT), BF16),
                        pltpu.VMEM((D_EXPERT, D_MODEL), BF16)],
    )
    return pl.pallas_call(
        _moe_kernel,
        out_shape=jax.ShapeDtypeStruct((p_rows, D_MODEL), F32),
        grid_spec=grid_spec,
        compiler_params=_cparams(("arbitrary",)),
        name="moe_experts",
    )(tile_expert, n_tiles, xs, w_gate, w_up, w_down)


def _final_kernel(x1_ref, y0_ref, y1_ref, w_ref, g_ref, o_ref):
    w = w_ref[...]
    x2 = x1_ref[...] + w[:, 0:1] * y0_ref[...] + w[:, 1:2] * y1_ref[...]
    o_ref[...] = x2 * lax.rsqrt(jnp.mean(x2 * x2, axis=1, keepdims=True) + EPS) * g_ref[...]


def _final(x1_all, yg0, yg1, wts, g_final, row0, n, tm):
    blk0 = row0 // tm
    rows = pl.BlockSpec((tm, D_MODEL), lambda i: (i + blk0, 0))
    return pl.pallas_call(
        _final_kernel,
        out_shape=jax.ShapeDtypeStruct((n, D_MODEL), F32),
        grid=(n // tm,),
        in_specs=[rows, rows, rows,
                  pl.BlockSpec((tm, LANES), lambda i: (i + blk0, 0)),
                  pl.BlockSpec((1, D_MODEL), lambda i: (0, 0))],
        out_specs=pl.BlockSpec((tm, D_MODEL), lambda i: (i, 0)),
        compiler_params=_cparams(("arbitrary",)),
        name="combine_final_norm",
    )(x1_all, yg0, yg1, wts, g_final)


def _s5_discretise(a_re, a_im, log_step, b_re, b_im):
    dt = jnp.exp(log_step)[:, None]
    mag = jnp.exp(a_re * dt)
    lb_re = mag * jnp.cos(a_im * dt)
    lb_im = mag * jnp.sin(a_im * dt)
    den = a_re * a_re + a_im * a_im
    nr = lb_re - 1.0
    coef_re = (nr * a_re + lb_im * a_im) / den
    coef_im = (lb_im * a_re - nr * a_im) / den
    bb_re = coef_re[..., None] * b_re - coef_im[..., None] * b_im
    bb_im = coef_re[..., None] * b_im + coef_im[..., None] * b_re
    return lb_re, lb_im, bb_re, bb_im


def _s5_chunk_params(a_re, a_im, log_step, b_re, b_im, c_re, c_im):
    lb_re, lb_im, bb_re, bb_im = _s5_discretise(a_re, a_im, log_step, b_re, b_im)
    L = S5_CHUNK
    gpb = LANES // S5_GROUP
    ncb = G_B // gpb
    pr, pi = [jnp.ones_like(lb_re)], [jnp.zeros_like(lb_re)]
    for _ in range(L):
        pr, pi = pr + [pr[-1] * lb_re - pi[-1] * lb_im], pi + [pr[-1] * lb_im + pi[-1] * lb_re]
    pw_r, pw_i = jnp.stack(pr), jnp.stack(pi)
    bt_re = bb_re.transpose(0, 2, 1)
    bt_im = bb_im.transpose(0, 2, 1)
    lbb_r = pw_r[:L, :, None, :] * bt_re - pw_i[:L, :, None, :] * bt_im
    lbb_i = pw_r[:L, :, None, :] * bt_im + pw_i[:L, :, None, :] * bt_re
    hp = lax.Precision.HIGHEST
    kdt = (jnp.einsum('gop,dgcp->dgco', c_re, lbb_r, precision=hp)
           - jnp.einsum('gop,dgcp->dgco', c_im, lbb_i, precision=hp))
    ct_re = c_re.transpose(0, 2, 1)
    ct_im = c_im.transpose(0, 2, 1)
    a_r = ct_re * pw_r[1:, :, :, None] - ct_im * pw_i[1:, :, :, None]
    a_i = -(ct_re * pw_i[1:, :, :, None] + ct_im * pw_r[1:, :, :, None])

    def blocks(m):
        return m.reshape(L, ncb, gpb * m.shape[2], m.shape[3])

    lbl_r = pw_r[L].reshape(ncb, 1, S5_CST)
    lbl_i = pw_i[L].reshape(ncb, 1, S5_CST)
    return blocks(lbb_r), blocks(lbb_i), blocks(kdt), blocks(a_r), blocks(a_i), lbl_r, lbl_i


def _s5_params(a_re, a_im, log_step, b_re, b_im, c_re, c_im):
    lb_re, lb_im, bb_re, bb_im = _s5_discretise(a_re, a_im, log_step, b_re, b_im)
    eye = jnp.eye(S5_GPB, dtype=F32)

    def blockdiag_in(m):
        m = m.reshape(S5_GB, S5_GPB, P_S5, S5_GROUP)
        return jnp.einsum('bgpc,gh->bgchp', m, eye).reshape(S5_GB, S5_UW, S5_XW)

    def blockdiag_out(m):
        m = m.reshape(S5_GB, S5_GPB, S5_GROUP, P_S5)
        return jnp.einsum('bgcp,gh->bgphc', m, eye).reshape(S5_GB, S5_XW, S5_UW)

    bb = jnp.concatenate([blockdiag_in(bb_re), blockdiag_in(bb_im)], axis=2).astype(BF16)
    cc = jnp.concatenate([blockdiag_out(c_re), -blockdiag_out(c_im)], axis=1).astype(BF16)
    lbr = lb_re.reshape(S5_GB, 1, S5_XW)
    lbi = lb_im.reshape(S5_GB, 1, S5_XW)
    return bb, cc, lbr, lbi


def _dispatch(ids, t_all, p_rows):
    e = ids[:, :2].reshape(-1)
    onehot = (e[:, None] == jnp.arange(N_EXPERTS, dtype=jnp.int32)[None, :]).astype(jnp.int32)
    csum = jnp.cumsum(onehot, axis=0)
    rank = jnp.sum((csum - onehot) * onehot, axis=1)
    counts = csum[-1]
    tiles = (counts + MOE_TILE - 1) // MOE_TILE
    tile_end = jnp.cumsum(tiles)
    tile_start = tile_end - tiles
    pos = jnp.sum(onehot * (tile_start * MOE_TILE)[None, :], axis=1) + rank
    tok = jnp.arange(2 * t_all, dtype=jnp.int32) // 2
    src = (jnp.arange(p_rows, dtype=jnp.int32) % t_all).at[pos].set(tok)
    n_tiles = tile_end[-1]
    tidx = jnp.arange(p_rows // MOE_TILE, dtype=jnp.int32)
    tclamp = jnp.minimum(tidx, n_tiles - 1)
    tile_expert = jnp.sum((tile_end[None, :] <= tclamp[:, None]).astype(jnp.int32), axis=1)
    return pos.reshape(t_all, 2), src, tile_expert, n_tiles.reshape(1).astype(jnp.int32)


def _pick_tile(n, candidates):
    for c in candidates:
        if n % c == 0:
            return c
    raise ValueError(f"no row tile for {n}")


def kernel(x_prompt, x_sample, state_mlstm_C, state_mlstm_n, state_mlstm_m, state_conv, state_s5_re,
           state_s5_im, norm_mix_g, w_in, b_i, b_f, w_conv, b_conv, head_norm_g, w_pa, s5_a_re, s5_a_im,
           s5_log_step, s5_b_re, s5_b_im, s5_c_re, s5_c_im, s5_d, s5_w_glu, s5_b_glu, w_pb, w_out,
           norm_ffn_g, w_rg, b_rg, w_rexp, b_rexp, w_gate, w_up, w_down, norm_final_g):
    assert state_mlstm_C.shape[0] == 1 and x_sample.shape[1] == 1
    batch, seq, _ = x_prompt.shape
    dec = x_sample.shape[0]
    t_p = batch * seq
    t_all = t_p + dec
    assert seq % CHUNK == 0 and t_p % dec == 0 and dec % LANES == 0
    d_a = H_A * DK
    tm_p = _pick_tile(t_p, (512, 256, 128))
    tm_all = _pick_tile(t_all, (640, 384, 128))
    sample_blk = t_p // dec

    xp = x_prompt.reshape(t_p, D_MODEL)
    xs = x_sample.reshape(dec, D_MODEL)

    g_mix = norm_mix_g[0]
    xn_all = _rmsnorm_rows(xp, g_mix, t_all, 0, tm_p)
    xn_all = _rmsnorm_rows(xs, g_mix, t_all, t_p, dec, alias=xn_all)
    w_in_t = w_in.reshape(w_in.shape[1:]).T
    n_qkvo = 4 * d_a
    tn = 1024
    qkvo = _matmul_t(xn_all, w_in_t, [j * tn for j in range(n_qkvo // tn)], jnp.zeros((1, n_qkvo), F32),
                     tm_all, tn, F32)
    n_gate_cols = 2 * H_A
    b_gates = jnp.pad(jnp.concatenate([b_i[0], b_f[0]]), (0, LANES - n_gate_cols)).reshape(1, LANES)
    gates = _matmul_t(xn_all, w_in_t, [n_qkvo], b_gates, tm_all, LANES, F32)
    c_u = n_qkvo + n_gate_cols
    c_ga = c_u + D_B
    starts = [c_ga + j * tn for j in range(2 * D_MODEL // tn)] + [c_u]
    proj2 = _matmul_t(xn_all, w_in_t, starts, jnp.zeros((1, len(starts) * tn), F32), tm_all, tn, F32)
    u_col0 = 2 * D_MODEL

    gates_t = gates[:t_p, :n_gate_cols].T
    head_g = head_norm_g[0].reshape(1, d_a)
    hn_all, c_p, n_p, m_p = _mlstm_prompt(qkvo, gates, gates_t, w_conv[0], b_conv[0].reshape(1, -1),
                                          head_g, batch, seq, t_all)
    conv_p = jnp.stack([qkvo[b * seq + seq - (CONV_W - 1):(b + 1) * seq, :2 * d_a] for b in range(batch)])

    bb, cc, lbr, lbi = _s5_params(s5_a_re[0], s5_a_im[0], s5_log_step[0], s5_b_re[0], s5_b_im[0],
                                  s5_c_re[0], s5_c_im[0])
    d_row = s5_d[0].reshape(1, D_B)
    chunk_w = _s5_chunk_params(s5_a_re[0], s5_a_im[0], s5_log_step[0], s5_b_re[0], s5_b_im[0],
                               s5_c_re[0], s5_c_im[0])
    ys_all, s5re_p, s5im_p = _s5_prompt(proj2, u_col0, *chunk_w, d_row, batch, seq, t_all)

    conv_s_in = state_conv[0].reshape(dec, (CONV_W - 1) * 2 * d_a)
    q_s, kw_s, a_e, s_e, den_e, conv_s, n_s, m_s = _sample_pre(
        qkvo, conv_s_in, w_conv[0], b_conv[0].reshape(1, -1), gates, state_mlstm_m[0],
        state_mlstm_n[0].reshape(dec, d_a), dec, sample_blk)
    v_s = qkvo[t_p:, 2 * d_a:3 * d_a]
    r3 = lambda a: a.reshape(dec, H_A, DK)
    c_s, qc = _sample_c(r3(q_s), r3(kw_s), r3(v_s), r3(a_e), state_mlstm_C[0], dec)
    hn_all = _sample_post(qc.reshape(dec, d_a), s_e, a_e, den_e, qkvo, head_g, hn_all, dec, sample_blk)

    ys_all, s5re_s, s5im_s = _s5_sample(proj2, u_col0, state_s5_re[0].reshape(dec, -1),
                                        state_s5_im[0].reshape(dec, -1),
                                        bb, cc, lbr, lbi, d_row, ys_all, dec, sample_blk)

    wr = jnp.pad(jnp.concatenate([w_rg[0], w_rexp[0]], axis=1), ((0, 0), (0, LANES - N_GROUPS - N_EXPERTS)))
    wr_hi = wr.astype(BF16)
    wr_lo = (wr - wr_hi.astype(F32)).astype(BF16)
    b_r = jnp.pad(jnp.concatenate([b_rg[0], b_rexp[0]]), (0, LANES - N_GROUPS - N_EXPERTS)).reshape(1, LANES)
    merge_w = (w_pa[0].astype(BF16), s5_w_glu[0].astype(BF16), w_pb[0].astype(BF16), w_out[0].astype(BF16))
    b_glu = s5_b_glu[0].reshape(1, D_B)
    g_ffn = norm_ffn_g[0].reshape(1, D_MODEL)
    tm_m = _pick_tile(t_p, (256, 128))
    outs = _merge(xp, hn_all, ys_all, proj2, b_glu, g_ffn, wr_hi, wr_lo, b_r, *merge_w, t_all, 0, tm_m)
    x1_all, xn2_all, ids, wts = _merge(xs, hn_all, ys_all, proj2, b_glu, g_ffn, wr_hi, wr_lo, b_r,
                                       *merge_w, t_all, t_p, dec, aliases=outs)

    p_rows = -(-(2 * t_all + N_EXPERTS * (MOE_TILE - 1)) // MOE_TILE) * MOE_TILE
    pos, src, tile_expert, n_tiles = _dispatch(ids, t_all, p_rows)
    take_rows = lambda a, idx: a.at[idx].get(mode='promise_in_bounds')
    xs_sorted = take_rows(xn2_all, src)
    yp = _moe_experts(tile_expert, n_tiles, xs_sorted, w_gate[0], w_up[0], w_down[0])
    yg0 = take_rows(yp, pos[:, 0])
    yg1 = take_rows(yp, pos[:, 1])

    g_fin = norm_final_g.reshape(1, D_MODEL)
    y_prompt = _final(x1_all, yg0, yg1, wts, g_fin, 0, t_p, tm_p).reshape(batch, seq, D_MODEL)
    y_sample = _final(x1_all, yg0, yg1, wts, g_fin, t_p, dec, dec).reshape(dec, 1, D_MODEL)

    lead = lambda a, shape: a.reshape((1,) + shape)
    return (y_prompt, y_sample,
            lead(c_p, (batch, H_A, DK, DK)), lead(n_p, (batch, H_A, DK)), lead(m_p[:, 0, :H_A], (batch, H_A)),
            lead(conv_p, (batch, CONV_W - 1, 2 * d_a)),
            lead(s5re_p, (batch, G_B, P_S5)), lead(s5im_p, (batch, G_B, P_S5)),
            lead(c_s, (dec, H_A, DK, DK)), lead(n_s, (dec, H_A, DK)), lead(m_s[:, :H_A], (dec, H_A)),
            lead(conv_s, (dec, CONV_W - 1, 2 * d_a)),
            lead(s5re_s, (dec, G_B, P_S5)), lead(s5im_s, (dec, G_B, P_S5)))
```

```python
import functools
import math

import jax
import jax.numpy as jnp
from jax import lax
from jax.experimental import pallas as pl
from jax.experimental.pallas import tpu as pltpu

F32 = jnp.float32
BF16 = jnp.bfloat16

D_MODEL = 2048
H_A = 8
DK = 256
CONV_W = 4
CHUNK = 128
D_B = 1024
S5_GROUP = 16
G_B = 64
P_S5 = 64
N_GROUPS = 4
EXP_PER_GROUP = 8
N_EXPERTS = 32
D_EXPERT = 512
EPS = 1e-6

LANES = 128
SUBLANES = 8
VMEM_LIMIT = 56 * 1024 * 1024

S5_GB = 4
S5_GPB = G_B // S5_GB
S5_UW = S5_GPB * S5_GROUP
S5_XW = S5_GPB * P_S5
S5_CHUNK = 8
S5_CST = (LANES // S5_GROUP) * P_S5
MOE_TILE = 256


def _cparams(sem):
    return pltpu.CompilerParams(dimension_semantics=sem, vmem_limit_bytes=VMEM_LIMIT)


def _silu(x):
    return x * (1.0 / (1.0 + jnp.exp(-x)))


def _sigmoid(x):
    return 1.0 / (1.0 + jnp.exp(-x))


def _log_sigmoid(x):
    return jnp.minimum(x, 0.0) - jnp.log1p(jnp.exp(-jnp.abs(x)))


def _gelu_tanh(x):
    c = math.sqrt(2.0 / math.pi)
    return 0.5 * x * (1.0 + jnp.tanh(c * (x + 0.044715 * (x * x * x))))


def _split3(x):
    hi = x.astype(BF16)
    r = x - hi.astype(F32)
    mid = r.astype(BF16)
    lo = (r - mid.astype(F32)).astype(BF16)
    return hi, mid, lo


def _dot(a, b):
    return jnp.dot(a, b, preferred_element_type=F32)


def _dot_nt(a, b):
    return lax.dot_general(a, b, (((1,), (1,)), ((), ())), preferred_element_type=F32)


def _dot_tn(a, b):
    return lax.dot_general(a, b, (((0,), (0,)), ((), ())), preferred_element_type=F32)


def _rmsnorm_kernel(x_ref, g_ref, *rest):
    o_ref = rest[-1]
    x = x_ref[...]
    r = lax.rsqrt(jnp.mean(x * x, axis=-1, keepdims=True) + EPS)
    o_ref[...] = (x * r * g_ref[...]).astype(o_ref.dtype)


def _rmsnorm_rows(x, g, t_all, row0, tm, alias=None):
    n = x.shape[0]
    blk0 = row0 // tm
    in_specs = [pl.BlockSpec((tm, D_MODEL), lambda i: (i, 0)),
                pl.BlockSpec((1, D_MODEL), lambda i: (0, 0))]
    args = [x, g.reshape(1, D_MODEL)]
    aliases = {}
    if alias is not None:
        in_specs.append(pl.BlockSpec(memory_space=pl.ANY))
        args.append(alias)
        aliases = {2: 0}
    return pl.pallas_call(
        _rmsnorm_kernel,
        out_shape=jax.ShapeDtypeStruct((t_all, D_MODEL), BF16),
        grid=(n // tm,),
        in_specs=in_specs,
        out_specs=pl.BlockSpec((tm, D_MODEL), lambda i: (i + blk0, 0)),
        input_output_aliases=aliases,
        compiler_params=_cparams(("arbitrary",)),
        name="rmsnorm_rows",
    )(*args)


def _mm_kernel(a_ref, w_ref, b_ref, o_ref, wb_ref):
    @pl.when(pl.program_id(1) == 0)
    def _():
        wb_ref[...] = w_ref[...].astype(BF16)

    o_ref[...] = (_dot(a_ref[...], wb_ref[...]) + b_ref[...]).astype(o_ref.dtype)


def _matmul(a, w, bias, n_out, tm, tn, out_dtype, col_blk0=0):
    m, k = a.shape
    return pl.pallas_call(
        _mm_kernel,
        out_shape=jax.ShapeDtypeStruct((m, n_out), out_dtype),
        grid=(n_out // tn, m // tm),
        in_specs=[pl.BlockSpec((tm, k), lambda j, i: (i, 0)),
                  pl.BlockSpec((k, tn), lambda j, i: (0, j + col_blk0)),
                  pl.BlockSpec((1, tn), lambda j, i: (0, j))],
        out_specs=pl.BlockSpec((tm, tn), lambda j, i: (i, j)),
        scratch_shapes=[pltpu.VMEM((k, tn), BF16)],
        compiler_params=_cparams(("arbitrary", "arbitrary")),
        name="rows_matmul",
    )(a, w, bias)


def _mm_t_kernel(starts_ref, a_ref, wt_ref, b_ref, o_ref, wb_ref):
    del starts_ref

    @pl.when(pl.program_id(1) == 0)
    def _():
        wb_ref[...] = wt_ref[...].astype(BF16)

    o_ref[...] = (_dot_nt(a_ref[...], wb_ref[...]) + b_ref[...]).astype(o_ref.dtype)


def _matmul_t(a, w_t, row_starts, bias, tm, tn, out_dtype):
    m, k = a.shape
    n_t = len(row_starts)
    assert all(s % SUBLANES == 0 for s in row_starts)
    grid_spec = pltpu.PrefetchScalarGridSpec(
        num_scalar_prefetch=1,
        grid=(n_t, m // tm),
        in_specs=[pl.BlockSpec((tm, k), lambda j, i, st: (i, 0)),
                  pl.BlockSpec((pl.Element(tn), pl.Element(k)), lambda j, i, st: (st[j] * SUBLANES, 0)),
                  pl.BlockSpec((1, tn), lambda j, i, st: (0, j))],
        out_specs=pl.BlockSpec((tm, tn), lambda j, i, st: (i, j)),
        scratch_shapes=[pltpu.VMEM((tn, k), BF16)],
    )
    return pl.pallas_call(
        _mm_t_kernel,
        out_shape=jax.ShapeDtypeStruct((m, n_t * tn), out_dtype),
        grid_spec=grid_spec,
        compiler_params=_cparams(("arbitrary", "arbitrary")),
        name="rows_matmul_t",
    )(jnp.asarray([s // SUBLANES for s in row_starts], jnp.int32), a, w_t, bias)


def _mm_shift_kernel(a_ref, wlo_ref, whi_ref, o_ref, wb_ref, *, shift):
    tn = wlo_ref.shape[1]
    rows = 256

    @pl.when(pl.program_id(1) == 0)
    def _():
        for r in range(wlo_ref.shape[0] // rows):
            sl = slice(r * rows, (r + 1) * rows)
            w = jnp.concatenate([wlo_ref[sl, :], whi_ref[sl, :]], axis=1)
            wb_ref[sl, :] = w[:, shift:shift + tn].astype(BF16)

    o_ref[...] = _dot(a_ref[...], wb_ref[...]).astype(o_ref.dtype)


def _matmul_shifted(a, w, col_starts, tm, tn, out_dtype):
    m, k = a.shape
    shift = col_starts[0] % tn
    assert all(c % tn == shift for c in col_starts) and 0 < shift < LANES
    lo_blk = [c // tn for c in col_starts]
    n_t = len(col_starts)
    base, first = min(lo_blk), lo_blk[0]
    assert all(lo_blk[j] == base + (j + first - base) % n_t for j in range(n_t))
    lo_idx = lambda j: base + (j + (first - base)) % n_t
    per = tn // LANES
    kern = functools.partial(_mm_shift_kernel, shift=shift)
    return pl.pallas_call(
        kern,
        out_shape=jax.ShapeDtypeStruct((m, n_t * tn), out_dtype),
        grid=(n_t, m // tm),
        in_specs=[pl.BlockSpec((tm, k), lambda j, i: (i, 0)),
                  pl.BlockSpec((k, tn), lambda j, i: (0, lo_idx(j))),
                  pl.BlockSpec((k, LANES), lambda j, i: (0, (lo_idx(j) + 1) * per))],
        out_specs=pl.BlockSpec((tm, tn), lambda j, i: (i, j)),
        scratch_shapes=[pltpu.VMEM((k, tn), BF16)],
        compiler_params=_cparams(("arbitrary", "arbitrary")),
        name="rows_matmul_shifted",
    )(a, w, w)


def _mlstm_kernel(qk_ref, v_ref, o_ref, gcol_ref, grow_ref, wc_ref, bc_ref, hg_ref,
                  h_ref, c_out, n_out, m_out, c_sc, n_sc, m_sc, ext_sc):
    c = pl.program_id(1)
    L = CHUNK
    pad = SUBLANES
    d_a = H_A * DK

    @pl.when(c == 0)
    def _():
        c_sc[...] = jnp.zeros_like(c_sc)
        n_sc[...] = jnp.zeros_like(n_sc)
        m_sc[...] = jnp.zeros_like(m_sc)
        ext_sc[0:pad, :] = jnp.zeros((pad, 2 * d_a), F32)

    x = qk_ref[...]
    ext_sc[pad:pad + L, :] = x
    y = bc_ref[...] + wc_ref[CONV_W - 1:CONV_W, :] * x
    for j in range(1, CONV_W):
        y = y + wc_ref[CONV_W - 1 - j:CONV_W - j, :] * ext_sc[pad - j:pad - j + L, :]
    ext_sc[0:pad, :] = x[L - pad:L, :]
    y = _silu(y)

    gcol = gcol_ref[...]
    grow = grow_ref[...]
    ri = lax.broadcasted_iota(jnp.int32, (L, L), 0)
    ci = lax.broadcasted_iota(jnp.int32, (L, L), 1)
    causal = ci <= ri
    tril = jnp.where(causal, 1.0, 0.0).astype(BF16)
    triu = jnp.where(ri <= ci, 1.0, 0.0).astype(BF16)
    b_cols = sum(_dot(tril, p) for p in _split3(_log_sigmoid(gcol)))
    b_rows = sum(_dot(p, triu) for p in _split3(_log_sigmoid(grow)))

    for h in range(H_A):
        sl = slice(h * DK, (h + 1) * DK)
        q = y[:, sl] * (DK ** -0.5)
        k = y[:, d_a + h * DK:d_a + (h + 1) * DK]
        ig_col = gcol[:, h:h + 1]
        ig_row = grow[h:h + 1, :]
        b_col = b_cols[:, H_A + h:H_A + h + 1]
        b_row = b_rows[H_A + h:H_A + h + 1, :]

        m_prev = m_sc[:, h:h + 1]
        d_log = jnp.where(causal, b_col - b_row + ig_row, -jnp.inf)
        inter_log = b_col + m_prev
        m_t = jnp.maximum(inter_log, jnp.max(d_log, axis=1, keepdims=True))
        qb = q.astype(BF16)
        kb = k.astype(BF16)
        vb = v_ref[:, sl].astype(BF16)
        s = _dot_nt(qb, kb) * jnp.exp(d_log - m_t)
        inter_w = jnp.exp(inter_log - m_t)
        c_prev = c_sc[h]
        n_prev = n_sc[h:h + 1, :]
        num = _dot(s.astype(BF16), vb) + inter_w * _dot(qb, c_prev.astype(BF16))
        nq = jnp.sum(s, axis=1, keepdims=True) + inter_w * jnp.sum(q * n_prev, axis=1, keepdims=True)
        den = jnp.maximum(jnp.abs(nq), jnp.exp(-m_t))
        hh = num / den
        hh = hh * _sigmoid(o_ref[:, sl])
        hh = hh * lax.rsqrt(jnp.mean(hh * hh, axis=1, keepdims=True) + EPS)
        h_ref[:, sl] = (hh * hg_ref[:, sl]).astype(h_ref.dtype)

        m_new = m_t[L - 1:L, :]
        b_last = b_col[L - 1:L, :]
        decay = jnp.exp(b_last + m_prev - m_new)
        w_end = jnp.exp(b_last - b_col + ig_col - m_new)
        kw = k * w_end
        c_sc[h] = decay * c_prev + _dot_tn(kw.astype(BF16), vb)
        n_sc[h:h + 1, :] = decay * n_prev + jnp.sum(kw, axis=0, keepdims=True)
        m_sc[:, h:h + 1] = m_new

    @pl.when(c == pl.num_programs(1) - 1)
    def _():
        c_out[...] = c_sc[...]
        n_out[...] = n_sc[...]
        m_out[...] = m_sc[...]


def _mlstm_prompt(qkvo, gates, gates_t, w_conv, b_conv, head_g, batch, seq, t_all):
    nc = seq // CHUNK
    L = CHUNK
    d_a = H_A * DK
    row = lambda b, c: b * nc + c
    in_specs = [
        pl.BlockSpec((L, 2 * d_a), lambda b, c: (row(b, c), 0)),
        pl.BlockSpec((L, d_a), lambda b, c: (row(b, c), 2)),
        pl.BlockSpec((L, d_a), lambda b, c: (row(b, c), 3)),
        pl.BlockSpec((L, LANES), lambda b, c: (row(b, c), 0)),
        pl.BlockSpec((2 * H_A, L), lambda b, c: (0, row(b, c))),
        pl.BlockSpec((CONV_W, 2 * d_a), lambda b, c: (0, 0)),
        pl.BlockSpec((1, 2 * d_a), lambda b, c: (0, 0)),
        pl.BlockSpec((1, d_a), lambda b, c: (0, 0)),
    ]
    out_shape = (
        jax.ShapeDtypeStruct((t_all, d_a), BF16),
        jax.ShapeDtypeStruct((batch, H_A, DK, DK), F32),
        jax.ShapeDtypeStruct((batch, H_A, DK), F32),
        jax.ShapeDtypeStruct((batch, 1, LANES), F32),
    )
    out_specs = (
        pl.BlockSpec((L, d_a), lambda b, c: (row(b, c), 0)),
        pl.BlockSpec((None, H_A, DK, DK), lambda b, c: (b, 0, 0, 0)),
        pl.BlockSpec((None, H_A, DK), lambda b, c: (b, 0, 0)),
        pl.BlockSpec((None, 1, LANES), lambda b, c: (b, 0, 0)),
    )
    return pl.pallas_call(
        _mlstm_kernel,
        out_shape=out_shape,
        grid=(batch, nc),
        in_specs=in_specs,
        out_specs=out_specs,
        scratch_shapes=[pltpu.VMEM((H_A, DK, DK), F32), pltpu.VMEM((H_A, DK), F32),
                        pltpu.VMEM((1, LANES), F32), pltpu.VMEM((SUBLANES + L, 2 * d_a), F32)],
        compiler_params=_cparams(("arbitrary", "arbitrary")),
        name="mlstm_prompt",
    )(qkvo, qkvo, qkvo, gates, gates_t, w_conv, b_conv, head_g)


def _lane_tile(x, reps):
    w = x.shape[1]
    ri = lax.broadcasted_iota(jnp.int32, (w, w * reps), 0)
    ci = lax.broadcasted_iota(jnp.int32, (w, w * reps), 1)
    rep = jnp.where(jnp.bitwise_and(ci, w - 1) == ri, 1.0, 0.0).astype(BF16)
    return _dot(x.astype(BF16), rep)


def _group_mask(rows, cols, row_per, col_per):
    ri = lax.broadcasted_iota(jnp.int32, (rows, cols), 0)
    ci = lax.broadcasted_iota(jnp.int32, (rows, cols), 1)
    return (jnp.right_shift(ri, int(math.log2(row_per))) == jnp.right_shift(ci, int(math.log2(col_per))))


def _s5_build_weights(bin_r, bin_i, kdt, aout_r, aout_i, winc_ref, wintra_ref, wout_ref):
    L = S5_CHUNK
    gpb = LANES // S5_GROUP
    m_inc = _group_mask(LANES, S5_CST, S5_GROUP, P_S5)
    m_lag = _group_mask(LANES, LANES, S5_GROUP, S5_GROUP)
    m_out = _group_mask(S5_CST, LANES, P_S5, S5_GROUP)
    zero = jnp.zeros((LANES, LANES), BF16)
    lag = [jnp.where(m_lag, _lane_tile(kdt[d], gpb), 0.0).astype(BF16) for d in range(L)]
    for t in range(L):
        rows = slice(t * LANES, (t + 1) * LANES)
        d = L - 1 - t
        winc_ref[rows, 0:S5_CST] = jnp.where(m_inc, _lane_tile(bin_r[d], gpb), 0.0).astype(BF16)
        winc_ref[rows, S5_CST:2 * S5_CST] = jnp.where(m_inc, _lane_tile(bin_i[d], gpb), 0.0).astype(BF16)
        wout_ref[0:S5_CST, rows] = jnp.where(m_out, _lane_tile(aout_r[t], gpb), 0.0).astype(BF16)
        wout_ref[S5_CST:2 * S5_CST, rows] = jnp.where(m_out, _lane_tile(aout_i[t], gpb), 0.0).astype(BF16)
        for t2 in range(L):
            wintra_ref[rows, t2 * LANES:(t2 + 1) * LANES] = lag[t2 - t] if t2 >= t else zero


def _s5_prompt_kernel(u_ref, binr_ref, bini_ref, kdt_ref, aoutr_ref, aouti_ref, lbr_ref, lbi_ref, d_ref,
                      ys_ref, sre_ref, sim_ref, x_sc, winc_ref, wintra_ref, wout_ref, *, batch, nchunk):
    L = S5_CHUNK
    nrow = batch * nchunk
    nst = S5_CST // LANES
    rstr = nchunk + SUBLANES
    _s5_build_weights(binr_ref, bini_ref, kdt_ref, aoutr_ref, aouti_ref, winc_ref, wintra_ref, wout_ref)
    u_t = [u_ref[pl.ds(t, nrow, stride=L), :] for t in range(L)]
    lhs = jnp.concatenate([a.astype(BF16) for a in u_t], axis=1)
    inc = _dot(lhs, winc_ref[...])
    for j in range(2 * nst):
        for b in range(batch):
            x_sc[j, b * rstr:b * rstr + nchunk, :] = inc[b * nchunk:(b + 1) * nchunk, j * LANES:(j + 1) * LANES]

    lbr = [jnp.broadcast_to(lbr_ref[:, j * LANES:(j + 1) * LANES], (batch, LANES)) for j in range(nst)]
    lbi = [jnp.broadcast_to(lbi_ref[:, j * LANES:(j + 1) * LANES], (batch, LANES)) for j in range(nst)]

    def scan_body(r, carry):
        rows = pl.ds(r, batch, stride=rstr)
        out = []
        for j in range(nst):
            xr, xi = carry[j]
            ir = x_sc[j, rows, :]
            ii = x_sc[nst + j, rows, :]
            x_sc[j, rows, :] = xr
            x_sc[nst + j, rows, :] = xi
            out.append((lbr[j] * xr - lbi[j] * xi + ir, lbr[j] * xi + lbi[j] * xr + ii))
        return tuple(out)

    z = jnp.zeros((batch, LANES), F32)
    fin = lax.fori_loop(0, nchunk, scan_body, tuple((z, z) for _ in range(nst)))
    for j in range(nst):
        sre_ref[:, j * LANES:(j + 1) * LANES] = fin[j][0]
        sim_ref[:, j * LANES:(j + 1) * LANES] = fin[j][1]

    xprev = jnp.concatenate(
        [jnp.concatenate([x_sc[j, b * rstr:b * rstr + nchunk, :] for b in range(batch)], axis=0)
         for j in range(2 * nst)], axis=1).astype(BF16)
    y = _dot(lhs, wintra_ref[...]) + _dot(xprev, wout_ref[...])
    for t in range(L):
        yt = y[:, t * LANES:(t + 1) * LANES] + d_ref[...] * u_t[t]
        ys_ref[pl.ds(t, nrow, stride=L), :] = _gelu_tanh(yt).astype(ys_ref.dtype)


def _s5_prompt(proj2, u_col0, bin_r, bin_i, kdt, aout_r, aout_i, lb8r, lb8i, d_row, batch, seq, t_all):
    nchunk = seq // S5_CHUNK
    t_p = batch * seq
    kern = functools.partial(_s5_prompt_kernel, batch=batch, nchunk=nchunk)
    ub0 = u_col0 // LANES
    ncb = D_B // LANES
    kw = S5_CHUNK * LANES
    per_blk = lambda a: pl.BlockSpec((a.shape[0], None) + a.shape[2:], lambda g: (0, g, 0, 0))
    return pl.pallas_call(
        kern,
        out_shape=(jax.ShapeDtypeStruct((t_all, D_B), F32),
                   jax.ShapeDtypeStruct((batch, G_B * P_S5), F32),
                   jax.ShapeDtypeStruct((batch, G_B * P_S5), F32)),
        grid=(ncb,),
        in_specs=[pl.BlockSpec((t_p, LANES), lambda g: (0, ub0 + g)),
                  per_blk(bin_r), per_blk(bin_i), per_blk(kdt), per_blk(aout_r), per_blk(aout_i),
                  pl.BlockSpec((None, 1, S5_CST), lambda g: (g, 0, 0)),
                  pl.BlockSpec((None, 1, S5_CST), lambda g: (g, 0, 0)),
                  pl.BlockSpec((1, LANES), lambda g: (0, g))],
        out_specs=(pl.BlockSpec((t_p, LANES), lambda g: (0, g)),
                   pl.BlockSpec((batch, S5_CST), lambda g: (0, g)),
                   pl.BlockSpec((batch, S5_CST), lambda g: (0, g))),
        scratch_shapes=[pltpu.VMEM((2 * S5_CST // LANES, batch * (nchunk + SUBLANES), LANES), F32),
                        pltpu.VMEM((kw, 2 * S5_CST), BF16), pltpu.VMEM((kw, kw), BF16),
                        pltpu.VMEM((2 * S5_CST, kw), BF16)],
        compiler_params=_cparams(("arbitrary",)),
        name="s5_prompt",
    )(proj2, bin_r, bin_i, kdt, aout_r, aout_i, lb8r, lb8i, d_row)


def _sample_pre_kernel(qk_ref, conv_ref, wc_ref, bc_ref, g_ref, m_ref, n_ref,
                       q_out, kw_out, a_out, s_out, den_out, conv_out, n_out, m_out):
    c2 = 2 * H_A * DK
    x_new = qk_ref[...]
    y = bc_ref[...] + wc_ref[CONV_W - 1:CONV_W, :] * x_new
    for j in range(CONV_W - 1):
        y = y + wc_ref[j:j + 1, :] * conv_ref[:, j * c2:(j + 1) * c2]
    y = _silu(y)
    conv_out[:, 0:(CONV_W - 2) * c2] = conv_ref[:, c2:(CONV_W - 1) * c2]
    conv_out[:, (CONV_W - 2) * c2:(CONV_W - 1) * c2] = x_new
    g = g_ref[...]
    bd = x_new.shape[0]
    m_cols = []
    for h in range(H_A):
        sl = slice(h * DK, (h + 1) * DK)
        q = y[:, sl] * (DK ** -0.5)
        k = y[:, H_A * DK + h * DK:H_A * DK + (h + 1) * DK]
        ig = g[:, h:h + 1]
        lf = _log_sigmoid(g[:, H_A + h:H_A + h + 1])
        m_prev = m_ref[:, h:h + 1]
        m_t = jnp.maximum(lf + m_prev, ig)
        a = jnp.exp(lf + m_prev - m_t)
        wgt = jnp.exp(ig - m_t)
        n_prev = n_ref[:, sl]
        s = jnp.sum(q * k, axis=1, keepdims=True) * wgt
        nq = s + a * jnp.sum(q * n_prev, axis=1, keepdims=True)
        den = jnp.maximum(jnp.abs(nq), jnp.exp(-m_t))
        kw = wgt * k
        q_out[:, sl] = q
        kw_out[:, sl] = kw
        a_out[:, sl] = jnp.broadcast_to(a, (bd, DK))
        s_out[:, sl] = jnp.broadcast_to(s, (bd, DK))
        den_out[:, sl] = jnp.broadcast_to(den, (bd, DK))
        n_out[:, sl] = a * n_prev + kw
        m_cols.append(m_t)
    lane = lax.broadcasted_iota(jnp.int32, (bd, LANES), 1)
    m_full = jnp.zeros((bd, LANES), F32)
    for h in range(H_A):
        m_full = jnp.where(lane == h, m_cols[h], m_full)
    m_out[...] = m_full


def _sample_pre(qkvo, conv_state, w_conv, b_conv, gates, m_state, n_state, dec, row_blk):
    c2 = 2 * H_A * DK
    d = H_A * DK
    full = lambda shape: pl.BlockSpec(shape, lambda i: (0,) * len(shape))
    rows = lambda: jax.ShapeDtypeStruct((dec, d), F32)
    return pl.pallas_call(
        _sample_pre_kernel,
        out_shape=(rows(), rows(), rows(), rows(), rows(),
                   jax.ShapeDtypeStruct((dec, (CONV_W - 1) * c2), F32), rows(),
                   jax.ShapeDtypeStruct((dec, LANES), F32)),
        grid=(1,),
        in_specs=[pl.BlockSpec((dec, c2), lambda i: (row_blk, 0)),
                  full((dec, (CONV_W - 1) * c2)), full((CONV_W, c2)), full((1, c2)),
                  pl.BlockSpec((dec, LANES), lambda i: (row_blk, 0)),
                  full((dec, H_A)), full((dec, d))],
        out_specs=(full((dec, d)), full((dec, d)), full((dec, d)), full((dec, d)), full((dec, d)),
                   full((dec, (CONV_W - 1) * c2)), full((dec, d)), full((dec, LANES))),
        compiler_params=_cparams(("arbitrary",)),
        name="sample_pre",
    )(qkvo, conv_state, w_conv, b_conv, gates, m_state, n_state)


def _sample_c_kernel(q_ref, kw_ref, v_ref, a_ref, c_ref, c_out, qc_out):
    eye = jnp.where(lax.broadcasted_iota(jnp.int32, (DK, DK), 0)
                    == lax.broadcasted_iota(jnp.int32, (DK, DK), 1), 1.0, 0.0).astype(BF16)
    for b in range(q_ref.shape[0]):
        q_t = sum(_dot_nt(eye, p) for p in _split3(q_ref[b]))
        kw_t = sum(_dot_nt(eye, p) for p in _split3(kw_ref[b]))
        qc_rows = []
        for h in range(H_A):
            c_prev = c_ref[b, h]
            qc_rows.append(jnp.sum(q_t[:, h:h + 1] * c_prev, axis=0, keepdims=True))
            c_out[b, h] = a_ref[b, h:h + 1, :] * c_prev + kw_t[:, h:h + 1] * v_ref[b, h:h + 1, :]
        qc_out[b] = jnp.concatenate(qc_rows, axis=0)


SAMPLE_C_BATCH = 4


def _sample_c(q3, kw3, v3, a3, c_state, dec):
    nb = SAMPLE_C_BATCH
    vec = pl.BlockSpec((nb, H_A, DK), lambda b: (b, 0, 0))
    mat = pl.BlockSpec((nb, H_A, DK, DK), lambda b: (b, 0, 0, 0))
    return pl.pallas_call(
        _sample_c_kernel,
        out_shape=(jax.ShapeDtypeStruct((dec, H_A, DK, DK), F32),
                   jax.ShapeDtypeStruct((dec, H_A, DK), F32)),
        grid=(dec // nb,),
        in_specs=[vec, vec, vec, vec, mat],
        out_specs=(mat, vec),
        compiler_params=_cparams(("arbitrary",)),
        name="sample_c_update",
    )(q3, kw3, v3, a3, c_state)


def _sample_post_kernel(qc_ref, s_ref, a_ref, den_ref, vo_ref, hg_ref, hn_in, h_ref):
    del hn_in
    d = H_A * DK
    num = s_ref[...] * vo_ref[:, 0:d] + a_ref[...] * qc_ref[...]
    hh = num / den_ref[...]
    hh = hh * _sigmoid(vo_ref[:, d:2 * d])
    for h in range(H_A):
        sl = slice(h * DK, (h + 1) * DK)
        seg = hh[:, sl]
        seg = seg * lax.rsqrt(jnp.mean(seg * seg, axis=1, keepdims=True) + EPS)
        h_ref[:, sl] = (seg * hg_ref[:, sl]).astype(h_ref.dtype)


def _sample_post(qc, s_e, a_e, den_e, qkvo, head_g, hn_all, dec, row_blk):
    d = H_A * DK
    full = lambda shape: pl.BlockSpec(shape, lambda i: (0,) * len(shape))
    return pl.pallas_call(
        _sample_post_kernel,
        out_shape=jax.ShapeDtypeStruct(hn_all.shape, hn_all.dtype),
        grid=(1,),
        in_specs=[full((dec, d)), full((dec, d)), full((dec, d)), full((dec, d)),
                  pl.BlockSpec((dec, 2 * d), lambda i: (row_blk, 1)),
                  full((1, d)), pl.BlockSpec(memory_space=pl.ANY)],
        out_specs=pl.BlockSpec((dec, d), lambda i: (row_blk, 0)),
        input_output_aliases={6: 0},
        compiler_params=_cparams(("arbitrary",)),
        name="sample_post",
    )(qc, s_e, a_e, den_e, qkvo, head_g, hn_all)


def _s5_sample_kernel(u_ref, sr_ref, si_ref, bb_ref, cc_ref, lbr_ref, lbi_ref, d_ref, ys_in,
                      ys_ref, sre_out, sim_out):
    del ys_in
    for g in range(S5_GB):
        u = u_ref[:, g * S5_UW:(g + 1) * S5_UW]
        bu = _dot(u.astype(BF16), bb_ref[g])
        sl = slice(g * S5_XW, (g + 1) * S5_XW)
        lbr = lbr_ref[g]
        lbi = lbi_ref[g]
        sr = sr_ref[:, sl]
        si = si_ref[:, sl]
        xr = lbr * sr - lbi * si + bu[:, 0:S5_XW]
        xi = lbr * si + lbi * sr + bu[:, S5_XW:2 * S5_XW]
        sre_out[:, sl] = xr
        sim_out[:, sl] = xi
        x = jnp.concatenate([xr, xi], axis=1).astype(BF16)
        y = _dot(x, cc_ref[g]) + d_ref[:, g * S5_UW:(g + 1) * S5_UW] * u
        ys_ref[:, g * S5_UW:(g + 1) * S5_UW] = _gelu_tanh(y).astype(ys_ref.dtype)


def _s5_sample(proj2, u_col0, s_re, s_im, bb, cc, lbr, lbi, d_row, ys_all, dec, row_blk):
    full = lambda shape: pl.BlockSpec(shape, lambda i: (0,) * len(shape))
    n_state = G_B * P_S5
    ub0 = u_col0 // D_B
    return pl.pallas_call(
        _s5_sample_kernel,
        out_shape=(jax.ShapeDtypeStruct(ys_all.shape, ys_all.dtype),
                   jax.ShapeDtypeStruct((dec, n_state), F32),
                   jax.ShapeDtypeStruct((dec, n_state), F32)),
        grid=(1,),
        in_specs=[pl.BlockSpec((dec, D_B), lambda i: (row_blk, ub0)),
                  full((dec, n_state)), full((dec, n_state)),
                  full(bb.shape), full(cc.shape), full(lbr.shape), full(lbi.shape), full((1, D_B)),
                  pl.BlockSpec(memory_space=pl.ANY)],
        out_specs=(pl.BlockSpec((dec, D_B), lambda i: (row_blk, 0)),
                   full((dec, n_state)), full((dec, n_state))),
        input_output_aliases={8: 0},
        compiler_params=_cparams(("arbitrary",)),
        name="s5_sample",
    )(proj2, s_re, s_im, bb, cc, lbr, lbi, d_row, ys_all)


def _merge_kernel(x_ref, hn_ref, ys_ref, ga_ref, gb_ref, bglu_ref, gffn_ref, wrh_ref, wrl_ref, br_ref,
                  wpa_hbm, wglu_hbm, wpb_hbm, wout_hbm, *rest):
    n_alias = len(rest) - 9
    x1_ref, xn_ref, ids_ref, wts_ref = rest[n_alias:n_alias + 4]
    wpa, wglu, wpb, wout, sem = rest[n_alias + 4:]

    @pl.when(pl.program_id(0) == 0)
    def _():
        copies = [pltpu.make_async_copy(src, dst, sem.at[i])
                  for i, (src, dst) in enumerate(((wpa_hbm, wpa), (wglu_hbm, wglu),
                                                  (wpb_hbm, wpb), (wout_hbm, wout)))]
        for cp in copies:
            cp.start()
        for cp in copies:
            cp.wait()

    ya = _dot(hn_ref[...], wpa[...])
    ys = ys_ref[...]
    gate = _sigmoid(_dot(ys.astype(BF16), wglu[...]) + bglu_ref[...])
    yb = _dot((ys * gate).astype(BF16), wpb[...])
    z = _sigmoid(ga_ref[...]) * ya + _sigmoid(gb_ref[...]) * yb
    x1 = x_ref[...] + _dot(z.astype(BF16), wout[...])
    x1_ref[...] = x1
    xn = x1 * lax.rsqrt(jnp.mean(x1 * x1, axis=1, keepdims=True) + EPS) * gffn_ref[...]
    xn_ref[...] = xn
    xh = xn.astype(BF16)
    xl = (xn - xh.astype(F32)).astype(BF16)
    logits = _dot(xh, wrh_ref[...]) + _dot(xl, wrh_ref[...]) + _dot(xh, wrl_ref[...]) + br_ref[...]

    lane_i = lax.broadcasted_iota(jnp.int32, logits.shape, 1)
    lane = lane_i.astype(F32)
    neg = -jnp.inf
    big = float(1 << 20)
    gl = jnp.where(lane_i < N_GROUPS, logits, neg)
    gmax = jnp.max(gl, axis=1, keepdims=True)
    gsum = jnp.sum(jnp.exp(gl - gmax), axis=1, keepdims=True)
    gidx = jnp.min(jnp.where(gl == gmax, lane, big), axis=1, keepdims=True)
    pg_sel = 1.0 / gsum
    lo = N_GROUPS + gidx * EXP_PER_GROUP
    in_grp = (lane >= lo) & (lane < lo + EXP_PER_GROUP)
    el = jnp.where(in_grp, logits, neg)
    emax = jnp.max(el, axis=1, keepdims=True)
    ee = jnp.exp(el - emax)
    pe = ee / jnp.sum(ee, axis=1, keepdims=True)
    v0 = jnp.max(pe, axis=1, keepdims=True)
    i0 = jnp.min(jnp.where(in_grp & (pe == v0), lane, big), axis=1, keepdims=True)
    rest_m = in_grp & (lane != i0)
    pe1 = jnp.where(rest_m, pe, neg)
    v1 = jnp.max(pe1, axis=1, keepdims=True)
    i1 = jnp.min(jnp.where(rest_m & (pe1 == v1), lane, big), axis=1, keepdims=True)
    tot = v0 + v1
    w0 = pg_sel * (v0 / tot)
    w1 = pg_sel * (v1 / tot)
    ids = jnp.where(lane_i == 0, i0 - N_GROUPS, jnp.where(lane_i == 1, i1 - N_GROUPS, 0.0))
    ids_ref[...] = ids.astype(jnp.int32)
    wts_ref[...] = jnp.where(lane_i == 0, w0, jnp.where(lane_i == 1, w1, 0.0))


def _merge(x, hn_all, ys_all, proj2, b_glu, g_ffn, wr_hi, wr_lo, b_r, wpa, wglu, wpb, wout,
           t_all, row0, tm, aliases=None):
    n = x.shape[0]
    blk0 = row0 // tm
    const = lambda shape: pl.BlockSpec(shape, lambda i: (0,) * len(shape))
    any_spec = pl.BlockSpec(memory_space=pl.ANY)
    in_specs = [pl.BlockSpec((tm, D_MODEL), lambda i: (i, 0)),
                pl.BlockSpec((tm, D_MODEL), lambda i: (i + blk0, 0)),
                pl.BlockSpec((tm, D_B), lambda i: (i + blk0, 0)),
                pl.BlockSpec((tm, D_MODEL), lambda i: (i + blk0, 0)),
                pl.BlockSpec((tm, D_MODEL), lambda i: (i + blk0, 1)),
                const((1, D_B)), const((1, D_MODEL)),
                const((D_MODEL, LANES)), const((D_MODEL, LANES)), const((1, LANES)),
                any_spec, any_spec, any_spec, any_spec]
    args = [x, hn_all, ys_all, proj2, proj2, b_glu, g_ffn, wr_hi, wr_lo, b_r, wpa, wglu, wpb, wout]
    io_alias = {}
    if aliases is not None:
        for j, a in enumerate(aliases):
            in_specs.append(any_spec)
            args.append(a)
            io_alias[14 + j] = j
    out_shape = (jax.ShapeDtypeStruct((t_all, D_MODEL), F32),
                 jax.ShapeDtypeStruct((t_all, D_MODEL), F32),
                 jax.ShapeDtypeStruct((t_all, LANES), jnp.int32),
                 jax.ShapeDtypeStruct((t_all, LANES), F32))
    out_specs = (pl.BlockSpec((tm, D_MODEL), lambda i: (i + blk0, 0)),
                 pl.BlockSpec((tm, D_MODEL), lambda i: (i + blk0, 0)),
                 pl.BlockSpec((tm, LANES), lambda i: (i + blk0, 0)),
                 pl.BlockSpec((tm, LANES), lambda i: (i + blk0, 0)))
    return pl.pallas_call(
        _merge_kernel,
        out_shape=out_shape,
        grid=(n // tm,),
        in_specs=in_specs,
        out_specs=out_specs,
        scratch_shapes=[pltpu.VMEM(wpa.shape, BF16), pltpu.VMEM(wglu.shape, BF16),
                        pltpu.VMEM(wpb.shape, BF16), pltpu.VMEM(wout.shape, BF16),
                        pltpu.SemaphoreType.DMA((4,))],
        input_output_aliases=io_alias,
        compiler_params=_cparams(("arbitrary",)),
        name="merge_router",
    )(*args)


def _moe_kernel(nt_ref, first_ref, ord_ref, elist_ref, nord_ref, xs_ref, wg_hbm, wu_hbm, wd_hbm, o_ref,
                stg_g, stg_u, stg_d, wg_sc, wu_sc, wd_sc, sem):
    i = pl.program_id(0)
    n_ord = nord_ref[0]

    def weight_copies(k, slot):
        e = elist_ref[k]
        return (pltpu.make_async_copy(wg_hbm.at[e], stg_g.at[slot], sem.at[slot, 0]),
                pltpu.make_async_copy(wu_hbm.at[e], stg_u.at[slot], sem.at[slot, 1]),
                pltpu.make_async_copy(wd_hbm.at[e], stg_d.at[slot], sem.at[slot, 2]))

    @pl.when(i == 0)
    def _():
        for cp in weight_copies(0, 0):
            cp.start()

        @pl.when(n_ord > 1)
        def _():
            for cp in weight_copies(1, 1):
                cp.start()

    valid = i < nt_ref[0]
    k = ord_ref[i]

    @pl.when(valid & (first_ref[i] == 1))
    def _():
        slot = k % 2
        for cp in weight_copies(k, slot):
            cp.wait()
        wg_sc[...] = stg_g[slot].astype(BF16)
        wu_sc[...] = stg_u[slot].astype(BF16)
        wd_sc[...] = stg_d[slot].astype(BF16)

        @pl.when(k + 2 < n_ord)
        def _():
            for cp in weight_copies(k + 2, slot):
                cp.start()

    @pl.when(valid)
    def _():
        x = xs_ref[...].astype(BF16)
        hg = _dot(x, wg_sc[...])
        hu = _dot(x, wu_sc[...])
        hh = (_silu(hg) * hu).astype(BF16)
        o_ref[...] = _dot(hh, wd_sc[...])


def _moe_experts(n_tiles, first, ordinal, elist, n_ord, xs, w_gate, w_up, w_down):
    p_rows = xs.shape[0]
    last = lambda i, nt, *_: jnp.minimum(i, nt[0] - 1)
    any_spec = pl.BlockSpec(memory_space=pl.ANY)
    grid_spec = pltpu.PrefetchScalarGridSpec(
        num_scalar_prefetch=5,
        grid=(p_rows // MOE_TILE,),
        in_specs=[pl.BlockSpec((MOE_TILE, D_MODEL), lambda i, *s: (last(i, *s), 0)),
                  any_spec, any_spec, any_spec],
        out_specs=pl.BlockSpec((MOE_TILE, D_MODEL), lambda i, *s: (last(i, *s), 0)),
        scratch_shapes=[pltpu.VMEM((2, D_MODEL, D_EXPERT), F32), pltpu.VMEM((2, D_MODEL, D_EXPERT), F32),
                        pltpu.VMEM((2, D_EXPERT, D_MODEL), F32),
                        pltpu.VMEM((D_MODEL, D_EXPERT), BF16), pltpu.VMEM((D_MODEL, D_EXPERT), BF16),
                        pltpu.VMEM((D_EXPERT, D_MODEL), BF16),
                        pltpu.SemaphoreType.DMA((2, 3))],
    )
    return pl.pallas_call(
        _moe_kernel,
        out_shape=jax.ShapeDtypeStruct((p_rows, D_MODEL), F32),
        grid_spec=grid_spec,
        compiler_params=_cparams(("arbitrary",)),
        name="moe_experts",
    )(n_tiles, first, ordinal, elist, n_ord, xs, w_gate, w_up, w_down)


def _final_kernel(x1_ref, y0_ref, y1_ref, w_ref, g_ref, o_ref):
    w = w_ref[...]
    x2 = x1_ref[...] + w[:, 0:1] * y0_ref[...] + w[:, 1:2] * y1_ref[...]
    o_ref[...] = x2 * lax.rsqrt(jnp.mean(x2 * x2, axis=1, keepdims=True) + EPS) * g_ref[...]


def _final(x1_all, yg0, yg1, wts, g_final, row0, n, tm):
    blk0 = row0 // tm
    rows = pl.BlockSpec((tm, D_MODEL), lambda i: (i + blk0, 0))
    return pl.pallas_call(
        _final_kernel,
        out_shape=jax.ShapeDtypeStruct((n, D_MODEL), F32),
        grid=(n // tm,),
        in_specs=[rows, rows, rows,
                  pl.BlockSpec((tm, LANES), lambda i: (i + blk0, 0)),
                  pl.BlockSpec((1, D_MODEL), lambda i: (0, 0))],
        out_specs=pl.BlockSpec((tm, D_MODEL), lambda i: (i, 0)),
        compiler_params=_cparams(("arbitrary",)),
        name="combine_final_norm",
    )(x1_all, yg0, yg1, wts, g_final)


def _s5_discretise(a_re, a_im, log_step, b_re, b_im):
    dt = jnp.exp(log_step)[:, None]
    mag = jnp.exp(a_re * dt)
    lb_re = mag * jnp.cos(a_im * dt)
    lb_im = mag * jnp.sin(a_im * dt)
    den = a_re * a_re + a_im * a_im
    nr = lb_re - 1.0
    coef_re = (nr * a_re + lb_im * a_im) / den
    coef_im = (lb_im * a_re - nr * a_im) / den
    bb_re = coef_re[..., None] * b_re - coef_im[..., None] * b_im
    bb_im = coef_re[..., None] * b_im + coef_im[..., None] * b_re
    return lb_re, lb_im, bb_re, bb_im


def _s5_chunk_params(a_re, a_im, log_step, b_re, b_im, c_re, c_im):
    lb_re, lb_im, bb_re, bb_im = _s5_discretise(a_re, a_im, log_step, b_re, b_im)
    L = S5_CHUNK
    gpb = LANES // S5_GROUP
    ncb = G_B // gpb
    pr, pi = [jnp.ones_like(lb_re)], [jnp.zeros_like(lb_re)]
    for _ in range(L):
        pr, pi = pr + [pr[-1] * lb_re - pi[-1] * lb_im], pi + [pr[-1] * lb_im + pi[-1] * lb_re]
    pw_r, pw_i = jnp.stack(pr), jnp.stack(pi)
    bt_re = bb_re.transpose(0, 2, 1)
    bt_im = bb_im.transpose(0, 2, 1)
    lbb_r = pw_r[:L, :, None, :] * bt_re - pw_i[:L, :, None, :] * bt_im
    lbb_i = pw_r[:L, :, None, :] * bt_im + pw_i[:L, :, None, :] * bt_re
    hp = lax.Precision.HIGHEST
    kdt = (jnp.einsum('gop,dgcp->dgco', c_re, lbb_r, precision=hp)
           - jnp.einsum('gop,dgcp->dgco', c_im, lbb_i, precision=hp))
    ct_re = c_re.transpose(0, 2, 1)
    ct_im = c_im.transpose(0, 2, 1)
    a_r = ct_re * pw_r[1:, :, :, None] - ct_im * pw_i[1:, :, :, None]
    a_i = -(ct_re * pw_i[1:, :, :, None] + ct_im * pw_r[1:, :, :, None])

    def blocks(m):
        return m.reshape(L, ncb, gpb * m.shape[2], m.shape[3])

    lbl_r = pw_r[L].reshape(ncb, 1, S5_CST)
    lbl_i = pw_i[L].reshape(ncb, 1, S5_CST)
    return blocks(lbb_r), blocks(lbb_i), blocks(kdt), blocks(a_r), blocks(a_i), lbl_r, lbl_i


def _s5_params(a_re, a_im, log_step, b_re, b_im, c_re, c_im):
    lb_re, lb_im, bb_re, bb_im = _s5_discretise(a_re, a_im, log_step, b_re, b_im)
    eye = jnp.eye(S5_GPB, dtype=F32)

    def blockdiag_in(m):
        m = m.reshape(S5_GB, S5_GPB, P_S5, S5_GROUP)
        return jnp.einsum('bgpc,gh->bgchp', m, eye).reshape(S5_GB, S5_UW, S5_XW)

    def blockdiag_out(m):
        m = m.reshape(S5_GB, S5_GPB, S5_GROUP, P_S5)
        return jnp.einsum('bgcp,gh->bgphc', m, eye).reshape(S5_GB, S5_XW, S5_UW)

    bb = jnp.concatenate([blockdiag_in(bb_re), blockdiag_in(bb_im)], axis=2).astype(BF16)
    cc = jnp.concatenate([blockdiag_out(c_re), -blockdiag_out(c_im)], axis=1).astype(BF16)
    lbr = lb_re.reshape(S5_GB, 1, S5_XW)
    lbi = lb_im.reshape(S5_GB, 1, S5_XW)
    return bb, cc, lbr, lbi


def _dispatch(ids, t_all, p_rows):
    e = ids[:, :2].reshape(-1)
    onehot = (e[:, None] == jnp.arange(N_EXPERTS, dtype=jnp.int32)[None, :]).astype(jnp.int32)
    csum = jnp.cumsum(onehot, axis=0)
    rank = jnp.sum((csum - onehot) * onehot, axis=1)
    counts = csum[-1]
    tiles = (counts + MOE_TILE - 1) // MOE_TILE
    tile_end = jnp.cumsum(tiles)
    tile_start = tile_end - tiles
    pos = jnp.sum(onehot * (tile_start * MOE_TILE)[None, :], axis=1) + rank
    tok = jnp.arange(2 * t_all, dtype=jnp.int32) // 2
    src = (jnp.arange(p_rows, dtype=jnp.int32) % t_all).at[pos].set(tok)
    n_tiles = tile_end[-1]
    tidx = jnp.arange(p_rows // MOE_TILE, dtype=jnp.int32)
    tclamp = jnp.minimum(tidx, n_tiles - 1)
    tile_expert = jnp.sum((tile_end[None, :] <= tclamp[:, None]).astype(jnp.int32), axis=1)
    present = (tiles > 0).astype(jnp.int32)
    ord_of_e = jnp.cumsum(present) - 1
    eids = jnp.arange(N_EXPERTS, dtype=jnp.int32)
    elist = jnp.sum(jnp.where((ord_of_e[None, :] == eids[:, None]) & (present[None, :] == 1), eids[None, :], 0), axis=1)
    ordinal = jnp.sum(jnp.where(tile_expert[:, None] == eids[None, :], ord_of_e[None, :], 0), axis=1)
    first = jnp.concatenate([jnp.ones((1,), jnp.int32),
                             (tile_expert[1:] != tile_expert[:-1]).astype(jnp.int32)])
    one = lambda v: v.reshape(1).astype(jnp.int32)
    meta = (one(n_tiles), first, ordinal.astype(jnp.int32), elist.astype(jnp.int32), one(jnp.sum(present)))
    return pos.reshape(t_all, 2), src, meta


def _pick_tile(n, candidates):
    for c in candidates:
        if n % c == 0:
            return c
    raise ValueError(f"no row tile for {n}")


def kernel(x_prompt, x_sample, state_mlstm_C, state_mlstm_n, state_mlstm_m, state_conv, state_s5_re,
           state_s5_im, norm_mix_g, w_in, b_i, b_f, w_conv, b_conv, head_norm_g, w_pa, s5_a_re, s5_a_im,
           s5_log_step, s5_b_re, s5_b_im, s5_c_re, s5_c_im, s5_d, s5_w_glu, s5_b_glu, w_pb, w_out,
           norm_ffn_g, w_rg, b_rg, w_rexp, b_rexp, w_gate, w_up, w_down, norm_final_g):
    assert state_mlstm_C.shape[0] == 1 and x_sample.shape[1] == 1
    batch, seq, _ = x_prompt.shape
    dec = x_sample.shape[0]
    t_p = batch * seq
    t_all = t_p + dec
    assert seq % CHUNK == 0 and t_p % dec == 0 and dec % LANES == 0
    d_a = H_A * DK
    tm_p = _pick_tile(t_p, (512, 256, 128))
    tm_all = _pick_tile(t_all, (640, 384, 128))
    sample_blk = t_p // dec

    xp = x_prompt.reshape(t_p, D_MODEL)
    xs = x_sample.reshape(dec, D_MODEL)

    g_mix = norm_mix_g[0]
    xn_all = _rmsnorm_rows(xp, g_mix, t_all, 0, tm_p)
    xn_all = _rmsnorm_rows(xs, g_mix, t_all, t_p, dec, alias=xn_all)
    w_in_t = w_in.reshape(w_in.shape[1:]).T
    n_qkvo = 4 * d_a
    tn = 1024
    qkvo = _matmul_t(xn_all, w_in_t, [j * tn for j in range(n_qkvo // tn)], jnp.zeros((1, n_qkvo), F32),
                     tm_all, tn, F32)
    n_gate_cols = 2 * H_A
    b_gates = jnp.pad(jnp.concatenate([b_i[0], b_f[0]]), (0, LANES - n_gate_cols)).reshape(1, LANES)
    gates = _matmul_t(xn_all, w_in_t, [n_qkvo], b_gates, tm_all, LANES, F32)
    c_u = n_qkvo + n_gate_cols
    c_ga = c_u + D_B
    starts = [c_ga + j * tn for j in range(2 * D_MODEL // tn)] + [c_u]
    proj2 = _matmul_t(xn_all, w_in_t, starts, jnp.zeros((1, len(starts) * tn), F32), tm_all, tn, F32)
    u_col0 = 2 * D_MODEL

    gates_t = gates[:t_p, :n_gate_cols].T
    head_g = head_norm_g[0].reshape(1, d_a)
    hn_all, c_p, n_p, m_p = _mlstm_prompt(qkvo, gates, gates_t, w_conv[0], b_conv[0].reshape(1, -1),
                                          head_g, batch, seq, t_all)
    conv_p = jnp.stack([qkvo[b * seq + seq - (CONV_W - 1):(b + 1) * seq, :2 * d_a] for b in range(batch)])

    bb, cc, lbr, lbi = _s5_params(s5_a_re[0], s5_a_im[0], s5_log_step[0], s5_b_re[0], s5_b_im[0],
                                  s5_c_re[0], s5_c_im[0])
    d_row = s5_d[0].reshape(1, D_B)
    chunk_w = _s5_chunk_params(s5_a_re[0], s5_a_im[0], s5_log_step[0], s5_b_re[0], s5_b_im[0],
                               s5_c_re[0], s5_c_im[0])
    ys_all, s5re_p, s5im_p = _s5_prompt(proj2, u_col0, *chunk_w, d_row, batch, seq, t_all)

    conv_s_in = state_conv[0].reshape(dec, (CONV_W - 1) * 2 * d_a)
    q_s, kw_s, a_e, s_e, den_e, conv_s, n_s, m_s = _sample_pre(
        qkvo, conv_s_in, w_conv[0], b_conv[0].reshape(1, -1), gates, state_mlstm_m[0],
        state_mlstm_n[0].reshape(dec, d_a), dec, sample_blk)
    v_s = qkvo[t_p:, 2 * d_a:3 * d_a]
    r3 = lambda a: a.reshape(dec, H_A, DK)
    c_s, qc = _sample_c(r3(q_s), r3(kw_s), r3(v_s), r3(a_e), state_mlstm_C[0], dec)
    hn_all = _sample_post(qc.reshape(dec, d_a), s_e, a_e, den_e, qkvo, head_g, hn_all, dec, sample_blk)

    ys_all, s5re_s, s5im_s = _s5_sample(proj2, u_col0, state_s5_re[0].reshape(dec, -1),
                                        state_s5_im[0].reshape(dec, -1),
                                        bb, cc, lbr, lbi, d_row, ys_all, dec, sample_blk)

    wr = jnp.pad(jnp.concatenate([w_rg[0], w_rexp[0]], axis=1), ((0, 0), (0, LANES - N_GROUPS - N_EXPERTS)))
    wr_hi = wr.astype(BF16)
    wr_lo = (wr - wr_hi.astype(F32)).astype(BF16)
    b_r = jnp.pad(jnp.concatenate([b_rg[0], b_rexp[0]]), (0, LANES - N_GROUPS - N_EXPERTS)).reshape(1, LANES)
    merge_w = (w_pa[0].astype(BF16), s5_w_glu[0].astype(BF16), w_pb[0].astype(BF16), w_out[0].astype(BF16))
    b_glu = s5_b_glu[0].reshape(1, D_B)
    g_ffn = norm_ffn_g[0].reshape(1, D_MODEL)
    tm_m = _pick_tile(t_p, (256, 128))
    outs = _merge(xp, hn_all, ys_all, proj2, b_glu, g_ffn, wr_hi, wr_lo, b_r, *merge_w, t_all, 0, tm_m)
    x1_all, xn2_all, ids, wts = _merge(xs, hn_all, ys_all, proj2, b_glu, g_ffn, wr_hi, wr_lo, b_r,
                                       *merge_w, t_all, t_p, dec, aliases=outs)

    p_rows = -(-(2 * t_all + N_EXPERTS * (MOE_TILE - 1)) // MOE_TILE) * MOE_TILE
    pos, src, meta = _dispatch(ids, t_all, p_rows)
    take_rows = lambda a, idx: a.at[idx].get(mode='promise_in_bounds')
    xs_sorted = take_rows(xn2_all, src)
    yp = _moe_experts(*meta, xs_sorted, w_gate[0], w_up[0], w_down[0])
    yg0 = take_rows(yp, pos[:, 0])
    yg1 = take_rows(yp, pos[:, 1])

    g_fin = norm_final_g.reshape(1, D_MODEL)
    y_prompt = _final(x1_all, yg0, yg1, wts, g_fin, 0, t_p, tm_p).reshape(batch, seq, D_MODEL)
    y_sample = _final(x1_all, yg0, yg1, wts, g_fin, t_p, dec, dec).reshape(dec, 1, D_MODEL)

    lead = lambda a, shape: a.reshape((1,) + shape)
    return (y_prompt, y_sample,
            lead(c_p, (batch, H_A, DK, DK)), lead(n_p, (batch, H_A, DK)), lead(m_p[:, 0, :H_A], (batch, H_A)),
            lead(conv_p, (batch, CONV_W - 1, 2 * d_a)),
            lead(s5re_p, (batch, G_B, P_S5)), lead(s5im_p, (batch, G_B, P_S5)),
            lead(c_s, (dec, H_A, DK, DK)), lead(n_s, (dec, H_A, DK)), lead(m_s[:, :H_A], (dec, H_A)),
            lead(conv_s, (dec, CONV_W - 1, 2 * d_a)),
            lead(s5re_s, (dec, G_B, P_S5)), lead(s5im_s, (dec, G_B, P_S5)))
```

```python
import functools
import math

import jax
import jax.numpy as jnp
from jax import lax
from jax.experimental import pallas as pl
from jax.experimental.pallas import tpu as pltpu

F32 = jnp.float32
BF16 = jnp.bfloat16

D_MODEL = 2048
H_A = 8
DK = 256
CONV_W = 4
CHUNK = 128
D_B = 1024
S5_GROUP = 16
G_B = 64
P_S5 = 64
N_GROUPS = 4
EXP_PER_GROUP = 8
N_EXPERTS = 32
D_EXPERT = 512
EPS = 1e-6

LANES = 128
SUBLANES = 8
VMEM_LIMIT = 56 * 1024 * 1024

S5_CHUNK = 8
S5_CST = (LANES // S5_GROUP) * P_S5
MOE_TILE = 256


def _cparams(sem):
    return pltpu.CompilerParams(dimension_semantics=sem, vmem_limit_bytes=VMEM_LIMIT)


def _silu(x):
    return x * (1.0 / (1.0 + jnp.exp(-x)))


def _sigmoid(x):
    return 1.0 / (1.0 + jnp.exp(-x))


def _log_sigmoid(x):
    return jnp.minimum(x, 0.0) - jnp.log1p(jnp.exp(-jnp.abs(x)))


def _gelu_tanh(x):
    c = math.sqrt(2.0 / math.pi)
    return 0.5 * x * (1.0 + jnp.tanh(c * (x + 0.044715 * (x * x * x))))


def _split3(x):
    hi = x.astype(BF16)
    r = x - hi.astype(F32)
    mid = r.astype(BF16)
    lo = (r - mid.astype(F32)).astype(BF16)
    return hi, mid, lo


def _dot(a, b):
    return jnp.dot(a, b, preferred_element_type=F32)


def _dot_nt(a, b):
    return lax.dot_general(a, b, (((1,), (1,)), ((), ())), preferred_element_type=F32)


def _dot_tn(a, b):
    return lax.dot_general(a, b, (((0,), (0,)), ((), ())), preferred_element_type=F32)


def _rmsnorm_kernel(x_ref, g_ref, *rest):
    o_ref = rest[-1]
    x = x_ref[...]
    r = lax.rsqrt(jnp.mean(x * x, axis=-1, keepdims=True) + EPS)
    o_ref[...] = (x * r * g_ref[...]).astype(o_ref.dtype)


def _rmsnorm_rows(x, g, t_all, row0, tm, alias=None):
    n = x.shape[0]
    blk0 = row0 // tm
    in_specs = [pl.BlockSpec((tm, D_MODEL), lambda i: (i, 0)),
                pl.BlockSpec((1, D_MODEL), lambda i: (0, 0))]
    args = [x, g.reshape(1, D_MODEL)]
    aliases = {}
    if alias is not None:
        in_specs.append(pl.BlockSpec(memory_space=pl.ANY))
        args.append(alias)
        aliases = {2: 0}
    return pl.pallas_call(
        _rmsnorm_kernel,
        out_shape=jax.ShapeDtypeStruct((t_all, D_MODEL), BF16),
        grid=(n // tm,),
        in_specs=in_specs,
        out_specs=pl.BlockSpec((tm, D_MODEL), lambda i: (i + blk0, 0)),
        input_output_aliases=aliases,
        compiler_params=_cparams(("arbitrary",)),
        name="rmsnorm_rows",
    )(*args)


def _mm_kernel(a_ref, w_ref, b_ref, o_ref, wb_ref):
    @pl.when(pl.program_id(1) == 0)
    def _():
        wb_ref[...] = w_ref[...].astype(BF16)

    o_ref[...] = (_dot(a_ref[...], wb_ref[...]) + b_ref[...]).astype(o_ref.dtype)


def _matmul(a, w, bias, n_out, tm, tn, out_dtype, col_blk0=0):
    m, k = a.shape
    return pl.pallas_call(
        _mm_kernel,
        out_shape=jax.ShapeDtypeStruct((m, n_out), out_dtype),
        grid=(n_out // tn, m // tm),
        in_specs=[pl.BlockSpec((tm, k), lambda j, i: (i, 0)),
                  pl.BlockSpec((k, tn), lambda j, i: (0, j + col_blk0)),
                  pl.BlockSpec((1, tn), lambda j, i: (0, j))],
        out_specs=pl.BlockSpec((tm, tn), lambda j, i: (i, j)),
        scratch_shapes=[pltpu.VMEM((k, tn), BF16)],
        compiler_params=_cparams(("arbitrary", "arbitrary")),
        name="rows_matmul",
    )(a, w, bias)


def _mm_t_kernel(starts_ref, a_ref, wt_ref, b_ref, o_ref, wb_ref):
    del starts_ref

    @pl.when(pl.program_id(1) == 0)
    def _():
        wb_ref[...] = wt_ref[...].astype(BF16)

    o_ref[...] = (_dot_nt(a_ref[...], wb_ref[...]) + b_ref[...]).astype(o_ref.dtype)


def _matmul_t(a, w_t, row_starts, bias, tm, tn, out_dtype):
    m, k = a.shape
    n_t = len(row_starts)
    assert all(s % SUBLANES == 0 for s in row_starts)
    grid_spec = pltpu.PrefetchScalarGridSpec(
        num_scalar_prefetch=1,
        grid=(n_t, m // tm),
        in_specs=[pl.BlockSpec((tm, k), lambda j, i, st: (i, 0)),
                  pl.BlockSpec((pl.Element(tn), pl.Element(k)), lambda j, i, st: (st[j] * SUBLANES, 0)),
                  pl.BlockSpec((1, tn), lambda j, i, st: (0, j))],
        out_specs=pl.BlockSpec((tm, tn), lambda j, i, st: (i, j)),
        scratch_shapes=[pltpu.VMEM((tn, k), BF16)],
    )
    return pl.pallas_call(
        _mm_t_kernel,
        out_shape=jax.ShapeDtypeStruct((m, n_t * tn), out_dtype),
        grid_spec=grid_spec,
        compiler_params=_cparams(("arbitrary", "arbitrary")),
        name="rows_matmul_t",
    )(jnp.asarray([s // SUBLANES for s in row_starts], jnp.int32), a, w_t, bias)


def _mm_shift_kernel(a_ref, wlo_ref, whi_ref, o_ref, wb_ref, *, shift):
    tn = wlo_ref.shape[1]
    rows = 256

    @pl.when(pl.program_id(1) == 0)
    def _():
        for r in range(wlo_ref.shape[0] // rows):
            sl = slice(r * rows, (r + 1) * rows)
            w = jnp.concatenate([wlo_ref[sl, :], whi_ref[sl, :]], axis=1)
            wb_ref[sl, :] = w[:, shift:shift + tn].astype(BF16)

    o_ref[...] = _dot(a_ref[...], wb_ref[...]).astype(o_ref.dtype)


def _matmul_shifted(a, w, col_starts, tm, tn, out_dtype):
    m, k = a.shape
    shift = col_starts[0] % tn
    assert all(c % tn == shift for c in col_starts) and 0 < shift < LANES
    lo_blk = [c // tn for c in col_starts]
    n_t = len(col_starts)
    base, first = min(lo_blk), lo_blk[0]
    assert all(lo_blk[j] == base + (j + first - base) % n_t for j in range(n_t))
    lo_idx = lambda j: base + (j + (first - base)) % n_t
    per = tn // LANES
    kern = functools.partial(_mm_shift_kernel, shift=shift)
    return pl.pallas_call(
        kern,
        out_shape=jax.ShapeDtypeStruct((m, n_t * tn), out_dtype),
        grid=(n_t, m // tm),
        in_specs=[pl.BlockSpec((tm, k), lambda j, i: (i, 0)),
                  pl.BlockSpec((k, tn), lambda j, i: (0, lo_idx(j))),
                  pl.BlockSpec((k, LANES), lambda j, i: (0, (lo_idx(j) + 1) * per))],
        out_specs=pl.BlockSpec((tm, tn), lambda j, i: (i, j)),
        scratch_shapes=[pltpu.VMEM((k, tn), BF16)],
        compiler_params=_cparams(("arbitrary", "arbitrary")),
        name="rows_matmul_shifted",
    )(a, w, w)


def _mlstm_kernel(qk_ref, v_ref, o_ref, gcol_ref, grow_ref, wc_ref, bc_ref, hg_ref,
                  h_ref, c_out, n_out, m_out, c_sc, n_sc, m_sc, ext_sc):
    c = pl.program_id(1)
    L = CHUNK
    pad = SUBLANES
    d_a = H_A * DK

    @pl.when(c == 0)
    def _():
        c_sc[...] = jnp.zeros_like(c_sc)
        n_sc[...] = jnp.zeros_like(n_sc)
        m_sc[...] = jnp.zeros_like(m_sc)
        ext_sc[0:pad, :] = jnp.zeros((pad, 2 * d_a), F32)

    x = qk_ref[...]
    ext_sc[pad:pad + L, :] = x
    y = bc_ref[...] + wc_ref[CONV_W - 1:CONV_W, :] * x
    for j in range(1, CONV_W):
        y = y + wc_ref[CONV_W - 1 - j:CONV_W - j, :] * ext_sc[pad - j:pad - j + L, :]
    ext_sc[0:pad, :] = x[L - pad:L, :]
    y = _silu(y)

    gcol = gcol_ref[...]
    grow = grow_ref[...]
    ri = lax.broadcasted_iota(jnp.int32, (L, L), 0)
    ci = lax.broadcasted_iota(jnp.int32, (L, L), 1)
    causal = ci <= ri
    tril = jnp.where(causal, 1.0, 0.0).astype(BF16)
    triu = jnp.where(ri <= ci, 1.0, 0.0).astype(BF16)
    b_cols = sum(_dot(tril, p) for p in _split3(_log_sigmoid(gcol)))
    b_rows = sum(_dot(p, triu) for p in _split3(_log_sigmoid(grow)))

    for h in range(H_A):
        sl = slice(h * DK, (h + 1) * DK)
        q = y[:, sl] * (DK ** -0.5)
        k = y[:, d_a + h * DK:d_a + (h + 1) * DK]
        ig_col = gcol[:, h:h + 1]
        ig_row = grow[h:h + 1, :]
        b_col = b_cols[:, H_A + h:H_A + h + 1]
        b_row = b_rows[H_A + h:H_A + h + 1, :]

        m_prev = m_sc[:, h:h + 1]
        d_log = jnp.where(causal, b_col - b_row + ig_row, -jnp.inf)
        inter_log = b_col + m_prev
        m_t = jnp.maximum(inter_log, jnp.max(d_log, axis=1, keepdims=True))
        qb = q.astype(BF16)
        kb = k.astype(BF16)
        vb = v_ref[:, sl].astype(BF16)
        s = _dot_nt(qb, kb) * jnp.exp(d_log - m_t)
        inter_w = jnp.exp(inter_log - m_t)
        c_prev = c_sc[h]
        n_prev = n_sc[h:h + 1, :]
        num = _dot(s.astype(BF16), vb) + inter_w * _dot(qb, c_prev.astype(BF16))
        nq = jnp.sum(s, axis=1, keepdims=True) + inter_w * jnp.sum(q * n_prev, axis=1, keepdims=True)
        den = jnp.maximum(jnp.abs(nq), jnp.exp(-m_t))
        hh = num / den
        hh = hh * _sigmoid(o_ref[:, sl])
        hh = hh * lax.rsqrt(jnp.mean(hh * hh, axis=1, keepdims=True) + EPS)
        h_ref[:, sl] = (hh * hg_ref[:, sl]).astype(h_ref.dtype)

        m_new = m_t[L - 1:L, :]
        b_last = b_col[L - 1:L, :]
        decay = jnp.exp(b_last + m_prev - m_new)
        w_end = jnp.exp(b_last - b_col + ig_col - m_new)
        kw = k * w_end
        c_sc[h] = decay * c_prev + _dot_tn(kw.astype(BF16), vb)
        n_sc[h:h + 1, :] = decay * n_prev + jnp.sum(kw, axis=0, keepdims=True)
        m_sc[:, h:h + 1] = m_new

    @pl.when(c == pl.num_programs(1) - 1)
    def _():
        c_out[...] = c_sc[...]
        n_out[...] = n_sc[...]
        m_out[...] = m_sc[...]


def _mlstm_prompt(qkvo, gates, gates_t, w_conv, b_conv, head_g, batch, seq, t_all):
    nc = seq // CHUNK
    L = CHUNK
    d_a = H_A * DK
    row = lambda b, c: b * nc + c
    in_specs = [
        pl.BlockSpec((L, 2 * d_a), lambda b, c: (row(b, c), 0)),
        pl.BlockSpec((L, d_a), lambda b, c: (row(b, c), 2)),
        pl.BlockSpec((L, d_a), lambda b, c: (row(b, c), 3)),
        pl.BlockSpec((L, LANES), lambda b, c: (row(b, c), 0)),
        pl.BlockSpec((2 * H_A, L), lambda b, c: (0, row(b, c))),
        pl.BlockSpec((CONV_W, 2 * d_a), lambda b, c: (0, 0)),
        pl.BlockSpec((1, 2 * d_a), lambda b, c: (0, 0)),
        pl.BlockSpec((1, d_a), lambda b, c: (0, 0)),
    ]
    out_shape = (
        jax.ShapeDtypeStruct((t_all, d_a), BF16),
        jax.ShapeDtypeStruct((batch, H_A, DK, DK), F32),
        jax.ShapeDtypeStruct((batch, H_A, DK), F32),
        jax.ShapeDtypeStruct((batch, 1, LANES), F32),
    )
    out_specs = (
        pl.BlockSpec((L, d_a), lambda b, c: (row(b, c), 0)),
        pl.BlockSpec((None, H_A, DK, DK), lambda b, c: (b, 0, 0, 0)),
        pl.BlockSpec((None, H_A, DK), lambda b, c: (b, 0, 0)),
        pl.BlockSpec((None, 1, LANES), lambda b, c: (b, 0, 0)),
    )
    return pl.pallas_call(
        _mlstm_kernel,
        out_shape=out_shape,
        grid=(batch, nc),
        in_specs=in_specs,
        out_specs=out_specs,
        scratch_shapes=[pltpu.VMEM((H_A, DK, DK), F32), pltpu.VMEM((H_A, DK), F32),
                        pltpu.VMEM((1, LANES), F32), pltpu.VMEM((SUBLANES + L, 2 * d_a), F32)],
        compiler_params=_cparams(("arbitrary", "arbitrary")),
        name="mlstm_prompt",
    )(qkvo, qkvo, qkvo, gates, gates_t, w_conv, b_conv, head_g)


def _lane_tile(x, reps):
    w = x.shape[1]
    ri = lax.broadcasted_iota(jnp.int32, (w, w * reps), 0)
    ci = lax.broadcasted_iota(jnp.int32, (w, w * reps), 1)
    rep = jnp.where(jnp.bitwise_and(ci, w - 1) == ri, 1.0, 0.0).astype(BF16)
    return _dot(x.astype(BF16), rep)


def _group_mask(rows, cols, row_per, col_per):
    ri = lax.broadcasted_iota(jnp.int32, (rows, cols), 0)
    ci = lax.broadcasted_iota(jnp.int32, (rows, cols), 1)
    return (jnp.right_shift(ri, int(math.log2(row_per))) == jnp.right_shift(ci, int(math.log2(col_per))))


def _s5_build_weights(bin_r, bin_i, kdt, aout_r, aout_i, winc_ref, wintra_ref, wout_ref):
    L = S5_CHUNK
    gpb = LANES // S5_GROUP
    m_inc = _group_mask(LANES, S5_CST, S5_GROUP, P_S5)
    m_lag = _group_mask(LANES, LANES, S5_GROUP, S5_GROUP)
    zero = jnp.zeros((LANES, LANES), BF16)
    lag = [jnp.where(m_lag, _lane_tile(kdt[d], gpb), 0.0).astype(BF16) for d in range(L)]
    for t in range(L):
        rows = slice(t * LANES, (t + 1) * LANES)
        d = L - 1 - t
        winc_ref[rows, 0:S5_CST] = jnp.where(m_inc, _lane_tile(bin_r[d], gpb), 0.0).astype(BF16)
        winc_ref[rows, S5_CST:2 * S5_CST] = jnp.where(m_inc, _lane_tile(bin_i[d], gpb), 0.0).astype(BF16)
        wout_ref[rows, 0:S5_CST] = jnp.where(m_inc, _lane_tile(aout_r[t], gpb), 0.0).astype(BF16)
        wout_ref[rows, S5_CST:2 * S5_CST] = jnp.where(m_inc, _lane_tile(aout_i[t], gpb), 0.0).astype(BF16)
        for t2 in range(L):
            wintra_ref[rows, t2 * LANES:(t2 + 1) * LANES] = lag[t2 - t] if t2 >= t else zero


def _s5_prompt_kernel(u_ref, binr_ref, bini_ref, kdt_ref, aoutr_ref, aouti_ref, lbr_ref, lbi_ref, d_ref,
                      ys_ref, sre_ref, sim_ref, x_sc, winc_ref, wintra_ref, wout_ref, *, batch, nchunk):
    L = S5_CHUNK
    nrow = batch * nchunk
    nst = S5_CST // LANES
    rstr = nchunk + SUBLANES
    _s5_build_weights(binr_ref, bini_ref, kdt_ref, aoutr_ref, aouti_ref, winc_ref, wintra_ref, wout_ref)
    u_t = [u_ref[pl.ds(t, nrow, stride=L), :] for t in range(L)]
    lhs = jnp.concatenate([a.astype(BF16) for a in u_t], axis=1)
    inc = _dot(lhs, winc_ref[...])
    for j in range(2 * nst):
        for b in range(batch):
            x_sc[j, b * rstr:b * rstr + nchunk, :] = inc[b * nchunk:(b + 1) * nchunk, j * LANES:(j + 1) * LANES]

    lbr = [jnp.broadcast_to(lbr_ref[:, j * LANES:(j + 1) * LANES], (batch, LANES)) for j in range(nst)]
    lbi = [jnp.broadcast_to(lbi_ref[:, j * LANES:(j + 1) * LANES], (batch, LANES)) for j in range(nst)]

    def scan_body(r, carry):
        rows = pl.ds(r, batch, stride=rstr)
        out = []
        for j in range(nst):
            xr, xi = carry[j]
            ir = x_sc[j, rows, :]
            ii = x_sc[nst + j, rows, :]
            x_sc[j, rows, :] = xr
            x_sc[nst + j, rows, :] = xi
            out.append((lbr[j] * xr - lbi[j] * xi + ir, lbr[j] * xi + lbi[j] * xr + ii))
        return tuple(out)

    z = jnp.zeros((batch, LANES), F32)
    fin = lax.fori_loop(0, nchunk, scan_body, tuple((z, z) for _ in range(nst)))
    for j in range(nst):
        sre_ref[:, j * LANES:(j + 1) * LANES] = fin[j][0]
        sim_ref[:, j * LANES:(j + 1) * LANES] = fin[j][1]

    xprev = jnp.concatenate(
        [jnp.concatenate([x_sc[j, b * rstr:b * rstr + nchunk, :] for b in range(batch)], axis=0)
         for j in range(2 * nst)], axis=1).astype(BF16)
    y = _dot(lhs, wintra_ref[...]) + _dot_nt(xprev, wout_ref[...])
    for t in range(L):
        yt = y[:, t * LANES:(t + 1) * LANES] + d_ref[...] * u_t[t]
        ys_ref[pl.ds(t, nrow, stride=L), :] = _gelu_tanh(yt).astype(ys_ref.dtype)


def _s5_prompt(proj2, u_col0, bin_r, bin_i, kdt, aout_r, aout_i, lb8r, lb8i, d_row, batch, seq, t_all):
    nchunk = seq // S5_CHUNK
    t_p = batch * seq
    kern = functools.partial(_s5_prompt_kernel, batch=batch, nchunk=nchunk)
    ub0 = u_col0 // LANES
    ncb = D_B // LANES
    kw = S5_CHUNK * LANES
    per_blk = lambda a: pl.BlockSpec((a.shape[0], None) + a.shape[2:], lambda g: (0, g, 0, 0))
    return pl.pallas_call(
        kern,
        out_shape=(jax.ShapeDtypeStruct((t_all, D_B), F32),
                   jax.ShapeDtypeStruct((batch, G_B * P_S5), F32),
                   jax.ShapeDtypeStruct((batch, G_B * P_S5), F32)),
        grid=(ncb,),
        in_specs=[pl.BlockSpec((t_p, LANES), lambda g: (0, ub0 + g)),
                  per_blk(bin_r), per_blk(bin_i), per_blk(kdt), per_blk(aout_r), per_blk(aout_i),
                  pl.BlockSpec((None, 1, S5_CST), lambda g: (g, 0, 0)),
                  pl.BlockSpec((None, 1, S5_CST), lambda g: (g, 0, 0)),
                  pl.BlockSpec((1, LANES), lambda g: (0, g))],
        out_specs=(pl.BlockSpec((t_p, LANES), lambda g: (0, g)),
                   pl.BlockSpec((batch, S5_CST), lambda g: (0, g)),
                   pl.BlockSpec((batch, S5_CST), lambda g: (0, g))),
        scratch_shapes=[pltpu.VMEM((2 * S5_CST // LANES, batch * (nchunk + SUBLANES), LANES), F32),
                        pltpu.VMEM((kw, 2 * S5_CST), BF16), pltpu.VMEM((kw, kw), BF16),
                        pltpu.VMEM((kw, 2 * S5_CST), BF16)],
        compiler_params=_cparams(("arbitrary",)),
        name="s5_prompt",
    )(proj2, bin_r, bin_i, kdt, aout_r, aout_i, lb8r, lb8i, d_row)


def _sample_pre_kernel(qk_ref, conv_ref, wc_ref, bc_ref, g_ref, m_ref, n_ref,
                       q_out, kw_out, a_out, s_out, den_out, conv_out, n_out, m_out):
    c2 = 2 * H_A * DK
    x_new = qk_ref[...]
    y = bc_ref[...] + wc_ref[CONV_W - 1:CONV_W, :] * x_new
    for j in range(CONV_W - 1):
        y = y + wc_ref[j:j + 1, :] * conv_ref[:, j * c2:(j + 1) * c2]
    y = _silu(y)
    conv_out[:, 0:(CONV_W - 2) * c2] = conv_ref[:, c2:(CONV_W - 1) * c2]
    conv_out[:, (CONV_W - 2) * c2:(CONV_W - 1) * c2] = x_new
    g = g_ref[...]
    bd = x_new.shape[0]
    m_cols = []
    for h in range(H_A):
        sl = slice(h * DK, (h + 1) * DK)
        q = y[:, sl] * (DK ** -0.5)
        k = y[:, H_A * DK + h * DK:H_A * DK + (h + 1) * DK]
        ig = g[:, h:h + 1]
        lf = _log_sigmoid(g[:, H_A + h:H_A + h + 1])
        m_prev = m_ref[:, h:h + 1]
        m_t = jnp.maximum(lf + m_prev, ig)
        a = jnp.exp(lf + m_prev - m_t)
        wgt = jnp.exp(ig - m_t)
        n_prev = n_ref[:, sl]
        s = jnp.sum(q * k, axis=1, keepdims=True) * wgt
        nq = s + a * jnp.sum(q * n_prev, axis=1, keepdims=True)
        den = jnp.maximum(jnp.abs(nq), jnp.exp(-m_t))
        kw = wgt * k
        q_out[:, sl] = q
        kw_out[:, sl] = kw
        a_out[:, sl] = jnp.broadcast_to(a, (bd, DK))
        s_out[:, sl] = jnp.broadcast_to(s, (bd, DK))
        den_out[:, sl] = jnp.broadcast_to(den, (bd, DK))
        n_out[:, sl] = a * n_prev + kw
        m_cols.append(m_t)
    lane = lax.broadcasted_iota(jnp.int32, (bd, LANES), 1)
    m_full = jnp.zeros((bd, LANES), F32)
    for h in range(H_A):
        m_full = jnp.where(lane == h, m_cols[h], m_full)
    m_out[...] = m_full


def _sample_pre(qkvo, conv_state, w_conv, b_conv, gates, m_state, n_state, dec, row_blk):
    c2 = 2 * H_A * DK
    d = H_A * DK
    full = lambda shape: pl.BlockSpec(shape, lambda i: (0,) * len(shape))
    rows = lambda: jax.ShapeDtypeStruct((dec, d), F32)
    return pl.pallas_call(
        _sample_pre_kernel,
        out_shape=(rows(), rows(), rows(), rows(), rows(),
                   jax.ShapeDtypeStruct((dec, (CONV_W - 1) * c2), F32), rows(),
                   jax.ShapeDtypeStruct((dec, LANES), F32)),
        grid=(1,),
        in_specs=[pl.BlockSpec((dec, c2), lambda i: (row_blk, 0)),
                  full((dec, (CONV_W - 1) * c2)), full((CONV_W, c2)), full((1, c2)),
                  pl.BlockSpec((dec, LANES), lambda i: (row_blk, 0)),
                  full((dec, H_A)), full((dec, d))],
        out_specs=(full((dec, d)), full((dec, d)), full((dec, d)), full((dec, d)), full((dec, d)),
                   full((dec, (CONV_W - 1) * c2)), full((dec, d)), full((dec, LANES))),
        compiler_params=_cparams(("arbitrary",)),
        name="sample_pre",
    )(qkvo, conv_state, w_conv, b_conv, gates, m_state, n_state)


def _sample_c_kernel(q_ref, kw_ref, v_ref, a_ref, c_ref, c_out, qc_out):
    eye = jnp.where(lax.broadcasted_iota(jnp.int32, (DK, DK), 0)
                    == lax.broadcasted_iota(jnp.int32, (DK, DK), 1), 1.0, 0.0).astype(BF16)
    for b in range(q_ref.shape[0]):
        q_t = sum(_dot_nt(eye, p) for p in _split3(q_ref[b]))
        kw_t = sum(_dot_nt(eye, p) for p in _split3(kw_ref[b]))
        qc_rows = []
        for h in range(H_A):
            c_prev = c_ref[b, h]
            qc_rows.append(jnp.sum(q_t[:, h:h + 1] * c_prev, axis=0, keepdims=True))
            c_out[b, h] = a_ref[b, h:h + 1, :] * c_prev + kw_t[:, h:h + 1] * v_ref[b, h:h + 1, :]
        qc_out[b] = jnp.concatenate(qc_rows, axis=0)


SAMPLE_C_BATCH = 4


def _sample_c(q3, kw3, v3, a3, c_state, dec):
    nb = SAMPLE_C_BATCH
    vec = pl.BlockSpec((nb, H_A, DK), lambda b: (b, 0, 0))
    mat = pl.BlockSpec((nb, H_A, DK, DK), lambda b: (b, 0, 0, 0))
    return pl.pallas_call(
        _sample_c_kernel,
        out_shape=(jax.ShapeDtypeStruct((dec, H_A, DK, DK), F32),
                   jax.ShapeDtypeStruct((dec, H_A, DK), F32)),
        grid=(dec // nb,),
        in_specs=[vec, vec, vec, vec, mat],
        out_specs=(mat, vec),
        compiler_params=_cparams(("arbitrary",)),
        name="sample_c_update",
    )(q3, kw3, v3, a3, c_state)


def _sample_post_kernel(qc_ref, s_ref, a_ref, den_ref, vo_ref, hg_ref, hn_in, h_ref):
    del hn_in
    d = H_A * DK
    num = s_ref[...] * vo_ref[:, 0:d] + a_ref[...] * qc_ref[...]
    hh = num / den_ref[...]
    hh = hh * _sigmoid(vo_ref[:, d:2 * d])
    for h in range(H_A):
        sl = slice(h * DK, (h + 1) * DK)
        seg = hh[:, sl]
        seg = seg * lax.rsqrt(jnp.mean(seg * seg, axis=1, keepdims=True) + EPS)
        h_ref[:, sl] = (seg * hg_ref[:, sl]).astype(h_ref.dtype)


def _sample_post(qc, s_e, a_e, den_e, qkvo, head_g, hn_all, dec, row_blk):
    d = H_A * DK
    full = lambda shape: pl.BlockSpec(shape, lambda i: (0,) * len(shape))
    return pl.pallas_call(
        _sample_post_kernel,
        out_shape=jax.ShapeDtypeStruct(hn_all.shape, hn_all.dtype),
        grid=(1,),
        in_specs=[full((dec, d)), full((dec, d)), full((dec, d)), full((dec, d)),
                  pl.BlockSpec((dec, 2 * d), lambda i: (row_blk, 1)),
                  full((1, d)), pl.BlockSpec(memory_space=pl.ANY)],
        out_specs=pl.BlockSpec((dec, d), lambda i: (row_blk, 0)),
        input_output_aliases={6: 0},
        compiler_params=_cparams(("arbitrary",)),
        name="sample_post",
    )(qc, s_e, a_e, den_e, qkvo, head_g, hn_all)


def _s5_sample_kernel(u_ref, sr_ref, si_ref, br_ref, bi_ref, cr_ref, ci_ref, lbr_ref, lbi_ref, d_ref, ys_in,
                      ys_ref, sre_out, sim_out):
    del ys_in
    gpb = LANES // S5_GROUP
    mask = _group_mask(LANES, S5_CST, S5_GROUP, P_S5)
    expand = lambda blk: jnp.where(mask, _lane_tile(blk, gpb), 0.0)
    for g in range(D_B // LANES):
        ch = slice(g * LANES, (g + 1) * LANES)
        sl = slice(g * S5_CST, (g + 1) * S5_CST)
        u = u_ref[:, ch]
        bmat = jnp.concatenate([expand(br_ref[g]), expand(bi_ref[g])], axis=1).astype(BF16)
        bu = _dot(u.astype(BF16), bmat)
        lbr = lbr_ref[g]
        lbi = lbi_ref[g]
        sr = sr_ref[:, sl]
        si = si_ref[:, sl]
        xr = lbr * sr - lbi * si + bu[:, 0:S5_CST]
        xi = lbr * si + lbi * sr + bu[:, S5_CST:2 * S5_CST]
        sre_out[:, sl] = xr
        sim_out[:, sl] = xi
        x = jnp.concatenate([xr, xi], axis=1).astype(BF16)
        cmat = jnp.concatenate([expand(cr_ref[g]), -expand(ci_ref[g])], axis=1).astype(BF16)
        y = _dot_nt(x, cmat) + d_ref[:, ch] * u
        ys_ref[:, ch] = _gelu_tanh(y).astype(ys_ref.dtype)


def _s5_sample(proj2, u_col0, s_re, s_im, b_r, b_i, c_r, c_i, lbr, lbi, d_row, ys_all, dec, row_blk):
    full = lambda shape: pl.BlockSpec(shape, lambda i: (0,) * len(shape))
    n_state = G_B * P_S5
    ub0 = u_col0 // D_B
    params = (b_r, b_i, c_r, c_i, lbr, lbi)
    return pl.pallas_call(
        _s5_sample_kernel,
        out_shape=(jax.ShapeDtypeStruct(ys_all.shape, ys_all.dtype),
                   jax.ShapeDtypeStruct((dec, n_state), F32),
                   jax.ShapeDtypeStruct((dec, n_state), F32)),
        grid=(1,),
        in_specs=[pl.BlockSpec((dec, D_B), lambda i: (row_blk, ub0)),
                  full((dec, n_state)), full((dec, n_state))]
                 + [full(p.shape) for p in params]
                 + [full((1, D_B)), pl.BlockSpec(memory_space=pl.ANY)],
        out_specs=(pl.BlockSpec((dec, D_B), lambda i: (row_blk, 0)),
                   full((dec, n_state)), full((dec, n_state))),
        input_output_aliases={10: 0},
        compiler_params=_cparams(("arbitrary",)),
        name="s5_sample",
    )(proj2, s_re, s_im, *params, d_row, ys_all)


def _merge_kernel(x_ref, hn_ref, ys_ref, ga_ref, gb_ref, bglu_ref, gffn_ref, wrh_ref, wrl_ref, br_ref,
                  wpa_hbm, wglu_hbm, wpb_hbm, wout_hbm, *rest):
    n_alias = len(rest) - 9
    x1_ref, xn_ref, ids_ref, wts_ref = rest[n_alias:n_alias + 4]
    wpa, wglu, wpb, wout, sem = rest[n_alias + 4:]

    @pl.when(pl.program_id(0) == 0)
    def _():
        copies = [pltpu.make_async_copy(src, dst, sem.at[i])
                  for i, (src, dst) in enumerate(((wpa_hbm, wpa), (wglu_hbm, wglu),
                                                  (wpb_hbm, wpb), (wout_hbm, wout)))]
        for cp in copies:
            cp.start()
        for cp in copies:
            cp.wait()

    ya = _dot(hn_ref[...], wpa[...])
    ys = ys_ref[...]
    gate = _sigmoid(_dot(ys.astype(BF16), wglu[...]) + bglu_ref[...])
    yb = _dot((ys * gate).astype(BF16), wpb[...])
    z = _sigmoid(ga_ref[...]) * ya + _sigmoid(gb_ref[...]) * yb
    x1 = x_ref[...] + _dot(z.astype(BF16), wout[...])
    x1_ref[...] = x1
    xn = x1 * lax.rsqrt(jnp.mean(x1 * x1, axis=1, keepdims=True) + EPS) * gffn_ref[...]
    xn_ref[...] = xn
    xh = xn.astype(BF16)
    xl = (xn - xh.astype(F32)).astype(BF16)
    logits = _dot(xh, wrh_ref[...]) + _dot(xl, wrh_ref[...]) + _dot(xh, wrl_ref[...]) + br_ref[...]

    lane_i = lax.broadcasted_iota(jnp.int32, logits.shape, 1)
    lane = lane_i.astype(F32)
    neg = -jnp.inf
    big = float(1 << 20)
    gl = jnp.where(lane_i < N_GROUPS, logits, neg)
    gmax = jnp.max(gl, axis=1, keepdims=True)
    gsum = jnp.sum(jnp.exp(gl - gmax), axis=1, keepdims=True)
    gidx = jnp.min(jnp.where(gl == gmax, lane, big), axis=1, keepdims=True)
    pg_sel = 1.0 / gsum
    lo = N_GROUPS + gidx * EXP_PER_GROUP
    in_grp = (lane >= lo) & (lane < lo + EXP_PER_GROUP)
    el = jnp.where(in_grp, logits, neg)
    emax = jnp.max(el, axis=1, keepdims=True)
    ee = jnp.exp(el - emax)
    pe = ee / jnp.sum(ee, axis=1, keepdims=True)
    v0 = jnp.max(pe, axis=1, keepdims=True)
    i0 = jnp.min(jnp.where(in_grp & (pe == v0), lane, big), axis=1, keepdims=True)
    rest_m = in_grp & (lane != i0)
    pe1 = jnp.where(rest_m, pe, neg)
    v1 = jnp.max(pe1, axis=1, keepdims=True)
    i1 = jnp.min(jnp.where(rest_m & (pe1 == v1), lane, big), axis=1, keepdims=True)
    tot = v0 + v1
    w0 = pg_sel * (v0 / tot)
    w1 = pg_sel * (v1 / tot)
    ids = jnp.where(lane_i == 0, i0 - N_GROUPS, jnp.where(lane_i == 1, i1 - N_GROUPS, 0.0))
    ids_ref[...] = ids.astype(jnp.int32)
    wts_ref[...] = jnp.where(lane_i == 0, w0, jnp.where(lane_i == 1, w1, 0.0))


def _merge(x, hn_all, ys_all, proj2, b_glu, g_ffn, wr_hi, wr_lo, b_r, wpa, wglu, wpb, wout,
           t_all, row0, tm, aliases=None):
    n = x.shape[0]
    blk0 = row0 // tm
    const = lambda shape: pl.BlockSpec(shape, lambda i: (0,) * len(shape))
    any_spec = pl.BlockSpec(memory_space=pl.ANY)
    in_specs = [pl.BlockSpec((tm, D_MODEL), lambda i: (i, 0)),
                pl.BlockSpec((tm, D_MODEL), lambda i: (i + blk0, 0)),
                pl.BlockSpec((tm, D_B), lambda i: (i + blk0, 0)),
                pl.BlockSpec((tm, D_MODEL), lambda i: (i + blk0, 0)),
                pl.BlockSpec((tm, D_MODEL), lambda i: (i + blk0, 1)),
                const((1, D_B)), const((1, D_MODEL)),
                const((D_MODEL, LANES)), const((D_MODEL, LANES)), const((1, LANES)),
                any_spec, any_spec, any_spec, any_spec]
    args = [x, hn_all, ys_all, proj2, proj2, b_glu, g_ffn, wr_hi, wr_lo, b_r, wpa, wglu, wpb, wout]
    io_alias = {}
    if aliases is not None:
        for j, a in enumerate(aliases):
            in_specs.append(any_spec)
            args.append(a)
            io_alias[14 + j] = j
    out_shape = (jax.ShapeDtypeStruct((t_all, D_MODEL), F32),
                 jax.ShapeDtypeStruct((t_all, D_MODEL), F32),
                 jax.ShapeDtypeStruct((t_all, LANES), jnp.int32),
                 jax.ShapeDtypeStruct((t_all, LANES), F32))
    out_specs = (pl.BlockSpec((tm, D_MODEL), lambda i: (i + blk0, 0)),
                 pl.BlockSpec((tm, D_MODEL), lambda i: (i + blk0, 0)),
                 pl.BlockSpec((tm, LANES), lambda i: (i + blk0, 0)),
                 pl.BlockSpec((tm, LANES), lambda i: (i + blk0, 0)))
    return pl.pallas_call(
        _merge_kernel,
        out_shape=out_shape,
        grid=(n // tm,),
        in_specs=in_specs,
        out_specs=out_specs,
        scratch_shapes=[pltpu.VMEM(wpa.shape, BF16), pltpu.VMEM(wglu.shape, BF16),
                        pltpu.VMEM(wpb.shape, BF16), pltpu.VMEM(wout.shape, BF16),
                        pltpu.SemaphoreType.DMA((4,))],
        input_output_aliases=io_alias,
        compiler_params=_cparams(("arbitrary",)),
        name="merge_router",
    )(*args)


def _moe_kernel(nt_ref, first_ref, ord_ref, elist_ref, nord_ref, xs_ref, wg_hbm, wu_hbm, wd_hbm, o_ref,
                stg_g, stg_u, stg_d, wg_sc, wu_sc, wd_sc, sem):
    i = pl.program_id(0)
    n_ord = nord_ref[0]

    def weight_copies(k, slot):
        e = elist_ref[k]
        return (pltpu.make_async_copy(wg_hbm.at[e], stg_g.at[slot], sem.at[slot, 0]),
                pltpu.make_async_copy(wu_hbm.at[e], stg_u.at[slot], sem.at[slot, 1]),
                pltpu.make_async_copy(wd_hbm.at[e], stg_d.at[slot], sem.at[slot, 2]))

    @pl.when(i == 0)
    def _():
        for cp in weight_copies(0, 0):
            cp.start()

        @pl.when(n_ord > 1)
        def _():
            for cp in weight_copies(1, 1):
                cp.start()

    valid = i < nt_ref[0]
    k = ord_ref[i]

    @pl.when(valid & (first_ref[i] == 1))
    def _():
        slot = k % 2
        for cp in weight_copies(k, slot):
            cp.wait()
        wg_sc[...] = stg_g[slot].astype(BF16)
        wu_sc[...] = stg_u[slot].astype(BF16)
        wd_sc[...] = stg_d[slot].astype(BF16)

        @pl.when(k + 2 < n_ord)
        def _():
            for cp in weight_copies(k + 2, slot):
                cp.start()

    @pl.when(valid)
    def _():
        x = xs_ref[...].astype(BF16)
        hg = _dot(x, wg_sc[...])
        hu = _dot(x, wu_sc[...])
        hh = (_silu(hg) * hu).astype(BF16)
        o_ref[...] = _dot(hh, wd_sc[...])


def _moe_experts(n_tiles, first, ordinal, elist, n_ord, xs, w_gate, w_up, w_down):
    p_rows = xs.shape[0]
    last = lambda i, nt, *_: jnp.minimum(i, nt[0] - 1)
    any_spec = pl.BlockSpec(memory_space=pl.ANY)
    grid_spec = pltpu.PrefetchScalarGridSpec(
        num_scalar_prefetch=5,
        grid=(p_rows // MOE_TILE,),
        in_specs=[pl.BlockSpec((MOE_TILE, D_MODEL), lambda i, *s: (last(i, *s), 0)),
                  any_spec, any_spec, any_spec],
        out_specs=pl.BlockSpec((MOE_TILE, D_MODEL), lambda i, *s: (last(i, *s), 0)),
        scratch_shapes=[pltpu.VMEM((2, D_MODEL, D_EXPERT), F32), pltpu.VMEM((2, D_MODEL, D_EXPERT), F32),
                        pltpu.VMEM((2, D_EXPERT, D_MODEL), F32),
                        pltpu.VMEM((D_MODEL, D_EXPERT), BF16), pltpu.VMEM((D_MODEL, D_EXPERT), BF16),
                        pltpu.VMEM((D_EXPERT, D_MODEL), BF16),
                        pltpu.SemaphoreType.DMA((2, 3))],
    )
    return pl.pallas_call(
        _moe_kernel,
        out_shape=jax.ShapeDtypeStruct((p_rows, D_MODEL), F32),
        grid_spec=grid_spec,
        compiler_params=_cparams(("arbitrary",)),
        name="moe_experts",
    )(n_tiles, first, ordinal, elist, n_ord, xs, w_gate, w_up, w_down)


def _final_kernel(x1_ref, y0_ref, y1_ref, w_ref, g_ref, o_ref):
    w = w_ref[...]
    x2 = x1_ref[...] + w[:, 0:1] * y0_ref[...] + w[:, 1:2] * y1_ref[...]
    o_ref[...] = x2 * lax.rsqrt(jnp.mean(x2 * x2, axis=1, keepdims=True) + EPS) * g_ref[...]


def _final(x1_all, yg0, yg1, wts, g_final, row0, n, tm):
    blk0 = row0 // tm
    rows = pl.BlockSpec((tm, D_MODEL), lambda i: (i + blk0, 0))
    return pl.pallas_call(
        _final_kernel,
        out_shape=jax.ShapeDtypeStruct((n, D_MODEL), F32),
        grid=(n // tm,),
        in_specs=[rows, rows, rows,
                  pl.BlockSpec((tm, LANES), lambda i: (i + blk0, 0)),
                  pl.BlockSpec((1, D_MODEL), lambda i: (0, 0))],
        out_specs=pl.BlockSpec((tm, D_MODEL), lambda i: (i, 0)),
        compiler_params=_cparams(("arbitrary",)),
        name="combine_final_norm",
    )(x1_all, yg0, yg1, wts, g_final)


def _s5_discretise(a_re, a_im, log_step, b_re, b_im):
    dt = jnp.exp(log_step)[:, None]
    mag = jnp.exp(a_re * dt)
    lb_re = mag * jnp.cos(a_im * dt)
    lb_im = mag * jnp.sin(a_im * dt)
    den = a_re * a_re + a_im * a_im
    nr = lb_re - 1.0
    coef_re = (nr * a_re + lb_im * a_im) / den
    coef_im = (lb_im * a_re - nr * a_im) / den
    bb_re = coef_re[..., None] * b_re - coef_im[..., None] * b_im
    bb_im = coef_re[..., None] * b_im + coef_im[..., None] * b_re
    return lb_re, lb_im, bb_re, bb_im


def _s5_chunk_params(a_re, a_im, log_step, b_re, b_im, c_re, c_im):
    lb_re, lb_im, bb_re, bb_im = _s5_discretise(a_re, a_im, log_step, b_re, b_im)
    L = S5_CHUNK
    gpb = LANES // S5_GROUP
    ncb = G_B // gpb
    pr, pi = [jnp.ones_like(lb_re)], [jnp.zeros_like(lb_re)]
    for _ in range(L):
        pr, pi = pr + [pr[-1] * lb_re - pi[-1] * lb_im], pi + [pr[-1] * lb_im + pi[-1] * lb_re]
    pw_r, pw_i = jnp.stack(pr), jnp.stack(pi)
    bt_re = bb_re.transpose(0, 2, 1)
    bt_im = bb_im.transpose(0, 2, 1)
    lbb_r = pw_r[:L, :, None, :] * bt_re - pw_i[:L, :, None, :] * bt_im
    lbb_i = pw_r[:L, :, None, :] * bt_im + pw_i[:L, :, None, :] * bt_re
    hp = lax.Precision.HIGHEST
    kdt = (jnp.einsum('gop,dgcp->dgco', c_re, lbb_r, precision=hp)
           - jnp.einsum('gop,dgcp->dgco', c_im, lbb_i, precision=hp))
    a_r = c_re * pw_r[1:, :, None, :] - c_im * pw_i[1:, :, None, :]
    a_i = -(c_re * pw_i[1:, :, None, :] + c_im * pw_r[1:, :, None, :])

    def blocks(m):
        return m.reshape(m.shape[:-3] + (ncb, gpb * m.shape[-2], m.shape[-1]))

    vec = lambda v: v.reshape(ncb, 1, S5_CST)
    prompt = (blocks(lbb_r), blocks(lbb_i), blocks(kdt), blocks(a_r), blocks(a_i), vec(pw_r[L]), vec(pw_i[L]))
    sample = (blocks(lbb_r[0]), blocks(lbb_i[0]), blocks(c_re), blocks(c_im), vec(lb_re), vec(lb_im))
    return prompt, sample


def _dispatch(ids, t_all, p_rows):
    e = ids[:, :2].reshape(-1)
    onehot = (e[:, None] == jnp.arange(N_EXPERTS, dtype=jnp.int32)[None, :]).astype(jnp.int32)
    csum = jnp.cumsum(onehot, axis=0)
    rank = jnp.sum((csum - onehot) * onehot, axis=1)
    counts = csum[-1]
    tiles = (counts + MOE_TILE - 1) // MOE_TILE
    tile_end = jnp.cumsum(tiles)
    tile_start = tile_end - tiles
    pos = jnp.sum(onehot * (tile_start * MOE_TILE)[None, :], axis=1) + rank
    tok = jnp.arange(2 * t_all, dtype=jnp.int32) // 2
    src = (jnp.arange(p_rows, dtype=jnp.int32) % t_all).at[pos].set(tok)
    n_tiles = tile_end[-1]
    tidx = jnp.arange(p_rows // MOE_TILE, dtype=jnp.int32)
    tclamp = jnp.minimum(tidx, n_tiles - 1)
    tile_expert = jnp.sum((tile_end[None, :] <= tclamp[:, None]).astype(jnp.int32), axis=1)
    present = (tiles > 0).astype(jnp.int32)
    ord_of_e = jnp.cumsum(present) - 1
    eids = jnp.arange(N_EXPERTS, dtype=jnp.int32)
    elist = jnp.sum(jnp.where((ord_of_e[None, :] == eids[:, None]) & (present[None, :] == 1), eids[None, :], 0), axis=1)
    ordinal = jnp.sum(jnp.where(tile_expert[:, None] == eids[None, :], ord_of_e[None, :], 0), axis=1)
    first = jnp.concatenate([jnp.ones((1,), jnp.int32),
                             (tile_expert[1:] != tile_expert[:-1]).astype(jnp.int32)])
    one = lambda v: v.reshape(1).astype(jnp.int32)
    meta = (one(n_tiles), first, ordinal.astype(jnp.int32), elist.astype(jnp.int32), one(jnp.sum(present)))
    return pos.reshape(t_all, 2), src, meta


def _pick_tile(n, candidates):
    for c in candidates:
        if n % c == 0:
            return c
    raise ValueError(f"no row tile for {n}")


def kernel(x_prompt, x_sample, state_mlstm_C, state_mlstm_n, state_mlstm_m, state_conv, state_s5_re,
           state_s5_im, norm_mix_g, w_in, b_i, b_f, w_conv, b_conv, head_norm_g, w_pa, s5_a_re, s5_a_im,
           s5_log_step, s5_b_re, s5_b_im, s5_c_re, s5_c_im, s5_d, s5_w_glu, s5_b_glu, w_pb, w_out,
           norm_ffn_g, w_rg, b_rg, w_rexp, b_rexp, w_gate, w_up, w_down, norm_final_g):
    assert state_mlstm_C.shape[0] == 1 and x_sample.shape[1] == 1
    batch, seq, _ = x_prompt.shape
    dec = x_sample.shape[0]
    t_p = batch * seq
    t_all = t_p + dec
    assert seq % CHUNK == 0 and t_p % dec == 0 and dec % LANES == 0
    d_a = H_A * DK
    tm_p = _pick_tile(t_p, (512, 256, 128))
    tm_all = _pick_tile(t_all, (640, 384, 128))
    sample_blk = t_p // dec

    xp = x_prompt.reshape(t_p, D_MODEL)
    xs = x_sample.reshape(dec, D_MODEL)

    g_mix = norm_mix_g[0]
    xn_all = _rmsnorm_rows(xp, g_mix, t_all, 0, tm_p)
    xn_all = _rmsnorm_rows(xs, g_mix, t_all, t_p, dec, alias=xn_all)
    w_in_t = w_in.reshape(w_in.shape[1:]).T
    n_qkvo = 4 * d_a
    tn = 1024
    qkvo = _matmul_t(xn_all, w_in_t, [j * tn for j in range(n_qkvo // tn)], jnp.zeros((1, n_qkvo), F32),
                     tm_all, tn, F32)
    n_gate_cols = 2 * H_A
    b_gates = jnp.pad(jnp.concatenate([b_i[0], b_f[0]]), (0, LANES - n_gate_cols)).reshape(1, LANES)
    gates = _matmul_t(xn_all, w_in_t, [n_qkvo], b_gates, tm_all, LANES, F32)
    c_u = n_qkvo + n_gate_cols
    c_ga = c_u + D_B
    starts = [c_ga + j * tn for j in range(2 * D_MODEL // tn)] + [c_u]
    proj2 = _matmul_t(xn_all, w_in_t, starts, jnp.zeros((1, len(starts) * tn), F32), tm_all, tn, F32)
    u_col0 = 2 * D_MODEL

    gates_t = gates[:t_p, :n_gate_cols].T
    head_g = head_norm_g[0].reshape(1, d_a)
    hn_all, c_p, n_p, m_p = _mlstm_prompt(qkvo, gates, gates_t, w_conv[0], b_conv[0].reshape(1, -1),
                                          head_g, batch, seq, t_all)
    conv_p = jnp.stack([qkvo[b * seq + seq - (CONV_W - 1):(b + 1) * seq, :2 * d_a] for b in range(batch)])

    d_row = s5_d[0].reshape(1, D_B)
    s5_prompt_w, s5_sample_w = _s5_chunk_params(s5_a_re[0], s5_a_im[0], s5_log_step[0], s5_b_re[0],
                                                s5_b_im[0], s5_c_re[0], s5_c_im[0])
    ys_all, s5re_p, s5im_p = _s5_prompt(proj2, u_col0, *s5_prompt_w, d_row, batch, seq, t_all)

    conv_s_in = state_conv[0].reshape(dec, (CONV_W - 1) * 2 * d_a)
    q_s, kw_s, a_e, s_e, den_e, conv_s, n_s, m_s = _sample_pre(
        qkvo, conv_s_in, w_conv[0], b_conv[0].reshape(1, -1), gates, state_mlstm_m[0],
        state_mlstm_n[0].reshape(dec, d_a), dec, sample_blk)
    v_s = qkvo[t_p:, 2 * d_a:3 * d_a]
    r3 = lambda a: a.reshape(dec, H_A, DK)
    c_s, qc = _sample_c(r3(q_s), r3(kw_s), r3(v_s), r3(a_e), state_mlstm_C[0], dec)
    hn_all = _sample_post(qc.reshape(dec, d_a), s_e, a_e, den_e, qkvo, head_g, hn_all, dec, sample_blk)

    ys_all, s5re_s, s5im_s = _s5_sample(proj2, u_col0, state_s5_re[0].reshape(dec, -1),
                                        state_s5_im[0].reshape(dec, -1),
                                        *s5_sample_w, d_row, ys_all, dec, sample_blk)

    wr = jnp.pad(jnp.concatenate([w_rg[0], w_rexp[0]], axis=1), ((0, 0), (0, LANES - N_GROUPS - N_EXPERTS)))
    wr_hi = wr.astype(BF16)
    wr_lo = (wr - wr_hi.astype(F32)).astype(BF16)
    b_r = jnp.pad(jnp.concatenate([b_rg[0], b_rexp[0]]), (0, LANES - N_GROUPS - N_EXPERTS)).reshape(1, LANES)
    merge_w = (w_pa[0].astype(BF16), s5_w_glu[0].astype(BF16), w_pb[0].astype(BF16), w_out[0].astype(BF16))
    b_glu = s5_b_glu[0].reshape(1, D_B)
    g_ffn = norm_ffn_g[0].reshape(1, D_MODEL)
    tm_m = _pick_tile(t_p, (256, 128))
    outs = _merge(xp, hn_all, ys_all, proj2, b_glu, g_ffn, wr_hi, wr_lo, b_r, *merge_w, t_all, 0, tm_m)
    x1_all, xn2_all, ids, wts = _merge(xs, hn_all, ys_all, proj2, b_glu, g_ffn, wr_hi, wr_lo, b_r,
                                       *merge_w, t_all, t_p, dec, aliases=outs)

    p_rows = -(-(2 * t_all + N_EXPERTS * (MOE_TILE - 1)) // MOE_TILE) * MOE_TILE
    pos, src, meta = _dispatch(ids, t_all, p_rows)
    take_rows = lambda a, idx: a.at[idx].get(mode='promise_in_bounds')
    xs_sorted = take_rows(xn2_all, src)
    yp = _moe_experts(*meta, xs_sorted, w_gate[0], w_up[0], w_down[0])
    yg0 = take_rows(yp, pos[:, 0])
    yg1 = take_rows(yp, pos[:, 1])

    g_fin = norm_final_g.reshape(1, D_MODEL)
    y_prompt = _final(x1_all, yg0, yg1, wts, g_fin, 0, t_p, tm_p).reshape(batch, seq, D_MODEL)
    y_sample = _final(x1_all, yg0, yg1, wts, g_fin, t_p, dec, dec).reshape(dec, 1, D_MODEL)

    lead = lambda a, shape: a.reshape((1,) + shape)
    return (y_prompt, y_sample,
            lead(c_p, (batch, H_A, DK, DK)), lead(n_p, (batch, H_A, DK)), lead(m_p[:, 0, :H_A], (batch, H_A)),
            lead(conv_p, (batch, CONV_W - 1, 2 * d_a)),
            lead(s5re_p, (batch, G_B, P_S5)), lead(s5im_p, (batch, G_B, P_S5)),
            lead(c_s, (dec, H_A, DK, DK)), lead(n_s, (dec, H_A, DK)), lead(m_s[:, :H_A], (dec, H_A)),
            lead(conv_s, (dec, CONV_W - 1, 2 * d_a)),
            lead(s5re_s, (dec, G_B, P_S5)), lead(s5im_s, (dec, G_B, P_S5)))
```

```python
import functools
import math

import jax
import jax.numpy as jnp
from jax import lax
from jax.experimental import pallas as pl
from jax.experimental.pallas import tpu as pltpu

F32 = jnp.float32
BF16 = jnp.bfloat16

D_MODEL = 2048
H_A = 8
DK = 256
CONV_W = 4
CHUNK = 128
D_B = 1024
S5_GROUP = 16
G_B = 64
P_S5 = 64
N_GROUPS = 4
EXP_PER_GROUP = 8
N_EXPERTS = 32
D_EXPERT = 512
EPS = 1e-6

LANES = 128
SUBLANES = 8
VMEM_LIMIT = 56 * 1024 * 1024

S5_CHUNK = 8
S5_CST = (LANES // S5_GROUP) * P_S5
MOE_TILE = 256


def _cparams(sem):
    return pltpu.CompilerParams(dimension_semantics=sem, vmem_limit_bytes=VMEM_LIMIT)


def _silu(x):
    return x * (1.0 / (1.0 + jnp.exp(-x)))


def _sigmoid(x):
    return 1.0 / (1.0 + jnp.exp(-x))


def _log_sigmoid(x):
    return jnp.minimum(x, 0.0) - jnp.log1p(jnp.exp(-jnp.abs(x)))


def _gelu_tanh(x):
    c = math.sqrt(2.0 / math.pi)
    return 0.5 * x * (1.0 + jnp.tanh(c * (x + 0.044715 * (x * x * x))))


def _split3(x):
    hi = x.astype(BF16)
    r = x - hi.astype(F32)
    mid = r.astype(BF16)
    lo = (r - mid.astype(F32)).astype(BF16)
    return hi, mid, lo


def _dot(a, b):
    return jnp.dot(a, b, preferred_element_type=F32)


def _dot_nt(a, b):
    return lax.dot_general(a, b, (((1,), (1,)), ((), ())), preferred_element_type=F32)


def _dot_tn(a, b):
    return lax.dot_general(a, b, (((0,), (0,)), ((), ())), preferred_element_type=F32)


def _rmsnorm_kernel(x_ref, g_ref, *rest):
    o_ref = rest[-1]
    x = x_ref[...]
    r = lax.rsqrt(jnp.mean(x * x, axis=-1, keepdims=True) + EPS)
    o_ref[...] = (x * r * g_ref[...]).astype(o_ref.dtype)


def _rmsnorm_rows(x, g, t_all, row0, tm, alias=None):
    n = x.shape[0]
    blk0 = row0 // tm
    in_specs = [pl.BlockSpec((tm, D_MODEL), lambda i: (i, 0)),
                pl.BlockSpec((1, D_MODEL), lambda i: (0, 0))]
    args = [x, g.reshape(1, D_MODEL)]
    aliases = {}
    if alias is not None:
        in_specs.append(pl.BlockSpec(memory_space=pl.ANY))
        args.append(alias)
        aliases = {2: 0}
    return pl.pallas_call(
        _rmsnorm_kernel,
        out_shape=jax.ShapeDtypeStruct((t_all, D_MODEL), BF16),
        grid=(n // tm,),
        in_specs=in_specs,
        out_specs=pl.BlockSpec((tm, D_MODEL), lambda i: (i + blk0, 0)),
        input_output_aliases=aliases,
        compiler_params=_cparams(("arbitrary",)),
        name="rmsnorm_rows",
    )(*args)


def _mm_kernel(a_ref, w_ref, b_ref, o_ref, wb_ref):
    @pl.when(pl.program_id(1) == 0)
    def _():
        wb_ref[...] = w_ref[...].astype(BF16)

    o_ref[...] = (_dot(a_ref[...], wb_ref[...]) + b_ref[...]).astype(o_ref.dtype)


def _matmul(a, w, bias, n_out, tm, tn, out_dtype, col_blk0=0):
    m, k = a.shape
    return pl.pallas_call(
        _mm_kernel,
        out_shape=jax.ShapeDtypeStruct((m, n_out), out_dtype),
        grid=(n_out // tn, m // tm),
        in_specs=[pl.BlockSpec((tm, k), lambda j, i: (i, 0)),
                  pl.BlockSpec((k, tn), lambda j, i: (0, j + col_blk0)),
                  pl.BlockSpec((1, tn), lambda j, i: (0, j))],
        out_specs=pl.BlockSpec((tm, tn), lambda j, i: (i, j)),
        scratch_shapes=[pltpu.VMEM((k, tn), BF16)],
        compiler_params=_cparams(("arbitrary", "arbitrary")),
        name="rows_matmul",
    )(a, w, bias)


def _mm_t_kernel(starts_ref, a_ref, wt_ref, b_ref, o_ref, wb_ref):
    del starts_ref

    @pl.when(pl.program_id(1) == 0)
    def _():
        wb_ref[...] = wt_ref[...].astype(BF16)

    o_ref[...] = (_dot_nt(a_ref[...], wb_ref[...]) + b_ref[...]).astype(o_ref.dtype)


def _matmul_t(a, w_t, row_starts, bias, tm, tn, out_dtype):
    m, k = a.shape
    n_t = len(row_starts)
    assert all(s % SUBLANES == 0 for s in row_starts)
    grid_spec = pltpu.PrefetchScalarGridSpec(
        num_scalar_prefetch=1,
        grid=(n_t, m // tm),
        in_specs=[pl.BlockSpec((tm, k), lambda j, i, st: (i, 0)),
                  pl.BlockSpec((pl.Element(tn), pl.Element(k)), lambda j, i, st: (st[j] * SUBLANES, 0)),
                  pl.BlockSpec((1, tn), lambda j, i, st: (0, j))],
        out_specs=pl.BlockSpec((tm, tn), lambda j, i, st: (i, j)),
        scratch_shapes=[pltpu.VMEM((tn, k), BF16)],
    )
    return pl.pallas_call(
        _mm_t_kernel,
        out_shape=jax.ShapeDtypeStruct((m, n_t * tn), out_dtype),
        grid_spec=grid_spec,
        compiler_params=_cparams(("arbitrary", "arbitrary")),
        name="rows_matmul_t",
    )(jnp.asarray([s // SUBLANES for s in row_starts], jnp.int32), a, w_t, bias)


def _mm_shift_kernel(a_ref, wlo_ref, whi_ref, o_ref, wb_ref, *, shift):
    tn = wlo_ref.shape[1]
    rows = 256

    @pl.when(pl.program_id(1) == 0)
    def _():
        for r in range(wlo_ref.shape[0] // rows):
            sl = slice(r * rows, (r + 1) * rows)
            w = jnp.concatenate([wlo_ref[sl, :], whi_ref[sl, :]], axis=1)
            wb_ref[sl, :] = w[:, shift:shift + tn].astype(BF16)

    o_ref[...] = _dot(a_ref[...], wb_ref[...]).astype(o_ref.dtype)


def _matmul_shifted(a, w, col_starts, tm, tn, out_dtype):
    m, k = a.shape
    shift = col_starts[0] % tn
    assert all(c % tn == shift for c in col_starts) and 0 < shift < LANES
    lo_blk = [c // tn for c in col_starts]
    n_t = len(col_starts)
    base, first = min(lo_blk), lo_blk[0]
    assert all(lo_blk[j] == base + (j + first - base) % n_t for j in range(n_t))
    lo_idx = lambda j: base + (j + (first - base)) % n_t
    per = tn // LANES
    kern = functools.partial(_mm_shift_kernel, shift=shift)
    return pl.pallas_call(
        kern,
        out_shape=jax.ShapeDtypeStruct((m, n_t * tn), out_dtype),
        grid=(n_t, m // tm),
        in_specs=[pl.BlockSpec((tm, k), lambda j, i: (i, 0)),
                  pl.BlockSpec((k, tn), lambda j, i: (0, lo_idx(j))),
                  pl.BlockSpec((k, LANES), lambda j, i: (0, (lo_idx(j) + 1) * per))],
        out_specs=pl.BlockSpec((tm, tn), lambda j, i: (i, j)),
        scratch_shapes=[pltpu.VMEM((k, tn), BF16)],
        compiler_params=_cparams(("arbitrary", "arbitrary")),
        name="rows_matmul_shifted",
    )(a, w, w)


def _sample_c_update(q_ref, kw_ref, v_ref, a_ref, c_ref, c_out, qc_out):
    eye = jnp.where(lax.broadcasted_iota(jnp.int32, (DK, DK), 0)
                    == lax.broadcasted_iota(jnp.int32, (DK, DK), 1), 1.0, 0.0).astype(BF16)
    for b in range(q_ref.shape[0]):
        q_t = sum(_dot_nt(eye, p) for p in _split3(q_ref[b]))
        kw_t = sum(_dot_nt(eye, p) for p in _split3(kw_ref[b]))
        qc_rows = []
        for h in range(H_A):
            c_prev = c_ref[b, h]
            qc_rows.append(jnp.sum(q_t[:, h:h + 1] * c_prev, axis=0, keepdims=True))
            c_out[b, h] = a_ref[b, h:h + 1, :] * c_prev + kw_t[:, h:h + 1] * v_ref[b, h:h + 1, :]
        qc_out[b] = jnp.concatenate(qc_rows, axis=0)


def _mlstm_kernel(qk_ref, v_ref, o_ref, gcol_ref, grow_ref, wc_ref, bc_ref, hg_ref,
                  sq_ref, skw_ref, sv_ref, sa_ref, sc_ref,
                  h_ref, c_out, n_out, m_out, sc_out, sqc_out, c_sc, n_sc, m_sc, ext_sc):
    c = pl.program_id(1)
    _sample_c_update(sq_ref, skw_ref, sv_ref, sa_ref, sc_ref, sc_out, sqc_out)
    L = CHUNK
    pad = SUBLANES
    d_a = H_A * DK

    @pl.when(c == 0)
    def _():
        c_sc[...] = jnp.zeros_like(c_sc)
        n_sc[...] = jnp.zeros_like(n_sc)
        m_sc[...] = jnp.zeros_like(m_sc)
        ext_sc[0:pad, :] = jnp.zeros((pad, 2 * d_a), F32)

    x = qk_ref[...]
    ext_sc[pad:pad + L, :] = x
    y = bc_ref[...] + wc_ref[CONV_W - 1:CONV_W, :] * x
    for j in range(1, CONV_W):
        y = y + wc_ref[CONV_W - 1 - j:CONV_W - j, :] * ext_sc[pad - j:pad - j + L, :]
    ext_sc[0:pad, :] = x[L - pad:L, :]
    y = _silu(y)

    gcol = gcol_ref[...]
    grow = grow_ref[...]
    ri = lax.broadcasted_iota(jnp.int32, (L, L), 0)
    ci = lax.broadcasted_iota(jnp.int32, (L, L), 1)
    causal = ci <= ri
    tril = jnp.where(causal, 1.0, 0.0).astype(BF16)
    triu = jnp.where(ri <= ci, 1.0, 0.0).astype(BF16)
    b_cols = sum(_dot(tril, p) for p in _split3(_log_sigmoid(gcol)))
    b_rows = sum(_dot(p, triu) for p in _split3(_log_sigmoid(grow)))

    for h in range(H_A):
        sl = slice(h * DK, (h + 1) * DK)
        q = y[:, sl] * (DK ** -0.5)
        k = y[:, d_a + h * DK:d_a + (h + 1) * DK]
        ig_col = gcol[:, h:h + 1]
        ig_row = grow[h:h + 1, :]
        b_col = b_cols[:, H_A + h:H_A + h + 1]
        b_row = b_rows[H_A + h:H_A + h + 1, :]

        m_prev = m_sc[:, h:h + 1]
        d_log = jnp.where(causal, b_col - b_row + ig_row, -jnp.inf)
        inter_log = b_col + m_prev
        m_t = jnp.maximum(inter_log, jnp.max(d_log, axis=1, keepdims=True))
        qb = q.astype(BF16)
        kb = k.astype(BF16)
        vb = v_ref[:, sl].astype(BF16)
        s = _dot_nt(qb, kb) * jnp.exp(d_log - m_t)
        inter_w = jnp.exp(inter_log - m_t)
        c_prev = c_sc[h]
        n_prev = n_sc[h:h + 1, :]
        num = _dot(s.astype(BF16), vb) + inter_w * _dot(qb, c_prev.astype(BF16))
        nq = jnp.sum(s, axis=1, keepdims=True) + inter_w * jnp.sum(q * n_prev, axis=1, keepdims=True)
        den = jnp.maximum(jnp.abs(nq), jnp.exp(-m_t))
        hh = num / den
        hh = hh * _sigmoid(o_ref[:, sl])
        hh = hh * lax.rsqrt(jnp.mean(hh * hh, axis=1, keepdims=True) + EPS)
        h_ref[:, sl] = (hh * hg_ref[:, sl]).astype(h_ref.dtype)

        m_new = m_t[L - 1:L, :]
        b_last = b_col[L - 1:L, :]
        decay = jnp.exp(b_last + m_prev - m_new)
        w_end = jnp.exp(b_last - b_col + ig_col - m_new)
        kw = k * w_end
        c_sc[h] = decay * c_prev + _dot_tn(kw.astype(BF16), vb)
        n_sc[h:h + 1, :] = decay * n_prev + jnp.sum(kw, axis=0, keepdims=True)
        m_sc[:, h:h + 1] = m_new

    @pl.when(c == pl.num_programs(1) - 1)
    def _():
        c_out[...] = c_sc[...]
        n_out[...] = n_sc[...]
        m_out[...] = m_sc[...]


def _mlstm_prompt(qkvo, gates, gates_t, w_conv, b_conv, head_g, sample, batch, seq, t_all):
    nc = seq // CHUNK
    L = CHUNK
    d_a = H_A * DK
    dec = sample[0].shape[0]
    nb = dec // (batch * nc)
    assert nb * batch * nc == dec
    row = lambda b, c: b * nc + c
    svec = pl.BlockSpec((nb, H_A, DK), lambda b, c: (row(b, c), 0, 0))
    smat = pl.BlockSpec((nb, H_A, DK, DK), lambda b, c: (row(b, c), 0, 0, 0))
    in_specs = [
        pl.BlockSpec((L, 2 * d_a), lambda b, c: (row(b, c), 0)),
        pl.BlockSpec((L, d_a), lambda b, c: (row(b, c), 2)),
        pl.BlockSpec((L, d_a), lambda b, c: (row(b, c), 3)),
        pl.BlockSpec((L, LANES), lambda b, c: (row(b, c), 0)),
        pl.BlockSpec((2 * H_A, L), lambda b, c: (0, row(b, c))),
        pl.BlockSpec((CONV_W, 2 * d_a), lambda b, c: (0, 0)),
        pl.BlockSpec((1, 2 * d_a), lambda b, c: (0, 0)),
        pl.BlockSpec((1, d_a), lambda b, c: (0, 0)),
        svec, svec, svec, svec, smat,
    ]
    out_shape = (
        jax.ShapeDtypeStruct((t_all, d_a), BF16),
        jax.ShapeDtypeStruct((batch, H_A, DK, DK), F32),
        jax.ShapeDtypeStruct((batch, H_A, DK), F32),
        jax.ShapeDtypeStruct((batch, 1, LANES), F32),
        jax.ShapeDtypeStruct((dec, H_A, DK, DK), F32),
        jax.ShapeDtypeStruct((dec, H_A, DK), F32),
    )
    out_specs = (
        pl.BlockSpec((L, d_a), lambda b, c: (row(b, c), 0)),
        pl.BlockSpec((None, H_A, DK, DK), lambda b, c: (b, 0, 0, 0)),
        pl.BlockSpec((None, H_A, DK), lambda b, c: (b, 0, 0)),
        pl.BlockSpec((None, 1, LANES), lambda b, c: (b, 0, 0)),
        smat, svec,
    )
    return pl.pallas_call(
        _mlstm_kernel,
        out_shape=out_shape,
        grid=(batch, nc),
        in_specs=in_specs,
        out_specs=out_specs,
        scratch_shapes=[pltpu.VMEM((H_A, DK, DK), F32), pltpu.VMEM((H_A, DK), F32),
                        pltpu.VMEM((1, LANES), F32), pltpu.VMEM((SUBLANES + L, 2 * d_a), F32)],
        compiler_params=_cparams(("arbitrary", "arbitrary")),
        name="mlstm_prompt",
    )(qkvo, qkvo, qkvo, gates, gates_t, w_conv, b_conv, head_g, *sample)


def _lane_tile(x, reps):
    w = x.shape[1]
    ri = lax.broadcasted_iota(jnp.int32, (w, w * reps), 0)
    ci = lax.broadcasted_iota(jnp.int32, (w, w * reps), 1)
    rep = jnp.where(jnp.bitwise_and(ci, w - 1) == ri, 1.0, 0.0).astype(BF16)
    return _dot(x.astype(BF16), rep)


def _group_mask(rows, cols, row_per, col_per):
    ri = lax.broadcasted_iota(jnp.int32, (rows, cols), 0)
    ci = lax.broadcasted_iota(jnp.int32, (rows, cols), 1)
    return (jnp.right_shift(ri, int(math.log2(row_per))) == jnp.right_shift(ci, int(math.log2(col_per))))


def _s5_build_weights(bin_r, bin_i, kdt, aout_r, aout_i, winc_ref, wintra_ref, wout_ref):
    L = S5_CHUNK
    gpb = LANES // S5_GROUP
    m_inc = _group_mask(LANES, S5_CST, S5_GROUP, P_S5)
    m_lag = _group_mask(LANES, LANES, S5_GROUP, S5_GROUP)
    zero = jnp.zeros((LANES, LANES), BF16)
    lag = [jnp.where(m_lag, _lane_tile(kdt[d], gpb), 0.0).astype(BF16) for d in range(L)]
    for t in range(L):
        rows = slice(t * LANES, (t + 1) * LANES)
        d = L - 1 - t
        winc_ref[rows, 0:S5_CST] = jnp.where(m_inc, _lane_tile(bin_r[d], gpb), 0.0).astype(BF16)
        winc_ref[rows, S5_CST:2 * S5_CST] = jnp.where(m_inc, _lane_tile(bin_i[d], gpb), 0.0).astype(BF16)
        wout_ref[rows, 0:S5_CST] = jnp.where(m_inc, _lane_tile(aout_r[t], gpb), 0.0).astype(BF16)
        wout_ref[rows, S5_CST:2 * S5_CST] = jnp.where(m_inc, _lane_tile(aout_i[t], gpb), 0.0).astype(BF16)
        for t2 in range(L):
            wintra_ref[rows, t2 * LANES:(t2 + 1) * LANES] = lag[t2 - t] if t2 >= t else zero


def _s5_prompt_kernel(u_ref, binr_ref, bini_ref, kdt_ref, aoutr_ref, aouti_ref, lbr_ref, lbi_ref, d_ref,
                      ys_ref, sre_ref, sim_ref, x_sc, winc_ref, wintra_ref, wout_ref, *, batch, nchunk):
    L = S5_CHUNK
    nrow = batch * nchunk
    nst = S5_CST // LANES
    rstr = nchunk + SUBLANES
    _s5_build_weights(binr_ref, bini_ref, kdt_ref, aoutr_ref, aouti_ref, winc_ref, wintra_ref, wout_ref)
    u_t = [u_ref[pl.ds(t, nrow, stride=L), :] for t in range(L)]
    lhs = jnp.concatenate([a.astype(BF16) for a in u_t], axis=1)
    inc = _dot(lhs, winc_ref[...])
    for j in range(2 * nst):
        for b in range(batch):
            x_sc[j, b * rstr:b * rstr + nchunk, :] = inc[b * nchunk:(b + 1) * nchunk, j * LANES:(j + 1) * LANES]

    lbr = [jnp.broadcast_to(lbr_ref[:, j * LANES:(j + 1) * LANES], (batch, LANES)) for j in range(nst)]
    lbi = [jnp.broadcast_to(lbi_ref[:, j * LANES:(j + 1) * LANES], (batch, LANES)) for j in range(nst)]

    def scan_body(r, carry):
        rows = pl.ds(r, batch, stride=rstr)
        out = []
        for j in range(nst):
            xr, xi = carry[j]
            ir = x_sc[j, rows, :]
            ii = x_sc[nst + j, rows, :]
            x_sc[j, rows, :] = xr
            x_sc[nst + j, rows, :] = xi
            out.append((lbr[j] * xr - lbi[j] * xi + ir, lbr[j] * xi + lbi[j] * xr + ii))
        return tuple(out)

    z = jnp.zeros((batch, LANES), F32)
    fin = lax.fori_loop(0, nchunk, scan_body, tuple((z, z) for _ in range(nst)))
    for j in range(nst):
        sre_ref[:, j * LANES:(j + 1) * LANES] = fin[j][0]
        sim_ref[:, j * LANES:(j + 1) * LANES] = fin[j][1]

    xprev = jnp.concatenate(
        [jnp.concatenate([x_sc[j, b * rstr:b * rstr + nchunk, :] for b in range(batch)], axis=0)
         for j in range(2 * nst)], axis=1).astype(BF16)
    y = _dot(lhs, wintra_ref[...]) + _dot_nt(xprev, wout_ref[...])
    for t in range(L):
        yt = y[:, t * LANES:(t + 1) * LANES] + d_ref[...] * u_t[t]
        ys_ref[pl.ds(t, nrow, stride=L), :] = _gelu_tanh(yt).astype(ys_ref.dtype)


def _s5_prompt(proj2, u_col0, bin_r, bin_i, kdt, aout_r, aout_i, lb8r, lb8i, d_row, batch, seq, t_all):
    nchunk = seq // S5_CHUNK
    t_p = batch * seq
    kern = functools.partial(_s5_prompt_kernel, batch=batch, nchunk=nchunk)
    ub0 = u_col0 // LANES
    ncb = D_B // LANES
    kw = S5_CHUNK * LANES
    per_blk = lambda a: pl.BlockSpec((a.shape[0], None) + a.shape[2:], lambda g: (0, g, 0, 0))
    return pl.pallas_call(
        kern,
        out_shape=(jax.ShapeDtypeStruct((t_all, D_B), F32),
                   jax.ShapeDtypeStruct((batch, G_B * P_S5), F32),
                   jax.ShapeDtypeStruct((batch, G_B * P_S5), F32)),
        grid=(ncb,),
        in_specs=[pl.BlockSpec((t_p, LANES), lambda g: (0, ub0 + g)),
                  per_blk(bin_r), per_blk(bin_i), per_blk(kdt), per_blk(aout_r), per_blk(aout_i),
                  pl.BlockSpec((None, 1, S5_CST), lambda g: (g, 0, 0)),
                  pl.BlockSpec((None, 1, S5_CST), lambda g: (g, 0, 0)),
                  pl.BlockSpec((1, LANES), lambda g: (0, g))],
        out_specs=(pl.BlockSpec((t_p, LANES), lambda g: (0, g)),
                   pl.BlockSpec((batch, S5_CST), lambda g: (0, g)),
                   pl.BlockSpec((batch, S5_CST), lambda g: (0, g))),
        scratch_shapes=[pltpu.VMEM((2 * S5_CST // LANES, batch * (nchunk + SUBLANES), LANES), F32),
                        pltpu.VMEM((kw, 2 * S5_CST), BF16), pltpu.VMEM((kw, kw), BF16),
                        pltpu.VMEM((kw, 2 * S5_CST), BF16)],
        compiler_params=_cparams(("arbitrary",)),
        name="s5_prompt",
    )(proj2, bin_r, bin_i, kdt, aout_r, aout_i, lb8r, lb8i, d_row)


def _sample_pre_kernel(qk_ref, conv_ref, wc_ref, bc_ref, g_ref, m_ref, n_ref,
                       q_out, kw_out, a_out, s_out, den_out, conv_out, n_out, m_out):
    c2 = 2 * H_A * DK
    x_new = qk_ref[...]
    y = bc_ref[...] + wc_ref[CONV_W - 1:CONV_W, :] * x_new
    for j in range(CONV_W - 1):
        y = y + wc_ref[j:j + 1, :] * conv_ref[:, j * c2:(j + 1) * c2]
    y = _silu(y)
    conv_out[:, 0:(CONV_W - 2) * c2] = conv_ref[:, c2:(CONV_W - 1) * c2]
    conv_out[:, (CONV_W - 2) * c2:(CONV_W - 1) * c2] = x_new
    g = g_ref[...]
    bd = x_new.shape[0]
    m_cols = []
    for h in range(H_A):
        sl = slice(h * DK, (h + 1) * DK)
        q = y[:, sl] * (DK ** -0.5)
        k = y[:, H_A * DK + h * DK:H_A * DK + (h + 1) * DK]
        ig = g[:, h:h + 1]
        lf = _log_sigmoid(g[:, H_A + h:H_A + h + 1])
        m_prev = m_ref[:, h:h + 1]
        m_t = jnp.maximum(lf + m_prev, ig)
        a = jnp.exp(lf + m_prev - m_t)
        wgt = jnp.exp(ig - m_t)
        n_prev = n_ref[:, sl]
        s = jnp.sum(q * k, axis=1, keepdims=True) * wgt
        nq = s + a * jnp.sum(q * n_prev, axis=1, keepdims=True)
        den = jnp.maximum(jnp.abs(nq), jnp.exp(-m_t))
        kw = wgt * k
        q_out[:, sl] = q
        kw_out[:, sl] = kw
        a_out[:, sl] = jnp.broadcast_to(a, (bd, DK))
        s_out[:, sl] = jnp.broadcast_to(s, (bd, DK))
        den_out[:, sl] = jnp.broadcast_to(den, (bd, DK))
        n_out[:, sl] = a * n_prev + kw
        m_cols.append(m_t)
    lane = lax.broadcasted_iota(jnp.int32, (bd, LANES), 1)
    m_full = jnp.zeros((bd, LANES), F32)
    for h in range(H_A):
        m_full = jnp.where(lane == h, m_cols[h], m_full)
    m_out[...] = m_full


def _sample_pre(qkvo, conv_state, w_conv, b_conv, gates, m_state, n_state, dec, row_blk):
    c2 = 2 * H_A * DK
    d = H_A * DK
    full = lambda shape: pl.BlockSpec(shape, lambda i: (0,) * len(shape))
    rows = lambda: jax.ShapeDtypeStruct((dec, d), F32)
    return pl.pallas_call(
        _sample_pre_kernel,
        out_shape=(rows(), rows(), rows(), rows(), rows(),
                   jax.ShapeDtypeStruct((dec, (CONV_W - 1) * c2), F32), rows(),
                   jax.ShapeDtypeStruct((dec, LANES), F32)),
        grid=(1,),
        in_specs=[pl.BlockSpec((dec, c2), lambda i: (row_blk, 0)),
                  full((dec, (CONV_W - 1) * c2)), full((CONV_W, c2)), full((1, c2)),
                  pl.BlockSpec((dec, LANES), lambda i: (row_blk, 0)),
                  full((dec, H_A)), full((dec, d))],
        out_specs=(full((dec, d)), full((dec, d)), full((dec, d)), full((dec, d)), full((dec, d)),
                   full((dec, (CONV_W - 1) * c2)), full((dec, d)), full((dec, LANES))),
        compiler_params=_cparams(("arbitrary",)),
        name="sample_pre",
    )(qkvo, conv_state, w_conv, b_conv, gates, m_state, n_state)


def _sample_post_kernel(qc_ref, s_ref, a_ref, den_ref, vo_ref, hg_ref, hn_in, h_ref):
    del hn_in
    d = H_A * DK
    num = s_ref[...] * vo_ref[:, 0:d] + a_ref[...] * qc_ref[...]
    hh = num / den_ref[...]
    hh = hh * _sigmoid(vo_ref[:, d:2 * d])
    for h in range(H_A):
        sl = slice(h * DK, (h + 1) * DK)
        seg = hh[:, sl]
        seg = seg * lax.rsqrt(jnp.mean(seg * seg, axis=1, keepdims=True) + EPS)
        h_ref[:, sl] = (seg * hg_ref[:, sl]).astype(h_ref.dtype)


def _sample_post(qc, s_e, a_e, den_e, qkvo, head_g, hn_all, dec, row_blk):
    d = H_A * DK
    full = lambda shape: pl.BlockSpec(shape, lambda i: (0,) * len(shape))
    return pl.pallas_call(
        _sample_post_kernel,
        out_shape=jax.ShapeDtypeStruct(hn_all.shape, hn_all.dtype),
        grid=(1,),
        in_specs=[full((dec, d)), full((dec, d)), full((dec, d)), full((dec, d)),
                  pl.BlockSpec((dec, 2 * d), lambda i: (row_blk, 1)),
                  full((1, d)), pl.BlockSpec(memory_space=pl.ANY)],
        out_specs=pl.BlockSpec((dec, d), lambda i: (row_blk, 0)),
        input_output_aliases={6: 0},
        compiler_params=_cparams(("arbitrary",)),
        name="sample_post",
    )(qc, s_e, a_e, den_e, qkvo, head_g, hn_all)


def _s5_sample_kernel(u_ref, sr_ref, si_ref, br_ref, bi_ref, cr_ref, ci_ref, lbr_ref, lbi_ref, d_ref, ys_in,
                      ys_ref, sre_out, sim_out):
    del ys_in
    gpb = LANES // S5_GROUP
    mask = _group_mask(LANES, S5_CST, S5_GROUP, P_S5)
    expand = lambda blk: jnp.where(mask, _lane_tile(blk, gpb), 0.0)
    for g in range(D_B // LANES):
        ch = slice(g * LANES, (g + 1) * LANES)
        sl = slice(g * S5_CST, (g + 1) * S5_CST)
        u = u_ref[:, ch]
        bmat = jnp.concatenate([expand(br_ref[g]), expand(bi_ref[g])], axis=1).astype(BF16)
        bu = _dot(u.astype(BF16), bmat)
        lbr = lbr_ref[g]
        lbi = lbi_ref[g]
        sr = sr_ref[:, sl]
        si = si_ref[:, sl]
        xr = lbr * sr - lbi * si + bu[:, 0:S5_CST]
        xi = lbr * si + lbi * sr + bu[:, S5_CST:2 * S5_CST]
        sre_out[:, sl] = xr
        sim_out[:, sl] = xi
        x = jnp.concatenate([xr, xi], axis=1).astype(BF16)
        cmat = jnp.concatenate([expand(cr_ref[g]), -expand(ci_ref[g])], axis=1).astype(BF16)
        y = _dot_nt(x, cmat) + d_ref[:, ch] * u
        ys_ref[:, ch] = _gelu_tanh(y).astype(ys_ref.dtype)


def _s5_sample(proj2, u_col0, s_re, s_im, b_r, b_i, c_r, c_i, lbr, lbi, d_row, ys_all, dec, row_blk):
    full = lambda shape: pl.BlockSpec(shape, lambda i: (0,) * len(shape))
    n_state = G_B * P_S5
    ub0 = u_col0 // D_B
    params = (b_r, b_i, c_r, c_i, lbr, lbi)
    return pl.pallas_call(
        _s5_sample_kernel,
        out_shape=(jax.ShapeDtypeStruct(ys_all.shape, ys_all.dtype),
                   jax.ShapeDtypeStruct((dec, n_state), F32),
                   jax.ShapeDtypeStruct((dec, n_state), F32)),
        grid=(1,),
        in_specs=[pl.BlockSpec((dec, D_B), lambda i: (row_blk, ub0)),
                  full((dec, n_state)), full((dec, n_state))]
                 + [full(p.shape) for p in params]
                 + [full((1, D_B)), pl.BlockSpec(memory_space=pl.ANY)],
        out_specs=(pl.BlockSpec((dec, D_B), lambda i: (row_blk, 0)),
                   full((dec, n_state)), full((dec, n_state))),
        input_output_aliases={10: 0},
        compiler_params=_cparams(("arbitrary",)),
        name="s5_sample",
    )(proj2, s_re, s_im, *params, d_row, ys_all)


def _merge_kernel(x_ref, hn_ref, ys_ref, ga_ref, gb_ref, bglu_ref, gffn_ref, wrh_ref, wrl_ref, br_ref,
                  wpa_hbm, wglu_hbm, wpb_hbm, wout_hbm, *rest):
    n_alias = len(rest) - 9
    x1_ref, xn_ref, ids_ref, wts_ref = rest[n_alias:n_alias + 4]
    wpa, wglu, wpb, wout, sem = rest[n_alias + 4:]

    @pl.when(pl.program_id(0) == 0)
    def _():
        copies = [pltpu.make_async_copy(src, dst, sem.at[i])
                  for i, (src, dst) in enumerate(((wpa_hbm, wpa), (wglu_hbm, wglu),
                                                  (wpb_hbm, wpb), (wout_hbm, wout)))]
        for cp in copies:
            cp.start()
        for cp in copies:
            cp.wait()

    ya = _dot(hn_ref[...], wpa[...])
    ys = ys_ref[...]
    gate = _sigmoid(_dot(ys.astype(BF16), wglu[...]) + bglu_ref[...])
    yb = _dot((ys * gate).astype(BF16), wpb[...])
    z = _sigmoid(ga_ref[...]) * ya + _sigmoid(gb_ref[...]) * yb
    x1 = x_ref[...] + _dot(z.astype(BF16), wout[...])
    x1_ref[...] = x1
    xn = x1 * lax.rsqrt(jnp.mean(x1 * x1, axis=1, keepdims=True) + EPS) * gffn_ref[...]
    xn_ref[...] = xn
    xh = xn.astype(BF16)
    xl = (xn - xh.astype(F32)).astype(BF16)
    logits = _dot(xh, wrh_ref[...]) + _dot(xl, wrh_ref[...]) + _dot(xh, wrl_ref[...]) + br_ref[...]

    lane_i = lax.broadcasted_iota(jnp.int32, logits.shape, 1)
    lane = lane_i.astype(F32)
    neg = -jnp.inf
    big = float(1 << 20)
    gl = jnp.where(lane_i < N_GROUPS, logits, neg)
    gmax = jnp.max(gl, axis=1, keepdims=True)
    gsum = jnp.sum(jnp.exp(gl - gmax), axis=1, keepdims=True)
    gidx = jnp.min(jnp.where(gl == gmax, lane, big), axis=1, keepdims=True)
    pg_sel = 1.0 / gsum
    lo = N_GROUPS + gidx * EXP_PER_GROUP
    in_grp = (lane >= lo) & (lane < lo + EXP_PER_GROUP)
    el = jnp.where(in_grp, logits, neg)
    emax = jnp.max(el, axis=1, keepdims=True)
    ee = jnp.exp(el - emax)
    pe = ee / jnp.sum(ee, axis=1, keepdims=True)
    v0 = jnp.max(pe, axis=1, keepdims=True)
    i0 = jnp.min(jnp.where(in_grp & (pe == v0), lane, big), axis=1, keepdims=True)
    rest_m = in_grp & (lane != i0)
    pe1 = jnp.where(rest_m, pe, neg)
    v1 = jnp.max(pe1, axis=1, keepdims=True)
    i1 = jnp.min(jnp.where(rest_m & (pe1 == v1), lane, big), axis=1, keepdims=True)
    tot = v0 + v1
    w0 = pg_sel * (v0 / tot)
    w1 = pg_sel * (v1 / tot)
    ids = jnp.where(lane_i == 0, i0 - N_GROUPS, jnp.where(lane_i == 1, i1 - N_GROUPS, 0.0))
    ids_ref[...] = ids.astype(jnp.int32)
    wts_ref[...] = jnp.where(lane_i == 0, w0, jnp.where(lane_i == 1, w1, 0.0))


def _merge(x, hn_all, ys_all, proj2, b_glu, g_ffn, wr_hi, wr_lo, b_r, wpa, wglu, wpb, wout,
           t_all, row0, tm, aliases=None):
    n = x.shape[0]
    blk0 = row0 // tm
    const = lambda shape: pl.BlockSpec(shape, lambda i: (0,) * len(shape))
    any_spec = pl.BlockSpec(memory_space=pl.ANY)
    in_specs = [pl.BlockSpec((tm, D_MODEL), lambda i: (i, 0)),
                pl.BlockSpec((tm, D_MODEL), lambda i: (i + blk0, 0)),
                pl.BlockSpec((tm, D_B), lambda i: (i + blk0, 0)),
                pl.BlockSpec((tm, D_MODEL), lambda i: (i + blk0, 0)),
                pl.BlockSpec((tm, D_MODEL), lambda i: (i + blk0, 1)),
                const((1, D_B)), const((1, D_MODEL)),
                const((D_MODEL, LANES)), const((D_MODEL, LANES)), const((1, LANES)),
                any_spec, any_spec, any_spec, any_spec]
    args = [x, hn_all, ys_all, proj2, proj2, b_glu, g_ffn, wr_hi, wr_lo, b_r, wpa, wglu, wpb, wout]
    io_alias = {}
    if aliases is not None:
        for j, a in enumerate(aliases):
            in_specs.append(any_spec)
            args.append(a)
            io_alias[14 + j] = j
    out_shape = (jax.ShapeDtypeStruct((t_all, D_MODEL), F32),
                 jax.ShapeDtypeStruct((t_all, D_MODEL), F32),
                 jax.ShapeDtypeStruct((t_all, LANES), jnp.int32),
                 jax.ShapeDtypeStruct((t_all, LANES), F32))
    out_specs = (pl.BlockSpec((tm, D_MODEL), lambda i: (i + blk0, 0)),
                 pl.BlockSpec((tm, D_MODEL), lambda i: (i + blk0, 0)),
                 pl.BlockSpec((tm, LANES), lambda i: (i + blk0, 0)),
                 pl.BlockSpec((tm, LANES), lambda i: (i + blk0, 0)))
    return pl.pallas_call(
        _merge_kernel,
        out_shape=out_shape,
        grid=(n // tm,),
        in_specs=in_specs,
        out_specs=out_specs,
        scratch_shapes=[pltpu.VMEM(wpa.shape, BF16), pltpu.VMEM(wglu.shape, BF16),
                        pltpu.VMEM(wpb.shape, BF16), pltpu.VMEM(wout.shape, BF16),
                        pltpu.SemaphoreType.DMA((4,))],
        input_output_aliases=io_alias,
        compiler_params=_cparams(("arbitrary",)),
        name="merge_router",
    )(*args)


def _moe_kernel(nt_ref, first_ref, ord_ref, elist_ref, nord_ref, xs_ref, wg_hbm, wu_hbm, wd_hbm, o_ref,
                stg_g, stg_u, stg_d, wg_sc, wu_sc, wd_sc, sem):
    i = pl.program_id(0)
    n_ord = nord_ref[0]

    def weight_copies(k, slot):
        e = elist_ref[k]
        return (pltpu.make_async_copy(wg_hbm.at[e], stg_g.at[slot], sem.at[slot, 0]),
                pltpu.make_async_copy(wu_hbm.at[e], stg_u.at[slot], sem.at[slot, 1]),
                pltpu.make_async_copy(wd_hbm.at[e], stg_d.at[slot], sem.at[slot, 2]))

    @pl.when(i == 0)
    def _():
        for cp in weight_copies(0, 0):
            cp.start()

        @pl.when(n_ord > 1)
        def _():
            for cp in weight_copies(1, 1):
                cp.start()

    valid = i < nt_ref[0]
    k = ord_ref[i]

    @pl.when(valid & (first_ref[i] == 1))
    def _():
        slot = k % 2
        for cp in weight_copies(k, slot):
            cp.wait()
        wg_sc[...] = stg_g[slot].astype(BF16)
        wu_sc[...] = stg_u[slot].astype(BF16)
        wd_sc[...] = stg_d[slot].astype(BF16)

        @pl.when(k + 2 < n_ord)
        def _():
            for cp in weight_copies(k + 2, slot):
                cp.start()

    @pl.when(valid)
    def _():
        x = xs_ref[...].astype(BF16)
        hg = _dot(x, wg_sc[...])
        hu = _dot(x, wu_sc[...])
        hh = (_silu(hg) * hu).astype(BF16)
        o_ref[...] = _dot(hh, wd_sc[...])


def _moe_experts(n_tiles, first, ordinal, elist, n_ord, xs, w_gate, w_up, w_down):
    p_rows = xs.shape[0]
    last = lambda i, nt, *_: jnp.minimum(i, nt[0] - 1)
    any_spec = pl.BlockSpec(memory_space=pl.ANY)
    grid_spec = pltpu.PrefetchScalarGridSpec(
        num_scalar_prefetch=5,
        grid=(p_rows // MOE_TILE,),
        in_specs=[pl.BlockSpec((MOE_TILE, D_MODEL), lambda i, *s: (last(i, *s), 0)),
                  any_spec, any_spec, any_spec],
        out_specs=pl.BlockSpec((MOE_TILE, D_MODEL), lambda i, *s: (last(i, *s), 0)),
        scratch_shapes=[pltpu.VMEM((2, D_MODEL, D_EXPERT), F32), pltpu.VMEM((2, D_MODEL, D_EXPERT), F32),
                        pltpu.VMEM((2, D_EXPERT, D_MODEL), F32),
                        pltpu.VMEM((D_MODEL, D_EXPERT), BF16), pltpu.VMEM((D_MODEL, D_EXPERT), BF16),
                        pltpu.VMEM((D_EXPERT, D_MODEL), BF16),
                        pltpu.SemaphoreType.DMA((2, 3))],
    )
    return pl.pallas_call(
        _moe_kernel,
        out_shape=jax.ShapeDtypeStruct((p_rows, D_MODEL), F32),
        grid_spec=grid_spec,
        compiler_params=_cparams(("arbitrary",)),
        name="moe_experts",
    )(n_tiles, first, ordinal, elist, n_ord, xs, w_gate, w_up, w_down)


def _final_kernel(x1_ref, y0_ref, y1_ref, w_ref, g_ref, o_ref):
    w = w_ref[...]
    x2 = x1_ref[...] + w[:, 0:1] * y0_ref[...] + w[:, 1:2] * y1_ref[...]
    o_ref[...] = x2 * lax.rsqrt(jnp.mean(x2 * x2, axis=1, keepdims=True) + EPS) * g_ref[...]


def _final(x1_all, yg0, yg1, wts, g_final, row0, n, tm):
    blk0 = row0 // tm
    rows = pl.BlockSpec((tm, D_MODEL), lambda i: (i + blk0, 0))
    return pl.pallas_call(
        _final_kernel,
        out_shape=jax.ShapeDtypeStruct((n, D_MODEL), F32),
        grid=(n // tm,),
        in_specs=[rows, rows, rows,
                  pl.BlockSpec((tm, LANES), lambda i: (i + blk0, 0)),
                  pl.BlockSpec((1, D_MODEL), lambda i: (0, 0))],
        out_specs=pl.BlockSpec((tm, D_MODEL), lambda i: (i, 0)),
        compiler_params=_cparams(("arbitrary",)),
        name="combine_final_norm",
    )(x1_all, yg0, yg1, wts, g_final)


def _s5_discretise(a_re, a_im, log_step, b_re, b_im):
    dt = jnp.exp(log_step)[:, None]
    mag = jnp.exp(a_re * dt)
    lb_re = mag * jnp.cos(a_im * dt)
    lb_im = mag * jnp.sin(a_im * dt)
    den = a_re * a_re + a_im * a_im
    nr = lb_re - 1.0
    coef_re = (nr * a_re + lb_im * a_im) / den
    coef_im = (lb_im * a_re - nr * a_im) / den
    bb_re = coef_re[..., None] * b_re - coef_im[..., None] * b_im
    bb_im = coef_re[..., None] * b_im + coef_im[..., None] * b_re
    return lb_re, lb_im, bb_re, bb_im


def _s5_chunk_params(a_re, a_im, log_step, b_re, b_im, c_re, c_im):
    lb_re, lb_im, bb_re, bb_im = _s5_discretise(a_re, a_im, log_step, b_re, b_im)
    L = S5_CHUNK
    gpb = LANES // S5_GROUP
    ncb = G_B // gpb
    pr, pi = [jnp.ones_like(lb_re)], [jnp.zeros_like(lb_re)]
    for _ in range(L):
        pr, pi = pr + [pr[-1] * lb_re - pi[-1] * lb_im], pi + [pr[-1] * lb_im + pi[-1] * lb_re]
    pw_r, pw_i = jnp.stack(pr), jnp.stack(pi)
    bt_re = bb_re.transpose(0, 2, 1)
    bt_im = bb_im.transpose(0, 2, 1)
    lbb_r = pw_r[:L, :, None, :] * bt_re - pw_i[:L, :, None, :] * bt_im
    lbb_i = pw_r[:L, :, None, :] * bt_im + pw_i[:L, :, None, :] * bt_re
    hp = lax.Precision.HIGHEST
    kdt = (jnp.einsum('gop,dgcp->dgco', c_re, lbb_r, precision=hp)
           - jnp.einsum('gop,dgcp->dgco', c_im, lbb_i, precision=hp))
    a_r = c_re * pw_r[1:, :, None, :] - c_im * pw_i[1:, :, None, :]
    a_i = -(c_re * pw_i[1:, :, None, :] + c_im * pw_r[1:, :, None, :])

    def blocks(m):
        return m.reshape(m.shape[:-3] + (ncb, gpb * m.shape[-2], m.shape[-1]))

    vec = lambda v: v.reshape(ncb, 1, S5_CST)
    prompt = (blocks(lbb_r), blocks(lbb_i), blocks(kdt), blocks(a_r), blocks(a_i), vec(pw_r[L]), vec(pw_i[L]))
    sample = (blocks(lbb_r[0]), blocks(lbb_i[0]), blocks(c_re), blocks(c_im), vec(lb_re), vec(lb_im))
    return prompt, sample


def _dispatch(ids, t_all, p_rows):
    e = ids[:, :2].reshape(-1)
    onehot = (e[:, None] == jnp.arange(N_EXPERTS, dtype=jnp.int32)[None, :]).astype(jnp.int32)
    csum = jnp.cumsum(onehot, axis=0)
    rank = jnp.sum((csum - onehot) * onehot, axis=1)
    counts = csum[-1]
    tiles = (counts + MOE_TILE - 1) // MOE_TILE
    tile_end = jnp.cumsum(tiles)
    tile_start = tile_end - tiles
    pos = jnp.sum(onehot * (tile_start * MOE_TILE)[None, :], axis=1) + rank
    tok = jnp.arange(2 * t_all, dtype=jnp.int32) // 2
    src = (jnp.arange(p_rows, dtype=jnp.int32) % t_all).at[pos].set(tok)
    n_tiles = tile_end[-1]
    tidx = jnp.arange(p_rows // MOE_TILE, dtype=jnp.int32)
    tclamp = jnp.minimum(tidx, n_tiles - 1)
    tile_expert = jnp.sum((tile_end[None, :] <= tclamp[:, None]).astype(jnp.int32), axis=1)
    present = (tiles > 0).astype(jnp.int32)
    ord_of_e = jnp.cumsum(present) - 1
    eids = jnp.arange(N_EXPERTS, dtype=jnp.int32)
    elist = jnp.sum(jnp.where((ord_of_e[None, :] == eids[:, None]) & (present[None, :] == 1), eids[None, :], 0), axis=1)
    ordinal = jnp.sum(jnp.where(tile_expert[:, None] == eids[None, :], ord_of_e[None, :], 0), axis=1)
    first = jnp.concatenate([jnp.ones((1,), jnp.int32),
                             (tile_expert[1:] != tile_expert[:-1]).astype(jnp.int32)])
    one = lambda v: v.reshape(1).astype(jnp.int32)
    meta = (one(n_tiles), first, ordinal.astype(jnp.int32), elist.astype(jnp.int32), one(jnp.sum(present)))
    return pos.reshape(t_all, 2), src, meta


def _pick_tile(n, candidates):
    for c in candidates:
        if n % c == 0:
            return c
    raise ValueError(f"no row tile for {n}")


def kernel(x_prompt, x_sample, state_mlstm_C, state_mlstm_n, state_mlstm_m, state_conv, state_s5_re,
           state_s5_im, norm_mix_g, w_in, b_i, b_f, w_conv, b_conv, head_norm_g, w_pa, s5_a_re, s5_a_im,
           s5_log_step, s5_b_re, s5_b_im, s5_c_re, s5_c_im, s5_d, s5_w_glu, s5_b_glu, w_pb, w_out,
           norm_ffn_g, w_rg, b_rg, w_rexp, b_rexp, w_gate, w_up, w_down, norm_final_g):
    assert state_mlstm_C.shape[0] == 1 and x_sample.shape[1] == 1
    batch, seq, _ = x_prompt.shape
    dec = x_sample.shape[0]
    t_p = batch * seq
    t_all = t_p + dec
    assert seq % CHUNK == 0 and t_p % dec == 0 and dec % LANES == 0
    d_a = H_A * DK
    tm_p = _pick_tile(t_p, (512, 256, 128))
    tm_all = _pick_tile(t_all, (640, 384, 128))
    sample_blk = t_p // dec

    xp = x_prompt.reshape(t_p, D_MODEL)
    xs = x_sample.reshape(dec, D_MODEL)

    g_mix = norm_mix_g[0]
    xn_all = _rmsnorm_rows(xp, g_mix, t_all, 0, tm_p)
    xn_all = _rmsnorm_rows(xs, g_mix, t_all, t_p, dec, alias=xn_all)
    w_in_t = w_in.reshape(w_in.shape[1:]).T
    n_qkvo = 4 * d_a
    tn = 1024
    qkvo = _matmul_t(xn_all, w_in_t, [j * tn for j in range(n_qkvo // tn)], jnp.zeros((1, n_qkvo), F32),
                     tm_all, tn, F32)
    n_gate_cols = 2 * H_A
    b_gates = jnp.pad(jnp.concatenate([b_i[0], b_f[0]]), (0, LANES - n_gate_cols)).reshape(1, LANES)
    gates = _matmul_t(xn_all, w_in_t, [n_qkvo], b_gates, tm_all, LANES, F32)
    c_u = n_qkvo + n_gate_cols
    c_ga = c_u + D_B
    starts = [c_ga + j * tn for j in range(2 * D_MODEL // tn)] + [c_u]
    proj2 = _matmul_t(xn_all, w_in_t, starts, jnp.zeros((1, len(starts) * tn), F32), tm_all, tn, F32)
    u_col0 = 2 * D_MODEL

    conv_s_in = state_conv[0].reshape(dec, (CONV_W - 1) * 2 * d_a)
    q_s, kw_s, a_e, s_e, den_e, conv_s, n_s, m_s = _sample_pre(
        qkvo, conv_s_in, w_conv[0], b_conv[0].reshape(1, -1), gates, state_mlstm_m[0],
        state_mlstm_n[0].reshape(dec, d_a), dec, sample_blk)
    v_s = qkvo[t_p:, 2 * d_a:3 * d_a]
    r3 = lambda a: a.reshape(dec, H_A, DK)

    gates_t = gates[:t_p, :n_gate_cols].T
    head_g = head_norm_g[0].reshape(1, d_a)
    hn_all, c_p, n_p, m_p, c_s, qc = _mlstm_prompt(
        qkvo, gates, gates_t, w_conv[0], b_conv[0].reshape(1, -1), head_g,
        (r3(q_s), r3(kw_s), r3(v_s), r3(a_e), state_mlstm_C[0]), batch, seq, t_all)
    conv_p = jnp.stack([qkvo[b * seq + seq - (CONV_W - 1):(b + 1) * seq, :2 * d_a] for b in range(batch)])
    hn_all = _sample_post(qc.reshape(dec, d_a), s_e, a_e, den_e, qkvo, head_g, hn_all, dec, sample_blk)

    d_row = s5_d[0].reshape(1, D_B)
    s5_prompt_w, s5_sample_w = _s5_chunk_params(s5_a_re[0], s5_a_im[0], s5_log_step[0], s5_b_re[0],
                                                s5_b_im[0], s5_c_re[0], s5_c_im[0])
    ys_all, s5re_p, s5im_p = _s5_prompt(proj2, u_col0, *s5_prompt_w, d_row, batch, seq, t_all)

    ys_all, s5re_s, s5im_s = _s5_sample(proj2, u_col0, state_s5_re[0].reshape(dec, -1),
                                        state_s5_im[0].reshape(dec, -1),
                                        *s5_sample_w, d_row, ys_all, dec, sample_blk)

    wr = jnp.pad(jnp.concatenate([w_rg[0], w_rexp[0]], axis=1), ((0, 0), (0, LANES - N_GROUPS - N_EXPERTS)))
    wr_hi = wr.astype(BF16)
    wr_lo = (wr - wr_hi.astype(F32)).astype(BF16)
    b_r = jnp.pad(jnp.concatenate([b_rg[0], b_rexp[0]]), (0, LANES - N_GROUPS - N_EXPERTS)).reshape(1, LANES)
    merge_w = (w_pa[0].astype(BF16), s5_w_glu[0].astype(BF16), w_pb[0].astype(BF16), w_out[0].astype(BF16))
    b_glu = s5_b_glu[0].reshape(1, D_B)
    g_ffn = norm_ffn_g[0].reshape(1, D_MODEL)
    tm_m = _pick_tile(t_p, (256, 128))
    outs = _merge(xp, hn_all, ys_all, proj2, b_glu, g_ffn, wr_hi, wr_lo, b_r, *merge_w, t_all, 0, tm_m)
    x1_all, xn2_all, ids, wts = _merge(xs, hn_all, ys_all, proj2, b_glu, g_ffn, wr_hi, wr_lo, b_r,
                                       *merge_w, t_all, t_p, dec, aliases=outs)

    p_rows = -(-(2 * t_all + N_EXPERTS * (MOE_TILE - 1)) // MOE_TILE) * MOE_TILE
    pos, src, meta = _dispatch(ids, t_all, p_rows)
    take_rows = lambda a, idx: a.at[idx].get(mode='promise_in_bounds')
    xs_sorted = take_rows(xn2_all, src)
    yp = _moe_experts(*meta, xs_sorted, w_gate[0], w_up[0], w_down[0])
    yg0 = take_rows(yp, pos[:, 0])
    yg1 = take_rows(yp, pos[:, 1])

    g_fin = norm_final_g.reshape(1, D_MODEL)
    y_prompt = _final(x1_all, yg0, yg1, wts, g_fin, 0, t_p, tm_p).reshape(batch, seq, D_MODEL)
    y_sample = _final(x1_all, yg0, yg1, wts, g_fin, t_p, dec, dec).reshape(dec, 1, D_MODEL)

    lead = lambda a, shape: a.reshape((1,) + shape)
    return (y_prompt, y_sample,
            lead(c_p, (batch, H_A, DK, DK)), lead(n_p, (batch, H_A, DK)), lead(m_p[:, 0, :H_A], (batch, H_A)),
            lead(conv_p, (batch, CONV_W - 1, 2 * d_a)),
            lead(s5re_p, (batch, G_B, P_S5)), lead(s5im_p, (batch, G_B, P_S5)),
            lead(c_s, (dec, H_A, DK, DK)), lead(n_s, (dec, H_A, DK)), lead(m_s[:, :H_A], (dec, H_A)),
            lead(conv_s, (dec, CONV_W - 1, 2 * d_a)),
            lead(s5re_s, (dec, G_B, P_S5)), lead(s5im_s, (dec, G_B, P_S5)))
```

```python
import functools
import math

import jax
import jax.numpy as jnp
from jax import lax
from jax.experimental import pallas as pl
from jax.experimental.pallas import tpu as pltpu

F32 = jnp.float32
BF16 = jnp.bfloat16

D_MODEL = 2048
H_A = 8
DK = 256
CONV_W = 4
CHUNK = 128
D_B = 1024
S5_GROUP = 16
G_B = 64
P_S5 = 64
N_GROUPS = 4
EXP_PER_GROUP = 8
N_EXPERTS = 32
D_EXPERT = 512
EPS = 1e-6

LANES = 128
SUBLANES = 8
VMEM_LIMIT = 56 * 1024 * 1024

S5_CHUNK = 8
S5_CST = (LANES // S5_GROUP) * P_S5
MOE_TILE = 256


def _cparams(sem):
    return pltpu.CompilerParams(dimension_semantics=sem, vmem_limit_bytes=VMEM_LIMIT)


def _silu(x):
    return x * (1.0 / (1.0 + jnp.exp(-x)))


def _sigmoid(x):
    return 1.0 / (1.0 + jnp.exp(-x))


def _log_sigmoid(x):
    return jnp.minimum(x, 0.0) - jnp.log1p(jnp.exp(-jnp.abs(x)))


def _gelu_tanh(x):
    c = math.sqrt(2.0 / math.pi)
    return 0.5 * x * (1.0 + jnp.tanh(c * (x + 0.044715 * (x * x * x))))


def _split3(x):
    hi = x.astype(BF16)
    r = x - hi.astype(F32)
    mid = r.astype(BF16)
    lo = (r - mid.astype(F32)).astype(BF16)
    return hi, mid, lo


def _dot(a, b):
    return jnp.dot(a, b, preferred_element_type=F32)


def _dot_nt(a, b):
    return lax.dot_general(a, b, (((1,), (1,)), ((), ())), preferred_element_type=F32)


def _dot_tn(a, b):
    return lax.dot_general(a, b, (((0,), (0,)), ((), ())), preferred_element_type=F32)


def _rmsnorm_kernel(x_ref, g_ref, wg_ref, bg_ref, bgt_ref, *rest):
    n_out = 3
    o_ref, gates_ref, gates_t_ref = rest[-n_out:]
    x = x_ref[...]
    r = lax.rsqrt(jnp.mean(x * x, axis=-1, keepdims=True) + EPS)
    xn = (x * r * g_ref[...]).astype(o_ref.dtype)
    o_ref[...] = xn
    wg = wg_ref[...].astype(BF16)
    gates_ref[...] = _dot_nt(xn, wg) + bg_ref[...]
    gt = _dot_nt(wg, xn) + bgt_ref[...]
    gates_t_ref[...] = gt[0:gates_t_ref.shape[0], :]


def _rmsnorm_rows(x, g, w_t, gate_row0, b_gates, t_all, row0, tm, alias=None):
    n = x.shape[0]
    blk0 = row0 // tm
    in_specs = [pl.BlockSpec((tm, D_MODEL), lambda i: (i, 0)),
                pl.BlockSpec((1, D_MODEL), lambda i: (0, 0)),
                pl.BlockSpec((pl.Element(LANES), pl.Element(D_MODEL)), lambda i: (gate_row0, 0)),
                pl.BlockSpec((1, LANES), lambda i: (0, 0)),
                pl.BlockSpec((LANES, 1), lambda i: (0, 0))]
    args = [x, g.reshape(1, D_MODEL), w_t, b_gates.reshape(1, LANES), b_gates.reshape(LANES, 1)]
    aliases = {}
    if alias is not None:
        for a in alias:
            aliases[len(args)] = len(aliases)
            in_specs.append(pl.BlockSpec(memory_space=pl.ANY))
            args.append(a)
    return pl.pallas_call(
        _rmsnorm_kernel,
        out_shape=(jax.ShapeDtypeStruct((t_all, D_MODEL), BF16),
                   jax.ShapeDtypeStruct((t_all, LANES), F32),
                   jax.ShapeDtypeStruct((2 * H_A, n), F32)),
        grid=(n // tm,),
        in_specs=in_specs,
        out_specs=(pl.BlockSpec((tm, D_MODEL), lambda i: (i + blk0, 0)),
                   pl.BlockSpec((tm, LANES), lambda i: (i + blk0, 0)),
                   pl.BlockSpec((2 * H_A, tm), lambda i: (0, i))),
        input_output_aliases=aliases,
        compiler_params=_cparams(("arbitrary",)),
        name="rmsnorm_rows",
    )(*args)


def _mm_kernel(a_ref, w_ref, b_ref, o_ref, wb_ref):
    @pl.when(pl.program_id(1) == 0)
    def _():
        wb_ref[...] = w_ref[...].astype(BF16)

    o_ref[...] = (_dot(a_ref[...], wb_ref[...]) + b_ref[...]).astype(o_ref.dtype)


def _matmul(a, w, bias, n_out, tm, tn, out_dtype, col_blk0=0):
    m, k = a.shape
    return pl.pallas_call(
        _mm_kernel,
        out_shape=jax.ShapeDtypeStruct((m, n_out), out_dtype),
        grid=(n_out // tn, m // tm),
        in_specs=[pl.BlockSpec((tm, k), lambda j, i: (i, 0)),
                  pl.BlockSpec((k, tn), lambda j, i: (0, j + col_blk0)),
                  pl.BlockSpec((1, tn), lambda j, i: (0, j))],
        out_specs=pl.BlockSpec((tm, tn), lambda j, i: (i, j)),
        scratch_shapes=[pltpu.VMEM((k, tn), BF16)],
        compiler_params=_cparams(("arbitrary", "arbitrary")),
        name="rows_matmul",
    )(a, w, bias)


def _mm_t_kernel(starts_ref, a_ref, wt_ref, b_ref, o_ref, wb_ref):
    del starts_ref

    @pl.when(pl.program_id(1) == 0)
    def _():
        wb_ref[...] = wt_ref[...].astype(BF16)

    o_ref[...] = (_dot_nt(a_ref[...], wb_ref[...]) + b_ref[...]).astype(o_ref.dtype)


def _matmul_t(a, w_t, row_starts, bias, tm, tn, out_dtype):
    m, k = a.shape
    n_t = len(row_starts)
    assert all(s % SUBLANES == 0 for s in row_starts)
    grid_spec = pltpu.PrefetchScalarGridSpec(
        num_scalar_prefetch=1,
        grid=(n_t, m // tm),
        in_specs=[pl.BlockSpec((tm, k), lambda j, i, st: (i, 0)),
                  pl.BlockSpec((pl.Element(tn), pl.Element(k)), lambda j, i, st: (st[j] * SUBLANES, 0)),
                  pl.BlockSpec((1, tn), lambda j, i, st: (0, j))],
        out_specs=pl.BlockSpec((tm, tn), lambda j, i, st: (i, j)),
        scratch_shapes=[pltpu.VMEM((tn, k), BF16)],
    )
    return pl.pallas_call(
        _mm_t_kernel,
        out_shape=jax.ShapeDtypeStruct((m, n_t * tn), out_dtype),
        grid_spec=grid_spec,
        compiler_params=_cparams(("arbitrary", "arbitrary")),
        name="rows_matmul_t",
    )(jnp.asarray([s // SUBLANES for s in row_starts], jnp.int32), a, w_t, bias)


def _mm_shift_kernel(a_ref, wlo_ref, whi_ref, o_ref, wb_ref, *, shift):
    tn = wlo_ref.shape[1]
    rows = 256

    @pl.when(pl.program_id(1) == 0)
    def _():
        for r in range(wlo_ref.shape[0] // rows):
            sl = slice(r * rows, (r + 1) * rows)
            w = jnp.concatenate([wlo_ref[sl, :], whi_ref[sl, :]], axis=1)
            wb_ref[sl, :] = w[:, shift:shift + tn].astype(BF16)

    o_ref[...] = _dot(a_ref[...], wb_ref[...]).astype(o_ref.dtype)


def _matmul_shifted(a, w, col_starts, tm, tn, out_dtype):
    m, k = a.shape
    shift = col_starts[0] % tn
    assert all(c % tn == shift for c in col_starts) and 0 < shift < LANES
    lo_blk = [c // tn for c in col_starts]
    n_t = len(col_starts)
    base, first = min(lo_blk), lo_blk[0]
    assert all(lo_blk[j] == base + (j + first - base) % n_t for j in range(n_t))
    lo_idx = lambda j: base + (j + (first - base)) % n_t
    per = tn // LANES
    kern = functools.partial(_mm_shift_kernel, shift=shift)
    return pl.pallas_call(
        kern,
        out_shape=jax.ShapeDtypeStruct((m, n_t * tn), out_dtype),
        grid=(n_t, m // tm),
        in_specs=[pl.BlockSpec((tm, k), lambda j, i: (i, 0)),
                  pl.BlockSpec((k, tn), lambda j, i: (0, lo_idx(j))),
                  pl.BlockSpec((k, LANES), lambda j, i: (0, (lo_idx(j) + 1) * per))],
        out_specs=pl.BlockSpec((tm, tn), lambda j, i: (i, j)),
        scratch_shapes=[pltpu.VMEM((k, tn), BF16)],
        compiler_params=_cparams(("arbitrary", "arbitrary")),
        name="rows_matmul_shifted",
    )(a, w, w)


def _sample_c_update(q_ref, kw_ref, v_ref, a_ref, c_ref, c_out, qc_out):
    rows = 2 * SUBLANES
    rid = lax.broadcasted_iota(jnp.int32, (rows, DK), 0)
    pad = jnp.zeros((rows - H_A, DK), F32)
    for b in range(q_ref.shape[0]):
        q16 = jnp.concatenate([q_ref[b], pad], axis=0).astype(BF16)
        kw16 = jnp.concatenate([kw_ref[b], pad], axis=0)
        v16 = jnp.concatenate([v_ref[b], pad], axis=0).astype(BF16)
        qc_rows = []
        for h in range(H_A):
            c_prev = c_ref[b, h]
            qc_rows.append(_dot(q16, c_prev.astype(BF16))[h:h + 1, :])
            kw_h = jnp.where(rid == h, kw16, 0.0).astype(BF16)
            c_out[b, h] = a_ref[b, h:h + 1, :] * c_prev + _dot_tn(kw_h, v16)
        qc_out[b] = jnp.concatenate(qc_rows, axis=0)


def _mlstm_kernel(qk_ref, v_ref, o_ref, gcol_ref, grow_ref, wc_ref, bc_ref, hg_ref,
                  sq_ref, skw_ref, sv_ref, sa_ref, sc_ref,
                  h_ref, c_out, n_out, m_out, sc_out, sqc_out, c_sc, n_sc, m_sc, ext_sc):
    c = pl.program_id(1)
    _sample_c_update(sq_ref, skw_ref, sv_ref, sa_ref, sc_ref, sc_out, sqc_out)
    L = CHUNK
    pad = SUBLANES
    d_a = H_A * DK

    @pl.when(c == 0)
    def _():
        c_sc[...] = jnp.zeros_like(c_sc)
        n_sc[...] = jnp.zeros_like(n_sc)
        m_sc[...] = jnp.zeros_like(m_sc)
        ext_sc[0:pad, :] = jnp.zeros((pad, 2 * d_a), F32)

    x = qk_ref[...]
    ext_sc[pad:pad + L, :] = x
    y = bc_ref[...] + wc_ref[CONV_W - 1:CONV_W, :] * x
    for j in range(1, CONV_W):
        y = y + wc_ref[CONV_W - 1 - j:CONV_W - j, :] * ext_sc[pad - j:pad - j + L, :]
    ext_sc[0:pad, :] = x[L - pad:L, :]
    y = _silu(y)

    gcol = gcol_ref[...]
    grow = grow_ref[...]
    ri = lax.broadcasted_iota(jnp.int32, (L, L), 0)
    ci = lax.broadcasted_iota(jnp.int32, (L, L), 1)
    causal = ci <= ri
    tril = jnp.where(causal, 1.0, 0.0).astype(BF16)
    triu = jnp.where(ri <= ci, 1.0, 0.0).astype(BF16)
    b_cols = sum(_dot(tril, p) for p in _split3(_log_sigmoid(gcol)))
    b_rows = sum(_dot(p, triu) for p in _split3(_log_sigmoid(grow)))

    for h in range(H_A):
        sl = slice(h * DK, (h + 1) * DK)
        q = y[:, sl] * (DK ** -0.5)
        k = y[:, d_a + h * DK:d_a + (h + 1) * DK]
        ig_col = gcol[:, h:h + 1]
        ig_row = grow[h:h + 1, :]
        b_col = b_cols[:, H_A + h:H_A + h + 1]
        b_row = b_rows[H_A + h:H_A + h + 1, :]

        m_prev = m_sc[:, h:h + 1]
        d_log = jnp.where(causal, b_col - b_row + ig_row, -jnp.inf)
        inter_log = b_col + m_prev
        m_t = jnp.maximum(inter_log, jnp.max(d_log, axis=1, keepdims=True))
        qb = q.astype(BF16)
        kb = k.astype(BF16)
        vb = v_ref[:, sl].astype(BF16)
        s = _dot_nt(qb, kb) * jnp.exp(d_log - m_t)
        inter_w = jnp.exp(inter_log - m_t)
        c_prev = c_sc[h]
        n_prev = n_sc[h:h + 1, :]
        num = _dot(s.astype(BF16), vb) + inter_w * _dot(qb, c_prev.astype(BF16))
        nq = jnp.sum(s, axis=1, keepdims=True) + inter_w * jnp.sum(q * n_prev, axis=1, keepdims=True)
        den = jnp.maximum(jnp.abs(nq), jnp.exp(-m_t))
        hh = num / den
        hh = hh * _sigmoid(o_ref[:, sl])
        hh = hh * lax.rsqrt(jnp.mean(hh * hh, axis=1, keepdims=True) + EPS)
        h_ref[:, sl] = (hh * hg_ref[:, sl]).astype(h_ref.dtype)

        m_new = m_t[L - 1:L, :]
        b_last = b_col[L - 1:L, :]
        decay = jnp.exp(b_last + m_prev - m_new)
        w_end = jnp.exp(b_last - b_col + ig_col - m_new)
        kw = k * w_end
        c_sc[h] = decay * c_prev + _dot_tn(kw.astype(BF16), vb)
        n_sc[h:h + 1, :] = decay * n_prev + jnp.sum(kw, axis=0, keepdims=True)
        m_sc[:, h:h + 1] = m_new

    @pl.when(c == pl.num_programs(1) - 1)
    def _():
        c_out[...] = c_sc[...]
        n_out[...] = n_sc[...]
        m_out[...] = m_sc[...]


def _mlstm_prompt(qkvo, gates, gates_t, w_conv, b_conv, head_g, sample, batch, seq, t_all):
    nc = seq // CHUNK
    L = CHUNK
    d_a = H_A * DK
    dec = sample[0].shape[0]
    nb = dec // (batch * nc)
    assert nb * batch * nc == dec
    row = lambda b, c: b * nc + c
    svec = pl.BlockSpec((nb, H_A, DK), lambda b, c: (row(b, c), 0, 0))
    smat = pl.BlockSpec((nb, H_A, DK, DK), lambda b, c: (row(b, c), 0, 0, 0))
    in_specs = [
        pl.BlockSpec((L, 2 * d_a), lambda b, c: (row(b, c), 0)),
        pl.BlockSpec((L, d_a), lambda b, c: (row(b, c), 2)),
        pl.BlockSpec((L, d_a), lambda b, c: (row(b, c), 3)),
        pl.BlockSpec((L, LANES), lambda b, c: (row(b, c), 0)),
        pl.BlockSpec((2 * H_A, L), lambda b, c: (0, row(b, c))),
        pl.BlockSpec((CONV_W, 2 * d_a), lambda b, c: (0, 0)),
        pl.BlockSpec((1, 2 * d_a), lambda b, c: (0, 0)),
        pl.BlockSpec((1, d_a), lambda b, c: (0, 0)),
        svec, svec, svec, svec, smat,
    ]
    out_shape = (
        jax.ShapeDtypeStruct((t_all, d_a), BF16),
        jax.ShapeDtypeStruct((batch, H_A, DK, DK), F32),
        jax.ShapeDtypeStruct((batch, H_A, DK), F32),
        jax.ShapeDtypeStruct((batch, 1, LANES), F32),
        jax.ShapeDtypeStruct((dec, H_A, DK, DK), F32),
        jax.ShapeDtypeStruct((dec, H_A, DK), F32),
    )
    out_specs = (
        pl.BlockSpec((L, d_a), lambda b, c: (row(b, c), 0)),
        pl.BlockSpec((None, H_A, DK, DK), lambda b, c: (b, 0, 0, 0)),
        pl.BlockSpec((None, H_A, DK), lambda b, c: (b, 0, 0)),
        pl.BlockSpec((None, 1, LANES), lambda b, c: (b, 0, 0)),
        smat, svec,
    )
    return pl.pallas_call(
        _mlstm_kernel,
        out_shape=out_shape,
        grid=(batch, nc),
        in_specs=in_specs,
        out_specs=out_specs,
        scratch_shapes=[pltpu.VMEM((H_A, DK, DK), F32), pltpu.VMEM((H_A, DK), F32),
                        pltpu.VMEM((1, LANES), F32), pltpu.VMEM((SUBLANES + L, 2 * d_a), F32)],
        compiler_params=_cparams(("arbitrary", "arbitrary")),
        name="mlstm_prompt",
    )(qkvo, qkvo, qkvo, gates, gates_t, w_conv, b_conv, head_g, *sample)


def _lane_tile(x, reps):
    w = x.shape[1]
    ri = lax.broadcasted_iota(jnp.int32, (w, w * reps), 0)
    ci = lax.broadcasted_iota(jnp.int32, (w, w * reps), 1)
    rep = jnp.where(jnp.bitwise_and(ci, w - 1) == ri, 1.0, 0.0).astype(BF16)
    return _dot(x.astype(BF16), rep)


def _group_mask(rows, cols, row_per, col_per):
    ri = lax.broadcasted_iota(jnp.int32, (rows, cols), 0)
    ci = lax.broadcasted_iota(jnp.int32, (rows, cols), 1)
    return (jnp.right_shift(ri, int(math.log2(row_per))) == jnp.right_shift(ci, int(math.log2(col_per))))


def _s5_build_weights(bin_r, bin_i, kdt, aout_r, aout_i, winc_ref, wintra_ref, wout_ref):
    L = S5_CHUNK
    gpb = LANES // S5_GROUP
    m_inc = _group_mask(LANES, S5_CST, S5_GROUP, P_S5)
    m_lag = _group_mask(LANES, LANES, S5_GROUP, S5_GROUP)
    zero = jnp.zeros((LANES, LANES), BF16)
    lag = [jnp.where(m_lag, _lane_tile(kdt[d], gpb), 0.0).astype(BF16) for d in range(L)]
    for t in range(L):
        rows = slice(t * LANES, (t + 1) * LANES)
        d = L - 1 - t
        winc_ref[rows, 0:S5_CST] = jnp.where(m_inc, _lane_tile(bin_r[d], gpb), 0.0).astype(BF16)
        winc_ref[rows, S5_CST:2 * S5_CST] = jnp.where(m_inc, _lane_tile(bin_i[d], gpb), 0.0).astype(BF16)
        wout_ref[rows, 0:S5_CST] = jnp.where(m_inc, _lane_tile(aout_r[t], gpb), 0.0).astype(BF16)
        wout_ref[rows, S5_CST:2 * S5_CST] = jnp.where(m_inc, _lane_tile(aout_i[t], gpb), 0.0).astype(BF16)
        for t2 in range(L):
            wintra_ref[rows, t2 * LANES:(t2 + 1) * LANES] = lag[t2 - t] if t2 >= t else zero


def _s5_prompt_kernel(u_ref, binr_ref, bini_ref, kdt_ref, aoutr_ref, aouti_ref, lbr_ref, lbi_ref, d_ref,
                      ys_ref, sre_ref, sim_ref, x_sc, winc_ref, wintra_ref, wout_ref, *, batch, nchunk):
    L = S5_CHUNK
    nrow = batch * nchunk
    nst = S5_CST // LANES
    rstr = nchunk + SUBLANES
    _s5_build_weights(binr_ref, bini_ref, kdt_ref, aoutr_ref, aouti_ref, winc_ref, wintra_ref, wout_ref)
    u_t = [u_ref[pl.ds(t, nrow, stride=L), :] for t in range(L)]
    lhs = jnp.concatenate([a.astype(BF16) for a in u_t], axis=1)
    inc = _dot(lhs, winc_ref[...])
    for j in range(2 * nst):
        for b in range(batch):
            x_sc[j, b * rstr:b * rstr + nchunk, :] = inc[b * nchunk:(b + 1) * nchunk, j * LANES:(j + 1) * LANES]

    lbr = [jnp.broadcast_to(lbr_ref[:, j * LANES:(j + 1) * LANES], (batch, LANES)) for j in range(nst)]
    lbi = [jnp.broadcast_to(lbi_ref[:, j * LANES:(j + 1) * LANES], (batch, LANES)) for j in range(nst)]

    def scan_body(r, carry):
        rows = pl.ds(r, batch, stride=rstr)
        out = []
        for j in range(nst):
            xr, xi = carry[j]
            ir = x_sc[j, rows, :]
            ii = x_sc[nst + j, rows, :]
            x_sc[j, rows, :] = xr
            x_sc[nst + j, rows, :] = xi
            out.append((lbr[j] * xr - lbi[j] * xi + ir, lbr[j] * xi + lbi[j] * xr + ii))
        return tuple(out)

    z = jnp.zeros((batch, LANES), F32)
    fin = lax.fori_loop(0, nchunk, scan_body, tuple((z, z) for _ in range(nst)))
    for j in range(nst):
        sre_ref[:, j * LANES:(j + 1) * LANES] = fin[j][0]
        sim_ref[:, j * LANES:(j + 1) * LANES] = fin[j][1]

    xprev = jnp.concatenate(
        [jnp.concatenate([x_sc[j, b * rstr:b * rstr + nchunk, :] for b in range(batch)], axis=0)
         for j in range(2 * nst)], axis=1).astype(BF16)
    y = _dot(lhs, wintra_ref[...]) + _dot_nt(xprev, wout_ref[...])
    for t in range(L):
        yt = y[:, t * LANES:(t + 1) * LANES] + d_ref[...] * u_t[t]
        ys_ref[pl.ds(t, nrow, stride=L), :] = _gelu_tanh(yt).astype(ys_ref.dtype)


def _s5_prompt(proj2, u_col0, bin_r, bin_i, kdt, aout_r, aout_i, lb8r, lb8i, d_row, batch, seq, t_all):
    nchunk = seq // S5_CHUNK
    t_p = batch * seq
    kern = functools.partial(_s5_prompt_kernel, batch=batch, nchunk=nchunk)
    ub0 = u_col0 // LANES
    ncb = D_B // LANES
    kw = S5_CHUNK * LANES
    per_blk = lambda a: pl.BlockSpec((a.shape[0], None) + a.shape[2:], lambda g: (0, g, 0, 0))
    return pl.pallas_call(
        kern,
        out_shape=(jax.ShapeDtypeStruct((t_all, D_B), F32),
                   jax.ShapeDtypeStruct((batch, G_B * P_S5), F32),
                   jax.ShapeDtypeStruct((batch, G_B * P_S5), F32)),
        grid=(ncb,),
        in_specs=[pl.BlockSpec((t_p, LANES), lambda g: (0, ub0 + g)),
                  per_blk(bin_r), per_blk(bin_i), per_blk(kdt), per_blk(aout_r), per_blk(aout_i),
                  pl.BlockSpec((None, 1, S5_CST), lambda g: (g, 0, 0)),
                  pl.BlockSpec((None, 1, S5_CST), lambda g: (g, 0, 0)),
                  pl.BlockSpec((1, LANES), lambda g: (0, g))],
        out_specs=(pl.BlockSpec((t_p, LANES), lambda g: (0, g)),
                   pl.BlockSpec((batch, S5_CST), lambda g: (0, g)),
                   pl.BlockSpec((batch, S5_CST), lambda g: (0, g))),
        scratch_shapes=[pltpu.VMEM((2 * S5_CST // LANES, batch * (nchunk + SUBLANES), LANES), F32),
                        pltpu.VMEM((kw, 2 * S5_CST), BF16), pltpu.VMEM((kw, kw), BF16),
                        pltpu.VMEM((kw, 2 * S5_CST), BF16)],
        compiler_params=_cparams(("arbitrary",)),
        name="s5_prompt",
    )(proj2, bin_r, bin_i, kdt, aout_r, aout_i, lb8r, lb8i, d_row)


def _sample_pre_kernel(qk_ref, conv_ref, wc_ref, bc_ref, g_ref, m_ref, n_ref,
                       q_out, kw_out, a_out, s_out, den_out, conv_out, n_out, m_out):
    c2 = 2 * H_A * DK
    x_new = qk_ref[...]
    y = bc_ref[...] + wc_ref[CONV_W - 1:CONV_W, :] * x_new
    for j in range(CONV_W - 1):
        y = y + wc_ref[j:j + 1, :] * conv_ref[:, j * c2:(j + 1) * c2]
    y = _silu(y)
    conv_out[:, 0:(CONV_W - 2) * c2] = conv_ref[:, c2:(CONV_W - 1) * c2]
    conv_out[:, (CONV_W - 2) * c2:(CONV_W - 1) * c2] = x_new
    g = g_ref[...]
    bd = x_new.shape[0]
    m_cols = []
    for h in range(H_A):
        sl = slice(h * DK, (h + 1) * DK)
        q = y[:, sl] * (DK ** -0.5)
        k = y[:, H_A * DK + h * DK:H_A * DK + (h + 1) * DK]
        ig = g[:, h:h + 1]
        lf = _log_sigmoid(g[:, H_A + h:H_A + h + 1])
        m_prev = m_ref[:, h:h + 1]
        m_t = jnp.maximum(lf + m_prev, ig)
        a = jnp.exp(lf + m_prev - m_t)
        wgt = jnp.exp(ig - m_t)
        n_prev = n_ref[:, sl]
        s = jnp.sum(q * k, axis=1, keepdims=True) * wgt
        nq = s + a * jnp.sum(q * n_prev, axis=1, keepdims=True)
        den = jnp.maximum(jnp.abs(nq), jnp.exp(-m_t))
        kw = wgt * k
        q_out[:, sl] = q
        kw_out[:, sl] = kw
        a_out[:, sl] = jnp.broadcast_to(a, (bd, DK))
        s_out[:, sl] = jnp.broadcast_to(s, (bd, DK))
        den_out[:, sl] = jnp.broadcast_to(den, (bd, DK))
        n_out[:, sl] = a * n_prev + kw
        m_cols.append(m_t)
    lane = lax.broadcasted_iota(jnp.int32, (bd, LANES), 1)
    m_full = jnp.zeros((bd, LANES), F32)
    for h in range(H_A):
        m_full = jnp.where(lane == h, m_cols[h], m_full)
    m_out[...] = m_full


def _sample_pre(qkvo, conv_state, w_conv, b_conv, gates, m_state, n_state, dec, row_blk):
    c2 = 2 * H_A * DK
    d = H_A * DK
    full = lambda shape: pl.BlockSpec(shape, lambda i: (0,) * len(shape))
    rows = lambda: jax.ShapeDtypeStruct((dec, d), F32)
    return pl.pallas_call(
        _sample_pre_kernel,
        out_shape=(rows(), rows(), rows(), rows(), rows(),
                   jax.ShapeDtypeStruct((dec, (CONV_W - 1) * c2), F32), rows(),
                   jax.ShapeDtypeStruct((dec, LANES), F32)),
        grid=(1,),
        in_specs=[pl.BlockSpec((dec, c2), lambda i: (row_blk, 0)),
                  full((dec, (CONV_W - 1) * c2)), full((CONV_W, c2)), full((1, c2)),
                  pl.BlockSpec((dec, LANES), lambda i: (row_blk, 0)),
                  full((dec, H_A)), full((dec, d))],
        out_specs=(full((dec, d)), full((dec, d)), full((dec, d)), full((dec, d)), full((dec, d)),
                   full((dec, (CONV_W - 1) * c2)), full((dec, d)), full((dec, LANES))),
        compiler_params=_cparams(("arbitrary",)),
        name="sample_pre",
    )(qkvo, conv_state, w_conv, b_conv, gates, m_state, n_state)


def _sample_post_kernel(qc_ref, s_ref, a_ref, den_ref, vo_ref, hg_ref, hn_in, h_ref):
    del hn_in
    d = H_A * DK
    num = s_ref[...] * vo_ref[:, 0:d] + a_ref[...] * qc_ref[...]
    hh = num / den_ref[...]
    hh = hh * _sigmoid(vo_ref[:, d:2 * d])
    for h in range(H_A):
        sl = slice(h * DK, (h + 1) * DK)
        seg = hh[:, sl]
        seg = seg * lax.rsqrt(jnp.mean(seg * seg, axis=1, keepdims=True) + EPS)
        h_ref[:, sl] = (seg * hg_ref[:, sl]).astype(h_ref.dtype)


def _sample_post(qc, s_e, a_e, den_e, qkvo, head_g, hn_all, dec, row_blk):
    d = H_A * DK
    full = lambda shape: pl.BlockSpec(shape, lambda i: (0,) * len(shape))
    return pl.pallas_call(
        _sample_post_kernel,
        out_shape=jax.ShapeDtypeStruct(hn_all.shape, hn_all.dtype),
        grid=(1,),
        in_specs=[full((dec, d)), full((dec, d)), full((dec, d)), full((dec, d)),
                  pl.BlockSpec((dec, 2 * d), lambda i: (row_blk, 1)),
                  full((1, d)), pl.BlockSpec(memory_space=pl.ANY)],
        out_specs=pl.BlockSpec((dec, d), lambda i: (row_blk, 0)),
        input_output_aliases={6: 0},
        compiler_params=_cparams(("arbitrary",)),
        name="sample_post",
    )(qc, s_e, a_e, den_e, qkvo, head_g, hn_all)


def _s5_sample_kernel(u_ref, sr_ref, si_ref, br_ref, bi_ref, cr_ref, ci_ref, lbr_ref, lbi_ref, d_ref, ys_in,
                      ys_ref, sre_out, sim_out):
    del ys_in
    gpb = LANES // S5_GROUP
    mask = _group_mask(LANES, S5_CST, S5_GROUP, P_S5)
    expand = lambda blk: jnp.where(mask, _lane_tile(blk, gpb), 0.0)
    for g in range(D_B // LANES):
        ch = slice(g * LANES, (g + 1) * LANES)
        sl = slice(g * S5_CST, (g + 1) * S5_CST)
        u = u_ref[:, ch]
        bmat = jnp.concatenate([expand(br_ref[g]), expand(bi_ref[g])], axis=1).astype(BF16)
        bu = _dot(u.astype(BF16), bmat)
        lbr = lbr_ref[g]
        lbi = lbi_ref[g]
        sr = sr_ref[:, sl]
        si = si_ref[:, sl]
        xr = lbr * sr - lbi * si + bu[:, 0:S5_CST]
        xi = lbr * si + lbi * sr + bu[:, S5_CST:2 * S5_CST]
        sre_out[:, sl] = xr
        sim_out[:, sl] = xi
        x = jnp.concatenate([xr, xi], axis=1).astype(BF16)
        cmat = jnp.concatenate([expand(cr_ref[g]), -expand(ci_ref[g])], axis=1).astype(BF16)
        y = _dot_nt(x, cmat) + d_ref[:, ch] * u
        ys_ref[:, ch] = _gelu_tanh(y).astype(ys_ref.dtype)


def _s5_sample(proj2, u_col0, s_re, s_im, b_r, b_i, c_r, c_i, lbr, lbi, d_row, ys_all, dec, row_blk):
    full = lambda shape: pl.BlockSpec(shape, lambda i: (0,) * len(shape))
    n_state = G_B * P_S5
    ub0 = u_col0 // D_B
    params = (b_r, b_i, c_r, c_i, lbr, lbi)
    return pl.pallas_call(
        _s5_sample_kernel,
        out_shape=(jax.ShapeDtypeStruct(ys_all.shape, ys_all.dtype),
                   jax.ShapeDtypeStruct((dec, n_state), F32),
                   jax.ShapeDtypeStruct((dec, n_state), F32)),
        grid=(1,),
        in_specs=[pl.BlockSpec((dec, D_B), lambda i: (row_blk, ub0)),
                  full((dec, n_state)), full((dec, n_state))]
                 + [full(p.shape) for p in params]
                 + [full((1, D_B)), pl.BlockSpec(memory_space=pl.ANY)],
        out_specs=(pl.BlockSpec((dec, D_B), lambda i: (row_blk, 0)),
                   full((dec, n_state)), full((dec, n_state))),
        input_output_aliases={10: 0},
        compiler_params=_cparams(("arbitrary",)),
        name="s5_sample",
    )(proj2, s_re, s_im, *params, d_row, ys_all)


def _merge_kernel(x_ref, hn_ref, ys_ref, ga_ref, gb_ref, bglu_ref, gffn_ref, wrh_ref, wrl_ref, br_ref,
                  wpa_hbm, wglu_hbm, wpb_hbm, wout_hbm, *rest):
    n_alias = len(rest) - 9
    x1_ref, xn_ref, ids_ref, wts_ref = rest[n_alias:n_alias + 4]
    wpa, wglu, wpb, wout, sem = rest[n_alias + 4:]

    @pl.when(pl.program_id(0) == 0)
    def _():
        copies = [pltpu.make_async_copy(src, dst, sem.at[i])
                  for i, (src, dst) in enumerate(((wpa_hbm, wpa), (wglu_hbm, wglu),
                                                  (wpb_hbm, wpb), (wout_hbm, wout)))]
        for cp in copies:
            cp.start()
        for cp in copies:
            cp.wait()

    ya = _dot(hn_ref[...], wpa[...])
    ys = ys_ref[...]
    gate = _sigmoid(_dot(ys.astype(BF16), wglu[...]) + bglu_ref[...])
    yb = _dot((ys * gate).astype(BF16), wpb[...])
    z = _sigmoid(ga_ref[...]) * ya + _sigmoid(gb_ref[...]) * yb
    x1 = x_ref[...] + _dot(z.astype(BF16), wout[...])
    x1_ref[...] = x1
    xn = x1 * lax.rsqrt(jnp.mean(x1 * x1, axis=1, keepdims=True) + EPS) * gffn_ref[...]
    xn_ref[...] = xn
    xh = xn.astype(BF16)
    xl = (xn - xh.astype(F32)).astype(BF16)
    logits = _dot(xh, wrh_ref[...]) + _dot(xl, wrh_ref[...]) + _dot(xh, wrl_ref[...]) + br_ref[...]

    lane_i = lax.broadcasted_iota(jnp.int32, logits.shape, 1)
    lane = lane_i.astype(F32)
    neg = -jnp.inf
    big = float(1 << 20)
    gl = jnp.where(lane_i < N_GROUPS, logits, neg)
    gmax = jnp.max(gl, axis=1, keepdims=True)
    gsum = jnp.sum(jnp.exp(gl - gmax), axis=1, keepdims=True)
    gidx = jnp.min(jnp.where(gl == gmax, lane, big), axis=1, keepdims=True)
    pg_sel = 1.0 / gsum
    lo = N_GROUPS + gidx * EXP_PER_GROUP
    in_grp = (lane >= lo) & (lane < lo + EXP_PER_GROUP)
    el = jnp.where(in_grp, logits, neg)
    emax = jnp.max(el, axis=1, keepdims=True)
    ee = jnp.exp(el - emax)
    pe = ee / jnp.sum(ee, axis=1, keepdims=True)
    v0 = jnp.max(pe, axis=1, keepdims=True)
    i0 = jnp.min(jnp.where(in_grp & (pe == v0), lane, big), axis=1, keepdims=True)
    rest_m = in_grp & (lane != i0)
    pe1 = jnp.where(rest_m, pe, neg)
    v1 = jnp.max(pe1, axis=1, keepdims=True)
    i1 = jnp.min(jnp.where(rest_m & (pe1 == v1), lane, big), axis=1, keepdims=True)
    tot = v0 + v1
    w0 = pg_sel * (v0 / tot)
    w1 = pg_sel * (v1 / tot)
    ids = jnp.where(lane_i == 0, i0 - N_GROUPS, jnp.where(lane_i == 1, i1 - N_GROUPS, 0.0))
    ids_ref[...] = ids.astype(jnp.int32)
    wts_ref[...] = jnp.where(lane_i == 0, w0, jnp.where(lane_i == 1, w1, 0.0))


def _merge(x, hn_all, ys_all, proj2, b_glu, g_ffn, wr_hi, wr_lo, b_r, wpa, wglu, wpb, wout,
           t_all, row0, tm, aliases=None):
    n = x.shape[0]
    blk0 = row0 // tm
    const = lambda shape: pl.BlockSpec(shape, lambda i: (0,) * len(shape))
    any_spec = pl.BlockSpec(memory_space=pl.ANY)
    in_specs = [pl.BlockSpec((tm, D_MODEL), lambda i: (i, 0)),
                pl.BlockSpec((tm, D_MODEL), lambda i: (i + blk0, 0)),
                pl.BlockSpec((tm, D_B), lambda i: (i + blk0, 0)),
                pl.BlockSpec((tm, D_MODEL), lambda i: (i + blk0, 0)),
                pl.BlockSpec((tm, D_MODEL), lambda i: (i + blk0, 1)),
                const((1, D_B)), const((1, D_MODEL)),
                const((D_MODEL, LANES)), const((D_MODEL, LANES)), const((1, LANES)),
                any_spec, any_spec, any_spec, any_spec]
    args = [x, hn_all, ys_all, proj2, proj2, b_glu, g_ffn, wr_hi, wr_lo, b_r, wpa, wglu, wpb, wout]
    io_alias = {}
    if aliases is not None:
        for j, a in enumerate(aliases):
            in_specs.append(any_spec)
            args.append(a)
            io_alias[14 + j] = j
    out_shape = (jax.ShapeDtypeStruct((t_all, D_MODEL), F32),
                 jax.ShapeDtypeStruct((t_all, D_MODEL), F32),
                 jax.ShapeDtypeStruct((t_all, LANES), jnp.int32),
                 jax.ShapeDtypeStruct((t_all, LANES), F32))
    out_specs = (pl.BlockSpec((tm, D_MODEL), lambda i: (i + blk0, 0)),
                 pl.BlockSpec((tm, D_MODEL), lambda i: (i + blk0, 0)),
                 pl.BlockSpec((tm, LANES), lambda i: (i + blk0, 0)),
                 pl.BlockSpec((tm, LANES), lambda i: (i + blk0, 0)))
    return pl.pallas_call(
        _merge_kernel,
        out_shape=out_shape,
        grid=(n // tm,),
        in_specs=in_specs,
        out_specs=out_specs,
        scratch_shapes=[pltpu.VMEM(wpa.shape, BF16), pltpu.VMEM(wglu.shape, BF16),
                        pltpu.VMEM(wpb.shape, BF16), pltpu.VMEM(wout.shape, BF16),
                        pltpu.SemaphoreType.DMA((4,))],
        input_output_aliases=io_alias,
        compiler_params=_cparams(("arbitrary",)),
        name="merge_router",
    )(*args)


def _moe_kernel(nt_ref, first_ref, ord_ref, elist_ref, nord_ref, xs_ref, wg_hbm, wu_hbm, wd_hbm, o_ref,
                stg_g, stg_u, stg_d, wg_sc, wu_sc, wd_sc, sem):
    i = pl.program_id(0)
    n_ord = nord_ref[0]

    def weight_copies(k, slot):
        e = elist_ref[k]
        return (pltpu.make_async_copy(wg_hbm.at[e], stg_g.at[slot], sem.at[slot, 0]),
                pltpu.make_async_copy(wu_hbm.at[e], stg_u.at[slot], sem.at[slot, 1]),
                pltpu.make_async_copy(wd_hbm.at[e], stg_d.at[slot], sem.at[slot, 2]))

    @pl.when(i == 0)
    def _():
        for cp in weight_copies(0, 0):
            cp.start()

        @pl.when(n_ord > 1)
        def _():
            for cp in weight_copies(1, 1):
                cp.start()

    valid = i < nt_ref[0]
    k = ord_ref[i]

    @pl.when(valid & (first_ref[i] == 1))
    def _():
        slot = k % 2
        for cp in weight_copies(k, slot):
            cp.wait()
        wg_sc[...] = stg_g[slot].astype(BF16)
        wu_sc[...] = stg_u[slot].astype(BF16)
        wd_sc[...] = stg_d[slot].astype(BF16)

        @pl.when(k + 2 < n_ord)
        def _():
            for cp in weight_copies(k + 2, slot):
                cp.start()

    @pl.when(valid)
    def _():
        x = xs_ref[...].astype(BF16)
        hg = _dot(x, wg_sc[...])
        hu = _dot(x, wu_sc[...])
        hh = (_silu(hg) * hu).astype(BF16)
        o_ref[...] = _dot(hh, wd_sc[...])


def _moe_experts(n_tiles, first, ordinal, elist, n_ord, xs, w_gate, w_up, w_down):
    p_rows = xs.shape[0]
    last = lambda i, nt, *_: jnp.minimum(i, nt[0] - 1)
    any_spec = pl.BlockSpec(memory_space=pl.ANY)
    grid_spec = pltpu.PrefetchScalarGridSpec(
        num_scalar_prefetch=5,
        grid=(p_rows // MOE_TILE,),
        in_specs=[pl.BlockSpec((MOE_TILE, D_MODEL), lambda i, *s: (last(i, *s), 0)),
                  any_spec, any_spec, any_spec],
        out_specs=pl.BlockSpec((MOE_TILE, D_MODEL), lambda i, *s: (last(i, *s), 0)),
        scratch_shapes=[pltpu.VMEM((2, D_MODEL, D_EXPERT), F32), pltpu.VMEM((2, D_MODEL, D_EXPERT), F32),
                        pltpu.VMEM((2, D_EXPERT, D_MODEL), F32),
                        pltpu.VMEM((D_MODEL, D_EXPERT), BF16), pltpu.VMEM((D_MODEL, D_EXPERT), BF16),
                        pltpu.VMEM((D_EXPERT, D_MODEL), BF16),
                        pltpu.SemaphoreType.DMA((2, 3))],
    )
    return pl.pallas_call(
        _moe_kernel,
        out_shape=jax.ShapeDtypeStruct((p_rows, D_MODEL), F32),
        grid_spec=grid_spec,
        compiler_params=_cparams(("arbitrary",)),
        name="moe_experts",
    )(n_tiles, first, ordinal, elist, n_ord, xs, w_gate, w_up, w_down)


def _final_kernel(x1_ref, y0_ref, y1_ref, w_ref, g_ref, o_ref):
    w = w_ref[...]
    x2 = x1_ref[...] + w[:, 0:1] * y0_ref[...] + w[:, 1:2] * y1_ref[...]
    o_ref[...] = x2 * lax.rsqrt(jnp.mean(x2 * x2, axis=1, keepdims=True) + EPS) * g_ref[...]


def _final(x1_all, yg0, yg1, wts, g_final, row0, n, tm):
    blk0 = row0 // tm
    rows = pl.BlockSpec((tm, D_MODEL), lambda i: (i + blk0, 0))
    return pl.pallas_call(
        _final_kernel,
        out_shape=jax.ShapeDtypeStruct((n, D_MODEL), F32),
        grid=(n // tm,),
        in_specs=[rows, rows, rows,
                  pl.BlockSpec((tm, LANES), lambda i: (i + blk0, 0)),
                  pl.BlockSpec((1, D_MODEL), lambda i: (0, 0))],
        out_specs=pl.BlockSpec((tm, D_MODEL), lambda i: (i, 0)),
        compiler_params=_cparams(("arbitrary",)),
        name="combine_final_norm",
    )(x1_all, yg0, yg1, wts, g_final)


def _s5_discretise(a_re, a_im, log_step, b_re, b_im):
    dt = jnp.exp(log_step)[:, None]
    mag = jnp.exp(a_re * dt)
    lb_re = mag * jnp.cos(a_im * dt)
    lb_im = mag * jnp.sin(a_im * dt)
    den = a_re * a_re + a_im * a_im
    nr = lb_re - 1.0
    coef_re = (nr * a_re + lb_im * a_im) / den
    coef_im = (lb_im * a_re - nr * a_im) / den
    bb_re = coef_re[..., None] * b_re - coef_im[..., None] * b_im
    bb_im = coef_re[..., None] * b_im + coef_im[..., None] * b_re
    return lb_re, lb_im, bb_re, bb_im


def _s5_chunk_params(a_re, a_im, log_step, b_re, b_im, c_re, c_im):
    lb_re, lb_im, bb_re, bb_im = _s5_discretise(a_re, a_im, log_step, b_re, b_im)
    L = S5_CHUNK
    gpb = LANES // S5_GROUP
    ncb = G_B // gpb
    pr, pi = [jnp.ones_like(lb_re)], [jnp.zeros_like(lb_re)]
    for _ in range(L):
        pr, pi = pr + [pr[-1] * lb_re - pi[-1] * lb_im], pi + [pr[-1] * lb_im + pi[-1] * lb_re]
    pw_r, pw_i = jnp.stack(pr), jnp.stack(pi)
    bt_re = bb_re.transpose(0, 2, 1)
    bt_im = bb_im.transpose(0, 2, 1)
    lbb_r = pw_r[:L, :, None, :] * bt_re - pw_i[:L, :, None, :] * bt_im
    lbb_i = pw_r[:L, :, None, :] * bt_im + pw_i[:L, :, None, :] * bt_re
    hp = lax.Precision.HIGHEST
    kdt = (jnp.einsum('gop,dgcp->dgco', c_re, lbb_r, precision=hp)
           - jnp.einsum('gop,dgcp->dgco', c_im, lbb_i, precision=hp))
    a_r = c_re * pw_r[1:, :, None, :] - c_im * pw_i[1:, :, None, :]
    a_i = -(c_re * pw_i[1:, :, None, :] + c_im * pw_r[1:, :, None, :])

    def blocks(m):
        return m.reshape(m.shape[:-3] + (ncb, gpb * m.shape[-2], m.shape[-1]))

    vec = lambda v: v.reshape(ncb, 1, S5_CST)
    prompt = (blocks(lbb_r), blocks(lbb_i), blocks(kdt), blocks(a_r), blocks(a_i), vec(pw_r[L]), vec(pw_i[L]))
    sample = (blocks(lbb_r[0]), blocks(lbb_i[0]), blocks(c_re), blocks(c_im), vec(lb_re), vec(lb_im))
    return prompt, sample


def _dispatch(ids, t_all, p_rows):
    e = ids[:, :2].reshape(-1)
    onehot = (e[:, None] == jnp.arange(N_EXPERTS, dtype=jnp.int32)[None, :]).astype(jnp.int32)
    csum = jnp.cumsum(onehot, axis=0)
    rank = jnp.sum((csum - onehot) * onehot, axis=1)
    counts = csum[-1]
    tiles = (counts + MOE_TILE - 1) // MOE_TILE
    tile_end = jnp.cumsum(tiles)
    tile_start = tile_end - tiles
    pos = jnp.sum(onehot * (tile_start * MOE_TILE)[None, :], axis=1) + rank
    tok = jnp.arange(2 * t_all, dtype=jnp.int32) // 2
    src = (jnp.arange(p_rows, dtype=jnp.int32) % t_all).at[pos].set(tok)
    n_tiles = tile_end[-1]
    tidx = jnp.arange(p_rows // MOE_TILE, dtype=jnp.int32)
    tclamp = jnp.minimum(tidx, n_tiles - 1)
    tile_expert = jnp.sum((tile_end[None, :] <= tclamp[:, None]).astype(jnp.int32), axis=1)
    present = (tiles > 0).astype(jnp.int32)
    ord_of_e = jnp.cumsum(present) - 1
    eids = jnp.arange(N_EXPERTS, dtype=jnp.int32)
    elist = jnp.sum(jnp.where((ord_of_e[None, :] == eids[:, None]) & (present[None, :] == 1), eids[None, :], 0), axis=1)
    ordinal = jnp.sum(jnp.where(tile_expert[:, None] == eids[None, :], ord_of_e[None, :], 0), axis=1)
    first = jnp.concatenate([jnp.ones((1,), jnp.int32),
                             (tile_expert[1:] != tile_expert[:-1]).astype(jnp.int32)])
    one = lambda v: v.reshape(1).astype(jnp.int32)
    meta = (one(n_tiles), first, ordinal.astype(jnp.int32), elist.astype(jnp.int32), one(jnp.sum(present)))
    return pos.reshape(t_all, 2), src, meta


def _pick_tile(n, candidates):
    for c in candidates:
        if n % c == 0:
            return c
    raise ValueError(f"no row tile for {n}")


def kernel(x_prompt, x_sample, state_mlstm_C, state_mlstm_n, state_mlstm_m, state_conv, state_s5_re,
           state_s5_im, norm_mix_g, w_in, b_i, b_f, w_conv, b_conv, head_norm_g, w_pa, s5_a_re, s5_a_im,
           s5_log_step, s5_b_re, s5_b_im, s5_c_re, s5_c_im, s5_d, s5_w_glu, s5_b_glu, w_pb, w_out,
           norm_ffn_g, w_rg, b_rg, w_rexp, b_rexp, w_gate, w_up, w_down, norm_final_g):
    assert state_mlstm_C.shape[0] == 1 and x_sample.shape[1] == 1
    batch, seq, _ = x_prompt.shape
    dec = x_sample.shape[0]
    t_p = batch * seq
    t_all = t_p + dec
    assert seq % CHUNK == 0 and t_p % dec == 0 and dec % LANES == 0
    d_a = H_A * DK
    tm_p = _pick_tile(t_p, (512, 256, 128))
    tm_all = _pick_tile(t_all, (640, 384, 128))
    sample_blk = t_p // dec

    xp = x_prompt.reshape(t_p, D_MODEL)
    xs = x_sample.reshape(dec, D_MODEL)

    g_mix = norm_mix_g[0]
    w_in_t = w_in.reshape(w_in.shape[1:]).T
    n_qkvo = 4 * d_a
    tn = 1024
    n_gate_cols = 2 * H_A
    b_gates = jnp.pad(jnp.concatenate([b_i[0], b_f[0]]), (0, LANES - n_gate_cols))
    xn_all, gates, gates_t = _rmsnorm_rows(xp, g_mix, w_in_t, n_qkvo, b_gates, t_all, 0, tm_p)
    xn_all, gates, _ = _rmsnorm_rows(xs, g_mix, w_in_t, n_qkvo, b_gates, t_all, t_p, dec,
                                     alias=(xn_all, gates))
    qkvo = _matmul_t(xn_all, w_in_t, [j * tn for j in range(n_qkvo // tn)], jnp.zeros((1, n_qkvo), F32),
                     tm_all, tn, F32)
    c_u = n_qkvo + n_gate_cols
    c_ga = c_u + D_B
    starts = [c_ga + j * tn for j in range(2 * D_MODEL // tn)] + [c_u]
    proj2 = _matmul_t(xn_all, w_in_t, starts, jnp.zeros((1, len(starts) * tn), F32), tm_all, tn, F32)
    u_col0 = 2 * D_MODEL

    conv_s_in = state_conv[0].reshape(dec, (CONV_W - 1) * 2 * d_a)
    q_s, kw_s, a_e, s_e, den_e, conv_s, n_s, m_s = _sample_pre(
        qkvo, conv_s_in, w_conv[0], b_conv[0].reshape(1, -1), gates, state_mlstm_m[0],
        state_mlstm_n[0].reshape(dec, d_a), dec, sample_blk)
    v_s = qkvo[t_p:, 2 * d_a:3 * d_a]
    r3 = lambda a: a.reshape(dec, H_A, DK)

    head_g = head_norm_g[0].reshape(1, d_a)
    hn_all, c_p, n_p, m_p, c_s, qc = _mlstm_prompt(
        qkvo, gates, gates_t, w_conv[0], b_conv[0].reshape(1, -1), head_g,
        (r3(q_s), r3(kw_s), r3(v_s), r3(a_e), state_mlstm_C[0]), batch, seq, t_all)
    conv_p = jnp.stack([qkvo[b * seq + seq - (CONV_W - 1):(b + 1) * seq, :2 * d_a] for b in range(batch)])
    hn_all = _sample_post(qc.reshape(dec, d_a), s_e, a_e, den_e, qkvo, head_g, hn_all, dec, sample_blk)

    d_row = s5_d[0].reshape(1, D_B)
    s5_prompt_w, s5_sample_w = _s5_chunk_params(s5_a_re[0], s5_a_im[0], s5_log_step[0], s5_b_re[0],
                                                s5_b_im[0], s5_c_re[0], s5_c_im[0])
    ys_all, s5re_p, s5im_p = _s5_prompt(proj2, u_col0, *s5_prompt_w, d_row, batch, seq, t_all)

    ys_all, s5re_s, s5im_s = _s5_sample(proj2, u_col0, state_s5_re[0].reshape(dec, -1),
                                        state_s5_im[0].reshape(dec, -1),
                                        *s5_sample_w, d_row, ys_all, dec, sample_blk)

    wr = jnp.pad(jnp.concatenate([w_rg[0], w_rexp[0]], axis=1), ((0, 0), (0, LANES - N_GROUPS - N_EXPERTS)))
    wr_hi = wr.astype(BF16)
    wr_lo = (wr - wr_hi.astype(F32)).astype(BF16)
    b_r = jnp.pad(jnp.concatenate([b_rg[0], b_rexp[0]]), (0, LANES - N_GROUPS - N_EXPERTS)).reshape(1, LANES)
    merge_w = (w_pa[0].astype(BF16), s5_w_glu[0].astype(BF16), w_pb[0].astype(BF16), w_out[0].astype(BF16))
    b_glu = s5_b_glu[0].reshape(1, D_B)
    g_ffn = norm_ffn_g[0].reshape(1, D_MODEL)
    tm_m = _pick_tile(t_p, (256, 128))
    outs = _merge(xp, hn_all, ys_all, proj2, b_glu, g_ffn, wr_hi, wr_lo, b_r, *merge_w, t_all, 0, tm_m)
    x1_all, xn2_all, ids, wts = _merge(xs, hn_all, ys_all, proj2, b_glu, g_ffn, wr_hi, wr_lo, b_r,
                                       *merge_w, t_all, t_p, dec, aliases=outs)

    p_rows = -(-(2 * t_all + N_EXPERTS * (MOE_TILE - 1)) // MOE_TILE) * MOE_TILE
    pos, src, meta = _dispatch(ids, t_all, p_rows)
    take_rows = lambda a, idx: a.at[idx].get(mode='promise_in_bounds')
    xs_sorted = take_rows(xn2_all, src)
    yp = _moe_experts(*meta, xs_sorted, w_gate[0], w_up[0], w_down[0])
    yg0 = take_rows(yp, pos[:, 0])
    yg1 = take_rows(yp, pos[:, 1])

    g_fin = norm_final_g.reshape(1, D_MODEL)
    y_prompt = _final(x1_all, yg0, yg1, wts, g_fin, 0, t_p, tm_p).reshape(batch, seq, D_MODEL)
    y_sample = _final(x1_all, yg0, yg1, wts, g_fin, t_p, dec, dec).reshape(dec, 1, D_MODEL)

    lead = lambda a, shape: a.reshape((1,) + shape)
    return (y_prompt, y_sample,
            lead(c_p, (batch, H_A, DK, DK)), lead(n_p, (batch, H_A, DK)), lead(m_p[:, 0, :H_A], (batch, H_A)),
            lead(conv_p, (batch, CONV_W - 1, 2 * d_a)),
            lead(s5re_p, (batch, G_B, P_S5)), lead(s5im_p, (batch, G_B, P_S5)),
            lead(c_s, (dec, H_A, DK, DK)), lead(n_s, (dec, H_A, DK)), lead(m_s[:, :H_A], (dec, H_A)),
            lead(conv_s, (dec, CONV_W - 1, 2 * d_a)),
            lead(s5re_s, (dec, G_B, P_S5)), lead(s5im_s, (dec, G_B, P_S5)))
```

```python
import functools
import math

import jax
import jax.numpy as jnp
from jax import lax
from jax.experimental import pallas as pl
from jax.experimental.pallas import tpu as pltpu

F32 = jnp.float32
BF16 = jnp.bfloat16

D_MODEL = 2048
H_A = 8
DK = 256
CONV_W = 4
CHUNK = 128
D_B = 1024
S5_GROUP = 16
G_B = 64
P_S5 = 64
N_GROUPS = 4
EXP_PER_GROUP = 8
N_EXPERTS = 32
D_EXPERT = 512
EPS = 1e-6

LANES = 128
SUBLANES = 8
VMEM_LIMIT = 56 * 1024 * 1024

S5_CHUNK = 8
S5_CST = (LANES // S5_GROUP) * P_S5
MOE_TILE = 256


def _cparams(sem):
    return pltpu.CompilerParams(dimension_semantics=sem, vmem_limit_bytes=VMEM_LIMIT)


def _silu(x):
    return x * (1.0 / (1.0 + jnp.exp(-x)))


def _sigmoid(x):
    return 1.0 / (1.0 + jnp.exp(-x))


def _log_sigmoid(x):
    return jnp.minimum(x, 0.0) - jnp.log1p(jnp.exp(-jnp.abs(x)))


def _gelu_tanh(x):
    c = math.sqrt(2.0 / math.pi)
    return 0.5 * x * (1.0 + jnp.tanh(c * (x + 0.044715 * (x * x * x))))


def _split3(x):
    hi = x.astype(BF16)
    r = x - hi.astype(F32)
    mid = r.astype(BF16)
    lo = (r - mid.astype(F32)).astype(BF16)
    return hi, mid, lo


def _dot(a, b):
    return jnp.dot(a, b, preferred_element_type=F32)


def _dot_nt(a, b):
    return lax.dot_general(a, b, (((1,), (1,)), ((), ())), preferred_element_type=F32)


def _dot_tn(a, b):
    return lax.dot_general(a, b, (((0,), (0,)), ((), ())), preferred_element_type=F32)


def _rmsnorm_kernel(x_ref, g_ref, wg_ref, bg_ref, bgt_ref, *rest):
    n_out = 3
    o_ref, gates_ref, gates_t_ref = rest[-n_out:]
    x = x_ref[...]
    r = lax.rsqrt(jnp.mean(x * x, axis=-1, keepdims=True) + EPS)
    xn = (x * r * g_ref[...]).astype(o_ref.dtype)
    o_ref[...] = xn
    wg = wg_ref[...].astype(BF16)
    gates_ref[...] = _dot_nt(xn, wg) + bg_ref[...]
    gt = _dot_nt(wg, xn) + bgt_ref[...]
    gates_t_ref[...] = gt[0:gates_t_ref.shape[0], :]


def _rmsnorm_rows(x, g, w_t, gate_row0, b_gates, t_all, row0, tm, alias=None):
    n = x.shape[0]
    blk0 = row0 // tm
    in_specs = [pl.BlockSpec((tm, D_MODEL), lambda i: (i, 0)),
                pl.BlockSpec((1, D_MODEL), lambda i: (0, 0)),
                pl.BlockSpec((pl.Element(LANES), pl.Element(D_MODEL)), lambda i: (gate_row0, 0)),
                pl.BlockSpec((1, LANES), lambda i: (0, 0)),
                pl.BlockSpec((LANES, 1), lambda i: (0, 0))]
    args = [x, g.reshape(1, D_MODEL), w_t, b_gates.reshape(1, LANES), b_gates.reshape(LANES, 1)]
    aliases = {}
    if alias is not None:
        for a in alias:
            aliases[len(args)] = len(aliases)
            in_specs.append(pl.BlockSpec(memory_space=pl.ANY))
            args.append(a)
    return pl.pallas_call(
        _rmsnorm_kernel,
        out_shape=(jax.ShapeDtypeStruct((t_all, D_MODEL), BF16),
                   jax.ShapeDtypeStruct((t_all, LANES), F32),
                   jax.ShapeDtypeStruct((2 * H_A, n), F32)),
        grid=(n // tm,),
        in_specs=in_specs,
        out_specs=(pl.BlockSpec((tm, D_MODEL), lambda i: (i + blk0, 0)),
                   pl.BlockSpec((tm, LANES), lambda i: (i + blk0, 0)),
                   pl.BlockSpec((2 * H_A, tm), lambda i: (0, i))),
        input_output_aliases=aliases,
        compiler_params=_cparams(("arbitrary",)),
        name="rmsnorm_rows",
    )(*args)


def _mm_kernel(a_ref, w_ref, b_ref, o_ref, wb_ref):
    @pl.when(pl.program_id(1) == 0)
    def _():
        wb_ref[...] = w_ref[...].astype(BF16)

    o_ref[...] = (_dot(a_ref[...], wb_ref[...]) + b_ref[...]).astype(o_ref.dtype)


def _matmul(a, w, bias, n_out, tm, tn, out_dtype, col_blk0=0):
    m, k = a.shape
    return pl.pallas_call(
        _mm_kernel,
        out_shape=jax.ShapeDtypeStruct((m, n_out), out_dtype),
        grid=(n_out // tn, m // tm),
        in_specs=[pl.BlockSpec((tm, k), lambda j, i: (i, 0)),
                  pl.BlockSpec((k, tn), lambda j, i: (0, j + col_blk0)),
                  pl.BlockSpec((1, tn), lambda j, i: (0, j))],
        out_specs=pl.BlockSpec((tm, tn), lambda j, i: (i, j)),
        scratch_shapes=[pltpu.VMEM((k, tn), BF16)],
        compiler_params=_cparams(("arbitrary", "arbitrary")),
        name="rows_matmul",
    )(a, w, bias)


def _mm_t_kernel(starts_ref, a_ref, wt_ref, b_ref, o_ref, wb_ref):
    del starts_ref

    @pl.when(pl.program_id(1) == 0)
    def _():
        wb_ref[...] = wt_ref[...].astype(BF16)

    o_ref[...] = (_dot_nt(a_ref[...], wb_ref[...]) + b_ref[...]).astype(o_ref.dtype)


def _matmul_t(a, w_t, row_starts, bias, tm, tn, out_dtype):
    m, k = a.shape
    n_t = len(row_starts)
    assert all(s % SUBLANES == 0 for s in row_starts)
    grid_spec = pltpu.PrefetchScalarGridSpec(
        num_scalar_prefetch=1,
        grid=(n_t, m // tm),
        in_specs=[pl.BlockSpec((tm, k), lambda j, i, st: (i, 0)),
                  pl.BlockSpec((pl.Element(tn), pl.Element(k)), lambda j, i, st: (st[j] * SUBLANES, 0)),
                  pl.BlockSpec((1, tn), lambda j, i, st: (0, j))],
        out_specs=pl.BlockSpec((tm, tn), lambda j, i, st: (i, j)),
        scratch_shapes=[pltpu.VMEM((tn, k), BF16)],
    )
    return pl.pallas_call(
        _mm_t_kernel,
        out_shape=jax.ShapeDtypeStruct((m, n_t * tn), out_dtype),
        grid_spec=grid_spec,
        compiler_params=_cparams(("arbitrary", "arbitrary")),
        name="rows_matmul_t",
    )(jnp.asarray([s // SUBLANES for s in row_starts], jnp.int32), a, w_t, bias)


def _mm_shift_kernel(a_ref, wlo_ref, whi_ref, o_ref, wb_ref, *, shift):
    tn = wlo_ref.shape[1]
    rows = 256

    @pl.when(pl.program_id(1) == 0)
    def _():
        for r in range(wlo_ref.shape[0] // rows):
            sl = slice(r * rows, (r + 1) * rows)
            w = jnp.concatenate([wlo_ref[sl, :], whi_ref[sl, :]], axis=1)
            wb_ref[sl, :] = w[:, shift:shift + tn].astype(BF16)

    o_ref[...] = _dot(a_ref[...], wb_ref[...]).astype(o_ref.dtype)


def _matmul_shifted(a, w, col_starts, tm, tn, out_dtype):
    m, k = a.shape
    shift = col_starts[0] % tn
    assert all(c % tn == shift for c in col_starts) and 0 < shift < LANES
    lo_blk = [c // tn for c in col_starts]
    n_t = len(col_starts)
    base, first = min(lo_blk), lo_blk[0]
    assert all(lo_blk[j] == base + (j + first - base) % n_t for j in range(n_t))
    lo_idx = lambda j: base + (j + (first - base)) % n_t
    per = tn // LANES
    kern = functools.partial(_mm_shift_kernel, shift=shift)
    return pl.pallas_call(
        kern,
        out_shape=jax.ShapeDtypeStruct((m, n_t * tn), out_dtype),
        grid=(n_t, m // tm),
        in_specs=[pl.BlockSpec((tm, k), lambda j, i: (i, 0)),
                  pl.BlockSpec((k, tn), lambda j, i: (0, lo_idx(j))),
                  pl.BlockSpec((k, LANES), lambda j, i: (0, (lo_idx(j) + 1) * per))],
        out_specs=pl.BlockSpec((tm, tn), lambda j, i: (i, j)),
        scratch_shapes=[pltpu.VMEM((k, tn), BF16)],
        compiler_params=_cparams(("arbitrary", "arbitrary")),
        name="rows_matmul_shifted",
    )(a, w, w)


def _sample_c_update(q_ref, kw_ref, v_ref, a_ref, c_ref, c_out, qc_out):
    rows = 2 * SUBLANES
    rid = lax.broadcasted_iota(jnp.int32, (rows, DK), 0)
    pad = jnp.zeros((rows - H_A, DK), F32)
    for b in range(q_ref.shape[0]):
        q16 = jnp.concatenate([q_ref[b], pad], axis=0).astype(BF16)
        kw16 = jnp.concatenate([kw_ref[b], pad], axis=0)
        v16 = jnp.concatenate([v_ref[b], pad], axis=0).astype(BF16)
        qc_rows = []
        for h in range(H_A):
            c_prev = c_ref[b, h]
            qc_rows.append(_dot(q16, c_prev.astype(BF16))[h:h + 1, :])
            kw_h = jnp.where(rid == h, kw16, 0.0).astype(BF16)
            c_out[b, h] = a_ref[b, h:h + 1, :] * c_prev + _dot_tn(kw_h, v16)
        qc_out[b] = jnp.concatenate(qc_rows, axis=0)


def _mlstm_kernel(qk_ref, v_ref, o_ref, gcol_ref, grow_ref, wc_ref, bc_ref, hg_ref,
                  sq_ref, skw_ref, sv_ref, sa_ref, sc_ref,
                  h_ref, c_out, n_out, m_out, sc_out, sqc_out, c_sc, n_sc, m_sc, ext_sc):
    c = pl.program_id(1)
    _sample_c_update(sq_ref, skw_ref, sv_ref, sa_ref, sc_ref, sc_out, sqc_out)
    L = CHUNK
    pad = SUBLANES
    d_a = H_A * DK

    @pl.when(c == 0)
    def _():
        c_sc[...] = jnp.zeros_like(c_sc)
        n_sc[...] = jnp.zeros_like(n_sc)
        m_sc[...] = jnp.zeros_like(m_sc)
        ext_sc[0:pad, :] = jnp.zeros((pad, 2 * d_a), F32)

    x = qk_ref[...]
    ext_sc[pad:pad + L, :] = x
    y = bc_ref[...] + wc_ref[CONV_W - 1:CONV_W, :] * x
    for j in range(1, CONV_W):
        y = y + wc_ref[CONV_W - 1 - j:CONV_W - j, :] * ext_sc[pad - j:pad - j + L, :]
    ext_sc[0:pad, :] = x[L - pad:L, :]
    y = _silu(y)

    gcol = gcol_ref[...]
    grow = grow_ref[...]
    ri = lax.broadcasted_iota(jnp.int32, (L, L), 0)
    ci = lax.broadcasted_iota(jnp.int32, (L, L), 1)
    causal = ci <= ri
    tril = jnp.where(causal, 1.0, 0.0).astype(BF16)
    triu = jnp.where(ri <= ci, 1.0, 0.0).astype(BF16)
    b_cols = sum(_dot(tril, p) for p in _split3(_log_sigmoid(gcol)))
    b_rows = sum(_dot(p, triu) for p in _split3(_log_sigmoid(grow)))

    for h in range(H_A):
        sl = slice(h * DK, (h + 1) * DK)
        q = y[:, sl] * (DK ** -0.5)
        k = y[:, d_a + h * DK:d_a + (h + 1) * DK]
        ig_col = gcol[:, h:h + 1]
        ig_row = grow[h:h + 1, :]
        b_col = b_cols[:, H_A + h:H_A + h + 1]
        b_row = b_rows[H_A + h:H_A + h + 1, :]

        m_prev = m_sc[:, h:h + 1]
        d_log = jnp.where(causal, b_col - b_row + ig_row, -jnp.inf)
        inter_log = b_col + m_prev
        m_t = jnp.maximum(inter_log, jnp.max(d_log, axis=1, keepdims=True))
        qb = q.astype(BF16)
        kb = k.astype(BF16)
        vb = v_ref[:, sl].astype(BF16)
        s = _dot_nt(qb, kb) * jnp.exp(d_log - m_t)
        inter_w = jnp.exp(inter_log - m_t)
        c_prev = c_sc[h]
        n_prev = n_sc[h:h + 1, :]
        num = _dot(s.astype(BF16), vb) + inter_w * _dot(qb, c_prev.astype(BF16))
        nq = jnp.sum(s, axis=1, keepdims=True) + inter_w * jnp.sum(q * n_prev, axis=1, keepdims=True)
        den = jnp.maximum(jnp.abs(nq), jnp.exp(-m_t))
        hh = num / den
        hh = hh * _sigmoid(o_ref[:, sl])
        hh = hh * lax.rsqrt(jnp.mean(hh * hh, axis=1, keepdims=True) + EPS)
        h_ref[:, sl] = (hh * hg_ref[:, sl]).astype(h_ref.dtype)

        m_new = m_t[L - 1:L, :]
        b_last = b_col[L - 1:L, :]
        decay = jnp.exp(b_last + m_prev - m_new)
        w_end = jnp.exp(b_last - b_col + ig_col - m_new)
        kw = k * w_end
        c_sc[h] = decay * c_prev + _dot_tn(kw.astype(BF16), vb)
        n_sc[h:h + 1, :] = decay * n_prev + jnp.sum(kw, axis=0, keepdims=True)
        m_sc[:, h:h + 1] = m_new

    @pl.when(c == pl.num_programs(1) - 1)
    def _():
        c_out[...] = c_sc[...]
        n_out[...] = n_sc[...]
        m_out[...] = m_sc[...]


def _mlstm_prompt(qkvo, gates, gates_t, w_conv, b_conv, head_g, sample, batch, seq, t_all):
    nc = seq // CHUNK
    L = CHUNK
    d_a = H_A * DK
    dec = sample[0].shape[0]
    nb = dec // (batch * nc)
    assert nb * batch * nc == dec
    row = lambda b, c: b * nc + c
    svec = pl.BlockSpec((nb, H_A, DK), lambda b, c: (row(b, c), 0, 0))
    smat = pl.BlockSpec((nb, H_A, DK, DK), lambda b, c: (row(b, c), 0, 0, 0))
    in_specs = [
        pl.BlockSpec((L, 2 * d_a), lambda b, c: (row(b, c), 0)),
        pl.BlockSpec((L, d_a), lambda b, c: (row(b, c), 2)),
        pl.BlockSpec((L, d_a), lambda b, c: (row(b, c), 3)),
        pl.BlockSpec((L, LANES), lambda b, c: (row(b, c), 0)),
        pl.BlockSpec((2 * H_A, L), lambda b, c: (0, row(b, c))),
        pl.BlockSpec((CONV_W, 2 * d_a), lambda b, c: (0, 0)),
        pl.BlockSpec((1, 2 * d_a), lambda b, c: (0, 0)),
        pl.BlockSpec((1, d_a), lambda b, c: (0, 0)),
        svec, svec, svec, svec, smat,
    ]
    out_shape = (
        jax.ShapeDtypeStruct((t_all, d_a), BF16),
        jax.ShapeDtypeStruct((batch, H_A, DK, DK), F32),
        jax.ShapeDtypeStruct((batch, H_A, DK), F32),
        jax.ShapeDtypeStruct((batch, 1, LANES), F32),
        jax.ShapeDtypeStruct((dec, H_A, DK, DK), F32),
        jax.ShapeDtypeStruct((dec, H_A, DK), F32),
    )
    out_specs = (
        pl.BlockSpec((L, d_a), lambda b, c: (row(b, c), 0)),
        pl.BlockSpec((None, H_A, DK, DK), lambda b, c: (b, 0, 0, 0)),
        pl.BlockSpec((None, H_A, DK), lambda b, c: (b, 0, 0)),
        pl.BlockSpec((None, 1, LANES), lambda b, c: (b, 0, 0)),
        smat, svec,
    )
    return pl.pallas_call(
        _mlstm_kernel,
        out_shape=out_shape,
        grid=(batch, nc),
        in_specs=in_specs,
        out_specs=out_specs,
        scratch_shapes=[pltpu.VMEM((H_A, DK, DK), F32), pltpu.VMEM((H_A, DK), F32),
                        pltpu.VMEM((1, LANES), F32), pltpu.VMEM((SUBLANES + L, 2 * d_a), F32)],
        compiler_params=_cparams(("arbitrary", "arbitrary")),
        name="mlstm_prompt",
    )(qkvo, qkvo, qkvo, gates, gates_t, w_conv, b_conv, head_g, *sample)


def _lane_tile(x, reps):
    w = x.shape[1]
    ri = lax.broadcasted_iota(jnp.int32, (w, w * reps), 0)
    ci = lax.broadcasted_iota(jnp.int32, (w, w * reps), 1)
    rep = jnp.where(jnp.bitwise_and(ci, w - 1) == ri, 1.0, 0.0).astype(BF16)
    return _dot(x.astype(BF16), rep)


def _group_mask(rows, cols, row_per, col_per):
    ri = lax.broadcasted_iota(jnp.int32, (rows, cols), 0)
    ci = lax.broadcasted_iota(jnp.int32, (rows, cols), 1)
    return (jnp.right_shift(ri, int(math.log2(row_per))) == jnp.right_shift(ci, int(math.log2(col_per))))


def _s5_build_weights(bin_r, bin_i, kdt, aout_r, aout_i, winc_ref, wintra_ref, wout_ref):
    L = S5_CHUNK
    gpb = LANES // S5_GROUP
    m_inc = _group_mask(LANES, S5_CST, S5_GROUP, P_S5)
    m_lag = _group_mask(LANES, LANES, S5_GROUP, S5_GROUP)
    zero = jnp.zeros((LANES, LANES), BF16)
    lag = [jnp.where(m_lag, _lane_tile(kdt[d], gpb), 0.0).astype(BF16) for d in range(L)]
    for t in range(L):
        rows = slice(t * LANES, (t + 1) * LANES)
        d = L - 1 - t
        winc_ref[rows, 0:S5_CST] = jnp.where(m_inc, _lane_tile(bin_r[d], gpb), 0.0).astype(BF16)
        winc_ref[rows, S5_CST:2 * S5_CST] = jnp.where(m_inc, _lane_tile(bin_i[d], gpb), 0.0).astype(BF16)
        wout_ref[rows, 0:S5_CST] = jnp.where(m_inc, _lane_tile(aout_r[t], gpb), 0.0).astype(BF16)
        wout_ref[rows, S5_CST:2 * S5_CST] = jnp.where(m_inc, _lane_tile(aout_i[t], gpb), 0.0).astype(BF16)
        for t2 in range(L):
            wintra_ref[rows, t2 * LANES:(t2 + 1) * LANES] = lag[t2 - t] if t2 >= t else zero


def _s5_prompt_kernel(u_ref, binr_ref, bini_ref, kdt_ref, aoutr_ref, aouti_ref, lbr_ref, lbi_ref, d_ref,
                      ys_ref, sre_ref, sim_ref, x_sc, winc_ref, wintra_ref, wout_ref, *, batch, nchunk):
    L = S5_CHUNK
    nrow = batch * nchunk
    nst = S5_CST // LANES
    rstr = nchunk + SUBLANES
    _s5_build_weights(binr_ref, bini_ref, kdt_ref, aoutr_ref, aouti_ref, winc_ref, wintra_ref, wout_ref)
    u_t = [u_ref[pl.ds(t, nrow, stride=L), :] for t in range(L)]
    lhs = jnp.concatenate([a.astype(BF16) for a in u_t], axis=1)
    inc = _dot(lhs, winc_ref[...])
    for j in range(2 * nst):
        for b in range(batch):
            x_sc[j, b * rstr:b * rstr + nchunk, :] = inc[b * nchunk:(b + 1) * nchunk, j * LANES:(j + 1) * LANES]

    lbr = [jnp.broadcast_to(lbr_ref[:, j * LANES:(j + 1) * LANES], (batch, LANES)) for j in range(nst)]
    lbi = [jnp.broadcast_to(lbi_ref[:, j * LANES:(j + 1) * LANES], (batch, LANES)) for j in range(nst)]

    def scan_body(r, carry):
        rows = pl.ds(r, batch, stride=rstr)
        out = []
        for j in range(nst):
            xr, xi = carry[j]
            ir = x_sc[j, rows, :]
            ii = x_sc[nst + j, rows, :]
            x_sc[j, rows, :] = xr
            x_sc[nst + j, rows, :] = xi
            out.append((lbr[j] * xr - lbi[j] * xi + ir, lbr[j] * xi + lbi[j] * xr + ii))
        return tuple(out)

    z = jnp.zeros((batch, LANES), F32)
    fin = lax.fori_loop(0, nchunk, scan_body, tuple((z, z) for _ in range(nst)))
    for j in range(nst):
        sre_ref[:, j * LANES:(j + 1) * LANES] = fin[j][0]
        sim_ref[:, j * LANES:(j + 1) * LANES] = fin[j][1]

    xprev = jnp.concatenate(
        [jnp.concatenate([x_sc[j, b * rstr:b * rstr + nchunk, :] for b in range(batch)], axis=0)
         for j in range(2 * nst)], axis=1).astype(BF16)
    y = _dot(lhs, wintra_ref[...]) + _dot_nt(xprev, wout_ref[...])
    for t in range(L):
        yt = y[:, t * LANES:(t + 1) * LANES] + d_ref[...] * u_t[t]
        ys_ref[pl.ds(t, nrow, stride=L), :] = _gelu_tanh(yt).astype(ys_ref.dtype)


def _s5_prompt(proj2, u_col0, bin_r, bin_i, kdt, aout_r, aout_i, lb8r, lb8i, d_row, batch, seq, t_all):
    nchunk = seq // S5_CHUNK
    t_p = batch * seq
    kern = functools.partial(_s5_prompt_kernel, batch=batch, nchunk=nchunk)
    ub0 = u_col0 // LANES
    ncb = D_B // LANES
    kw = S5_CHUNK * LANES
    per_blk = lambda a: pl.BlockSpec((a.shape[0], None) + a.shape[2:], lambda g: (0, g, 0, 0))
    return pl.pallas_call(
        kern,
        out_shape=(jax.ShapeDtypeStruct((t_all, D_B), F32),
                   jax.ShapeDtypeStruct((batch, G_B * P_S5), F32),
                   jax.ShapeDtypeStruct((batch, G_B * P_S5), F32)),
        grid=(ncb,),
        in_specs=[pl.BlockSpec((t_p, LANES), lambda g: (0, ub0 + g)),
                  per_blk(bin_r), per_blk(bin_i), per_blk(kdt), per_blk(aout_r), per_blk(aout_i),
                  pl.BlockSpec((None, 1, S5_CST), lambda g: (g, 0, 0)),
                  pl.BlockSpec((None, 1, S5_CST), lambda g: (g, 0, 0)),
                  pl.BlockSpec((1, LANES), lambda g: (0, g))],
        out_specs=(pl.BlockSpec((t_p, LANES), lambda g: (0, g)),
                   pl.BlockSpec((batch, S5_CST), lambda g: (0, g)),
                   pl.BlockSpec((batch, S5_CST), lambda g: (0, g))),
        scratch_shapes=[pltpu.VMEM((2 * S5_CST // LANES, batch * (nchunk + SUBLANES), LANES), F32),
                        pltpu.VMEM((kw, 2 * S5_CST), BF16), pltpu.VMEM((kw, kw), BF16),
                        pltpu.VMEM((kw, 2 * S5_CST), BF16)],
        compiler_params=_cparams(("arbitrary",)),
        name="s5_prompt",
    )(proj2, bin_r, bin_i, kdt, aout_r, aout_i, lb8r, lb8i, d_row)


def _sample_pre_kernel(qk_ref, conv_ref, wc_ref, bc_ref, g_ref, m_ref, n_ref,
                       q_out, kw_out, a_out, s_out, den_out, conv_out, n_out, m_out):
    c2 = 2 * H_A * DK
    x_new = qk_ref[...]
    y = bc_ref[...] + wc_ref[CONV_W - 1:CONV_W, :] * x_new
    for j in range(CONV_W - 1):
        y = y + wc_ref[j:j + 1, :] * conv_ref[:, j * c2:(j + 1) * c2]
    y = _silu(y)
    conv_out[:, 0:(CONV_W - 2) * c2] = conv_ref[:, c2:(CONV_W - 1) * c2]
    conv_out[:, (CONV_W - 2) * c2:(CONV_W - 1) * c2] = x_new
    g = g_ref[...]
    bd = x_new.shape[0]
    m_cols = []
    for h in range(H_A):
        sl = slice(h * DK, (h + 1) * DK)
        q = y[:, sl] * (DK ** -0.5)
        k = y[:, H_A * DK + h * DK:H_A * DK + (h + 1) * DK]
        ig = g[:, h:h + 1]
        lf = _log_sigmoid(g[:, H_A + h:H_A + h + 1])
        m_prev = m_ref[:, h:h + 1]
        m_t = jnp.maximum(lf + m_prev, ig)
        a = jnp.exp(lf + m_prev - m_t)
        wgt = jnp.exp(ig - m_t)
        n_prev = n_ref[:, sl]
        s = jnp.sum(q * k, axis=1, keepdims=True) * wgt
        nq = s + a * jnp.sum(q * n_prev, axis=1, keepdims=True)
        den = jnp.maximum(jnp.abs(nq), jnp.exp(-m_t))
        kw = wgt * k
        q_out[:, sl] = q
        kw_out[:, sl] = kw
        a_out[:, sl] = jnp.broadcast_to(a, (bd, DK))
        s_out[:, sl] = jnp.broadcast_to(s, (bd, DK))
        den_out[:, sl] = jnp.broadcast_to(den, (bd, DK))
        n_out[:, sl] = a * n_prev + kw
        m_cols.append(m_t)
    lane = lax.broadcasted_iota(jnp.int32, (bd, LANES), 1)
    m_full = jnp.zeros((bd, LANES), F32)
    for h in range(H_A):
        m_full = jnp.where(lane == h, m_cols[h], m_full)
    m_out[...] = m_full


def _sample_pre(qkvo, conv_state, w_conv, b_conv, gates, m_state, n_state, dec, row_blk):
    c2 = 2 * H_A * DK
    d = H_A * DK
    full = lambda shape: pl.BlockSpec(shape, lambda i: (0,) * len(shape))
    rows = lambda: jax.ShapeDtypeStruct((dec, d), F32)
    return pl.pallas_call(
        _sample_pre_kernel,
        out_shape=(rows(), rows(), rows(), rows(), rows(),
                   jax.ShapeDtypeStruct((dec, (CONV_W - 1) * c2), F32), rows(),
                   jax.ShapeDtypeStruct((dec, LANES), F32)),
        grid=(1,),
        in_specs=[pl.BlockSpec((dec, c2), lambda i: (row_blk, 0)),
                  full((dec, (CONV_W - 1) * c2)), full((CONV_W, c2)), full((1, c2)),
                  pl.BlockSpec((dec, LANES), lambda i: (row_blk, 0)),
                  full((dec, H_A)), full((dec, d))],
        out_specs=(full((dec, d)), full((dec, d)), full((dec, d)), full((dec, d)), full((dec, d)),
                   full((dec, (CONV_W - 1) * c2)), full((dec, d)), full((dec, LANES))),
        compiler_params=_cparams(("arbitrary",)),
        name="sample_pre",
    )(qkvo, conv_state, w_conv, b_conv, gates, m_state, n_state)


def _sample_post_kernel(qc_ref, s_ref, a_ref, den_ref, vo_ref, hg_ref, hn_in, h_ref):
    del hn_in
    d = H_A * DK
    num = s_ref[...] * vo_ref[:, 0:d] + a_ref[...] * qc_ref[...]
    hh = num / den_ref[...]
    hh = hh * _sigmoid(vo_ref[:, d:2 * d])
    for h in range(H_A):
        sl = slice(h * DK, (h + 1) * DK)
        seg = hh[:, sl]
        seg = seg * lax.rsqrt(jnp.mean(seg * seg, axis=1, keepdims=True) + EPS)
        h_ref[:, sl] = (seg * hg_ref[:, sl]).astype(h_ref.dtype)


def _sample_post(qc, s_e, a_e, den_e, qkvo, head_g, hn_all, dec, row_blk):
    d = H_A * DK
    full = lambda shape: pl.BlockSpec(shape, lambda i: (0,) * len(shape))
    return pl.pallas_call(
        _sample_post_kernel,
        out_shape=jax.ShapeDtypeStruct(hn_all.shape, hn_all.dtype),
        grid=(1,),
        in_specs=[full((dec, d)), full((dec, d)), full((dec, d)), full((dec, d)),
                  pl.BlockSpec((dec, 2 * d), lambda i: (row_blk, 1)),
                  full((1, d)), pl.BlockSpec(memory_space=pl.ANY)],
        out_specs=pl.BlockSpec((dec, d), lambda i: (row_blk, 0)),
        input_output_aliases={6: 0},
        compiler_params=_cparams(("arbitrary",)),
        name="sample_post",
    )(qc, s_e, a_e, den_e, qkvo, head_g, hn_all)


def _s5_sample_kernel(u_ref, sr_ref, si_ref, br_ref, bi_ref, cr_ref, ci_ref, lbr_ref, lbi_ref, d_ref, ys_in,
                      ys_ref, sre_out, sim_out):
    del ys_in
    gpb = LANES // S5_GROUP
    mask = _group_mask(LANES, S5_CST, S5_GROUP, P_S5)
    expand = lambda blk: jnp.where(mask, _lane_tile(blk, gpb), 0.0)
    for g in range(D_B // LANES):
        ch = slice(g * LANES, (g + 1) * LANES)
        sl = slice(g * S5_CST, (g + 1) * S5_CST)
        u = u_ref[:, ch]
        bmat = jnp.concatenate([expand(br_ref[g]), expand(bi_ref[g])], axis=1).astype(BF16)
        bu = _dot(u.astype(BF16), bmat)
        lbr = lbr_ref[g]
        lbi = lbi_ref[g]
        sr = sr_ref[:, sl]
        si = si_ref[:, sl]
        xr = lbr * sr - lbi * si + bu[:, 0:S5_CST]
        xi = lbr * si + lbi * sr + bu[:, S5_CST:2 * S5_CST]
        sre_out[:, sl] = xr
        sim_out[:, sl] = xi
        x = jnp.concatenate([xr, xi], axis=1).astype(BF16)
        cmat = jnp.concatenate([expand(cr_ref[g]), -expand(ci_ref[g])], axis=1).astype(BF16)
        y = _dot_nt(x, cmat) + d_ref[:, ch] * u
        ys_ref[:, ch] = _gelu_tanh(y).astype(ys_ref.dtype)


def _s5_sample(proj2, u_col0, s_re, s_im, b_r, b_i, c_r, c_i, lbr, lbi, d_row, ys_all, dec, row_blk):
    full = lambda shape: pl.BlockSpec(shape, lambda i: (0,) * len(shape))
    n_state = G_B * P_S5
    ub0 = u_col0 // D_B
    params = (b_r, b_i, c_r, c_i, lbr, lbi)
    return pl.pallas_call(
        _s5_sample_kernel,
        out_shape=(jax.ShapeDtypeStruct(ys_all.shape, ys_all.dtype),
                   jax.ShapeDtypeStruct((dec, n_state), F32),
                   jax.ShapeDtypeStruct((dec, n_state), F32)),
        grid=(1,),
        in_specs=[pl.BlockSpec((dec, D_B), lambda i: (row_blk, ub0)),
                  full((dec, n_state)), full((dec, n_state))]
                 + [full(p.shape) for p in params]
                 + [full((1, D_B)), pl.BlockSpec(memory_space=pl.ANY)],
        out_specs=(pl.BlockSpec((dec, D_B), lambda i: (row_blk, 0)),
                   full((dec, n_state)), full((dec, n_state))),
        input_output_aliases={10: 0},
        compiler_params=_cparams(("arbitrary",)),
        name="s5_sample",
    )(proj2, s_re, s_im, *params, d_row, ys_all)


def _merge_kernel(x_ref, hn_ref, ys_ref, ga_ref, gb_ref, bglu_ref, gffn_ref, wrh_ref, wrl_ref, br_ref,
                  wpa_hbm, wglu_hbm, wpb_hbm, wout_hbm, *rest):
    n_alias = len(rest) - 9
    x1_ref, xn_ref, ids_ref, wts_ref = rest[n_alias:n_alias + 4]
    wpa, wglu, wpb, wout, sem = rest[n_alias + 4:]

    @pl.when(pl.program_id(0) == 0)
    def _():
        copies = [pltpu.make_async_copy(src, dst, sem.at[i])
                  for i, (src, dst) in enumerate(((wpa_hbm, wpa), (wglu_hbm, wglu),
                                                  (wpb_hbm, wpb), (wout_hbm, wout)))]
        for cp in copies:
            cp.start()
        for cp in copies:
            cp.wait()

    ya = _dot(hn_ref[...], wpa[...])
    ys = ys_ref[...]
    gate = _sigmoid(_dot(ys.astype(BF16), wglu[...]) + bglu_ref[...])
    yb = _dot((ys * gate).astype(BF16), wpb[...])
    z = _sigmoid(ga_ref[...]) * ya + _sigmoid(gb_ref[...]) * yb
    x1 = x_ref[...] + _dot(z.astype(BF16), wout[...])
    x1_ref[...] = x1
    xn = x1 * lax.rsqrt(jnp.mean(x1 * x1, axis=1, keepdims=True) + EPS) * gffn_ref[...]
    xn_ref[...] = xn
    xh = xn.astype(BF16)
    xl = (xn - xh.astype(F32)).astype(BF16)
    logits = _dot(xh, wrh_ref[...]) + _dot(xl, wrh_ref[...]) + _dot(xh, wrl_ref[...]) + br_ref[...]

    lane_i = lax.broadcasted_iota(jnp.int32, logits.shape, 1)
    lane = lane_i.astype(F32)
    neg = -jnp.inf
    big = float(1 << 20)
    gl = jnp.where(lane_i < N_GROUPS, logits, neg)
    gmax = jnp.max(gl, axis=1, keepdims=True)
    gsum = jnp.sum(jnp.exp(gl - gmax), axis=1, keepdims=True)
    gidx = jnp.min(jnp.where(gl == gmax, lane, big), axis=1, keepdims=True)
    pg_sel = 1.0 / gsum
    lo = N_GROUPS + gidx * EXP_PER_GROUP
    in_grp = (lane >= lo) & (lane < lo + EXP_PER_GROUP)
    el = jnp.where(in_grp, logits, neg)
    emax = jnp.max(el, axis=1, keepdims=True)
    ee = jnp.exp(el - emax)
    pe = ee / jnp.sum(ee, axis=1, keepdims=True)
    v0 = jnp.max(pe, axis=1, keepdims=True)
    i0 = jnp.min(jnp.where(in_grp & (pe == v0), lane, big), axis=1, keepdims=True)
    rest_m = in_grp & (lane != i0)
    pe1 = jnp.where(rest_m, pe, neg)
    v1 = jnp.max(pe1, axis=1, keepdims=True)
    i1 = jnp.min(jnp.where(rest_m & (pe1 == v1), lane, big), axis=1, keepdims=True)
    tot = v0 + v1
    w0 = pg_sel * (v0 / tot)
    w1 = pg_sel * (v1 / tot)
    ids = jnp.where(lane_i == 0, i0 - N_GROUPS, jnp.where(lane_i == 1, i1 - N_GROUPS, 0.0))
    ids_ref[...] = ids.astype(jnp.int32)
    wts_ref[...] = jnp.where(lane_i == 0, w0, jnp.where(lane_i == 1, w1, 0.0))


def _merge(x, hn_all, ys_all, proj2, b_glu, g_ffn, wr_hi, wr_lo, b_r, wpa, wglu, wpb, wout,
           t_all, row0, tm, aliases=None):
    n = x.shape[0]
    blk0 = row0 // tm
    const = lambda shape: pl.BlockSpec(shape, lambda i: (0,) * len(shape))
    any_spec = pl.BlockSpec(memory_space=pl.ANY)
    in_specs = [pl.BlockSpec((tm, D_MODEL), lambda i: (i, 0)),
                pl.BlockSpec((tm, D_MODEL), lambda i: (i + blk0, 0)),
                pl.BlockSpec((tm, D_B), lambda i: (i + blk0, 0)),
                pl.BlockSpec((tm, D_MODEL), lambda i: (i + blk0, 0)),
                pl.BlockSpec((tm, D_MODEL), lambda i: (i + blk0, 1)),
                const((1, D_B)), const((1, D_MODEL)),
                const((D_MODEL, LANES)), const((D_MODEL, LANES)), const((1, LANES)),
                any_spec, any_spec, any_spec, any_spec]
    args = [x, hn_all, ys_all, proj2, proj2, b_glu, g_ffn, wr_hi, wr_lo, b_r, wpa, wglu, wpb, wout]
    io_alias = {}
    if aliases is not None:
        for j, a in enumerate(aliases):
            in_specs.append(any_spec)
            args.append(a)
            io_alias[14 + j] = j
    out_shape = (jax.ShapeDtypeStruct((t_all, D_MODEL), F32),
                 jax.ShapeDtypeStruct((t_all, D_MODEL), F32),
                 jax.ShapeDtypeStruct((t_all, LANES), jnp.int32),
                 jax.ShapeDtypeStruct((t_all, LANES), F32))
    out_specs = (pl.BlockSpec((tm, D_MODEL), lambda i: (i + blk0, 0)),
                 pl.BlockSpec((tm, D_MODEL), lambda i: (i + blk0, 0)),
                 pl.BlockSpec((tm, LANES), lambda i: (i + blk0, 0)),
                 pl.BlockSpec((tm, LANES), lambda i: (i + blk0, 0)))
    return pl.pallas_call(
        _merge_kernel,
        out_shape=out_shape,
        grid=(n // tm,),
        in_specs=in_specs,
        out_specs=out_specs,
        scratch_shapes=[pltpu.VMEM(wpa.shape, BF16), pltpu.VMEM(wglu.shape, BF16),
                        pltpu.VMEM(wpb.shape, BF16), pltpu.VMEM(wout.shape, BF16),
                        pltpu.SemaphoreType.DMA((4,))],
        input_output_aliases=io_alias,
        compiler_params=_cparams(("arbitrary",)),
        name="merge_router",
    )(*args)


def _moe_kernel(nt_ref, first_ref, ord_ref, elist_ref, nord_ref, xs_ref, wg_hbm, wu_hbm, wd_hbm, o_ref,
                stg_g, stg_u, stg_d, wg_sc, wu_sc, wd_sc, sem):
    i = pl.program_id(0)
    n_ord = nord_ref[0]

    def weight_copies(k, slot):
        e = elist_ref[k]
        return (pltpu.make_async_copy(wg_hbm.at[e], stg_g.at[slot], sem.at[slot, 0]),
                pltpu.make_async_copy(wu_hbm.at[e], stg_u.at[slot], sem.at[slot, 1]),
                pltpu.make_async_copy(wd_hbm.at[e], stg_d.at[slot], sem.at[slot, 2]))

    @pl.when(i == 0)
    def _():
        for cp in weight_copies(0, 0):
            cp.start()

        @pl.when(n_ord > 1)
        def _():
            for cp in weight_copies(1, 1):
                cp.start()

    valid = i < nt_ref[0]
    k = ord_ref[i]

    @pl.when(valid & (first_ref[i] == 1))
    def _():
        slot = k % 2
        for cp in weight_copies(k, slot):
            cp.wait()
        wg_sc[...] = stg_g[slot].astype(BF16)
        wu_sc[...] = stg_u[slot].astype(BF16)
        wd_sc[...] = stg_d[slot].astype(BF16)

        @pl.when(k + 2 < n_ord)
        def _():
            for cp in weight_copies(k + 2, slot):
                cp.start()

    @pl.when(valid)
    def _():
        x = xs_ref[...].astype(BF16)
        hg = _dot(x, wg_sc[...])
        hu = _dot(x, wu_sc[...])
        hh = (_silu(hg) * hu).astype(BF16)
        o_ref[...] = _dot(hh, wd_sc[...])


def _moe_experts(n_tiles, first, ordinal, elist, n_ord, xs, w_gate, w_up, w_down):
    p_rows = xs.shape[0]
    last = lambda i, nt, *_: jnp.minimum(i, nt[0] - 1)
    any_spec = pl.BlockSpec(memory_space=pl.ANY)
    grid_spec = pltpu.PrefetchScalarGridSpec(
        num_scalar_prefetch=5,
        grid=(p_rows // MOE_TILE,),
        in_specs=[pl.BlockSpec((MOE_TILE, D_MODEL), lambda i, *s: (last(i, *s), 0)),
                  any_spec, any_spec, any_spec],
        out_specs=pl.BlockSpec((MOE_TILE, D_MODEL), lambda i, *s: (last(i, *s), 0)),
        scratch_shapes=[pltpu.VMEM((2, D_MODEL, D_EXPERT), F32), pltpu.VMEM((2, D_MODEL, D_EXPERT), F32),
                        pltpu.VMEM((2, D_EXPERT, D_MODEL), F32),
                        pltpu.VMEM((D_MODEL, D_EXPERT), BF16), pltpu.VMEM((D_MODEL, D_EXPERT), BF16),
                        pltpu.VMEM((D_EXPERT, D_MODEL), BF16),
                        pltpu.SemaphoreType.DMA((2, 3))],
    )
    return pl.pallas_call(
        _moe_kernel,
        out_shape=jax.ShapeDtypeStruct((p_rows, D_MODEL), F32),
        grid_spec=grid_spec,
        compiler_params=_cparams(("arbitrary",)),
        name="moe_experts",
    )(n_tiles, first, ordinal, elist, n_ord, xs, w_gate, w_up, w_down)


def _final_kernel(x1_ref, y0_ref, y1_ref, w_ref, g_ref, o_ref):
    w = w_ref[...]
    x2 = x1_ref[...] + w[:, 0:1] * y0_ref[...] + w[:, 1:2] * y1_ref[...]
    o_ref[...] = x2 * lax.rsqrt(jnp.mean(x2 * x2, axis=1, keepdims=True) + EPS) * g_ref[...]


def _final(x1_all, yg0, yg1, wts, g_final, row0, n, tm):
    blk0 = row0 // tm
    rows = pl.BlockSpec((tm, D_MODEL), lambda i: (i + blk0, 0))
    return pl.pallas_call(
        _final_kernel,
        out_shape=jax.ShapeDtypeStruct((n, D_MODEL), F32),
        grid=(n // tm,),
        in_specs=[rows, rows, rows,
                  pl.BlockSpec((tm, LANES), lambda i: (i + blk0, 0)),
                  pl.BlockSpec((1, D_MODEL), lambda i: (0, 0))],
        out_specs=pl.BlockSpec((tm, D_MODEL), lambda i: (i, 0)),
        compiler_params=_cparams(("arbitrary",)),
        name="combine_final_norm",
    )(x1_all, yg0, yg1, wts, g_final)


def _s5_discretise(a_re, a_im, log_step, b_re, b_im):
    dt = jnp.exp(log_step)[:, None]
    mag = jnp.exp(a_re * dt)
    lb_re = mag * jnp.cos(a_im * dt)
    lb_im = mag * jnp.sin(a_im * dt)
    den = a_re * a_re + a_im * a_im
    nr = lb_re - 1.0
    coef_re = (nr * a_re + lb_im * a_im) / den
    coef_im = (lb_im * a_re - nr * a_im) / den
    bb_re = coef_re[..., None] * b_re - coef_im[..., None] * b_im
    bb_im = coef_re[..., None] * b_im + coef_im[..., None] * b_re
    return lb_re, lb_im, bb_re, bb_im


def _s5_chunk_params(a_re, a_im, log_step, b_re, b_im, c_re, c_im):
    lb_re, lb_im, bb_re, bb_im = _s5_discretise(a_re, a_im, log_step, b_re, b_im)
    L = S5_CHUNK
    gpb = LANES // S5_GROUP
    ncb = G_B // gpb
    pr, pi = [jnp.ones_like(lb_re)], [jnp.zeros_like(lb_re)]
    for _ in range(L):
        pr, pi = pr + [pr[-1] * lb_re - pi[-1] * lb_im], pi + [pr[-1] * lb_im + pi[-1] * lb_re]
    pw_r, pw_i = jnp.stack(pr), jnp.stack(pi)
    bt_re = bb_re.transpose(0, 2, 1)
    bt_im = bb_im.transpose(0, 2, 1)
    lbb_r = pw_r[:L, :, None, :] * bt_re - pw_i[:L, :, None, :] * bt_im
    lbb_i = pw_r[:L, :, None, :] * bt_im + pw_i[:L, :, None, :] * bt_re
    hp = lax.Precision.HIGHEST
    kdt = (jnp.einsum('gop,dgcp->dgco', c_re, lbb_r, precision=hp)
           - jnp.einsum('gop,dgcp->dgco', c_im, lbb_i, precision=hp))
    a_r = c_re * pw_r[1:, :, None, :] - c_im * pw_i[1:, :, None, :]
    a_i = -(c_re * pw_i[1:, :, None, :] + c_im * pw_r[1:, :, None, :])

    def blocks(m):
        return m.reshape(m.shape[:-3] + (ncb, gpb * m.shape[-2], m.shape[-1]))

    vec = lambda v: v.reshape(ncb, 1, S5_CST)
    prompt = (blocks(lbb_r), blocks(lbb_i), blocks(kdt), blocks(a_r), blocks(a_i), vec(pw_r[L]), vec(pw_i[L]))
    sample = (blocks(lbb_r[0]), blocks(lbb_i[0]), blocks(c_re), blocks(c_im), vec(lb_re), vec(lb_im))
    return prompt, sample


def _dispatch(ids, t_all, p_rows):
    e = ids[:, :2].reshape(-1)
    onehot = (e[:, None] == jnp.arange(N_EXPERTS, dtype=jnp.int32)[None, :]).astype(jnp.int32)
    csum = jnp.cumsum(onehot, axis=0)
    rank = jnp.sum((csum - onehot) * onehot, axis=1)
    counts = csum[-1]
    tiles = (counts + MOE_TILE - 1) // MOE_TILE
    tile_end = jnp.cumsum(tiles)
    tile_start = tile_end - tiles
    pos = jnp.sum(onehot * (tile_start * MOE_TILE)[None, :], axis=1) + rank
    tok = jnp.arange(2 * t_all, dtype=jnp.int32) // 2
    src = (jnp.arange(p_rows, dtype=jnp.int32) % t_all).at[pos].set(tok)
    n_tiles = tile_end[-1]
    tidx = jnp.arange(p_rows // MOE_TILE, dtype=jnp.int32)
    tclamp = jnp.minimum(tidx, n_tiles - 1)
    tile_expert = jnp.sum((tile_end[None, :] <= tclamp[:, None]).astype(jnp.int32), axis=1)
    present = (tiles > 0).astype(jnp.int32)
    ord_of_e = jnp.cumsum(present) - 1
    eids = jnp.arange(N_EXPERTS, dtype=jnp.int32)
    elist = jnp.sum(jnp.where((ord_of_e[None, :] == eids[:, None]) & (present[None, :] == 1), eids[None, :], 0), axis=1)
    ordinal = jnp.sum(jnp.where(tile_expert[:, None] == eids[None, :], ord_of_e[None, :], 0), axis=1)
    first = jnp.concatenate([jnp.ones((1,), jnp.int32),
                             (tile_expert[1:] != tile_expert[:-1]).astype(jnp.int32)])
    one = lambda v: v.reshape(1).astype(jnp.int32)
    meta = (one(n_tiles), first, ordinal.astype(jnp.int32), elist.astype(jnp.int32), one(jnp.sum(present)))
    return pos.reshape(t_all, 2), src, meta


def _pick_tile(n, candidates):
    for c in candidates:
        if n % c == 0:
            return c
    raise ValueError(f"no row tile for {n}")


def kernel(x_prompt, x_sample, state_mlstm_C, state_mlstm_n, state_mlstm_m, state_conv, state_s5_re,
           state_s5_im, norm_mix_g, w_in, b_i, b_f, w_conv, b_conv, head_norm_g, w_pa, s5_a_re, s5_a_im,
           s5_log_step, s5_b_re, s5_b_im, s5_c_re, s5_c_im, s5_d, s5_w_glu, s5_b_glu, w_pb, w_out,
           norm_ffn_g, w_rg, b_rg, w_rexp, b_rexp, w_gate, w_up, w_down, norm_final_g):
    assert state_mlstm_C.shape[0] == 1 and x_sample.shape[1] == 1
    batch, seq, _ = x_prompt.shape
    dec = x_sample.shape[0]
    t_p = batch * seq
    t_all = t_p + dec
    assert seq % CHUNK == 0 and t_p % dec == 0 and dec % LANES == 0
    d_a = H_A * DK
    tm_p = _pick_tile(t_p, (512, 256, 128))
    tm_all = _pick_tile(t_all, (1664, 640, 384, 128))
    sample_blk = t_p // dec

    xp = x_prompt.reshape(t_p, D_MODEL)
    xs = x_sample.reshape(dec, D_MODEL)

    g_mix = norm_mix_g[0]
    w_in_t = w_in.reshape(w_in.shape[1:]).T
    n_qkvo = 4 * d_a
    tn = 1024
    n_gate_cols = 2 * H_A
    b_gates = jnp.pad(jnp.concatenate([b_i[0], b_f[0]]), (0, LANES - n_gate_cols))
    xn_all, gates, gates_t = _rmsnorm_rows(xp, g_mix, w_in_t, n_qkvo, b_gates, t_all, 0, tm_p)
    xn_all, gates, _ = _rmsnorm_rows(xs, g_mix, w_in_t, n_qkvo, b_gates, t_all, t_p, dec,
                                     alias=(xn_all, gates))
    qkvo = _matmul_t(xn_all, w_in_t, [j * tn for j in range(n_qkvo // tn)], jnp.zeros((1, n_qkvo), F32),
                     tm_all, tn, F32)
    c_u = n_qkvo + n_gate_cols
    c_ga = c_u + D_B
    starts = [c_ga + j * tn for j in range(2 * D_MODEL // tn)] + [c_u]
    proj2 = _matmul_t(xn_all, w_in_t, starts, jnp.zeros((1, len(starts) * tn), F32), tm_all, tn, F32)
    u_col0 = 2 * D_MODEL

    conv_s_in = state_conv[0].reshape(dec, (CONV_W - 1) * 2 * d_a)
    q_s, kw_s, a_e, s_e, den_e, conv_s, n_s, m_s = _sample_pre(
        qkvo, conv_s_in, w_conv[0], b_conv[0].reshape(1, -1), gates, state_mlstm_m[0],
        state_mlstm_n[0].reshape(dec, d_a), dec, sample_blk)
    v_s = qkvo[t_p:, 2 * d_a:3 * d_a]
    r3 = lambda a: a.reshape(dec, H_A, DK)

    head_g = head_norm_g[0].reshape(1, d_a)
    hn_all, c_p, n_p, m_p, c_s, qc = _mlstm_prompt(
        qkvo, gates, gates_t, w_conv[0], b_conv[0].reshape(1, -1), head_g,
        (r3(q_s), r3(kw_s), r3(v_s), r3(a_e), state_mlstm_C[0]), batch, seq, t_all)
    conv_p = jnp.stack([qkvo[b * seq + seq - (CONV_W - 1):(b + 1) * seq, :2 * d_a] for b in range(batch)])
    hn_all = _sample_post(qc.reshape(dec, d_a), s_e, a_e, den_e, qkvo, head_g, hn_all, dec, sample_blk)

    d_row = s5_d[0].reshape(1, D_B)
    s5_prompt_w, s5_sample_w = _s5_chunk_params(s5_a_re[0], s5_a_im[0], s5_log_step[0], s5_b_re[0],
                                                s5_b_im[0], s5_c_re[0], s5_c_im[0])
    ys_all, s5re_p, s5im_p = _s5_prompt(proj2, u_col0, *s5_prompt_w, d_row, batch, seq, t_all)

    ys_all, s5re_s, s5im_s = _s5_sample(proj2, u_col0, state_s5_re[0].reshape(dec, -1),
                                        state_s5_im[0].reshape(dec, -1),
                                        *s5_sample_w, d_row, ys_all, dec, sample_blk)

    wr = jnp.pad(jnp.concatenate([w_rg[0], w_rexp[0]], axis=1), ((0, 0), (0, LANES - N_GROUPS - N_EXPERTS)))
    wr_hi = wr.astype(BF16)
    wr_lo = (wr - wr_hi.astype(F32)).astype(BF16)
    b_r = jnp.pad(jnp.concatenate([b_rg[0], b_rexp[0]]), (0, LANES - N_GROUPS - N_EXPERTS)).reshape(1, LANES)
    merge_w = (w_pa[0].astype(BF16), s5_w_glu[0].astype(BF16), w_pb[0].astype(BF16), w_out[0].astype(BF16))
    b_glu = s5_b_glu[0].reshape(1, D_B)
    g_ffn = norm_ffn_g[0].reshape(1, D_MODEL)
    tm_m = _pick_tile(t_p, (256, 128))
    outs = _merge(xp, hn_all, ys_all, proj2, b_glu, g_ffn, wr_hi, wr_lo, b_r, *merge_w, t_all, 0, tm_m)
    x1_all, xn2_all, ids, wts = _merge(xs, hn_all, ys_all, proj2, b_glu, g_ffn, wr_hi, wr_lo, b_r,
                                       *merge_w, t_all, t_p, dec, aliases=outs)

    p_rows = -(-(2 * t_all + N_EXPERTS * (MOE_TILE - 1)) // MOE_TILE) * MOE_TILE
    pos, src, meta = _dispatch(ids, t_all, p_rows)
    take_rows = lambda a, idx: a.at[idx].get(mode='promise_in_bounds')
    xs_sorted = take_rows(xn2_all, src)
    yp = _moe_experts(*meta, xs_sorted, w_gate[0], w_up[0], w_down[0])
    yg0 = take_rows(yp, pos[:, 0])
    yg1 = take_rows(yp, pos[:, 1])

    g_fin = norm_final_g.reshape(1, D_MODEL)
    y_prompt = _final(x1_all, yg0, yg1, wts, g_fin, 0, t_p, tm_p).reshape(batch, seq, D_MODEL)
    y_sample = _final(x1_all, yg0, yg1, wts, g_fin, t_p, dec, dec).reshape(dec, 1, D_MODEL)

    lead = lambda a, shape: a.reshape((1,) + shape)
    return (y_prompt, y_sample,
            lead(c_p, (batch, H_A, DK, DK)), lead(n_p, (batch, H_A, DK)), lead(m_p[:, 0, :H_A], (batch, H_A)),
            lead(conv_p, (batch, CONV_W - 1, 2 * d_a)),
            lead(s5re_p, (batch, G_B, P_S5)), lead(s5im_p, (batch, G_B, P_S5)),
            lead(c_s, (dec, H_A, DK, DK)), lead(n_s, (dec, H_A, DK)), lead(m_s[:, :H_A], (dec, H_A)),
            lead(conv_s, (dec, CONV_W - 1, 2 * d_a)),
            lead(s5re_s, (dec, G_B, P_S5)), lead(s5im_s, (dec, G_B, P_S5)))
```

```python
import functools
import math

import jax
import jax.numpy as jnp
from jax import lax
from jax.experimental import pallas as pl
from jax.experimental.pallas import tpu as pltpu

F32 = jnp.float32
BF16 = jnp.bfloat16

D_MODEL = 2048
H_A = 8
DK = 256
CONV_W = 4
CHUNK = 128
D_B = 1024
S5_GROUP = 16
G_B = 64
P_S5 = 64
N_GROUPS = 4
EXP_PER_GROUP = 8
N_EXPERTS = 32
D_EXPERT = 512
EPS = 1e-6

LANES = 128
SUBLANES = 8
VMEM_LIMIT = 56 * 1024 * 1024

S5_CHUNK = 8
S5_CST = (LANES // S5_GROUP) * P_S5
MOE_TILE = 256


def _cparams(sem):
    return pltpu.CompilerParams(dimension_semantics=sem, vmem_limit_bytes=VMEM_LIMIT)


def _silu(x):
    return x * (1.0 / (1.0 + jnp.exp(-x)))


def _sigmoid(x):
    return 1.0 / (1.0 + jnp.exp(-x))


def _log_sigmoid(x):
    return jnp.minimum(x, 0.0) - jnp.log1p(jnp.exp(-jnp.abs(x)))


def _gelu_tanh(x):
    c = math.sqrt(2.0 / math.pi)
    return 0.5 * x * (1.0 + jnp.tanh(c * (x + 0.044715 * (x * x * x))))


def _split3(x):
    hi = x.astype(BF16)
    r = x - hi.astype(F32)
    mid = r.astype(BF16)
    lo = (r - mid.astype(F32)).astype(BF16)
    return hi, mid, lo


def _dot(a, b):
    return jnp.dot(a, b, preferred_element_type=F32)


def _dot_nt(a, b):
    return lax.dot_general(a, b, (((1,), (1,)), ((), ())), preferred_element_type=F32)


def _dot_tn(a, b):
    return lax.dot_general(a, b, (((0,), (0,)), ((), ())), preferred_element_type=F32)


def _rmsnorm_kernel(x_ref, g_ref, wg_ref, bg_ref, bgt_ref, *rest):
    n_out = 3
    o_ref, gates_ref, gates_t_ref = rest[-n_out:]
    x = x_ref[...]
    r = lax.rsqrt(jnp.mean(x * x, axis=-1, keepdims=True) + EPS)
    xn = (x * r * g_ref[...]).astype(o_ref.dtype)
    o_ref[...] = xn
    wg = wg_ref[...].astype(BF16)
    gates_ref[...] = _dot_nt(xn, wg) + bg_ref[...]
    gt = _dot_nt(wg, xn) + bgt_ref[...]
    gates_t_ref[...] = gt[0:gates_t_ref.shape[0], :]


def _rmsnorm_rows(x, g, w_t, gate_row0, b_gates, t_all, row0, tm, alias=None):
    n = x.shape[0]
    blk0 = row0 // tm
    in_specs = [pl.BlockSpec((tm, D_MODEL), lambda i: (i, 0)),
                pl.BlockSpec((1, D_MODEL), lambda i: (0, 0)),
                pl.BlockSpec((pl.Element(LANES), pl.Element(D_MODEL)), lambda i: (gate_row0, 0)),
                pl.BlockSpec((1, LANES), lambda i: (0, 0)),
                pl.BlockSpec((LANES, 1), lambda i: (0, 0))]
    args = [x, g.reshape(1, D_MODEL), w_t, b_gates.reshape(1, LANES), b_gates.reshape(LANES, 1)]
    aliases = {}
    if alias is not None:
        for a in alias:
            aliases[len(args)] = len(aliases)
            in_specs.append(pl.BlockSpec(memory_space=pl.ANY))
            args.append(a)
    return pl.pallas_call(
        _rmsnorm_kernel,
        out_shape=(jax.ShapeDtypeStruct((t_all, D_MODEL), BF16),
                   jax.ShapeDtypeStruct((t_all, LANES), F32),
                   jax.ShapeDtypeStruct((2 * H_A, n), F32)),
        grid=(n // tm,),
        in_specs=in_specs,
        out_specs=(pl.BlockSpec((tm, D_MODEL), lambda i: (i + blk0, 0)),
                   pl.BlockSpec((tm, LANES), lambda i: (i + blk0, 0)),
                   pl.BlockSpec((2 * H_A, tm), lambda i: (0, i))),
        input_output_aliases=aliases,
        compiler_params=_cparams(("arbitrary",)),
        name="rmsnorm_rows",
    )(*args)


def _mm_t_kernel(starts_ref, a_ref, wt_ref, b_ref, o_ref, wb_ref):
    del starts_ref

    @pl.when(pl.program_id(1) == 0)
    def _():
        wb_ref[...] = wt_ref[...].astype(BF16)

    o_ref[...] = (_dot_nt(a_ref[...], wb_ref[...]) + b_ref[...]).astype(o_ref.dtype)


def _matmul_t(a, w_t, row_starts, bias, tm, tn, out_dtype):
    m, k = a.shape
    n_t = len(row_starts)
    assert all(s % SUBLANES == 0 for s in row_starts)
    grid_spec = pltpu.PrefetchScalarGridSpec(
        num_scalar_prefetch=1,
        grid=(n_t, m // tm),
        in_specs=[pl.BlockSpec((tm, k), lambda j, i, st: (i, 0)),
                  pl.BlockSpec((pl.Element(tn), pl.Element(k)), lambda j, i, st: (st[j] * SUBLANES, 0)),
                  pl.BlockSpec((1, tn), lambda j, i, st: (0, j))],
        out_specs=pl.BlockSpec((tm, tn), lambda j, i, st: (i, j)),
        scratch_shapes=[pltpu.VMEM((tn, k), BF16)],
    )
    return pl.pallas_call(
        _mm_t_kernel,
        out_shape=jax.ShapeDtypeStruct((m, n_t * tn), out_dtype),
        grid_spec=grid_spec,
        compiler_params=_cparams(("arbitrary", "arbitrary")),
        name="rows_matmul_t",
    )(jnp.asarray([s // SUBLANES for s in row_starts], jnp.int32), a, w_t, bias)


def _sample_c_update(q_ref, kw_ref, v_ref, a_ref, c_ref, c_out, qc_out):
    rows = 2 * SUBLANES
    rid = lax.broadcasted_iota(jnp.int32, (rows, DK), 0)
    pad = jnp.zeros((rows - H_A, DK), F32)
    for b in range(q_ref.shape[0]):
        q16 = jnp.concatenate([q_ref[b], pad], axis=0).astype(BF16)
        kw16 = jnp.concatenate([kw_ref[b], pad], axis=0)
        v16 = jnp.concatenate([v_ref[b], pad], axis=0).astype(BF16)
        qc_rows = []
        for h in range(H_A):
            c_prev = c_ref[b, h]
            qc_rows.append(_dot(q16, c_prev.astype(BF16))[h:h + 1, :])
            kw_h = jnp.where(rid == h, kw16, 0.0).astype(BF16)
            c_out[b, h] = a_ref[b, h:h + 1, :] * c_prev + _dot_tn(kw_h, v16)
        qc_out[b] = jnp.concatenate(qc_rows, axis=0)


def _mlstm_kernel(qk_ref, v_ref, o_ref, gcol_ref, grow_ref, wc_ref, bc_ref, hg_ref, wpa_ref,
                  sq_ref, skw_ref, sv_ref, sa_ref, sc_ref,
                  ya_ref, c_out, n_out, m_out, sc_out, sqc_out, c_sc, n_sc, m_sc, ext_sc):
    c = pl.program_id(1)
    _sample_c_update(sq_ref, skw_ref, sv_ref, sa_ref, sc_ref, sc_out, sqc_out)
    L = CHUNK
    pad = SUBLANES
    d_a = H_A * DK

    @pl.when(c == 0)
    def _():
        c_sc[...] = jnp.zeros_like(c_sc)
        n_sc[...] = jnp.zeros_like(n_sc)
        m_sc[...] = jnp.zeros_like(m_sc)
        ext_sc[0:pad, :] = jnp.zeros((pad, 2 * d_a), F32)

    x = qk_ref[...]
    ext_sc[pad:pad + L, :] = x
    y = bc_ref[...] + wc_ref[CONV_W - 1:CONV_W, :] * x
    for j in range(1, CONV_W):
        y = y + wc_ref[CONV_W - 1 - j:CONV_W - j, :] * ext_sc[pad - j:pad - j + L, :]
    ext_sc[0:pad, :] = x[L - pad:L, :]
    y = _silu(y)

    gcol = gcol_ref[...]
    grow = grow_ref[...]
    ri = lax.broadcasted_iota(jnp.int32, (L, L), 0)
    ci = lax.broadcasted_iota(jnp.int32, (L, L), 1)
    causal = ci <= ri
    tril = jnp.where(causal, 1.0, 0.0).astype(BF16)
    triu = jnp.where(ri <= ci, 1.0, 0.0).astype(BF16)
    b_cols = sum(_dot(tril, p) for p in _split3(_log_sigmoid(gcol)))
    b_rows = sum(_dot(p, triu) for p in _split3(_log_sigmoid(grow)))

    ya = None
    for h in range(H_A):
        sl = slice(h * DK, (h + 1) * DK)
        q = y[:, sl] * (DK ** -0.5)
        k = y[:, d_a + h * DK:d_a + (h + 1) * DK]
        ig_col = gcol[:, h:h + 1]
        ig_row = grow[h:h + 1, :]
        b_col = b_cols[:, H_A + h:H_A + h + 1]
        b_row = b_rows[H_A + h:H_A + h + 1, :]

        m_prev = m_sc[:, h:h + 1]
        d_log = jnp.where(causal, b_col - b_row + ig_row, -jnp.inf)
        inter_log = b_col + m_prev
        m_t = jnp.maximum(inter_log, jnp.max(d_log, axis=1, keepdims=True))
        qb = q.astype(BF16)
        kb = k.astype(BF16)
        vb = v_ref[:, sl].astype(BF16)
        s = _dot_nt(qb, kb) * jnp.exp(d_log - m_t)
        inter_w = jnp.exp(inter_log - m_t)
        c_prev = c_sc[h]
        n_prev = n_sc[h:h + 1, :]
        num = _dot(s.astype(BF16), vb) + inter_w * _dot(qb, c_prev.astype(BF16))
        nq = jnp.sum(s, axis=1, keepdims=True) + inter_w * jnp.sum(q * n_prev, axis=1, keepdims=True)
        den = jnp.maximum(jnp.abs(nq), jnp.exp(-m_t))
        hh = num / den
        hh = hh * _sigmoid(o_ref[:, sl])
        hh = hh * lax.rsqrt(jnp.mean(hh * hh, axis=1, keepdims=True) + EPS)
        part = _dot((hh * hg_ref[:, sl]).astype(BF16), wpa_ref[sl, :])
        ya = part if ya is None else ya + part

        m_new = m_t[L - 1:L, :]
        b_last = b_col[L - 1:L, :]
        decay = jnp.exp(b_last + m_prev - m_new)
        w_end = jnp.exp(b_last - b_col + ig_col - m_new)
        kw = k * w_end
        c_sc[h] = decay * c_prev + _dot_tn(kw.astype(BF16), vb)
        n_sc[h:h + 1, :] = decay * n_prev + jnp.sum(kw, axis=0, keepdims=True)
        m_sc[:, h:h + 1] = m_new
    ya_ref[...] = ya.astype(ya_ref.dtype)

    @pl.when(c == pl.num_programs(1) - 1)
    def _():
        c_out[...] = c_sc[...]
        n_out[...] = n_sc[...]
        m_out[...] = m_sc[...]


def _mlstm_prompt(qkvo, gates, gates_t, w_conv, b_conv, head_g, w_pa, sample, batch, seq, t_all):
    nc = seq // CHUNK
    L = CHUNK
    d_a = H_A * DK
    dec = sample[0].shape[0]
    nb = dec // (batch * nc)
    assert nb * batch * nc == dec
    row = lambda b, c: b * nc + c
    svec = pl.BlockSpec((nb, H_A, DK), lambda b, c: (row(b, c), 0, 0))
    smat = pl.BlockSpec((nb, H_A, DK, DK), lambda b, c: (row(b, c), 0, 0, 0))
    in_specs = [
        pl.BlockSpec((L, 2 * d_a), lambda b, c: (row(b, c), 0)),
        pl.BlockSpec((L, d_a), lambda b, c: (row(b, c), 2)),
        pl.BlockSpec((L, d_a), lambda b, c: (row(b, c), 3)),
        pl.BlockSpec((L, LANES), lambda b, c: (row(b, c), 0)),
        pl.BlockSpec((2 * H_A, L), lambda b, c: (0, row(b, c))),
        pl.BlockSpec((CONV_W, 2 * d_a), lambda b, c: (0, 0)),
        pl.BlockSpec((1, 2 * d_a), lambda b, c: (0, 0)),
        pl.BlockSpec((1, d_a), lambda b, c: (0, 0)),
        pl.BlockSpec(w_pa.shape, lambda b, c: (0, 0)),
        svec, svec, svec, svec, smat,
    ]
    out_shape = (
        jax.ShapeDtypeStruct((t_all, w_pa.shape[1]), BF16),
        jax.ShapeDtypeStruct((batch, H_A, DK, DK), F32),
        jax.ShapeDtypeStruct((batch, H_A, DK), F32),
        jax.ShapeDtypeStruct((batch, 1, LANES), F32),
        jax.ShapeDtypeStruct((dec, H_A, DK, DK), F32),
        jax.ShapeDtypeStruct((dec, H_A, DK), F32),
    )
    out_specs = (
        pl.BlockSpec((L, d_a), lambda b, c: (row(b, c), 0)),
        pl.BlockSpec((None, H_A, DK, DK), lambda b, c: (b, 0, 0, 0)),
        pl.BlockSpec((None, H_A, DK), lambda b, c: (b, 0, 0)),
        pl.BlockSpec((None, 1, LANES), lambda b, c: (b, 0, 0)),
        smat, svec,
    )
    return pl.pallas_call(
        _mlstm_kernel,
        out_shape=out_shape,
        grid=(batch, nc),
        in_specs=in_specs,
        out_specs=out_specs,
        scratch_shapes=[pltpu.VMEM((H_A, DK, DK), F32), pltpu.VMEM((H_A, DK), F32),
                        pltpu.VMEM((1, LANES), F32), pltpu.VMEM((SUBLANES + L, 2 * d_a), F32)],
        compiler_params=_cparams(("arbitrary", "arbitrary")),
        name="mlstm_prompt",
    )(qkvo, qkvo, qkvo, gates, gates_t, w_conv, b_conv, head_g, w_pa, *sample)


def _lane_tile(x, reps):
    w = x.shape[1]
    ri = lax.broadcasted_iota(jnp.int32, (w, w * reps), 0)
    ci = lax.broadcasted_iota(jnp.int32, (w, w * reps), 1)
    rep = jnp.where(jnp.bitwise_and(ci, w - 1) == ri, 1.0, 0.0).astype(BF16)
    return _dot(x.astype(BF16), rep)


def _group_mask(rows, cols, row_per, col_per):
    ri = lax.broadcasted_iota(jnp.int32, (rows, cols), 0)
    ci = lax.broadcasted_iota(jnp.int32, (rows, cols), 1)
    return (jnp.right_shift(ri, int(math.log2(row_per))) == jnp.right_shift(ci, int(math.log2(col_per))))


def _s5_build_weights(bin_r, bin_i, kdt, aout_r, aout_i, winc_ref, wintra_ref, wout_ref):
    L = S5_CHUNK
    gpb = LANES // S5_GROUP
    m_inc = _group_mask(LANES, S5_CST, S5_GROUP, P_S5)
    m_lag = _group_mask(LANES, LANES, S5_GROUP, S5_GROUP)
    zero = jnp.zeros((LANES, LANES), BF16)
    lag = [jnp.where(m_lag, _lane_tile(kdt[d], gpb), 0.0).astype(BF16) for d in range(L)]
    for t in range(L):
        rows = slice(t * LANES, (t + 1) * LANES)
        d = L - 1 - t
        winc_ref[rows, 0:S5_CST] = jnp.where(m_inc, _lane_tile(bin_r[d], gpb), 0.0).astype(BF16)
        winc_ref[rows, S5_CST:2 * S5_CST] = jnp.where(m_inc, _lane_tile(bin_i[d], gpb), 0.0).astype(BF16)
        wout_ref[rows, 0:S5_CST] = jnp.where(m_inc, _lane_tile(aout_r[t], gpb), 0.0).astype(BF16)
        wout_ref[rows, S5_CST:2 * S5_CST] = jnp.where(m_inc, _lane_tile(aout_i[t], gpb), 0.0).astype(BF16)
        for t2 in range(L):
            wintra_ref[rows, t2 * LANES:(t2 + 1) * LANES] = lag[t2 - t] if t2 >= t else zero


def _s5_prompt_kernel(u_ref, binr_ref, bini_ref, kdt_ref, aoutr_ref, aouti_ref, lbr_ref, lbi_ref, d_ref,
                      ys_ref, sre_ref, sim_ref, x_sc, winc_ref, wintra_ref, wout_ref, *, batch, nchunk):
    L = S5_CHUNK
    nrow = batch * nchunk
    nst = S5_CST // LANES
    rstr = nchunk + SUBLANES
    _s5_build_weights(binr_ref, bini_ref, kdt_ref, aoutr_ref, aouti_ref, winc_ref, wintra_ref, wout_ref)
    u_t = [u_ref[pl.ds(t, nrow, stride=L), :] for t in range(L)]
    lhs = jnp.concatenate([a.astype(BF16) for a in u_t], axis=1)
    inc = _dot(lhs, winc_ref[...])
    for j in range(2 * nst):
        for b in range(batch):
            x_sc[j, b * rstr:b * rstr + nchunk, :] = inc[b * nchunk:(b + 1) * nchunk, j * LANES:(j + 1) * LANES]

    lbr = [jnp.broadcast_to(lbr_ref[:, j * LANES:(j + 1) * LANES], (batch, LANES)) for j in range(nst)]
    lbi = [jnp.broadcast_to(lbi_ref[:, j * LANES:(j + 1) * LANES], (batch, LANES)) for j in range(nst)]

    def scan_body(r, carry):
        rows = pl.ds(r, batch, stride=rstr)
        out = []
        for j in range(nst):
            xr, xi = carry[j]
            ir = x_sc[j, rows, :]
            ii = x_sc[nst + j, rows, :]
            x_sc[j, rows, :] = xr
            x_sc[nst + j, rows, :] = xi
            out.append((lbr[j] * xr - lbi[j] * xi + ir, lbr[j] * xi + lbi[j] * xr + ii))
        return tuple(out)

    z = jnp.zeros((batch, LANES), F32)
    fin = lax.fori_loop(0, nchunk, scan_body, tuple((z, z) for _ in range(nst)))
    for j in range(nst):
        sre_ref[:, j * LANES:(j + 1) * LANES] = fin[j][0]
        sim_ref[:, j * LANES:(j + 1) * LANES] = fin[j][1]

    xprev = jnp.concatenate(
        [jnp.concatenate([x_sc[j, b * rstr:b * rstr + nchunk, :] for b in range(batch)], axis=0)
         for j in range(2 * nst)], axis=1).astype(BF16)
    y = _dot(lhs, wintra_ref[...]) + _dot_nt(xprev, wout_ref[...])
    for t in range(L):
        yt = y[:, t * LANES:(t + 1) * LANES] + d_ref[...] * u_t[t]
        ys_ref[pl.ds(t, nrow, stride=L), :] = _gelu_tanh(yt).astype(ys_ref.dtype)


def _s5_prompt(proj2, u_col0, bin_r, bin_i, kdt, aout_r, aout_i, lb8r, lb8i, d_row, batch, seq, t_all):
    nchunk = seq // S5_CHUNK
    t_p = batch * seq
    kern = functools.partial(_s5_prompt_kernel, batch=batch, nchunk=nchunk)
    ub0 = u_col0 // LANES
    ncb = D_B // LANES
    kw = S5_CHUNK * LANES
    per_blk = lambda a: pl.BlockSpec((a.shape[0], None) + a.shape[2:], lambda g: (0, g, 0, 0))
    return pl.pallas_call(
        kern,
        out_shape=(jax.ShapeDtypeStruct((t_all, D_B), F32),
                   jax.ShapeDtypeStruct((batch, G_B * P_S5), F32),
                   jax.ShapeDtypeStruct((batch, G_B * P_S5), F32)),
        grid=(ncb,),
        in_specs=[pl.BlockSpec((t_p, LANES), lambda g: (0, ub0 + g)),
                  per_blk(bin_r), per_blk(bin_i), per_blk(kdt), per_blk(aout_r), per_blk(aout_i),
                  pl.BlockSpec((None, 1, S5_CST), lambda g: (g, 0, 0)),
                  pl.BlockSpec((None, 1, S5_CST), lambda g: (g, 0, 0)),
                  pl.BlockSpec((1, LANES), lambda g: (0, g))],
        out_specs=(pl.BlockSpec((t_p, LANES), lambda g: (0, g)),
                   pl.BlockSpec((batch, S5_CST), lambda g: (0, g)),
                   pl.BlockSpec((batch, S5_CST), lambda g: (0, g))),
        scratch_shapes=[pltpu.VMEM((2 * S5_CST // LANES, batch * (nchunk + SUBLANES), LANES), F32),
                        pltpu.VMEM((kw, 2 * S5_CST), BF16), pltpu.VMEM((kw, kw), BF16),
                        pltpu.VMEM((kw, 2 * S5_CST), BF16)],
        compiler_params=_cparams(("arbitrary",)),
        name="s5_prompt",
    )(proj2, bin_r, bin_i, kdt, aout_r, aout_i, lb8r, lb8i, d_row)


def _sample_pre_kernel(qk_ref, conv_ref, wc_ref, bc_ref, g_ref, m_ref, n_ref,
                       q_out, kw_out, a_out, s_out, den_out, conv_out, n_out, m_out):
    c2 = 2 * H_A * DK
    x_new = qk_ref[...]
    y = bc_ref[...] + wc_ref[CONV_W - 1:CONV_W, :] * x_new
    for j in range(CONV_W - 1):
        y = y + wc_ref[j:j + 1, :] * conv_ref[:, j * c2:(j + 1) * c2]
    y = _silu(y)
    conv_out[:, 0:(CONV_W - 2) * c2] = conv_ref[:, c2:(CONV_W - 1) * c2]
    conv_out[:, (CONV_W - 2) * c2:(CONV_W - 1) * c2] = x_new
    g = g_ref[...]
    bd = x_new.shape[0]
    m_cols = []
    for h in range(H_A):
        sl = slice(h * DK, (h + 1) * DK)
        q = y[:, sl] * (DK ** -0.5)
        k = y[:, H_A * DK + h * DK:H_A * DK + (h + 1) * DK]
        ig = g[:, h:h + 1]
        lf = _log_sigmoid(g[:, H_A + h:H_A + h + 1])
        m_prev = m_ref[:, h:h + 1]
        m_t = jnp.maximum(lf + m_prev, ig)
        a = jnp.exp(lf + m_prev - m_t)
        wgt = jnp.exp(ig - m_t)
        n_prev = n_ref[:, sl]
        s = jnp.sum(q * k, axis=1, keepdims=True) * wgt
        nq = s + a * jnp.sum(q * n_prev, axis=1, keepdims=True)
        den = jnp.maximum(jnp.abs(nq), jnp.exp(-m_t))
        kw = wgt * k
        q_out[:, sl] = q
        kw_out[:, sl] = kw
        a_out[:, sl] = jnp.broadcast_to(a, (bd, DK))
        s_out[:, sl] = jnp.broadcast_to(s, (bd, DK))
        den_out[:, sl] = jnp.broadcast_to(den, (bd, DK))
        n_out[:, sl] = a * n_prev + kw
        m_cols.append(m_t)
    lane = lax.broadcasted_iota(jnp.int32, (bd, LANES), 1)
    m_full = jnp.zeros((bd, LANES), F32)
    for h in range(H_A):
        m_full = jnp.where(lane == h, m_cols[h], m_full)
    m_out[...] = m_full


def _sample_pre(qkvo, conv_state, w_conv, b_conv, gates, m_state, n_state, dec, row_blk):
    c2 = 2 * H_A * DK
    d = H_A * DK
    full = lambda shape: pl.BlockSpec(shape, lambda i: (0,) * len(shape))
    rows = lambda: jax.ShapeDtypeStruct((dec, d), F32)
    return pl.pallas_call(
        _sample_pre_kernel,
        out_shape=(rows(), rows(), rows(), rows(), rows(),
                   jax.ShapeDtypeStruct((dec, (CONV_W - 1) * c2), F32), rows(),
                   jax.ShapeDtypeStruct((dec, LANES), F32)),
        grid=(1,),
        in_specs=[pl.BlockSpec((dec, c2), lambda i: (row_blk, 0)),
                  full((dec, (CONV_W - 1) * c2)), full((CONV_W, c2)), full((1, c2)),
                  pl.BlockSpec((dec, LANES), lambda i: (row_blk, 0)),
                  full((dec, H_A)), full((dec, d))],
        out_specs=(full((dec, d)), full((dec, d)), full((dec, d)), full((dec, d)), full((dec, d)),
                   full((dec, (CONV_W - 1) * c2)), full((dec, d)), full((dec, LANES))),
        compiler_params=_cparams(("arbitrary",)),
        name="sample_pre",
    )(qkvo, conv_state, w_conv, b_conv, gates, m_state, n_state)


def _sample_post_kernel(qc_ref, s_ref, a_ref, den_ref, vo_ref, hg_ref, wpa_ref, ya_in, ya_ref):
    del ya_in
    d = H_A * DK
    num = s_ref[...] * vo_ref[:, 0:d] + a_ref[...] * qc_ref[...]
    hh = num / den_ref[...]
    hh = hh * _sigmoid(vo_ref[:, d:2 * d])
    segs = []
    for h in range(H_A):
        sl = slice(h * DK, (h + 1) * DK)
        seg = hh[:, sl]
        seg = seg * lax.rsqrt(jnp.mean(seg * seg, axis=1, keepdims=True) + EPS)
        segs.append((seg * hg_ref[:, sl]).astype(BF16))
    ya_ref[...] = _dot(jnp.concatenate(segs, axis=1), wpa_ref[...]).astype(ya_ref.dtype)


def _sample_post(qc, s_e, a_e, den_e, qkvo, head_g, w_pa, ya_all, dec, row_blk):
    d = H_A * DK
    full = lambda shape: pl.BlockSpec(shape, lambda i: (0,) * len(shape))
    return pl.pallas_call(
        _sample_post_kernel,
        out_shape=jax.ShapeDtypeStruct(ya_all.shape, ya_all.dtype),
        grid=(1,),
        in_specs=[full((dec, d)), full((dec, d)), full((dec, d)), full((dec, d)),
                  pl.BlockSpec((dec, 2 * d), lambda i: (row_blk, 1)),
                  full((1, d)), full(w_pa.shape), pl.BlockSpec(memory_space=pl.ANY)],
        out_specs=pl.BlockSpec((dec, ya_all.shape[1]), lambda i: (row_blk, 0)),
        input_output_aliases={7: 0},
        compiler_params=_cparams(("arbitrary",)),
        name="sample_post",
    )(qc, s_e, a_e, den_e, qkvo, head_g, w_pa, ya_all)


def _s5_sample_kernel(u_ref, sr_ref, si_ref, br_ref, bi_ref, cr_ref, ci_ref, lbr_ref, lbi_ref, d_ref, ys_in,
                      ys_ref, sre_out, sim_out):
    del ys_in
    gpb = LANES // S5_GROUP
    mask = _group_mask(LANES, S5_CST, S5_GROUP, P_S5)
    expand = lambda blk: jnp.where(mask, _lane_tile(blk, gpb), 0.0)
    for g in range(D_B // LANES):
        ch = slice(g * LANES, (g + 1) * LANES)
        sl = slice(g * S5_CST, (g + 1) * S5_CST)
        u = u_ref[:, ch]
        bmat = jnp.concatenate([expand(br_ref[g]), expand(bi_ref[g])], axis=1).astype(BF16)
        bu = _dot(u.astype(BF16), bmat)
        lbr = lbr_ref[g]
        lbi = lbi_ref[g]
        sr = sr_ref[:, sl]
        si = si_ref[:, sl]
        xr = lbr * sr - lbi * si + bu[:, 0:S5_CST]
        xi = lbr * si + lbi * sr + bu[:, S5_CST:2 * S5_CST]
        sre_out[:, sl] = xr
        sim_out[:, sl] = xi
        x = jnp.concatenate([xr, xi], axis=1).astype(BF16)
        cmat = jnp.concatenate([expand(cr_ref[g]), -expand(ci_ref[g])], axis=1).astype(BF16)
        y = _dot_nt(x, cmat) + d_ref[:, ch] * u
        ys_ref[:, ch] = _gelu_tanh(y).astype(ys_ref.dtype)


def _s5_sample(proj2, u_col0, s_re, s_im, b_r, b_i, c_r, c_i, lbr, lbi, d_row, ys_all, dec, row_blk):
    full = lambda shape: pl.BlockSpec(shape, lambda i: (0,) * len(shape))
    n_state = G_B * P_S5
    ub0 = u_col0 // D_B
    params = (b_r, b_i, c_r, c_i, lbr, lbi)
    return pl.pallas_call(
        _s5_sample_kernel,
        out_shape=(jax.ShapeDtypeStruct(ys_all.shape, ys_all.dtype),
                   jax.ShapeDtypeStruct((dec, n_state), F32),
                   jax.ShapeDtypeStruct((dec, n_state), F32)),
        grid=(1,),
        in_specs=[pl.BlockSpec((dec, D_B), lambda i: (row_blk, ub0)),
                  full((dec, n_state)), full((dec, n_state))]
                 + [full(p.shape) for p in params]
                 + [full((1, D_B)), pl.BlockSpec(memory_space=pl.ANY)],
        out_specs=(pl.BlockSpec((dec, D_B), lambda i: (row_blk, 0)),
                   full((dec, n_state)), full((dec, n_state))),
        input_output_aliases={10: 0},
        compiler_params=_cparams(("arbitrary",)),
        name="s5_sample",
    )(proj2, s_re, s_im, *params, d_row, ys_all)


def _merge_kernel(x_ref, ya_ref, ys_ref, ga_ref, gb_ref, bglu_ref, gffn_ref, wrh_ref, wrl_ref, br_ref,
                  wglu_hbm, wpb_hbm, wout_hbm, *rest):
    n_alias = len(rest) - 8
    x1_ref, xn_ref, ids_ref, wts_ref = rest[n_alias:n_alias + 4]
    wglu, wpb, wout, sem = rest[n_alias + 4:]

    @pl.when(pl.program_id(0) == 0)
    def _():
        copies = [pltpu.make_async_copy(src, dst, sem.at[i])
                  for i, (src, dst) in enumerate(((wglu_hbm, wglu), (wpb_hbm, wpb), (wout_hbm, wout)))]
        for cp in copies:
            cp.start()
        for cp in copies:
            cp.wait()

    ya = ya_ref[...].astype(F32)
    ys = ys_ref[...]
    gate = _sigmoid(_dot(ys.astype(BF16), wglu[...]) + bglu_ref[...])
    yb = _dot((ys * gate).astype(BF16), wpb[...])
    z = _sigmoid(ga_ref[...]) * ya + _sigmoid(gb_ref[...]) * yb
    x1 = x_ref[...] + _dot(z.astype(BF16), wout[...])
    x1_ref[...] = x1
    xn = x1 * lax.rsqrt(jnp.mean(x1 * x1, axis=1, keepdims=True) + EPS) * gffn_ref[...]
    xn_ref[...] = xn
    xh = xn.astype(BF16)
    xl = (xn - xh.astype(F32)).astype(BF16)
    logits = _dot(xh, wrh_ref[...]) + _dot(xl, wrh_ref[...]) + _dot(xh, wrl_ref[...]) + br_ref[...]

    lane_i = lax.broadcasted_iota(jnp.int32, logits.shape, 1)
    lane = lane_i.astype(F32)
    neg = -jnp.inf
    big = float(1 << 20)
    gl = jnp.where(lane_i < N_GROUPS, logits, neg)
    gmax = jnp.max(gl, axis=1, keepdims=True)
    gsum = jnp.sum(jnp.exp(gl - gmax), axis=1, keepdims=True)
    gidx = jnp.min(jnp.where(gl == gmax, lane, big), axis=1, keepdims=True)
    pg_sel = 1.0 / gsum
    lo = N_GROUPS + gidx * EXP_PER_GROUP
    in_grp = (lane >= lo) & (lane < lo + EXP_PER_GROUP)
    el = jnp.where(in_grp, logits, neg)
    emax = jnp.max(el, axis=1, keepdims=True)
    ee = jnp.exp(el - emax)
    pe = ee / jnp.sum(ee, axis=1, keepdims=True)
    v0 = jnp.max(pe, axis=1, keepdims=True)
    i0 = jnp.min(jnp.where(in_grp & (pe == v0), lane, big), axis=1, keepdims=True)
    rest_m = in_grp & (lane != i0)
    pe1 = jnp.where(rest_m, pe, neg)
    v1 = jnp.max(pe1, axis=1, keepdims=True)
    i1 = jnp.min(jnp.where(rest_m & (pe1 == v1), lane, big), axis=1, keepdims=True)
    tot = v0 + v1
    w0 = pg_sel * (v0 / tot)
    w1 = pg_sel * (v1 / tot)
    ids = jnp.where(lane_i == 0, i0 - N_GROUPS, jnp.where(lane_i == 1, i1 - N_GROUPS, 0.0))
    ids_ref[...] = ids.astype(jnp.int32)
    wts_ref[...] = jnp.where(lane_i == 0, w0, jnp.where(lane_i == 1, w1, 0.0))


def _merge(x, ya_all, ys_all, proj2, b_glu, g_ffn, wr_hi, wr_lo, b_r, wglu, wpb, wout,
           t_all, row0, tm, aliases=None):
    n = x.shape[0]
    blk0 = row0 // tm
    const = lambda shape: pl.BlockSpec(shape, lambda i: (0,) * len(shape))
    any_spec = pl.BlockSpec(memory_space=pl.ANY)
    in_specs = [pl.BlockSpec((tm, D_MODEL), lambda i: (i, 0)),
                pl.BlockSpec((tm, D_MODEL), lambda i: (i + blk0, 0)),
                pl.BlockSpec((tm, D_B), lambda i: (i + blk0, 0)),
                pl.BlockSpec((tm, D_MODEL), lambda i: (i + blk0, 0)),
                pl.BlockSpec((tm, D_MODEL), lambda i: (i + blk0, 1)),
                const((1, D_B)), const((1, D_MODEL)),
                const((D_MODEL, LANES)), const((D_MODEL, LANES)), const((1, LANES)),
                any_spec, any_spec, any_spec]
    args = [x, ya_all, ys_all, proj2, proj2, b_glu, g_ffn, wr_hi, wr_lo, b_r, wglu, wpb, wout]
    io_alias = {}
    if aliases is not None:
        n_in = len(args)
        for j, a in enumerate(aliases):
            in_specs.append(any_spec)
            args.append(a)
            io_alias[n_in + j] = j
    out_shape = (jax.ShapeDtypeStruct((t_all, D_MODEL), F32),
                 jax.ShapeDtypeStruct((t_all, D_MODEL), F32),
                 jax.ShapeDtypeStruct((t_all, LANES), jnp.int32),
                 jax.ShapeDtypeStruct((t_all, LANES), F32))
    out_specs = (pl.BlockSpec((tm, D_MODEL), lambda i: (i + blk0, 0)),
                 pl.BlockSpec((tm, D_MODEL), lambda i: (i + blk0, 0)),
                 pl.BlockSpec((tm, LANES), lambda i: (i + blk0, 0)),
                 pl.BlockSpec((tm, LANES), lambda i: (i + blk0, 0)))
    return pl.pallas_call(
        _merge_kernel,
        out_shape=out_shape,
        grid=(n // tm,),
        in_specs=in_specs,
        out_specs=out_specs,
        scratch_shapes=[pltpu.VMEM(wglu.shape, BF16), pltpu.VMEM(wpb.shape, BF16),
                        pltpu.VMEM(wout.shape, BF16), pltpu.SemaphoreType.DMA((3,))],
        input_output_aliases=io_alias,
        compiler_params=_cparams(("arbitrary",)),
        name="merge_router",
    )(*args)


def _moe_kernel(nt_ref, first_ref, ord_ref, elist_ref, nord_ref, xs_ref, wg_hbm, wu_hbm, wd_hbm, o_ref,
                stg_g, stg_u, stg_d, wg_sc, wu_sc, wd_sc, sem):
    i = pl.program_id(0)
    n_ord = nord_ref[0]

    def weight_copies(k, slot):
        e = elist_ref[k]
        return (pltpu.make_async_copy(wg_hbm.at[e], stg_g.at[slot], sem.at[slot, 0]),
                pltpu.make_async_copy(wu_hbm.at[e], stg_u.at[slot], sem.at[slot, 1]),
                pltpu.make_async_copy(wd_hbm.at[e], stg_d.at[slot], sem.at[slot, 2]))

    @pl.when(i == 0)
    def _():
        for cp in weight_copies(0, 0):
            cp.start()

        @pl.when(n_ord > 1)
        def _():
            for cp in weight_copies(1, 1):
                cp.start()

    valid = i < nt_ref[0]
    k = ord_ref[i]

    @pl.when(valid & (first_ref[i] == 1))
    def _():
        slot = k % 2
        for cp in weight_copies(k, slot):
            cp.wait()
        wg_sc[...] = stg_g[slot].astype(BF16)
        wu_sc[...] = stg_u[slot].astype(BF16)
        wd_sc[...] = stg_d[slot].astype(BF16)

        @pl.when(k + 2 < n_ord)
        def _():
            for cp in weight_copies(k + 2, slot):
                cp.start()

    @pl.when(valid)
    def _():
        x = xs_ref[...].astype(BF16)
        hg = _dot(x, wg_sc[...])
        hu = _dot(x, wu_sc[...])
        hh = (_silu(hg) * hu).astype(BF16)
        o_ref[...] = _dot(hh, wd_sc[...])


def _moe_experts(n_tiles, first, ordinal, elist, n_ord, xs, w_gate, w_up, w_down):
    p_rows = xs.shape[0]
    last = lambda i, nt, *_: jnp.minimum(i, nt[0] - 1)
    any_spec = pl.BlockSpec(memory_space=pl.ANY)
    grid_spec = pltpu.PrefetchScalarGridSpec(
        num_scalar_prefetch=5,
        grid=(p_rows // MOE_TILE,),
        in_specs=[pl.BlockSpec((MOE_TILE, D_MODEL), lambda i, *s: (last(i, *s), 0)),
                  any_spec, any_spec, any_spec],
        out_specs=pl.BlockSpec((MOE_TILE, D_MODEL), lambda i, *s: (last(i, *s), 0)),
        scratch_shapes=[pltpu.VMEM((2, D_MODEL, D_EXPERT), F32), pltpu.VMEM((2, D_MODEL, D_EXPERT), F32),
                        pltpu.VMEM((2, D_EXPERT, D_MODEL), F32),
                        pltpu.VMEM((D_MODEL, D_EXPERT), BF16), pltpu.VMEM((D_MODEL, D_EXPERT), BF16),
                        pltpu.VMEM((D_EXPERT, D_MODEL), BF16),
                        pltpu.SemaphoreType.DMA((2, 3))],
    )
    return pl.pallas_call(
        _moe_kernel,
        out_shape=jax.ShapeDtypeStruct((p_rows, D_MODEL), F32),
        grid_spec=grid_spec,
        compiler_params=_cparams(("arbitrary",)),
        name="moe_experts",
    )(n_tiles, first, ordinal, elist, n_ord, xs, w_gate, w_up, w_down)


def _final_kernel(x1_ref, y0_ref, y1_ref, w_ref, g_ref, o_ref):
    w = w_ref[...]
    x2 = x1_ref[...] + w[:, 0:1] * y0_ref[...] + w[:, 1:2] * y1_ref[...]
    o_ref[...] = x2 * lax.rsqrt(jnp.mean(x2 * x2, axis=1, keepdims=True) + EPS) * g_ref[...]


def _final(x1_all, yg0, yg1, wts, g_final, row0, n, tm):
    blk0 = row0 // tm
    rows = pl.BlockSpec((tm, D_MODEL), lambda i: (i + blk0, 0))
    return pl.pallas_call(
        _final_kernel,
        out_shape=jax.ShapeDtypeStruct((n, D_MODEL), F32),
        grid=(n // tm,),
        in_specs=[rows, rows, rows,
                  pl.BlockSpec((tm, LANES), lambda i: (i + blk0, 0)),
                  pl.BlockSpec((1, D_MODEL), lambda i: (0, 0))],
        out_specs=pl.BlockSpec((tm, D_MODEL), lambda i: (i, 0)),
        compiler_params=_cparams(("arbitrary",)),
        name="combine_final_norm",
    )(x1_all, yg0, yg1, wts, g_final)


def _s5_discretise(a_re, a_im, log_step, b_re, b_im):
    dt = jnp.exp(log_step)[:, None]
    mag = jnp.exp(a_re * dt)
    lb_re = mag * jnp.cos(a_im * dt)
    lb_im = mag * jnp.sin(a_im * dt)
    den = a_re * a_re + a_im * a_im
    nr = lb_re - 1.0
    coef_re = (nr * a_re + lb_im * a_im) / den
    coef_im = (lb_im * a_re - nr * a_im) / den
    bb_re = coef_re[..., None] * b_re - coef_im[..., None] * b_im
    bb_im = coef_re[..., None] * b_im + coef_im[..., None] * b_re
    return lb_re, lb_im, bb_re, bb_im


def _s5_chunk_params(a_re, a_im, log_step, b_re, b_im, c_re, c_im):
    lb_re, lb_im, bb_re, bb_im = _s5_discretise(a_re, a_im, log_step, b_re, b_im)
    L = S5_CHUNK
    gpb = LANES // S5_GROUP
    ncb = G_B // gpb
    pr, pi = [jnp.ones_like(lb_re)], [jnp.zeros_like(lb_re)]
    for _ in range(L):
        pr, pi = pr + [pr[-1] * lb_re - pi[-1] * lb_im], pi + [pr[-1] * lb_im + pi[-1] * lb_re]
    pw_r, pw_i = jnp.stack(pr), jnp.stack(pi)
    bt_re = bb_re.transpose(0, 2, 1)
    bt_im = bb_im.transpose(0, 2, 1)
    lbb_r = pw_r[:L, :, None, :] * bt_re - pw_i[:L, :, None, :] * bt_im
    lbb_i = pw_r[:L, :, None, :] * bt_im + pw_i[:L, :, None, :] * bt_re
    hp = lax.Precision.HIGHEST
    kdt = (jnp.einsum('gop,dgcp->dgco', c_re, lbb_r, precision=hp)
           - jnp.einsum('gop,dgcp->dgco', c_im, lbb_i, precision=hp))
    a_r = c_re * pw_r[1:, :, None, :] - c_im * pw_i[1:, :, None, :]
    a_i = -(c_re * pw_i[1:, :, None, :] + c_im * pw_r[1:, :, None, :])

    def blocks(m):
        return m.reshape(m.shape[:-3] + (ncb, gpb * m.shape[-2], m.shape[-1]))

    vec = lambda v: v.reshape(ncb, 1, S5_CST)
    prompt = (blocks(lbb_r), blocks(lbb_i), blocks(kdt), blocks(a_r), blocks(a_i), vec(pw_r[L]), vec(pw_i[L]))
    sample = (blocks(lbb_r[0]), blocks(lbb_i[0]), blocks(c_re), blocks(c_im), vec(lb_re), vec(lb_im))
    return prompt, sample


def _dispatch(ids, t_all, p_rows):
    e = ids[:, :2].reshape(-1)
    onehot = (e[:, None] == jnp.arange(N_EXPERTS, dtype=jnp.int32)[None, :]).astype(jnp.int32)
    csum = jnp.cumsum(onehot, axis=0)
    rank = jnp.sum((csum - onehot) * onehot, axis=1)
    counts = csum[-1]
    tiles = (counts + MOE_TILE - 1) // MOE_TILE
    tile_end = jnp.cumsum(tiles)
    tile_start = tile_end - tiles
    pos = jnp.sum(onehot * (tile_start * MOE_TILE)[None, :], axis=1) + rank
    tok = jnp.arange(2 * t_all, dtype=jnp.int32) // 2
    src = (jnp.arange(p_rows, dtype=jnp.int32) % t_all).at[pos].set(tok)
    n_tiles = tile_end[-1]
    tidx = jnp.arange(p_rows // MOE_TILE, dtype=jnp.int32)
    tclamp = jnp.minimum(tidx, n_tiles - 1)
    tile_expert = jnp.sum((tile_end[None, :] <= tclamp[:, None]).astype(jnp.int32), axis=1)
    present = (tiles > 0).astype(jnp.int32)
    ord_of_e = jnp.cumsum(present) - 1
    eids = jnp.arange(N_EXPERTS, dtype=jnp.int32)
    elist = jnp.sum(jnp.where((ord_of_e[None, :] == eids[:, None]) & (present[None, :] == 1), eids[None, :], 0), axis=1)
    ordinal = jnp.sum(jnp.where(tile_expert[:, None] == eids[None, :], ord_of_e[None, :], 0), axis=1)
    first = jnp.concatenate([jnp.ones((1,), jnp.int32),
                             (tile_expert[1:] != tile_expert[:-1]).astype(jnp.int32)])
    one = lambda v: v.reshape(1).astype(jnp.int32)
    meta = (one(n_tiles), first, ordinal.astype(jnp.int32), elist.astype(jnp.int32), one(jnp.sum(present)))
    return pos.reshape(t_all, 2), src, meta


def _pick_tile(n, candidates):
    for c in candidates:
        if n % c == 0:
            return c
    raise ValueError(f"no row tile for {n}")


def kernel(x_prompt, x_sample, state_mlstm_C, state_mlstm_n, state_mlstm_m, state_conv, state_s5_re,
           state_s5_im, norm_mix_g, w_in, b_i, b_f, w_conv, b_conv, head_norm_g, w_pa, s5_a_re, s5_a_im,
           s5_log_step, s5_b_re, s5_b_im, s5_c_re, s5_c_im, s5_d, s5_w_glu, s5_b_glu, w_pb, w_out,
           norm_ffn_g, w_rg, b_rg, w_rexp, b_rexp, w_gate, w_up, w_down, norm_final_g):
    assert state_mlstm_C.shape[0] == 1 and x_sample.shape[1] == 1
    batch, seq, _ = x_prompt.shape
    dec = x_sample.shape[0]
    t_p = batch * seq
    t_all = t_p + dec
    assert seq % CHUNK == 0 and t_p % dec == 0 and dec % LANES == 0
    d_a = H_A * DK
    tm_p = _pick_tile(t_p, (512, 256, 128))
    tm_all = _pick_tile(t_all, (1664, 640, 384, 128))
    sample_blk = t_p // dec

    xp = x_prompt.reshape(t_p, D_MODEL)
    xs = x_sample.reshape(dec, D_MODEL)

    g_mix = norm_mix_g[0]
    w_in_t = w_in.reshape(w_in.shape[1:]).T
    n_qkvo = 4 * d_a
    tn = 1024
    n_gate_cols = 2 * H_A
    b_gates = jnp.pad(jnp.concatenate([b_i[0], b_f[0]]), (0, LANES - n_gate_cols))
    xn_all, gates, gates_t = _rmsnorm_rows(xp, g_mix, w_in_t, n_qkvo, b_gates, t_all, 0, tm_p)
    xn_all, gates, _ = _rmsnorm_rows(xs, g_mix, w_in_t, n_qkvo, b_gates, t_all, t_p, dec,
                                     alias=(xn_all, gates))
    qkvo = _matmul_t(xn_all, w_in_t, [j * tn for j in range(n_qkvo // tn)], jnp.zeros((1, n_qkvo), F32),
                     tm_all, tn, F32)
    c_u = n_qkvo + n_gate_cols
    c_ga = c_u + D_B
    starts = [c_ga + j * tn for j in range(2 * D_MODEL // tn)] + [c_u]
    proj2 = _matmul_t(xn_all, w_in_t, starts, jnp.zeros((1, len(starts) * tn), F32), tm_all, tn, F32)
    u_col0 = 2 * D_MODEL

    conv_s_in = state_conv[0].reshape(dec, (CONV_W - 1) * 2 * d_a)
    q_s, kw_s, a_e, s_e, den_e, conv_s, n_s, m_s = _sample_pre(
        qkvo, conv_s_in, w_conv[0], b_conv[0].reshape(1, -1), gates, state_mlstm_m[0],
        state_mlstm_n[0].reshape(dec, d_a), dec, sample_blk)
    v_s = qkvo[t_p:, 2 * d_a:3 * d_a]
    r3 = lambda a: a.reshape(dec, H_A, DK)

    head_g = head_norm_g[0].reshape(1, d_a)
    w_pa_b = w_pa[0].astype(BF16)
    ya_all, c_p, n_p, m_p, c_s, qc = _mlstm_prompt(
        qkvo, gates, gates_t, w_conv[0], b_conv[0].reshape(1, -1), head_g, w_pa_b,
        (r3(q_s), r3(kw_s), r3(v_s), r3(a_e), state_mlstm_C[0]), batch, seq, t_all)
    conv_p = jnp.stack([qkvo[b * seq + seq - (CONV_W - 1):(b + 1) * seq, :2 * d_a] for b in range(batch)])
    ya_all = _sample_post(qc.reshape(dec, d_a), s_e, a_e, den_e, qkvo, head_g, w_pa_b, ya_all, dec,
                          sample_blk)

    d_row = s5_d[0].reshape(1, D_B)
    s5_prompt_w, s5_sample_w = _s5_chunk_params(s5_a_re[0], s5_a_im[0], s5_log_step[0], s5_b_re[0],
                                                s5_b_im[0], s5_c_re[0], s5_c_im[0])
    ys_all, s5re_p, s5im_p = _s5_prompt(proj2, u_col0, *s5_prompt_w, d_row, batch, seq, t_all)

    ys_all, s5re_s, s5im_s = _s5_sample(proj2, u_col0, state_s5_re[0].reshape(dec, -1),
                                        state_s5_im[0].reshape(dec, -1),
                                        *s5_sample_w, d_row, ys_all, dec, sample_blk)

    wr = jnp.pad(jnp.concatenate([w_rg[0], w_rexp[0]], axis=1), ((0, 0), (0, LANES - N_GROUPS - N_EXPERTS)))
    wr_hi = wr.astype(BF16)
    wr_lo = (wr - wr_hi.astype(F32)).astype(BF16)
    b_r = jnp.pad(jnp.concatenate([b_rg[0], b_rexp[0]]), (0, LANES - N_GROUPS - N_EXPERTS)).reshape(1, LANES)
    merge_w = (s5_w_glu[0].astype(BF16), w_pb[0].astype(BF16), w_out[0].astype(BF16))
    b_glu = s5_b_glu[0].reshape(1, D_B)
    g_ffn = norm_ffn_g[0].reshape(1, D_MODEL)
    tm_m = _pick_tile(t_p, (256, 128))
    outs = _merge(xp, ya_all, ys_all, proj2, b_glu, g_ffn, wr_hi, wr_lo, b_r, *merge_w, t_all, 0, tm_m)
    x1_all, xn2_all, ids, wts = _merge(xs, ya_all, ys_all, proj2, b_glu, g_ffn, wr_hi, wr_lo, b_r,
                                       *merge_w, t_all, t_p, dec, aliases=outs)

    p_rows = -(-(2 * t_all + N_EXPERTS * (MOE_TILE - 1)) // MOE_TILE) * MOE_TILE
    pos, src, meta = _dispatch(ids, t_all, p_rows)
    take_rows = lambda a, idx: a.at[idx].get(mode='promise_in_bounds')
    xs_sorted = take_rows(xn2_all, src)
    yp = _moe_experts(*meta, xs_sorted, w_gate[0], w_up[0], w_down[0])
    yg0 = take_rows(yp, pos[:, 0])
    yg1 = take_rows(yp, pos[:, 1])

    g_fin = norm_final_g.reshape(1, D_MODEL)
    y_prompt = _final(x1_all, yg0, yg1, wts, g_fin, 0, t_p, tm_p).reshape(batch, seq, D_MODEL)
    y_sample = _final(x1_all, yg0, yg1, wts, g_fin, t_p, dec, dec).reshape(dec, 1, D_MODEL)

    lead = lambda a, shape: a.reshape((1,) + shape)
    return (y_prompt, y_sample,
            lead(c_p, (batch, H_A, DK, DK)), lead(n_p, (batch, H_A, DK)), lead(m_p[:, 0, :H_A], (batch, H_A)),
            lead(conv_p, (batch, CONV_W - 1, 2 * d_a)),
            lead(s5re_p, (batch, G_B, P_S5)), lead(s5im_p, (batch, G_B, P_S5)),
            lead(c_s, (dec, H_A, DK, DK)), lead(n_s, (dec, H_A, DK)), lead(m_s[:, :H_A], (dec, H_A)),
            lead(conv_s, (dec, CONV_W - 1, 2 * d_a)),
            lead(s5re_s, (dec, G_B, P_S5)), lead(s5im_s, (dec, G_B, P_S5)))
```

```python
import functools
import math

import jax
import jax.numpy as jnp
from jax import lax
from jax.experimental import pallas as pl
from jax.experimental.pallas import tpu as pltpu

F32 = jnp.float32
BF16 = jnp.bfloat16

D_MODEL = 2048
H_A = 8
DK = 256
CONV_W = 4
CHUNK = 128
D_B = 1024
S5_GROUP = 16
G_B = 64
P_S5 = 64
N_GROUPS = 4
EXP_PER_GROUP = 8
N_EXPERTS = 32
D_EXPERT = 512
EPS = 1e-6

LANES = 128
SUBLANES = 8
VMEM_LIMIT = 56 * 1024 * 1024

S5_CHUNK = 8
S5_CST = (LANES // S5_GROUP) * P_S5
MOE_TILE = 256


def _cparams(sem):
    return pltpu.CompilerParams(dimension_semantics=sem, vmem_limit_bytes=VMEM_LIMIT)


def _silu(x):
    return x * (1.0 / (1.0 + jnp.exp(-x)))


def _sigmoid(x):
    return 1.0 / (1.0 + jnp.exp(-x))


def _log_sigmoid(x):
    return jnp.minimum(x, 0.0) - jnp.log1p(jnp.exp(-jnp.abs(x)))


def _gelu_tanh(x):
    c = math.sqrt(2.0 / math.pi)
    return 0.5 * x * (1.0 + jnp.tanh(c * (x + 0.044715 * (x * x * x))))


def _split3(x):
    hi = x.astype(BF16)
    r = x - hi.astype(F32)
    mid = r.astype(BF16)
    lo = (r - mid.astype(F32)).astype(BF16)
    return hi, mid, lo


def _dot(a, b):
    return jnp.dot(a, b, preferred_element_type=F32)


def _dot_nt(a, b):
    return lax.dot_general(a, b, (((1,), (1,)), ((), ())), preferred_element_type=F32)


def _dot_tn(a, b):
    return lax.dot_general(a, b, (((0,), (0,)), ((), ())), preferred_element_type=F32)


def _rmsnorm_kernel(x_ref, g_ref, wg_ref, bg_ref, bgt_ref, *rest):
    n_out = 3
    o_ref, gates_ref, gates_t_ref = rest[-n_out:]
    x = x_ref[...]
    r = lax.rsqrt(jnp.mean(x * x, axis=-1, keepdims=True) + EPS)
    xn = (x * r * g_ref[...]).astype(o_ref.dtype)
    o_ref[...] = xn
    wg = wg_ref[...].astype(BF16)
    gates_ref[...] = _dot_nt(xn, wg) + bg_ref[...]
    gt = _dot_nt(wg, xn) + bgt_ref[...]
    gates_t_ref[...] = gt[0:gates_t_ref.shape[0], :]


def _rmsnorm_rows(x, g, w_t, gate_row0, b_gates, t_all, row0, tm, alias=None):
    n = x.shape[0]
    blk0 = row0 // tm
    in_specs = [pl.BlockSpec((tm, D_MODEL), lambda i: (i, 0)),
                pl.BlockSpec((1, D_MODEL), lambda i: (0, 0)),
                pl.BlockSpec((pl.Element(LANES), pl.Element(D_MODEL)), lambda i: (gate_row0, 0)),
                pl.BlockSpec((1, LANES), lambda i: (0, 0)),
                pl.BlockSpec((LANES, 1), lambda i: (0, 0))]
    args = [x, g.reshape(1, D_MODEL), w_t, b_gates.reshape(1, LANES), b_gates.reshape(LANES, 1)]
    aliases = {}
    if alias is not None:
        for a in alias:
            aliases[len(args)] = len(aliases)
            in_specs.append(pl.BlockSpec(memory_space=pl.ANY))
            args.append(a)
    return pl.pallas_call(
        _rmsnorm_kernel,
        out_shape=(jax.ShapeDtypeStruct((t_all, D_MODEL), BF16),
                   jax.ShapeDtypeStruct((t_all, LANES), F32),
                   jax.ShapeDtypeStruct((2 * H_A, n), F32)),
        grid=(n // tm,),
        in_specs=in_specs,
        out_specs=(pl.BlockSpec((tm, D_MODEL), lambda i: (i + blk0, 0)),
                   pl.BlockSpec((tm, LANES), lambda i: (i + blk0, 0)),
                   pl.BlockSpec((2 * H_A, tm), lambda i: (0, i))),
        input_output_aliases=aliases,
        compiler_params=_cparams(("arbitrary",)),
        name="rmsnorm_rows",
    )(*args)


def _mm_t_kernel(starts_ref, a_ref, wt_ref, b_ref, o_ref, wb_ref):
    del starts_ref

    @pl.when(pl.program_id(1) == 0)
    def _():
        wb_ref[...] = wt_ref[...].astype(BF16)

    o_ref[...] = (_dot_nt(a_ref[...], wb_ref[...]) + b_ref[...]).astype(o_ref.dtype)


def _matmul_t(a, w_t, row_starts, bias, tm, tn, out_dtype):
    m, k = a.shape
    n_t = len(row_starts)
    assert all(s % SUBLANES == 0 for s in row_starts)
    grid_spec = pltpu.PrefetchScalarGridSpec(
        num_scalar_prefetch=1,
        grid=(n_t, m // tm),
        in_specs=[pl.BlockSpec((tm, k), lambda j, i, st: (i, 0)),
                  pl.BlockSpec((pl.Element(tn), pl.Element(k)), lambda j, i, st: (st[j] * SUBLANES, 0)),
                  pl.BlockSpec((1, tn), lambda j, i, st: (0, j))],
        out_specs=pl.BlockSpec((tm, tn), lambda j, i, st: (i, j)),
        scratch_shapes=[pltpu.VMEM((tn, k), BF16)],
    )
    return pl.pallas_call(
        _mm_t_kernel,
        out_shape=jax.ShapeDtypeStruct((m, n_t * tn), out_dtype),
        grid_spec=grid_spec,
        compiler_params=_cparams(("arbitrary", "arbitrary")),
        name="rows_matmul_t",
    )(jnp.asarray([s // SUBLANES for s in row_starts], jnp.int32), a, w_t, bias)


def _sample_c_update(q_ref, kw_ref, v_ref, a_ref, c_ref, c_out, qc_out):
    rows = 2 * SUBLANES
    rid = lax.broadcasted_iota(jnp.int32, (rows, DK), 0)
    pad = jnp.zeros((rows - H_A, DK), F32)
    for b in range(q_ref.shape[0]):
        q16 = jnp.concatenate([q_ref[b], pad], axis=0).astype(BF16)
        kw16 = jnp.concatenate([kw_ref[b], pad], axis=0)
        v16 = jnp.concatenate([v_ref[b], pad], axis=0).astype(BF16)
        qc_rows = []
        for h in range(H_A):
            c_prev = c_ref[b, h]
            qc_rows.append(_dot(q16, c_prev.astype(BF16))[h:h + 1, :])
            kw_h = jnp.where(rid == h, kw16, 0.0).astype(BF16)
            c_out[b, h] = a_ref[b, h:h + 1, :] * c_prev + _dot_tn(kw_h, v16)
        qc_out[b] = jnp.concatenate(qc_rows, axis=0)


def _mlstm_kernel(qk_ref, v_ref, o_ref, gcol_ref, grow_ref, wc_ref, bc_ref, hg_ref,
                  sq_ref, skw_ref, sv_ref, sa_ref, sc_ref,
                  h_ref, c_out, n_out, m_out, sc_out, sqc_out, c_sc, n_sc, m_sc, ext_sc):
    c = pl.program_id(1)
    _sample_c_update(sq_ref, skw_ref, sv_ref, sa_ref, sc_ref, sc_out, sqc_out)
    L = CHUNK
    pad = SUBLANES
    d_a = H_A * DK

    @pl.when(c == 0)
    def _():
        c_sc[...] = jnp.zeros_like(c_sc)
        n_sc[...] = jnp.zeros_like(n_sc)
        m_sc[...] = jnp.zeros_like(m_sc)
        ext_sc[0:pad, :] = jnp.zeros((pad, 2 * d_a), F32)

    x = qk_ref[...]
    ext_sc[pad:pad + L, :] = x
    y = bc_ref[...] + wc_ref[CONV_W - 1:CONV_W, :] * x
    for j in range(1, CONV_W):
        y = y + wc_ref[CONV_W - 1 - j:CONV_W - j, :] * ext_sc[pad - j:pad - j + L, :]
    ext_sc[0:pad, :] = x[L - pad:L, :]
    y = _silu(y)

    gcol = gcol_ref[...]
    grow = grow_ref[...]
    ri = lax.broadcasted_iota(jnp.int32, (L, L), 0)
    ci = lax.broadcasted_iota(jnp.int32, (L, L), 1)
    causal = ci <= ri
    tril = jnp.where(causal, 1.0, 0.0).astype(BF16)
    triu = jnp.where(ri <= ci, 1.0, 0.0).astype(BF16)
    b_cols = sum(_dot(tril, p) for p in _split3(_log_sigmoid(gcol)))
    b_rows = sum(_dot(p, triu) for p in _split3(_log_sigmoid(grow)))

    for h in range(H_A):
        sl = slice(h * DK, (h + 1) * DK)
        q = y[:, sl] * (DK ** -0.5)
        k = y[:, d_a + h * DK:d_a + (h + 1) * DK]
        ig_col = gcol[:, h:h + 1]
        ig_row = grow[h:h + 1, :]
        b_col = b_cols[:, H_A + h:H_A + h + 1]
        b_row = b_rows[H_A + h:H_A + h + 1, :]

        m_prev = m_sc[:, h:h + 1]
        d_log = jnp.where(causal, b_col - b_row + ig_row, -jnp.inf)
        inter_log = b_col + m_prev
        m_t = jnp.maximum(inter_log, jnp.max(d_log, axis=1, keepdims=True))
        qb = q.astype(BF16)
        kb = k.astype(BF16)
        vb = v_ref[:, sl].astype(BF16)
        s = _dot_nt(qb, kb) * jnp.exp(d_log - m_t)
        inter_w = jnp.exp(inter_log - m_t)
        c_prev = c_sc[h]
        n_prev = n_sc[h:h + 1, :]
        num = _dot(s.astype(BF16), vb) + inter_w * _dot(qb, c_prev.astype(BF16))
        nq = jnp.sum(s, axis=1, keepdims=True) + inter_w * jnp.sum(q * n_prev, axis=1, keepdims=True)
        den = jnp.maximum(jnp.abs(nq), jnp.exp(-m_t))
        hh = num / den
        hh = hh * _sigmoid(o_ref[:, sl])
        hh = hh * lax.rsqrt(jnp.mean(hh * hh, axis=1, keepdims=True) + EPS)
        h_ref[:, sl] = (hh * hg_ref[:, sl]).astype(h_ref.dtype)

        m_new = m_t[L - 1:L, :]
        b_last = b_col[L - 1:L, :]
        decay = jnp.exp(b_last + m_prev - m_new)
        w_end = jnp.exp(b_last - b_col + ig_col - m_new)
        kw = k * w_end
        c_sc[h] = decay * c_prev + _dot_tn(kw.astype(BF16), vb)
        n_sc[h:h + 1, :] = decay * n_prev + jnp.sum(kw, axis=0, keepdims=True)
        m_sc[:, h:h + 1] = m_new

    @pl.when(c == pl.num_programs(1) - 1)
    def _():
        c_out[...] = c_sc[...]
        n_out[...] = n_sc[...]
        m_out[...] = m_sc[...]


def _mlstm_prompt(qkvo, gates, gates_t, w_conv, b_conv, head_g, sample, batch, seq, t_all):
    nc = seq // CHUNK
    L = CHUNK
    d_a = H_A * DK
    dec = sample[0].shape[0]
    nb = dec // (batch * nc)
    assert nb * batch * nc == dec
    row = lambda b, c: b * nc + c
    svec = pl.BlockSpec((nb, H_A, DK), lambda b, c: (row(b, c), 0, 0))
    smat = pl.BlockSpec((nb, H_A, DK, DK), lambda b, c: (row(b, c), 0, 0, 0))
    in_specs = [
        pl.BlockSpec((L, 2 * d_a), lambda b, c: (row(b, c), 0)),
        pl.BlockSpec((L, d_a), lambda b, c: (row(b, c), 2)),
        pl.BlockSpec((L, d_a), lambda b, c: (row(b, c), 3)),
        pl.BlockSpec((L, LANES), lambda b, c: (row(b, c), 0)),
        pl.BlockSpec((2 * H_A, L), lambda b, c: (0, row(b, c))),
        pl.BlockSpec((CONV_W, 2 * d_a), lambda b, c: (0, 0)),
        pl.BlockSpec((1, 2 * d_a), lambda b, c: (0, 0)),
        pl.BlockSpec((1, d_a), lambda b, c: (0, 0)),
        svec, svec, svec, svec, smat,
    ]
    out_shape = (
        jax.ShapeDtypeStruct((t_all, d_a), BF16),
        jax.ShapeDtypeStruct((batch, H_A, DK, DK), F32),
        jax.ShapeDtypeStruct((batch, H_A, DK), F32),
        jax.ShapeDtypeStruct((batch, 1, LANES), F32),
        jax.ShapeDtypeStruct((dec, H_A, DK, DK), F32),
        jax.ShapeDtypeStruct((dec, H_A, DK), F32),
    )
    out_specs = (
        pl.BlockSpec((L, d_a), lambda b, c: (row(b, c), 0)),
        pl.BlockSpec((None, H_A, DK, DK), lambda b, c: (b, 0, 0, 0)),
        pl.BlockSpec((None, H_A, DK), lambda b, c: (b, 0, 0)),
        pl.BlockSpec((None, 1, LANES), lambda b, c: (b, 0, 0)),
        smat, svec,
    )
    return pl.pallas_call(
        _mlstm_kernel,
        out_shape=out_shape,
        grid=(batch, nc),
        in_specs=in_specs,
        out_specs=out_specs,
        scratch_shapes=[pltpu.VMEM((H_A, DK, DK), F32), pltpu.VMEM((H_A, DK), F32),
                        pltpu.VMEM((1, LANES), F32), pltpu.VMEM((SUBLANES + L, 2 * d_a), F32)],
        compiler_params=_cparams(("arbitrary", "arbitrary")),
        name="mlstm_prompt",
    )(qkvo, qkvo, qkvo, gates, gates_t, w_conv, b_conv, head_g, *sample)


def _lane_tile(x, reps):
    w = x.shape[1]
    ri = lax.broadcasted_iota(jnp.int32, (w, w * reps), 0)
    ci = lax.broadcasted_iota(jnp.int32, (w, w * reps), 1)
    rep = jnp.where(jnp.bitwise_and(ci, w - 1) == ri, 1.0, 0.0).astype(BF16)
    return _dot(x.astype(BF16), rep)


def _group_mask(rows, cols, row_per, col_per):
    ri = lax.broadcasted_iota(jnp.int32, (rows, cols), 0)
    ci = lax.broadcasted_iota(jnp.int32, (rows, cols), 1)
    return (jnp.right_shift(ri, int(math.log2(row_per))) == jnp.right_shift(ci, int(math.log2(col_per))))


def _s5_build_weights(bin_r, bin_i, kdt, aout_r, aout_i, winc_ref, wintra_ref, wout_ref):
    L = S5_CHUNK
    gpb = LANES // S5_GROUP
    m_inc = _group_mask(LANES, S5_CST, S5_GROUP, P_S5)
    m_lag = _group_mask(LANES, LANES, S5_GROUP, S5_GROUP)
    zero = jnp.zeros((LANES, LANES), BF16)
    lag = [jnp.where(m_lag, _lane_tile(kdt[d], gpb), 0.0).astype(BF16) for d in range(L)]
    for t in range(L):
        rows = slice(t * LANES, (t + 1) * LANES)
        d = L - 1 - t
        winc_ref[rows, 0:S5_CST] = jnp.where(m_inc, _lane_tile(bin_r[d], gpb), 0.0).astype(BF16)
        winc_ref[rows, S5_CST:2 * S5_CST] = jnp.where(m_inc, _lane_tile(bin_i[d], gpb), 0.0).astype(BF16)
        wout_ref[rows, 0:S5_CST] = jnp.where(m_inc, _lane_tile(aout_r[t], gpb), 0.0).astype(BF16)
        wout_ref[rows, S5_CST:2 * S5_CST] = jnp.where(m_inc, _lane_tile(aout_i[t], gpb), 0.0).astype(BF16)
        for t2 in range(L):
            wintra_ref[rows, t2 * LANES:(t2 + 1) * LANES] = lag[t2 - t] if t2 >= t else zero


def _s5_prompt_kernel(u_ref, binr_ref, bini_ref, kdt_ref, aoutr_ref, aouti_ref, lbr_ref, lbi_ref, d_ref,
                      ys_ref, sre_ref, sim_ref, x_sc, winc_ref, wintra_ref, wout_ref, *, batch, nchunk):
    L = S5_CHUNK
    nrow = batch * nchunk
    nst = S5_CST // LANES
    rstr = nchunk + SUBLANES
    _s5_build_weights(binr_ref, bini_ref, kdt_ref, aoutr_ref, aouti_ref, winc_ref, wintra_ref, wout_ref)
    u_t = [u_ref[pl.ds(t, nrow, stride=L), :] for t in range(L)]
    lhs = jnp.concatenate([a.astype(BF16) for a in u_t], axis=1)
    inc = _dot(lhs, winc_ref[...])
    for j in range(2 * nst):
        for b in range(batch):
            x_sc[j, b * rstr:b * rstr + nchunk, :] = inc[b * nchunk:(b + 1) * nchunk, j * LANES:(j + 1) * LANES]

    lbr = [jnp.broadcast_to(lbr_ref[:, j * LANES:(j + 1) * LANES], (batch, LANES)) for j in range(nst)]
    lbi = [jnp.broadcast_to(lbi_ref[:, j * LANES:(j + 1) * LANES], (batch, LANES)) for j in range(nst)]

    def scan_body(r, carry):
        rows = pl.ds(r, batch, stride=rstr)
        out = []
        for j in range(nst):
            xr, xi = carry[j]
            ir = x_sc[j, rows, :]
            ii = x_sc[nst + j, rows, :]
            x_sc[j, rows, :] = xr
            x_sc[nst + j, rows, :] = xi
            out.append((lbr[j] * xr - lbi[j] * xi + ir, lbr[j] * xi + lbi[j] * xr + ii))
        return tuple(out)

    z = jnp.zeros((batch, LANES), F32)
    fin = lax.fori_loop(0, nchunk, scan_body, tuple((z, z) for _ in range(nst)))
    for j in range(nst):
        sre_ref[:, j * LANES:(j + 1) * LANES] = fin[j][0]
        sim_ref[:, j * LANES:(j + 1) * LANES] = fin[j][1]

    xprev = jnp.concatenate(
        [jnp.concatenate([x_sc[j, b * rstr:b * rstr + nchunk, :] for b in range(batch)], axis=0)
         for j in range(2 * nst)], axis=1).astype(BF16)
    y = _dot(lhs, wintra_ref[...]) + _dot_nt(xprev, wout_ref[...])
    for t in range(L):
        yt = y[:, t * LANES:(t + 1) * LANES] + d_ref[...] * u_t[t]
        ys_ref[pl.ds(t, nrow, stride=L), :] = _gelu_tanh(yt).astype(ys_ref.dtype)


def _s5_prompt(proj2, u_col0, bin_r, bin_i, kdt, aout_r, aout_i, lb8r, lb8i, d_row, batch, seq, t_all):
    nchunk = seq // S5_CHUNK
    t_p = batch * seq
    kern = functools.partial(_s5_prompt_kernel, batch=batch, nchunk=nchunk)
    ub0 = u_col0 // LANES
    ncb = D_B // LANES
    kw = S5_CHUNK * LANES
    per_blk = lambda a: pl.BlockSpec((a.shape[0], None) + a.shape[2:], lambda g: (0, g, 0, 0))
    return pl.pallas_call(
        kern,
        out_shape=(jax.ShapeDtypeStruct((t_all, D_B), F32),
                   jax.ShapeDtypeStruct((batch, G_B * P_S5), F32),
                   jax.ShapeDtypeStruct((batch, G_B * P_S5), F32)),
        grid=(ncb,),
        in_specs=[pl.BlockSpec((t_p, LANES), lambda g: (0, ub0 + g)),
                  per_blk(bin_r), per_blk(bin_i), per_blk(kdt), per_blk(aout_r), per_blk(aout_i),
                  pl.BlockSpec((None, 1, S5_CST), lambda g: (g, 0, 0)),
                  pl.BlockSpec((None, 1, S5_CST), lambda g: (g, 0, 0)),
                  pl.BlockSpec((1, LANES), lambda g: (0, g))],
        out_specs=(pl.BlockSpec((t_p, LANES), lambda g: (0, g)),
                   pl.BlockSpec((batch, S5_CST), lambda g: (0, g)),
                   pl.BlockSpec((batch, S5_CST), lambda g: (0, g))),
        scratch_shapes=[pltpu.VMEM((2 * S5_CST // LANES, batch * (nchunk + SUBLANES), LANES), F32),
                        pltpu.VMEM((kw, 2 * S5_CST), BF16), pltpu.VMEM((kw, kw), BF16),
                        pltpu.VMEM((kw, 2 * S5_CST), BF16)],
        compiler_params=_cparams(("arbitrary",)),
        name="s5_prompt",
    )(proj2, bin_r, bin_i, kdt, aout_r, aout_i, lb8r, lb8i, d_row)


def _sample_pre_kernel(qk_ref, conv_ref, wc_ref, bc_ref, g_ref, m_ref, n_ref,
                       q_out, kw_out, a_out, s_out, den_out, conv_out, n_out, m_out):
    c2 = 2 * H_A * DK
    x_new = qk_ref[...]
    y = bc_ref[...] + wc_ref[CONV_W - 1:CONV_W, :] * x_new
    for j in range(CONV_W - 1):
        y = y + wc_ref[j:j + 1, :] * conv_ref[:, j * c2:(j + 1) * c2]
    y = _silu(y)
    conv_out[:, 0:(CONV_W - 2) * c2] = conv_ref[:, c2:(CONV_W - 1) * c2]
    conv_out[:, (CONV_W - 2) * c2:(CONV_W - 1) * c2] = x_new
    g = g_ref[...]
    bd = x_new.shape[0]
    m_cols = []
    for h in range(H_A):
        sl = slice(h * DK, (h + 1) * DK)
        q = y[:, sl] * (DK ** -0.5)
        k = y[:, H_A * DK + h * DK:H_A * DK + (h + 1) * DK]
        ig = g[:, h:h + 1]
        lf = _log_sigmoid(g[:, H_A + h:H_A + h + 1])
        m_prev = m_ref[:, h:h + 1]
        m_t = jnp.maximum(lf + m_prev, ig)
        a = jnp.exp(lf + m_prev - m_t)
        wgt = jnp.exp(ig - m_t)
        n_prev = n_ref[:, sl]
        s = jnp.sum(q * k, axis=1, keepdims=True) * wgt
        nq = s + a * jnp.sum(q * n_prev, axis=1, keepdims=True)
        den = jnp.maximum(jnp.abs(nq), jnp.exp(-m_t))
        kw = wgt * k
        q_out[:, sl] = q
        kw_out[:, sl] = kw
        a_out[:, sl] = jnp.broadcast_to(a, (bd, DK))
        s_out[:, sl] = jnp.broadcast_to(s, (bd, DK))
        den_out[:, sl] = jnp.broadcast_to(den, (bd, DK))
        n_out[:, sl] = a * n_prev + kw
        m_cols.append(m_t)
    lane = lax.broadcasted_iota(jnp.int32, (bd, LANES), 1)
    m_full = jnp.zeros((bd, LANES), F32)
    for h in range(H_A):
        m_full = jnp.where(lane == h, m_cols[h], m_full)
    m_out[...] = m_full


def _sample_pre(qkvo, conv_state, w_conv, b_conv, gates, m_state, n_state, dec, row_blk):
    c2 = 2 * H_A * DK
    d = H_A * DK
    full = lambda shape: pl.BlockSpec(shape, lambda i: (0,) * len(shape))
    rows = lambda: jax.ShapeDtypeStruct((dec, d), F32)
    return pl.pallas_call(
        _sample_pre_kernel,
        out_shape=(rows(), rows(), rows(), rows(), rows(),
                   jax.ShapeDtypeStruct((dec, (CONV_W - 1) * c2), F32), rows(),
                   jax.ShapeDtypeStruct((dec, LANES), F32)),
        grid=(1,),
        in_specs=[pl.BlockSpec((dec, c2), lambda i: (row_blk, 0)),
                  full((dec, (CONV_W - 1) * c2)), full((CONV_W, c2)), full((1, c2)),
                  pl.BlockSpec((dec, LANES), lambda i: (row_blk, 0)),
                  full((dec, H_A)), full((dec, d))],
        out_specs=(full((dec, d)), full((dec, d)), full((dec, d)), full((dec, d)), full((dec, d)),
                   full((dec, (CONV_W - 1) * c2)), full((dec, d)), full((dec, LANES))),
        compiler_params=_cparams(("arbitrary",)),
        name="sample_pre",
    )(qkvo, conv_state, w_conv, b_conv, gates, m_state, n_state)


def _sample_post_kernel(qc_ref, s_ref, a_ref, den_ref, vo_ref, hg_ref, hn_in, h_ref):
    del hn_in
    d = H_A * DK
    num = s_ref[...] * vo_ref[:, 0:d] + a_ref[...] * qc_ref[...]
    hh = num / den_ref[...]
    hh = hh * _sigmoid(vo_ref[:, d:2 * d])
    for h in range(H_A):
        sl = slice(h * DK, (h + 1) * DK)
        seg = hh[:, sl]
        seg = seg * lax.rsqrt(jnp.mean(seg * seg, axis=1, keepdims=True) + EPS)
        h_ref[:, sl] = (seg * hg_ref[:, sl]).astype(h_ref.dtype)


def _sample_post(qc, s_e, a_e, den_e, qkvo, head_g, hn_all, dec, row_blk):
    d = H_A * DK
    full = lambda shape: pl.BlockSpec(shape, lambda i: (0,) * len(shape))
    return pl.pallas_call(
        _sample_post_kernel,
        out_shape=jax.ShapeDtypeStruct(hn_all.shape, hn_all.dtype),
        grid=(1,),
        in_specs=[full((dec, d)), full((dec, d)), full((dec, d)), full((dec, d)),
                  pl.BlockSpec((dec, 2 * d), lambda i: (row_blk, 1)),
                  full((1, d)), pl.BlockSpec(memory_space=pl.ANY)],
        out_specs=pl.BlockSpec((dec, d), lambda i: (row_blk, 0)),
        input_output_aliases={6: 0},
        compiler_params=_cparams(("arbitrary",)),
        name="sample_post",
    )(qc, s_e, a_e, den_e, qkvo, head_g, hn_all)


def _s5_sample_kernel(u_ref, sr_ref, si_ref, br_ref, bi_ref, cr_ref, ci_ref, lbr_ref, lbi_ref, d_ref, ys_in,
                      ys_ref, sre_out, sim_out):
    del ys_in
    gpb = LANES // S5_GROUP
    mask = _group_mask(LANES, S5_CST, S5_GROUP, P_S5)
    expand = lambda blk: jnp.where(mask, _lane_tile(blk, gpb), 0.0)
    for g in range(D_B // LANES):
        ch = slice(g * LANES, (g + 1) * LANES)
        sl = slice(g * S5_CST, (g + 1) * S5_CST)
        u = u_ref[:, ch]
        bmat = jnp.concatenate([expand(br_ref[g]), expand(bi_ref[g])], axis=1).astype(BF16)
        bu = _dot(u.astype(BF16), bmat)
        lbr = lbr_ref[g]
        lbi = lbi_ref[g]
        sr = sr_ref[:, sl]
        si = si_ref[:, sl]
        xr = lbr * sr - lbi * si + bu[:, 0:S5_CST]
        xi = lbr * si + lbi * sr + bu[:, S5_CST:2 * S5_CST]
        sre_out[:, sl] = xr
        sim_out[:, sl] = xi
        x = jnp.concatenate([xr, xi], axis=1).astype(BF16)
        cmat = jnp.concatenate([expand(cr_ref[g]), -expand(ci_ref[g])], axis=1).astype(BF16)
        y = _dot_nt(x, cmat) + d_ref[:, ch] * u
        ys_ref[:, ch] = _gelu_tanh(y).astype(ys_ref.dtype)


def _s5_sample(proj2, u_col0, s_re, s_im, b_r, b_i, c_r, c_i, lbr, lbi, d_row, ys_all, dec, row_blk):
    full = lambda shape: pl.BlockSpec(shape, lambda i: (0,) * len(shape))
    n_state = G_B * P_S5
    ub0 = u_col0 // D_B
    params = (b_r, b_i, c_r, c_i, lbr, lbi)
    return pl.pallas_call(
        _s5_sample_kernel,
        out_shape=(jax.ShapeDtypeStruct(ys_all.shape, ys_all.dtype),
                   jax.ShapeDtypeStruct((dec, n_state), F32),
                   jax.ShapeDtypeStruct((dec, n_state), F32)),
        grid=(1,),
        in_specs=[pl.BlockSpec((dec, D_B), lambda i: (row_blk, ub0)),
                  full((dec, n_state)), full((dec, n_state))]
                 + [full(p.shape) for p in params]
                 + [full((1, D_B)), pl.BlockSpec(memory_space=pl.ANY)],
        out_specs=(pl.BlockSpec((dec, D_B), lambda i: (row_blk, 0)),
                   full((dec, n_state)), full((dec, n_state))),
        input_output_aliases={10: 0},
        compiler_params=_cparams(("arbitrary",)),
        name="s5_sample",
    )(proj2, s_re, s_im, *params, d_row, ys_all)


def _merge_kernel(x_ref, hn_ref, ys_ref, ga_ref, gb_ref, bglu_ref, gffn_ref, wrh_ref, wrl_ref, br_ref,
                  wpa_hbm, wglu_hbm, wpb_hbm, wout_hbm, *rest):
    n_alias = len(rest) - 9
    x1_ref, xn_ref, ids_ref, wts_ref = rest[n_alias:n_alias + 4]
    wpa, wglu, wpb, wout, sem = rest[n_alias + 4:]

    @pl.when(pl.program_id(0) == 0)
    def _():
        copies = [pltpu.make_async_copy(src, dst, sem.at[i])
                  for i, (src, dst) in enumerate(((wpa_hbm, wpa), (wglu_hbm, wglu),
                                                  (wpb_hbm, wpb), (wout_hbm, wout)))]
        for cp in copies:
            cp.start()
        for cp in copies:
            cp.wait()

    ya = _dot(hn_ref[...], wpa[...])
    ys = ys_ref[...]
    gate = _sigmoid(_dot(ys.astype(BF16), wglu[...]) + bglu_ref[...])
    yb = _dot((ys * gate).astype(BF16), wpb[...])
    z = _sigmoid(ga_ref[...]) * ya + _sigmoid(gb_ref[...]) * yb
    x1 = x_ref[...] + _dot(z.astype(BF16), wout[...])
    x1_ref[...] = x1
    xn = x1 * lax.rsqrt(jnp.mean(x1 * x1, axis=1, keepdims=True) + EPS) * gffn_ref[...]
    xn_ref[...] = xn
    xh = xn.astype(BF16)
    xl = (xn - xh.astype(F32)).astype(BF16)
    logits = _dot(xh, wrh_ref[...]) + _dot(xl, wrh_ref[...]) + _dot(xh, wrl_ref[...]) + br_ref[...]

    lane_i = lax.broadcasted_iota(jnp.int32, logits.shape, 1)
    lane = lane_i.astype(F32)
    neg = -jnp.inf
    big = float(1 << 20)
    gl = jnp.where(lane_i < N_GROUPS, logits, neg)
    gmax = jnp.max(gl, axis=1, keepdims=True)
    gsum = jnp.sum(jnp.exp(gl - gmax), axis=1, keepdims=True)
    gidx = jnp.min(jnp.where(gl == gmax, lane, big), axis=1, keepdims=True)
    pg_sel = 1.0 / gsum
    lo = N_GROUPS + gidx * EXP_PER_GROUP
    in_grp = (lane >= lo) & (lane < lo + EXP_PER_GROUP)
    el = jnp.where(in_grp, logits, neg)
    emax = jnp.max(el, axis=1, keepdims=True)
    ee = jnp.exp(el - emax)
    pe = ee / jnp.sum(ee, axis=1, keepdims=True)
    v0 = jnp.max(pe, axis=1, keepdims=True)
    i0 = jnp.min(jnp.where(in_grp & (pe == v0), lane, big), axis=1, keepdims=True)
    rest_m = in_grp & (lane != i0)
    pe1 = jnp.where(rest_m, pe, neg)
    v1 = jnp.max(pe1, axis=1, keepdims=True)
    i1 = jnp.min(jnp.where(rest_m & (pe1 == v1), lane, big), axis=1, keepdims=True)
    tot = v0 + v1
    w0 = pg_sel * (v0 / tot)
    w1 = pg_sel * (v1 / tot)
    ids = jnp.where(lane_i == 0, i0 - N_GROUPS, jnp.where(lane_i == 1, i1 - N_GROUPS, 0.0))
    ids_ref[...] = ids.astype(jnp.int32)
    wts_ref[...] = jnp.where(lane_i == 0, w0, jnp.where(lane_i == 1, w1, 0.0))


def _merge(x, hn_all, ys_all, proj2, b_glu, g_ffn, wr_hi, wr_lo, b_r, wpa, wglu, wpb, wout,
           t_all, row0, tm, aliases=None):
    n = x.shape[0]
    blk0 = row0 // tm
    const = lambda shape: pl.BlockSpec(shape, lambda i: (0,) * len(shape))
    any_spec = pl.BlockSpec(memory_space=pl.ANY)
    in_specs = [pl.BlockSpec((tm, D_MODEL), lambda i: (i, 0)),
                pl.BlockSpec((tm, D_MODEL), lambda i: (i + blk0, 0)),
                pl.BlockSpec((tm, D_B), lambda i: (i + blk0, 0)),
                pl.BlockSpec((tm, D_MODEL), lambda i: (i + blk0, 0)),
                pl.BlockSpec((tm, D_MODEL), lambda i: (i + blk0, 1)),
                const((1, D_B)), const((1, D_MODEL)),
                const((D_MODEL, LANES)), const((D_MODEL, LANES)), const((1, LANES)),
                any_spec, any_spec, any_spec, any_spec]
    args = [x, hn_all, ys_all, proj2, proj2, b_glu, g_ffn, wr_hi, wr_lo, b_r, wpa, wglu, wpb, wout]
    io_alias = {}
    if aliases is not None:
        n_in = len(args)
        for j, a in enumerate(aliases):
            in_specs.append(any_spec)
            args.append(a)
            io_alias[n_in + j] = j
    out_shape = (jax.ShapeDtypeStruct((t_all, D_MODEL), F32),
                 jax.ShapeDtypeStruct((t_all, D_MODEL), F32),
                 jax.ShapeDtypeStruct((t_all, LANES), jnp.int32),
                 jax.ShapeDtypeStruct((t_all, LANES), F32))
    out_specs = (pl.BlockSpec((tm, D_MODEL), lambda i: (i + blk0, 0)),
                 pl.BlockSpec((tm, D_MODEL), lambda i: (i + blk0, 0)),
                 pl.BlockSpec((tm, LANES), lambda i: (i + blk0, 0)),
                 pl.BlockSpec((tm, LANES), lambda i: (i + blk0, 0)))
    return pl.pallas_call(
        _merge_kernel,
        out_shape=out_shape,
        grid=(n // tm,),
        in_specs=in_specs,
        out_specs=out_specs,
        scratch_shapes=[pltpu.VMEM(wpa.shape, BF16), pltpu.VMEM(wglu.shape, BF16),
                        pltpu.VMEM(wpb.shape, BF16), pltpu.VMEM(wout.shape, BF16),
                        pltpu.SemaphoreType.DMA((4,))],
        input_output_aliases=io_alias,
        compiler_params=_cparams(("arbitrary",)),
        name="merge_router",
    )(*args)


def _moe_kernel(nt_ref, first_ref, ord_ref, elist_ref, nord_ref, ofirst_ref, xs_ref, wg_hbm, wu_hbm, wd_hbm,
                *rest, tile0, tile_end):
    o_ref, stg_g, stg_u, stg_d, wg_sc, wu_sc, wd_sc, sem = rest[-8:]
    i = pl.program_id(0)
    g = i + tile0
    n_ord = nord_ref[0]
    lim = jnp.minimum(nt_ref[0], tile_end)

    def weight_copies(k, slot):
        e = elist_ref[k]
        return (pltpu.make_async_copy(wg_hbm.at[e], stg_g.at[slot], sem.at[slot, 0]),
                pltpu.make_async_copy(wu_hbm.at[e], stg_u.at[slot], sem.at[slot, 1]),
                pltpu.make_async_copy(wd_hbm.at[e], stg_d.at[slot], sem.at[slot, 2]))

    def starts_here(j):
        return (j < n_ord) & (ofirst_ref[jnp.minimum(j, N_EXPERTS - 1)] < lim)

    valid = g < lim
    k = ord_ref[g]

    @pl.when((i == 0) & valid)
    def _():
        for cp in weight_copies(k, k % 2):
            cp.start()

        @pl.when(starts_here(k + 1))
        def _():
            for cp in weight_copies(k + 1, (k + 1) % 2):
                cp.start()

    @pl.when(valid & ((first_ref[g] == 1) | (i == 0)))
    def _():
        slot = k % 2
        for cp in weight_copies(k, slot):
            cp.wait()
        wg_sc[...] = stg_g[slot].astype(BF16)
        wu_sc[...] = stg_u[slot].astype(BF16)
        wd_sc[...] = stg_d[slot].astype(BF16)

        @pl.when(starts_here(k + 2))
        def _():
            for cp in weight_copies(k + 2, slot):
                cp.start()

    @pl.when(valid)
    def _():
        x = xs_ref[...].astype(BF16)
        hg = _dot(x, wg_sc[...])
        hu = _dot(x, wu_sc[...])
        hh = (_silu(hg) * hu).astype(BF16)
        o_ref[...] = _dot(hh, wd_sc[...])


def _moe_experts(meta, xs, tile0, p_rows, w_gate, w_up, w_down, alias=None):
    nt_call = xs.shape[0] // MOE_TILE
    tile_end = tile0 + nt_call
    tile = lambda i, nt, *_: jnp.maximum(jnp.minimum(i + tile0, nt[0] - 1), tile0)
    any_spec = pl.BlockSpec(memory_space=pl.ANY)
    in_specs = [pl.BlockSpec((MOE_TILE, D_MODEL), lambda i, *s: (tile(i, *s) - tile0, 0)),
                any_spec, any_spec, any_spec]
    args = [xs, w_gate, w_up, w_down]
    io_alias = {}
    if alias is not None:
        io_alias[len(meta) + len(args)] = 0
        in_specs.append(any_spec)
        args.append(alias)
    grid_spec = pltpu.PrefetchScalarGridSpec(
        num_scalar_prefetch=len(meta),
        grid=(nt_call,),
        in_specs=in_specs,
        out_specs=pl.BlockSpec((MOE_TILE, D_MODEL), lambda i, *s: (tile(i, *s), 0)),
        scratch_shapes=[pltpu.VMEM((2, D_MODEL, D_EXPERT), F32), pltpu.VMEM((2, D_MODEL, D_EXPERT), F32),
                        pltpu.VMEM((2, D_EXPERT, D_MODEL), F32),
                        pltpu.VMEM((D_MODEL, D_EXPERT), BF16), pltpu.VMEM((D_MODEL, D_EXPERT), BF16),
                        pltpu.VMEM((D_EXPERT, D_MODEL), BF16),
                        pltpu.SemaphoreType.DMA((2, 3))],
    )
    return pl.pallas_call(
        functools.partial(_moe_kernel, tile0=tile0, tile_end=tile_end),
        out_shape=jax.ShapeDtypeStruct((p_rows, D_MODEL), F32),
        grid_spec=grid_spec,
        input_output_aliases=io_alias,
        compiler_params=_cparams(("arbitrary",)),
        name="moe_experts",
    )(*meta, *args)


def _final_kernel(x1_ref, y0_ref, y1_ref, w_ref, g_ref, o_ref):
    w = w_ref[...]
    x2 = x1_ref[...] + w[:, 0:1] * y0_ref[...] + w[:, 1:2] * y1_ref[...]
    o_ref[...] = x2 * lax.rsqrt(jnp.mean(x2 * x2, axis=1, keepdims=True) + EPS) * g_ref[...]


def _final(x1_all, yg0, yg1, wts, g_final, row0, n, tm):
    blk0 = row0 // tm
    rows = pl.BlockSpec((tm, D_MODEL), lambda i: (i + blk0, 0))
    return pl.pallas_call(
        _final_kernel,
        out_shape=jax.ShapeDtypeStruct((n, D_MODEL), F32),
        grid=(n // tm,),
        in_specs=[rows, rows, rows,
                  pl.BlockSpec((tm, LANES), lambda i: (i + blk0, 0)),
                  pl.BlockSpec((1, D_MODEL), lambda i: (0, 0))],
        out_specs=pl.BlockSpec((tm, D_MODEL), lambda i: (i, 0)),
        compiler_params=_cparams(("arbitrary",)),
        name="combine_final_norm",
    )(x1_all, yg0, yg1, wts, g_final)


def _s5_discretise(a_re, a_im, log_step, b_re, b_im):
    dt = jnp.exp(log_step)[:, None]
    mag = jnp.exp(a_re * dt)
    lb_re = mag * jnp.cos(a_im * dt)
    lb_im = mag * jnp.sin(a_im * dt)
    den = a_re * a_re + a_im * a_im
    nr = lb_re - 1.0
    coef_re = (nr * a_re + lb_im * a_im) / den
    coef_im = (lb_im * a_re - nr * a_im) / den
    bb_re = coef_re[..., None] * b_re - coef_im[..., None] * b_im
    bb_im = coef_re[..., None] * b_im + coef_im[..., None] * b_re
    return lb_re, lb_im, bb_re, bb_im


def _s5_chunk_params(a_re, a_im, log_step, b_re, b_im, c_re, c_im):
    lb_re, lb_im, bb_re, bb_im = _s5_discretise(a_re, a_im, log_step, b_re, b_im)
    L = S5_CHUNK
    gpb = LANES // S5_GROUP
    ncb = G_B // gpb
    pr, pi = [jnp.ones_like(lb_re)], [jnp.zeros_like(lb_re)]
    for _ in range(L):
        pr, pi = pr + [pr[-1] * lb_re - pi[-1] * lb_im], pi + [pr[-1] * lb_im + pi[-1] * lb_re]
    pw_r, pw_i = jnp.stack(pr), jnp.stack(pi)
    bt_re = bb_re.transpose(0, 2, 1)
    bt_im = bb_im.transpose(0, 2, 1)
    lbb_r = pw_r[:L, :, None, :] * bt_re - pw_i[:L, :, None, :] * bt_im
    lbb_i = pw_r[:L, :, None, :] * bt_im + pw_i[:L, :, None, :] * bt_re
    hp = lax.Precision.HIGHEST
    kdt = (jnp.einsum('gop,dgcp->dgco', c_re, lbb_r, precision=hp)
           - jnp.einsum('gop,dgcp->dgco', c_im, lbb_i, precision=hp))
    a_r = c_re * pw_r[1:, :, None, :] - c_im * pw_i[1:, :, None, :]
    a_i = -(c_re * pw_i[1:, :, None, :] + c_im * pw_r[1:, :, None, :])

    def blocks(m):
        return m.reshape(m.shape[:-3] + (ncb, gpb * m.shape[-2], m.shape[-1]))

    vec = lambda v: v.reshape(ncb, 1, S5_CST)
    prompt = (blocks(lbb_r), blocks(lbb_i), blocks(kdt), blocks(a_r), blocks(a_i), vec(pw_r[L]), vec(pw_i[L]))
    sample = (blocks(lbb_r[0]), blocks(lbb_i[0]), blocks(c_re), blocks(c_im), vec(lb_re), vec(lb_im))
    return prompt, sample


def _dispatch(ids, t_all, p_rows):
    e = ids[:, :2].reshape(-1)
    onehot = (e[:, None] == jnp.arange(N_EXPERTS, dtype=jnp.int32)[None, :]).astype(jnp.int32)
    csum = jnp.cumsum(onehot, axis=0)
    rank = jnp.sum((csum - onehot) * onehot, axis=1)
    counts = csum[-1]
    tiles = (counts + MOE_TILE - 1) // MOE_TILE
    tile_end = jnp.cumsum(tiles)
    tile_start = tile_end - tiles
    pos = jnp.sum(onehot * (tile_start * MOE_TILE)[None, :], axis=1) + rank
    tok = jnp.arange(2 * t_all, dtype=jnp.int32) // 2
    src = (jnp.arange(p_rows, dtype=jnp.int32) % t_all).at[pos].set(tok)
    n_tiles = tile_end[-1]
    tidx = jnp.arange(p_rows // MOE_TILE, dtype=jnp.int32)
    tclamp = jnp.minimum(tidx, n_tiles - 1)
    tile_expert = jnp.sum((tile_end[None, :] <= tclamp[:, None]).astype(jnp.int32), axis=1)
    present = (tiles > 0).astype(jnp.int32)
    ord_of_e = jnp.cumsum(present) - 1
    eids = jnp.arange(N_EXPERTS, dtype=jnp.int32)
    elist = jnp.sum(jnp.where((ord_of_e[None, :] == eids[:, None]) & (present[None, :] == 1), eids[None, :], 0), axis=1)
    ordinal = jnp.sum(jnp.where(tile_expert[:, None] == eids[None, :], ord_of_e[None, :], 0), axis=1)
    first = jnp.concatenate([jnp.ones((1,), jnp.int32),
                             (tile_expert[1:] != tile_expert[:-1]).astype(jnp.int32)])
    n_ord = jnp.sum(present)
    ofirst = jnp.sum(jnp.where((ord_of_e[None, :] == eids[:, None]) & (present[None, :] == 1),
                               tile_start[None, :], 0), axis=1)
    ofirst = jnp.where(eids < n_ord, ofirst, p_rows // MOE_TILE)
    one = lambda v: v.reshape(1).astype(jnp.int32)
    meta = (one(n_tiles), first, ordinal.astype(jnp.int32), elist.astype(jnp.int32), one(n_ord),
            ofirst.astype(jnp.int32))
    return pos.reshape(t_all, 2), src, meta


def _pick_tile(n, candidates):
    for c in candidates:
        if n % c == 0:
            return c
    raise ValueError(f"no row tile for {n}")


def kernel(x_prompt, x_sample, state_mlstm_C, state_mlstm_n, state_mlstm_m, state_conv, state_s5_re,
           state_s5_im, norm_mix_g, w_in, b_i, b_f, w_conv, b_conv, head_norm_g, w_pa, s5_a_re, s5_a_im,
           s5_log_step, s5_b_re, s5_b_im, s5_c_re, s5_c_im, s5_d, s5_w_glu, s5_b_glu, w_pb, w_out,
           norm_ffn_g, w_rg, b_rg, w_rexp, b_rexp, w_gate, w_up, w_down, norm_final_g):
    assert state_mlstm_C.shape[0] == 1 and x_sample.shape[1] == 1
    batch, seq, _ = x_prompt.shape
    dec = x_sample.shape[0]
    t_p = batch * seq
    t_all = t_p + dec
    assert seq % CHUNK == 0 and t_p % dec == 0 and dec % LANES == 0
    d_a = H_A * DK
    tm_p = _pick_tile(t_p, (512, 256, 128))
    tm_all = _pick_tile(t_all, (1664, 640, 384, 128))
    sample_blk = t_p // dec

    xp = x_prompt.reshape(t_p, D_MODEL)
    xs = x_sample.reshape(dec, D_MODEL)

    g_mix = norm_mix_g[0]
    w_in_t = w_in.reshape(w_in.shape[1:]).T
    n_qkvo = 4 * d_a
    tn = 1024
    n_gate_cols = 2 * H_A
    b_gates = jnp.pad(jnp.concatenate([b_i[0], b_f[0]]), (0, LANES - n_gate_cols))
    xn_all, gates, gates_t = _rmsnorm_rows(xp, g_mix, w_in_t, n_qkvo, b_gates, t_all, 0, tm_p)
    xn_all, gates, _ = _rmsnorm_rows(xs, g_mix, w_in_t, n_qkvo, b_gates, t_all, t_p, dec,
                                     alias=(xn_all, gates))
    qkvo = _matmul_t(xn_all, w_in_t, [j * tn for j in range(n_qkvo // tn)], jnp.zeros((1, n_qkvo), F32),
                     tm_all, tn, F32)
    c_u = n_qkvo + n_gate_cols
    c_ga = c_u + D_B
    starts = [c_ga + j * tn for j in range(2 * D_MODEL // tn)] + [c_u]
    proj2 = _matmul_t(xn_all, w_in_t, starts, jnp.zeros((1, len(starts) * tn), F32), tm_all, tn, F32)
    u_col0 = 2 * D_MODEL

    conv_s_in = state_conv[0].reshape(dec, (CONV_W - 1) * 2 * d_a)
    q_s, kw_s, a_e, s_e, den_e, conv_s, n_s, m_s = _sample_pre(
        qkvo, conv_s_in, w_conv[0], b_conv[0].reshape(1, -1), gates, state_mlstm_m[0],
        state_mlstm_n[0].reshape(dec, d_a), dec, sample_blk)
    v_s = qkvo[t_p:, 2 * d_a:3 * d_a]
    r3 = lambda a: a.reshape(dec, H_A, DK)

    head_g = head_norm_g[0].reshape(1, d_a)
    hn_all, c_p, n_p, m_p, c_s, qc = _mlstm_prompt(
        qkvo, gates, gates_t, w_conv[0], b_conv[0].reshape(1, -1), head_g,
        (r3(q_s), r3(kw_s), r3(v_s), r3(a_e), state_mlstm_C[0]), batch, seq, t_all)
    conv_p = jnp.stack([qkvo[b * seq + seq - (CONV_W - 1):(b + 1) * seq, :2 * d_a] for b in range(batch)])
    hn_all = _sample_post(qc.reshape(dec, d_a), s_e, a_e, den_e, qkvo, head_g, hn_all, dec, sample_blk)

    d_row = s5_d[0].reshape(1, D_B)
    s5_prompt_w, s5_sample_w = _s5_chunk_params(s5_a_re[0], s5_a_im[0], s5_log_step[0], s5_b_re[0],
                                                s5_b_im[0], s5_c_re[0], s5_c_im[0])
    ys_all, s5re_p, s5im_p = _s5_prompt(proj2, u_col0, *s5_prompt_w, d_row, batch, seq, t_all)

    ys_all, s5re_s, s5im_s = _s5_sample(proj2, u_col0, state_s5_re[0].reshape(dec, -1),
                                        state_s5_im[0].reshape(dec, -1),
                                        *s5_sample_w, d_row, ys_all, dec, sample_blk)

    wr = jnp.pad(jnp.concatenate([w_rg[0], w_rexp[0]], axis=1), ((0, 0), (0, LANES - N_GROUPS - N_EXPERTS)))
    wr_hi = wr.astype(BF16)
    wr_lo = (wr - wr_hi.astype(F32)).astype(BF16)
    b_r = jnp.pad(jnp.concatenate([b_rg[0], b_rexp[0]]), (0, LANES - N_GROUPS - N_EXPERTS)).reshape(1, LANES)
    merge_w = (w_pa[0].astype(BF16), s5_w_glu[0].astype(BF16), w_pb[0].astype(BF16), w_out[0].astype(BF16))
    b_glu = s5_b_glu[0].reshape(1, D_B)
    g_ffn = norm_ffn_g[0].reshape(1, D_MODEL)
    tm_m = _pick_tile(t_p, (256, 128))
    outs = _merge(xp, hn_all, ys_all, proj2, b_glu, g_ffn, wr_hi, wr_lo, b_r, *merge_w, t_all, 0, tm_m)
    x1_all, xn2_all, ids, wts = _merge(xs, hn_all, ys_all, proj2, b_glu, g_ffn, wr_hi, wr_lo, b_r,
                                       *merge_w, t_all, t_p, dec, aliases=outs)

    p_rows = -(-(2 * t_all + N_EXPERTS * (MOE_TILE - 1)) // MOE_TILE) * MOE_TILE
    pos, src, meta = _dispatch(ids, t_all, p_rows)
    take_rows = lambda a, idx: a.at[idx].get(mode='promise_in_bounds')
    split = (p_rows // MOE_TILE // 2) * MOE_TILE
    yp = _moe_experts(meta, take_rows(xn2_all, src[:split]), 0, p_rows, w_gate[0], w_up[0], w_down[0])
    yp = _moe_experts(meta, take_rows(xn2_all, src[split:]), split // MOE_TILE, p_rows,
                      w_gate[0], w_up[0], w_down[0], alias=yp)
    yg0 = take_rows(yp, pos[:, 0])
    yg1 = take_rows(yp, pos[:, 1])

    g_fin = norm_final_g.reshape(1, D_MODEL)
    y_prompt = _final(x1_all, yg0, yg1, wts, g_fin, 0, t_p, tm_p).reshape(batch, seq, D_MODEL)
    y_sample = _final(x1_all, yg0, yg1, wts, g_fin, t_p, dec, dec).reshape(dec, 1, D_MODEL)

    lead = lambda a, shape: a.reshape((1,) + shape)
    return (y_prompt, y_sample,
            lead(c_p, (batch, H_A, DK, DK)), lead(n_p, (batch, H_A, DK)), lead(m_p[:, 0, :H_A], (batch, H_A)),
            lead(conv_p, (batch, CONV_W - 1, 2 * d_a)),
            lead(s5re_p, (batch, G_B, P_S5)), lead(s5im_p, (batch, G_B, P_S5)),
            lead(c_s, (dec, H_A, DK, DK)), lead(n_s, (dec, H_A, DK)), lead(m_s[:, :H_A], (dec, H_A)),
            lead(conv_s, (dec, CONV_W - 1, 2 * d_a)),
            lead(s5re_s, (dec, G_B, P_S5)), lead(s5im_s, (dec, G_B, P_S5)))
```

```python
import functools
import math

import jax
import jax.numpy as jnp
from jax import lax
from jax.experimental import pallas as pl
from jax.experimental.pallas import tpu as pltpu

F32 = jnp.float32
BF16 = jnp.bfloat16

D_MODEL = 2048
H_A = 8
DK = 256
CONV_W = 4
CHUNK = 128
D_B = 1024
S5_GROUP = 16
G_B = 64
P_S5 = 64
N_GROUPS = 4
EXP_PER_GROUP = 8
N_EXPERTS = 32
D_EXPERT = 512
EPS = 1e-6

LANES = 128
SUBLANES = 8
VMEM_LIMIT = 56 * 1024 * 1024

S5_CHUNK = 8
S5_CST = (LANES // S5_GROUP) * P_S5
MOE_TILE = 256


def _cparams(sem):
    return pltpu.CompilerParams(dimension_semantics=sem, vmem_limit_bytes=VMEM_LIMIT)


def _silu(x):
    return x * (1.0 / (1.0 + jnp.exp(-x)))


def _sigmoid(x):
    return 1.0 / (1.0 + jnp.exp(-x))


def _log_sigmoid(x):
    return jnp.minimum(x, 0.0) - jnp.log1p(jnp.exp(-jnp.abs(x)))


def _gelu_tanh(x):
    c = math.sqrt(2.0 / math.pi)
    return 0.5 * x * (1.0 + jnp.tanh(c * (x + 0.044715 * (x * x * x))))


def _split3(x):
    hi = x.astype(BF16)
    r = x - hi.astype(F32)
    mid = r.astype(BF16)
    lo = (r - mid.astype(F32)).astype(BF16)
    return hi, mid, lo


def _dot(a, b):
    return jnp.dot(a, b, preferred_element_type=F32)


def _dot_nt(a, b):
    return lax.dot_general(a, b, (((1,), (1,)), ((), ())), preferred_element_type=F32)


def _dot_tn(a, b):
    return lax.dot_general(a, b, (((0,), (0,)), ((), ())), preferred_element_type=F32)


def _rmsnorm_kernel(x_ref, g_ref, wg_ref, bg_ref, bgt_ref, *rest):
    n_out = 3
    o_ref, gates_ref, gates_t_ref = rest[-n_out:]
    x = x_ref[...]
    r = lax.rsqrt(jnp.mean(x * x, axis=-1, keepdims=True) + EPS)
    xn = (x * r * g_ref[...]).astype(o_ref.dtype)
    o_ref[...] = xn
    wg = wg_ref[...].astype(BF16)
    gates_ref[...] = _dot_nt(xn, wg) + bg_ref[...]
    gt = _dot_nt(wg, xn) + bgt_ref[...]
    gates_t_ref[...] = gt[0:gates_t_ref.shape[0], :]


def _rmsnorm_rows(x, g, w_t, gate_row0, b_gates, t_all, row0, tm, alias=None):
    n = x.shape[0]
    blk0 = row0 // tm
    in_specs = [pl.BlockSpec((tm, D_MODEL), lambda i: (i, 0)),
                pl.BlockSpec((1, D_MODEL), lambda i: (0, 0)),
                pl.BlockSpec((pl.Element(LANES), pl.Element(D_MODEL)), lambda i: (gate_row0, 0)),
                pl.BlockSpec((1, LANES), lambda i: (0, 0)),
                pl.BlockSpec((LANES, 1), lambda i: (0, 0))]
    args = [x, g.reshape(1, D_MODEL), w_t, b_gates.reshape(1, LANES), b_gates.reshape(LANES, 1)]
    aliases = {}
    if alias is not None:
        for a in alias:
            aliases[len(args)] = len(aliases)
            in_specs.append(pl.BlockSpec(memory_space=pl.ANY))
            args.append(a)
    return pl.pallas_call(
        _rmsnorm_kernel,
        out_shape=(jax.ShapeDtypeStruct((t_all, D_MODEL), BF16),
                   jax.ShapeDtypeStruct((t_all, LANES), F32),
                   jax.ShapeDtypeStruct((2 * H_A, n), F32)),
        grid=(n // tm,),
        in_specs=in_specs,
        out_specs=(pl.BlockSpec((tm, D_MODEL), lambda i: (i + blk0, 0)),
                   pl.BlockSpec((tm, LANES), lambda i: (i + blk0, 0)),
                   pl.BlockSpec((2 * H_A, tm), lambda i: (0, i))),
        input_output_aliases=aliases,
        compiler_params=_cparams(("arbitrary",)),
        name="rmsnorm_rows",
    )(*args)


def _mm_t_kernel(starts_ref, a_ref, wt_ref, b_ref, o_ref, wb_ref):
    del starts_ref

    @pl.when(pl.program_id(1) == 0)
    def _():
        wb_ref[...] = wt_ref[...].astype(BF16)

    o_ref[...] = (_dot_nt(a_ref[...], wb_ref[...]) + b_ref[...]).astype(o_ref.dtype)


def _matmul_t(a, w_t, row_starts, bias, tm, tn, out_dtype):
    m, k = a.shape
    n_t = len(row_starts)
    assert all(s % SUBLANES == 0 for s in row_starts)
    grid_spec = pltpu.PrefetchScalarGridSpec(
        num_scalar_prefetch=1,
        grid=(n_t, m // tm),
        in_specs=[pl.BlockSpec((tm, k), lambda j, i, st: (i, 0)),
                  pl.BlockSpec((pl.Element(tn), pl.Element(k)), lambda j, i, st: (st[j] * SUBLANES, 0)),
                  pl.BlockSpec((1, tn), lambda j, i, st: (0, j))],
        out_specs=pl.BlockSpec((tm, tn), lambda j, i, st: (i, j)),
        scratch_shapes=[pltpu.VMEM((tn, k), BF16)],
    )
    return pl.pallas_call(
        _mm_t_kernel,
        out_shape=jax.ShapeDtypeStruct((m, n_t * tn), out_dtype),
        grid_spec=grid_spec,
        compiler_params=_cparams(("arbitrary", "arbitrary")),
        name="rows_matmul_t",
    )(jnp.asarray([s // SUBLANES for s in row_starts], jnp.int32), a, w_t, bias)


def _sample_c_update(q_ref, kw_ref, v_ref, a_ref, c_ref, c_out, qc_out):
    rows = 2 * SUBLANES
    rid = lax.broadcasted_iota(jnp.int32, (rows, DK), 0)
    pad = jnp.zeros((rows - H_A, DK), F32)
    for b in range(q_ref.shape[0]):
        q16 = jnp.concatenate([q_ref[b], pad], axis=0).astype(BF16)
        kw16 = jnp.concatenate([kw_ref[b], pad], axis=0)
        v16 = jnp.concatenate([v_ref[b], pad], axis=0).astype(BF16)
        qc_rows = []
        for h in range(H_A):
            c_prev = c_ref[b, h]
            qc_rows.append(_dot(q16, c_prev.astype(BF16))[h:h + 1, :])
            kw_h = jnp.where(rid == h, kw16, 0.0).astype(BF16)
            c_out[b, h] = a_ref[b, h:h + 1, :] * c_prev + _dot_tn(kw_h, v16)
        qc_out[b] = jnp.concatenate(qc_rows, axis=0)


def _mlstm_kernel(qk_ref, v_ref, o_ref, gcol_ref, grow_ref, wc_ref, bc_ref, hg_ref,
                  sq_ref, skw_ref, sv_ref, sa_ref, sc_ref,
                  h_ref, c_out, n_out, m_out, sc_out, sqc_out, c_sc, n_sc, m_sc, ext_sc):
    c = pl.program_id(1)
    _sample_c_update(sq_ref, skw_ref, sv_ref, sa_ref, sc_ref, sc_out, sqc_out)
    L = CHUNK
    pad = SUBLANES
    d_a = H_A * DK

    @pl.when(c == 0)
    def _():
        c_sc[...] = jnp.zeros_like(c_sc)
        n_sc[...] = jnp.zeros_like(n_sc)
        m_sc[...] = jnp.zeros_like(m_sc)
        ext_sc[0:pad, :] = jnp.zeros((pad, 2 * d_a), F32)

    x = qk_ref[...]
    ext_sc[pad:pad + L, :] = x
    y = bc_ref[...] + wc_ref[CONV_W - 1:CONV_W, :] * x
    for j in range(1, CONV_W):
        y = y + wc_ref[CONV_W - 1 - j:CONV_W - j, :] * ext_sc[pad - j:pad - j + L, :]
    ext_sc[0:pad, :] = x[L - pad:L, :]
    y = _silu(y)

    gcol = gcol_ref[...]
    grow = grow_ref[...]
    ri = lax.broadcasted_iota(jnp.int32, (L, L), 0)
    ci = lax.broadcasted_iota(jnp.int32, (L, L), 1)
    causal = ci <= ri
    tril = jnp.where(causal, 1.0, 0.0).astype(BF16)
    triu = jnp.where(ri <= ci, 1.0, 0.0).astype(BF16)
    b_cols = sum(_dot(tril, p) for p in _split3(_log_sigmoid(gcol)))
    b_rows = sum(_dot(p, triu) for p in _split3(_log_sigmoid(grow)))

    for h in range(H_A):
        sl = slice(h * DK, (h + 1) * DK)
        q = y[:, sl] * (DK ** -0.5)
        k = y[:, d_a + h * DK:d_a + (h + 1) * DK]
        ig_col = gcol[:, h:h + 1]
        ig_row = grow[h:h + 1, :]
        b_col = b_cols[:, H_A + h:H_A + h + 1]
        b_row = b_rows[H_A + h:H_A + h + 1, :]

        m_prev = m_sc[:, h:h + 1]
        d_log = jnp.where(causal, b_col - b_row + ig_row, -jnp.inf)
        inter_log = b_col + m_prev
        m_t = jnp.maximum(inter_log, jnp.max(d_log, axis=1, keepdims=True))
        qb = q.astype(BF16)
        kb = k.astype(BF16)
        vb = v_ref[:, sl].astype(BF16)
        s = _dot_nt(qb, kb) * jnp.exp(d_log - m_t)
        inter_w = jnp.exp(inter_log - m_t)
        c_prev = c_sc[h]
        n_prev = n_sc[h:h + 1, :]
        num = _dot(s.astype(BF16), vb) + inter_w * _dot(qb, c_prev.astype(BF16))
        nq = jnp.sum(s, axis=1, keepdims=True) + inter_w * jnp.sum(q * n_prev, axis=1, keepdims=True)
        den = jnp.maximum(jnp.abs(nq), jnp.exp(-m_t))
        hh = num / den
        hh = hh * _sigmoid(o_ref[:, sl])
        hh = hh * lax.rsqrt(jnp.mean(hh * hh, axis=1, keepdims=True) + EPS)
        h_ref[:, sl] = (hh * hg_ref[:, sl]).astype(h_ref.dtype)

        m_new = m_t[L - 1:L, :]
        b_last = b_col[L - 1:L, :]
        decay = jnp.exp(b_last + m_prev - m_new)
        w_end = jnp.exp(b_last - b_col + ig_col - m_new)
        kw = k * w_end
        c_sc[h] = decay * c_prev + _dot_tn(kw.astype(BF16), vb)
        n_sc[h:h + 1, :] = decay * n_prev + jnp.sum(kw, axis=0, keepdims=True)
        m_sc[:, h:h + 1] = m_new

    @pl.when(c == pl.num_programs(1) - 1)
    def _():
        c_out[...] = c_sc[...]
        n_out[...] = n_sc[...]
        m_out[...] = m_sc[...]


def _mlstm_prompt(qkvo, gates, gates_t, w_conv, b_conv, head_g, sample, batch, seq, t_all):
    nc = seq // CHUNK
    L = CHUNK
    d_a = H_A * DK
    dec = sample[0].shape[0]
    nb = dec // (batch * nc)
    assert nb * batch * nc == dec
    row = lambda b, c: b * nc + c
    svec = pl.BlockSpec((nb, H_A, DK), lambda b, c: (row(b, c), 0, 0))
    smat = pl.BlockSpec((nb, H_A, DK, DK), lambda b, c: (row(b, c), 0, 0, 0))
    in_specs = [
        pl.BlockSpec((L, 2 * d_a), lambda b, c: (row(b, c), 0)),
        pl.BlockSpec((L, d_a), lambda b, c: (row(b, c), 2)),
        pl.BlockSpec((L, d_a), lambda b, c: (row(b, c), 3)),
        pl.BlockSpec((L, LANES), lambda b, c: (row(b, c), 0)),
        pl.BlockSpec((2 * H_A, L), lambda b, c: (0, row(b, c))),
        pl.BlockSpec((CONV_W, 2 * d_a), lambda b, c: (0, 0)),
        pl.BlockSpec((1, 2 * d_a), lambda b, c: (0, 0)),
        pl.BlockSpec((1, d_a), lambda b, c: (0, 0)),
        svec, svec, svec, svec, smat,
    ]
    out_shape = (
        jax.ShapeDtypeStruct((t_all, d_a), BF16),
        jax.ShapeDtypeStruct((batch, H_A, DK, DK), F32),
        jax.ShapeDtypeStruct((batch, H_A, DK), F32),
        jax.ShapeDtypeStruct((batch, 1, LANES), F32),
        jax.ShapeDtypeStruct((dec, H_A, DK, DK), F32),
        jax.ShapeDtypeStruct((dec, H_A, DK), F32),
    )
    out_specs = (
        pl.BlockSpec((L, d_a), lambda b, c: (row(b, c), 0)),
        pl.BlockSpec((None, H_A, DK, DK), lambda b, c: (b, 0, 0, 0)),
        pl.BlockSpec((None, H_A, DK), lambda b, c: (b, 0, 0)),
        pl.BlockSpec((None, 1, LANES), lambda b, c: (b, 0, 0)),
        smat, svec,
    )
    return pl.pallas_call(
        _mlstm_kernel,
        out_shape=out_shape,
        grid=(batch, nc),
        in_specs=in_specs,
        out_specs=out_specs,
        scratch_shapes=[pltpu.VMEM((H_A, DK, DK), F32), pltpu.VMEM((H_A, DK), F32),
                        pltpu.VMEM((1, LANES), F32), pltpu.VMEM((SUBLANES + L, 2 * d_a), F32)],
        compiler_params=_cparams(("arbitrary", "arbitrary")),
        name="mlstm_prompt",
    )(qkvo, qkvo, qkvo, gates, gates_t, w_conv, b_conv, head_g, *sample)


def _lane_tile(x, reps):
    w = x.shape[1]
    ri = lax.broadcasted_iota(jnp.int32, (w, w * reps), 0)
    ci = lax.broadcasted_iota(jnp.int32, (w, w * reps), 1)
    rep = jnp.where(jnp.bitwise_and(ci, w - 1) == ri, 1.0, 0.0).astype(BF16)
    return _dot(x.astype(BF16), rep)


def _group_mask(rows, cols, row_per, col_per):
    ri = lax.broadcasted_iota(jnp.int32, (rows, cols), 0)
    ci = lax.broadcasted_iota(jnp.int32, (rows, cols), 1)
    return (jnp.right_shift(ri, int(math.log2(row_per))) == jnp.right_shift(ci, int(math.log2(col_per))))


def _s5_build_weights(bin_r, bin_i, kdt, aout_r, aout_i, winc_ref, wintra_ref, wout_ref):
    L = S5_CHUNK
    gpb = LANES // S5_GROUP
    m_inc = _group_mask(LANES, S5_CST, S5_GROUP, P_S5)
    m_lag = _group_mask(LANES, LANES, S5_GROUP, S5_GROUP)
    zero = jnp.zeros((LANES, LANES), BF16)
    lag = [jnp.where(m_lag, _lane_tile(kdt[d], gpb), 0.0).astype(BF16) for d in range(L)]
    for t in range(L):
        rows = slice(t * LANES, (t + 1) * LANES)
        d = L - 1 - t
        winc_ref[rows, 0:S5_CST] = jnp.where(m_inc, _lane_tile(bin_r[d], gpb), 0.0).astype(BF16)
        winc_ref[rows, S5_CST:2 * S5_CST] = jnp.where(m_inc, _lane_tile(bin_i[d], gpb), 0.0).astype(BF16)
        wout_ref[rows, 0:S5_CST] = jnp.where(m_inc, _lane_tile(aout_r[t], gpb), 0.0).astype(BF16)
        wout_ref[rows, S5_CST:2 * S5_CST] = jnp.where(m_inc, _lane_tile(aout_i[t], gpb), 0.0).astype(BF16)
        for t2 in range(L):
            wintra_ref[rows, t2 * LANES:(t2 + 1) * LANES] = lag[t2 - t] if t2 >= t else zero


def _s5_prompt_kernel(u_ref, binr_ref, bini_ref, kdt_ref, aoutr_ref, aouti_ref, lbr_ref, lbi_ref, d_ref,
                      ys_ref, sre_ref, sim_ref, x_sc, winc_ref, wintra_ref, wout_ref, *, batch, nchunk):
    L = S5_CHUNK
    nrow = batch * nchunk
    nst = S5_CST // LANES
    rstr = nchunk + SUBLANES
    _s5_build_weights(binr_ref, bini_ref, kdt_ref, aoutr_ref, aouti_ref, winc_ref, wintra_ref, wout_ref)
    u_t = [u_ref[pl.ds(t, nrow, stride=L), :] for t in range(L)]
    lhs = jnp.concatenate([a.astype(BF16) for a in u_t], axis=1)
    inc = _dot(lhs, winc_ref[...])
    for j in range(2 * nst):
        for b in range(batch):
            x_sc[j, b * rstr:b * rstr + nchunk, :] = inc[b * nchunk:(b + 1) * nchunk, j * LANES:(j + 1) * LANES]

    lbr = [jnp.broadcast_to(lbr_ref[:, j * LANES:(j + 1) * LANES], (batch, LANES)) for j in range(nst)]
    lbi = [jnp.broadcast_to(lbi_ref[:, j * LANES:(j + 1) * LANES], (batch, LANES)) for j in range(nst)]

    def scan_body(r, carry):
        rows = pl.ds(r, batch, stride=rstr)
        out = []
        for j in range(nst):
            xr, xi = carry[j]
            ir = x_sc[j, rows, :]
            ii = x_sc[nst + j, rows, :]
            x_sc[j, rows, :] = xr
            x_sc[nst + j, rows, :] = xi
            out.append((lbr[j] * xr - lbi[j] * xi + ir, lbr[j] * xi + lbi[j] * xr + ii))
        return tuple(out)

    z = jnp.zeros((batch, LANES), F32)
    fin = lax.fori_loop(0, nchunk, scan_body, tuple((z, z) for _ in range(nst)))
    for j in range(nst):
        sre_ref[:, j * LANES:(j + 1) * LANES] = fin[j][0]
        sim_ref[:, j * LANES:(j + 1) * LANES] = fin[j][1]

    xprev = jnp.concatenate(
        [jnp.concatenate([x_sc[j, b * rstr:b * rstr + nchunk, :] for b in range(batch)], axis=0)
         for j in range(2 * nst)], axis=1).astype(BF16)
    y = _dot(lhs, wintra_ref[...]) + _dot_nt(xprev, wout_ref[...])
    for t in range(L):
        yt = y[:, t * LANES:(t + 1) * LANES] + d_ref[...] * u_t[t]
        ys_ref[pl.ds(t, nrow, stride=L), :] = _gelu_tanh(yt).astype(ys_ref.dtype)


def _s5_prompt(proj2, u_col0, bin_r, bin_i, kdt, aout_r, aout_i, lb8r, lb8i, d_row, batch, seq, t_all):
    nchunk = seq // S5_CHUNK
    t_p = batch * seq
    kern = functools.partial(_s5_prompt_kernel, batch=batch, nchunk=nchunk)
    ub0 = u_col0 // LANES
    ncb = D_B // LANES
    kw = S5_CHUNK * LANES
    per_blk = lambda a: pl.BlockSpec((a.shape[0], None) + a.shape[2:], lambda g: (0, g, 0, 0))
    return pl.pallas_call(
        kern,
        out_shape=(jax.ShapeDtypeStruct((t_all, D_B), F32),
                   jax.ShapeDtypeStruct((batch, G_B * P_S5), F32),
                   jax.ShapeDtypeStruct((batch, G_B * P_S5), F32)),
        grid=(ncb,),
        in_specs=[pl.BlockSpec((t_p, LANES), lambda g: (0, ub0 + g)),
                  per_blk(bin_r), per_blk(bin_i), per_blk(kdt), per_blk(aout_r), per_blk(aout_i),
                  pl.BlockSpec((None, 1, S5_CST), lambda g: (g, 0, 0)),
                  pl.BlockSpec((None, 1, S5_CST), lambda g: (g, 0, 0)),
                  pl.BlockSpec((1, LANES), lambda g: (0, g))],
        out_specs=(pl.BlockSpec((t_p, LANES), lambda g: (0, g)),
                   pl.BlockSpec((batch, S5_CST), lambda g: (0, g)),
                   pl.BlockSpec((batch, S5_CST), lambda g: (0, g))),
        scratch_shapes=[pltpu.VMEM((2 * S5_CST // LANES, batch * (nchunk + SUBLANES), LANES), F32),
                        pltpu.VMEM((kw, 2 * S5_CST), BF16), pltpu.VMEM((kw, kw), BF16),
                        pltpu.VMEM((kw, 2 * S5_CST), BF16)],
        compiler_params=_cparams(("arbitrary",)),
        name="s5_prompt",
    )(proj2, bin_r, bin_i, kdt, aout_r, aout_i, lb8r, lb8i, d_row)


def _sample_pre_kernel(qk_ref, conv_ref, wc_ref, bc_ref, g_ref, m_ref, n_ref,
                       q_out, kw_out, a_out, s_out, den_out, conv_out, n_out, m_out):
    c2 = 2 * H_A * DK
    x_new = qk_ref[...]
    y = bc_ref[...] + wc_ref[CONV_W - 1:CONV_W, :] * x_new
    for j in range(CONV_W - 1):
        y = y + wc_ref[j:j + 1, :] * conv_ref[:, j * c2:(j + 1) * c2]
    y = _silu(y)
    conv_out[:, 0:(CONV_W - 2) * c2] = conv_ref[:, c2:(CONV_W - 1) * c2]
    conv_out[:, (CONV_W - 2) * c2:(CONV_W - 1) * c2] = x_new
    g = g_ref[...]
    bd = x_new.shape[0]
    m_cols = []
    for h in range(H_A):
        sl = slice(h * DK, (h + 1) * DK)
        q = y[:, sl] * (DK ** -0.5)
        k = y[:, H_A * DK + h * DK:H_A * DK + (h + 1) * DK]
        ig = g[:, h:h + 1]
        lf = _log_sigmoid(g[:, H_A + h:H_A + h + 1])
        m_prev = m_ref[:, h:h + 1]
        m_t = jnp.maximum(lf + m_prev, ig)
        a = jnp.exp(lf + m_prev - m_t)
        wgt = jnp.exp(ig - m_t)
        n_prev = n_ref[:, sl]
        s = jnp.sum(q * k, axis=1, keepdims=True) * wgt
        nq = s + a * jnp.sum(q * n_prev, axis=1, keepdims=True)
        den = jnp.maximum(jnp.abs(nq), jnp.exp(-m_t))
        kw = wgt * k
        q_out[:, sl] = q
        kw_out[:, sl] = kw
        a_out[:, sl] = jnp.broadcast_to(a, (bd, DK))
        s_out[:, sl] = jnp.broadcast_to(s, (bd, DK))
        den_out[:, sl] = jnp.broadcast_to(den, (bd, DK))
        n_out[:, sl] = a * n_prev + kw
        m_cols.append(m_t)
    lane = lax.broadcasted_iota(jnp.int32, (bd, LANES), 1)
    m_full = jnp.zeros((bd, LANES), F32)
    for h in range(H_A):
        m_full = jnp.where(lane == h, m_cols[h], m_full)
    m_out[...] = m_full


def _sample_pre(qkvo, conv_state, w_conv, b_conv, gates, m_state, n_state, dec, row_blk):
    c2 = 2 * H_A * DK
    d = H_A * DK
    full = lambda shape: pl.BlockSpec(shape, lambda i: (0,) * len(shape))
    rows = lambda: jax.ShapeDtypeStruct((dec, d), F32)
    return pl.pallas_call(
        _sample_pre_kernel,
        out_shape=(rows(), rows(), rows(), rows(), rows(),
                   jax.ShapeDtypeStruct((dec, (CONV_W - 1) * c2), F32), rows(),
                   jax.ShapeDtypeStruct((dec, LANES), F32)),
        grid=(1,),
        in_specs=[pl.BlockSpec((dec, c2), lambda i: (row_blk, 0)),
                  full((dec, (CONV_W - 1) * c2)), full((CONV_W, c2)), full((1, c2)),
                  pl.BlockSpec((dec, LANES), lambda i: (row_blk, 0)),
                  full((dec, H_A)), full((dec, d))],
        out_specs=(full((dec, d)), full((dec, d)), full((dec, d)), full((dec, d)), full((dec, d)),
                   full((dec, (CONV_W - 1) * c2)), full((dec, d)), full((dec, LANES))),
        compiler_params=_cparams(("arbitrary",)),
        name="sample_pre",
    )(qkvo, conv_state, w_conv, b_conv, gates, m_state, n_state)


def _sample_post_kernel(qc_ref, s_ref, a_ref, den_ref, vo_ref, hg_ref, hn_in, h_ref):
    del hn_in
    d = H_A * DK
    num = s_ref[...] * vo_ref[:, 0:d] + a_ref[...] * qc_ref[...]
    hh = num / den_ref[...]
    hh = hh * _sigmoid(vo_ref[:, d:2 * d])
    for h in range(H_A):
        sl = slice(h * DK, (h + 1) * DK)
        seg = hh[:, sl]
        seg = seg * lax.rsqrt(jnp.mean(seg * seg, axis=1, keepdims=True) + EPS)
        h_ref[:, sl] = (seg * hg_ref[:, sl]).astype(h_ref.dtype)


def _sample_post(qc, s_e, a_e, den_e, qkvo, head_g, hn_all, dec, row_blk):
    d = H_A * DK
    full = lambda shape: pl.BlockSpec(shape, lambda i: (0,) * len(shape))
    return pl.pallas_call(
        _sample_post_kernel,
        out_shape=jax.ShapeDtypeStruct(hn_all.shape, hn_all.dtype),
        grid=(1,),
        in_specs=[full((dec, d)), full((dec, d)), full((dec, d)), full((dec, d)),
                  pl.BlockSpec((dec, 2 * d), lambda i: (row_blk, 1)),
                  full((1, d)), pl.BlockSpec(memory_space=pl.ANY)],
        out_specs=pl.BlockSpec((dec, d), lambda i: (row_blk, 0)),
        input_output_aliases={6: 0},
        compiler_params=_cparams(("arbitrary",)),
        name="sample_post",
    )(qc, s_e, a_e, den_e, qkvo, head_g, hn_all)


def _s5_sample_kernel(u_ref, sr_ref, si_ref, br_ref, bi_ref, cr_ref, ci_ref, lbr_ref, lbi_ref, d_ref, ys_in,
                      ys_ref, sre_out, sim_out):
    del ys_in
    gpb = LANES // S5_GROUP
    mask = _group_mask(LANES, S5_CST, S5_GROUP, P_S5)
    expand = lambda blk: jnp.where(mask, _lane_tile(blk, gpb), 0.0)
    for g in range(D_B // LANES):
        ch = slice(g * LANES, (g + 1) * LANES)
        sl = slice(g * S5_CST, (g + 1) * S5_CST)
        u = u_ref[:, ch]
        bmat = jnp.concatenate([expand(br_ref[g]), expand(bi_ref[g])], axis=1).astype(BF16)
        bu = _dot(u.astype(BF16), bmat)
        lbr = lbr_ref[g]
        lbi = lbi_ref[g]
        sr = sr_ref[:, sl]
        si = si_ref[:, sl]
        xr = lbr * sr - lbi * si + bu[:, 0:S5_CST]
        xi = lbr * si + lbi * sr + bu[:, S5_CST:2 * S5_CST]
        sre_out[:, sl] = xr
        sim_out[:, sl] = xi
        x = jnp.concatenate([xr, xi], axis=1).astype(BF16)
        cmat = jnp.concatenate([expand(cr_ref[g]), -expand(ci_ref[g])], axis=1).astype(BF16)
        y = _dot_nt(x, cmat) + d_ref[:, ch] * u
        ys_ref[:, ch] = _gelu_tanh(y).astype(ys_ref.dtype)


def _s5_sample(proj2, u_col0, s_re, s_im, b_r, b_i, c_r, c_i, lbr, lbi, d_row, ys_all, dec, row_blk):
    full = lambda shape: pl.BlockSpec(shape, lambda i: (0,) * len(shape))
    n_state = G_B * P_S5
    ub0 = u_col0 // D_B
    params = (b_r, b_i, c_r, c_i, lbr, lbi)
    return pl.pallas_call(
        _s5_sample_kernel,
        out_shape=(jax.ShapeDtypeStruct(ys_all.shape, ys_all.dtype),
                   jax.ShapeDtypeStruct((dec, n_state), F32),
                   jax.ShapeDtypeStruct((dec, n_state), F32)),
        grid=(1,),
        in_specs=[pl.BlockSpec((dec, D_B), lambda i: (row_blk, ub0)),
                  full((dec, n_state)), full((dec, n_state))]
                 + [full(p.shape) for p in params]
                 + [full((1, D_B)), pl.BlockSpec(memory_space=pl.ANY)],
        out_specs=(pl.BlockSpec((dec, D_B), lambda i: (row_blk, 0)),
                   full((dec, n_state)), full((dec, n_state))),
        input_output_aliases={10: 0},
        compiler_params=_cparams(("arbitrary",)),
        name="s5_sample",
    )(proj2, s_re, s_im, *params, d_row, ys_all)


def _merge_kernel(x_ref, hn_ref, ys_ref, ga_ref, gb_ref, bglu_ref, gffn_ref, wrh_ref, wrl_ref, br_ref,
                  wpa_hbm, wglu_hbm, wpb_hbm, wout_hbm, *rest):
    n_alias = len(rest) - 9
    x1_ref, xn_ref, ids_ref, wts_ref = rest[n_alias:n_alias + 4]
    wpa, wglu, wpb, wout, sem = rest[n_alias + 4:]

    @pl.when(pl.program_id(0) == 0)
    def _():
        copies = [pltpu.make_async_copy(src, dst, sem.at[i])
                  for i, (src, dst) in enumerate(((wpa_hbm, wpa), (wglu_hbm, wglu),
                                                  (wpb_hbm, wpb), (wout_hbm, wout)))]
        for cp in copies:
            cp.start()
        for cp in copies:
            cp.wait()

    ya = _dot(hn_ref[...], wpa[...])
    ys = ys_ref[...]
    gate = _sigmoid(_dot(ys.astype(BF16), wglu[...]) + bglu_ref[...])
    yb = _dot((ys * gate).astype(BF16), wpb[...])
    z = _sigmoid(ga_ref[...]) * ya + _sigmoid(gb_ref[...]) * yb
    x1 = x_ref[...] + _dot(z.astype(BF16), wout[...])
    x1_ref[...] = x1
    xn = x1 * lax.rsqrt(jnp.mean(x1 * x1, axis=1, keepdims=True) + EPS) * gffn_ref[...]
    xn_ref[...] = xn
    xh = xn.astype(BF16)
    xl = (xn - xh.astype(F32)).astype(BF16)
    logits = _dot(xh, wrh_ref[...]) + _dot(xl, wrh_ref[...]) + _dot(xh, wrl_ref[...]) + br_ref[...]

    lane_i = lax.broadcasted_iota(jnp.int32, logits.shape, 1)
    lane = lane_i.astype(F32)
    neg = -jnp.inf
    big = float(1 << 20)
    gl = jnp.where(lane_i < N_GROUPS, logits, neg)
    gmax = jnp.max(gl, axis=1, keepdims=True)
    gsum = jnp.sum(jnp.exp(gl - gmax), axis=1, keepdims=True)
    gidx = jnp.min(jnp.where(gl == gmax, lane, big), axis=1, keepdims=True)
    pg_sel = 1.0 / gsum
    lo = N_GROUPS + gidx * EXP_PER_GROUP
    in_grp = (lane >= lo) & (lane < lo + EXP_PER_GROUP)
    el = jnp.where(in_grp, logits, neg)
    emax = jnp.max(el, axis=1, keepdims=True)
    ee = jnp.exp(el - emax)
    pe = ee / jnp.sum(ee, axis=1, keepdims=True)
    v0 = jnp.max(pe, axis=1, keepdims=True)
    i0 = jnp.min(jnp.where(in_grp & (pe == v0), lane, big), axis=1, keepdims=True)
    rest_m = in_grp & (lane != i0)
    pe1 = jnp.where(rest_m, pe, neg)
    v1 = jnp.max(pe1, axis=1, keepdims=True)
    i1 = jnp.min(jnp.where(rest_m & (pe1 == v1), lane, big), axis=1, keepdims=True)
    tot = v0 + v1
    w0 = pg_sel * (v0 / tot)
    w1 = pg_sel * (v1 / tot)
    ids = jnp.where(lane_i == 0, i0 - N_GROUPS, jnp.where(lane_i == 1, i1 - N_GROUPS, 0.0))
    ids_ref[...] = ids.astype(jnp.int32)
    wts_ref[...] = jnp.where(lane_i == 0, w0, jnp.where(lane_i == 1, w1, 0.0))


def _merge(x, hn_all, ys_all, proj2, b_glu, g_ffn, wr_hi, wr_lo, b_r, wpa, wglu, wpb, wout,
           t_all, row0, tm, aliases=None):
    n = x.shape[0]
    blk0 = row0 // tm
    const = lambda shape: pl.BlockSpec(shape, lambda i: (0,) * len(shape))
    any_spec = pl.BlockSpec(memory_space=pl.ANY)
    in_specs = [pl.BlockSpec((tm, D_MODEL), lambda i: (i, 0)),
                pl.BlockSpec((tm, D_MODEL), lambda i: (i + blk0, 0)),
                pl.BlockSpec((tm, D_B), lambda i: (i + blk0, 0)),
                pl.BlockSpec((tm, D_MODEL), lambda i: (i + blk0, 0)),
                pl.BlockSpec((tm, D_MODEL), lambda i: (i + blk0, 1)),
                const((1, D_B)), const((1, D_MODEL)),
                const((D_MODEL, LANES)), const((D_MODEL, LANES)), const((1, LANES)),
                any_spec, any_spec, any_spec, any_spec]
    args = [x, hn_all, ys_all, proj2, proj2, b_glu, g_ffn, wr_hi, wr_lo, b_r, wpa, wglu, wpb, wout]
    io_alias = {}
    if aliases is not None:
        n_in = len(args)
        for j, a in enumerate(aliases):
            in_specs.append(any_spec)
            args.append(a)
            io_alias[n_in + j] = j
    out_shape = (jax.ShapeDtypeStruct((t_all, D_MODEL), F32),
                 jax.ShapeDtypeStruct((t_all, D_MODEL), F32),
                 jax.ShapeDtypeStruct((t_all, LANES), jnp.int32),
                 jax.ShapeDtypeStruct((t_all, LANES), F32))
    out_specs = (pl.BlockSpec((tm, D_MODEL), lambda i: (i + blk0, 0)),
                 pl.BlockSpec((tm, D_MODEL), lambda i: (i + blk0, 0)),
                 pl.BlockSpec((tm, LANES), lambda i: (i + blk0, 0)),
                 pl.BlockSpec((tm, LANES), lambda i: (i + blk0, 0)))
    return pl.pallas_call(
        _merge_kernel,
        out_shape=out_shape,
        grid=(n // tm,),
        in_specs=in_specs,
        out_specs=out_specs,
        scratch_shapes=[pltpu.VMEM(wpa.shape, BF16), pltpu.VMEM(wglu.shape, BF16),
                        pltpu.VMEM(wpb.shape, BF16), pltpu.VMEM(wout.shape, BF16),
                        pltpu.SemaphoreType.DMA((4,))],
        input_output_aliases=io_alias,
        compiler_params=_cparams(("arbitrary",)),
        name="merge_router",
    )(*args)


def _moe_kernel(nt_ref, first_ref, ord_ref, elist_ref, nord_ref, xs_ref, wg_hbm, wu_hbm, wd_hbm, o_ref,
                stg_g, stg_u, stg_d, wg_sc, wu_sc, wd_sc, sem):
    i = pl.program_id(0)
    n_ord = nord_ref[0]

    def weight_copies(k, slot):
        e = elist_ref[k]
        return (pltpu.make_async_copy(wg_hbm.at[e], stg_g.at[slot], sem.at[slot, 0]),
                pltpu.make_async_copy(wu_hbm.at[e], stg_u.at[slot], sem.at[slot, 1]),
                pltpu.make_async_copy(wd_hbm.at[e], stg_d.at[slot], sem.at[slot, 2]))

    @pl.when(i == 0)
    def _():
        for cp in weight_copies(0, 0):
            cp.start()

        @pl.when(n_ord > 1)
        def _():
            for cp in weight_copies(1, 1):
                cp.start()

    valid = i < nt_ref[0]
    k = ord_ref[i]

    @pl.when(valid & (first_ref[i] == 1))
    def _():
        slot = k % 2
        for cp in weight_copies(k, slot):
            cp.wait()
        wg_sc[...] = stg_g[slot].astype(BF16)
        wu_sc[...] = stg_u[slot].astype(BF16)
        wd_sc[...] = stg_d[slot].astype(BF16)

        @pl.when(k + 2 < n_ord)
        def _():
            for cp in weight_copies(k + 2, slot):
                cp.start()

    @pl.when(valid)
    def _():
        x = xs_ref[...].astype(BF16)
        hg = _dot(x, wg_sc[...])
        hu = _dot(x, wu_sc[...])
        hh = (_silu(hg) * hu).astype(BF16)
        o_ref[...] = _dot(hh, wd_sc[...])


def _moe_experts(n_tiles, first, ordinal, elist, n_ord, xs, w_gate, w_up, w_down):
    p_rows = xs.shape[0]
    last = lambda i, nt, *_: jnp.minimum(i, nt[0] - 1)
    any_spec = pl.BlockSpec(memory_space=pl.ANY)
    grid_spec = pltpu.PrefetchScalarGridSpec(
        num_scalar_prefetch=5,
        grid=(p_rows // MOE_TILE,),
        in_specs=[pl.BlockSpec((MOE_TILE, D_MODEL), lambda i, *s: (last(i, *s), 0)),
                  any_spec, any_spec, any_spec],
        out_specs=pl.BlockSpec((MOE_TILE, D_MODEL), lambda i, *s: (last(i, *s), 0)),
        scratch_shapes=[pltpu.VMEM((2, D_MODEL, D_EXPERT), F32), pltpu.VMEM((2, D_MODEL, D_EXPERT), F32),
                        pltpu.VMEM((2, D_EXPERT, D_MODEL), F32),
                        pltpu.VMEM((D_MODEL, D_EXPERT), BF16), pltpu.VMEM((D_MODEL, D_EXPERT), BF16),
                        pltpu.VMEM((D_EXPERT, D_MODEL), BF16),
                        pltpu.SemaphoreType.DMA((2, 3))],
    )
    return pl.pallas_call(
        _moe_kernel,
        out_shape=jax.ShapeDtypeStruct((p_rows, D_MODEL), F32),
        grid_spec=grid_spec,
        compiler_params=_cparams(("arbitrary",)),
        name="moe_experts",
    )(n_tiles, first, ordinal, elist, n_ord, xs, w_gate, w_up, w_down)


def _final_kernel(x1_ref, y0_ref, y1_ref, w_ref, g_ref, o_ref):
    w = w_ref[...]
    x2 = x1_ref[...] + w[:, 0:1] * y0_ref[...] + w[:, 1:2] * y1_ref[...]
    o_ref[...] = x2 * lax.rsqrt(jnp.mean(x2 * x2, axis=1, keepdims=True) + EPS) * g_ref[...]


def _final(x1_all, yg0, yg1, wts, g_final, row0, n, tm):
    blk0 = row0 // tm
    rows = pl.BlockSpec((tm, D_MODEL), lambda i: (i + blk0, 0))
    return pl.pallas_call(
        _final_kernel,
        out_shape=jax.ShapeDtypeStruct((n, D_MODEL), F32),
        grid=(n // tm,),
        in_specs=[rows, rows, rows,
                  pl.BlockSpec((tm, LANES), lambda i: (i + blk0, 0)),
                  pl.BlockSpec((1, D_MODEL), lambda i: (0, 0))],
        out_specs=pl.BlockSpec((tm, D_MODEL), lambda i: (i, 0)),
        compiler_params=_cparams(("arbitrary",)),
        name="combine_final_norm",
    )(x1_all, yg0, yg1, wts, g_final)


def _s5_discretise(a_re, a_im, log_step, b_re, b_im):
    dt = jnp.exp(log_step)[:, None]
    mag = jnp.exp(a_re * dt)
    lb_re = mag * jnp.cos(a_im * dt)
    lb_im = mag * jnp.sin(a_im * dt)
    den = a_re * a_re + a_im * a_im
    nr = lb_re - 1.0
    coef_re = (nr * a_re + lb_im * a_im) / den
    coef_im = (lb_im * a_re - nr * a_im) / den
    bb_re = coef_re[..., None] * b_re - coef_im[..., None] * b_im
    bb_im = coef_re[..., None] * b_im + coef_im[..., None] * b_re
    return lb_re, lb_im, bb_re, bb_im


def _s5_chunk_params(a_re, a_im, log_step, b_re, b_im, c_re, c_im):
    lb_re, lb_im, bb_re, bb_im = _s5_discretise(a_re, a_im, log_step, b_re, b_im)
    L = S5_CHUNK
    gpb = LANES // S5_GROUP
    ncb = G_B // gpb
    pr, pi = [jnp.ones_like(lb_re)], [jnp.zeros_like(lb_re)]
    for _ in range(L):
        pr, pi = pr + [pr[-1] * lb_re - pi[-1] * lb_im], pi + [pr[-1] * lb_im + pi[-1] * lb_re]
    pw_r, pw_i = jnp.stack(pr), jnp.stack(pi)
    bt_re = bb_re.transpose(0, 2, 1)
    bt_im = bb_im.transpose(0, 2, 1)
    lbb_r = pw_r[:L, :, None, :] * bt_re - pw_i[:L, :, None, :] * bt_im
    lbb_i = pw_r[:L, :, None, :] * bt_im + pw_i[:L, :, None, :] * bt_re
    hp = lax.Precision.HIGHEST
    kdt = (jnp.einsum('gop,dgcp->dgco', c_re, lbb_r, precision=hp)
           - jnp.einsum('gop,dgcp->dgco', c_im, lbb_i, precision=hp))
    a_r = c_re * pw_r[1:, :, None, :] - c_im * pw_i[1:, :, None, :]
    a_i = -(c_re * pw_i[1:, :, None, :] + c_im * pw_r[1:, :, None, :])

    def blocks(m):
        return m.reshape(m.shape[:-3] + (ncb, gpb * m.shape[-2], m.shape[-1]))

    vec = lambda v: v.reshape(ncb, 1, S5_CST)
    prompt = (blocks(lbb_r), blocks(lbb_i), blocks(kdt), blocks(a_r), blocks(a_i), vec(pw_r[L]), vec(pw_i[L]))
    sample = (blocks(lbb_r[0]), blocks(lbb_i[0]), blocks(c_re), blocks(c_im), vec(lb_re), vec(lb_im))
    return prompt, sample


def _dispatch(ids, t_all, p_rows):
    e = ids[:, :2].reshape(-1)
    onehot = (e[:, None] == jnp.arange(N_EXPERTS, dtype=jnp.int32)[None, :]).astype(jnp.int32)
    csum = jnp.cumsum(onehot, axis=0)
    rank = jnp.sum((csum - onehot) * onehot, axis=1)
    counts = csum[-1]
    tiles = (counts + MOE_TILE - 1) // MOE_TILE
    tile_end = jnp.cumsum(tiles)
    tile_start = tile_end - tiles
    pos = jnp.sum(onehot * (tile_start * MOE_TILE)[None, :], axis=1) + rank
    tok = jnp.arange(2 * t_all, dtype=jnp.int32) // 2
    src = (jnp.arange(p_rows, dtype=jnp.int32) % t_all).at[pos].set(tok)
    n_tiles = tile_end[-1]
    tidx = jnp.arange(p_rows // MOE_TILE, dtype=jnp.int32)
    tclamp = jnp.minimum(tidx, n_tiles - 1)
    tile_expert = jnp.sum((tile_end[None, :] <= tclamp[:, None]).astype(jnp.int32), axis=1)
    present = (tiles > 0).astype(jnp.int32)
    ord_of_e = jnp.cumsum(present) - 1
    eids = jnp.arange(N_EXPERTS, dtype=jnp.int32)
    elist = jnp.sum(jnp.where((ord_of_e[None, :] == eids[:, None]) & (present[None, :] == 1), eids[None, :], 0), axis=1)
    ordinal = jnp.sum(jnp.where(tile_expert[:, None] == eids[None, :], ord_of_e[None, :], 0), axis=1)
    first = jnp.concatenate([jnp.ones((1,), jnp.int32),
                             (tile_expert[1:] != tile_expert[:-1]).astype(jnp.int32)])
    one = lambda v: v.reshape(1).astype(jnp.int32)
    meta = (one(n_tiles), first, ordinal.astype(jnp.int32), elist.astype(jnp.int32), one(jnp.sum(present)))
    return pos.reshape(t_all, 2), src, meta


def _pick_tile(n, candidates):
    for c in candidates:
        if n % c == 0:
            return c
    raise ValueError(f"no row tile for {n}")


def kernel(x_prompt, x_sample, state_mlstm_C, state_mlstm_n, state_mlstm_m, state_conv, state_s5_re,
           state_s5_im, norm_mix_g, w_in, b_i, b_f, w_conv, b_conv, head_norm_g, w_pa, s5_a_re, s5_a_im,
           s5_log_step, s5_b_re, s5_b_im, s5_c_re, s5_c_im, s5_d, s5_w_glu, s5_b_glu, w_pb, w_out,
           norm_ffn_g, w_rg, b_rg, w_rexp, b_rexp, w_gate, w_up, w_down, norm_final_g):
    assert state_mlstm_C.shape[0] == 1 and x_sample.shape[1] == 1
    batch, seq, _ = x_prompt.shape
    dec = x_sample.shape[0]
    t_p = batch * seq
    t_all = t_p + dec
    assert seq % CHUNK == 0 and t_p % dec == 0 and dec % LANES == 0
    d_a = H_A * DK
    tm_p = _pick_tile(t_p, (512, 256, 128))
    tm_all = _pick_tile(t_all, (1664, 640, 384, 128))
    sample_blk = t_p // dec

    xp = x_prompt.reshape(t_p, D_MODEL)
    xs = x_sample.reshape(dec, D_MODEL)

    g_mix = norm_mix_g[0]
    w_in_t = w_in.reshape(w_in.shape[1:]).T
    n_qkvo = 4 * d_a
    tn = 1024
    n_gate_cols = 2 * H_A
    b_gates = jnp.pad(jnp.concatenate([b_i[0], b_f[0]]), (0, LANES - n_gate_cols))
    xn_all, gates, gates_t = _rmsnorm_rows(xp, g_mix, w_in_t, n_qkvo, b_gates, t_all, 0, tm_p)
    xn_all, gates, _ = _rmsnorm_rows(xs, g_mix, w_in_t, n_qkvo, b_gates, t_all, t_p, dec,
                                     alias=(xn_all, gates))
    qkvo = _matmul_t(xn_all, w_in_t, [j * tn for j in range(n_qkvo // tn)], jnp.zeros((1, n_qkvo), F32),
                     tm_all, tn, F32)
    c_u = n_qkvo + n_gate_cols
    c_ga = c_u + D_B
    starts = [c_ga + j * tn for j in range(2 * D_MODEL // tn)] + [c_u]
    proj2 = _matmul_t(xn_all, w_in_t, starts, jnp.zeros((1, len(starts) * tn), F32), tm_all, tn, F32)
    u_col0 = 2 * D_MODEL

    conv_s_in = state_conv[0].reshape(dec, (CONV_W - 1) * 2 * d_a)
    q_s, kw_s, a_e, s_e, den_e, conv_s, n_s, m_s = _sample_pre(
        qkvo, conv_s_in, w_conv[0], b_conv[0].reshape(1, -1), gates, state_mlstm_m[0],
        state_mlstm_n[0].reshape(dec, d_a), dec, sample_blk)
    v_s = qkvo[t_p:, 2 * d_a:3 * d_a]
    r3 = lambda a: a.reshape(dec, H_A, DK)

    head_g = head_norm_g[0].reshape(1, d_a)
    hn_all, c_p, n_p, m_p, c_s, qc = _mlstm_prompt(
        qkvo, gates, gates_t, w_conv[0], b_conv[0].reshape(1, -1), head_g,
        (r3(q_s), r3(kw_s), r3(v_s), r3(a_e), state_mlstm_C[0]), batch, seq, t_all)
    conv_p = jnp.stack([qkvo[b * seq + seq - (CONV_W - 1):(b + 1) * seq, :2 * d_a] for b in range(batch)])
    hn_all = _sample_post(qc.reshape(dec, d_a), s_e, a_e, den_e, qkvo, head_g, hn_all, dec, sample_blk)

    d_row = s5_d[0].reshape(1, D_B)
    s5_prompt_w, s5_sample_w = _s5_chunk_params(s5_a_re[0], s5_a_im[0], s5_log_step[0], s5_b_re[0],
                                                s5_b_im[0], s5_c_re[0], s5_c_im[0])
    ys_all, s5re_p, s5im_p = _s5_prompt(proj2, u_col0, *s5_prompt_w, d_row, batch, seq, t_all)

    ys_all, s5re_s, s5im_s = _s5_sample(proj2, u_col0, state_s5_re[0].reshape(dec, -1),
                                        state_s5_im[0].reshape(dec, -1),
                                        *s5_sample_w, d_row, ys_all, dec, sample_blk)

    wr = jnp.pad(jnp.concatenate([w_rg[0], w_rexp[0]], axis=1), ((0, 0), (0, LANES - N_GROUPS - N_EXPERTS)))
    wr_hi = wr.astype(BF16)
    wr_lo = (wr - wr_hi.astype(F32)).astype(BF16)
    b_r = jnp.pad(jnp.concatenate([b_rg[0], b_rexp[0]]), (0, LANES - N_GROUPS - N_EXPERTS)).reshape(1, LANES)
    merge_w = (w_pa[0].astype(BF16), s5_w_glu[0].astype(BF16), w_pb[0].astype(BF16), w_out[0].astype(BF16))
    b_glu = s5_b_glu[0].reshape(1, D_B)
    g_ffn = norm_ffn_g[0].reshape(1, D_MODEL)
    tm_m = _pick_tile(t_p, (256, 128))
    outs = _merge(xp, hn_all, ys_all, proj2, b_glu, g_ffn, wr_hi, wr_lo, b_r, *merge_w, t_all, 0, tm_m)
    x1_all, xn2_all, ids, wts = _merge(xs, hn_all, ys_all, proj2, b_glu, g_ffn, wr_hi, wr_lo, b_r,
                                       *merge_w, t_all, t_p, dec, aliases=outs)

    p_rows = -(-(2 * t_all + N_EXPERTS * (MOE_TILE - 1)) // MOE_TILE) * MOE_TILE
    pos, src, meta = _dispatch(ids, t_all, p_rows)
    take_rows = lambda a, idx: a.at[idx].get(mode='promise_in_bounds')
    xs_sorted = take_rows(xn2_all, src)
    yp = _moe_experts(*meta, xs_sorted, w_gate[0], w_up[0], w_down[0])
    yg0 = take_rows(yp, pos[:, 0])
    yg1 = take_rows(yp, pos[:, 1])

    g_fin = norm_final_g.reshape(1, D_MODEL)
    y_prompt = _final(x1_all, yg0, yg1, wts, g_fin, 0, t_p, tm_p).reshape(batch, seq, D_MODEL)
    y_sample = _final(x1_all, yg0, yg1, wts, g_fin, t_p, dec, dec).reshape(dec, 1, D_MODEL)

    lead = lambda a, shape: a.reshape((1,) + shape)
    return (y_prompt, y_sample,
            lead(c_p, (batch, H_A, DK, DK)), lead(n_p, (batch, H_A, DK)), lead(m_p[:, 0, :H_A], (batch, H_A)),
            lead(conv_p, (batch, CONV_W - 1, 2 * d_a)),
            lead(s5re_p, (batch, G_B, P_S5)), lead(s5im_p, (batch, G_B, P_S5)),
            lead(c_s, (dec, H_A, DK, DK)), lead(n_s, (dec, H_A, DK)), lead(m_s[:, :H_A], (dec, H_A)),
            lead(conv_s, (dec, CONV_W - 1, 2 * d_a)),
            lead(s5re_s, (dec, G_B, P_S5)), lead(s5im_s, (dec, G_B, P_S5)))
```

```python
import functools
import math

import jax
import jax.numpy as jnp
from jax import lax
from jax.experimental import pallas as pl
from jax.experimental.pallas import tpu as pltpu

F32 = jnp.float32
BF16 = jnp.bfloat16

D_MODEL = 2048
H_A = 8
DK = 256
CONV_W = 4
CHUNK = 128
D_B = 1024
S5_GROUP = 16
G_B = 64
P_S5 = 64
N_GROUPS = 4
EXP_PER_GROUP = 8
N_EXPERTS = 32
D_EXPERT = 512
EPS = 1e-6

LANES = 128
SUBLANES = 8
VMEM_LIMIT = 56 * 1024 * 1024

S5_CHUNK = 8
S5_CST = (LANES // S5_GROUP) * P_S5
MOE_TILE = 256


def _cparams(sem):
    return pltpu.CompilerParams(dimension_semantics=sem, vmem_limit_bytes=VMEM_LIMIT)


def _silu(x):
    return x * (1.0 / (1.0 + jnp.exp(-x)))


def _sigmoid(x):
    return 1.0 / (1.0 + jnp.exp(-x))


def _log_sigmoid(x):
    return jnp.minimum(x, 0.0) - jnp.log1p(jnp.exp(-jnp.abs(x)))


def _gelu_tanh(x):
    c = math.sqrt(2.0 / math.pi)
    return 0.5 * x * (1.0 + jnp.tanh(c * (x + 0.044715 * (x * x * x))))


def _split3(x):
    hi = x.astype(BF16)
    r = x - hi.astype(F32)
    mid = r.astype(BF16)
    lo = (r - mid.astype(F32)).astype(BF16)
    return hi, mid, lo


def _dot(a, b):
    return jnp.dot(a, b, preferred_element_type=F32)


def _dot_nt(a, b):
    return lax.dot_general(a, b, (((1,), (1,)), ((), ())), preferred_element_type=F32)


def _dot_tn(a, b):
    return lax.dot_general(a, b, (((0,), (0,)), ((), ())), preferred_element_type=F32)


def _rmsnorm_kernel(x_ref, g_ref, wg_ref, bg_ref, bgt_ref, *rest):
    n_out = 3
    o_ref, gates_ref, gates_t_ref = rest[-n_out:]
    x = x_ref[...]
    r = lax.rsqrt(jnp.mean(x * x, axis=-1, keepdims=True) + EPS)
    xn = (x * r * g_ref[...]).astype(o_ref.dtype)
    o_ref[...] = xn
    wg = wg_ref[...].astype(BF16)
    gates_ref[...] = _dot_nt(xn, wg) + bg_ref[...]
    gt = _dot_nt(wg, xn) + bgt_ref[...]
    gates_t_ref[...] = gt[0:gates_t_ref.shape[0], :]


def _rmsnorm_rows(x, g, w_t, gate_row0, b_gates, t_all, row0, tm, alias=None):
    n = x.shape[0]
    blk0 = row0 // tm
    in_specs = [pl.BlockSpec((tm, D_MODEL), lambda i: (i, 0)),
                pl.BlockSpec((1, D_MODEL), lambda i: (0, 0)),
                pl.BlockSpec((pl.Element(LANES), pl.Element(D_MODEL)), lambda i: (gate_row0, 0)),
                pl.BlockSpec((1, LANES), lambda i: (0, 0)),
                pl.BlockSpec((LANES, 1), lambda i: (0, 0))]
    args = [x, g.reshape(1, D_MODEL), w_t, b_gates.reshape(1, LANES), b_gates.reshape(LANES, 1)]
    aliases = {}
    if alias is not None:
        for a in alias:
            aliases[len(args)] = len(aliases)
            in_specs.append(pl.BlockSpec(memory_space=pl.ANY))
            args.append(a)
    return pl.pallas_call(
        _rmsnorm_kernel,
        out_shape=(jax.ShapeDtypeStruct((t_all, D_MODEL), BF16),
                   jax.ShapeDtypeStruct((t_all, LANES), F32),
                   jax.ShapeDtypeStruct((2 * H_A, n), F32)),
        grid=(n // tm,),
        in_specs=in_specs,
        out_specs=(pl.BlockSpec((tm, D_MODEL), lambda i: (i + blk0, 0)),
                   pl.BlockSpec((tm, LANES), lambda i: (i + blk0, 0)),
                   pl.BlockSpec((2 * H_A, tm), lambda i: (0, i))),
        input_output_aliases=aliases,
        compiler_params=_cparams(("arbitrary",)),
        name="rmsnorm_rows",
    )(*args)


def _mm_t_kernel(starts_ref, a_ref, wt_ref, b_ref, o_ref, wb_ref):
    del starts_ref

    @pl.when(pl.program_id(1) == 0)
    def _():
        wb_ref[...] = wt_ref[...].astype(BF16)

    o_ref[...] = (_dot_nt(a_ref[...], wb_ref[...]) + b_ref[...]).astype(o_ref.dtype)


def _matmul_t(a, w_t, row_starts, bias, tm, tn, out_dtype):
    m, k = a.shape
    n_t = len(row_starts)
    assert all(s % SUBLANES == 0 for s in row_starts)
    grid_spec = pltpu.PrefetchScalarGridSpec(
        num_scalar_prefetch=1,
        grid=(n_t, m // tm),
        in_specs=[pl.BlockSpec((tm, k), lambda j, i, st: (i, 0)),
                  pl.BlockSpec((pl.Element(tn), pl.Element(k)), lambda j, i, st: (st[j] * SUBLANES, 0)),
                  pl.BlockSpec((1, tn), lambda j, i, st: (0, j))],
        out_specs=pl.BlockSpec((tm, tn), lambda j, i, st: (i, j)),
        scratch_shapes=[pltpu.VMEM((tn, k), BF16)],
    )
    return pl.pallas_call(
        _mm_t_kernel,
        out_shape=jax.ShapeDtypeStruct((m, n_t * tn), out_dtype),
        grid_spec=grid_spec,
        compiler_params=_cparams(("arbitrary", "arbitrary")),
        name="rows_matmul_t",
    )(jnp.asarray([s // SUBLANES for s in row_starts], jnp.int32), a, w_t, bias)


def _sample_c_update(q_ref, kw_ref, v_ref, a_ref, c_ref, c_out, qc_out):
    rows = 2 * SUBLANES
    rid = lax.broadcasted_iota(jnp.int32, (rows, DK), 0)
    pad = jnp.zeros((rows - H_A, DK), F32)
    for b in range(q_ref.shape[0]):
        q16 = jnp.concatenate([q_ref[b], pad], axis=0).astype(BF16)
        kw16 = jnp.concatenate([kw_ref[b], pad], axis=0)
        v16 = jnp.concatenate([v_ref[b], pad], axis=0).astype(BF16)
        qc_rows = []
        for h in range(H_A):
            c_prev = c_ref[b, h]
            qc_rows.append(_dot(q16, c_prev.astype(BF16))[h:h + 1, :])
            kw_h = jnp.where(rid == h, kw16, 0.0).astype(BF16)
            c_out[b, h] = a_ref[b, h:h + 1, :] * c_prev + _dot_tn(kw_h, v16)
        qc_out[b] = jnp.concatenate(qc_rows, axis=0)


def _mlstm_kernel(qk_ref, v_ref, o_ref, gcol_ref, grow_ref, wc_ref, bc_ref, hg_ref,
                  sq_ref, skw_ref, sv_ref, sa_ref, sc_ref,
                  h_ref, c_out, n_out, m_out, sc_out, sqc_out, c_sc, n_sc, m_sc, ext_sc):
    c = pl.program_id(1)
    _sample_c_update(sq_ref, skw_ref, sv_ref, sa_ref, sc_ref, sc_out, sqc_out)
    L = CHUNK
    pad = SUBLANES
    d_a = H_A * DK

    @pl.when(c == 0)
    def _():
        c_sc[...] = jnp.zeros_like(c_sc)
        n_sc[...] = jnp.zeros_like(n_sc)
        m_sc[...] = jnp.zeros_like(m_sc)
        ext_sc[0:pad, :] = jnp.zeros((pad, 2 * d_a), F32)

    x = qk_ref[...]
    ext_sc[pad:pad + L, :] = x
    y = bc_ref[...] + wc_ref[CONV_W - 1:CONV_W, :] * x
    for j in range(1, CONV_W):
        y = y + wc_ref[CONV_W - 1 - j:CONV_W - j, :] * ext_sc[pad - j:pad - j + L, :]
    ext_sc[0:pad, :] = x[L - pad:L, :]
    y = _silu(y)

    gcol = gcol_ref[...]
    grow = grow_ref[...]
    ri = lax.broadcasted_iota(jnp.int32, (L, L), 0)
    ci = lax.broadcasted_iota(jnp.int32, (L, L), 1)
    causal = ci <= ri
    tril = jnp.where(causal, 1.0, 0.0).astype(BF16)
    triu = jnp.where(ri <= ci, 1.0, 0.0).astype(BF16)
    b_cols = sum(_dot(tril, p) for p in _split3(_log_sigmoid(gcol)))
    b_rows = sum(_dot(p, triu) for p in _split3(_log_sigmoid(grow)))

    for h in range(H_A):
        sl = slice(h * DK, (h + 1) * DK)
        q = y[:, sl] * (DK ** -0.5)
        k = y[:, d_a + h * DK:d_a + (h + 1) * DK]
        ig_col = gcol[:, h:h + 1]
        ig_row = grow[h:h + 1, :]
        b_col = b_cols[:, H_A + h:H_A + h + 1]
        b_row = b_rows[H_A + h:H_A + h + 1, :]

        m_prev = m_sc[:, h:h + 1]
        d_log = jnp.where(causal, b_col - b_row + ig_row, -jnp.inf)
        inter_log = b_col + m_prev
        m_t = jnp.maximum(inter_log, jnp.max(d_log, axis=1, keepdims=True))
        qb = q.astype(BF16)
        kb = k.astype(BF16)
        vb = v_ref[:, sl].astype(BF16)
        s = _dot_nt(qb, kb) * jnp.exp(d_log - m_t)
        inter_w = jnp.exp(inter_log - m_t)
        c_prev = c_sc[h]
        n_prev = n_sc[h:h + 1, :]
        num = _dot(s.astype(BF16), vb) + inter_w * _dot(qb, c_prev.astype(BF16))
        nq = jnp.sum(s, axis=1, keepdims=True) + inter_w * jnp.sum(q * n_prev, axis=1, keepdims=True)
        den = jnp.maximum(jnp.abs(nq), jnp.exp(-m_t))
        hh = num / den
        hh = hh * _sigmoid(o_ref[:, sl])
        hh = hh * lax.rsqrt(jnp.mean(hh * hh, axis=1, keepdims=True) + EPS)
        h_ref[:, sl] = (hh * hg_ref[:, sl]).astype(h_ref.dtype)

        m_new = m_t[L - 1:L, :]
        b_last = b_col[L - 1:L, :]
        decay = jnp.exp(b_last + m_prev - m_new)
        w_end = jnp.exp(b_last - b_col + ig_col - m_new)
        kw = k * w_end
        c_sc[h] = decay * c_prev + _dot_tn(kw.astype(BF16), vb)
        n_sc[h:h + 1, :] = decay * n_prev + jnp.sum(kw, axis=0, keepdims=True)
        m_sc[:, h:h + 1] = m_new

    @pl.when(c == pl.num_programs(1) - 1)
    def _():
        c_out[...] = c_sc[...]
        n_out[...] = n_sc[...]
        m_out[...] = m_sc[...]


def _mlstm_prompt(qkvo, gates, gates_t, w_conv, b_conv, head_g, sample, batch, seq, t_all):
    nc = seq // CHUNK
    L = CHUNK
    d_a = H_A * DK
    dec = sample[0].shape[0]
    nb = dec // (batch * nc)
    assert nb * batch * nc == dec
    row = lambda b, c: b * nc + c
    svec = pl.BlockSpec((nb, H_A, DK), lambda b, c: (row(b, c), 0, 0))
    smat = pl.BlockSpec((nb, H_A, DK, DK), lambda b, c: (row(b, c), 0, 0, 0))
    in_specs = [
        pl.BlockSpec((L, 2 * d_a), lambda b, c: (row(b, c), 0)),
        pl.BlockSpec((L, d_a), lambda b, c: (row(b, c), 2)),
        pl.BlockSpec((L, d_a), lambda b, c: (row(b, c), 3)),
        pl.BlockSpec((L, LANES), lambda b, c: (row(b, c), 0)),
        pl.BlockSpec((2 * H_A, L), lambda b, c: (0, row(b, c))),
        pl.BlockSpec((CONV_W, 2 * d_a), lambda b, c: (0, 0)),
        pl.BlockSpec((1, 2 * d_a), lambda b, c: (0, 0)),
        pl.BlockSpec((1, d_a), lambda b, c: (0, 0)),
        svec, svec, svec, svec, smat,
    ]
    out_shape = (
        jax.ShapeDtypeStruct((t_all, d_a), BF16),
        jax.ShapeDtypeStruct((batch, H_A, DK, DK), F32),
        jax.ShapeDtypeStruct((batch, H_A, DK), F32),
        jax.ShapeDtypeStruct((batch, 1, LANES), F32),
        jax.ShapeDtypeStruct((dec, H_A, DK, DK), F32),
        jax.ShapeDtypeStruct((dec, H_A, DK), F32),
    )
    out_specs = (
        pl.BlockSpec((L, d_a), lambda b, c: (row(b, c), 0)),
        pl.BlockSpec((None, H_A, DK, DK), lambda b, c: (b, 0, 0, 0)),
        pl.BlockSpec((None, H_A, DK), lambda b, c: (b, 0, 0)),
        pl.BlockSpec((None, 1, LANES), lambda b, c: (b, 0, 0)),
        smat, svec,
    )
    return pl.pallas_call(
        _mlstm_kernel,
        out_shape=out_shape,
        grid=(batch, nc),
        in_specs=in_specs,
        out_specs=out_specs,
        scratch_shapes=[pltpu.VMEM((H_A, DK, DK), F32), pltpu.VMEM((H_A, DK), F32),
                        pltpu.VMEM((1, LANES), F32), pltpu.VMEM((SUBLANES + L, 2 * d_a), F32)],
        compiler_params=_cparams(("arbitrary", "arbitrary")),
        name="mlstm_prompt",
    )(qkvo, qkvo, qkvo, gates, gates_t, w_conv, b_conv, head_g, *sample)


def _lane_tile(x, reps):
    w = x.shape[1]
    ri = lax.broadcasted_iota(jnp.int32, (w, w * reps), 0)
    ci = lax.broadcasted_iota(jnp.int32, (w, w * reps), 1)
    rep = jnp.where(jnp.bitwise_and(ci, w - 1) == ri, 1.0, 0.0).astype(BF16)
    return _dot(x.astype(BF16), rep)


def _group_mask(rows, cols, row_per, col_per):
    ri = lax.broadcasted_iota(jnp.int32, (rows, cols), 0)
    ci = lax.broadcasted_iota(jnp.int32, (rows, cols), 1)
    return (jnp.right_shift(ri, int(math.log2(row_per))) == jnp.right_shift(ci, int(math.log2(col_per))))


def _s5_build_weights(bin_r, bin_i, kdt, aout_r, aout_i, winc_ref, wintra_ref, wout_ref):
    L = S5_CHUNK
    gpb = LANES // S5_GROUP
    m_inc = _group_mask(LANES, S5_CST, S5_GROUP, P_S5)
    m_lag = _group_mask(LANES, LANES, S5_GROUP, S5_GROUP)
    zero = jnp.zeros((LANES, LANES), BF16)
    lag = [jnp.where(m_lag, _lane_tile(kdt[d], gpb), 0.0).astype(BF16) for d in range(L)]
    for t in range(L):
        rows = slice(t * LANES, (t + 1) * LANES)
        d = L - 1 - t
        winc_ref[rows, 0:S5_CST] = jnp.where(m_inc, _lane_tile(bin_r[d], gpb), 0.0).astype(BF16)
        winc_ref[rows, S5_CST:2 * S5_CST] = jnp.where(m_inc, _lane_tile(bin_i[d], gpb), 0.0).astype(BF16)
        wout_ref[rows, 0:S5_CST] = jnp.where(m_inc, _lane_tile(aout_r[t], gpb), 0.0).astype(BF16)
        wout_ref[rows, S5_CST:2 * S5_CST] = jnp.where(m_inc, _lane_tile(aout_i[t], gpb), 0.0).astype(BF16)
        for t2 in range(L):
            wintra_ref[rows, t2 * LANES:(t2 + 1) * LANES] = lag[t2 - t] if t2 >= t else zero


def _s5_prompt_kernel(u_ref, binr_ref, bini_ref, kdt_ref, aoutr_ref, aouti_ref, lbr_ref, lbi_ref, d_ref,
                      ys_ref, sre_ref, sim_ref, x_sc, winc_ref, wintra_ref, wout_ref, *, batch, nchunk):
    L = S5_CHUNK
    nrow = batch * nchunk
    nst = S5_CST // LANES
    rstr = nchunk + SUBLANES
    _s5_build_weights(binr_ref, bini_ref, kdt_ref, aoutr_ref, aouti_ref, winc_ref, wintra_ref, wout_ref)
    u_t = [u_ref[pl.ds(t, nrow, stride=L), :] for t in range(L)]
    lhs = jnp.concatenate([a.astype(BF16) for a in u_t], axis=1)
    inc = _dot(lhs, winc_ref[...])
    for j in range(2 * nst):
        for b in range(batch):
            x_sc[j, b * rstr:b * rstr + nchunk, :] = inc[b * nchunk:(b + 1) * nchunk, j * LANES:(j + 1) * LANES]

    lbr = [jnp.broadcast_to(lbr_ref[:, j * LANES:(j + 1) * LANES], (batch, LANES)) for j in range(nst)]
    lbi = [jnp.broadcast_to(lbi_ref[:, j * LANES:(j + 1) * LANES], (batch, LANES)) for j in range(nst)]

    def scan_body(r, carry):
        rows = pl.ds(r, batch, stride=rstr)
        out = []
        for j in range(nst):
            xr, xi = carry[j]
            ir = x_sc[j, rows, :]
            ii = x_sc[nst + j, rows, :]
            x_sc[j, rows, :] = xr
            x_sc[nst + j, rows, :] = xi
            out.append((lbr[j] * xr - lbi[j] * xi + ir, lbr[j] * xi + lbi[j] * xr + ii))
        return tuple(out)

    z = jnp.zeros((batch, LANES), F32)
    fin = lax.fori_loop(0, nchunk, scan_body, tuple((z, z) for _ in range(nst)), unroll=4)
    for j in range(nst):
        sre_ref[:, j * LANES:(j + 1) * LANES] = fin[j][0]
        sim_ref[:, j * LANES:(j + 1) * LANES] = fin[j][1]

    xprev = jnp.concatenate(
        [jnp.concatenate([x_sc[j, b * rstr:b * rstr + nchunk, :] for b in range(batch)], axis=0)
         for j in range(2 * nst)], axis=1).astype(BF16)
    y = _dot(lhs, wintra_ref[...]) + _dot_nt(xprev, wout_ref[...])
    for t in range(L):
        yt = y[:, t * LANES:(t + 1) * LANES] + d_ref[...] * u_t[t]
        ys_ref[pl.ds(t, nrow, stride=L), :] = _gelu_tanh(yt).astype(ys_ref.dtype)


def _s5_prompt(proj2, u_col0, bin_r, bin_i, kdt, aout_r, aout_i, lb8r, lb8i, d_row, batch, seq, t_all):
    nchunk = seq // S5_CHUNK
    t_p = batch * seq
    kern = functools.partial(_s5_prompt_kernel, batch=batch, nchunk=nchunk)
    ub0 = u_col0 // LANES
    ncb = D_B // LANES
    kw = S5_CHUNK * LANES
    per_blk = lambda a: pl.BlockSpec((a.shape[0], None) + a.shape[2:], lambda g: (0, g, 0, 0))
    return pl.pallas_call(
        kern,
        out_shape=(jax.ShapeDtypeStruct((t_all, D_B), F32),
                   jax.ShapeDtypeStruct((batch, G_B * P_S5), F32),
                   jax.ShapeDtypeStruct((batch, G_B * P_S5), F32)),
        grid=(ncb,),
        in_specs=[pl.BlockSpec((t_p, LANES), lambda g: (0, ub0 + g)),
                  per_blk(bin_r), per_blk(bin_i), per_blk(kdt), per_blk(aout_r), per_blk(aout_i),
                  pl.BlockSpec((None, 1, S5_CST), lambda g: (g, 0, 0)),
                  pl.BlockSpec((None, 1, S5_CST), lambda g: (g, 0, 0)),
                  pl.BlockSpec((1, LANES), lambda g: (0, g))],
        out_specs=(pl.BlockSpec((t_p, LANES), lambda g: (0, g)),
                   pl.BlockSpec((batch, S5_CST), lambda g: (0, g)),
                   pl.BlockSpec((batch, S5_CST), lambda g: (0, g))),
        scratch_shapes=[pltpu.VMEM((2 * S5_CST // LANES, batch * (nchunk + SUBLANES), LANES), F32),
                        pltpu.VMEM((kw, 2 * S5_CST), BF16), pltpu.VMEM((kw, kw), BF16),
                        pltpu.VMEM((kw, 2 * S5_CST), BF16)],
        compiler_params=_cparams(("arbitrary",)),
        name="s5_prompt",
    )(proj2, bin_r, bin_i, kdt, aout_r, aout_i, lb8r, lb8i, d_row)


def _sample_pre_kernel(qk_ref, conv_ref, wc_ref, bc_ref, g_ref, m_ref, n_ref,
                       q_out, kw_out, a_out, s_out, den_out, conv_out, n_out, m_out):
    c2 = 2 * H_A * DK
    x_new = qk_ref[...]
    y = bc_ref[...] + wc_ref[CONV_W - 1:CONV_W, :] * x_new
    for j in range(CONV_W - 1):
        y = y + wc_ref[j:j + 1, :] * conv_ref[:, j * c2:(j + 1) * c2]
    y = _silu(y)
    conv_out[:, 0:(CONV_W - 2) * c2] = conv_ref[:, c2:(CONV_W - 1) * c2]
    conv_out[:, (CONV_W - 2) * c2:(CONV_W - 1) * c2] = x_new
    g = g_ref[...]
    bd = x_new.shape[0]
    m_cols = []
    for h in range(H_A):
        sl = slice(h * DK, (h + 1) * DK)
        q = y[:, sl] * (DK ** -0.5)
        k = y[:, H_A * DK + h * DK:H_A * DK + (h + 1) * DK]
        ig = g[:, h:h + 1]
        lf = _log_sigmoid(g[:, H_A + h:H_A + h + 1])
        m_prev = m_ref[:, h:h + 1]
        m_t = jnp.maximum(lf + m_prev, ig)
        a = jnp.exp(lf + m_prev - m_t)
        wgt = jnp.exp(ig - m_t)
        n_prev = n_ref[:, sl]
        s = jnp.sum(q * k, axis=1, keepdims=True) * wgt
        nq = s + a * jnp.sum(q * n_prev, axis=1, keepdims=True)
        den = jnp.maximum(jnp.abs(nq), jnp.exp(-m_t))
        kw = wgt * k
        q_out[:, sl] = q
        kw_out[:, sl] = kw
        a_out[:, sl] = jnp.broadcast_to(a, (bd, DK))
        s_out[:, sl] = jnp.broadcast_to(s, (bd, DK))
        den_out[:, sl] = jnp.broadcast_to(den, (bd, DK))
        n_out[:, sl] = a * n_prev + kw
        m_cols.append(m_t)
    lane = lax.broadcasted_iota(jnp.int32, (bd, LANES), 1)
    m_full = jnp.zeros((bd, LANES), F32)
    for h in range(H_A):
        m_full = jnp.where(lane == h, m_cols[h], m_full)
    m_out[...] = m_full


def _sample_pre(qkvo, conv_state, w_conv, b_conv, gates, m_state, n_state, dec, row_blk):
    c2 = 2 * H_A * DK
    d = H_A * DK
    full = lambda shape: pl.BlockSpec(shape, lambda i: (0,) * len(shape))
    rows = lambda: jax.ShapeDtypeStruct((dec, d), F32)
    return pl.pallas_call(
        _sample_pre_kernel,
        out_shape=(rows(), rows(), rows(), rows(), rows(),
                   jax.ShapeDtypeStruct((dec, (CONV_W - 1) * c2), F32), rows(),
                   jax.ShapeDtypeStruct((dec, LANES), F32)),
        grid=(1,),
        in_specs=[pl.BlockSpec((dec, c2), lambda i: (row_blk, 0)),
                  full((dec, (CONV_W - 1) * c2)), full((CONV_W, c2)), full((1, c2)),
                  pl.BlockSpec((dec, LANES), lambda i: (row_blk, 0)),
                  full((dec, H_A)), full((dec, d))],
        out_specs=(full((dec, d)), full((dec, d)), full((dec, d)), full((dec, d)), full((dec, d)),
                   full((dec, (CONV_W - 1) * c2)), full((dec, d)), full((dec, LANES))),
        compiler_params=_cparams(("arbitrary",)),
        name="sample_pre",
    )(qkvo, conv_state, w_conv, b_conv, gates, m_state, n_state)


def _sample_post_kernel(qc_ref, s_ref, a_ref, den_ref, vo_ref, hg_ref, hn_in, h_ref):
    del hn_in
    d = H_A * DK
    num = s_ref[...] * vo_ref[:, 0:d] + a_ref[...] * qc_ref[...]
    hh = num / den_ref[...]
    hh = hh * _sigmoid(vo_ref[:, d:2 * d])
    for h in range(H_A):
        sl = slice(h * DK, (h + 1) * DK)
        seg = hh[:, sl]
        seg = seg * lax.rsqrt(jnp.mean(seg * seg, axis=1, keepdims=True) + EPS)
        h_ref[:, sl] = (seg * hg_ref[:, sl]).astype(h_ref.dtype)


def _sample_post(qc, s_e, a_e, den_e, qkvo, head_g, hn_all, dec, row_blk):
    d = H_A * DK
    full = lambda shape: pl.BlockSpec(shape, lambda i: (0,) * len(shape))
    return pl.pallas_call(
        _sample_post_kernel,
        out_shape=jax.ShapeDtypeStruct(hn_all.shape, hn_all.dtype),
        grid=(1,),
        in_specs=[full((dec, d)), full((dec, d)), full((dec, d)), full((dec, d)),
                  pl.BlockSpec((dec, 2 * d), lambda i: (row_blk, 1)),
                  full((1, d)), pl.BlockSpec(memory_space=pl.ANY)],
        out_specs=pl.BlockSpec((dec, d), lambda i: (row_blk, 0)),
        input_output_aliases={6: 0},
        compiler_params=_cparams(("arbitrary",)),
        name="sample_post",
    )(qc, s_e, a_e, den_e, qkvo, head_g, hn_all)


def _s5_sample_kernel(u_ref, sr_ref, si_ref, br_ref, bi_ref, cr_ref, ci_ref, lbr_ref, lbi_ref, d_ref, ys_in,
                      ys_ref, sre_out, sim_out):
    del ys_in
    gpb = LANES // S5_GROUP
    mask = _group_mask(LANES, S5_CST, S5_GROUP, P_S5)
    expand = lambda blk: jnp.where(mask, _lane_tile(blk, gpb), 0.0)
    for g in range(D_B // LANES):
        ch = slice(g * LANES, (g + 1) * LANES)
        sl = slice(g * S5_CST, (g + 1) * S5_CST)
        u = u_ref[:, ch]
        bmat = jnp.concatenate([expand(br_ref[g]), expand(bi_ref[g])], axis=1).astype(BF16)
        bu = _dot(u.astype(BF16), bmat)
        lbr = lbr_ref[g]
        lbi = lbi_ref[g]
        sr = sr_ref[:, sl]
        si = si_ref[:, sl]
        xr = lbr * sr - lbi * si + bu[:, 0:S5_CST]
        xi = lbr * si + lbi * sr + bu[:, S5_CST:2 * S5_CST]
        sre_out[:, sl] = xr
        sim_out[:, sl] = xi
        x = jnp.concatenate([xr, xi], axis=1).astype(BF16)
        cmat = jnp.concatenate([expand(cr_ref[g]), -expand(ci_ref[g])], axis=1).astype(BF16)
        y = _dot_nt(x, cmat) + d_ref[:, ch] * u
        ys_ref[:, ch] = _gelu_tanh(y).astype(ys_ref.dtype)


def _s5_sample(proj2, u_col0, s_re, s_im, b_r, b_i, c_r, c_i, lbr, lbi, d_row, ys_all, dec, row_blk):
    full = lambda shape: pl.BlockSpec(shape, lambda i: (0,) * len(shape))
    n_state = G_B * P_S5
    ub0 = u_col0 // D_B
    params = (b_r, b_i, c_r, c_i, lbr, lbi)
    return pl.pallas_call(
        _s5_sample_kernel,
        out_shape=(jax.ShapeDtypeStruct(ys_all.shape, ys_all.dtype),
                   jax.ShapeDtypeStruct((dec, n_state), F32),
                   jax.ShapeDtypeStruct((dec, n_state), F32)),
        grid=(1,),
        in_specs=[pl.BlockSpec((dec, D_B), lambda i: (row_blk, ub0)),
                  full((dec, n_state)), full((dec, n_state))]
                 + [full(p.shape) for p in params]
                 + [full((1, D_B)), pl.BlockSpec(memory_space=pl.ANY)],
        out_specs=(pl.BlockSpec((dec, D_B), lambda i: (row_blk, 0)),
                   full((dec, n_state)), full((dec, n_state))),
        input_output_aliases={10: 0},
        compiler_params=_cparams(("arbitrary",)),
        name="s5_sample",
    )(proj2, s_re, s_im, *params, d_row, ys_all)


def _merge_kernel(x_ref, hn_ref, ys_ref, ga_ref, gb_ref, bglu_ref, gffn_ref, wrh_ref, wrl_ref, br_ref,
                  wpa_hbm, wglu_hbm, wpb_hbm, wout_hbm, *rest):
    n_alias = len(rest) - 9
    x1_ref, xn_ref, ids_ref, wts_ref = rest[n_alias:n_alias + 4]
    wpa, wglu, wpb, wout, sem = rest[n_alias + 4:]

    @pl.when(pl.program_id(0) == 0)
    def _():
        copies = [pltpu.make_async_copy(src, dst, sem.at[i])
                  for i, (src, dst) in enumerate(((wpa_hbm, wpa), (wglu_hbm, wglu),
                                                  (wpb_hbm, wpb), (wout_hbm, wout)))]
        for cp in copies:
            cp.start()
        for cp in copies:
            cp.wait()

    ya = _dot(hn_ref[...], wpa[...])
    ys = ys_ref[...]
    gate = _sigmoid(_dot(ys.astype(BF16), wglu[...]) + bglu_ref[...])
    yb = _dot((ys * gate).astype(BF16), wpb[...])
    z = _sigmoid(ga_ref[...]) * ya + _sigmoid(gb_ref[...]) * yb
    x1 = x_ref[...] + _dot(z.astype(BF16), wout[...])
    x1_ref[...] = x1
    xn = x1 * lax.rsqrt(jnp.mean(x1 * x1, axis=1, keepdims=True) + EPS) * gffn_ref[...]
    xn_ref[...] = xn
    xh = xn.astype(BF16)
    xl = (xn - xh.astype(F32)).astype(BF16)
    logits = _dot(xh, wrh_ref[...]) + _dot(xl, wrh_ref[...]) + _dot(xh, wrl_ref[...]) + br_ref[...]

    lane_i = lax.broadcasted_iota(jnp.int32, logits.shape, 1)
    lane = lane_i.astype(F32)
    neg = -jnp.inf
    big = float(1 << 20)
    gl = jnp.where(lane_i < N_GROUPS, logits, neg)
    gmax = jnp.max(gl, axis=1, keepdims=True)
    gsum = jnp.sum(jnp.exp(gl - gmax), axis=1, keepdims=True)
    gidx = jnp.min(jnp.where(gl == gmax, lane, big), axis=1, keepdims=True)
    pg_sel = 1.0 / gsum
    lo = N_GROUPS + gidx * EXP_PER_GROUP
    in_grp = (lane >= lo) & (lane < lo + EXP_PER_GROUP)
    el = jnp.where(in_grp, logits, neg)
    emax = jnp.max(el, axis=1, keepdims=True)
    ee = jnp.exp(el - emax)
    pe = ee / jnp.sum(ee, axis=1, keepdims=True)
    v0 = jnp.max(pe, axis=1, keepdims=True)
    i0 = jnp.min(jnp.where(in_grp & (pe == v0), lane, big), axis=1, keepdims=True)
    rest_m = in_grp & (lane != i0)
    pe1 = jnp.where(rest_m, pe, neg)
    v1 = jnp.max(pe1, axis=1, keepdims=True)
    i1 = jnp.min(jnp.where(rest_m & (pe1 == v1), lane, big), axis=1, keepdims=True)
    tot = v0 + v1
    w0 = pg_sel * (v0 / tot)
    w1 = pg_sel * (v1 / tot)
    ids = jnp.where(lane_i == 0, i0 - N_GROUPS, jnp.where(lane_i == 1, i1 - N_GROUPS, 0.0))
    ids_ref[...] = ids.astype(jnp.int32)
    wts_ref[...] = jnp.where(lane_i == 0, w0, jnp.where(lane_i == 1, w1, 0.0))


def _merge(x, hn_all, ys_all, proj2, b_glu, g_ffn, wr_hi, wr_lo, b_r, wpa, wglu, wpb, wout,
           t_all, row0, tm, aliases=None):
    n = x.shape[0]
    blk0 = row0 // tm
    const = lambda shape: pl.BlockSpec(shape, lambda i: (0,) * len(shape))
    any_spec = pl.BlockSpec(memory_space=pl.ANY)
    in_specs = [pl.BlockSpec((tm, D_MODEL), lambda i: (i, 0)),
                pl.BlockSpec((tm, D_MODEL), lambda i: (i + blk0, 0)),
                pl.BlockSpec((tm, D_B), lambda i: (i + blk0, 0)),
                pl.BlockSpec((tm, D_MODEL), lambda i: (i + blk0, 0)),
                pl.BlockSpec((tm, D_MODEL), lambda i: (i + blk0, 1)),
                const((1, D_B)), const((1, D_MODEL)),
                const((D_MODEL, LANES)), const((D_MODEL, LANES)), const((1, LANES)),
                any_spec, any_spec, any_spec, any_spec]
    args = [x, hn_all, ys_all, proj2, proj2, b_glu, g_ffn, wr_hi, wr_lo, b_r, wpa, wglu, wpb, wout]
    io_alias = {}
    if aliases is not None:
        n_in = len(args)
        for j, a in enumerate(aliases):
            in_specs.append(any_spec)
            args.append(a)
            io_alias[n_in + j] = j
    out_shape = (jax.ShapeDtypeStruct((t_all, D_MODEL), F32),
                 jax.ShapeDtypeStruct((t_all, D_MODEL), F32),
                 jax.ShapeDtypeStruct((t_all, LANES), jnp.int32),
                 jax.ShapeDtypeStruct((t_all, LANES), F32))
    out_specs = (pl.BlockSpec((tm, D_MODEL), lambda i: (i + blk0, 0)),
                 pl.BlockSpec((tm, D_MODEL), lambda i: (i + blk0, 0)),
                 pl.BlockSpec((tm, LANES), lambda i: (i + blk0, 0)),
                 pl.BlockSpec((tm, LANES), lambda i: (i + blk0, 0)))
    return pl.pallas_call(
        _merge_kernel,
        out_shape=out_shape,
        grid=(n // tm,),
        in_specs=in_specs,
        out_specs=out_specs,
        scratch_shapes=[pltpu.VMEM(wpa.shape, BF16), pltpu.VMEM(wglu.shape, BF16),
                        pltpu.VMEM(wpb.shape, BF16), pltpu.VMEM(wout.shape, BF16),
                        pltpu.SemaphoreType.DMA((4,))],
        input_output_aliases=io_alias,
        compiler_params=_cparams(("arbitrary",)),
        name="merge_router",
    )(*args)


def _moe_kernel(nt_ref, first_ref, ord_ref, elist_ref, nord_ref, xs_ref, wg_hbm, wu_hbm, wd_hbm, o_ref,
                stg_g, stg_u, stg_d, wg_sc, wu_sc, wd_sc, sem):
    i = pl.program_id(0)
    n_ord = nord_ref[0]

    def weight_copies(k, slot):
        e = elist_ref[k]
        return (pltpu.make_async_copy(wg_hbm.at[e], stg_g.at[slot], sem.at[slot, 0]),
                pltpu.make_async_copy(wu_hbm.at[e], stg_u.at[slot], sem.at[slot, 1]),
                pltpu.make_async_copy(wd_hbm.at[e], stg_d.at[slot], sem.at[slot, 2]))

    @pl.when(i == 0)
    def _():
        for cp in weight_copies(0, 0):
            cp.start()

        @pl.when(n_ord > 1)
        def _():
            for cp in weight_copies(1, 1):
                cp.start()

    valid = i < nt_ref[0]
    k = ord_ref[i]

    @pl.when(valid & (first_ref[i] == 1))
    def _():
        slot = k % 2
        for cp in weight_copies(k, slot):
            cp.wait()
        wg_sc[...] = stg_g[slot].astype(BF16)
        wu_sc[...] = stg_u[slot].astype(BF16)
        wd_sc[...] = stg_d[slot].astype(BF16)

        @pl.when(k + 2 < n_ord)
        def _():
            for cp in weight_copies(k + 2, slot):
                cp.start()

    @pl.when(valid)
    def _():
        x = xs_ref[...].astype(BF16)
        hg = _dot(x, wg_sc[...])
        hu = _dot(x, wu_sc[...])
        hh = (_silu(hg) * hu).astype(BF16)
        o_ref[...] = _dot(hh, wd_sc[...])


def _moe_experts(n_tiles, first, ordinal, elist, n_ord, xs, w_gate, w_up, w_down):
    p_rows = xs.shape[0]
    last = lambda i, nt, *_: jnp.minimum(i, nt[0] - 1)
    any_spec = pl.BlockSpec(memory_space=pl.ANY)
    grid_spec = pltpu.PrefetchScalarGridSpec(
        num_scalar_prefetch=5,
        grid=(p_rows // MOE_TILE,),
        in_specs=[pl.BlockSpec((MOE_TILE, D_MODEL), lambda i, *s: (last(i, *s), 0)),
                  any_spec, any_spec, any_spec],
        out_specs=pl.BlockSpec((MOE_TILE, D_MODEL), lambda i, *s: (last(i, *s), 0)),
        scratch_shapes=[pltpu.VMEM((2, D_MODEL, D_EXPERT), F32), pltpu.VMEM((2, D_MODEL, D_EXPERT), F32),
                        pltpu.VMEM((2, D_EXPERT, D_MODEL), F32),
                        pltpu.VMEM((D_MODEL, D_EXPERT), BF16), pltpu.VMEM((D_MODEL, D_EXPERT), BF16),
                        pltpu.VMEM((D_EXPERT, D_MODEL), BF16),
                        pltpu.SemaphoreType.DMA((2, 3))],
    )
    return pl.pallas_call(
        _moe_kernel,
        out_shape=jax.ShapeDtypeStruct((p_rows, D_MODEL), F32),
        grid_spec=grid_spec,
        compiler_params=_cparams(("arbitrary",)),
        name="moe_experts",
    )(n_tiles, first, ordinal, elist, n_ord, xs, w_gate, w_up, w_down)


def _final_kernel(x1_ref, y0_ref, y1_ref, w_ref, g_ref, o_ref):
    w = w_ref[...]
    x2 = x1_ref[...] + w[:, 0:1] * y0_ref[...] + w[:, 1:2] * y1_ref[...]
    o_ref[...] = x2 * lax.rsqrt(jnp.mean(x2 * x2, axis=1, keepdims=True) + EPS) * g_ref[...]


def _final(x1_all, yg0, yg1, wts, g_final, row0, n, tm):
    blk0 = row0 // tm
    rows = pl.BlockSpec((tm, D_MODEL), lambda i: (i + blk0, 0))
    return pl.pallas_call(
        _final_kernel,
        out_shape=jax.ShapeDtypeStruct((n, D_MODEL), F32),
        grid=(n // tm,),
        in_specs=[rows, rows, rows,
                  pl.BlockSpec((tm, LANES), lambda i: (i + blk0, 0)),
                  pl.BlockSpec((1, D_MODEL), lambda i: (0, 0))],
        out_specs=pl.BlockSpec((tm, D_MODEL), lambda i: (i, 0)),
        compiler_params=_cparams(("arbitrary",)),
        name="combine_final_norm",
    )(x1_all, yg0, yg1, wts, g_final)


def _s5_discretise(a_re, a_im, log_step, b_re, b_im):
    dt = jnp.exp(log_step)[:, None]
    mag = jnp.exp(a_re * dt)
    lb_re = mag * jnp.cos(a_im * dt)
    lb_im = mag * jnp.sin(a_im * dt)
    den = a_re * a_re + a_im * a_im
    nr = lb_re - 1.0
    coef_re = (nr * a_re + lb_im * a_im) / den
    coef_im = (lb_im * a_re - nr * a_im) / den
    bb_re = coef_re[..., None] * b_re - coef_im[..., None] * b_im
    bb_im = coef_re[..., None] * b_im + coef_im[..., None] * b_re
    return lb_re, lb_im, bb_re, bb_im


def _s5_chunk_params(a_re, a_im, log_step, b_re, b_im, c_re, c_im):
    lb_re, lb_im, bb_re, bb_im = _s5_discretise(a_re, a_im, log_step, b_re, b_im)
    L = S5_CHUNK
    gpb = LANES // S5_GROUP
    ncb = G_B // gpb
    pr, pi = [jnp.ones_like(lb_re)], [jnp.zeros_like(lb_re)]
    for _ in range(L):
        pr, pi = pr + [pr[-1] * lb_re - pi[-1] * lb_im], pi + [pr[-1] * lb_im + pi[-1] * lb_re]
    pw_r, pw_i = jnp.stack(pr), jnp.stack(pi)
    bt_re = bb_re.transpose(0, 2, 1)
    bt_im = bb_im.transpose(0, 2, 1)
    lbb_r = pw_r[:L, :, None, :] * bt_re - pw_i[:L, :, None, :] * bt_im
    lbb_i = pw_r[:L, :, None, :] * bt_im + pw_i[:L, :, None, :] * bt_re
    hp = lax.Precision.HIGHEST
    kdt = (jnp.einsum('gop,dgcp->dgco', c_re, lbb_r, precision=hp)
           - jnp.einsum('gop,dgcp->dgco', c_im, lbb_i, precision=hp))
    a_r = c_re * pw_r[1:, :, None, :] - c_im * pw_i[1:, :, None, :]
    a_i = -(c_re * pw_i[1:, :, None, :] + c_im * pw_r[1:, :, None, :])

    def blocks(m):
        return m.reshape(m.shape[:-3] + (ncb, gpb * m.shape[-2], m.shape[-1]))

    vec = lambda v: v.reshape(ncb, 1, S5_CST)
    prompt = (blocks(lbb_r), blocks(lbb_i), blocks(kdt), blocks(a_r), blocks(a_i), vec(pw_r[L]), vec(pw_i[L]))
    sample = (blocks(lbb_r[0]), blocks(lbb_i[0]), blocks(c_re), blocks(c_im), vec(lb_re), vec(lb_im))
    return prompt, sample


def _dispatch(ids, t_all, p_rows):
    e = ids[:, :2].reshape(-1)
    onehot = (e[:, None] == jnp.arange(N_EXPERTS, dtype=jnp.int32)[None, :]).astype(jnp.int32)
    csum = jnp.cumsum(onehot, axis=0)
    rank = jnp.sum((csum - onehot) * onehot, axis=1)
    counts = csum[-1]
    tiles = (counts + MOE_TILE - 1) // MOE_TILE
    tile_end = jnp.cumsum(tiles)
    tile_start = tile_end - tiles
    pos = jnp.sum(onehot * (tile_start * MOE_TILE)[None, :], axis=1) + rank
    tok = jnp.arange(2 * t_all, dtype=jnp.int32) // 2
    src = (jnp.arange(p_rows, dtype=jnp.int32) % t_all).at[pos].set(tok)
    n_tiles = tile_end[-1]
    tidx = jnp.arange(p_rows // MOE_TILE, dtype=jnp.int32)
    tclamp = jnp.minimum(tidx, n_tiles - 1)
    tile_expert = jnp.sum((tile_end[None, :] <= tclamp[:, None]).astype(jnp.int32), axis=1)
    present = (tiles > 0).astype(jnp.int32)
    ord_of_e = jnp.cumsum(present) - 1
    eids = jnp.arange(N_EXPERTS, dtype=jnp.int32)
    elist = jnp.sum(jnp.where((ord_of_e[None, :] == eids[:, None]) & (present[None, :] == 1), eids[None, :], 0), axis=1)
    ordinal = jnp.sum(jnp.where(tile_expert[:, None] == eids[None, :], ord_of_e[None, :], 0), axis=1)
    first = jnp.concatenate([jnp.ones((1,), jnp.int32),
                             (tile_expert[1:] != tile_expert[:-1]).astype(jnp.int32)])
    one = lambda v: v.reshape(1).astype(jnp.int32)
    meta = (one(n_tiles), first, ordinal.astype(jnp.int32), elist.astype(jnp.int32), one(jnp.sum(present)))
    return pos.reshape(t_all, 2), src, meta


def _pick_tile(n, candidates):
    for c in candidates:
        if n % c == 0:
            return c
    raise ValueError(f"no row tile for {n}")


def kernel(x_prompt, x_sample, state_mlstm_C, state_mlstm_n, state_mlstm_m, state_conv, state_s5_re,
           state_s5_im, norm_mix_g, w_in, b_i, b_f, w_conv, b_conv, head_norm_g, w_pa, s5_a_re, s5_a_im,
           s5_log_step, s5_b_re, s5_b_im, s5_c_re, s5_c_im, s5_d, s5_w_glu, s5_b_glu, w_pb, w_out,
           norm_ffn_g, w_rg, b_rg, w_rexp, b_rexp, w_gate, w_up, w_down, norm_final_g):
    assert state_mlstm_C.shape[0] == 1 and x_sample.shape[1] == 1
    batch, seq, _ = x_prompt.shape
    dec = x_sample.shape[0]
    t_p = batch * seq
    t_all = t_p + dec
    assert seq % CHUNK == 0 and t_p % dec == 0 and dec % LANES == 0
    d_a = H_A * DK
    tm_p = _pick_tile(t_p, (512, 256, 128))
    tm_all = _pick_tile(t_all, (1664, 640, 384, 128))
    sample_blk = t_p // dec

    xp = x_prompt.reshape(t_p, D_MODEL)
    xs = x_sample.reshape(dec, D_MODEL)

    g_mix = norm_mix_g[0]
    w_in_t = w_in.reshape(w_in.shape[1:]).T
    n_qkvo = 4 * d_a
    tn = 1024
    n_gate_cols = 2 * H_A
    b_gates = jnp.pad(jnp.concatenate([b_i[0], b_f[0]]), (0, LANES - n_gate_cols))
    xn_all, gates, gates_t = _rmsnorm_rows(xp, g_mix, w_in_t, n_qkvo, b_gates, t_all, 0, tm_p)
    xn_all, gates, _ = _rmsnorm_rows(xs, g_mix, w_in_t, n_qkvo, b_gates, t_all, t_p, dec,
                                     alias=(xn_all, gates))
    qkvo = _matmul_t(xn_all, w_in_t, [j * tn for j in range(n_qkvo // tn)], jnp.zeros((1, n_qkvo), F32),
                     tm_all, tn, F32)
    c_u = n_qkvo + n_gate_cols
    c_ga = c_u + D_B
    starts = [c_ga + j * tn for j in range(2 * D_MODEL // tn)] + [c_u]
    proj2 = _matmul_t(xn_all, w_in_t, starts, jnp.zeros((1, len(starts) * tn), F32), tm_all, tn, F32)
    u_col0 = 2 * D_MODEL

    conv_s_in = state_conv[0].reshape(dec, (CONV_W - 1) * 2 * d_a)
    q_s, kw_s, a_e, s_e, den_e, conv_s, n_s, m_s = _sample_pre(
        qkvo, conv_s_in, w_conv[0], b_conv[0].reshape(1, -1), gates, state_mlstm_m[0],
        state_mlstm_n[0].reshape(dec, d_a), dec, sample_blk)
    v_s = qkvo[t_p:, 2 * d_a:3 * d_a]
    r3 = lambda a: a.reshape(dec, H_A, DK)

    head_g = head_norm_g[0].reshape(1, d_a)
    hn_all, c_p, n_p, m_p, c_s, qc = _mlstm_prompt(
        qkvo, gates, gates_t, w_conv[0], b_conv[0].reshape(1, -1), head_g,
        (r3(q_s), r3(kw_s), r3(v_s), r3(a_e), state_mlstm_C[0]), batch, seq, t_all)
    conv_p = jnp.stack([qkvo[b * seq + seq - (CONV_W - 1):(b + 1) * seq, :2 * d_a] for b in range(batch)])
    hn_all = _sample_post(qc.reshape(dec, d_a), s_e, a_e, den_e, qkvo, head_g, hn_all, dec, sample_blk)

    d_row = s5_d[0].reshape(1, D_B)
    s5_prompt_w, s5_sample_w = _s5_chunk_params(s5_a_re[0], s5_a_im[0], s5_log_step[0], s5_b_re[0],
                                                s5_b_im[0], s5_c_re[0], s5_c_im[0])
    ys_all, s5re_p, s5im_p = _s5_prompt(proj2, u_col0, *s5_prompt_w, d_row, batch, seq, t_all)

    ys_all, s5re_s, s5im_s = _s5_sample(proj2, u_col0, state_s5_re[0].reshape(dec, -1),
                                        state_s5_im[0].reshape(dec, -1),
                                        *s5_sample_w, d_row, ys_all, dec, sample_blk)

    wr = jnp.pad(jnp.concatenate([w_rg[0], w_rexp[0]], axis=1), ((0, 0), (0, LANES - N_GROUPS - N_EXPERTS)))
    wr_hi = wr.astype(BF16)
    wr_lo = (wr - wr_hi.astype(F32)).astype(BF16)
    b_r = jnp.pad(jnp.concatenate([b_rg[0], b_rexp[0]]), (0, LANES - N_GROUPS - N_EXPERTS)).reshape(1, LANES)
    merge_w = (w_pa[0].astype(BF16), s5_w_glu[0].astype(BF16), w_pb[0].astype(BF16), w_out[0].astype(BF16))
    b_glu = s5_b_glu[0].reshape(1, D_B)
    g_ffn = norm_ffn_g[0].reshape(1, D_MODEL)
    tm_m = _pick_tile(t_p, (256, 128))
    outs = _merge(xp, hn_all, ys_all, proj2, b_glu, g_ffn, wr_hi, wr_lo, b_r, *merge_w, t_all, 0, tm_m)
    x1_all, xn2_all, ids, wts = _merge(xs, hn_all, ys_all, proj2, b_glu, g_ffn, wr_hi, wr_lo, b_r,
                                       *merge_w, t_all, t_p, dec, aliases=outs)

    p_rows = -(-(2 * t_all + N_EXPERTS * (MOE_TILE - 1)) // MOE_TILE) * MOE_TILE
    pos, src, meta = _dispatch(ids, t_all, p_rows)
    take_rows = lambda a, idx: a.at[idx].get(mode='promise_in_bounds')
    xs_sorted = take_rows(xn2_all, src)
    yp = _moe_experts(*meta, xs_sorted, w_gate[0], w_up[0], w_down[0])
    yg0 = take_rows(yp, pos[:, 0])
    yg1 = take_rows(yp, pos[:, 1])

    g_fin = norm_final_g.reshape(1, D_MODEL)
    y_prompt = _final(x1_all, yg0, yg1, wts, g_fin, 0, t_p, tm_p).reshape(batch, seq, D_MODEL)
    y_sample = _final(x1_all, yg0, yg1, wts, g_fin, t_p, dec, dec).reshape(dec, 1, D_MODEL)

    lead = lambda a, shape: a.reshape((1,) + shape)
    return (y_prompt, y_sample,
            lead(c_p, (batch, H_A, DK, DK)), lead(n_p, (batch, H_A, DK)), lead(m_p[:, 0, :H_A], (batch, H_A)),
            lead(conv_p, (batch, CONV_W - 1, 2 * d_a)),
            lead(s5re_p, (batch, G_B, P_S5)), lead(s5im_p, (batch, G_B, P_S5)),
            lead(c_s, (dec, H_A, DK, DK)), lead(n_s, (dec, H_A, DK)), lead(m_s[:, :H_A], (dec, H_A)),
            lead(conv_s, (dec, CONV_W - 1, 2 * d_a)),
            lead(s5re_s, (dec, G_B, P_S5)), lead(s5im_s, (dec, G_B, P_S5)))
```

```python
import functools
import math

import jax
import jax.numpy as jnp
from jax import lax
from jax.experimental import pallas as pl
from jax.experimental.pallas import tpu as pltpu

F32 = jnp.float32
BF16 = jnp.bfloat16

D_MODEL = 2048
H_A = 8
DK = 256
CONV_W = 4
CHUNK = 128
D_B = 1024
S5_GROUP = 16
G_B = 64
P_S5 = 64
N_GROUPS = 4
EXP_PER_GROUP = 8
N_EXPERTS = 32
D_EXPERT = 512
EPS = 1e-6

LANES = 128
SUBLANES = 8
VMEM_LIMIT = 56 * 1024 * 1024

S5_CHUNK = 8
S5_CST = (LANES // S5_GROUP) * P_S5
MOE_TILE = 128


def _cparams(sem):
    return pltpu.CompilerParams(dimension_semantics=sem, vmem_limit_bytes=VMEM_LIMIT)


def _silu(x):
    return x * (1.0 / (1.0 + jnp.exp(-x)))


def _sigmoid(x):
    return 1.0 / (1.0 + jnp.exp(-x))


def _log_sigmoid(x):
    return jnp.minimum(x, 0.0) - jnp.log1p(jnp.exp(-jnp.abs(x)))


def _gelu_tanh(x):
    c = math.sqrt(2.0 / math.pi)
    return 0.5 * x * (1.0 + jnp.tanh(c * (x + 0.044715 * (x * x * x))))


def _split3(x):
    hi = x.astype(BF16)
    r = x - hi.astype(F32)
    mid = r.astype(BF16)
    lo = (r - mid.astype(F32)).astype(BF16)
    return hi, mid, lo


def _dot(a, b):
    return jnp.dot(a, b, preferred_element_type=F32)


def _dot_nt(a, b):
    return lax.dot_general(a, b, (((1,), (1,)), ((), ())), preferred_element_type=F32)


def _dot_tn(a, b):
    return lax.dot_general(a, b, (((0,), (0,)), ((), ())), preferred_element_type=F32)


def _rmsnorm_kernel(x_ref, g_ref, wg_ref, bg_ref, bgt_ref, *rest):
    n_out = 3
    o_ref, gates_ref, gates_t_ref = rest[-n_out:]
    x = x_ref[...]
    r = lax.rsqrt(jnp.mean(x * x, axis=-1, keepdims=True) + EPS)
    xn = (x * r * g_ref[...]).astype(o_ref.dtype)
    o_ref[...] = xn
    wg = wg_ref[...].astype(BF16)
    gates_ref[...] = _dot_nt(xn, wg) + bg_ref[...]
    gt = _dot_nt(wg, xn) + bgt_ref[...]
    gates_t_ref[...] = gt[0:gates_t_ref.shape[0], :]


def _rmsnorm_rows(x, g, w_t, gate_row0, b_gates, t_all, row0, tm, alias=None):
    n = x.shape[0]
    blk0 = row0 // tm
    in_specs = [pl.BlockSpec((tm, D_MODEL), lambda i: (i, 0)),
                pl.BlockSpec((1, D_MODEL), lambda i: (0, 0)),
                pl.BlockSpec((pl.Element(LANES), pl.Element(D_MODEL)), lambda i: (gate_row0, 0)),
                pl.BlockSpec((1, LANES), lambda i: (0, 0)),
                pl.BlockSpec((LANES, 1), lambda i: (0, 0))]
    args = [x, g.reshape(1, D_MODEL), w_t, b_gates.reshape(1, LANES), b_gates.reshape(LANES, 1)]
    aliases = {}
    if alias is not None:
        for a in alias:
            aliases[len(args)] = len(aliases)
            in_specs.append(pl.BlockSpec(memory_space=pl.ANY))
            args.append(a)
    return pl.pallas_call(
        _rmsnorm_kernel,
        out_shape=(jax.ShapeDtypeStruct((t_all, D_MODEL), BF16),
                   jax.ShapeDtypeStruct((t_all, LANES), F32),
                   jax.ShapeDtypeStruct((2 * H_A, n), F32)),
        grid=(n // tm,),
        in_specs=in_specs,
        out_specs=(pl.BlockSpec((tm, D_MODEL), lambda i: (i + blk0, 0)),
                   pl.BlockSpec((tm, LANES), lambda i: (i + blk0, 0)),
                   pl.BlockSpec((2 * H_A, tm), lambda i: (0, i))),
        input_output_aliases=aliases,
        compiler_params=_cparams(("arbitrary",)),
        name="rmsnorm_rows",
    )(*args)


def _mm_t_kernel(starts_ref, a_ref, wt_ref, b_ref, o_ref, wb_ref):
    del starts_ref

    @pl.when(pl.program_id(1) == 0)
    def _():
        wb_ref[...] = wt_ref[...].astype(BF16)

    o_ref[...] = (_dot_nt(a_ref[...], wb_ref[...]) + b_ref[...]).astype(o_ref.dtype)


def _matmul_t(a, w_t, row_starts, bias, tm, tn, out_dtype):
    m, k = a.shape
    n_t = len(row_starts)
    assert all(s % SUBLANES == 0 for s in row_starts)
    grid_spec = pltpu.PrefetchScalarGridSpec(
        num_scalar_prefetch=1,
        grid=(n_t, m // tm),
        in_specs=[pl.BlockSpec((tm, k), lambda j, i, st: (i, 0)),
                  pl.BlockSpec((pl.Element(tn), pl.Element(k)), lambda j, i, st: (st[j] * SUBLANES, 0)),
                  pl.BlockSpec((1, tn), lambda j, i, st: (0, j))],
        out_specs=pl.BlockSpec((tm, tn), lambda j, i, st: (i, j)),
        scratch_shapes=[pltpu.VMEM((tn, k), BF16)],
    )
    return pl.pallas_call(
        _mm_t_kernel,
        out_shape=jax.ShapeDtypeStruct((m, n_t * tn), out_dtype),
        grid_spec=grid_spec,
        compiler_params=_cparams(("arbitrary", "arbitrary")),
        name="rows_matmul_t",
    )(jnp.asarray([s // SUBLANES for s in row_starts], jnp.int32), a, w_t, bias)


def _sample_c_update(q_ref, kw_ref, v_ref, a_ref, c_ref, c_out, qc_out):
    rows = 2 * SUBLANES
    rid = lax.broadcasted_iota(jnp.int32, (rows, DK), 0)
    pad = jnp.zeros((rows - H_A, DK), F32)
    for b in range(q_ref.shape[0]):
        q16 = jnp.concatenate([q_ref[b], pad], axis=0).astype(BF16)
        kw16 = jnp.concatenate([kw_ref[b], pad], axis=0)
        v16 = jnp.concatenate([v_ref[b], pad], axis=0).astype(BF16)
        qc_rows = []
        for h in range(H_A):
            c_prev = c_ref[b, h]
            qc_rows.append(_dot(q16, c_prev.astype(BF16))[h:h + 1, :])
            kw_h = jnp.where(rid == h, kw16, 0.0).astype(BF16)
            c_out[b, h] = a_ref[b, h:h + 1, :] * c_prev + _dot_tn(kw_h, v16)
        qc_out[b] = jnp.concatenate(qc_rows, axis=0)


def _mlstm_kernel(qk_ref, v_ref, o_ref, gcol_ref, grow_ref, wc_ref, bc_ref, hg_ref,
                  sq_ref, skw_ref, sv_ref, sa_ref, sc_ref,
                  h_ref, c_out, n_out, m_out, sc_out, sqc_out, c_sc, n_sc, m_sc, ext_sc):
    c = pl.program_id(1)
    _sample_c_update(sq_ref, skw_ref, sv_ref, sa_ref, sc_ref, sc_out, sqc_out)
    L = CHUNK
    pad = SUBLANES
    d_a = H_A * DK

    @pl.when(c == 0)
    def _():
        c_sc[...] = jnp.zeros_like(c_sc)
        n_sc[...] = jnp.zeros_like(n_sc)
        m_sc[...] = jnp.zeros_like(m_sc)
        ext_sc[0:pad, :] = jnp.zeros((pad, 2 * d_a), F32)

    x = qk_ref[...]
    ext_sc[pad:pad + L, :] = x
    y = bc_ref[...] + wc_ref[CONV_W - 1:CONV_W, :] * x
    for j in range(1, CONV_W):
        y = y + wc_ref[CONV_W - 1 - j:CONV_W - j, :] * ext_sc[pad - j:pad - j + L, :]
    ext_sc[0:pad, :] = x[L - pad:L, :]
    y = _silu(y)

    gcol = gcol_ref[...]
    grow = grow_ref[...]
    ri = lax.broadcasted_iota(jnp.int32, (L, L), 0)
    ci = lax.broadcasted_iota(jnp.int32, (L, L), 1)
    causal = ci <= ri
    tril = jnp.where(causal, 1.0, 0.0).astype(BF16)
    triu = jnp.where(ri <= ci, 1.0, 0.0).astype(BF16)
    b_cols = sum(_dot(tril, p) for p in _split3(_log_sigmoid(gcol)))
    b_rows = sum(_dot(p, triu) for p in _split3(_log_sigmoid(grow)))

    for h in range(H_A):
        sl = slice(h * DK, (h + 1) * DK)
        q = y[:, sl] * (DK ** -0.5)
        k = y[:, d_a + h * DK:d_a + (h + 1) * DK]
        ig_col = gcol[:, h:h + 1]
        ig_row = grow[h:h + 1, :]
        b_col = b_cols[:, H_A + h:H_A + h + 1]
        b_row = b_rows[H_A + h:H_A + h + 1, :]

        m_prev = m_sc[:, h:h + 1]
        d_log = jnp.where(causal, b_col - b_row + ig_row, -jnp.inf)
        inter_log = b_col + m_prev
        m_t = jnp.maximum(inter_log, jnp.max(d_log, axis=1, keepdims=True))
        qb = q.astype(BF16)
        kb = k.astype(BF16)
        vb = v_ref[:, sl].astype(BF16)
        s = _dot_nt(qb, kb) * jnp.exp(d_log - m_t)
        inter_w = jnp.exp(inter_log - m_t)
        c_prev = c_sc[h]
        n_prev = n_sc[h:h + 1, :]
        num = _dot(s.astype(BF16), vb) + inter_w * _dot(qb, c_prev.astype(BF16))
        nq = jnp.sum(s, axis=1, keepdims=True) + inter_w * jnp.sum(q * n_prev, axis=1, keepdims=True)
        den = jnp.maximum(jnp.abs(nq), jnp.exp(-m_t))
        hh = num / den
        hh = hh * _sigmoid(o_ref[:, sl])
        hh = hh * lax.rsqrt(jnp.mean(hh * hh, axis=1, keepdims=True) + EPS)
        h_ref[:, sl] = (hh * hg_ref[:, sl]).astype(h_ref.dtype)

        m_new = m_t[L - 1:L, :]
        b_last = b_col[L - 1:L, :]
        decay = jnp.exp(b_last + m_prev - m_new)
        w_end = jnp.exp(b_last - b_col + ig_col - m_new)
        kw = k * w_end
        c_sc[h] = decay * c_prev + _dot_tn(kw.astype(BF16), vb)
        n_sc[h:h + 1, :] = decay * n_prev + jnp.sum(kw, axis=0, keepdims=True)
        m_sc[:, h:h + 1] = m_new

    @pl.when(c == pl.num_programs(1) - 1)
    def _():
        c_out[...] = c_sc[...]
        n_out[...] = n_sc[...]
        m_out[...] = m_sc[...]


def _mlstm_prompt(qkvo, gates, gates_t, w_conv, b_conv, head_g, sample, batch, seq, t_all):
    nc = seq // CHUNK
    L = CHUNK
    d_a = H_A * DK
    dec = sample[0].shape[0]
    nb = dec // (batch * nc)
    assert nb * batch * nc == dec
    row = lambda b, c: b * nc + c
    svec = pl.BlockSpec((nb, H_A, DK), lambda b, c: (row(b, c), 0, 0))
    smat = pl.BlockSpec((nb, H_A, DK, DK), lambda b, c: (row(b, c), 0, 0, 0))
    in_specs = [
        pl.BlockSpec((L, 2 * d_a), lambda b, c: (row(b, c), 0)),
        pl.BlockSpec((L, d_a), lambda b, c: (row(b, c), 2)),
        pl.BlockSpec((L, d_a), lambda b, c: (row(b, c), 3)),
        pl.BlockSpec((L, LANES), lambda b, c: (row(b, c), 0)),
        pl.BlockSpec((2 * H_A, L), lambda b, c: (0, row(b, c))),
        pl.BlockSpec((CONV_W, 2 * d_a), lambda b, c: (0, 0)),
        pl.BlockSpec((1, 2 * d_a), lambda b, c: (0, 0)),
        pl.BlockSpec((1, d_a), lambda b, c: (0, 0)),
        svec, svec, svec, svec, smat,
    ]
    out_shape = (
        jax.ShapeDtypeStruct((t_all, d_a), BF16),
        jax.ShapeDtypeStruct((batch, H_A, DK, DK), F32),
        jax.ShapeDtypeStruct((batch, H_A, DK), F32),
        jax.ShapeDtypeStruct((batch, 1, LANES), F32),
        jax.ShapeDtypeStruct((dec, H_A, DK, DK), F32),
        jax.ShapeDtypeStruct((dec, H_A, DK), F32),
    )
    out_specs = (
        pl.BlockSpec((L, d_a), lambda b, c: (row(b, c), 0)),
        pl.BlockSpec((None, H_A, DK, DK), lambda b, c: (b, 0, 0, 0)),
        pl.BlockSpec((None, H_A, DK), lambda b, c: (b, 0, 0)),
        pl.BlockSpec((None, 1, LANES), lambda b, c: (b, 0, 0)),
        smat, svec,
    )
    return pl.pallas_call(
        _mlstm_kernel,
        out_shape=out_shape,
        grid=(batch, nc),
        in_specs=in_specs,
        out_specs=out_specs,
        scratch_shapes=[pltpu.VMEM((H_A, DK, DK), F32), pltpu.VMEM((H_A, DK), F32),
                        pltpu.VMEM((1, LANES), F32), pltpu.VMEM((SUBLANES + L, 2 * d_a), F32)],
        compiler_params=_cparams(("arbitrary", "arbitrary")),
        name="mlstm_prompt",
    )(qkvo, qkvo, qkvo, gates, gates_t, w_conv, b_conv, head_g, *sample)


def _lane_tile(x, reps):
    w = x.shape[1]
    ri = lax.broadcasted_iota(jnp.int32, (w, w * reps), 0)
    ci = lax.broadcasted_iota(jnp.int32, (w, w * reps), 1)
    rep = jnp.where(jnp.bitwise_and(ci, w - 1) == ri, 1.0, 0.0).astype(BF16)
    return _dot(x.astype(BF16), rep)


def _group_mask(rows, cols, row_per, col_per):
    ri = lax.broadcasted_iota(jnp.int32, (rows, cols), 0)
    ci = lax.broadcasted_iota(jnp.int32, (rows, cols), 1)
    return (jnp.right_shift(ri, int(math.log2(row_per))) == jnp.right_shift(ci, int(math.log2(col_per))))


def _s5_build_weights(bin_r, bin_i, kdt, aout_r, aout_i, winc_ref, wintra_ref, wout_ref):
    L = S5_CHUNK
    gpb = LANES // S5_GROUP
    m_inc = _group_mask(LANES, S5_CST, S5_GROUP, P_S5)
    m_lag = _group_mask(LANES, LANES, S5_GROUP, S5_GROUP)
    zero = jnp.zeros((LANES, LANES), BF16)
    lag = [jnp.where(m_lag, _lane_tile(kdt[d], gpb), 0.0).astype(BF16) for d in range(L)]
    for t in range(L):
        rows = slice(t * LANES, (t + 1) * LANES)
        d = L - 1 - t
        winc_ref[rows, 0:S5_CST] = jnp.where(m_inc, _lane_tile(bin_r[d], gpb), 0.0).astype(BF16)
        winc_ref[rows, S5_CST:2 * S5_CST] = jnp.where(m_inc, _lane_tile(bin_i[d], gpb), 0.0).astype(BF16)
        wout_ref[rows, 0:S5_CST] = jnp.where(m_inc, _lane_tile(aout_r[t], gpb), 0.0).astype(BF16)
        wout_ref[rows, S5_CST:2 * S5_CST] = jnp.where(m_inc, _lane_tile(aout_i[t], gpb), 0.0).astype(BF16)
        for t2 in range(L):
            wintra_ref[rows, t2 * LANES:(t2 + 1) * LANES] = lag[t2 - t] if t2 >= t else zero


def _s5_prompt_kernel(u_ref, binr_ref, bini_ref, kdt_ref, aoutr_ref, aouti_ref, lbr_ref, lbi_ref, d_ref,
                      ys_ref, sre_ref, sim_ref, x_sc, winc_ref, wintra_ref, wout_ref, *, batch, nchunk):
    L = S5_CHUNK
    nrow = batch * nchunk
    nst = S5_CST // LANES
    rstr = nchunk + SUBLANES
    _s5_build_weights(binr_ref, bini_ref, kdt_ref, aoutr_ref, aouti_ref, winc_ref, wintra_ref, wout_ref)
    u_t = [u_ref[pl.ds(t, nrow, stride=L), :] for t in range(L)]
    lhs = jnp.concatenate([a.astype(BF16) for a in u_t], axis=1)
    inc = _dot(lhs, winc_ref[...])
    for j in range(2 * nst):
        for b in range(batch):
            x_sc[j, b * rstr:b * rstr + nchunk, :] = inc[b * nchunk:(b + 1) * nchunk, j * LANES:(j + 1) * LANES]

    lbr = [jnp.broadcast_to(lbr_ref[:, j * LANES:(j + 1) * LANES], (batch, LANES)) for j in range(nst)]
    lbi = [jnp.broadcast_to(lbi_ref[:, j * LANES:(j + 1) * LANES], (batch, LANES)) for j in range(nst)]

    def scan_body(r, carry):
        rows = pl.ds(r, batch, stride=rstr)
        out = []
        for j in range(nst):
            xr, xi = carry[j]
            ir = x_sc[j, rows, :]
            ii = x_sc[nst + j, rows, :]
            x_sc[j, rows, :] = xr
            x_sc[nst + j, rows, :] = xi
            out.append((lbr[j] * xr - lbi[j] * xi + ir, lbr[j] * xi + lbi[j] * xr + ii))
        return tuple(out)

    z = jnp.zeros((batch, LANES), F32)
    fin = lax.fori_loop(0, nchunk, scan_body, tuple((z, z) for _ in range(nst)), unroll=4)
    for j in range(nst):
        sre_ref[:, j * LANES:(j + 1) * LANES] = fin[j][0]
        sim_ref[:, j * LANES:(j + 1) * LANES] = fin[j][1]

    xprev = jnp.concatenate(
        [jnp.concatenate([x_sc[j, b * rstr:b * rstr + nchunk, :] for b in range(batch)], axis=0)
         for j in range(2 * nst)], axis=1).astype(BF16)
    y = _dot(lhs, wintra_ref[...]) + _dot_nt(xprev, wout_ref[...])
    for t in range(L):
        yt = y[:, t * LANES:(t + 1) * LANES] + d_ref[...] * u_t[t]
        ys_ref[pl.ds(t, nrow, stride=L), :] = _gelu_tanh(yt).astype(ys_ref.dtype)


def _s5_prompt(proj2, u_col0, bin_r, bin_i, kdt, aout_r, aout_i, lb8r, lb8i, d_row, batch, seq, t_all):
    nchunk = seq // S5_CHUNK
    t_p = batch * seq
    kern = functools.partial(_s5_prompt_kernel, batch=batch, nchunk=nchunk)
    ub0 = u_col0 // LANES
    ncb = D_B // LANES
    kw = S5_CHUNK * LANES
    per_blk = lambda a: pl.BlockSpec((a.shape[0], None) + a.shape[2:], lambda g: (0, g, 0, 0))
    return pl.pallas_call(
        kern,
        out_shape=(jax.ShapeDtypeStruct((t_all, D_B), F32),
                   jax.ShapeDtypeStruct((batch, G_B * P_S5), F32),
                   jax.ShapeDtypeStruct((batch, G_B * P_S5), F32)),
        grid=(ncb,),
        in_specs=[pl.BlockSpec((t_p, LANES), lambda g: (0, ub0 + g)),
                  per_blk(bin_r), per_blk(bin_i), per_blk(kdt), per_blk(aout_r), per_blk(aout_i),
                  pl.BlockSpec((None, 1, S5_CST), lambda g: (g, 0, 0)),
                  pl.BlockSpec((None, 1, S5_CST), lambda g: (g, 0, 0)),
                  pl.BlockSpec((1, LANES), lambda g: (0, g))],
        out_specs=(pl.BlockSpec((t_p, LANES), lambda g: (0, g)),
                   pl.BlockSpec((batch, S5_CST), lambda g: (0, g)),
                   pl.BlockSpec((batch, S5_CST), lambda g: (0, g))),
        scratch_shapes=[pltpu.VMEM((2 * S5_CST // LANES, batch * (nchunk + SUBLANES), LANES), F32),
                        pltpu.VMEM((kw, 2 * S5_CST), BF16), pltpu.VMEM((kw, kw), BF16),
                        pltpu.VMEM((kw, 2 * S5_CST), BF16)],
        compiler_params=_cparams(("arbitrary",)),
        name="s5_prompt",
    )(proj2, bin_r, bin_i, kdt, aout_r, aout_i, lb8r, lb8i, d_row)


def _sample_pre_kernel(qk_ref, conv_ref, wc_ref, bc_ref, g_ref, m_ref, n_ref,
                       q_out, kw_out, a_out, s_out, den_out, conv_out, n_out, m_out):
    c2 = 2 * H_A * DK
    x_new = qk_ref[...]
    y = bc_ref[...] + wc_ref[CONV_W - 1:CONV_W, :] * x_new
    for j in range(CONV_W - 1):
        y = y + wc_ref[j:j + 1, :] * conv_ref[:, j * c2:(j + 1) * c2]
    y = _silu(y)
    conv_out[:, 0:(CONV_W - 2) * c2] = conv_ref[:, c2:(CONV_W - 1) * c2]
    conv_out[:, (CONV_W - 2) * c2:(CONV_W - 1) * c2] = x_new
    g = g_ref[...]
    bd = x_new.shape[0]
    m_cols = []
    for h in range(H_A):
        sl = slice(h * DK, (h + 1) * DK)
        q = y[:, sl] * (DK ** -0.5)
        k = y[:, H_A * DK + h * DK:H_A * DK + (h + 1) * DK]
        ig = g[:, h:h + 1]
        lf = _log_sigmoid(g[:, H_A + h:H_A + h + 1])
        m_prev = m_ref[:, h:h + 1]
        m_t = jnp.maximum(lf + m_prev, ig)
        a = jnp.exp(lf + m_prev - m_t)
        wgt = jnp.exp(ig - m_t)
        n_prev = n_ref[:, sl]
        s = jnp.sum(q * k, axis=1, keepdims=True) * wgt
        nq = s + a * jnp.sum(q * n_prev, axis=1, keepdims=True)
        den = jnp.maximum(jnp.abs(nq), jnp.exp(-m_t))
        kw = wgt * k
        q_out[:, sl] = q
        kw_out[:, sl] = kw
        a_out[:, sl] = jnp.broadcast_to(a, (bd, DK))
        s_out[:, sl] = jnp.broadcast_to(s, (bd, DK))
        den_out[:, sl] = jnp.broadcast_to(den, (bd, DK))
        n_out[:, sl] = a * n_prev + kw
        m_cols.append(m_t)
    lane = lax.broadcasted_iota(jnp.int32, (bd, LANES), 1)
    m_full = jnp.zeros((bd, LANES), F32)
    for h in range(H_A):
        m_full = jnp.where(lane == h, m_cols[h], m_full)
    m_out[...] = m_full


def _sample_pre(qkvo, conv_state, w_conv, b_conv, gates, m_state, n_state, dec, row_blk):
    c2 = 2 * H_A * DK
    d = H_A * DK
    full = lambda shape: pl.BlockSpec(shape, lambda i: (0,) * len(shape))
    rows = lambda: jax.ShapeDtypeStruct((dec, d), F32)
    return pl.pallas_call(
        _sample_pre_kernel,
        out_shape=(rows(), rows(), rows(), rows(), rows(),
                   jax.ShapeDtypeStruct((dec, (CONV_W - 1) * c2), F32), rows(),
                   jax.ShapeDtypeStruct((dec, LANES), F32)),
        grid=(1,),
        in_specs=[pl.BlockSpec((dec, c2), lambda i: (row_blk, 0)),
                  full((dec, (CONV_W - 1) * c2)), full((CONV_W, c2)), full((1, c2)),
                  pl.BlockSpec((dec, LANES), lambda i: (row_blk, 0)),
                  full((dec, H_A)), full((dec, d))],
        out_specs=(full((dec, d)), full((dec, d)), full((dec, d)), full((dec, d)), full((dec, d)),
                   full((dec, (CONV_W - 1) * c2)), full((dec, d)), full((dec, LANES))),
        compiler_params=_cparams(("arbitrary",)),
        name="sample_pre",
    )(qkvo, conv_state, w_conv, b_conv, gates, m_state, n_state)


def _sample_post_kernel(qc_ref, s_ref, a_ref, den_ref, vo_ref, hg_ref, hn_in, h_ref):
    del hn_in
    d = H_A * DK
    num = s_ref[...] * vo_ref[:, 0:d] + a_ref[...] * qc_ref[...]
    hh = num / den_ref[...]
    hh = hh * _sigmoid(vo_ref[:, d:2 * d])
    for h in range(H_A):
        sl = slice(h * DK, (h + 1) * DK)
        seg = hh[:, sl]
        seg = seg * lax.rsqrt(jnp.mean(seg * seg, axis=1, keepdims=True) + EPS)
        h_ref[:, sl] = (seg * hg_ref[:, sl]).astype(h_ref.dtype)


def _sample_post(qc, s_e, a_e, den_e, qkvo, head_g, hn_all, dec, row_blk):
    d = H_A * DK
    full = lambda shape: pl.BlockSpec(shape, lambda i: (0,) * len(shape))
    return pl.pallas_call(
        _sample_post_kernel,
        out_shape=jax.ShapeDtypeStruct(hn_all.shape, hn_all.dtype),
        grid=(1,),
        in_specs=[full((dec, d)), full((dec, d)), full((dec, d)), full((dec, d)),
                  pl.BlockSpec((dec, 2 * d), lambda i: (row_blk, 1)),
                  full((1, d)), pl.BlockSpec(memory_space=pl.ANY)],
        out_specs=pl.BlockSpec((dec, d), lambda i: (row_blk, 0)),
        input_output_aliases={6: 0},
        compiler_params=_cparams(("arbitrary",)),
        name="sample_post",
    )(qc, s_e, a_e, den_e, qkvo, head_g, hn_all)


def _s5_sample_kernel(u_ref, sr_ref, si_ref, br_ref, bi_ref, cr_ref, ci_ref, lbr_ref, lbi_ref, d_ref, ys_in,
                      ys_ref, sre_out, sim_out):
    del ys_in
    gpb = LANES // S5_GROUP
    mask = _group_mask(LANES, S5_CST, S5_GROUP, P_S5)
    expand = lambda blk: jnp.where(mask, _lane_tile(blk, gpb), 0.0)
    for g in range(D_B // LANES):
        ch = slice(g * LANES, (g + 1) * LANES)
        sl = slice(g * S5_CST, (g + 1) * S5_CST)
        u = u_ref[:, ch]
        bmat = jnp.concatenate([expand(br_ref[g]), expand(bi_ref[g])], axis=1).astype(BF16)
        bu = _dot(u.astype(BF16), bmat)
        lbr = lbr_ref[g]
        lbi = lbi_ref[g]
        sr = sr_ref[:, sl]
        si = si_ref[:, sl]
        xr = lbr * sr - lbi * si + bu[:, 0:S5_CST]
        xi = lbr * si + lbi * sr + bu[:, S5_CST:2 * S5_CST]
        sre_out[:, sl] = xr
        sim_out[:, sl] = xi
        x = jnp.concatenate([xr, xi], axis=1).astype(BF16)
        cmat = jnp.concatenate([expand(cr_ref[g]), -expand(ci_ref[g])], axis=1).astype(BF16)
        y = _dot_nt(x, cmat) + d_ref[:, ch] * u
        ys_ref[:, ch] = _gelu_tanh(y).astype(ys_ref.dtype)


def _s5_sample(proj2, u_col0, s_re, s_im, b_r, b_i, c_r, c_i, lbr, lbi, d_row, ys_all, dec, row_blk):
    full = lambda shape: pl.BlockSpec(shape, lambda i: (0,) * len(shape))
    n_state = G_B * P_S5
    ub0 = u_col0 // D_B
    params = (b_r, b_i, c_r, c_i, lbr, lbi)
    return pl.pallas_call(
        _s5_sample_kernel,
        out_shape=(jax.ShapeDtypeStruct(ys_all.shape, ys_all.dtype),
                   jax.ShapeDtypeStruct((dec, n_state), F32),
                   jax.ShapeDtypeStruct((dec, n_state), F32)),
        grid=(1,),
        in_specs=[pl.BlockSpec((dec, D_B), lambda i: (row_blk, ub0)),
                  full((dec, n_state)), full((dec, n_state))]
                 + [full(p.shape) for p in params]
                 + [full((1, D_B)), pl.BlockSpec(memory_space=pl.ANY)],
        out_specs=(pl.BlockSpec((dec, D_B), lambda i: (row_blk, 0)),
                   full((dec, n_state)), full((dec, n_state))),
        input_output_aliases={10: 0},
        compiler_params=_cparams(("arbitrary",)),
        name="s5_sample",
    )(proj2, s_re, s_im, *params, d_row, ys_all)


def _merge_kernel(x_ref, hn_ref, ys_ref, ga_ref, gb_ref, bglu_ref, gffn_ref, wrh_ref, wrl_ref, br_ref,
                  wpa_hbm, wglu_hbm, wpb_hbm, wout_hbm, *rest):
    n_alias = len(rest) - 9
    x1_ref, xn_ref, ids_ref, wts_ref = rest[n_alias:n_alias + 4]
    wpa, wglu, wpb, wout, sem = rest[n_alias + 4:]

    @pl.when(pl.program_id(0) == 0)
    def _():
        copies = [pltpu.make_async_copy(src, dst, sem.at[i])
                  for i, (src, dst) in enumerate(((wpa_hbm, wpa), (wglu_hbm, wglu),
                                                  (wpb_hbm, wpb), (wout_hbm, wout)))]
        for cp in copies:
            cp.start()
        for cp in copies:
            cp.wait()

    ya = _dot(hn_ref[...], wpa[...])
    ys = ys_ref[...]
    gate = _sigmoid(_dot(ys.astype(BF16), wglu[...]) + bglu_ref[...])
    yb = _dot((ys * gate).astype(BF16), wpb[...])
    z = _sigmoid(ga_ref[...]) * ya + _sigmoid(gb_ref[...]) * yb
    x1 = x_ref[...] + _dot(z.astype(BF16), wout[...])
    x1_ref[...] = x1
    xn = x1 * lax.rsqrt(jnp.mean(x1 * x1, axis=1, keepdims=True) + EPS) * gffn_ref[...]
    xn_ref[...] = xn
    xh = xn.astype(BF16)
    xl = (xn - xh.astype(F32)).astype(BF16)
    logits = _dot(xh, wrh_ref[...]) + _dot(xl, wrh_ref[...]) + _dot(xh, wrl_ref[...]) + br_ref[...]

    lane_i = lax.broadcasted_iota(jnp.int32, logits.shape, 1)
    lane = lane_i.astype(F32)
    neg = -jnp.inf
    big = float(1 << 20)
    gl = jnp.where(lane_i < N_GROUPS, logits, neg)
    gmax = jnp.max(gl, axis=1, keepdims=True)
    gsum = jnp.sum(jnp.exp(gl - gmax), axis=1, keepdims=True)
    gidx = jnp.min(jnp.where(gl == gmax, lane, big), axis=1, keepdims=True)
    pg_sel = 1.0 / gsum
    lo = N_GROUPS + gidx * EXP_PER_GROUP
    in_grp = (lane >= lo) & (lane < lo + EXP_PER_GROUP)
    el = jnp.where(in_grp, logits, neg)
    emax = jnp.max(el, axis=1, keepdims=True)
    ee = jnp.exp(el - emax)
    pe = ee / jnp.sum(ee, axis=1, keepdims=True)
    v0 = jnp.max(pe, axis=1, keepdims=True)
    i0 = jnp.min(jnp.where(in_grp & (pe == v0), lane, big), axis=1, keepdims=True)
    rest_m = in_grp & (lane != i0)
    pe1 = jnp.where(rest_m, pe, neg)
    v1 = jnp.max(pe1, axis=1, keepdims=True)
    i1 = jnp.min(jnp.where(rest_m & (pe1 == v1), lane, big), axis=1, keepdims=True)
    tot = v0 + v1
    w0 = pg_sel * (v0 / tot)
    w1 = pg_sel * (v1 / tot)
    ids = jnp.where(lane_i == 0, i0 - N_GROUPS, jnp.where(lane_i == 1, i1 - N_GROUPS, 0.0))
    ids_ref[...] = ids.astype(jnp.int32)
    wts_ref[...] = jnp.where(lane_i == 0, w0, jnp.where(lane_i == 1, w1, 0.0))


def _merge(x, hn_all, ys_all, proj2, b_glu, g_ffn, wr_hi, wr_lo, b_r, wpa, wglu, wpb, wout,
           t_all, row0, tm, aliases=None):
    n = x.shape[0]
    blk0 = row0 // tm
    const = lambda shape: pl.BlockSpec(shape, lambda i: (0,) * len(shape))
    any_spec = pl.BlockSpec(memory_space=pl.ANY)
    in_specs = [pl.BlockSpec((tm, D_MODEL), lambda i: (i, 0)),
                pl.BlockSpec((tm, D_MODEL), lambda i: (i + blk0, 0)),
                pl.BlockSpec((tm, D_B), lambda i: (i + blk0, 0)),
                pl.BlockSpec((tm, D_MODEL), lambda i: (i + blk0, 0)),
                pl.BlockSpec((tm, D_MODEL), lambda i: (i + blk0, 1)),
                const((1, D_B)), const((1, D_MODEL)),
                const((D_MODEL, LANES)), const((D_MODEL, LANES)), const((1, LANES)),
                any_spec, any_spec, any_spec, any_spec]
    args = [x, hn_all, ys_all, proj2, proj2, b_glu, g_ffn, wr_hi, wr_lo, b_r, wpa, wglu, wpb, wout]
    io_alias = {}
    if aliases is not None:
        n_in = len(args)
        for j, a in enumerate(aliases):
            in_specs.append(any_spec)
            args.append(a)
            io_alias[n_in + j] = j
    out_shape = (jax.ShapeDtypeStruct((t_all, D_MODEL), F32),
                 jax.ShapeDtypeStruct((t_all, D_MODEL), F32),
                 jax.ShapeDtypeStruct((t_all, LANES), jnp.int32),
                 jax.ShapeDtypeStruct((t_all, LANES), F32))
    out_specs = (pl.BlockSpec((tm, D_MODEL), lambda i: (i + blk0, 0)),
                 pl.BlockSpec((tm, D_MODEL), lambda i: (i + blk0, 0)),
                 pl.BlockSpec((tm, LANES), lambda i: (i + blk0, 0)),
                 pl.BlockSpec((tm, LANES), lambda i: (i + blk0, 0)))
    return pl.pallas_call(
        _merge_kernel,
        out_shape=out_shape,
        grid=(n // tm,),
        in_specs=in_specs,
        out_specs=out_specs,
        scratch_shapes=[pltpu.VMEM(wpa.shape, BF16), pltpu.VMEM(wglu.shape, BF16),
                        pltpu.VMEM(wpb.shape, BF16), pltpu.VMEM(wout.shape, BF16),
                        pltpu.SemaphoreType.DMA((4,))],
        input_output_aliases=io_alias,
        compiler_params=_cparams(("arbitrary",)),
        name="merge_router",
    )(*args)


def _moe_kernel(nt_ref, first_ref, ord_ref, elist_ref, nord_ref, xs_ref, wg_hbm, wu_hbm, wd_hbm, o_ref,
                stg_g, stg_u, stg_d, wg_sc, wu_sc, wd_sc, sem):
    i = pl.program_id(0)
    n_ord = nord_ref[0]

    def weight_copies(k, slot):
        e = elist_ref[k]
        return (pltpu.make_async_copy(wg_hbm.at[e], stg_g.at[slot], sem.at[slot, 0]),
                pltpu.make_async_copy(wu_hbm.at[e], stg_u.at[slot], sem.at[slot, 1]),
                pltpu.make_async_copy(wd_hbm.at[e], stg_d.at[slot], sem.at[slot, 2]))

    @pl.when(i == 0)
    def _():
        for cp in weight_copies(0, 0):
            cp.start()

        @pl.when(n_ord > 1)
        def _():
            for cp in weight_copies(1, 1):
                cp.start()

    valid = i < nt_ref[0]
    k = ord_ref[i]

    @pl.when(valid & (first_ref[i] == 1))
    def _():
        slot = k % 2
        for cp in weight_copies(k, slot):
            cp.wait()
        wg_sc[...] = stg_g[slot].astype(BF16)
        wu_sc[...] = stg_u[slot].astype(BF16)
        wd_sc[...] = stg_d[slot].astype(BF16)

        @pl.when(k + 2 < n_ord)
        def _():
            for cp in weight_copies(k + 2, slot):
                cp.start()

    @pl.when(valid)
    def _():
        x = xs_ref[...].astype(BF16)
        hg = _dot(x, wg_sc[...])
        hu = _dot(x, wu_sc[...])
        hh = (_silu(hg) * hu).astype(BF16)
        o_ref[...] = _dot(hh, wd_sc[...])


def _moe_experts(n_tiles, first, ordinal, elist, n_ord, xs, w_gate, w_up, w_down):
    p_rows = xs.shape[0]
    last = lambda i, nt, *_: jnp.minimum(i, nt[0] - 1)
    any_spec = pl.BlockSpec(memory_space=pl.ANY)
    grid_spec = pltpu.PrefetchScalarGridSpec(
        num_scalar_prefetch=5,
        grid=(p_rows // MOE_TILE,),
        in_specs=[pl.BlockSpec((MOE_TILE, D_MODEL), lambda i, *s: (last(i, *s), 0)),
                  any_spec, any_spec, any_spec],
        out_specs=pl.BlockSpec((MOE_TILE, D_MODEL), lambda i, *s: (last(i, *s), 0)),
        scratch_shapes=[pltpu.VMEM((2, D_MODEL, D_EXPERT), F32), pltpu.VMEM((2, D_MODEL, D_EXPERT), F32),
                        pltpu.VMEM((2, D_EXPERT, D_MODEL), F32),
                        pltpu.VMEM((D_MODEL, D_EXPERT), BF16), pltpu.VMEM((D_MODEL, D_EXPERT), BF16),
                        pltpu.VMEM((D_EXPERT, D_MODEL), BF16),
                        pltpu.SemaphoreType.DMA((2, 3))],
    )
    return pl.pallas_call(
        _moe_kernel,
        out_shape=jax.ShapeDtypeStruct((p_rows, D_MODEL), F32),
        grid_spec=grid_spec,
        compiler_params=_cparams(("arbitrary",)),
        name="moe_experts",
    )(n_tiles, first, ordinal, elist, n_ord, xs, w_gate, w_up, w_down)


def _final_kernel(x1_ref, y0_ref, y1_ref, w_ref, g_ref, o_ref):
    w = w_ref[...]
    x2 = x1_ref[...] + w[:, 0:1] * y0_ref[...] + w[:, 1:2] * y1_ref[...]
    o_ref[...] = x2 * lax.rsqrt(jnp.mean(x2 * x2, axis=1, keepdims=True) + EPS) * g_ref[...]


def _final(x1_all, yg0, yg1, wts, g_final, row0, n, tm):
    blk0 = row0 // tm
    rows = pl.BlockSpec((tm, D_MODEL), lambda i: (i + blk0, 0))
    return pl.pallas_call(
        _final_kernel,
        out_shape=jax.ShapeDtypeStruct((n, D_MODEL), F32),
        grid=(n // tm,),
        in_specs=[rows, rows, rows,
                  pl.BlockSpec((tm, LANES), lambda i: (i + blk0, 0)),
                  pl.BlockSpec((1, D_MODEL), lambda i: (0, 0))],
        out_specs=pl.BlockSpec((tm, D_MODEL), lambda i: (i, 0)),
        compiler_params=_cparams(("arbitrary",)),
        name="combine_final_norm",
    )(x1_all, yg0, yg1, wts, g_final)


def _s5_discretise(a_re, a_im, log_step, b_re, b_im):
    dt = jnp.exp(log_step)[:, None]
    mag = jnp.exp(a_re * dt)
    lb_re = mag * jnp.cos(a_im * dt)
    lb_im = mag * jnp.sin(a_im * dt)
    den = a_re * a_re + a_im * a_im
    nr = lb_re - 1.0
    coef_re = (nr * a_re + lb_im * a_im) / den
    coef_im = (lb_im * a_re - nr * a_im) / den
    bb_re = coef_re[..., None] * b_re - coef_im[..., None] * b_im
    bb_im = coef_re[..., None] * b_im + coef_im[..., None] * b_re
    return lb_re, lb_im, bb_re, bb_im


def _s5_chunk_params(a_re, a_im, log_step, b_re, b_im, c_re, c_im):
    lb_re, lb_im, bb_re, bb_im = _s5_discretise(a_re, a_im, log_step, b_re, b_im)
    L = S5_CHUNK
    gpb = LANES // S5_GROUP
    ncb = G_B // gpb
    pr, pi = [jnp.ones_like(lb_re)], [jnp.zeros_like(lb_re)]
    for _ in range(L):
        pr, pi = pr + [pr[-1] * lb_re - pi[-1] * lb_im], pi + [pr[-1] * lb_im + pi[-1] * lb_re]
    pw_r, pw_i = jnp.stack(pr), jnp.stack(pi)
    bt_re = bb_re.transpose(0, 2, 1)
    bt_im = bb_im.transpose(0, 2, 1)
    lbb_r = pw_r[:L, :, None, :] * bt_re - pw_i[:L, :, None, :] * bt_im
    lbb_i = pw_r[:L, :, None, :] * bt_im + pw_i[:L, :, None, :] * bt_re
    hp = lax.Precision.HIGHEST
    kdt = (jnp.einsum('gop,dgcp->dgco', c_re, lbb_r, precision=hp)
           - jnp.einsum('gop,dgcp->dgco', c_im, lbb_i, precision=hp))
    a_r = c_re * pw_r[1:, :, None, :] - c_im * pw_i[1:, :, None, :]
    a_i = -(c_re * pw_i[1:, :, None, :] + c_im * pw_r[1:, :, None, :])

    def blocks(m):
        return m.reshape(m.shape[:-3] + (ncb, gpb * m.shape[-2], m.shape[-1]))

    vec = lambda v: v.reshape(ncb, 1, S5_CST)
    prompt = (blocks(lbb_r), blocks(lbb_i), blocks(kdt), blocks(a_r), blocks(a_i), vec(pw_r[L]), vec(pw_i[L]))
    sample = (blocks(lbb_r[0]), blocks(lbb_i[0]), blocks(c_re), blocks(c_im), vec(lb_re), vec(lb_im))
    return prompt, sample


def _dispatch(ids, t_all, p_rows):
    e = ids[:, :2].reshape(-1)
    onehot = (e[:, None] == jnp.arange(N_EXPERTS, dtype=jnp.int32)[None, :]).astype(jnp.int32)
    csum = jnp.cumsum(onehot, axis=0)
    rank = jnp.sum((csum - onehot) * onehot, axis=1)
    counts = csum[-1]
    tiles = (counts + MOE_TILE - 1) // MOE_TILE
    tile_end = jnp.cumsum(tiles)
    tile_start = tile_end - tiles
    pos = jnp.sum(onehot * (tile_start * MOE_TILE)[None, :], axis=1) + rank
    tok = jnp.arange(2 * t_all, dtype=jnp.int32) // 2
    src = (jnp.arange(p_rows, dtype=jnp.int32) % t_all).at[pos].set(tok)
    n_tiles = tile_end[-1]
    tidx = jnp.arange(p_rows // MOE_TILE, dtype=jnp.int32)
    tclamp = jnp.minimum(tidx, n_tiles - 1)
    tile_expert = jnp.sum((tile_end[None, :] <= tclamp[:, None]).astype(jnp.int32), axis=1)
    present = (tiles > 0).astype(jnp.int32)
    ord_of_e = jnp.cumsum(present) - 1
    eids = jnp.arange(N_EXPERTS, dtype=jnp.int32)
    elist = jnp.sum(jnp.where((ord_of_e[None, :] == eids[:, None]) & (present[None, :] == 1), eids[None, :], 0), axis=1)
    ordinal = jnp.sum(jnp.where(tile_expert[:, None] == eids[None, :], ord_of_e[None, :], 0), axis=1)
    first = jnp.concatenate([jnp.ones((1,), jnp.int32),
                             (tile_expert[1:] != tile_expert[:-1]).astype(jnp.int32)])
    one = lambda v: v.reshape(1).astype(jnp.int32)
    meta = (one(n_tiles), first, ordinal.astype(jnp.int32), elist.astype(jnp.int32), one(jnp.sum(present)))
    return pos.reshape(t_all, 2), src, meta


def _pick_tile(n, candidates):
    for c in candidates:
        if n % c == 0:
            return c
    raise ValueError(f"no row tile for {n}")


def kernel(x_prompt, x_sample, state_mlstm_C, state_mlstm_n, state_mlstm_m, state_conv, state_s5_re,
           state_s5_im, norm_mix_g, w_in, b_i, b_f, w_conv, b_conv, head_norm_g, w_pa, s5_a_re, s5_a_im,
           s5_log_step, s5_b_re, s5_b_im, s5_c_re, s5_c_im, s5_d, s5_w_glu, s5_b_glu, w_pb, w_out,
           norm_ffn_g, w_rg, b_rg, w_rexp, b_rexp, w_gate, w_up, w_down, norm_final_g):
    assert state_mlstm_C.shape[0] == 1 and x_sample.shape[1] == 1
    batch, seq, _ = x_prompt.shape
    dec = x_sample.shape[0]
    t_p = batch * seq
    t_all = t_p + dec
    assert seq % CHUNK == 0 and t_p % dec == 0 and dec % LANES == 0
    d_a = H_A * DK
    tm_p = _pick_tile(t_p, (512, 256, 128))
    tm_all = _pick_tile(t_all, (1664, 640, 384, 128))
    sample_blk = t_p // dec

    xp = x_prompt.reshape(t_p, D_MODEL)
    xs = x_sample.reshape(dec, D_MODEL)

    g_mix = norm_mix_g[0]
    w_in_t = w_in.reshape(w_in.shape[1:]).T
    n_qkvo = 4 * d_a
    tn = 1024
    n_gate_cols = 2 * H_A
    b_gates = jnp.pad(jnp.concatenate([b_i[0], b_f[0]]), (0, LANES - n_gate_cols))
    xn_all, gates, gates_t = _rmsnorm_rows(xp, g_mix, w_in_t, n_qkvo, b_gates, t_all, 0, tm_p)
    xn_all, gates, _ = _rmsnorm_rows(xs, g_mix, w_in_t, n_qkvo, b_gates, t_all, t_p, dec,
                                     alias=(xn_all, gates))
    qkvo = _matmul_t(xn_all, w_in_t, [j * tn for j in range(n_qkvo // tn)], jnp.zeros((1, n_qkvo), F32),
                     tm_all, tn, F32)
    c_u = n_qkvo + n_gate_cols
    c_ga = c_u + D_B
    starts = [c_ga + j * tn for j in range(2 * D_MODEL // tn)] + [c_u]
    proj2 = _matmul_t(xn_all, w_in_t, starts, jnp.zeros((1, len(starts) * tn), F32), tm_all, tn, F32)
    u_col0 = 2 * D_MODEL

    conv_s_in = state_conv[0].reshape(dec, (CONV_W - 1) * 2 * d_a)
    q_s, kw_s, a_e, s_e, den_e, conv_s, n_s, m_s = _sample_pre(
        qkvo, conv_s_in, w_conv[0], b_conv[0].reshape(1, -1), gates, state_mlstm_m[0],
        state_mlstm_n[0].reshape(dec, d_a), dec, sample_blk)
    v_s = qkvo[t_p:, 2 * d_a:3 * d_a]
    r3 = lambda a: a.reshape(dec, H_A, DK)

    head_g = head_norm_g[0].reshape(1, d_a)
    hn_all, c_p, n_p, m_p, c_s, qc = _mlstm_prompt(
        qkvo, gates, gates_t, w_conv[0], b_conv[0].reshape(1, -1), head_g,
        (r3(q_s), r3(kw_s), r3(v_s), r3(a_e), state_mlstm_C[0]), batch, seq, t_all)
    conv_p = jnp.stack([qkvo[b * seq + seq - (CONV_W - 1):(b + 1) * seq, :2 * d_a] for b in range(batch)])
    hn_all = _sample_post(qc.reshape(dec, d_a), s_e, a_e, den_e, qkvo, head_g, hn_all, dec, sample_blk)

    d_row = s5_d[0].reshape(1, D_B)
    s5_prompt_w, s5_sample_w = _s5_chunk_params(s5_a_re[0], s5_a_im[0], s5_log_step[0], s5_b_re[0],
                                                s5_b_im[0], s5_c_re[0], s5_c_im[0])
    ys_all, s5re_p, s5im_p = _s5_prompt(proj2, u_col0, *s5_prompt_w, d_row, batch, seq, t_all)

    ys_all, s5re_s, s5im_s = _s5_sample(proj2, u_col0, state_s5_re[0].reshape(dec, -1),
                                        state_s5_im[0].reshape(dec, -1),
                                        *s5_sample_w, d_row, ys_all, dec, sample_blk)

    wr = jnp.pad(jnp.concatenate([w_rg[0], w_rexp[0]], axis=1), ((0, 0), (0, LANES - N_GROUPS - N_EXPERTS)))
    wr_hi = wr.astype(BF16)
    wr_lo = (wr - wr_hi.astype(F32)).astype(BF16)
    b_r = jnp.pad(jnp.concatenate([b_rg[0], b_rexp[0]]), (0, LANES - N_GROUPS - N_EXPERTS)).reshape(1, LANES)
    merge_w = (w_pa[0].astype(BF16), s5_w_glu[0].astype(BF16), w_pb[0].astype(BF16), w_out[0].astype(BF16))
    b_glu = s5_b_glu[0].reshape(1, D_B)
    g_ffn = norm_ffn_g[0].reshape(1, D_MODEL)
    tm_m = _pick_tile(t_p, (256, 128))
    outs = _merge(xp, hn_all, ys_all, proj2, b_glu, g_ffn, wr_hi, wr_lo, b_r, *merge_w, t_all, 0, tm_m)
    x1_all, xn2_all, ids, wts = _merge(xs, hn_all, ys_all, proj2, b_glu, g_ffn, wr_hi, wr_lo, b_r,
                                       *merge_w, t_all, t_p, dec, aliases=outs)

    p_rows = -(-(2 * t_all + N_EXPERTS * (MOE_TILE - 1)) // MOE_TILE) * MOE_TILE
    pos, src, meta = _dispatch(ids, t_all, p_rows)
    take_rows = lambda a, idx: a.at[idx].get(mode='promise_in_bounds')
    xs_sorted = take_rows(xn2_all, src)
    yp = _moe_experts(*meta, xs_sorted, w_gate[0], w_up[0], w_down[0])
    yg0 = take_rows(yp, pos[:, 0])
    yg1 = take_rows(yp, pos[:, 1])

    g_fin = norm_final_g.reshape(1, D_MODEL)
    y_prompt = _final(x1_all, yg0, yg1, wts, g_fin, 0, t_p, tm_p).reshape(batch, seq, D_MODEL)
    y_sample = _final(x1_all, yg0, yg1, wts, g_fin, t_p, dec, dec).reshape(dec, 1, D_MODEL)

    lead = lambda a, shape: a.reshape((1,) + shape)
    return (y_prompt, y_sample,
            lead(c_p, (batch, H_A, DK, DK)), lead(n_p, (batch, H_A, DK)), lead(m_p[:, 0, :H_A], (batch, H_A)),
            lead(conv_p, (batch, CONV_W - 1, 2 * d_a)),
            lead(s5re_p, (batch, G_B, P_S5)), lead(s5im_p, (batch, G_B, P_S5)),
            lead(c_s, (dec, H_A, DK, DK)), lead(n_s, (dec, H_A, DK)), lead(m_s[:, :H_A], (dec, H_A)),
            lead(conv_s, (dec, CONV_W - 1, 2 * d_a)),
            lead(s5re_s, (dec, G_B, P_S5)), lead(s5im_s, (dec, G_B, P_S5)))
```

```python
import functools
import math

import jax
import jax.numpy as jnp
from jax import lax
from jax.experimental import pallas as pl
from jax.experimental.pallas import tpu as pltpu

F32 = jnp.float32
BF16 = jnp.bfloat16

D_MODEL = 2048
H_A = 8
DK = 256
CONV_W = 4
CHUNK = 128
D_B = 1024
S5_GROUP = 16
G_B = 64
P_S5 = 64
N_GROUPS = 4
EXP_PER_GROUP = 8
N_EXPERTS = 32
D_EXPERT = 512
EPS = 1e-6

LANES = 128
SUBLANES = 8
VMEM_LIMIT = 56 * 1024 * 1024

S5_CHUNK = 8
S5_CST = (LANES // S5_GROUP) * P_S5
MOE_TILE = 256
MOE_RING = 3


def _cparams(sem):
    return pltpu.CompilerParams(dimension_semantics=sem, vmem_limit_bytes=VMEM_LIMIT)


def _silu(x):
    return x * (1.0 / (1.0 + jnp.exp(-x)))


def _sigmoid(x):
    return 1.0 / (1.0 + jnp.exp(-x))


def _log_sigmoid(x):
    return jnp.minimum(x, 0.0) - jnp.log1p(jnp.exp(-jnp.abs(x)))


def _gelu_tanh(x):
    c = math.sqrt(2.0 / math.pi)
    return 0.5 * x * (1.0 + jnp.tanh(c * (x + 0.044715 * (x * x * x))))


def _split3(x):
    hi = x.astype(BF16)
    r = x - hi.astype(F32)
    mid = r.astype(BF16)
    lo = (r - mid.astype(F32)).astype(BF16)
    return hi, mid, lo


def _dot(a, b):
    return jnp.dot(a, b, preferred_element_type=F32)


def _dot_nt(a, b):
    return lax.dot_general(a, b, (((1,), (1,)), ((), ())), preferred_element_type=F32)


def _dot_tn(a, b):
    return lax.dot_general(a, b, (((0,), (0,)), ((), ())), preferred_element_type=F32)


def _rmsnorm_kernel(x_ref, g_ref, wg_ref, bg_ref, bgt_ref, *rest):
    n_out = 3
    o_ref, gates_ref, gates_t_ref = rest[-n_out:]
    x = x_ref[...]
    r = lax.rsqrt(jnp.mean(x * x, axis=-1, keepdims=True) + EPS)
    xn = (x * r * g_ref[...]).astype(o_ref.dtype)
    o_ref[...] = xn
    wg = wg_ref[...].astype(BF16)
    gates_ref[...] = _dot_nt(xn, wg) + bg_ref[...]
    gt = _dot_nt(wg, xn) + bgt_ref[...]
    gates_t_ref[...] = gt[0:gates_t_ref.shape[0], :]


def _rmsnorm_rows(x, g, w_t, gate_row0, b_gates, t_all, row0, tm, alias=None):
    n = x.shape[0]
    blk0 = row0 // tm
    in_specs = [pl.BlockSpec((tm, D_MODEL), lambda i: (i, 0)),
                pl.BlockSpec((1, D_MODEL), lambda i: (0, 0)),
                pl.BlockSpec((pl.Element(LANES), pl.Element(D_MODEL)), lambda i: (gate_row0, 0)),
                pl.BlockSpec((1, LANES), lambda i: (0, 0)),
                pl.BlockSpec((LANES, 1), lambda i: (0, 0))]
    args = [x, g.reshape(1, D_MODEL), w_t, b_gates.reshape(1, LANES), b_gates.reshape(LANES, 1)]
    aliases = {}
    if alias is not None:
        for a in alias:
            aliases[len(args)] = len(aliases)
            in_specs.append(pl.BlockSpec(memory_space=pl.ANY))
            args.append(a)
    return pl.pallas_call(
        _rmsnorm_kernel,
        out_shape=(jax.ShapeDtypeStruct((t_all, D_MODEL), BF16),
                   jax.ShapeDtypeStruct((t_all, LANES), F32),
                   jax.ShapeDtypeStruct((2 * H_A, n), F32)),
        grid=(n // tm,),
        in_specs=in_specs,
        out_specs=(pl.BlockSpec((tm, D_MODEL), lambda i: (i + blk0, 0)),
                   pl.BlockSpec((tm, LANES), lambda i: (i + blk0, 0)),
                   pl.BlockSpec((2 * H_A, tm), lambda i: (0, i))),
        input_output_aliases=aliases,
        compiler_params=_cparams(("arbitrary",)),
        name="rmsnorm_rows",
    )(*args)


def _mm_t_kernel(starts_ref, a_ref, wt_ref, b_ref, o_ref, wb_ref):
    del starts_ref

    @pl.when(pl.program_id(1) == 0)
    def _():
        wb_ref[...] = wt_ref[...].astype(BF16)

    o_ref[...] = (_dot_nt(a_ref[...], wb_ref[...]) + b_ref[...]).astype(o_ref.dtype)


def _matmul_t(a, w_t, row_starts, bias, tm, tn, out_dtype):
    m, k = a.shape
    n_t = len(row_starts)
    assert all(s % SUBLANES == 0 for s in row_starts)
    grid_spec = pltpu.PrefetchScalarGridSpec(
        num_scalar_prefetch=1,
        grid=(n_t, m // tm),
        in_specs=[pl.BlockSpec((tm, k), lambda j, i, st: (i, 0)),
                  pl.BlockSpec((pl.Element(tn), pl.Element(k)), lambda j, i, st: (st[j] * SUBLANES, 0)),
                  pl.BlockSpec((1, tn), lambda j, i, st: (0, j))],
        out_specs=pl.BlockSpec((tm, tn), lambda j, i, st: (i, j)),
        scratch_shapes=[pltpu.VMEM((tn, k), BF16)],
    )
    return pl.pallas_call(
        _mm_t_kernel,
        out_shape=jax.ShapeDtypeStruct((m, n_t * tn), out_dtype),
        grid_spec=grid_spec,
        compiler_params=_cparams(("arbitrary", "arbitrary")),
        name="rows_matmul_t",
    )(jnp.asarray([s // SUBLANES for s in row_starts], jnp.int32), a, w_t, bias)


def _sample_c_update(q_ref, kw_ref, v_ref, a_ref, c_ref, c_out, qc_out):
    rows = 2 * SUBLANES
    rid = lax.broadcasted_iota(jnp.int32, (rows, DK), 0)
    pad = jnp.zeros((rows - H_A, DK), F32)
    for b in range(q_ref.shape[0]):
        q16 = jnp.concatenate([q_ref[b], pad], axis=0).astype(BF16)
        kw16 = jnp.concatenate([kw_ref[b], pad], axis=0)
        v16 = jnp.concatenate([v_ref[b], pad], axis=0).astype(BF16)
        qc_rows = []
        for h in range(H_A):
            c_prev = c_ref[b, h]
            qc_rows.append(_dot(q16, c_prev.astype(BF16))[h:h + 1, :])
            kw_h = jnp.where(rid == h, kw16, 0.0).astype(BF16)
            c_out[b, h] = a_ref[b, h:h + 1, :] * c_prev + _dot_tn(kw_h, v16)
        qc_out[b] = jnp.concatenate(qc_rows, axis=0)


def _mlstm_kernel(qk_ref, v_ref, o_ref, gcol_ref, grow_ref, wc_ref, bc_ref, hg_ref,
                  sq_ref, skw_ref, sv_ref, sa_ref, sc_ref,
                  h_ref, c_out, n_out, m_out, sc_out, sqc_out, c_sc, n_sc, m_sc, ext_sc):
    c = pl.program_id(1)
    _sample_c_update(sq_ref, skw_ref, sv_ref, sa_ref, sc_ref, sc_out, sqc_out)
    L = CHUNK
    pad = SUBLANES
    d_a = H_A * DK

    @pl.when(c == 0)
    def _():
        c_sc[...] = jnp.zeros_like(c_sc)
        n_sc[...] = jnp.zeros_like(n_sc)
        m_sc[...] = jnp.zeros_like(m_sc)
        ext_sc[0:pad, :] = jnp.zeros((pad, 2 * d_a), F32)

    x = qk_ref[...]
    ext_sc[pad:pad + L, :] = x
    y = bc_ref[...] + wc_ref[CONV_W - 1:CONV_W, :] * x
    for j in range(1, CONV_W):
        y = y + wc_ref[CONV_W - 1 - j:CONV_W - j, :] * ext_sc[pad - j:pad - j + L, :]
    ext_sc[0:pad, :] = x[L - pad:L, :]
    y = _silu(y)

    gcol = gcol_ref[...]
    grow = grow_ref[...]
    ri = lax.broadcasted_iota(jnp.int32, (L, L), 0)
    ci = lax.broadcasted_iota(jnp.int32, (L, L), 1)
    causal = ci <= ri
    tril = jnp.where(causal, 1.0, 0.0).astype(BF16)
    triu = jnp.where(ri <= ci, 1.0, 0.0).astype(BF16)
    b_cols = sum(_dot(tril, p) for p in _split3(_log_sigmoid(gcol)))
    b_rows = sum(_dot(p, triu) for p in _split3(_log_sigmoid(grow)))

    for h in range(H_A):
        sl = slice(h * DK, (h + 1) * DK)
        q = y[:, sl] * (DK ** -0.5)
        k = y[:, d_a + h * DK:d_a + (h + 1) * DK]
        ig_col = gcol[:, h:h + 1]
        ig_row = grow[h:h + 1, :]
        b_col = b_cols[:, H_A + h:H_A + h + 1]
        b_row = b_rows[H_A + h:H_A + h + 1, :]

        m_prev = m_sc[:, h:h + 1]
        d_log = jnp.where(causal, b_col - b_row + ig_row, -jnp.inf)
        inter_log = b_col + m_prev
        m_t = jnp.maximum(inter_log, jnp.max(d_log, axis=1, keepdims=True))
        qb = q.astype(BF16)
        kb = k.astype(BF16)
        vb = v_ref[:, sl].astype(BF16)
        s = _dot_nt(qb, kb) * jnp.exp(d_log - m_t)
        inter_w = jnp.exp(inter_log - m_t)
        c_prev = c_sc[h]
        n_prev = n_sc[h:h + 1, :]
        num = _dot(s.astype(BF16), vb) + inter_w * _dot(qb, c_prev.astype(BF16))
        nq = jnp.sum(s, axis=1, keepdims=True) + inter_w * jnp.sum(q * n_prev, axis=1, keepdims=True)
        den = jnp.maximum(jnp.abs(nq), jnp.exp(-m_t))
        hh = num / den
        hh = hh * _sigmoid(o_ref[:, sl])
        hh = hh * lax.rsqrt(jnp.mean(hh * hh, axis=1, keepdims=True) + EPS)
        h_ref[:, sl] = (hh * hg_ref[:, sl]).astype(h_ref.dtype)

        m_new = m_t[L - 1:L, :]
        b_last = b_col[L - 1:L, :]
        decay = jnp.exp(b_last + m_prev - m_new)
        w_end = jnp.exp(b_last - b_col + ig_col - m_new)
        kw = k * w_end
        c_sc[h] = decay * c_prev + _dot_tn(kw.astype(BF16), vb)
        n_sc[h:h + 1, :] = decay * n_prev + jnp.sum(kw, axis=0, keepdims=True)
        m_sc[:, h:h + 1] = m_new

    @pl.when(c == pl.num_programs(1) - 1)
    def _():
        c_out[...] = c_sc[...]
        n_out[...] = n_sc[...]
        m_out[...] = m_sc[...]


def _mlstm_prompt(qkvo, gates, gates_t, w_conv, b_conv, head_g, sample, batch, seq, t_all):
    nc = seq // CHUNK
    L = CHUNK
    d_a = H_A * DK
    dec = sample[0].shape[0]
    nb = dec // (batch * nc)
    assert nb * batch * nc == dec
    row = lambda b, c: b * nc + c
    svec = pl.BlockSpec((nb, H_A, DK), lambda b, c: (row(b, c), 0, 0))
    smat = pl.BlockSpec((nb, H_A, DK, DK), lambda b, c: (row(b, c), 0, 0, 0))
    in_specs = [
        pl.BlockSpec((L, 2 * d_a), lambda b, c: (row(b, c), 0)),
        pl.BlockSpec((L, d_a), lambda b, c: (row(b, c), 2)),
        pl.BlockSpec((L, d_a), lambda b, c: (row(b, c), 3)),
        pl.BlockSpec((L, LANES), lambda b, c: (row(b, c), 0)),
        pl.BlockSpec((2 * H_A, L), lambda b, c: (0, row(b, c))),
        pl.BlockSpec((CONV_W, 2 * d_a), lambda b, c: (0, 0)),
        pl.BlockSpec((1, 2 * d_a), lambda b, c: (0, 0)),
        pl.BlockSpec((1, d_a), lambda b, c: (0, 0)),
        svec, svec, svec, svec, smat,
    ]
    out_shape = (
        jax.ShapeDtypeStruct((t_all, d_a), BF16),
        jax.ShapeDtypeStruct((batch, H_A, DK, DK), F32),
        jax.ShapeDtypeStruct((batch, H_A, DK), F32),
        jax.ShapeDtypeStruct((batch, 1, LANES), F32),
        jax.ShapeDtypeStruct((dec, H_A, DK, DK), F32),
        jax.ShapeDtypeStruct((dec, H_A, DK), F32),
    )
    out_specs = (
        pl.BlockSpec((L, d_a), lambda b, c: (row(b, c), 0)),
        pl.BlockSpec((None, H_A, DK, DK), lambda b, c: (b, 0, 0, 0)),
        pl.BlockSpec((None, H_A, DK), lambda b, c: (b, 0, 0)),
        pl.BlockSpec((None, 1, LANES), lambda b, c: (b, 0, 0)),
        smat, svec,
    )
    return pl.pallas_call(
        _mlstm_kernel,
        out_shape=out_shape,
        grid=(batch, nc),
        in_specs=in_specs,
        out_specs=out_specs,
        scratch_shapes=[pltpu.VMEM((H_A, DK, DK), F32), pltpu.VMEM((H_A, DK), F32),
                        pltpu.VMEM((1, LANES), F32), pltpu.VMEM((SUBLANES + L, 2 * d_a), F32)],
        compiler_params=_cparams(("arbitrary", "arbitrary")),
        name="mlstm_prompt",
    )(qkvo, qkvo, qkvo, gates, gates_t, w_conv, b_conv, head_g, *sample)


def _lane_tile(x, reps):
    w = x.shape[1]
    ri = lax.broadcasted_iota(jnp.int32, (w, w * reps), 0)
    ci = lax.broadcasted_iota(jnp.int32, (w, w * reps), 1)
    rep = jnp.where(jnp.bitwise_and(ci, w - 1) == ri, 1.0, 0.0).astype(BF16)
    return _dot(x.astype(BF16), rep)


def _group_mask(rows, cols, row_per, col_per):
    ri = lax.broadcasted_iota(jnp.int32, (rows, cols), 0)
    ci = lax.broadcasted_iota(jnp.int32, (rows, cols), 1)
    return (jnp.right_shift(ri, int(math.log2(row_per))) == jnp.right_shift(ci, int(math.log2(col_per))))


def _s5_build_weights(bin_r, bin_i, kdt, aout_r, aout_i, winc_ref, wintra_ref, wout_ref):
    L = S5_CHUNK
    gpb = LANES // S5_GROUP
    m_inc = _group_mask(LANES, S5_CST, S5_GROUP, P_S5)
    m_lag = _group_mask(LANES, LANES, S5_GROUP, S5_GROUP)
    zero = jnp.zeros((LANES, LANES), BF16)
    lag = [jnp.where(m_lag, _lane_tile(kdt[d], gpb), 0.0).astype(BF16) for d in range(L)]
    for t in range(L):
        rows = slice(t * LANES, (t + 1) * LANES)
        d = L - 1 - t
        winc_ref[rows, 0:S5_CST] = jnp.where(m_inc, _lane_tile(bin_r[d], gpb), 0.0).astype(BF16)
        winc_ref[rows, S5_CST:2 * S5_CST] = jnp.where(m_inc, _lane_tile(bin_i[d], gpb), 0.0).astype(BF16)
        wout_ref[rows, 0:S5_CST] = jnp.where(m_inc, _lane_tile(aout_r[t], gpb), 0.0).astype(BF16)
        wout_ref[rows, S5_CST:2 * S5_CST] = jnp.where(m_inc, _lane_tile(aout_i[t], gpb), 0.0).astype(BF16)
        for t2 in range(L):
            wintra_ref[rows, t2 * LANES:(t2 + 1) * LANES] = lag[t2 - t] if t2 >= t else zero


def _s5_prompt_kernel(u_ref, binr_ref, bini_ref, kdt_ref, aoutr_ref, aouti_ref, lbr_ref, lbi_ref, d_ref,
                      ys_ref, sre_ref, sim_ref, x_sc, winc_ref, wintra_ref, wout_ref, *, batch, nchunk):
    L = S5_CHUNK
    nrow = batch * nchunk
    nst = S5_CST // LANES
    rstr = nchunk + SUBLANES
    _s5_build_weights(binr_ref, bini_ref, kdt_ref, aoutr_ref, aouti_ref, winc_ref, wintra_ref, wout_ref)
    u_t = [u_ref[pl.ds(t, nrow, stride=L), :] for t in range(L)]
    lhs = jnp.concatenate([a.astype(BF16) for a in u_t], axis=1)
    inc = _dot(lhs, winc_ref[...])
    for j in range(2 * nst):
        for b in range(batch):
            x_sc[j, b * rstr:b * rstr + nchunk, :] = inc[b * nchunk:(b + 1) * nchunk, j * LANES:(j + 1) * LANES]

    lbr = [jnp.broadcast_to(lbr_ref[:, j * LANES:(j + 1) * LANES], (batch, LANES)) for j in range(nst)]
    lbi = [jnp.broadcast_to(lbi_ref[:, j * LANES:(j + 1) * LANES], (batch, LANES)) for j in range(nst)]

    def scan_body(r, carry):
        rows = pl.ds(r, batch, stride=rstr)
        out = []
        for j in range(nst):
            xr, xi = carry[j]
            ir = x_sc[j, rows, :]
            ii = x_sc[nst + j, rows, :]
            x_sc[j, rows, :] = xr
            x_sc[nst + j, rows, :] = xi
            out.append((lbr[j] * xr - lbi[j] * xi + ir, lbr[j] * xi + lbi[j] * xr + ii))
        return tuple(out)

    z = jnp.zeros((batch, LANES), F32)
    fin = lax.fori_loop(0, nchunk, scan_body, tuple((z, z) for _ in range(nst)), unroll=4)
    for j in range(nst):
        sre_ref[:, j * LANES:(j + 1) * LANES] = fin[j][0]
        sim_ref[:, j * LANES:(j + 1) * LANES] = fin[j][1]

    xprev = jnp.concatenate(
        [jnp.concatenate([x_sc[j, b * rstr:b * rstr + nchunk, :] for b in range(batch)], axis=0)
         for j in range(2 * nst)], axis=1).astype(BF16)
    y = _dot(lhs, wintra_ref[...]) + _dot_nt(xprev, wout_ref[...])
    for t in range(L):
        yt = y[:, t * LANES:(t + 1) * LANES] + d_ref[...] * u_t[t]
        ys_ref[pl.ds(t, nrow, stride=L), :] = _gelu_tanh(yt).astype(ys_ref.dtype)


def _s5_prompt(proj2, u_col0, bin_r, bin_i, kdt, aout_r, aout_i, lb8r, lb8i, d_row, batch, seq, t_all):
    nchunk = seq // S5_CHUNK
    t_p = batch * seq
    kern = functools.partial(_s5_prompt_kernel, batch=batch, nchunk=nchunk)
    ub0 = u_col0 // LANES
    ncb = D_B // LANES
    kw = S5_CHUNK * LANES
    per_blk = lambda a: pl.BlockSpec((a.shape[0], None) + a.shape[2:], lambda g: (0, g, 0, 0))
    return pl.pallas_call(
        kern,
        out_shape=(jax.ShapeDtypeStruct((t_all, D_B), F32),
                   jax.ShapeDtypeStruct((batch, G_B * P_S5), F32),
                   jax.ShapeDtypeStruct((batch, G_B * P_S5), F32)),
        grid=(ncb,),
        in_specs=[pl.BlockSpec((t_p, LANES), lambda g: (0, ub0 + g)),
                  per_blk(bin_r), per_blk(bin_i), per_blk(kdt), per_blk(aout_r), per_blk(aout_i),
                  pl.BlockSpec((None, 1, S5_CST), lambda g: (g, 0, 0)),
                  pl.BlockSpec((None, 1, S5_CST), lambda g: (g, 0, 0)),
                  pl.BlockSpec((1, LANES), lambda g: (0, g))],
        out_specs=(pl.BlockSpec((t_p, LANES), lambda g: (0, g)),
                   pl.BlockSpec((batch, S5_CST), lambda g: (0, g)),
                   pl.BlockSpec((batch, S5_CST), lambda g: (0, g))),
        scratch_shapes=[pltpu.VMEM((2 * S5_CST // LANES, batch * (nchunk + SUBLANES), LANES), F32),
                        pltpu.VMEM((kw, 2 * S5_CST), BF16), pltpu.VMEM((kw, kw), BF16),
                        pltpu.VMEM((kw, 2 * S5_CST), BF16)],
        compiler_params=_cparams(("arbitrary",)),
        name="s5_prompt",
    )(proj2, bin_r, bin_i, kdt, aout_r, aout_i, lb8r, lb8i, d_row)


def _sample_pre_kernel(qk_ref, conv_ref, wc_ref, bc_ref, g_ref, m_ref, n_ref,
                       q_out, kw_out, a_out, s_out, den_out, conv_out, n_out, m_out):
    c2 = 2 * H_A * DK
    x_new = qk_ref[...]
    y = bc_ref[...] + wc_ref[CONV_W - 1:CONV_W, :] * x_new
    for j in range(CONV_W - 1):
        y = y + wc_ref[j:j + 1, :] * conv_ref[:, j * c2:(j + 1) * c2]
    y = _silu(y)
    conv_out[:, 0:(CONV_W - 2) * c2] = conv_ref[:, c2:(CONV_W - 1) * c2]
    conv_out[:, (CONV_W - 2) * c2:(CONV_W - 1) * c2] = x_new
    g = g_ref[...]
    bd = x_new.shape[0]
    m_cols = []
    for h in range(H_A):
        sl = slice(h * DK, (h + 1) * DK)
        q = y[:, sl] * (DK ** -0.5)
        k = y[:, H_A * DK + h * DK:H_A * DK + (h + 1) * DK]
        ig = g[:, h:h + 1]
        lf = _log_sigmoid(g[:, H_A + h:H_A + h + 1])
        m_prev = m_ref[:, h:h + 1]
        m_t = jnp.maximum(lf + m_prev, ig)
        a = jnp.exp(lf + m_prev - m_t)
        wgt = jnp.exp(ig - m_t)
        n_prev = n_ref[:, sl]
        s = jnp.sum(q * k, axis=1, keepdims=True) * wgt
        nq = s + a * jnp.sum(q * n_prev, axis=1, keepdims=True)
        den = jnp.maximum(jnp.abs(nq), jnp.exp(-m_t))
        kw = wgt * k
        q_out[:, sl] = q
        kw_out[:, sl] = kw
        a_out[:, sl] = jnp.broadcast_to(a, (bd, DK))
        s_out[:, sl] = jnp.broadcast_to(s, (bd, DK))
        den_out[:, sl] = jnp.broadcast_to(den, (bd, DK))
        n_out[:, sl] = a * n_prev + kw
        m_cols.append(m_t)
    lane = lax.broadcasted_iota(jnp.int32, (bd, LANES), 1)
    m_full = jnp.zeros((bd, LANES), F32)
    for h in range(H_A):
        m_full = jnp.where(lane == h, m_cols[h], m_full)
    m_out[...] = m_full


def _sample_pre(qkvo, conv_state, w_conv, b_conv, gates, m_state, n_state, dec, row_blk):
    c2 = 2 * H_A * DK
    d = H_A * DK
    full = lambda shape: pl.BlockSpec(shape, lambda i: (0,) * len(shape))
    rows = lambda: jax.ShapeDtypeStruct((dec, d), F32)
    return pl.pallas_call(
        _sample_pre_kernel,
        out_shape=(rows(), rows(), rows(), rows(), rows(),
                   jax.ShapeDtypeStruct((dec, (CONV_W - 1) * c2), F32), rows(),
                   jax.ShapeDtypeStruct((dec, LANES), F32)),
        grid=(1,),
        in_specs=[pl.BlockSpec((dec, c2), lambda i: (row_blk, 0)),
                  full((dec, (CONV_W - 1) * c2)), full((CONV_W, c2)), full((1, c2)),
                  pl.BlockSpec((dec, LANES), lambda i: (row_blk, 0)),
                  full((dec, H_A)), full((dec, d))],
        out_specs=(full((dec, d)), full((dec, d)), full((dec, d)), full((dec, d)), full((dec, d)),
                   full((dec, (CONV_W - 1) * c2)), full((dec, d)), full((dec, LANES))),
        compiler_params=_cparams(("arbitrary",)),
        name="sample_pre",
    )(qkvo, conv_state, w_conv, b_conv, gates, m_state, n_state)


def _sample_post_kernel(qc_ref, s_ref, a_ref, den_ref, vo_ref, hg_ref, hn_in, h_ref):
    del hn_in
    d = H_A * DK
    num = s_ref[...] * vo_ref[:, 0:d] + a_ref[...] * qc_ref[...]
    hh = num / den_ref[...]
    hh = hh * _sigmoid(vo_ref[:, d:2 * d])
    for h in range(H_A):
        sl = slice(h * DK, (h + 1) * DK)
        seg = hh[:, sl]
        seg = seg * lax.rsqrt(jnp.mean(seg * seg, axis=1, keepdims=True) + EPS)
        h_ref[:, sl] = (seg * hg_ref[:, sl]).astype(h_ref.dtype)


def _sample_post(qc, s_e, a_e, den_e, qkvo, head_g, hn_all, dec, row_blk):
    d = H_A * DK
    full = lambda shape: pl.BlockSpec(shape, lambda i: (0,) * len(shape))
    return pl.pallas_call(
        _sample_post_kernel,
        out_shape=jax.ShapeDtypeStruct(hn_all.shape, hn_all.dtype),
        grid=(1,),
        in_specs=[full((dec, d)), full((dec, d)), full((dec, d)), full((dec, d)),
                  pl.BlockSpec((dec, 2 * d), lambda i: (row_blk, 1)),
                  full((1, d)), pl.BlockSpec(memory_space=pl.ANY)],
        out_specs=pl.BlockSpec((dec, d), lambda i: (row_blk, 0)),
        input_output_aliases={6: 0},
        compiler_params=_cparams(("arbitrary",)),
        name="sample_post",
    )(qc, s_e, a_e, den_e, qkvo, head_g, hn_all)


def _s5_sample_kernel(u_ref, sr_ref, si_ref, br_ref, bi_ref, cr_ref, ci_ref, lbr_ref, lbi_ref, d_ref, ys_in,
                      ys_ref, sre_out, sim_out):
    del ys_in
    gpb = LANES // S5_GROUP
    mask = _group_mask(LANES, S5_CST, S5_GROUP, P_S5)
    expand = lambda blk: jnp.where(mask, _lane_tile(blk, gpb), 0.0)
    for g in range(D_B // LANES):
        ch = slice(g * LANES, (g + 1) * LANES)
        sl = slice(g * S5_CST, (g + 1) * S5_CST)
        u = u_ref[:, ch]
        bmat = jnp.concatenate([expand(br_ref[g]), expand(bi_ref[g])], axis=1).astype(BF16)
        bu = _dot(u.astype(BF16), bmat)
        lbr = lbr_ref[g]
        lbi = lbi_ref[g]
        sr = sr_ref[:, sl]
        si = si_ref[:, sl]
        xr = lbr * sr - lbi * si + bu[:, 0:S5_CST]
        xi = lbr * si + lbi * sr + bu[:, S5_CST:2 * S5_CST]
        sre_out[:, sl] = xr
        sim_out[:, sl] = xi
        x = jnp.concatenate([xr, xi], axis=1).astype(BF16)
        cmat = jnp.concatenate([expand(cr_ref[g]), -expand(ci_ref[g])], axis=1).astype(BF16)
        y = _dot_nt(x, cmat) + d_ref[:, ch] * u
        ys_ref[:, ch] = _gelu_tanh(y).astype(ys_ref.dtype)


def _s5_sample(proj2, u_col0, s_re, s_im, b_r, b_i, c_r, c_i, lbr, lbi, d_row, ys_all, dec, row_blk):
    full = lambda shape: pl.BlockSpec(shape, lambda i: (0,) * len(shape))
    n_state = G_B * P_S5
    ub0 = u_col0 // D_B
    params = (b_r, b_i, c_r, c_i, lbr, lbi)
    return pl.pallas_call(
        _s5_sample_kernel,
        out_shape=(jax.ShapeDtypeStruct(ys_all.shape, ys_all.dtype),
                   jax.ShapeDtypeStruct((dec, n_state), F32),
                   jax.ShapeDtypeStruct((dec, n_state), F32)),
        grid=(1,),
        in_specs=[pl.BlockSpec((dec, D_B), lambda i: (row_blk, ub0)),
                  full((dec, n_state)), full((dec, n_state))]
                 + [full(p.shape) for p in params]
                 + [full((1, D_B)), pl.BlockSpec(memory_space=pl.ANY)],
        out_specs=(pl.BlockSpec((dec, D_B), lambda i: (row_blk, 0)),
                   full((dec, n_state)), full((dec, n_state))),
        input_output_aliases={10: 0},
        compiler_params=_cparams(("arbitrary",)),
        name="s5_sample",
    )(proj2, s_re, s_im, *params, d_row, ys_all)


def _merge_kernel(x_ref, hn_ref, ys_ref, ga_ref, gb_ref, bglu_ref, gffn_ref, wrh_ref, wrl_ref, br_ref,
                  wpa_hbm, wglu_hbm, wpb_hbm, wout_hbm, *rest):
    n_alias = len(rest) - 9
    x1_ref, xn_ref, ids_ref, wts_ref = rest[n_alias:n_alias + 4]
    wpa, wglu, wpb, wout, sem = rest[n_alias + 4:]

    @pl.when(pl.program_id(0) == 0)
    def _():
        copies = [pltpu.make_async_copy(src, dst, sem.at[i])
                  for i, (src, dst) in enumerate(((wpa_hbm, wpa), (wglu_hbm, wglu),
                                                  (wpb_hbm, wpb), (wout_hbm, wout)))]
        for cp in copies:
            cp.start()
        for cp in copies:
            cp.wait()

    ya = _dot(hn_ref[...], wpa[...])
    ys = ys_ref[...]
    gate = _sigmoid(_dot(ys.astype(BF16), wglu[...]) + bglu_ref[...])
    yb = _dot((ys * gate).astype(BF16), wpb[...])
    z = _sigmoid(ga_ref[...]) * ya + _sigmoid(gb_ref[...]) * yb
    x1 = x_ref[...] + _dot(z.astype(BF16), wout[...])
    x1_ref[...] = x1
    xn = x1 * lax.rsqrt(jnp.mean(x1 * x1, axis=1, keepdims=True) + EPS) * gffn_ref[...]
    xn_ref[...] = xn
    xh = xn.astype(BF16)
    xl = (xn - xh.astype(F32)).astype(BF16)
    logits = _dot(xh, wrh_ref[...]) + _dot(xl, wrh_ref[...]) + _dot(xh, wrl_ref[...]) + br_ref[...]

    lane_i = lax.broadcasted_iota(jnp.int32, logits.shape, 1)
    lane = lane_i.astype(F32)
    neg = -jnp.inf
    big = float(1 << 20)
    gl = jnp.where(lane_i < N_GROUPS, logits, neg)
    gmax = jnp.max(gl, axis=1, keepdims=True)
    gsum = jnp.sum(jnp.exp(gl - gmax), axis=1, keepdims=True)
    gidx = jnp.min(jnp.where(gl == gmax, lane, big), axis=1, keepdims=True)
    pg_sel = 1.0 / gsum
    lo = N_GROUPS + gidx * EXP_PER_GROUP
    in_grp = (lane >= lo) & (lane < lo + EXP_PER_GROUP)
    el = jnp.where(in_grp, logits, neg)
    emax = jnp.max(el, axis=1, keepdims=True)
    ee = jnp.exp(el - emax)
    pe = ee / jnp.sum(ee, axis=1, keepdims=True)
    v0 = jnp.max(pe, axis=1, keepdims=True)
    i0 = jnp.min(jnp.where(in_grp & (pe == v0), lane, big), axis=1, keepdims=True)
    rest_m = in_grp & (lane != i0)
    pe1 = jnp.where(rest_m, pe, neg)
    v1 = jnp.max(pe1, axis=1, keepdims=True)
    i1 = jnp.min(jnp.where(rest_m & (pe1 == v1), lane, big), axis=1, keepdims=True)
    tot = v0 + v1
    w0 = pg_sel * (v0 / tot)
    w1 = pg_sel * (v1 / tot)
    ids = jnp.where(lane_i == 0, i0 - N_GROUPS, jnp.where(lane_i == 1, i1 - N_GROUPS, 0.0))
    ids_ref[...] = ids.astype(jnp.int32)
    wts_ref[...] = jnp.where(lane_i == 0, w0, jnp.where(lane_i == 1, w1, 0.0))


def _merge(x, hn_all, ys_all, proj2, b_glu, g_ffn, wr_hi, wr_lo, b_r, wpa, wglu, wpb, wout,
           t_all, row0, tm, aliases=None):
    n = x.shape[0]
    blk0 = row0 // tm
    const = lambda shape: pl.BlockSpec(shape, lambda i: (0,) * len(shape))
    any_spec = pl.BlockSpec(memory_space=pl.ANY)
    in_specs = [pl.BlockSpec((tm, D_MODEL), lambda i: (i, 0)),
                pl.BlockSpec((tm, D_MODEL), lambda i: (i + blk0, 0)),
                pl.BlockSpec((tm, D_B), lambda i: (i + blk0, 0)),
                pl.BlockSpec((tm, D_MODEL), lambda i: (i + blk0, 0)),
                pl.BlockSpec((tm, D_MODEL), lambda i: (i + blk0, 1)),
                const((1, D_B)), const((1, D_MODEL)),
                const((D_MODEL, LANES)), const((D_MODEL, LANES)), const((1, LANES)),
                any_spec, any_spec, any_spec, any_spec]
    args = [x, hn_all, ys_all, proj2, proj2, b_glu, g_ffn, wr_hi, wr_lo, b_r, wpa, wglu, wpb, wout]
    io_alias = {}
    if aliases is not None:
        n_in = len(args)
        for j, a in enumerate(aliases):
            in_specs.append(any_spec)
            args.append(a)
            io_alias[n_in + j] = j
    out_shape = (jax.ShapeDtypeStruct((t_all, D_MODEL), F32),
                 jax.ShapeDtypeStruct((t_all, D_MODEL), F32),
                 jax.ShapeDtypeStruct((t_all, LANES), jnp.int32),
                 jax.ShapeDtypeStruct((t_all, LANES), F32))
    out_specs = (pl.BlockSpec((tm, D_MODEL), lambda i: (i + blk0, 0)),
                 pl.BlockSpec((tm, D_MODEL), lambda i: (i + blk0, 0)),
                 pl.BlockSpec((tm, LANES), lambda i: (i + blk0, 0)),
                 pl.BlockSpec((tm, LANES), lambda i: (i + blk0, 0)))
    return pl.pallas_call(
        _merge_kernel,
        out_shape=out_shape,
        grid=(n // tm,),
        in_specs=in_specs,
        out_specs=out_specs,
        scratch_shapes=[pltpu.VMEM(wpa.shape, BF16), pltpu.VMEM(wglu.shape, BF16),
                        pltpu.VMEM(wpb.shape, BF16), pltpu.VMEM(wout.shape, BF16),
                        pltpu.SemaphoreType.DMA((4,))],
        input_output_aliases=io_alias,
        compiler_params=_cparams(("arbitrary",)),
        name="merge_router",
    )(*args)


def _moe_kernel(nt_ref, first_ref, ord_ref, elist_ref, nord_ref, xs_hbm, wg_hbm, wu_hbm, wd_hbm, o_ref,
                stg_g, stg_u, stg_d, wg_sc, wu_sc, wd_sc, xs_buf, sem, xs_sem):
    i = pl.program_id(0)
    n_ord = nord_ref[0]
    n_t = nt_ref[0]

    def tile_copy(t):
        slot = t % MOE_RING
        return pltpu.make_async_copy(xs_hbm.at[pl.ds(pl.multiple_of(t * MOE_TILE, MOE_TILE), MOE_TILE)],
                                     xs_buf.at[slot], xs_sem.at[slot])

    @pl.when(i == 0)
    def _():
        for t in range(MOE_RING - 1):
            @pl.when(t < n_t)
            def _():
                tile_copy(t).start()

    def weight_copies(k, slot):
        e = elist_ref[k]
        return (pltpu.make_async_copy(wg_hbm.at[e], stg_g.at[slot], sem.at[slot, 0]),
                pltpu.make_async_copy(wu_hbm.at[e], stg_u.at[slot], sem.at[slot, 1]),
                pltpu.make_async_copy(wd_hbm.at[e], stg_d.at[slot], sem.at[slot, 2]))

    @pl.when(i == 0)
    def _():
        for cp in weight_copies(0, 0):
            cp.start()

        @pl.when(n_ord > 1)
        def _():
            for cp in weight_copies(1, 1):
                cp.start()

    valid = i < nt_ref[0]
    k = ord_ref[i]

    @pl.when(valid & (first_ref[i] == 1))
    def _():
        slot = k % 2
        for cp in weight_copies(k, slot):
            cp.wait()
        wg_sc[...] = stg_g[slot].astype(BF16)
        wu_sc[...] = stg_u[slot].astype(BF16)
        wd_sc[...] = stg_d[slot].astype(BF16)

        @pl.when(k + 2 < n_ord)
        def _():
            for cp in weight_copies(k + 2, slot):
                cp.start()

    @pl.when(valid)
    def _():
        ahead = i + (MOE_RING - 1)

        @pl.when(ahead < n_t)
        def _():
            tile_copy(ahead).start()

        tile_copy(i).wait()
        x = xs_buf[i % MOE_RING].astype(BF16)
        hg = _dot(x, wg_sc[...])
        hu = _dot(x, wu_sc[...])
        hh = (_silu(hg) * hu).astype(BF16)
        o_ref[...] = _dot(hh, wd_sc[...])


def _moe_experts(n_tiles, first, ordinal, elist, n_ord, xs, w_gate, w_up, w_down):
    p_rows = xs.shape[0]
    last = lambda i, nt, *_: jnp.minimum(i, nt[0] - 1)
    any_spec = pl.BlockSpec(memory_space=pl.ANY)
    grid_spec = pltpu.PrefetchScalarGridSpec(
        num_scalar_prefetch=5,
        grid=(p_rows // MOE_TILE,),
        in_specs=[any_spec, any_spec, any_spec, any_spec],
        out_specs=pl.BlockSpec((MOE_TILE, D_MODEL), lambda i, *s: (last(i, *s), 0)),
        scratch_shapes=[pltpu.VMEM((2, D_MODEL, D_EXPERT), F32), pltpu.VMEM((2, D_MODEL, D_EXPERT), F32),
                        pltpu.VMEM((2, D_EXPERT, D_MODEL), F32),
                        pltpu.VMEM((D_MODEL, D_EXPERT), BF16), pltpu.VMEM((D_MODEL, D_EXPERT), BF16),
                        pltpu.VMEM((D_EXPERT, D_MODEL), BF16),
                        pltpu.VMEM((MOE_RING, MOE_TILE, D_MODEL), F32),
                        pltpu.SemaphoreType.DMA((2, 3)), pltpu.SemaphoreType.DMA((MOE_RING,))],
    )
    return pl.pallas_call(
        _moe_kernel,
        out_shape=jax.ShapeDtypeStruct((p_rows, D_MODEL), F32),
        grid_spec=grid_spec,
        compiler_params=_cparams(("arbitrary",)),
        name="moe_experts",
    )(n_tiles, first, ordinal, elist, n_ord, xs, w_gate, w_up, w_down)


def _final_kernel(x1_ref, y0_ref, y1_ref, w_ref, g_ref, o_ref):
    w = w_ref[...]
    x2 = x1_ref[...] + w[:, 0:1] * y0_ref[...] + w[:, 1:2] * y1_ref[...]
    o_ref[...] = x2 * lax.rsqrt(jnp.mean(x2 * x2, axis=1, keepdims=True) + EPS) * g_ref[...]


def _final(x1_all, yg0, yg1, wts, g_final, row0, n, tm):
    blk0 = row0 // tm
    rows = pl.BlockSpec((tm, D_MODEL), lambda i: (i + blk0, 0))
    return pl.pallas_call(
        _final_kernel,
        out_shape=jax.ShapeDtypeStruct((n, D_MODEL), F32),
        grid=(n // tm,),
        in_specs=[rows, rows, rows,
                  pl.BlockSpec((tm, LANES), lambda i: (i + blk0, 0)),
                  pl.BlockSpec((1, D_MODEL), lambda i: (0, 0))],
        out_specs=pl.BlockSpec((tm, D_MODEL), lambda i: (i, 0)),
        compiler_params=_cparams(("arbitrary",)),
        name="combine_final_norm",
    )(x1_all, yg0, yg1, wts, g_final)


def _s5_discretise(a_re, a_im, log_step, b_re, b_im):
    dt = jnp.exp(log_step)[:, None]
    mag = jnp.exp(a_re * dt)
    lb_re = mag * jnp.cos(a_im * dt)
    lb_im = mag * jnp.sin(a_im * dt)
    den = a_re * a_re + a_im * a_im
    nr = lb_re - 1.0
    coef_re = (nr * a_re + lb_im * a_im) / den
    coef_im = (lb_im * a_re - nr * a_im) / den
    bb_re = coef_re[..., None] * b_re - coef_im[..., None] * b_im
    bb_im = coef_re[..., None] * b_im + coef_im[..., None] * b_re
    return lb_re, lb_im, bb_re, bb_im


def _s5_chunk_params(a_re, a_im, log_step, b_re, b_im, c_re, c_im):
    lb_re, lb_im, bb_re, bb_im = _s5_discretise(a_re, a_im, log_step, b_re, b_im)
    L = S5_CHUNK
    gpb = LANES // S5_GROUP
    ncb = G_B // gpb
    pr, pi = [jnp.ones_like(lb_re)], [jnp.zeros_like(lb_re)]
    for _ in range(L):
        pr, pi = pr + [pr[-1] * lb_re - pi[-1] * lb_im], pi + [pr[-1] * lb_im + pi[-1] * lb_re]
    pw_r, pw_i = jnp.stack(pr), jnp.stack(pi)
    bt_re = bb_re.transpose(0, 2, 1)
    bt_im = bb_im.transpose(0, 2, 1)
    lbb_r = pw_r[:L, :, None, :] * bt_re - pw_i[:L, :, None, :] * bt_im
    lbb_i = pw_r[:L, :, None, :] * bt_im + pw_i[:L, :, None, :] * bt_re
    hp = lax.Precision.HIGHEST
    kdt = (jnp.einsum('gop,dgcp->dgco', c_re, lbb_r, precision=hp)
           - jnp.einsum('gop,dgcp->dgco', c_im, lbb_i, precision=hp))
    a_r = c_re * pw_r[1:, :, None, :] - c_im * pw_i[1:, :, None, :]
    a_i = -(c_re * pw_i[1:, :, None, :] + c_im * pw_r[1:, :, None, :])

    def blocks(m):
        return m.reshape(m.shape[:-3] + (ncb, gpb * m.shape[-2], m.shape[-1]))

    vec = lambda v: v.reshape(ncb, 1, S5_CST)
    prompt = (blocks(lbb_r), blocks(lbb_i), blocks(kdt), blocks(a_r), blocks(a_i), vec(pw_r[L]), vec(pw_i[L]))
    sample = (blocks(lbb_r[0]), blocks(lbb_i[0]), blocks(c_re), blocks(c_im), vec(lb_re), vec(lb_im))
    return prompt, sample


def _dispatch(ids, t_all, p_rows):
    e = ids[:, :2].reshape(-1)
    onehot = (e[:, None] == jnp.arange(N_EXPERTS, dtype=jnp.int32)[None, :]).astype(jnp.int32)
    csum = jnp.cumsum(onehot, axis=0)
    rank = jnp.sum((csum - onehot) * onehot, axis=1)
    counts = csum[-1]
    tiles = (counts + MOE_TILE - 1) // MOE_TILE
    tile_end = jnp.cumsum(tiles)
    tile_start = tile_end - tiles
    pos = jnp.sum(onehot * (tile_start * MOE_TILE)[None, :], axis=1) + rank
    tok = jnp.arange(2 * t_all, dtype=jnp.int32) // 2
    src = (jnp.arange(p_rows, dtype=jnp.int32) % t_all).at[pos].set(tok)
    n_tiles = tile_end[-1]
    tidx = jnp.arange(p_rows // MOE_TILE, dtype=jnp.int32)
    tclamp = jnp.minimum(tidx, n_tiles - 1)
    tile_expert = jnp.sum((tile_end[None, :] <= tclamp[:, None]).astype(jnp.int32), axis=1)
    present = (tiles > 0).astype(jnp.int32)
    ord_of_e = jnp.cumsum(present) - 1
    eids = jnp.arange(N_EXPERTS, dtype=jnp.int32)
    elist = jnp.sum(jnp.where((ord_of_e[None, :] == eids[:, None]) & (present[None, :] == 1), eids[None, :], 0), axis=1)
    ordinal = jnp.sum(jnp.where(tile_expert[:, None] == eids[None, :], ord_of_e[None, :], 0), axis=1)
    first = jnp.concatenate([jnp.ones((1,), jnp.int32),
                             (tile_expert[1:] != tile_expert[:-1]).astype(jnp.int32)])
    one = lambda v: v.reshape(1).astype(jnp.int32)
    meta = (one(n_tiles), first, ordinal.astype(jnp.int32), elist.astype(jnp.int32), one(jnp.sum(present)))
    return pos.reshape(t_all, 2), src, meta


def _pick_tile(n, candidates):
    for c in candidates:
        if n % c == 0:
            return c
    raise ValueError(f"no row tile for {n}")


def kernel(x_prompt, x_sample, state_mlstm_C, state_mlstm_n, state_mlstm_m, state_conv, state_s5_re,
           state_s5_im, norm_mix_g, w_in, b_i, b_f, w_conv, b_conv, head_norm_g, w_pa, s5_a_re, s5_a_im,
           s5_log_step, s5_b_re, s5_b_im, s5_c_re, s5_c_im, s5_d, s5_w_glu, s5_b_glu, w_pb, w_out,
           norm_ffn_g, w_rg, b_rg, w_rexp, b_rexp, w_gate, w_up, w_down, norm_final_g):
    assert state_mlstm_C.shape[0] == 1 and x_sample.shape[1] == 1
    batch, seq, _ = x_prompt.shape
    dec = x_sample.shape[0]
    t_p = batch * seq
    t_all = t_p + dec
    assert seq % CHUNK == 0 and t_p % dec == 0 and dec % LANES == 0
    d_a = H_A * DK
    tm_p = _pick_tile(t_p, (512, 256, 128))
    tm_all = _pick_tile(t_all, (1664, 640, 384, 128))
    sample_blk = t_p // dec

    xp = x_prompt.reshape(t_p, D_MODEL)
    xs = x_sample.reshape(dec, D_MODEL)

    g_mix = norm_mix_g[0]
    w_in_t = w_in.reshape(w_in.shape[1:]).T
    n_qkvo = 4 * d_a
    tn = 1024
    n_gate_cols = 2 * H_A
    b_gates = jnp.pad(jnp.concatenate([b_i[0], b_f[0]]), (0, LANES - n_gate_cols))
    xn_all, gates, gates_t = _rmsnorm_rows(xp, g_mix, w_in_t, n_qkvo, b_gates, t_all, 0, tm_p)
    xn_all, gates, _ = _rmsnorm_rows(xs, g_mix, w_in_t, n_qkvo, b_gates, t_all, t_p, dec,
                                     alias=(xn_all, gates))
    qkvo = _matmul_t(xn_all, w_in_t, [j * tn for j in range(n_qkvo // tn)], jnp.zeros((1, n_qkvo), F32),
                     tm_all, tn, F32)
    c_u = n_qkvo + n_gate_cols
    c_ga = c_u + D_B
    starts = [c_ga + j * tn for j in range(2 * D_MODEL // tn)] + [c_u]
    proj2 = _matmul_t(xn_all, w_in_t, starts, jnp.zeros((1, len(starts) * tn), F32), tm_all, tn, F32)
    u_col0 = 2 * D_MODEL

    conv_s_in = state_conv[0].reshape(dec, (CONV_W - 1) * 2 * d_a)
    q_s, kw_s, a_e, s_e, den_e, conv_s, n_s, m_s = _sample_pre(
        qkvo, conv_s_in, w_conv[0], b_conv[0].reshape(1, -1), gates, state_mlstm_m[0],
        state_mlstm_n[0].reshape(dec, d_a), dec, sample_blk)
    v_s = qkvo[t_p:, 2 * d_a:3 * d_a]
    r3 = lambda a: a.reshape(dec, H_A, DK)

    head_g = head_norm_g[0].reshape(1, d_a)
    hn_all, c_p, n_p, m_p, c_s, qc = _mlstm_prompt(
        qkvo, gates, gates_t, w_conv[0], b_conv[0].reshape(1, -1), head_g,
        (r3(q_s), r3(kw_s), r3(v_s), r3(a_e), state_mlstm_C[0]), batch, seq, t_all)
    conv_p = jnp.stack([qkvo[b * seq + seq - (CONV_W - 1):(b + 1) * seq, :2 * d_a] for b in range(batch)])
    hn_all = _sample_post(qc.reshape(dec, d_a), s_e, a_e, den_e, qkvo, head_g, hn_all, dec, sample_blk)

    d_row = s5_d[0].reshape(1, D_B)
    s5_prompt_w, s5_sample_w = _s5_chunk_params(s5_a_re[0], s5_a_im[0], s5_log_step[0], s5_b_re[0],
                                                s5_b_im[0], s5_c_re[0], s5_c_im[0])
    ys_all, s5re_p, s5im_p = _s5_prompt(proj2, u_col0, *s5_prompt_w, d_row, batch, seq, t_all)

    ys_all, s5re_s, s5im_s = _s5_sample(proj2, u_col0, state_s5_re[0].reshape(dec, -1),
                                        state_s5_im[0].reshape(dec, -1),
                                        *s5_sample_w, d_row, ys_all, dec, sample_blk)

    wr = jnp.pad(jnp.concatenate([w_rg[0], w_rexp[0]], axis=1), ((0, 0), (0, LANES - N_GROUPS - N_EXPERTS)))
    wr_hi = wr.astype(BF16)
    wr_lo = (wr - wr_hi.astype(F32)).astype(BF16)
    b_r = jnp.pad(jnp.concatenate([b_rg[0], b_rexp[0]]), (0, LANES - N_GROUPS - N_EXPERTS)).reshape(1, LANES)
    merge_w = (w_pa[0].astype(BF16), s5_w_glu[0].astype(BF16), w_pb[0].astype(BF16), w_out[0].astype(BF16))
    b_glu = s5_b_glu[0].reshape(1, D_B)
    g_ffn = norm_ffn_g[0].reshape(1, D_MODEL)
    tm_m = _pick_tile(t_p, (256, 128))
    outs = _merge(xp, hn_all, ys_all, proj2, b_glu, g_ffn, wr_hi, wr_lo, b_r, *merge_w, t_all, 0, tm_m)
    x1_all, xn2_all, ids, wts = _merge(xs, hn_all, ys_all, proj2, b_glu, g_ffn, wr_hi, wr_lo, b_r,
                                       *merge_w, t_all, t_p, dec, aliases=outs)

    p_rows = -(-(2 * t_all + N_EXPERTS * (MOE_TILE - 1)) // MOE_TILE) * MOE_TILE
    pos, src, meta = _dispatch(ids, t_all, p_rows)
    take_rows = lambda a, idx: a.at[idx].get(mode='promise_in_bounds')
    xs_sorted = take_rows(xn2_all, src)
    yp = _moe_experts(*meta, xs_sorted, w_gate[0], w_up[0], w_down[0])
    yg0 = take_rows(yp, pos[:, 0])
    yg1 = take_rows(yp, pos[:, 1])

    g_fin = norm_final_g.reshape(1, D_MODEL)
    y_prompt = _final(x1_all, yg0, yg1, wts, g_fin, 0, t_p, tm_p).reshape(batch, seq, D_MODEL)
    y_sample = _final(x1_all, yg0, yg1, wts, g_fin, t_p, dec, dec).reshape(dec, 1, D_MODEL)

    lead = lambda a, shape: a.reshape((1,) + shape)
    return (y_prompt, y_sample,
            lead(c_p, (batch, H_A, DK, DK)), lead(n_p, (batch, H_A, DK)), lead(m_p[:, 0, :H_A], (batch, H_A)),
            lead(conv_p, (batch, CONV_W - 1, 2 * d_a)),
            lead(s5re_p, (batch, G_B, P_S5)), lead(s5im_p, (batch, G_B, P_S5)),
            lead(c_s, (dec, H_A, DK, DK)), lead(n_s, (dec, H_A, DK)), lead(m_s[:, :H_A], (dec, H_A)),
            lead(conv_s, (dec, CONV_W - 1, 2 * d_a)),
            lead(s5re_s, (dec, G_B, P_S5)), lead(s5im_s, (dec, G_B, P_S5)))
```

```python
import functools
import math

import jax
import jax.numpy as jnp
from jax import lax
from jax.experimental import pallas as pl
from jax.experimental.pallas import tpu as pltpu

F32 = jnp.float32
BF16 = jnp.bfloat16

D_MODEL = 2048
H_A = 8
DK = 256
CONV_W = 4
CHUNK = 128
D_B = 1024
S5_GROUP = 16
G_B = 64
P_S5 = 64
N_GROUPS = 4
EXP_PER_GROUP = 8
N_EXPERTS = 32
D_EXPERT = 512
EPS = 1e-6

LANES = 128
SUBLANES = 8
VMEM_LIMIT = 56 * 1024 * 1024

S5_CHUNK = 8
S5_CST = (LANES // S5_GROUP) * P_S5
MOE_TILE = 256
MOE_RING = 3
WEIGHT_DMA_PRIORITY = 1


def _cparams(sem):
    return pltpu.CompilerParams(dimension_semantics=sem, vmem_limit_bytes=VMEM_LIMIT)


def _silu(x):
    return x * (1.0 / (1.0 + jnp.exp(-x)))


def _sigmoid(x):
    return 1.0 / (1.0 + jnp.exp(-x))


def _log_sigmoid(x):
    return jnp.minimum(x, 0.0) - jnp.log1p(jnp.exp(-jnp.abs(x)))


def _gelu_tanh(x):
    c = math.sqrt(2.0 / math.pi)
    return 0.5 * x * (1.0 + jnp.tanh(c * (x + 0.044715 * (x * x * x))))


def _split3(x):
    hi = x.astype(BF16)
    r = x - hi.astype(F32)
    mid = r.astype(BF16)
    lo = (r - mid.astype(F32)).astype(BF16)
    return hi, mid, lo


def _dot(a, b):
    return jnp.dot(a, b, preferred_element_type=F32)


def _dot_nt(a, b):
    return lax.dot_general(a, b, (((1,), (1,)), ((), ())), preferred_element_type=F32)


def _dot_tn(a, b):
    return lax.dot_general(a, b, (((0,), (0,)), ((), ())), preferred_element_type=F32)


def _rmsnorm_kernel(x_ref, g_ref, wg_ref, bg_ref, bgt_ref, *rest):
    n_out = 3
    o_ref, gates_ref, gates_t_ref = rest[-n_out:]
    x = x_ref[...]
    r = lax.rsqrt(jnp.mean(x * x, axis=-1, keepdims=True) + EPS)
    xn = (x * r * g_ref[...]).astype(o_ref.dtype)
    o_ref[...] = xn
    wg = wg_ref[...].astype(BF16)
    gates_ref[...] = _dot_nt(xn, wg) + bg_ref[...]
    gt = _dot_nt(wg, xn) + bgt_ref[...]
    gates_t_ref[...] = gt[0:gates_t_ref.shape[0], :]


def _rmsnorm_rows(x, g, w_t, gate_row0, b_gates, t_all, row0, tm, alias=None):
    n = x.shape[0]
    blk0 = row0 // tm
    in_specs = [pl.BlockSpec((tm, D_MODEL), lambda i: (i, 0)),
                pl.BlockSpec((1, D_MODEL), lambda i: (0, 0)),
                pl.BlockSpec((pl.Element(LANES), pl.Element(D_MODEL)), lambda i: (gate_row0, 0)),
                pl.BlockSpec((1, LANES), lambda i: (0, 0)),
                pl.BlockSpec((LANES, 1), lambda i: (0, 0))]
    args = [x, g.reshape(1, D_MODEL), w_t, b_gates.reshape(1, LANES), b_gates.reshape(LANES, 1)]
    aliases = {}
    if alias is not None:
        for a in alias:
            aliases[len(args)] = len(aliases)
            in_specs.append(pl.BlockSpec(memory_space=pl.ANY))
            args.append(a)
    return pl.pallas_call(
        _rmsnorm_kernel,
        out_shape=(jax.ShapeDtypeStruct((t_all, D_MODEL), BF16),
                   jax.ShapeDtypeStruct((t_all, LANES), F32),
                   jax.ShapeDtypeStruct((2 * H_A, n), F32)),
        grid=(n // tm,),
        in_specs=in_specs,
        out_specs=(pl.BlockSpec((tm, D_MODEL), lambda i: (i + blk0, 0)),
                   pl.BlockSpec((tm, LANES), lambda i: (i + blk0, 0)),
                   pl.BlockSpec((2 * H_A, tm), lambda i: (0, i))),
        input_output_aliases=aliases,
        compiler_params=_cparams(("arbitrary",)),
        name="rmsnorm_rows",
    )(*args)


def _mm_t_kernel(starts_ref, a_ref, wt_ref, b_ref, o_ref, wb_ref):
    del starts_ref

    @pl.when(pl.program_id(1) == 0)
    def _():
        wb_ref[...] = wt_ref[...].astype(BF16)

    o_ref[...] = (_dot_nt(a_ref[...], wb_ref[...]) + b_ref[...]).astype(o_ref.dtype)


def _matmul_t(a, w_t, row_starts, bias, tm, tn, out_dtype):
    m, k = a.shape
    n_t = len(row_starts)
    assert all(s % SUBLANES == 0 for s in row_starts)
    grid_spec = pltpu.PrefetchScalarGridSpec(
        num_scalar_prefetch=1,
        grid=(n_t, m // tm),
        in_specs=[pl.BlockSpec((tm, k), lambda j, i, st: (i, 0)),
                  pl.BlockSpec((pl.Element(tn), pl.Element(k)), lambda j, i, st: (st[j] * SUBLANES, 0)),
                  pl.BlockSpec((1, tn), lambda j, i, st: (0, j))],
        out_specs=pl.BlockSpec((tm, tn), lambda j, i, st: (i, j)),
        scratch_shapes=[pltpu.VMEM((tn, k), BF16)],
    )
    return pl.pallas_call(
        _mm_t_kernel,
        out_shape=jax.ShapeDtypeStruct((m, n_t * tn), out_dtype),
        grid_spec=grid_spec,
        compiler_params=_cparams(("arbitrary", "arbitrary")),
        name="rows_matmul_t",
    )(jnp.asarray([s // SUBLANES for s in row_starts], jnp.int32), a, w_t, bias)


def _sample_c_update(q_ref, kw_ref, v_ref, a_ref, c_ref, c_out, qc_out):
    rows = 2 * SUBLANES
    rid = lax.broadcasted_iota(jnp.int32, (rows, DK), 0)
    pad = jnp.zeros((rows - H_A, DK), F32)
    for b in range(q_ref.shape[0]):
        q16 = jnp.concatenate([q_ref[b], pad], axis=0).astype(BF16)
        kw16 = jnp.concatenate([kw_ref[b], pad], axis=0)
        v16 = jnp.concatenate([v_ref[b], pad], axis=0).astype(BF16)
        qc_rows = []
        for h in range(H_A):
            c_prev = c_ref[b, h]
            qc_rows.append(_dot(q16, c_prev.astype(BF16))[h:h + 1, :])
            kw_h = jnp.where(rid == h, kw16, 0.0).astype(BF16)
            c_out[b, h] = a_ref[b, h:h + 1, :] * c_prev + _dot_tn(kw_h, v16)
        qc_out[b] = jnp.concatenate(qc_rows, axis=0)


def _mlstm_kernel(qk_ref, v_ref, o_ref, gcol_ref, grow_ref, wc_ref, bc_ref, hg_ref,
                  sq_ref, skw_ref, sv_ref, sa_ref, sc_ref,
                  h_ref, c_out, n_out, m_out, sc_out, sqc_out, c_sc, n_sc, m_sc, ext_sc):
    c = pl.program_id(1)
    _sample_c_update(sq_ref, skw_ref, sv_ref, sa_ref, sc_ref, sc_out, sqc_out)
    L = CHUNK
    pad = SUBLANES
    d_a = H_A * DK

    @pl.when(c == 0)
    def _():
        c_sc[...] = jnp.zeros_like(c_sc)
        n_sc[...] = jnp.zeros_like(n_sc)
        m_sc[...] = jnp.zeros_like(m_sc)
        ext_sc[0:pad, :] = jnp.zeros((pad, 2 * d_a), F32)

    x = qk_ref[...]
    ext_sc[pad:pad + L, :] = x
    y = bc_ref[...] + wc_ref[CONV_W - 1:CONV_W, :] * x
    for j in range(1, CONV_W):
        y = y + wc_ref[CONV_W - 1 - j:CONV_W - j, :] * ext_sc[pad - j:pad - j + L, :]
    ext_sc[0:pad, :] = x[L - pad:L, :]
    y = _silu(y)

    gcol = gcol_ref[...]
    grow = grow_ref[...]
    ri = lax.broadcasted_iota(jnp.int32, (L, L), 0)
    ci = lax.broadcasted_iota(jnp.int32, (L, L), 1)
    causal = ci <= ri
    tril = jnp.where(causal, 1.0, 0.0).astype(BF16)
    triu = jnp.where(ri <= ci, 1.0, 0.0).astype(BF16)
    b_cols = sum(_dot(tril, p) for p in _split3(_log_sigmoid(gcol)))
    b_rows = sum(_dot(p, triu) for p in _split3(_log_sigmoid(grow)))

    for h in range(H_A):
        sl = slice(h * DK, (h + 1) * DK)
        q = y[:, sl] * (DK ** -0.5)
        k = y[:, d_a + h * DK:d_a + (h + 1) * DK]
        ig_col = gcol[:, h:h + 1]
        ig_row = grow[h:h + 1, :]
        b_col = b_cols[:, H_A + h:H_A + h + 1]
        b_row = b_rows[H_A + h:H_A + h + 1, :]

        m_prev = m_sc[:, h:h + 1]
        d_log = jnp.where(causal, b_col - b_row + ig_row, -jnp.inf)
        inter_log = b_col + m_prev
        m_t = jnp.maximum(inter_log, jnp.max(d_log, axis=1, keepdims=True))
        qb = q.astype(BF16)
        kb = k.astype(BF16)
        vb = v_ref[:, sl].astype(BF16)
        s = _dot_nt(qb, kb) * jnp.exp(d_log - m_t)
        inter_w = jnp.exp(inter_log - m_t)
        c_prev = c_sc[h]
        n_prev = n_sc[h:h + 1, :]
        num = _dot(s.astype(BF16), vb) + inter_w * _dot(qb, c_prev.astype(BF16))
        nq = jnp.sum(s, axis=1, keepdims=True) + inter_w * jnp.sum(q * n_prev, axis=1, keepdims=True)
        den = jnp.maximum(jnp.abs(nq), jnp.exp(-m_t))
        hh = num / den
        hh = hh * _sigmoid(o_ref[:, sl])
        hh = hh * lax.rsqrt(jnp.mean(hh * hh, axis=1, keepdims=True) + EPS)
        h_ref[:, sl] = (hh * hg_ref[:, sl]).astype(h_ref.dtype)

        m_new = m_t[L - 1:L, :]
        b_last = b_col[L - 1:L, :]
        decay = jnp.exp(b_last + m_prev - m_new)
        w_end = jnp.exp(b_last - b_col + ig_col - m_new)
        kw = k * w_end
        c_sc[h] = decay * c_prev + _dot_tn(kw.astype(BF16), vb)
        n_sc[h:h + 1, :] = decay * n_prev + jnp.sum(kw, axis=0, keepdims=True)
        m_sc[:, h:h + 1] = m_new

    @pl.when(c == pl.num_programs(1) - 1)
    def _():
        c_out[...] = c_sc[...]
        n_out[...] = n_sc[...]
        m_out[...] = m_sc[...]


def _mlstm_prompt(qkvo, gates, gates_t, w_conv, b_conv, head_g, sample, batch, seq, t_all):
    nc = seq // CHUNK
    L = CHUNK
    d_a = H_A * DK
    dec = sample[0].shape[0]
    nb = dec // (batch * nc)
    assert nb * batch * nc == dec
    row = lambda b, c: b * nc + c
    svec = pl.BlockSpec((nb, H_A, DK), lambda b, c: (row(b, c), 0, 0))
    smat = pl.BlockSpec((nb, H_A, DK, DK), lambda b, c: (row(b, c), 0, 0, 0))
    in_specs = [
        pl.BlockSpec((L, 2 * d_a), lambda b, c: (row(b, c), 0)),
        pl.BlockSpec((L, d_a), lambda b, c: (row(b, c), 2)),
        pl.BlockSpec((L, d_a), lambda b, c: (row(b, c), 3)),
        pl.BlockSpec((L, LANES), lambda b, c: (row(b, c), 0)),
        pl.BlockSpec((2 * H_A, L), lambda b, c: (0, row(b, c))),
        pl.BlockSpec((CONV_W, 2 * d_a), lambda b, c: (0, 0)),
        pl.BlockSpec((1, 2 * d_a), lambda b, c: (0, 0)),
        pl.BlockSpec((1, d_a), lambda b, c: (0, 0)),
        svec, svec, svec, svec, smat,
    ]
    out_shape = (
        jax.ShapeDtypeStruct((t_all, d_a), BF16),
        jax.ShapeDtypeStruct((batch, H_A, DK, DK), F32),
        jax.ShapeDtypeStruct((batch, H_A, DK), F32),
        jax.ShapeDtypeStruct((batch, 1, LANES), F32),
        jax.ShapeDtypeStruct((dec, H_A, DK, DK), F32),
        jax.ShapeDtypeStruct((dec, H_A, DK), F32),
    )
    out_specs = (
        pl.BlockSpec((L, d_a), lambda b, c: (row(b, c), 0)),
        pl.BlockSpec((None, H_A, DK, DK), lambda b, c: (b, 0, 0, 0)),
        pl.BlockSpec((None, H_A, DK), lambda b, c: (b, 0, 0)),
        pl.BlockSpec((None, 1, LANES), lambda b, c: (b, 0, 0)),
        smat, svec,
    )
    return pl.pallas_call(
        _mlstm_kernel,
        out_shape=out_shape,
        grid=(batch, nc),
        in_specs=in_specs,
        out_specs=out_specs,
        scratch_shapes=[pltpu.VMEM((H_A, DK, DK), F32), pltpu.VMEM((H_A, DK), F32),
                        pltpu.VMEM((1, LANES), F32), pltpu.VMEM((SUBLANES + L, 2 * d_a), F32)],
        compiler_params=_cparams(("arbitrary", "arbitrary")),
        name="mlstm_prompt",
    )(qkvo, qkvo, qkvo, gates, gates_t, w_conv, b_conv, head_g, *sample)


def _lane_tile(x, reps):
    w = x.shape[1]
    ri = lax.broadcasted_iota(jnp.int32, (w, w * reps), 0)
    ci = lax.broadcasted_iota(jnp.int32, (w, w * reps), 1)
    rep = jnp.where(jnp.bitwise_and(ci, w - 1) == ri, 1.0, 0.0).astype(BF16)
    return _dot(x.astype(BF16), rep)


def _group_mask(rows, cols, row_per, col_per):
    ri = lax.broadcasted_iota(jnp.int32, (rows, cols), 0)
    ci = lax.broadcasted_iota(jnp.int32, (rows, cols), 1)
    return (jnp.right_shift(ri, int(math.log2(row_per))) == jnp.right_shift(ci, int(math.log2(col_per))))


def _s5_build_weights(bin_r, bin_i, kdt, aout_r, aout_i, winc_ref, wintra_ref, wout_ref):
    L = S5_CHUNK
    gpb = LANES // S5_GROUP
    m_inc = _group_mask(LANES, S5_CST, S5_GROUP, P_S5)
    m_lag = _group_mask(LANES, LANES, S5_GROUP, S5_GROUP)
    zero = jnp.zeros((LANES, LANES), BF16)
    lag = [jnp.where(m_lag, _lane_tile(kdt[d], gpb), 0.0).astype(BF16) for d in range(L)]
    for t in range(L):
        rows = slice(t * LANES, (t + 1) * LANES)
        d = L - 1 - t
        winc_ref[rows, 0:S5_CST] = jnp.where(m_inc, _lane_tile(bin_r[d], gpb), 0.0).astype(BF16)
        winc_ref[rows, S5_CST:2 * S5_CST] = jnp.where(m_inc, _lane_tile(bin_i[d], gpb), 0.0).astype(BF16)
        wout_ref[rows, 0:S5_CST] = jnp.where(m_inc, _lane_tile(aout_r[t], gpb), 0.0).astype(BF16)
        wout_ref[rows, S5_CST:2 * S5_CST] = jnp.where(m_inc, _lane_tile(aout_i[t], gpb), 0.0).astype(BF16)
        for t2 in range(L):
            wintra_ref[rows, t2 * LANES:(t2 + 1) * LANES] = lag[t2 - t] if t2 >= t else zero


def _s5_prompt_kernel(u_ref, binr_ref, bini_ref, kdt_ref, aoutr_ref, aouti_ref, lbr_ref, lbi_ref, d_ref,
                      ys_ref, sre_ref, sim_ref, x_sc, winc_ref, wintra_ref, wout_ref, *, batch, nchunk):
    L = S5_CHUNK
    nrow = batch * nchunk
    nst = S5_CST // LANES
    rstr = nchunk + SUBLANES
    _s5_build_weights(binr_ref, bini_ref, kdt_ref, aoutr_ref, aouti_ref, winc_ref, wintra_ref, wout_ref)
    u_t = [u_ref[pl.ds(t, nrow, stride=L), :] for t in range(L)]
    lhs = jnp.concatenate([a.astype(BF16) for a in u_t], axis=1)
    inc = _dot(lhs, winc_ref[...])
    for j in range(2 * nst):
        for b in range(batch):
            x_sc[j, b * rstr:b * rstr + nchunk, :] = inc[b * nchunk:(b + 1) * nchunk, j * LANES:(j + 1) * LANES]

    lbr = [jnp.broadcast_to(lbr_ref[:, j * LANES:(j + 1) * LANES], (batch, LANES)) for j in range(nst)]
    lbi = [jnp.broadcast_to(lbi_ref[:, j * LANES:(j + 1) * LANES], (batch, LANES)) for j in range(nst)]

    def scan_body(r, carry):
        rows = pl.ds(r, batch, stride=rstr)
        out = []
        for j in range(nst):
            xr, xi = carry[j]
            ir = x_sc[j, rows, :]
            ii = x_sc[nst + j, rows, :]
            x_sc[j, rows, :] = xr
            x_sc[nst + j, rows, :] = xi
            out.append((lbr[j] * xr - lbi[j] * xi + ir, lbr[j] * xi + lbi[j] * xr + ii))
        return tuple(out)

    z = jnp.zeros((batch, LANES), F32)
    fin = lax.fori_loop(0, nchunk, scan_body, tuple((z, z) for _ in range(nst)), unroll=4)
    for j in range(nst):
        sre_ref[:, j * LANES:(j + 1) * LANES] = fin[j][0]
        sim_ref[:, j * LANES:(j + 1) * LANES] = fin[j][1]

    xprev = jnp.concatenate(
        [jnp.concatenate([x_sc[j, b * rstr:b * rstr + nchunk, :] for b in range(batch)], axis=0)
         for j in range(2 * nst)], axis=1).astype(BF16)
    y = _dot(lhs, wintra_ref[...]) + _dot_nt(xprev, wout_ref[...])
    for t in range(L):
        yt = y[:, t * LANES:(t + 1) * LANES] + d_ref[...] * u_t[t]
        ys_ref[pl.ds(t, nrow, stride=L), :] = _gelu_tanh(yt).astype(ys_ref.dtype)


def _s5_prompt(proj2, u_col0, bin_r, bin_i, kdt, aout_r, aout_i, lb8r, lb8i, d_row, batch, seq, t_all):
    nchunk = seq // S5_CHUNK
    t_p = batch * seq
    kern = functools.partial(_s5_prompt_kernel, batch=batch, nchunk=nchunk)
    ub0 = u_col0 // LANES
    ncb = D_B // LANES
    kw = S5_CHUNK * LANES
    per_blk = lambda a: pl.BlockSpec((a.shape[0], None) + a.shape[2:], lambda g: (0, g, 0, 0))
    return pl.pallas_call(
        kern,
        out_shape=(jax.ShapeDtypeStruct((t_all, D_B), F32),
                   jax.ShapeDtypeStruct((batch, G_B * P_S5), F32),
                   jax.ShapeDtypeStruct((batch, G_B * P_S5), F32)),
        grid=(ncb,),
        in_specs=[pl.BlockSpec((t_p, LANES), lambda g: (0, ub0 + g)),
                  per_blk(bin_r), per_blk(bin_i), per_blk(kdt), per_blk(aout_r), per_blk(aout_i),
                  pl.BlockSpec((None, 1, S5_CST), lambda g: (g, 0, 0)),
                  pl.BlockSpec((None, 1, S5_CST), lambda g: (g, 0, 0)),
                  pl.BlockSpec((1, LANES), lambda g: (0, g))],
        out_specs=(pl.BlockSpec((t_p, LANES), lambda g: (0, g)),
                   pl.BlockSpec((batch, S5_CST), lambda g: (0, g)),
                   pl.BlockSpec((batch, S5_CST), lambda g: (0, g))),
        scratch_shapes=[pltpu.VMEM((2 * S5_CST // LANES, batch * (nchunk + SUBLANES), LANES), F32),
                        pltpu.VMEM((kw, 2 * S5_CST), BF16), pltpu.VMEM((kw, kw), BF16),
                        pltpu.VMEM((kw, 2 * S5_CST), BF16)],
        compiler_params=_cparams(("arbitrary",)),
        name="s5_prompt",
    )(proj2, bin_r, bin_i, kdt, aout_r, aout_i, lb8r, lb8i, d_row)


def _sample_pre_kernel(qk_ref, conv_ref, wc_ref, bc_ref, g_ref, m_ref, n_ref,
                       q_out, kw_out, a_out, s_out, den_out, conv_out, n_out, m_out):
    c2 = 2 * H_A * DK
    x_new = qk_ref[...]
    y = bc_ref[...] + wc_ref[CONV_W - 1:CONV_W, :] * x_new
    for j in range(CONV_W - 1):
        y = y + wc_ref[j:j + 1, :] * conv_ref[:, j * c2:(j + 1) * c2]
    y = _silu(y)
    conv_out[:, 0:(CONV_W - 2) * c2] = conv_ref[:, c2:(CONV_W - 1) * c2]
    conv_out[:, (CONV_W - 2) * c2:(CONV_W - 1) * c2] = x_new
    g = g_ref[...]
    bd = x_new.shape[0]
    m_cols = []
    for h in range(H_A):
        sl = slice(h * DK, (h + 1) * DK)
        q = y[:, sl] * (DK ** -0.5)
        k = y[:, H_A * DK + h * DK:H_A * DK + (h + 1) * DK]
        ig = g[:, h:h + 1]
        lf = _log_sigmoid(g[:, H_A + h:H_A + h + 1])
        m_prev = m_ref[:, h:h + 1]
        m_t = jnp.maximum(lf + m_prev, ig)
        a = jnp.exp(lf + m_prev - m_t)
        wgt = jnp.exp(ig - m_t)
        n_prev = n_ref[:, sl]
        s = jnp.sum(q * k, axis=1, keepdims=True) * wgt
        nq = s + a * jnp.sum(q * n_prev, axis=1, keepdims=True)
        den = jnp.maximum(jnp.abs(nq), jnp.exp(-m_t))
        kw = wgt * k
        q_out[:, sl] = q
        kw_out[:, sl] = kw
        a_out[:, sl] = jnp.broadcast_to(a, (bd, DK))
        s_out[:, sl] = jnp.broadcast_to(s, (bd, DK))
        den_out[:, sl] = jnp.broadcast_to(den, (bd, DK))
        n_out[:, sl] = a * n_prev + kw
        m_cols.append(m_t)
    lane = lax.broadcasted_iota(jnp.int32, (bd, LANES), 1)
    m_full = jnp.zeros((bd, LANES), F32)
    for h in range(H_A):
        m_full = jnp.where(lane == h, m_cols[h], m_full)
    m_out[...] = m_full


def _sample_pre(qkvo, conv_state, w_conv, b_conv, gates, m_state, n_state, dec, row_blk):
    c2 = 2 * H_A * DK
    d = H_A * DK
    full = lambda shape: pl.BlockSpec(shape, lambda i: (0,) * len(shape))
    rows = lambda: jax.ShapeDtypeStruct((dec, d), F32)
    return pl.pallas_call(
        _sample_pre_kernel,
        out_shape=(rows(), rows(), rows(), rows(), rows(),
                   jax.ShapeDtypeStruct((dec, (CONV_W - 1) * c2), F32), rows(),
                   jax.ShapeDtypeStruct((dec, LANES), F32)),
        grid=(1,),
        in_specs=[pl.BlockSpec((dec, c2), lambda i: (row_blk, 0)),
                  full((dec, (CONV_W - 1) * c2)), full((CONV_W, c2)), full((1, c2)),
                  pl.BlockSpec((dec, LANES), lambda i: (row_blk, 0)),
                  full((dec, H_A)), full((dec, d))],
        out_specs=(full((dec, d)), full((dec, d)), full((dec, d)), full((dec, d)), full((dec, d)),
                   full((dec, (CONV_W - 1) * c2)), full((dec, d)), full((dec, LANES))),
        compiler_params=_cparams(("arbitrary",)),
        name="sample_pre",
    )(qkvo, conv_state, w_conv, b_conv, gates, m_state, n_state)


def _sample_post_kernel(qc_ref, s_ref, a_ref, den_ref, vo_ref, hg_ref, hn_in, h_ref):
    del hn_in
    d = H_A * DK
    num = s_ref[...] * vo_ref[:, 0:d] + a_ref[...] * qc_ref[...]
    hh = num / den_ref[...]
    hh = hh * _sigmoid(vo_ref[:, d:2 * d])
    for h in range(H_A):
        sl = slice(h * DK, (h + 1) * DK)
        seg = hh[:, sl]
        seg = seg * lax.rsqrt(jnp.mean(seg * seg, axis=1, keepdims=True) + EPS)
        h_ref[:, sl] = (seg * hg_ref[:, sl]).astype(h_ref.dtype)


def _sample_post(qc, s_e, a_e, den_e, qkvo, head_g, hn_all, dec, row_blk):
    d = H_A * DK
    full = lambda shape: pl.BlockSpec(shape, lambda i: (0,) * len(shape))
    return pl.pallas_call(
        _sample_post_kernel,
        out_shape=jax.ShapeDtypeStruct(hn_all.shape, hn_all.dtype),
        grid=(1,),
        in_specs=[full((dec, d)), full((dec, d)), full((dec, d)), full((dec, d)),
                  pl.BlockSpec((dec, 2 * d), lambda i: (row_blk, 1)),
                  full((1, d)), pl.BlockSpec(memory_space=pl.ANY)],
        out_specs=pl.BlockSpec((dec, d), lambda i: (row_blk, 0)),
        input_output_aliases={6: 0},
        compiler_params=_cparams(("arbitrary",)),
        name="sample_post",
    )(qc, s_e, a_e, den_e, qkvo, head_g, hn_all)


def _s5_sample_kernel(u_ref, sr_ref, si_ref, br_ref, bi_ref, cr_ref, ci_ref, lbr_ref, lbi_ref, d_ref, ys_in,
                      ys_ref, sre_out, sim_out):
    del ys_in
    gpb = LANES // S5_GROUP
    mask = _group_mask(LANES, S5_CST, S5_GROUP, P_S5)
    expand = lambda blk: jnp.where(mask, _lane_tile(blk, gpb), 0.0)
    for g in range(D_B // LANES):
        ch = slice(g * LANES, (g + 1) * LANES)
        sl = slice(g * S5_CST, (g + 1) * S5_CST)
        u = u_ref[:, ch]
        bmat = jnp.concatenate([expand(br_ref[g]), expand(bi_ref[g])], axis=1).astype(BF16)
        bu = _dot(u.astype(BF16), bmat)
        lbr = lbr_ref[g]
        lbi = lbi_ref[g]
        sr = sr_ref[:, sl]
        si = si_ref[:, sl]
        xr = lbr * sr - lbi * si + bu[:, 0:S5_CST]
        xi = lbr * si + lbi * sr + bu[:, S5_CST:2 * S5_CST]
        sre_out[:, sl] = xr
        sim_out[:, sl] = xi
        x = jnp.concatenate([xr, xi], axis=1).astype(BF16)
        cmat = jnp.concatenate([expand(cr_ref[g]), -expand(ci_ref[g])], axis=1).astype(BF16)
        y = _dot_nt(x, cmat) + d_ref[:, ch] * u
        ys_ref[:, ch] = _gelu_tanh(y).astype(ys_ref.dtype)


def _s5_sample(proj2, u_col0, s_re, s_im, b_r, b_i, c_r, c_i, lbr, lbi, d_row, ys_all, dec, row_blk):
    full = lambda shape: pl.BlockSpec(shape, lambda i: (0,) * len(shape))
    n_state = G_B * P_S5
    ub0 = u_col0 // D_B
    params = (b_r, b_i, c_r, c_i, lbr, lbi)
    return pl.pallas_call(
        _s5_sample_kernel,
        out_shape=(jax.ShapeDtypeStruct(ys_all.shape, ys_all.dtype),
                   jax.ShapeDtypeStruct((dec, n_state), F32),
                   jax.ShapeDtypeStruct((dec, n_state), F32)),
        grid=(1,),
        in_specs=[pl.BlockSpec((dec, D_B), lambda i: (row_blk, ub0)),
                  full((dec, n_state)), full((dec, n_state))]
                 + [full(p.shape) for p in params]
                 + [full((1, D_B)), pl.BlockSpec(memory_space=pl.ANY)],
        out_specs=(pl.BlockSpec((dec, D_B), lambda i: (row_blk, 0)),
                   full((dec, n_state)), full((dec, n_state))),
        input_output_aliases={10: 0},
        compiler_params=_cparams(("arbitrary",)),
        name="s5_sample",
    )(proj2, s_re, s_im, *params, d_row, ys_all)


def _merge_kernel(x_ref, hn_ref, ys_ref, ga_ref, gb_ref, bglu_ref, gffn_ref, wrh_ref, wrl_ref, br_ref,
                  wpa_hbm, wglu_hbm, wpb_hbm, wout_hbm, *rest):
    n_alias = len(rest) - 9
    x1_ref, xn_ref, ids_ref, wts_ref = rest[n_alias:n_alias + 4]
    wpa, wglu, wpb, wout, sem = rest[n_alias + 4:]

    @pl.when(pl.program_id(0) == 0)
    def _():
        copies = [pltpu.make_async_copy(src, dst, sem.at[i])
                  for i, (src, dst) in enumerate(((wpa_hbm, wpa), (wglu_hbm, wglu),
                                                  (wpb_hbm, wpb), (wout_hbm, wout)))]
        for cp in copies:
            cp.start()
        for cp in copies:
            cp.wait()

    ya = _dot(hn_ref[...], wpa[...])
    ys = ys_ref[...]
    gate = _sigmoid(_dot(ys.astype(BF16), wglu[...]) + bglu_ref[...])
    yb = _dot((ys * gate).astype(BF16), wpb[...])
    z = _sigmoid(ga_ref[...]) * ya + _sigmoid(gb_ref[...]) * yb
    x1 = x_ref[...] + _dot(z.astype(BF16), wout[...])
    x1_ref[...] = x1
    xn = x1 * lax.rsqrt(jnp.mean(x1 * x1, axis=1, keepdims=True) + EPS) * gffn_ref[...]
    xn_ref[...] = xn
    xh = xn.astype(BF16)
    xl = (xn - xh.astype(F32)).astype(BF16)
    logits = _dot(xh, wrh_ref[...]) + _dot(xl, wrh_ref[...]) + _dot(xh, wrl_ref[...]) + br_ref[...]

    lane_i = lax.broadcasted_iota(jnp.int32, logits.shape, 1)
    lane = lane_i.astype(F32)
    neg = -jnp.inf
    big = float(1 << 20)
    gl = jnp.where(lane_i < N_GROUPS, logits, neg)
    gmax = jnp.max(gl, axis=1, keepdims=True)
    gsum = jnp.sum(jnp.exp(gl - gmax), axis=1, keepdims=True)
    gidx = jnp.min(jnp.where(gl == gmax, lane, big), axis=1, keepdims=True)
    pg_sel = 1.0 / gsum
    lo = N_GROUPS + gidx * EXP_PER_GROUP
    in_grp = (lane >= lo) & (lane < lo + EXP_PER_GROUP)
    el = jnp.where(in_grp, logits, neg)
    emax = jnp.max(el, axis=1, keepdims=True)
    ee = jnp.exp(el - emax)
    pe = ee / jnp.sum(ee, axis=1, keepdims=True)
    v0 = jnp.max(pe, axis=1, keepdims=True)
    i0 = jnp.min(jnp.where(in_grp & (pe == v0), lane, big), axis=1, keepdims=True)
    rest_m = in_grp & (lane != i0)
    pe1 = jnp.where(rest_m, pe, neg)
    v1 = jnp.max(pe1, axis=1, keepdims=True)
    i1 = jnp.min(jnp.where(rest_m & (pe1 == v1), lane, big), axis=1, keepdims=True)
    tot = v0 + v1
    w0 = pg_sel * (v0 / tot)
    w1 = pg_sel * (v1 / tot)
    ids = jnp.where(lane_i == 0, i0 - N_GROUPS, jnp.where(lane_i == 1, i1 - N_GROUPS, 0.0))
    ids_ref[...] = ids.astype(jnp.int32)
    wts_ref[...] = jnp.where(lane_i == 0, w0, jnp.where(lane_i == 1, w1, 0.0))


def _merge(x, hn_all, ys_all, proj2, b_glu, g_ffn, wr_hi, wr_lo, b_r, wpa, wglu, wpb, wout,
           t_all, row0, tm, aliases=None):
    n = x.shape[0]
    blk0 = row0 // tm
    const = lambda shape: pl.BlockSpec(shape, lambda i: (0,) * len(shape))
    any_spec = pl.BlockSpec(memory_space=pl.ANY)
    in_specs = [pl.BlockSpec((tm, D_MODEL), lambda i: (i, 0)),
                pl.BlockSpec((tm, D_MODEL), lambda i: (i + blk0, 0)),
                pl.BlockSpec((tm, D_B), lambda i: (i + blk0, 0)),
                pl.BlockSpec((tm, D_MODEL), lambda i: (i + blk0, 0)),
                pl.BlockSpec((tm, D_MODEL), lambda i: (i + blk0, 1)),
                const((1, D_B)), const((1, D_MODEL)),
                const((D_MODEL, LANES)), const((D_MODEL, LANES)), const((1, LANES)),
                any_spec, any_spec, any_spec, any_spec]
    args = [x, hn_all, ys_all, proj2, proj2, b_glu, g_ffn, wr_hi, wr_lo, b_r, wpa, wglu, wpb, wout]
    io_alias = {}
    if aliases is not None:
        n_in = len(args)
        for j, a in enumerate(aliases):
            in_specs.append(any_spec)
            args.append(a)
            io_alias[n_in + j] = j
    out_shape = (jax.ShapeDtypeStruct((t_all, D_MODEL), F32),
                 jax.ShapeDtypeStruct((t_all, D_MODEL), F32),
                 jax.ShapeDtypeStruct((t_all, LANES), jnp.int32),
                 jax.ShapeDtypeStruct((t_all, LANES), F32))
    out_specs = (pl.BlockSpec((tm, D_MODEL), lambda i: (i + blk0, 0)),
                 pl.BlockSpec((tm, D_MODEL), lambda i: (i + blk0, 0)),
                 pl.BlockSpec((tm, LANES), lambda i: (i + blk0, 0)),
                 pl.BlockSpec((tm, LANES), lambda i: (i + blk0, 0)))
    return pl.pallas_call(
        _merge_kernel,
        out_shape=out_shape,
        grid=(n // tm,),
        in_specs=in_specs,
        out_specs=out_specs,
        scratch_shapes=[pltpu.VMEM(wpa.shape, BF16), pltpu.VMEM(wglu.shape, BF16),
                        pltpu.VMEM(wpb.shape, BF16), pltpu.VMEM(wout.shape, BF16),
                        pltpu.SemaphoreType.DMA((4,))],
        input_output_aliases=io_alias,
        compiler_params=_cparams(("arbitrary",)),
        name="merge_router",
    )(*args)


def _moe_kernel(nt_ref, first_ref, ord_ref, elist_ref, nord_ref, xs_hbm, wg_hbm, wu_hbm, wd_hbm, o_ref,
                stg_g, stg_u, stg_d, wg_sc, wu_sc, wd_sc, xs_buf, sem, xs_sem):
    i = pl.program_id(0)
    n_ord = nord_ref[0]
    n_t = nt_ref[0]

    def tile_copy(t):
        slot = t % MOE_RING
        return pltpu.make_async_copy(xs_hbm.at[pl.ds(pl.multiple_of(t * MOE_TILE, MOE_TILE), MOE_TILE)],
                                     xs_buf.at[slot], xs_sem.at[slot])

    @pl.when(i == 0)
    def _():
        for t in range(MOE_RING - 1):
            @pl.when(t < n_t)
            def _():
                tile_copy(t).start()

    def weight_copies(k, slot):
        e = elist_ref[k]
        return (pltpu.make_async_copy(wg_hbm.at[e], stg_g.at[slot], sem.at[slot, 0]),
                pltpu.make_async_copy(wu_hbm.at[e], stg_u.at[slot], sem.at[slot, 1]),
                pltpu.make_async_copy(wd_hbm.at[e], stg_d.at[slot], sem.at[slot, 2]))

    @pl.when(i == 0)
    def _():
        for cp in weight_copies(0, 0):
            cp.start(priority=WEIGHT_DMA_PRIORITY)

        @pl.when(n_ord > 1)
        def _():
            for cp in weight_copies(1, 1):
                cp.start(priority=WEIGHT_DMA_PRIORITY)

    valid = i < nt_ref[0]
    k = ord_ref[i]

    @pl.when(valid & (first_ref[i] == 1))
    def _():
        slot = k % 2
        for cp in weight_copies(k, slot):
            cp.wait()
        wg_sc[...] = stg_g[slot].astype(BF16)
        wu_sc[...] = stg_u[slot].astype(BF16)
        wd_sc[...] = stg_d[slot].astype(BF16)

        @pl.when(k + 2 < n_ord)
        def _():
            for cp in weight_copies(k + 2, slot):
                cp.start(priority=WEIGHT_DMA_PRIORITY)

    @pl.when(valid)
    def _():
        ahead = i + (MOE_RING - 1)

        @pl.when(ahead < n_t)
        def _():
            tile_copy(ahead).start()

        tile_copy(i).wait()
        x = xs_buf[i % MOE_RING].astype(BF16)
        hg = _dot(x, wg_sc[...])
        hu = _dot(x, wu_sc[...])
        hh = (_silu(hg) * hu).astype(BF16)
        o_ref[...] = _dot(hh, wd_sc[...])


def _moe_experts(n_tiles, first, ordinal, elist, n_ord, xs, w_gate, w_up, w_down):
    p_rows = xs.shape[0]
    last = lambda i, nt, *_: jnp.minimum(i, nt[0] - 1)
    any_spec = pl.BlockSpec(memory_space=pl.ANY)
    grid_spec = pltpu.PrefetchScalarGridSpec(
        num_scalar_prefetch=5,
        grid=(p_rows // MOE_TILE,),
        in_specs=[any_spec, any_spec, any_spec, any_spec],
        out_specs=pl.BlockSpec((MOE_TILE, D_MODEL), lambda i, *s: (last(i, *s), 0)),
        scratch_shapes=[pltpu.VMEM((2, D_MODEL, D_EXPERT), F32), pltpu.VMEM((2, D_MODEL, D_EXPERT), F32),
                        pltpu.VMEM((2, D_EXPERT, D_MODEL), F32),
                        pltpu.VMEM((D_MODEL, D_EXPERT), BF16), pltpu.VMEM((D_MODEL, D_EXPERT), BF16),
                        pltpu.VMEM((D_EXPERT, D_MODEL), BF16),
                        pltpu.VMEM((MOE_RING, MOE_TILE, D_MODEL), F32),
                        pltpu.SemaphoreType.DMA((2, 3)), pltpu.SemaphoreType.DMA((MOE_RING,))],
    )
    return pl.pallas_call(
        _moe_kernel,
        out_shape=jax.ShapeDtypeStruct((p_rows, D_MODEL), F32),
        grid_spec=grid_spec,
        compiler_params=_cparams(("arbitrary",)),
        name="moe_experts",
    )(n_tiles, first, ordinal, elist, n_ord, xs, w_gate, w_up, w_down)


def _final_kernel(x1_ref, y0_ref, y1_ref, w_ref, g_ref, o_ref):
    w = w_ref[...]
    x2 = x1_ref[...] + w[:, 0:1] * y0_ref[...] + w[:, 1:2] * y1_ref[...]
    o_ref[...] = x2 * lax.rsqrt(jnp.mean(x2 * x2, axis=1, keepdims=True) + EPS) * g_ref[...]


def _final(x1_all, yg0, yg1, wts, g_final, row0, n, tm):
    blk0 = row0 // tm
    rows = pl.BlockSpec((tm, D_MODEL), lambda i: (i + blk0, 0))
    return pl.pallas_call(
        _final_kernel,
        out_shape=jax.ShapeDtypeStruct((n, D_MODEL), F32),
        grid=(n // tm,),
        in_specs=[rows, rows, rows,
                  pl.BlockSpec((tm, LANES), lambda i: (i + blk0, 0)),
                  pl.BlockSpec((1, D_MODEL), lambda i: (0, 0))],
        out_specs=pl.BlockSpec((tm, D_MODEL), lambda i: (i, 0)),
        compiler_params=_cparams(("arbitrary",)),
        name="combine_final_norm",
    )(x1_all, yg0, yg1, wts, g_final)


def _s5_discretise(a_re, a_im, log_step, b_re, b_im):
    dt = jnp.exp(log_step)[:, None]
    mag = jnp.exp(a_re * dt)
    lb_re = mag * jnp.cos(a_im * dt)
    lb_im = mag * jnp.sin(a_im * dt)
    den = a_re * a_re + a_im * a_im
    nr = lb_re - 1.0
    coef_re = (nr * a_re + lb_im * a_im) / den
    coef_im = (lb_im * a_re - nr * a_im) / den
    bb_re = coef_re[..., None] * b_re - coef_im[..., None] * b_im
    bb_im = coef_re[..., None] * b_im + coef_im[..., None] * b_re
    return lb_re, lb_im, bb_re, bb_im


def _s5_chunk_params(a_re, a_im, log_step, b_re, b_im, c_re, c_im):
    lb_re, lb_im, bb_re, bb_im = _s5_discretise(a_re, a_im, log_step, b_re, b_im)
    L = S5_CHUNK
    gpb = LANES // S5_GROUP
    ncb = G_B // gpb
    pr, pi = [jnp.ones_like(lb_re)], [jnp.zeros_like(lb_re)]
    for _ in range(L):
        pr, pi = pr + [pr[-1] * lb_re - pi[-1] * lb_im], pi + [pr[-1] * lb_im + pi[-1] * lb_re]
    pw_r, pw_i = jnp.stack(pr), jnp.stack(pi)
    bt_re = bb_re.transpose(0, 2, 1)
    bt_im = bb_im.transpose(0, 2, 1)
    lbb_r = pw_r[:L, :, None, :] * bt_re - pw_i[:L, :, None, :] * bt_im
    lbb_i = pw_r[:L, :, None, :] * bt_im + pw_i[:L, :, None, :] * bt_re
    hp = lax.Precision.HIGHEST
    kdt = (jnp.einsum('gop,dgcp->dgco', c_re, lbb_r, precision=hp)
           - jnp.einsum('gop,dgcp->dgco', c_im, lbb_i, precision=hp))
    a_r = c_re * pw_r[1:, :, None, :] - c_im * pw_i[1:, :, None, :]
    a_i = -(c_re * pw_i[1:, :, None, :] + c_im * pw_r[1:, :, None, :])

    def blocks(m):
        return m.reshape(m.shape[:-3] + (ncb, gpb * m.shape[-2], m.shape[-1]))

    vec = lambda v: v.reshape(ncb, 1, S5_CST)
    prompt = (blocks(lbb_r), blocks(lbb_i), blocks(kdt), blocks(a_r), blocks(a_i), vec(pw_r[L]), vec(pw_i[L]))
    sample = (blocks(lbb_r[0]), blocks(lbb_i[0]), blocks(c_re), blocks(c_im), vec(lb_re), vec(lb_im))
    return prompt, sample


def _dispatch(ids, t_all, p_rows):
    e = ids[:, :2].reshape(-1)
    onehot = (e[:, None] == jnp.arange(N_EXPERTS, dtype=jnp.int32)[None, :]).astype(jnp.int32)
    csum = jnp.cumsum(onehot, axis=0)
    rank = jnp.sum((csum - onehot) * onehot, axis=1)
    counts = csum[-1]
    tiles = (counts + MOE_TILE - 1) // MOE_TILE
    tile_end = jnp.cumsum(tiles)
    tile_start = tile_end - tiles
    pos = jnp.sum(onehot * (tile_start * MOE_TILE)[None, :], axis=1) + rank
    tok = jnp.arange(2 * t_all, dtype=jnp.int32) // 2
    src = (jnp.arange(p_rows, dtype=jnp.int32) % t_all).at[pos].set(tok)
    n_tiles = tile_end[-1]
    tidx = jnp.arange(p_rows // MOE_TILE, dtype=jnp.int32)
    tclamp = jnp.minimum(tidx, n_tiles - 1)
    tile_expert = jnp.sum((tile_end[None, :] <= tclamp[:, None]).astype(jnp.int32), axis=1)
    present = (tiles > 0).astype(jnp.int32)
    ord_of_e = jnp.cumsum(present) - 1
    eids = jnp.arange(N_EXPERTS, dtype=jnp.int32)
    elist = jnp.sum(jnp.where((ord_of_e[None, :] == eids[:, None]) & (present[None, :] == 1), eids[None, :], 0), axis=1)
    ordinal = jnp.sum(jnp.where(tile_expert[:, None] == eids[None, :], ord_of_e[None, :], 0), axis=1)
    first = jnp.concatenate([jnp.ones((1,), jnp.int32),
                             (tile_expert[1:] != tile_expert[:-1]).astype(jnp.int32)])
    one = lambda v: v.reshape(1).astype(jnp.int32)
    meta = (one(n_tiles), first, ordinal.astype(jnp.int32), elist.astype(jnp.int32), one(jnp.sum(present)))
    return pos.reshape(t_all, 2), src, meta


def _pick_tile(n, candidates):
    for c in candidates:
        if n % c == 0:
            return c
    raise ValueError(f"no row tile for {n}")


def kernel(x_prompt, x_sample, state_mlstm_C, state_mlstm_n, state_mlstm_m, state_conv, state_s5_re,
           state_s5_im, norm_mix_g, w_in, b_i, b_f, w_conv, b_conv, head_norm_g, w_pa, s5_a_re, s5_a_im,
           s5_log_step, s5_b_re, s5_b_im, s5_c_re, s5_c_im, s5_d, s5_w_glu, s5_b_glu, w_pb, w_out,
           norm_ffn_g, w_rg, b_rg, w_rexp, b_rexp, w_gate, w_up, w_down, norm_final_g):
    assert state_mlstm_C.shape[0] == 1 and x_sample.shape[1] == 1
    batch, seq, _ = x_prompt.shape
    dec = x_sample.shape[0]
    t_p = batch * seq
    t_all = t_p + dec
    assert seq % CHUNK == 0 and t_p % dec == 0 and dec % LANES == 0
    d_a = H_A * DK
    tm_p = _pick_tile(t_p, (512, 256, 128))
    tm_all = _pick_tile(t_all, (1664, 640, 384, 128))
    sample_blk = t_p // dec

    xp = x_prompt.reshape(t_p, D_MODEL)
    xs = x_sample.reshape(dec, D_MODEL)

    g_mix = norm_mix_g[0]
    w_in_t = w_in.reshape(w_in.shape[1:]).T
    n_qkvo = 4 * d_a
    tn = 1024
    n_gate_cols = 2 * H_A
    b_gates = jnp.pad(jnp.concatenate([b_i[0], b_f[0]]), (0, LANES - n_gate_cols))
    xn_all, gates, gates_t = _rmsnorm_rows(xp, g_mix, w_in_t, n_qkvo, b_gates, t_all, 0, tm_p)
    xn_all, gates, _ = _rmsnorm_rows(xs, g_mix, w_in_t, n_qkvo, b_gates, t_all, t_p, dec,
                                     alias=(xn_all, gates))
    qkvo = _matmul_t(xn_all, w_in_t, [j * tn for j in range(n_qkvo // tn)], jnp.zeros((1, n_qkvo), F32),
                     tm_all, tn, F32)
    c_u = n_qkvo + n_gate_cols
    c_ga = c_u + D_B
    starts = [c_ga + j * tn for j in range(2 * D_MODEL // tn)] + [c_u]
    proj2 = _matmul_t(xn_all, w_in_t, starts, jnp.zeros((1, len(starts) * tn), F32), tm_all, tn, F32)
    u_col0 = 2 * D_MODEL

    conv_s_in = state_conv[0].reshape(dec, (CONV_W - 1) * 2 * d_a)
    q_s, kw_s, a_e, s_e, den_e, conv_s, n_s, m_s = _sample_pre(
        qkvo, conv_s_in, w_conv[0], b_conv[0].reshape(1, -1), gates, state_mlstm_m[0],
        state_mlstm_n[0].reshape(dec, d_a), dec, sample_blk)
    v_s = qkvo[t_p:, 2 * d_a:3 * d_a]
    r3 = lambda a: a.reshape(dec, H_A, DK)

    head_g = head_norm_g[0].reshape(1, d_a)
    hn_all, c_p, n_p, m_p, c_s, qc = _mlstm_prompt(
        qkvo, gates, gates_t, w_conv[0], b_conv[0].reshape(1, -1), head_g,
        (r3(q_s), r3(kw_s), r3(v_s), r3(a_e), state_mlstm_C[0]), batch, seq, t_all)
    conv_p = jnp.stack([qkvo[b * seq + seq - (CONV_W - 1):(b + 1) * seq, :2 * d_a] for b in range(batch)])
    hn_all = _sample_post(qc.reshape(dec, d_a), s_e, a_e, den_e, qkvo, head_g, hn_all, dec, sample_blk)

    d_row = s5_d[0].reshape(1, D_B)
    s5_prompt_w, s5_sample_w = _s5_chunk_params(s5_a_re[0], s5_a_im[0], s5_log_step[0], s5_b_re[0],
                                                s5_b_im[0], s5_c_re[0], s5_c_im[0])
    ys_all, s5re_p, s5im_p = _s5_prompt(proj2, u_col0, *s5_prompt_w, d_row, batch, seq, t_all)

    ys_all, s5re_s, s5im_s = _s5_sample(proj2, u_col0, state_s5_re[0].reshape(dec, -1),
                                        state_s5_im[0].reshape(dec, -1),
                                        *s5_sample_w, d_row, ys_all, dec, sample_blk)

    wr = jnp.pad(jnp.concatenate([w_rg[0], w_rexp[0]], axis=1), ((0, 0), (0, LANES - N_GROUPS - N_EXPERTS)))
    wr_hi = wr.astype(BF16)
    wr_lo = (wr - wr_hi.astype(F32)).astype(BF16)
    b_r = jnp.pad(jnp.concatenate([b_rg[0], b_rexp[0]]), (0, LANES - N_GROUPS - N_EXPERTS)).reshape(1, LANES)
    merge_w = (w_pa[0].astype(BF16), s5_w_glu[0].astype(BF16), w_pb[0].astype(BF16), w_out[0].astype(BF16))
    b_glu = s5_b_glu[0].reshape(1, D_B)
    g_ffn = norm_ffn_g[0].reshape(1, D_MODEL)
    tm_m = _pick_tile(t_p, (256, 128))
    outs = _merge(xp, hn_all, ys_all, proj2, b_glu, g_ffn, wr_hi, wr_lo, b_r, *merge_w, t_all, 0, tm_m)
    x1_all, xn2_all, ids, wts = _merge(xs, hn_all, ys_all, proj2, b_glu, g_ffn, wr_hi, wr_lo, b_r,
                                       *merge_w, t_all, t_p, dec, aliases=outs)

    p_rows = -(-(2 * t_all + N_EXPERTS * (MOE_TILE - 1)) // MOE_TILE) * MOE_TILE
    pos, src, meta = _dispatch(ids, t_all, p_rows)
    take_rows = lambda a, idx: a.at[idx].get(mode='promise_in_bounds')
    xs_sorted = take_rows(xn2_all, src)
    yp = _moe_experts(*meta, xs_sorted, w_gate[0], w_up[0], w_down[0])
    yg0 = take_rows(yp, pos[:, 0])
    yg1 = take_rows(yp, pos[:, 1])

    g_fin = norm_final_g.reshape(1, D_MODEL)
    y_prompt = _final(x1_all, yg0, yg1, wts, g_fin, 0, t_p, tm_p).reshape(batch, seq, D_MODEL)
    y_sample = _final(x1_all, yg0, yg1, wts, g_fin, t_p, dec, dec).reshape(dec, 1, D_MODEL)

    lead = lambda a, shape: a.reshape((1,) + shape)
    return (y_prompt, y_sample,
            lead(c_p, (batch, H_A, DK, DK)), lead(n_p, (batch, H_A, DK)), lead(m_p[:, 0, :H_A], (batch, H_A)),
            lead(conv_p, (batch, CONV_W - 1, 2 * d_a)),
            lead(s5re_p, (batch, G_B, P_S5)), lead(s5im_p, (batch, G_B, P_S5)),
            lead(c_s, (dec, H_A, DK, DK)), lead(n_s, (dec, H_A, DK)), lead(m_s[:, :H_A], (dec, H_A)),
            lead(conv_s, (dec, CONV_W - 1, 2 * d_a)),
            lead(s5re_s, (dec, G_B, P_S5)), lead(s5im_s, (dec, G_B, P_S5)))
```
